```python
import jax, jax.numpy as jnp
from jax import lax
import numpy as np

D_MODEL = 1024
BATCH = 8
SEQ = 8192
DEPTH = 1

HEAD_DIM = 64
A_Q_HEADS = 8
A_KV_HEADS = 2
A_HALF_WINDOW = 128
B_HEADS = 8
B_PATTERNS = ((128, 1), (512, 4), (2048, 16))
D_FF = 2816
ROPE_THETA = 10000.0
NORM_EPS = 1e-6
FFN_RES_WEIGHT = 0.5

A_Q_W = A_Q_HEADS * HEAD_DIM
A_KV_W = A_KV_HEADS * HEAD_DIM
B_W = B_HEADS * HEAD_DIM
IN_W = A_Q_W + 2 * A_KV_W + 3 * B_W
MIX_W = A_Q_W + B_W

kernel_name = "hybrid_window_gqa_dilated_macaron_encoder"


def rms_norm(x, g):
    xf = x.astype(jnp.float32)
    y = xf * lax.rsqrt(jnp.mean(xf * xf, axis=-1, keepdims=True) + NORM_EPS)
    return (y * g.astype(jnp.float32)).astype(x.dtype)


def swiglu(h, w_gate, w_up, w_down):
    return (jax.nn.silu(h @ w_gate) * (h @ w_up)) @ w_down


def rope_tables(positions):
    inv_freq = 1.0 / (ROPE_THETA ** (jnp.arange(0, HEAD_DIM, 2, dtype=jnp.float32) / HEAD_DIM))
    ang = positions.astype(jnp.float32)[..., None] * inv_freq
    return jnp.cos(ang)[:, :, None, :], jnp.sin(ang)[:, :, None, :]


def apply_rope(t, cos, sin):
    tf = t.astype(jnp.float32)
    t1, t2 = jnp.split(tf, 2, axis=-1)
    return jnp.concatenate([t1 * cos - t2 * sin, t2 * cos + t1 * sin], axis=-1).astype(t.dtype)


def banded_attention(q, k, v, half_window, sink=None):
    blk = half_window
    B, L, Hq, Dh = q.shape
    Hkv = k.shape[2]
    G = Hq // Hkv
    nb = -(-L // blk)
    Lp = nb * blk
    pad = Lp - L
    qb = jnp.pad(q, ((0, 0), (0, pad), (0, 0), (0, 0))).astype(jnp.float32).reshape(B, nb, blk, Hkv, G, Dh)
    kp = jnp.pad(k, ((0, 0), (blk, blk + pad), (0, 0), (0, 0))).astype(jnp.float32)
    vp = jnp.pad(v, ((0, 0), (blk, blk + pad), (0, 0), (0, 0))).astype(jnp.float32)
    kw = jnp.concatenate([kp[:, j * blk:j * blk + Lp].reshape(B, nb, blk, Hkv, Dh) for j in range(3)], axis=2)
    vw = jnp.concatenate([vp[:, j * blk:j * blk + Lp].reshape(B, nb, blk, Hkv, Dh) for j in range(3)], axis=2)
    qpos = jnp.arange(Lp).reshape(nb, blk)
    kpos = jnp.arange(nb)[:, None] * blk + jnp.arange(3 * blk)[None, :] - blk
    valid = (jnp.abs(qpos[:, :, None] - kpos[:, None, :]) <= half_window) & (kpos[:, None, :] >= 0) & (kpos[:, None, :] < L)
    s = jnp.einsum('bnqhgd,bnkhd->bnhgqk', qb, kw) * (Dh ** -0.5)
    s = jnp.where(valid[None, :, None, None], s, -jnp.inf)
    m = jnp.max(s, axis=-1)
    if sink is not None:
        sk = sink.astype(jnp.float32).reshape(Hkv, G)[None, None, :, :, None]
        m = jnp.maximum(m, sk)
    p = jnp.exp(s - m[..., None])
    den = jnp.sum(p, axis=-1)
    if sink is not None:
        den = den + jnp.exp(sk - m)
    o = jnp.einsum('bnhgqk,bnkhd->bnqhgd', p, vw) / jnp.transpose(den, (0, 1, 4, 2, 3))[..., None]
    lse = jnp.transpose(m + jnp.log(den), (0, 1, 4, 2, 3)).reshape(B, Lp, Hq)[:, :L]
    return o.reshape(B, Lp, Hq, Dh)[:, :L], lse


def dilated_window_attention(q, k, v, window, dilation):
    B, S, H, Dh = q.shape
    msub = S // dilation

    def to_sub(t):
        return t.reshape(B, msub, dilation, H, Dh).transpose(0, 2, 1, 3, 4).reshape(B * dilation, msub, H, Dh)

    o, lse = banded_attention(to_sub(q), to_sub(k), to_sub(v), window // (2 * dilation))
    o = o.reshape(B, dilation, msub, H, Dh).transpose(0, 2, 1, 3, 4).reshape(B, S, H, Dh)
    lse = lse.reshape(B, dilation, msub, H).transpose(0, 2, 1, 3).reshape(B, S, H)
    return o, lse


def mixer(h, w_in, a_sink, w_out, cos, sin):
    B, S, _ = h.shape
    proj = h @ w_in
    cuts = np.cumsum([A_Q_W, A_KV_W, A_KV_W, B_W, B_W]).tolist()
    aq, ak, av, bq, bk, bv = jnp.split(proj, cuts, axis=-1)
    aq = apply_rope(aq.reshape(B, S, A_Q_HEADS, HEAD_DIM), cos, sin)
    ak = apply_rope(ak.reshape(B, S, A_KV_HEADS, HEAD_DIM), cos, sin)
    av = av.reshape(B, S, A_KV_HEADS, HEAD_DIM)
    bq = apply_rope(bq.reshape(B, S, B_HEADS, HEAD_DIM), cos, sin)
    bk = apply_rope(bk.reshape(B, S, B_HEADS, HEAD_DIM), cos, sin)
    bv = bv.reshape(B, S, B_HEADS, HEAD_DIM)
    a_out, _ = banded_attention(aq, ak, av, A_HALF_WINDOW, sink=a_sink)
    outs, lses = [], []
    for w, d in B_PATTERNS:
        o, l = dilated_window_attention(bq, bk, bv, w, d)
        outs.append(o)
        lses.append(l)
    wts = jax.nn.softmax(jnp.stack(lses, axis=0), axis=0)
    b_out = jnp.sum(wts[..., None] * jnp.stack(outs, axis=0), axis=0)
    cat = jnp.concatenate([a_out.reshape(B, S, A_Q_W), b_out.reshape(B, S, B_W)], axis=-1).astype(h.dtype)
    return cat @ w_out


def _fwd_setup_inputs(seed: int = 0) -> dict:
    key = jax.random.key(seed)
    ks = jax.random.split(key, 16)
    f32 = jnp.float32
    nrm = lambda k, shape, scale: jax.random.normal(k, shape, f32) * scale
    gain = lambda k: 1.0 + 0.02 * jax.random.normal(k, (DEPTH, D_MODEL), f32)
    x = jax.random.normal(ks[0], (BATCH, SEQ, D_MODEL), f32)
    offsets = jax.random.randint(ks[1], (BATCH, 1), 0, 4096, dtype=jnp.int32)
    positions = (jnp.arange(SEQ, dtype=jnp.int32)[None, :] + offsets).astype(jnp.int32)
    return {
        "x": x,
        "positions": positions,
        "norm_ffn1": gain(ks[2]),
        "w_gate1": nrm(ks[3], (DEPTH, D_MODEL, D_FF), D_MODEL ** -0.5),
        "w_up1": nrm(ks[4], (DEPTH, D_MODEL, D_FF), D_MODEL ** -0.5),
        "w_down1": nrm(ks[5], (DEPTH, D_FF, D_MODEL), D_FF ** -0.5),
        "norm_mix": gain(ks[6]),
        "w_in": nrm(ks[7], (DEPTH, D_MODEL, IN_W), D_MODEL ** -0.5),
        "a_sink": nrm(ks[8], (DEPTH, A_Q_HEADS), 0.5),
        "w_out": nrm(ks[9], (DEPTH, MIX_W, D_MODEL), MIX_W ** -0.5),
        "norm_ffn2": gain(ks[10]),
        "w_gate2": nrm(ks[11], (DEPTH, D_MODEL, D_FF), D_MODEL ** -0.5),
        "w_up2": nrm(ks[12], (DEPTH, D_MODEL, D_FF), D_MODEL ** -0.5),
        "w_down2": nrm(ks[13], (DEPTH, D_FF, D_MODEL), D_FF ** -0.5),
        "norm_final": 1.0 + 0.02 * jax.random.normal(ks[14], (D_MODEL,), f32),
    }


def _fwd_reference(x, positions, norm_ffn1, w_gate1, w_up1, w_down1, norm_mix, w_in, a_sink, w_out,
              norm_ffn2, w_gate2, w_up2, w_down2, norm_final):
    cos, sin = rope_tables(positions)
    for l in range(DEPTH):
        x = x + FFN_RES_WEIGHT * swiglu(rms_norm(x, norm_ffn1[l]), w_gate1[l], w_up1[l], w_down1[l])
        x = x + mixer(rms_norm(x, norm_mix[l]), w_in[l], a_sink[l], w_out[l], cos, sin)
        x = x + FFN_RES_WEIGHT * swiglu(rms_norm(x, norm_ffn2[l]), w_gate2[l], w_up2[l], w_down2[l])
    return rms_norm(x, norm_final)


import jax as _jax
import jax.numpy as _jnp

TWIN_FORMAT = 'train_step'
FWD_PARAMS = ['x', 'positions', 'norm_ffn1', 'w_gate1', 'w_up1', 'w_down1', 'norm_mix', 'w_in', 'a_sink', 'w_out', 'norm_ffn2', 'w_gate2', 'w_up2', 'w_down2', 'norm_final']
TWIN_WEIGHTS = ['norm_ffn1', 'w_gate1', 'w_up1', 'w_down1', 'norm_mix', 'w_in', 'a_sink', 'w_out', 'norm_ffn2', 'w_gate2', 'w_up2', 'w_down2', 'norm_final']
TWIN_DIFF_INPUT = 'x'
TWIN_INPUTS = ['x', 'positions', 'norm_ffn1', 'w_gate1', 'w_up1', 'w_down1', 'norm_mix', 'w_in', 'a_sink', 'w_out', 'norm_ffn2', 'w_gate2', 'w_up2', 'w_down2', 'norm_final', 'loss_target', 'm_norm_ffn1', 'm_w_gate1', 'm_w_up1', 'm_w_down1', 'm_norm_mix', 'm_w_in', 'm_a_sink', 'm_w_out', 'm_norm_ffn2', 'm_w_gate2', 'm_w_up2', 'm_w_down2', 'm_norm_final', 'v_norm_ffn1', 'v_w_gate1', 'v_w_up1', 'v_w_down1', 'v_norm_mix', 'v_w_in', 'v_a_sink', 'v_w_out', 'v_norm_ffn2', 'v_w_gate2', 'v_w_up2', 'v_w_down2', 'v_norm_final']
TWIN_OUTPUTS = ['loss', 'grad_x', 'grad_norm_ffn1', 'grad_w_gate1', 'grad_w_up1', 'grad_w_down1', 'grad_norm_mix', 'grad_w_in', 'grad_a_sink', 'grad_w_out', 'grad_norm_ffn2', 'grad_w_gate2', 'grad_w_up2', 'grad_w_down2', 'grad_norm_final', 'delta_norm_ffn1', 'delta_w_gate1', 'delta_w_up1', 'delta_w_down1', 'delta_norm_mix', 'delta_w_in', 'delta_a_sink', 'delta_w_out', 'delta_norm_ffn2', 'delta_w_gate2', 'delta_w_up2', 'delta_w_down2', 'delta_norm_final', 'new_m_norm_ffn1', 'new_m_w_gate1', 'new_m_w_up1', 'new_m_w_down1', 'new_m_norm_mix', 'new_m_w_in', 'new_m_a_sink', 'new_m_w_out', 'new_m_norm_ffn2', 'new_m_w_gate2', 'new_m_w_up2', 'new_m_w_down2', 'new_m_norm_final', 'new_v_norm_ffn1', 'new_v_w_gate1', 'new_v_w_up1', 'new_v_w_down1', 'new_v_norm_mix', 'new_v_w_in', 'new_v_a_sink', 'new_v_w_out', 'new_v_norm_ffn2', 'new_v_w_gate2', 'new_v_w_up2', 'new_v_w_down2', 'new_v_norm_final']
TWIN_LEAF_KINDS = {'loss': 'loss', 'grad_x': 'grad_x', 'grad_norm_ffn1': 'grad_w', 'grad_w_gate1': 'grad_w', 'grad_w_up1': 'grad_w', 'grad_w_down1': 'grad_w', 'grad_norm_mix': 'grad_w', 'grad_w_in': 'grad_w', 'grad_a_sink': 'grad_w', 'grad_w_out': 'grad_w', 'grad_norm_ffn2': 'grad_w', 'grad_w_gate2': 'grad_w', 'grad_w_up2': 'grad_w', 'grad_w_down2': 'grad_w', 'grad_norm_final': 'grad_w', 'delta_norm_ffn1': 'delta_w', 'delta_w_gate1': 'delta_w', 'delta_w_up1': 'delta_w', 'delta_w_down1': 'delta_w', 'delta_norm_mix': 'delta_w', 'delta_w_in': 'delta_w', 'delta_a_sink': 'delta_w', 'delta_w_out': 'delta_w', 'delta_norm_ffn2': 'delta_w', 'delta_w_gate2': 'delta_w', 'delta_w_up2': 'delta_w', 'delta_w_down2': 'delta_w', 'delta_norm_final': 'delta_w', 'new_m_norm_ffn1': 'new_m', 'new_m_w_gate1': 'new_m', 'new_m_w_up1': 'new_m', 'new_m_w_down1': 'new_m', 'new_m_norm_mix': 'new_m', 'new_m_w_in': 'new_m', 'new_m_a_sink': 'new_m', 'new_m_w_out': 'new_m', 'new_m_norm_ffn2': 'new_m', 'new_m_w_gate2': 'new_m', 'new_m_w_up2': 'new_m', 'new_m_w_down2': 'new_m', 'new_m_norm_final': 'new_m', 'new_v_norm_ffn1': 'new_v', 'new_v_w_gate1': 'new_v', 'new_v_w_up1': 'new_v', 'new_v_w_down1': 'new_v', 'new_v_norm_mix': 'new_v', 'new_v_w_in': 'new_v', 'new_v_a_sink': 'new_v', 'new_v_w_out': 'new_v', 'new_v_norm_ffn2': 'new_v', 'new_v_w_gate2': 'new_v', 'new_v_w_up2': 'new_v', 'new_v_w_down2': 'new_v', 'new_v_norm_final': 'new_v'}


def _forward(args):
    return _fwd_reference(*[args[k] for k in FWD_PARAMS])


def _output_shape():
    def fwd():
        inp = _fwd_setup_inputs(0)
        return _fwd_reference(*[inp[k] for k in FWD_PARAMS])
    out = _jax.eval_shape(fwd)
    return out.shape, out.dtype

N_MICROBATCH = 1
ADAM_LR = 0.001
ADAM_B1 = 0.9
ADAM_B2 = 0.999
ADAM_EPS = 1e-08
ADAM_WD = 0.01
ADAM_STEP = 10
PER_EXAMPLE_BATCH_AXIS = {'x': 0, 'positions': 0, 'loss_target': 0}
SHARED_INPUTS = []
_WEIGHT_DTYPES = {'norm_ffn1': _jnp.float32, 'w_gate1': _jnp.float32, 'w_up1': _jnp.float32, 'w_down1': _jnp.float32, 'norm_mix': _jnp.float32, 'w_in': _jnp.float32, 'a_sink': _jnp.float32, 'w_out': _jnp.float32, 'norm_ffn2': _jnp.float32, 'w_gate2': _jnp.float32, 'w_up2': _jnp.float32, 'w_down2': _jnp.float32, 'norm_final': _jnp.float32}
MOMENT_SCALE = {'norm_ffn1': 1.126808e-01, 'w_gate1': 4.695442e-02, 'w_up1': 4.555026e-02, 'w_down1': 7.546195e-02, 'norm_mix': 5.853595e-02, 'w_in': 3.752140e-02, 'a_sink': 5.495862e-04, 'w_out': 3.338404e-02, 'norm_ffn2': 9.925536e-02, 'w_gate2': 4.278787e-02, 'w_up2': 4.154136e-02, 'w_down2': 6.895557e-02, 'norm_final': 6.395962e+01}


def _to_microbatches(a, axis):
    t = _jnp.moveaxis(a, axis, 0)
    t = t.reshape((N_MICROBATCH, t.shape[0] // N_MICROBATCH) + t.shape[1:])
    return _jnp.moveaxis(t, 1, axis + 1)


def setup_inputs(seed: int = 0) -> dict:
    inp = _fwd_setup_inputs(seed)
    key = _jax.random.fold_in(_jax.random.key(seed), 7919)
    shape, _ = _output_shape()
    out = dict(inp)
    out["loss_target"] = _jax.random.normal(_jax.random.fold_in(key, 0), shape, _jnp.float32)
    for i, name in enumerate(TWIN_WEIGHTS):
        w = inp[name].astype(_jnp.float32)
        if MOMENT_SCALE is None:
            s = _jnp.sqrt(_jnp.mean(_jnp.square(w)) + 1e-30)
        else:
            s = MOMENT_SCALE[name]
        km, kv = _jax.random.split(_jax.random.fold_in(key, i + 1))
        out[name] = w
        out["m_" + name] = s * _jax.random.normal(km, w.shape, _jnp.float32)
        out["v_" + name] = (s * s) * _jax.random.uniform(kv, w.shape, _jnp.float32, 0.5, 1.5)
    if N_MICROBATCH > 1:
        for name, axis in PER_EXAMPLE_BATCH_AXIS.items():
            out[name] = _to_microbatches(out[name], axis)
    return {'x': out['x'], 'positions': out['positions'], 'norm_ffn1': out['norm_ffn1'], 'w_gate1': out['w_gate1'], 'w_up1': out['w_up1'], 'w_down1': out['w_down1'], 'norm_mix': out['norm_mix'], 'w_in': out['w_in'], 'a_sink': out['a_sink'], 'w_out': out['w_out'], 'norm_ffn2': out['norm_ffn2'], 'w_gate2': out['w_gate2'], 'w_up2': out['w_up2'], 'w_down2': out['w_down2'], 'norm_final': out['norm_final'], 'loss_target': out['loss_target'], 'm_norm_ffn1': out['m_norm_ffn1'], 'm_w_gate1': out['m_w_gate1'], 'm_w_up1': out['m_w_up1'], 'm_w_down1': out['m_w_down1'], 'm_norm_mix': out['m_norm_mix'], 'm_w_in': out['m_w_in'], 'm_a_sink': out['m_a_sink'], 'm_w_out': out['m_w_out'], 'm_norm_ffn2': out['m_norm_ffn2'], 'm_w_gate2': out['m_w_gate2'], 'm_w_up2': out['m_w_up2'], 'm_w_down2': out['m_w_down2'], 'm_norm_final': out['m_norm_final'], 'v_norm_ffn1': out['v_norm_ffn1'], 'v_w_gate1': out['v_w_gate1'], 'v_w_up1': out['v_w_up1'], 'v_w_down1': out['v_w_down1'], 'v_norm_mix': out['v_norm_mix'], 'v_w_in': out['v_w_in'], 'v_a_sink': out['v_a_sink'], 'v_w_out': out['v_w_out'], 'v_norm_ffn2': out['v_norm_ffn2'], 'v_w_gate2': out['v_w_gate2'], 'v_w_up2': out['v_w_up2'], 'v_w_down2': out['v_w_down2'], 'v_norm_final': out['v_norm_final']}


def _loss(weights, diff, rest, loss_target):
    with _jax.named_scope("forward"):
        args = {**rest, TWIN_DIFF_INPUT: diff, **{k: w.astype(_WEIGHT_DTYPES[k]) for k, w in weights.items()}}
        y = _forward(args)
    with _jax.named_scope("loss_head"):
        err = _jnp.square(y.astype(_jnp.float32) - loss_target)
        return 0.5 * _jnp.sum(_jnp.mean(err, axis=-1)) if err.ndim else 0.5 * err


def _adamw(w, g, m, v):
    m = ADAM_B1 * m + (1.0 - ADAM_B1) * g
    v = ADAM_B2 * v + (1.0 - ADAM_B2) * _jnp.square(g)
    m_hat = m / (1.0 - ADAM_B1 ** ADAM_STEP)
    v_hat = v / (1.0 - ADAM_B2 ** ADAM_STEP)
    delta = -ADAM_LR * (m_hat / (_jnp.sqrt(v_hat) + ADAM_EPS) + ADAM_WD * w)
    return delta, m, v


def reference(x, positions, norm_ffn1, w_gate1, w_up1, w_down1, norm_mix, w_in, a_sink, w_out, norm_ffn2, w_gate2, w_up2, w_down2, norm_final, loss_target, m_norm_ffn1, m_w_gate1, m_w_up1, m_w_down1, m_norm_mix, m_w_in, m_a_sink, m_w_out, m_norm_ffn2, m_w_gate2, m_w_up2, m_w_down2, m_norm_final, v_norm_ffn1, v_w_gate1, v_w_up1, v_w_down1, v_norm_mix, v_w_in, v_a_sink, v_w_out, v_norm_ffn2, v_w_gate2, v_w_up2, v_w_down2, v_norm_final):
    given = dict(x=x, positions=positions, norm_ffn1=norm_ffn1, w_gate1=w_gate1, w_up1=w_up1, w_down1=w_down1, norm_mix=norm_mix, w_in=w_in, a_sink=a_sink, w_out=w_out, norm_ffn2=norm_ffn2, w_gate2=w_gate2, w_up2=w_up2, w_down2=w_down2, norm_final=norm_final, loss_target=loss_target, m_norm_ffn1=m_norm_ffn1, m_w_gate1=m_w_gate1, m_w_up1=m_w_up1, m_w_down1=m_w_down1, m_norm_mix=m_norm_mix, m_w_in=m_w_in, m_a_sink=m_a_sink, m_w_out=m_w_out, m_norm_ffn2=m_norm_ffn2, m_w_gate2=m_w_gate2, m_w_up2=m_w_up2, m_w_down2=m_w_down2, m_norm_final=m_norm_final, v_norm_ffn1=v_norm_ffn1, v_w_gate1=v_w_gate1, v_w_up1=v_w_up1, v_w_down1=v_w_down1, v_norm_mix=v_norm_mix, v_w_in=v_w_in, v_a_sink=v_a_sink, v_w_out=v_w_out, v_norm_ffn2=v_norm_ffn2, v_w_gate2=v_w_gate2, v_w_up2=v_w_up2, v_w_down2=v_w_down2, v_norm_final=v_norm_final)
    weights = {n: given[n] for n in TWIN_WEIGHTS}
    shared = {n: given[n] for n in SHARED_INPUTS}
    per_example = {n: given[n] for n in ['x', 'positions']}
    grad_fn = _jax.value_and_grad(_loss, argnums=(0, 1))

    def one_microbatch(ex, loss_target):
        ex = dict(ex)
        diff = ex.pop(TWIN_DIFF_INPUT)
        return grad_fn(weights, diff, {**shared, **ex}, loss_target)

    if N_MICROBATCH == 1:
        loss, (grad_w, grad_x) = one_microbatch(per_example, given["loss_target"])
    else:
        def body(carry, xs):
            loss_sum, grad_sum = carry
            l_k, (gw_k, gx_k) = one_microbatch(xs[0], xs[1])
            with _jax.named_scope("update"):
                return (loss_sum + l_k, _jax.tree.map(_jnp.add, grad_sum, gw_k)), gx_k

        init = (_jnp.zeros((), _jnp.float32), _jax.tree.map(_jnp.zeros_like, weights))
        (loss, grad_w), grad_x = _jax.lax.scan(body, init, (per_example, given["loss_target"]))
    with _jax.named_scope("update"):
        delta_w, new_m, new_v = {}, {}, {}
        for n in TWIN_WEIGHTS:
            delta_w[n], new_m[n], new_v[n] = _adamw(weights[n], grad_w[n], given["m_" + n], given["v_" + n])
    return (loss, grad_x, *[grad_w[n] for n in TWIN_WEIGHTS], *[delta_w[n] for n in TWIN_WEIGHTS],
            *[new_m[n] for n in TWIN_WEIGHTS], *[new_v[n] for n in TWIN_WEIGHTS])
```

```python
import functools
import itertools

import numpy as np
import jax
import jax.numpy as jnp
from jax import lax
from jax.experimental import pallas as pl
from jax.experimental.pallas import tpu as pltpu

F32 = jnp.float32
BF16 = jnp.bfloat16

N_DEV = 8
HEAD_DIM = 64
LANES = 128
A_Q_W, A_KV_W, B_W = 512, 128, 512
A_HALF_WINDOW = 128
B_PATTERNS = ((128, 1), (512, 4), (2048, 16))
ROPE_THETA = 10000.0
NORM_EPS = 1e-6
FFN_RES_WEIGHT = 0.5
QK_SCALE = HEAD_DIM ** -0.5
NEG = -1e30

ADAM_LR = 0.001
ADAM_B1 = 0.9
ADAM_B2 = 0.999
ADAM_EPS = 1e-08
ADAM_WD = 0.01
ADAM_STEP = 10

MESH_T = pl.DeviceIdType.MESH
VMEM_LIMIT = 56 * 1024 * 1024


def _cp(sem=None, vmem=VMEM_LIMIT):
    return pltpu.CompilerParams(dimension_semantics=sem, vmem_limit_bytes=vmem)


def _dot_nn(a, b):
    return jnp.dot(a, b, preferred_element_type=F32)


def _dot_nt(a, b):
    return lax.dot_general(a, b, (((1,), (1,)), ((), ())), preferred_element_type=F32)


def _dot_tn(a, b):
    return lax.dot_general(a, b, (((0,), (0,)), ((), ())), preferred_element_type=F32)


def _rstd(xv):
    return lax.rsqrt(jnp.mean(xv * xv, axis=-1, keepdims=True) + NORM_EPS)


def _norm_bwd(dh, xv, r, gn):
    gy = dh * gn
    c = jnp.sum(gy * xv, axis=-1, keepdims=True) * (1.0 / xv.shape[-1])
    dx = r * gy - xv * (r * r * r * c)
    dgn = jnp.sum(dh * (xv * r), axis=0, keepdims=True)
    return dx, dgn


def _accumulate(ref, val, first):
    @pl.when(first)
    def _():
        ref[...] = val

    @pl.when(jnp.logical_not(first))
    def _():
        ref[...] += val


def _mesh_pos():
    return lax.axis_index("x"), lax.axis_index("y"), lax.axis_index("c")


def _dev_index(d):
    return 4 * d[0] + 2 * d[1] + d[2]


def _all_gather_rows(packed, rows_list, name):
    n_w = len(rows_list)
    offs = [int(o) for o in np.cumsum([0] + list(rows_list[:-1]))]
    d = packed.shape[1]

    def body(p_ref, *refs):
        outs = refs[:n_w]
        send_sems, recv_sems, local_sem = refs[n_w:]
        x, y, c = _mesh_pos()
        me, sibling = (x, y, c), (x, y, 1 - c)
        chips = [(1 - x, y), (x, 1 - y), (1 - x, 1 - y)]

        def rows(w, dev):
            start = pl.multiple_of(_dev_index(dev) * rows_list[w], 16)
            return outs[w].at[pl.ds(start, rows_list[w]), :]

        def mine(w):
            return p_ref.at[pl.ds(offs[w], rows_list[w]), :]

        def copy(k, w, block, to, own):
            return pltpu.make_async_remote_copy(
                src_ref=mine(w) if own else rows(w, block), dst_ref=rows(w, block),
                send_sem=send_sems.at[k], recv_sem=recv_sems.at[k], device_id=to, device_id_type=MESH_T)

        def all_blocks(k):
            return pltpu.make_async_remote_copy(
                src_ref=p_ref, dst_ref=p_ref, send_sem=send_sems.at[k], recv_sem=recv_sems.at[k],
                device_id=me, device_id_type=MESH_T)

        for w in range(n_w):
            pltpu.make_async_copy(mine(w), rows(w, me), local_sem).start()
        for w in range(n_w):
            copy(0, w, me, sibling, True).start()
        for j, chip in enumerate(chips):
            for w in range(n_w):
                copy(1 + j, w, me, (*chip, c), True).start()
        for j, chip in enumerate(chips):
            all_blocks(1 + j).wait_recv()
            for w in range(n_w):
                copy(4 + j, w, (*chip, c), sibling, False).start()
        all_blocks(0).wait_recv()
        for j in range(3):
            all_blocks(4 + j).wait_recv()
        for k in range(7):
            all_blocks(k).wait_send()
        pltpu.make_async_copy(p_ref, p_ref, local_sem).wait()

    any_spec = pl.BlockSpec(memory_space=pl.ANY)
    return pl.pallas_call(
        body, name=name,
        out_shape=[jax.ShapeDtypeStruct((N_DEV * r, d), packed.dtype) for r in rows_list],
        in_specs=[any_spec], out_specs=[any_spec] * n_w,
        scratch_shapes=[pltpu.SemaphoreType.DMA((7,)), pltpu.SemaphoreType.DMA((7,)), pltpu.SemaphoreType.DMA],
    )(packed)


def _exchange_partials(groups, name):
    flat = [a for g in groups for a in g]
    n_g = len(groups)
    sizes = [len(g) for g in groups]
    rows = [g[0].shape[0] // N_DEV for g in groups]
    first = [int(o) for o in np.cumsum([0] + sizes[:-1])]

    def body(*refs):
        srcs = refs[:len(flat)]
        lands = refs[len(flat):len(flat) + n_g]
        send_sems, recv_sems, local_sems = refs[len(flat) + n_g:]
        x, y, c = _mesh_pos()
        me = (x, y, c)
        me_idx = _dev_index(me)

        def block(g, i, dev):
            start = pl.multiple_of(_dev_index(dev) * rows[g], 8)
            return srcs[first[g] + i].at[pl.ds(start, rows[g]), :]

        def slot(g, i):
            return lands[g].at[me_idx, pl.ds(i * rows[g], rows[g]), :]

        for g in range(n_g):
            for i in range(sizes[g]):
                pltpu.make_async_copy(block(g, i, me), slot(g, i), local_sems.at[g]).start()
        flips = [f for f in itertools.product((0, 1), repeat=3) if any(f)]
        for k, (fx, fy, fc) in enumerate(flips):
            peer = (1 - x if fx else x, 1 - y if fy else y, 1 - c if fc else c)
            for g in range(n_g):
                for i in range(sizes[g]):
                    pltpu.make_async_remote_copy(
                        src_ref=block(g, i, peer), dst_ref=slot(g, i), send_sem=send_sems.at[g, k],
                        recv_sem=recv_sems.at[g, k], device_id=peer, device_id_type=MESH_T).start()
        for k in range(7):
            for g in range(n_g):
                pltpu.make_async_remote_copy(
                    src_ref=lands[g].at[0], dst_ref=lands[g].at[0], send_sem=send_sems.at[g, k],
                    recv_sem=recv_sems.at[g, k], device_id=me, device_id_type=MESH_T).wait()
        for g in range(n_g):
            pltpu.make_async_copy(lands[g].at[0], lands[g].at[0], local_sems.at[g]).wait()

    any_spec = pl.BlockSpec(memory_space=pl.ANY)
    return pl.pallas_call(
        body, name=name,
        out_shape=[jax.ShapeDtypeStruct((N_DEV, sizes[g] * rows[g], groups[g][0].shape[1]), groups[g][0].dtype)
                   for g in range(n_g)],
        in_specs=[any_spec] * len(flat), out_specs=[any_spec] * n_g,
        scratch_shapes=[pltpu.SemaphoreType.DMA((n_g, 7)), pltpu.SemaphoreType.DMA((n_g, 7)),
                        pltpu.SemaphoreType.DMA((n_g,))],
    )(*flat)


def _sum_slots(land, n_blocks, name):
    _, total, d = land.shape
    rows = total // n_blocks

    def body(l_ref, o_ref):
        s = pl.program_id(1)
        _accumulate(o_ref, l_ref[...].astype(F32), s == 0)

    return pl.pallas_call(
        body, name=name, grid=(n_blocks, N_DEV),
        out_shape=jax.ShapeDtypeStruct((total, d), F32),
        in_specs=[pl.BlockSpec((None, rows, d), lambda w, s: (s, w, 0))],
        out_specs=pl.BlockSpec((rows, d), lambda w, s: (w, 0)),
        compiler_params=_cp(("arbitrary", "arbitrary")),
    )(land)


def _ffn_fwd(x, gn, wg_t, wu_t, wd, name, tm, tf):
    t, d = x.shape
    f_all = wg_t.shape[0]
    n_f = f_all // tf

    def body(x_ref, gn_ref, wg_ref, wu_ref, wd_ref, y_ref, h_ref, gp_ref, up_ref, h_s, acc_s):
        f = pl.program_id(1)

        @pl.when(f == 0)
        def _():
            xv = x_ref[...]
            h = ((xv * _rstd(xv)) * gn_ref[...]).astype(BF16)
            h_s[...] = h
            h_ref[...] = h
            acc_s[...] = jnp.zeros_like(acc_s)

        h = h_s[...]
        g = _dot_nt(h, wg_ref[...])
        u = _dot_nt(h, wu_ref[...])
        gp_ref[...] = g.astype(BF16)
        up_ref[...] = u.astype(BF16)
        hid = (g * jax.nn.sigmoid(g)) * u
        acc_s[...] += _dot_nn(hid.astype(BF16), wd_ref[...])

        @pl.when(f == n_f - 1)
        def _():
            y_ref[...] = x_ref[...] + FFN_RES_WEIGHT * acc_s[...]

    tok = pl.BlockSpec((tm, d), lambda i, f: (i, 0))
    wblk = pl.BlockSpec((tf, d), lambda i, f: (f, 0))
    act = pl.BlockSpec((tm, tf), lambda i, f: (i, f))
    return pl.pallas_call(
        body, name=name, grid=(t // tm, n_f),
        out_shape=[jax.ShapeDtypeStruct((t, d), F32), jax.ShapeDtypeStruct((t, d), BF16),
                   jax.ShapeDtypeStruct((t, f_all), BF16), jax.ShapeDtypeStruct((t, f_all), BF16)],
        in_specs=[tok, pl.BlockSpec((1, d), lambda i, f: (0, 0)), wblk, wblk, wblk],
        out_specs=[tok, tok, act, act],
        scratch_shapes=[pltpu.VMEM((tm, d), BF16), pltpu.VMEM((tm, d), F32)],
        compiler_params=_cp(("arbitrary", "arbitrary")),
    )(x, gn, wg_t, wu_t, wd)


def _ffn_bwd_act(dy, x, gn, gp, up, wg_t, wu_t, wd, name, tm, tf):
    t, d = x.shape
    f_all = wg_t.shape[0]
    n_f = f_all // tf

    def body(dy_ref, x_ref, gn_ref, gp_ref, up_ref, wg_ref, wu_ref, wd_ref,
             dx_ref, dgn_ref, dg_ref, du_ref, hid_ref, dout_ref, dout_s, dh_s):
        i, f = pl.program_id(0), pl.program_id(1)

        @pl.when(f == 0)
        def _():
            dout = (FFN_RES_WEIGHT * dy_ref[...]).astype(BF16)
            dout_s[...] = dout
            dout_ref[...] = dout
            dh_s[...] = jnp.zeros_like(dh_s)

        dhid = _dot_nt(dout_s[...], wd_ref[...])
        g = gp_ref[...].astype(F32)
        u = up_ref[...].astype(F32)
        sg = jax.nn.sigmoid(g)
        silu = g * sg
        dg = ((dhid * u) * (sg * (1.0 + g * (1.0 - sg)))).astype(BF16)
        du = (dhid * silu).astype(BF16)
        dg_ref[...] = dg
        du_ref[...] = du
        hid_ref[...] = (silu * u).astype(BF16)
        dh_s[...] += _dot_nn(dg, wg_ref[...]) + _dot_nn(du, wu_ref[...])

        @pl.when(f == n_f - 1)
        def _():
            xv = x_ref[...]
            dxn, dgn = _norm_bwd(dh_s[...], xv, _rstd(xv), gn_ref[...])
            dx_ref[...] = dy_ref[...] + dxn
            _accumulate(dgn_ref, dgn, i == 0)

    tok = pl.BlockSpec((tm, d), lambda i, f: (i, 0))
    row = pl.BlockSpec((1, d), lambda i, f: (0, 0))
    wblk = pl.BlockSpec((tf, d), lambda i, f: (f, 0))
    act = pl.BlockSpec((tm, tf), lambda i, f: (i, f))
    act_shape = jax.ShapeDtypeStruct((t, f_all), BF16)
    return pl.pallas_call(
        body, name=name, grid=(t // tm, n_f),
        out_shape=[jax.ShapeDtypeStruct((t, d), F32), jax.ShapeDtypeStruct((1, d), F32),
                   act_shape, act_shape, act_shape, jax.ShapeDtypeStruct((t, d), BF16)],
        in_specs=[tok, tok, row, act, act, wblk, wblk, wblk],
        out_specs=[tok, row, act, act, act, tok],
        scratch_shapes=[pltpu.VMEM((tm, d), BF16), pltpu.VMEM((tm, d), F32)],
        compiler_params=_cp(("arbitrary", "arbitrary")),
    )(dy, x, gn, gp, up, wg_t, wu_t, wd)


def _ffn_bwd_weights(dg, du, hid, h, dout, name, tt, tf):
    t, f_all = dg.shape
    d = h.shape[1]
    n_t = t // tt

    def body(dg_ref, du_ref, hid_ref, h_ref, dout_ref, gg_ref, gu_ref, gd_ref, a_g, a_u, a_d):
        s = pl.program_id(1)

        @pl.when(s == 0)
        def _():
            a_g[...] = jnp.zeros_like(a_g)
            a_u[...] = jnp.zeros_like(a_u)
            a_d[...] = jnp.zeros_like(a_d)

        hv = h_ref[...]
        a_g[...] += _dot_tn(dg_ref[...], hv)
        a_u[...] += _dot_tn(du_ref[...], hv)
        a_d[...] += _dot_tn(hid_ref[...], dout_ref[...])

        @pl.when(s == n_t - 1)
        def _():
            gg_ref[...] = a_g[...].astype(BF16)
            gu_ref[...] = a_u[...].astype(BF16)
            gd_ref[...] = a_d[...].astype(BF16)

    act = pl.BlockSpec((tt, tf), lambda f, s: (s, f))
    tok = pl.BlockSpec((tt, d), lambda f, s: (s, 0))
    wblk = pl.BlockSpec((tf, d), lambda f, s: (f, 0))
    w_shape = jax.ShapeDtypeStruct((f_all, d), BF16)
    return pl.pallas_call(
        body, name=name, grid=(f_all // tf, n_t),
        out_shape=[w_shape, w_shape, w_shape],
        in_specs=[act, act, act, tok, tok], out_specs=[wblk, wblk, wblk],
        scratch_shapes=[pltpu.VMEM((tf, d), F32)] * 3,
        compiler_params=_cp(("arbitrary", "arbitrary")),
    )(dg, du, hid, h, dout)


def _swap_halves(t):
    w = t.shape[-1]
    lane = lax.broadcasted_iota(jnp.int32, (1, w), 1)
    return jnp.where((lane % HEAD_DIM) < HEAD_DIM // 2, pltpu.roll(t, w - HEAD_DIM // 2, 1), pltpu.roll(t, HEAD_DIM // 2, 1))


def _rope(t, cos, sin_signed):
    reps = t.shape[-1] // LANES
    return t * jnp.tile(cos, (1, reps)) + _swap_halves(t) * jnp.tile(sin_signed, (1, reps))


def _rope_bwd(dt, cos, sin_signed):
    reps = dt.shape[-1] // LANES
    return dt * jnp.tile(cos, (1, reps)) + _swap_halves(dt * jnp.tile(sin_signed, (1, reps)))


def _in_proj_fwd(x, gn, win_t, cos, sin_signed, name, tm):
    t, d = x.shape
    in_w = win_t.shape[0]

    def body(x_ref, gn_ref, w_ref, cos_ref, sin_ref, h_ref, aq_ref, akx_ref, avx_ref, bq_ref, bk_ref, bv_ref):
        xv = x_ref[...]
        h = ((xv * _rstd(xv)) * gn_ref[...]).astype(BF16)
        h_ref[...] = h
        p = _dot_nt(h, w_ref[...])
        cs, sn = cos_ref[...], sin_ref[...]
        o = 0
        aq_ref[...] = (_rope(p[:, o:o + A_Q_W], cs, sn) * QK_SCALE).astype(BF16)
        o += A_Q_W
        ak = _rope(p[:, o:o + A_KV_W], cs, sn)
        o += A_KV_W
        av = p[:, o:o + A_KV_W]
        o += A_KV_W
        low = lax.broadcasted_iota(jnp.int32, (1, LANES), 1) < HEAD_DIM
        for src, dst in ((ak, akx_ref), (av, avx_ref)):
            other = pltpu.roll(src, HEAD_DIM, 1)
            dst[0] = jnp.where(low, src, other).astype(BF16)
            dst[1] = jnp.where(low, other, src).astype(BF16)
        bq_ref[...] = (_rope(p[:, o:o + B_W], cs, sn) * QK_SCALE).astype(BF16)
        o += B_W
        bk_ref[...] = _rope(p[:, o:o + B_W], cs, sn).astype(BF16)
        o += B_W
        bv_ref[...] = p[:, o:o + B_W].astype(BF16)

    tok = lambda w: pl.BlockSpec((tm, w), lambda i: (i, 0))
    kvx = pl.BlockSpec((2, tm, LANES), lambda i: (0, i, 0))
    sd = lambda *s: jax.ShapeDtypeStruct(s, BF16)
    return pl.pallas_call(
        body, name=name, grid=(t // tm,),
        out_shape=[sd(t, d), sd(t, A_Q_W), sd(2, t, LANES), sd(2, t, LANES), sd(t, B_W), sd(t, B_W), sd(t, B_W)],
        in_specs=[tok(d), pl.BlockSpec((1, d), lambda i: (0, 0)), pl.BlockSpec((in_w, d), lambda i: (0, 0)),
                  tok(LANES), tok(LANES)],
        out_specs=[tok(d), tok(A_Q_W), kvx, kvx, tok(B_W), tok(B_W), tok(B_W)],
        compiler_params=_cp(("arbitrary",)),
    )(x, gn, win_t, cos, sin_signed)


def _in_proj_bwd(dres, x, gn, win_t, h, cos, sin_signed, daq, dakx, davx, dbq, dbk, dbv, name, tm):
    t, d = x.shape
    in_w = win_t.shape[0]
    n_t = t // tm

    def body(dres_ref, x_ref, gn_ref, w_ref, h_ref, cos_ref, sin_ref, daq_ref, dakx_ref, davx_ref, dbq_ref, dbk_ref,
             dbv_ref, dx_ref, dgn_ref, gw_ref, dp_s, gw_s):
        i = pl.program_id(0)
        cs, sn = cos_ref[...], sin_ref[...]
        low = lax.broadcasted_iota(jnp.int32, (1, LANES), 1) < HEAD_DIM

        def fold(ref):
            a, b = ref[0], ref[1]
            return jnp.where(low, a + pltpu.roll(a, HEAD_DIM, 1), b + pltpu.roll(b, HEAD_DIM, 1))

        o = 0
        dp_s[:, o:o + A_Q_W] = _rope_bwd(daq_ref[...] * QK_SCALE, cs, sn).astype(BF16)
        o += A_Q_W
        dp_s[:, o:o + A_KV_W] = _rope_bwd(fold(dakx_ref), cs, sn).astype(BF16)
        o += A_KV_W
        dp_s[:, o:o + A_KV_W] = fold(davx_ref).astype(BF16)
        o += A_KV_W
        dp_s[:, o:o + B_W] = _rope_bwd(dbq_ref[...] * QK_SCALE, cs, sn).astype(BF16)
        o += B_W
        dp_s[:, o:o + B_W] = _rope_bwd(dbk_ref[...], cs, sn).astype(BF16)
        o += B_W
        dp_s[:, o:o + B_W] = dbv_ref[...].astype(BF16)
        dh = _dot_nn(dp_s[...], w_ref[...])
        hv = h_ref[...]
        for c0 in range(0, in_w, 2 * LANES):
            _accumulate(gw_s.at[pl.ds(c0, 2 * LANES), :], _dot_tn(dp_s[:, c0:c0 + 2 * LANES], hv), i == 0)
        xv = x_ref[...]
        dxn, dgn = _norm_bwd(dh, xv, _rstd(xv), gn_ref[...])
        dx_ref[...] = dres_ref[...] + dxn
        _accumulate(dgn_ref, dgn, i == 0)

        @pl.when(i == n_t - 1)
        def _():
            gw_ref[...] = gw_s[...].astype(BF16)

    tok = lambda w: pl.BlockSpec((tm, w), lambda i: (i, 0))
    row = pl.BlockSpec((1, d), lambda i: (0, 0))
    whole = pl.BlockSpec((in_w, d), lambda i: (0, 0))
    kvx = pl.BlockSpec((2, tm, LANES), lambda i: (0, i, 0))
    return pl.pallas_call(
        body, name=name, grid=(n_t,),
        out_shape=[jax.ShapeDtypeStruct((t, d), F32), jax.ShapeDtypeStruct((1, d), F32),
                   jax.ShapeDtypeStruct((in_w, d), BF16)],
        in_specs=[tok(d), tok(d), row, whole, tok(d), tok(LANES), tok(LANES), tok(A_Q_W), kvx, kvx,
                  tok(B_W), tok(B_W), tok(B_W)],
        out_specs=[tok(d), row, whole],
        scratch_shapes=[pltpu.VMEM((tm, in_w), BF16), pltpu.VMEM((in_w, d), F32)],
        compiler_params=_cp(("arbitrary",)),
    )(dres, x, gn, win_t, h, cos, sin_signed, daq, dakx, davx, dbq, dbk, dbv)


def _merge_out_proj_fwd(x, a_out, outs, lses, wout, name, tm):
    t, d = x.shape

    def body(x_ref, a_ref, o0, o1, o2, l0, l1, l2, w_ref, y_ref, b_ref, lt_ref):
        ls = [l0[...], l1[...], l2[...]]
        mx = jnp.maximum(jnp.maximum(ls[0], ls[1]), ls[2])
        es = [jnp.exp(l - mx) for l in ls]
        den = es[0] + es[1] + es[2]
        b = (es[0] * o0[...].astype(F32) + es[1] * o1[...].astype(F32) + es[2] * o2[...].astype(F32)) / den
        bb = b.astype(BF16)
        b_ref[...] = bb
        lt_ref[...] = mx + jnp.log(den)
        y_ref[...] = x_ref[...] + _dot_nn(a_ref[...], w_ref[0:A_Q_W, :]) + _dot_nn(bb, w_ref[A_Q_W:A_Q_W + B_W, :])

    tok = lambda w: pl.BlockSpec((tm, w), lambda i: (i, 0))
    return pl.pallas_call(
        body, name=name, grid=(t // tm,),
        out_shape=[jax.ShapeDtypeStruct((t, d), F32), jax.ShapeDtypeStruct((t, B_W), BF16),
                   jax.ShapeDtypeStruct((t, B_W), F32)],
        in_specs=[tok(d), tok(A_Q_W)] + [tok(B_W)] * 6 + [pl.BlockSpec(wout.shape, lambda i: (0, 0))],
        out_specs=[tok(d), tok(B_W), tok(B_W)],
        compiler_params=_cp(("arbitrary",)),
    )(x, a_out, *outs, *lses, wout)


def _out_proj_bwd(dy, a_out, b_out, wout, name, tm):
    t, d = dy.shape
    n_t = t // tm

    def body(dy_ref, a_ref, b_ref, w_ref, da_ref, db_ref, gw_ref, gw_s):
        i = pl.program_id(0)
        dyb = dy_ref[...].astype(BF16)
        da_ref[...] = _dot_nt(dyb, w_ref[0:A_Q_W, :]).astype(BF16)
        db_ref[...] = _dot_nt(dyb, w_ref[A_Q_W:A_Q_W + B_W, :]).astype(BF16)
        ga = _dot_tn(a_ref[...], dyb)
        gb = _dot_tn(b_ref[...], dyb)

        @pl.when(i == 0)
        def _():
            gw_s[0:A_Q_W, :] = ga
            gw_s[A_Q_W:A_Q_W + B_W, :] = gb

        @pl.when(i > 0)
        def _():
            gw_s[0:A_Q_W, :] += ga
            gw_s[A_Q_W:A_Q_W + B_W, :] += gb

        @pl.when(i == n_t - 1)
        def _():
            gw_ref[...] = gw_s[...].astype(BF16)

    tok = lambda w: pl.BlockSpec((tm, w), lambda i: (i, 0))
    whole = pl.BlockSpec(wout.shape, lambda i: (0, 0))
    return pl.pallas_call(
        body, name=name, grid=(n_t,),
        out_shape=[jax.ShapeDtypeStruct((t, A_Q_W), BF16), jax.ShapeDtypeStruct((t, B_W), BF16),
                   jax.ShapeDtypeStruct(wout.shape, BF16)],
        in_specs=[tok(d), tok(A_Q_W), tok(B_W), whole], out_specs=[tok(A_Q_W), tok(B_W), whole],
        scratch_shapes=[pltpu.VMEM(wout.shape, F32)],
        compiler_params=_cp(("arbitrary",)),
    )(dy, a_out, b_out, wout)


def _band(i, tq, tk, hw, seg_len, t):
    q0 = i * tq
    ws = pl.multiple_of(jnp.clip(q0 - hw, 0, t - tk), 64)
    qpos = q0 + lax.broadcasted_iota(jnp.int32, (tq, 1), 0)
    kpos = ws + lax.broadcasted_iota(jnp.int32, (1, tk), 1)
    seg_lo = (q0 // seg_len) * seg_len
    valid = (jnp.abs(qpos - kpos) <= hw) & (kpos >= seg_lo) & (kpos < seg_lo + seg_len)
    return ws, valid


def _kv_spec(kv, t):
    if kv.ndim == 3:
        return pl.BlockSpec((None, t, LANES), lambda p, i: (p // 2, 0, 0))
    return pl.BlockSpec((t, LANES), lambda p, i: (0, p))


def _attn_fwd(q, k, v, sink, name, hw, seg_len, tq, has_sink):
    t, width = q.shape
    tk = tq + 2 * hw

    def body(sink_ref, q_ref, k_ref, v_ref, o_ref, lse_ref):
        p, i = pl.program_id(0), pl.program_id(1)
        ws, valid = _band(i, tq, tk, hw, seg_len, t)
        kw = k_ref[pl.ds(ws, tk), :]
        vw = v_ref[pl.ds(ws, tk), :]
        low = lax.broadcasted_iota(jnp.int32, (1, LANES), 1) < HEAD_DIM
        qv = q_ref[...]
        outs, lses = [], []
        for a in range(2):
            mine = low if a == 0 else jnp.logical_not(low)
            s = _dot_nt(jnp.where(mine, qv, jnp.zeros_like(qv)), kw)
            s = jnp.where(valid, s, NEG)
            m = jnp.max(s, axis=1, keepdims=True)
            if has_sink:
                sk = sink_ref[2 * p + a]
                m = jnp.maximum(m, sk)
            e = jnp.exp(s - m)
            den = jnp.sum(e, axis=1, keepdims=True)
            if has_sink:
                den = den + jnp.exp(sk - m)
            outs.append(_dot_nn(e.astype(BF16), vw) / den)
            lses.append(m + jnp.log(den))
        o_ref[...] = jnp.where(low, outs[0], outs[1]).astype(BF16)
        lse_ref[...] = jnp.where(low, lses[0], lses[1])

    tile = pl.BlockSpec((tq, LANES), lambda p, i: (i, p))
    return pl.pallas_call(
        body, name=name, grid=(width // LANES, t // tq),
        out_shape=[jax.ShapeDtypeStruct((t, width), BF16), jax.ShapeDtypeStruct((t, width), F32)],
        in_specs=[pl.BlockSpec(memory_space=pltpu.SMEM), tile, _kv_spec(k, t), _kv_spec(v, t)],
        out_specs=[tile, tile],
        compiler_params=_cp(("arbitrary", "arbitrary")),
    )(sink, q, k, v)


def _attn_bwd(q, k, v, o, do, lse, sink, name, hw, seg_len, tq, has_sink):
    t, width = q.shape
    tk = tq + 2 * hw
    n_q = t // tq
    shared_kv = k.ndim == 3

    def body(sink_ref, q_ref, k_ref, v_ref, o_ref, do_ref, lse_ref, dq_ref, dk_ref, dv_ref, ds_ref):
        p, i = pl.program_id(0), pl.program_id(1)
        fresh = (i == 0) & (p % 2 == 0) if shared_kv else i == 0

        @pl.when(fresh)
        def _():
            dk_ref[...] = jnp.zeros_like(dk_ref)
            dv_ref[...] = jnp.zeros_like(dv_ref)

        ws, valid = _band(i, tq, tk, hw, seg_len, t)
        kw = k_ref[pl.ds(ws, tk), :]
        vw = v_ref[pl.ds(ws, tk), :]
        low = lax.broadcasted_iota(jnp.int32, (1, LANES), 1) < HEAD_DIM
        qv, dov = q_ref[...], do_ref[...]
        prod = dov.astype(F32) * o_ref[...].astype(F32)
        lse_t = lse_ref[...]
        dqs, dsinks = [], []
        dk_acc = jnp.zeros((tk, LANES), F32)
        dv_acc = jnp.zeros((tk, LANES), F32)
        for a in range(2):
            mine = low if a == 0 else jnp.logical_not(low)
            qa = jnp.where(mine, qv, jnp.zeros_like(qv))
            doa = jnp.where(mine, dov, jnp.zeros_like(dov))
            lse_a = jnp.max(jnp.where(mine, lse_t, -jnp.inf), axis=1, keepdims=True)
            delta = jnp.sum(jnp.where(mine, prod, 0.0), axis=1, keepdims=True)
            s = jnp.where(valid, _dot_nt(qa, kw), NEG)
            pr = jnp.exp(s - lse_a)
            dpr = _dot_nt(doa, vw)
            dsc = (pr * (dpr - delta)).astype(BF16)
            dv_acc += _dot_tn(pr.astype(BF16), doa)
            dk_acc += _dot_tn(dsc, qa)
            dqs.append(_dot_nn(dsc, kw))
            if has_sink:
                dsinks.append(-jnp.sum(jnp.exp(sink_ref[2 * p + a] - lse_a) * delta, axis=0, keepdims=True))
        dq_ref[...] = jnp.where(low, dqs[0], dqs[1])
        dk_ref[pl.ds(ws, tk), :] += dk_acc
        dv_ref[pl.ds(ws, tk), :] += dv_acc
        if has_sink:
            ds_ref[...] = jnp.broadcast_to(jnp.where(low, dsinks[0], dsinks[1]), ds_ref.shape)
        else:
            ds_ref[...] = jnp.zeros_like(ds_ref)

    tile = pl.BlockSpec((tq, LANES), lambda p, i: (i, p))
    kv_shape = jax.ShapeDtypeStruct(k.shape, F32)
    return pl.pallas_call(
        body, name=name, grid=(width // LANES, n_q),
        out_shape=[jax.ShapeDtypeStruct((t, width), F32), kv_shape, kv_shape,
                   jax.ShapeDtypeStruct((width // LANES, n_q, 8, LANES), F32)],
        in_specs=[pl.BlockSpec(memory_space=pltpu.SMEM), tile, _kv_spec(k, t), _kv_spec(v, t), tile, tile, tile],
        out_specs=[tile, _kv_spec(k, t), _kv_spec(v, t), pl.BlockSpec((None, None, 8, LANES), lambda p, i: (p, i, 0, 0))],
        compiler_params=_cp(("arbitrary", "arbitrary")),
    )(sink, q, k, v, o, do, lse)


def _final_norm_loss(x, gn, target, name, tm):
    t, d = x.shape

    def body(x_ref, gn_ref, tg_ref, dx_ref, dgn_ref, sq_ref):
        i = pl.program_id(0)
        xv, gnv = x_ref[...], gn_ref[...]
        r = _rstd(xv)
        diff = (xv * r) * gnv - tg_ref[...]
        dxn, dgn = _norm_bwd(diff * (1.0 / d), xv, r, gnv)
        dx_ref[...] = dxn
        _accumulate(dgn_ref, dgn, i == 0)
        _accumulate(sq_ref, jnp.sum(diff * diff, axis=0, keepdims=True), i == 0)

    tok = pl.BlockSpec((tm, d), lambda i: (i, 0))
    row = pl.BlockSpec((1, d), lambda i: (0, 0))
    return pl.pallas_call(
        body, name=name, grid=(t // tm,),
        out_shape=[jax.ShapeDtypeStruct((t, d), F32), jax.ShapeDtypeStruct((1, d), F32), jax.ShapeDtypeStruct((1, d), F32)],
        in_specs=[tok, row, tok], out_specs=[tok, row, row],
        compiler_params=_cp(("arbitrary",)),
    )(x, gn, target)


def _adamw(w, g, m, v, name):
    def body(w_ref, g_ref, m_ref, v_ref, d_ref, nm_ref, nv_ref):
        gv = g_ref[...]
        nm = ADAM_B1 * m_ref[...] + (1.0 - ADAM_B1) * gv
        nv = ADAM_B2 * v_ref[...] + (1.0 - ADAM_B2) * (gv * gv)
        m_hat = nm / (1.0 - ADAM_B1 ** ADAM_STEP)
        v_hat = nv / (1.0 - ADAM_B2 ** ADAM_STEP)
        d_ref[...] = -ADAM_LR * (m_hat / (jnp.sqrt(v_hat) + ADAM_EPS) + ADAM_WD * w_ref[...])
        nm_ref[...] = nm
        nv_ref[...] = nv

    shape = jax.ShapeDtypeStruct(w.shape, F32)
    return pl.pallas_call(body, name=name, out_shape=[shape, shape, shape], compiler_params=_cp())(w, g, m, v)


def _sum_small(land, name):
    def body(l_ref, o_ref):
        acc = l_ref[0]
        for s in range(1, N_DEV):
            acc = acc + l_ref[s]
        o_ref[...] = acc

    return pl.pallas_call(body, name=name, out_shape=jax.ShapeDtypeStruct(land.shape[1:], F32), compiler_params=_cp())(land)


def _to_segments(a, dil):
    if dil == 1:
        return a
    t, w = a.shape
    return a.reshape(t // dil, dil, w).transpose(1, 0, 2).reshape(t, w)


def _from_segments(a, dil):
    if dil == 1:
        return a
    t, w = a.shape
    return a.reshape(dil, t // dil, w).transpose(1, 0, 2).reshape(t, w)


def _rope_lanes(positions):
    inv_freq = 1.0 / (ROPE_THETA ** (jnp.arange(0, HEAD_DIM, 2, dtype=F32) / HEAD_DIM))
    ang = positions.astype(F32)[:, None] * inv_freq
    cos, sin = jnp.cos(ang), jnp.sin(ang)
    return jnp.concatenate([cos, cos, cos, cos], axis=1), jnp.concatenate([-sin, sin, -sin, sin], axis=1)


def kernel(x, positions, norm_ffn1, w_gate1, w_up1, w_down1, norm_mix, w_in, a_sink, w_out, norm_ffn2, w_gate2, w_up2, w_down2, norm_final, loss_target, m_norm_ffn1, m_w_gate1, m_w_up1, m_w_down1, m_norm_mix, m_w_in, m_a_sink, m_w_out, m_norm_ffn2, m_w_gate2, m_w_up2, m_w_down2, m_norm_final, v_norm_ffn1, v_w_gate1, v_w_up1, v_w_down1, v_norm_mix, v_w_in, v_a_sink, v_w_out, v_norm_ffn2, v_w_gate2, v_w_up2, v_w_down2, v_norm_final):
    x = x[0]
    target = loss_target[0]
    t, d = x.shape
    tm = min(1024, t)
    tm_mix = min(512, t)
    tf = 256
    tq = min(256, t // 16)

    shards = [w_gate1[0].T, w_up1[0].T, w_down1[0], w_in[0].T, w_out[0], w_gate2[0].T, w_up2[0].T, w_down2[0]]
    rows_list = [s.shape[0] for s in shards]
    packed = jnp.concatenate([s.astype(BF16) for s in shards], axis=0)
    wg1, wu1, wd1, win, wout, wg2, wu2, wd2 = _all_gather_rows(packed, rows_list, "gather_weights")

    cos, sin_signed = _rope_lanes(positions[0])
    sink = a_sink[0]
    no_sink = jnp.zeros_like(sink)

    x1, h1, gp1, up1 = _ffn_fwd(x, norm_ffn1, wg1, wu1, wd1, "ffn1_fwd", tm, tf)
    h_mix, aq, akx, avx, bq, bk, bv = _in_proj_fwd(x1, norm_mix, win, cos, sin_signed, "in_proj_fwd", tm_mix)
    a_out, a_lse = _attn_fwd(aq, akx, avx, sink, "attn_a_fwd", A_HALF_WINDOW, t, tq, True)
    b_seg, b_outs, b_lses = [], [], []
    for window, dil in B_PATTERNS:
        qs, ks, vs = (_to_segments(a, dil) for a in (bq, bk, bv))
        o_seg, lse_seg = _attn_fwd(qs, ks, vs, no_sink, f"attn_b{dil}_fwd", window // (2 * dil), t // dil, tq, False)
        b_seg.append((qs, ks, vs))
        b_outs.append(_from_segments(o_seg, dil))
        b_lses.append(_from_segments(lse_seg, dil))
    x2, b_out, b_lse = _merge_out_proj_fwd(x1, a_out, b_outs, b_lses, wout, "out_proj_fwd", tm_mix)
    x3, h2, gp2, up2 = _ffn_fwd(x2, norm_ffn2, wg2, wu2, wd2, "ffn2_fwd", tm, tf)

    gfinal = norm_final.reshape(1, d)
    dx3, dg_final, sq = _final_norm_loss(x3, gfinal, target, "final_norm_loss", tm)
    loss = lax.psum(0.5 * jnp.sum(sq) / d, ("x", "y", "c"))

    dx2, dg_ffn2, dgt2, dut2, hid2, dout2 = _ffn_bwd_act(dx3, x2, norm_ffn2, gp2, up2, wg2, wu2, wd2, "ffn2_bwd_act", tm, tf)
    gwg2, gwu2, gwd2 = _ffn_bwd_weights(dgt2, dut2, hid2, h2, dout2, "ffn2_bwd_weights", tm, tf)

    da, db, gwout = _out_proj_bwd(dx2, a_out, b_out, wout, "out_proj_bwd", tm_mix)
    daq, dakx, davx, dsink_parts = _attn_bwd(aq, akx, avx, a_out, da, a_lse, sink, "attn_a_bwd", A_HALF_WINDOW, t, tq, True)
    dbq = dbk = dbv = None
    for (window, dil), (qs, ks, vs) in zip(B_PATTERNS, b_seg):
        os_, dos, ls = (_to_segments(a, dil) for a in (b_out, db, b_lse))
        dq_s, dk_s, dv_s, _ = _attn_bwd(qs, ks, vs, os_, dos, ls, no_sink, f"attn_b{dil}_bwd", window // (2 * dil), t // dil, tq, False)
        parts = [_from_segments(a, dil) for a in (dq_s, dk_s, dv_s)]
        dbq, dbk, dbv = parts if dbq is None else (dbq + parts[0], dbk + parts[1], dbv + parts[2])
    dx1, dg_mix, gwin = _in_proj_bwd(dx2, x1, norm_mix, win, h_mix, cos, sin_signed, daq, dakx, davx, dbq, dbk, dbv, "in_proj_bwd", tm_mix)

    grad_x, dg_ffn1, dgt1, dut1, hid1, dout1 = _ffn_bwd_act(dx1, x, norm_ffn1, gp1, up1, wg1, wu1, wd1, "ffn1_bwd_act", tm, tf)
    gwg1, gwu1, gwd1 = _ffn_bwd_weights(dgt1, dut1, hid1, h1, dout1, "ffn1_bwd_weights", tm, tf)

    dsink_pairs = jnp.sum(dsink_parts[:, :, 0, :], axis=1)
    dsink = jnp.stack([dsink_pairs[:, 0], dsink_pairs[:, HEAD_DIM]], axis=1).reshape(1, -1)
    small = jnp.concatenate([dg_ffn1, dg_mix, dg_ffn2, dg_final, jnp.pad(dsink, ((0, 0), (0, d - dsink.shape[1]))),
                             jnp.zeros((3, d), F32)], axis=0)
    land_ffn, land_in, land_out, land_small = _exchange_partials(
        [[gwg1, gwu1, gwd1, gwg2, gwu2, gwd2], [gwin], [gwout], [jnp.tile(small, (N_DEV, 1))]], "scatter_gradients")
    red_ffn = _sum_slots(land_ffn, 6, "sum_ffn_grads")
    red_in = _sum_slots(land_in, 1, "sum_in_grads")
    red_out = _sum_slots(land_out, 1, "sum_out_grads")
    red_small = _sum_small(land_small, "sum_small_grads")

    rf = rows_list[0]
    ffn_block = lambda j: red_ffn[j * rf:(j + 1) * rf]
    n_sink = a_sink.shape[1]
    grads = {
        "norm_ffn1": red_small[0:1], "w_gate1": ffn_block(0).T[None], "w_up1": ffn_block(1).T[None], "w_down1": ffn_block(2)[None],
        "norm_mix": red_small[1:2], "w_in": red_in.T[None], "a_sink": red_small[4:5, :n_sink], "w_out": red_out[None],
        "norm_ffn2": red_small[2:3], "w_gate2": ffn_block(3).T[None], "w_up2": ffn_block(4).T[None], "w_down2": ffn_block(5)[None],
        "norm_final": red_small[3],
    }
    params = {
        "norm_ffn1": (norm_ffn1, m_norm_ffn1, v_norm_ffn1), "w_gate1": (w_gate1, m_w_gate1, v_w_gate1),
        "w_up1": (w_up1, m_w_up1, v_w_up1), "w_down1": (w_down1, m_w_down1, v_w_down1),
        "norm_mix": (norm_mix, m_norm_mix, v_norm_mix), "w_in": (w_in, m_w_in, v_w_in),
        "a_sink": (a_sink, m_a_sink, v_a_sink), "w_out": (w_out, m_w_out, v_w_out),
        "norm_ffn2": (norm_ffn2, m_norm_ffn2, v_norm_ffn2), "w_gate2": (w_gate2, m_w_gate2, v_w_gate2),
        "w_up2": (w_up2, m_w_up2, v_w_up2), "w_down2": (w_down2, m_w_down2, v_w_down2),
        "norm_final": (norm_final, m_norm_final, v_norm_final),
    }
    deltas, new_ms, new_vs = [], [], []
    for name, (w, m, v) in params.items():
        two_d = (-1, w.shape[-1])
        g2 = grads[name].reshape(two_d)
        dl, nm, nv = _adamw(w.reshape(two_d), g2, m.reshape(two_d), v.reshape(two_d), f"adamw_{name}")
        deltas.append(dl.reshape(w.shape))
        new_ms.append(nm.reshape(w.shape))
        new_vs.append(nv.reshape(w.shape))
    grad_list = [grads[name].reshape(params[name][0].shape) for name in params]
    return (loss, grad_x[None], *grad_list, *deltas, *new_ms, *new_vs)
```

```python
import functools
import itertools

import numpy as np
import jax
import jax.numpy as jnp
from jax import lax
from jax.experimental import pallas as pl
from jax.experimental.pallas import tpu as pltpu

F32 = jnp.float32
BF16 = jnp.bfloat16

N_DEV = 8
HEAD_DIM = 64
LANES = 128
A_Q_W, A_KV_W, B_W = 512, 128, 512
A_HALF_WINDOW = 128
B_PATTERNS = ((128, 1), (512, 4), (2048, 16))
ROPE_THETA = 10000.0
NORM_EPS = 1e-6
FFN_RES_WEIGHT = 0.5
QK_SCALE = HEAD_DIM ** -0.5
NEG = -1e30

ADAM_LR = 0.001
ADAM_B1 = 0.9
ADAM_B2 = 0.999
ADAM_EPS = 1e-08
ADAM_WD = 0.01
ADAM_STEP = 10

MESH_T = pl.DeviceIdType.MESH
VMEM_LIMIT = 56 * 1024 * 1024


def _cp(sem=None, vmem=VMEM_LIMIT):
    return pltpu.CompilerParams(dimension_semantics=sem, vmem_limit_bytes=vmem)


def _dot_nn(a, b):
    return jnp.dot(a, b, preferred_element_type=F32)


def _dot_nt(a, b):
    return lax.dot_general(a, b, (((1,), (1,)), ((), ())), preferred_element_type=F32)


def _dot_tn(a, b):
    return lax.dot_general(a, b, (((0,), (0,)), ((), ())), preferred_element_type=F32)


def _rstd(xv):
    return lax.rsqrt(jnp.mean(xv * xv, axis=-1, keepdims=True) + NORM_EPS)


def _norm_bwd(dh, xv, r, gn):
    gy = dh * gn
    c = jnp.sum(gy * xv, axis=-1, keepdims=True) * (1.0 / xv.shape[-1])
    dx = r * gy - xv * (r * r * r * c)
    dgn = jnp.sum(dh * (xv * r), axis=0, keepdims=True)
    return dx, dgn


def _accumulate(ref, val, first):
    @pl.when(first)
    def _():
        ref[...] = val

    @pl.when(jnp.logical_not(first))
    def _():
        ref[...] += val


def _mesh_pos():
    return lax.axis_index("x"), lax.axis_index("y"), lax.axis_index("c")


def _dev_index(d):
    return 4 * d[0] + 2 * d[1] + d[2]


def _all_gather_rows(packed, rows_list, name):
    n_w = len(rows_list)
    offs = [int(o) for o in np.cumsum([0] + list(rows_list[:-1]))]
    d = packed.shape[1]

    def body(p_ref, *refs):
        outs = refs[:n_w]
        send_sems, recv_sems, local_sem = refs[n_w:]
        x, y, c = _mesh_pos()
        me, sibling = (x, y, c), (x, y, 1 - c)
        chips = [(1 - x, y), (x, 1 - y), (1 - x, 1 - y)]

        def rows(w, dev):
            start = pl.multiple_of(_dev_index(dev) * rows_list[w], 16)
            return outs[w].at[pl.ds(start, rows_list[w]), :]

        def mine(w):
            return p_ref.at[pl.ds(offs[w], rows_list[w]), :]

        def copy(k, w, block, to, own):
            return pltpu.make_async_remote_copy(
                src_ref=mine(w) if own else rows(w, block), dst_ref=rows(w, block),
                send_sem=send_sems.at[k], recv_sem=recv_sems.at[k], device_id=to, device_id_type=MESH_T)

        def all_blocks(k):
            return pltpu.make_async_remote_copy(
                src_ref=p_ref, dst_ref=p_ref, send_sem=send_sems.at[k], recv_sem=recv_sems.at[k],
                device_id=me, device_id_type=MESH_T)

        for w in range(n_w):
            pltpu.make_async_copy(mine(w), rows(w, me), local_sem).start()
        for w in range(n_w):
            copy(0, w, me, sibling, True).start()
        for j, chip in enumerate(chips):
            for w in range(n_w):
                copy(1 + j, w, me, (*chip, c), True).start()
        for j, chip in enumerate(chips):
            all_blocks(1 + j).wait_recv()
            for w in range(n_w):
                copy(4 + j, w, (*chip, c), sibling, False).start()
        all_blocks(0).wait_recv()
        for j in range(3):
            all_blocks(4 + j).wait_recv()
        for k in range(7):
            all_blocks(k).wait_send()
        pltpu.make_async_copy(p_ref, p_ref, local_sem).wait()

    any_spec = pl.BlockSpec(memory_space=pl.ANY)
    return pl.pallas_call(
        body, name=name,
        out_shape=[jax.ShapeDtypeStruct((N_DEV * r, d), packed.dtype) for r in rows_list],
        in_specs=[any_spec], out_specs=[any_spec] * n_w,
        scratch_shapes=[pltpu.SemaphoreType.DMA((7,)), pltpu.SemaphoreType.DMA((7,)), pltpu.SemaphoreType.DMA],
    )(packed)


def _exchange_partials(groups, name):
    flat = [a for g in groups for a in g]
    n_g = len(groups)
    sizes = [len(g) for g in groups]
    rows = [g[0].shape[0] // N_DEV for g in groups]
    first = [int(o) for o in np.cumsum([0] + sizes[:-1])]

    def body(*refs):
        srcs = refs[:len(flat)]
        lands = refs[len(flat):len(flat) + n_g]
        send_sems, recv_sems, local_sems = refs[len(flat) + n_g:]
        x, y, c = _mesh_pos()
        me = (x, y, c)
        me_idx = _dev_index(me)

        def block(g, i, dev):
            start = pl.multiple_of(_dev_index(dev) * rows[g], 8)
            return srcs[first[g] + i].at[pl.ds(start, rows[g]), :]

        def slot(g, i):
            return lands[g].at[me_idx, pl.ds(i * rows[g], rows[g]), :]

        for g in range(n_g):
            for i in range(sizes[g]):
                pltpu.make_async_copy(block(g, i, me), slot(g, i), local_sems.at[g]).start()
        flips = [f for f in itertools.product((0, 1), repeat=3) if any(f)]
        for k, (fx, fy, fc) in enumerate(flips):
            peer = (1 - x if fx else x, 1 - y if fy else y, 1 - c if fc else c)
            for g in range(n_g):
                for i in range(sizes[g]):
                    pltpu.make_async_remote_copy(
                        src_ref=block(g, i, peer), dst_ref=slot(g, i), send_sem=send_sems.at[g, k],
                        recv_sem=recv_sems.at[g, k], device_id=peer, device_id_type=MESH_T).start()
        for k in range(7):
            for g in range(n_g):
                pltpu.make_async_remote_copy(
                    src_ref=lands[g].at[0], dst_ref=lands[g].at[0], send_sem=send_sems.at[g, k],
                    recv_sem=recv_sems.at[g, k], device_id=me, device_id_type=MESH_T).wait()
        for g in range(n_g):
            pltpu.make_async_copy(lands[g].at[0], lands[g].at[0], local_sems.at[g]).wait()

    any_spec = pl.BlockSpec(memory_space=pl.ANY)
    return pl.pallas_call(
        body, name=name,
        out_shape=[jax.ShapeDtypeStruct((N_DEV, sizes[g] * rows[g], groups[g][0].shape[1]), groups[g][0].dtype)
                   for g in range(n_g)],
        in_specs=[any_spec] * len(flat), out_specs=[any_spec] * n_g,
        scratch_shapes=[pltpu.SemaphoreType.DMA((n_g, 7)), pltpu.SemaphoreType.DMA((n_g, 7)),
                        pltpu.SemaphoreType.DMA((n_g,))],
    )(*flat)


def _sum_slots(land, n_blocks, name):
    _, total, d = land.shape
    rows = total // n_blocks

    def body(l_ref, o_ref):
        s = pl.program_id(1)
        _accumulate(o_ref, l_ref[...].astype(F32), s == 0)

    return pl.pallas_call(
        body, name=name, grid=(n_blocks, N_DEV),
        out_shape=jax.ShapeDtypeStruct((total, d), F32),
        in_specs=[pl.BlockSpec((None, rows, d), lambda w, s: (s, w, 0))],
        out_specs=pl.BlockSpec((rows, d), lambda w, s: (w, 0)),
        compiler_params=_cp(("arbitrary", "arbitrary")),
    )(land)


def _ffn_fwd(x, gn, wg_t, wu_t, wd, name, tm, tf):
    t, d = x.shape
    f_all = wg_t.shape[0]
    n_f = f_all // tf

    def body(x_ref, gn_ref, wg_ref, wu_ref, wd_ref, y_ref, h_ref, gp_ref, up_ref, h_s, acc_s):
        f = pl.program_id(1)

        @pl.when(f == 0)
        def _():
            xv = x_ref[...]
            h = ((xv * _rstd(xv)) * gn_ref[...]).astype(BF16)
            h_s[...] = h
            h_ref[...] = h
            acc_s[...] = jnp.zeros_like(acc_s)

        h = h_s[...]
        g = _dot_nt(h, wg_ref[...])
        u = _dot_nt(h, wu_ref[...])
        gp_ref[...] = g.astype(BF16)
        up_ref[...] = u.astype(BF16)
        hid = (g * jax.nn.sigmoid(g)) * u
        acc_s[...] += _dot_nn(hid.astype(BF16), wd_ref[...])

        @pl.when(f == n_f - 1)
        def _():
            y_ref[...] = x_ref[...] + FFN_RES_WEIGHT * acc_s[...]

    tok = pl.BlockSpec((tm, d), lambda i, f: (i, 0))
    wblk = pl.BlockSpec((tf, d), lambda i, f: (f, 0))
    act = pl.BlockSpec((tm, tf), lambda i, f: (i, f))
    return pl.pallas_call(
        body, name=name, grid=(t // tm, n_f),
        out_shape=[jax.ShapeDtypeStruct((t, d), F32), jax.ShapeDtypeStruct((t, d), BF16),
                   jax.ShapeDtypeStruct((t, f_all), BF16), jax.ShapeDtypeStruct((t, f_all), BF16)],
        in_specs=[tok, pl.BlockSpec((1, d), lambda i, f: (0, 0)), wblk, wblk, wblk],
        out_specs=[tok, tok, act, act],
        scratch_shapes=[pltpu.VMEM((tm, d), BF16), pltpu.VMEM((tm, d), F32)],
        compiler_params=_cp(("arbitrary", "arbitrary")),
    )(x, gn, wg_t, wu_t, wd)


def _ffn_bwd_act(dy, x, gn, gp, up, wg_t, wu_t, wd, name, tm, tf):
    t, d = x.shape
    f_all = wg_t.shape[0]
    n_f = f_all // tf

    def body(dy_ref, x_ref, gn_ref, gp_ref, up_ref, wg_ref, wu_ref, wd_ref,
             dx_ref, dgn_ref, dg_ref, du_ref, hid_ref, dout_ref, dout_s, dh_s):
        i, f = pl.program_id(0), pl.program_id(1)

        @pl.when(f == 0)
        def _():
            dout = (FFN_RES_WEIGHT * dy_ref[...]).astype(BF16)
            dout_s[...] = dout
            dout_ref[...] = dout
            dh_s[...] = jnp.zeros_like(dh_s)

        dhid = _dot_nt(dout_s[...], wd_ref[...])
        g = gp_ref[...].astype(F32)
        u = up_ref[...].astype(F32)
        sg = jax.nn.sigmoid(g)
        silu = g * sg
        dg = ((dhid * u) * (sg * (1.0 + g * (1.0 - sg)))).astype(BF16)
        du = (dhid * silu).astype(BF16)
        dg_ref[...] = dg
        du_ref[...] = du
        hid_ref[...] = (silu * u).astype(BF16)
        dh_s[...] += _dot_nn(dg, wg_ref[...]) + _dot_nn(du, wu_ref[...])

        @pl.when(f == n_f - 1)
        def _():
            xv = x_ref[...]
            dxn, dgn = _norm_bwd(dh_s[...], xv, _rstd(xv), gn_ref[...])
            dx_ref[...] = dy_ref[...] + dxn
            _accumulate(dgn_ref, dgn, i == 0)

    tok = pl.BlockSpec((tm, d), lambda i, f: (i, 0))
    row = pl.BlockSpec((1, d), lambda i, f: (0, 0))
    wblk = pl.BlockSpec((tf, d), lambda i, f: (f, 0))
    act = pl.BlockSpec((tm, tf), lambda i, f: (i, f))
    act_shape = jax.ShapeDtypeStruct((t, f_all), BF16)
    return pl.pallas_call(
        body, name=name, grid=(t // tm, n_f),
        out_shape=[jax.ShapeDtypeStruct((t, d), F32), jax.ShapeDtypeStruct((1, d), F32),
                   act_shape, act_shape, act_shape, jax.ShapeDtypeStruct((t, d), BF16)],
        in_specs=[tok, tok, row, act, act, wblk, wblk, wblk],
        out_specs=[tok, row, act, act, act, tok],
        scratch_shapes=[pltpu.VMEM((tm, d), BF16), pltpu.VMEM((tm, d), F32)],
        compiler_params=_cp(("arbitrary", "arbitrary")),
    )(dy, x, gn, gp, up, wg_t, wu_t, wd)


def _ffn_bwd_weights(dg, du, hid, h, dout, name, tt, tf):
    t, f_all = dg.shape
    d = h.shape[1]
    n_t = t // tt

    def body(dg_ref, du_ref, hid_ref, h_ref, dout_ref, gg_ref, gu_ref, gd_ref, a_g, a_u, a_d):
        s = pl.program_id(1)

        @pl.when(s == 0)
        def _():
            a_g[...] = jnp.zeros_like(a_g)
            a_u[...] = jnp.zeros_like(a_u)
            a_d[...] = jnp.zeros_like(a_d)

        hv = h_ref[...]
        a_g[...] += _dot_tn(dg_ref[...], hv)
        a_u[...] += _dot_tn(du_ref[...], hv)
        a_d[...] += _dot_tn(hid_ref[...], dout_ref[...])

        @pl.when(s == n_t - 1)
        def _():
            gg_ref[...] = a_g[...].astype(BF16)
            gu_ref[...] = a_u[...].astype(BF16)
            gd_ref[...] = a_d[...].astype(BF16)

    act = pl.BlockSpec((tt, tf), lambda f, s: (s, f))
    tok = pl.BlockSpec((tt, d), lambda f, s: (s, 0))
    wblk = pl.BlockSpec((tf, d), lambda f, s: (f, 0))
    w_shape = jax.ShapeDtypeStruct((f_all, d), BF16)
    return pl.pallas_call(
        body, name=name, grid=(f_all // tf, n_t),
        out_shape=[w_shape, w_shape, w_shape],
        in_specs=[act, act, act, tok, tok], out_specs=[wblk, wblk, wblk],
        scratch_shapes=[pltpu.VMEM((tf, d), F32)] * 3,
        compiler_params=_cp(("arbitrary", "arbitrary")),
    )(dg, du, hid, h, dout)


def _swap_halves(t):
    w = t.shape[-1]
    lane = lax.broadcasted_iota(jnp.int32, (1, w), 1)
    return jnp.where((lane % HEAD_DIM) < HEAD_DIM // 2, pltpu.roll(t, w - HEAD_DIM // 2, 1), pltpu.roll(t, HEAD_DIM // 2, 1))


def _rope(t, cos, sin_signed):
    reps = t.shape[-1] // LANES
    return t * jnp.tile(cos, (1, reps)) + _swap_halves(t) * jnp.tile(sin_signed, (1, reps))


def _rope_bwd(dt, cos, sin_signed):
    reps = dt.shape[-1] // LANES
    return dt * jnp.tile(cos, (1, reps)) + _swap_halves(dt * jnp.tile(sin_signed, (1, reps)))


def _in_proj_fwd(x, gn, win_t, cos, sin_signed, name, tm):
    t, d = x.shape
    in_w = win_t.shape[0]

    def body(x_ref, gn_ref, w_ref, cos_ref, sin_ref, h_ref, aq_ref, akx_ref, avx_ref, bq_ref, bk_ref, bv_ref):
        xv = x_ref[...]
        h = ((xv * _rstd(xv)) * gn_ref[...]).astype(BF16)
        h_ref[...] = h
        p = _dot_nt(h, w_ref[...])
        cs, sn = cos_ref[...], sin_ref[...]
        o = 0
        aq_ref[...] = (_rope(p[:, o:o + A_Q_W], cs, sn) * QK_SCALE).astype(BF16)
        o += A_Q_W
        ak = _rope(p[:, o:o + A_KV_W], cs, sn)
        o += A_KV_W
        av = p[:, o:o + A_KV_W]
        o += A_KV_W
        low = lax.broadcasted_iota(jnp.int32, (1, LANES), 1) < HEAD_DIM
        for src, dst in ((ak, akx_ref), (av, avx_ref)):
            other = pltpu.roll(src, HEAD_DIM, 1)
            dst[0] = jnp.where(low, src, other).astype(BF16)
            dst[1] = jnp.where(low, other, src).astype(BF16)
        bq_ref[...] = (_rope(p[:, o:o + B_W], cs, sn) * QK_SCALE).astype(BF16)
        o += B_W
        bk_ref[...] = _rope(p[:, o:o + B_W], cs, sn).astype(BF16)
        o += B_W
        bv_ref[...] = p[:, o:o + B_W].astype(BF16)

    tok = lambda w: pl.BlockSpec((tm, w), lambda i: (i, 0))
    kvx = pl.BlockSpec((2, tm, LANES), lambda i: (0, i, 0))
    sd = lambda *s: jax.ShapeDtypeStruct(s, BF16)
    return pl.pallas_call(
        body, name=name, grid=(t // tm,),
        out_shape=[sd(t, d), sd(t, A_Q_W), sd(2, t, LANES), sd(2, t, LANES), sd(t, B_W), sd(t, B_W), sd(t, B_W)],
        in_specs=[tok(d), pl.BlockSpec((1, d), lambda i: (0, 0)), pl.BlockSpec((in_w, d), lambda i: (0, 0)),
                  tok(LANES), tok(LANES)],
        out_specs=[tok(d), tok(A_Q_W), kvx, kvx, tok(B_W), tok(B_W), tok(B_W)],
        compiler_params=_cp(("arbitrary",)),
    )(x, gn, win_t, cos, sin_signed)


def _in_proj_bwd(dres, x, gn, win_t, h, cos, sin_signed, daq, dakx, davx, dbq, dbk, dbv, name, tm):
    t, d = x.shape
    in_w = win_t.shape[0]
    n_t = t // tm

    def body(dres_ref, x_ref, gn_ref, w_ref, h_ref, cos_ref, sin_ref, daq_ref, dakx_ref, davx_ref, dbq_ref, dbk_ref,
             dbv_ref, dx_ref, dgn_ref, gw_ref, dp_s, gw_s):
        i = pl.program_id(0)
        cs, sn = cos_ref[...], sin_ref[...]
        low = lax.broadcasted_iota(jnp.int32, (1, LANES), 1) < HEAD_DIM

        def fold(ref):
            a, b = ref[0], ref[1]
            return jnp.where(low, a + pltpu.roll(a, HEAD_DIM, 1), b + pltpu.roll(b, HEAD_DIM, 1))

        o = 0
        dp_s[:, o:o + A_Q_W] = _rope_bwd(daq_ref[...] * QK_SCALE, cs, sn).astype(BF16)
        o += A_Q_W
        dp_s[:, o:o + A_KV_W] = _rope_bwd(fold(dakx_ref), cs, sn).astype(BF16)
        o += A_KV_W
        dp_s[:, o:o + A_KV_W] = fold(davx_ref).astype(BF16)
        o += A_KV_W
        dp_s[:, o:o + B_W] = _rope_bwd(dbq_ref[...] * QK_SCALE, cs, sn).astype(BF16)
        o += B_W
        dp_s[:, o:o + B_W] = _rope_bwd(dbk_ref[...], cs, sn).astype(BF16)
        o += B_W
        dp_s[:, o:o + B_W] = dbv_ref[...].astype(BF16)
        dh = _dot_nn(dp_s[...], w_ref[...])
        hv = h_ref[...]
        for c0 in range(0, in_w, 2 * LANES):
            _accumulate(gw_s.at[pl.ds(c0, 2 * LANES), :], _dot_tn(dp_s[:, c0:c0 + 2 * LANES], hv), i == 0)
        xv = x_ref[...]
        dxn, dgn = _norm_bwd(dh, xv, _rstd(xv), gn_ref[...])
        dx_ref[...] = dres_ref[...] + dxn
        _accumulate(dgn_ref, dgn, i == 0)

        @pl.when(i == n_t - 1)
        def _():
            gw_ref[...] = gw_s[...].astype(BF16)

    tok = lambda w: pl.BlockSpec((tm, w), lambda i: (i, 0))
    row = pl.BlockSpec((1, d), lambda i: (0, 0))
    whole = pl.BlockSpec((in_w, d), lambda i: (0, 0))
    kvx = pl.BlockSpec((2, tm, LANES), lambda i: (0, i, 0))
    return pl.pallas_call(
        body, name=name, grid=(n_t,),
        out_shape=[jax.ShapeDtypeStruct((t, d), F32), jax.ShapeDtypeStruct((1, d), F32),
                   jax.ShapeDtypeStruct((in_w, d), BF16)],
        in_specs=[tok(d), tok(d), row, whole, tok(d), tok(LANES), tok(LANES), tok(A_Q_W), kvx, kvx,
                  tok(B_W), tok(B_W), tok(B_W)],
        out_specs=[tok(d), row, whole],
        scratch_shapes=[pltpu.VMEM((tm, in_w), BF16), pltpu.VMEM((in_w, d), F32)],
        compiler_params=_cp(("arbitrary",)),
    )(dres, x, gn, win_t, h, cos, sin_signed, daq, dakx, davx, dbq, dbk, dbv)


def _merge_out_proj_fwd(x, a_out, outs, lses, wout, name, tm):
    t, d = x.shape

    def body(x_ref, a_ref, o0, o1, o2, l0, l1, l2, w_ref, y_ref, b_ref, lt_ref):
        ls = [l0[...], l1[...], l2[...]]
        mx = jnp.maximum(jnp.maximum(ls[0], ls[1]), ls[2])
        es = [jnp.exp(l - mx) for l in ls]
        den = es[0] + es[1] + es[2]
        b = (es[0] * o0[...].astype(F32) + es[1] * o1[...].astype(F32) + es[2] * o2[...].astype(F32)) / den
        bb = b.astype(BF16)
        b_ref[...] = bb
        lt_ref[...] = mx + jnp.log(den)
        y_ref[...] = x_ref[...] + _dot_nn(a_ref[...], w_ref[0:A_Q_W, :]) + _dot_nn(bb, w_ref[A_Q_W:A_Q_W + B_W, :])

    tok = lambda w: pl.BlockSpec((tm, w), lambda i: (i, 0))
    return pl.pallas_call(
        body, name=name, grid=(t // tm,),
        out_shape=[jax.ShapeDtypeStruct((t, d), F32), jax.ShapeDtypeStruct((t, B_W), BF16),
                   jax.ShapeDtypeStruct((t, B_W), F32)],
        in_specs=[tok(d), tok(A_Q_W)] + [tok(B_W)] * 6 + [pl.BlockSpec(wout.shape, lambda i: (0, 0))],
        out_specs=[tok(d), tok(B_W), tok(B_W)],
        compiler_params=_cp(("arbitrary",)),
    )(x, a_out, *outs, *lses, wout)


def _out_proj_bwd(dy, a_out, b_out, wout, name, tm):
    t, d = dy.shape
    n_t = t // tm

    def body(dy_ref, a_ref, b_ref, w_ref, da_ref, db_ref, gw_ref, gw_s):
        i = pl.program_id(0)
        dyb = dy_ref[...].astype(BF16)
        da_ref[...] = _dot_nt(dyb, w_ref[0:A_Q_W, :]).astype(BF16)
        db_ref[...] = _dot_nt(dyb, w_ref[A_Q_W:A_Q_W + B_W, :]).astype(BF16)
        ga = _dot_tn(a_ref[...], dyb)
        gb = _dot_tn(b_ref[...], dyb)

        @pl.when(i == 0)
        def _():
            gw_s[0:A_Q_W, :] = ga
            gw_s[A_Q_W:A_Q_W + B_W, :] = gb

        @pl.when(i > 0)
        def _():
            gw_s[0:A_Q_W, :] += ga
            gw_s[A_Q_W:A_Q_W + B_W, :] += gb

        @pl.when(i == n_t - 1)
        def _():
            gw_ref[...] = gw_s[...].astype(BF16)

    tok = lambda w: pl.BlockSpec((tm, w), lambda i: (i, 0))
    whole = pl.BlockSpec(wout.shape, lambda i: (0, 0))
    return pl.pallas_call(
        body, name=name, grid=(n_t,),
        out_shape=[jax.ShapeDtypeStruct((t, A_Q_W), BF16), jax.ShapeDtypeStruct((t, B_W), BF16),
                   jax.ShapeDtypeStruct(wout.shape, BF16)],
        in_specs=[tok(d), tok(A_Q_W), tok(B_W), whole], out_specs=[tok(A_Q_W), tok(B_W), whole],
        scratch_shapes=[pltpu.VMEM(wout.shape, F32)],
        compiler_params=_cp(("arbitrary",)),
    )(dy, a_out, b_out, wout)


SUB_ROWS = 64


def _sub_band(r0, hw, win, seg_lo, seg_len, t, rel, col):
    ks = pl.multiple_of(jnp.clip(r0 - hw, 0, t - win), SUB_ROWS)
    kpos = col + ks
    valid = (jnp.abs(rel + (ks - r0)) <= hw) & (kpos >= seg_lo) & (kpos < seg_lo + seg_len)
    return ks, valid


def _split_heads(v, low):
    zero = jnp.zeros_like(v)
    return jnp.concatenate([jnp.where(low, v, zero), jnp.where(low, zero, v)], axis=0)


def _kv_spec(kv, t):
    if kv.ndim == 3:
        return pl.BlockSpec((None, t, LANES), lambda p, i: (p // 2, 0, 0))
    return pl.BlockSpec((t, LANES), lambda p, i: (0, p))


def _attn_fwd(q, k, v, sink, name, hw, seg_len, tq, has_sink):
    t, width = q.shape
    sb = SUB_ROWS
    win = 2 * hw + LANES

    def body(sink_ref, q_ref, k_ref, v_ref, o_ref, lse_ref):
        p, i = pl.program_id(0), pl.program_id(1)
        q0 = i * tq
        seg_lo = (q0 // seg_len) * seg_len
        low = lax.broadcasted_iota(jnp.int32, (1, LANES), 1) < HEAD_DIM
        rel = lax.broadcasted_iota(jnp.int32, (sb, win), 1) - lax.broadcasted_iota(jnp.int32, (sb, win), 0)
        col = lax.broadcasted_iota(jnp.int32, (1, win), 1)
        for j in range(tq // sb):
            rows = pl.ds(j * sb, sb)
            ks, valid = _sub_band(q0 + j * sb, hw, win, seg_lo, seg_len, t, rel, col)
            kw = k_ref[pl.ds(ks, win), :]
            vw = v_ref[pl.ds(ks, win), :]
            s = _dot_nt(_split_heads(q_ref[rows, :], low), kw)
            es, inv, lses = [], [], []
            for a in range(2):
                sa = jnp.where(valid, s[a * sb:(a + 1) * sb], NEG)
                m = jnp.max(sa, axis=1, keepdims=True)
                if has_sink:
                    sk = sink_ref[2 * p + a]
                    m = jnp.maximum(m, sk)
                e = jnp.exp(sa - m)
                den = jnp.sum(e, axis=1, keepdims=True)
                if has_sink:
                    den = den + jnp.exp(sk - m)
                es.append(e.astype(BF16))
                inv.append(1.0 / den)
                lses.append(m + jnp.log(den))
            pv = _dot_nn(jnp.concatenate(es, axis=0), vw)
            o_ref[rows, :] = jnp.where(low, pv[0:sb] * inv[0], pv[sb:2 * sb] * inv[1]).astype(BF16)
            lse_ref[rows, :] = jnp.where(low, lses[0], lses[1])

    tile = pl.BlockSpec((tq, LANES), lambda p, i: (i, p))
    return pl.pallas_call(
        body, name=name, grid=(width // LANES, t // tq),
        out_shape=[jax.ShapeDtypeStruct((t, width), BF16), jax.ShapeDtypeStruct((t, width), F32)],
        in_specs=[pl.BlockSpec(memory_space=pltpu.SMEM), tile, _kv_spec(k, t), _kv_spec(v, t)],
        out_specs=[tile, tile],
        compiler_params=_cp(("arbitrary", "arbitrary")),
    )(sink, q, k, v)


def _attn_bwd(q, k, v, o, do, lse, sink, name, hw, seg_len, tq, has_sink):
    t, width = q.shape
    sb = SUB_ROWS
    win = 2 * hw + LANES
    n_q = t // tq
    shared_kv = k.ndim == 3

    def body(sink_ref, q_ref, k_ref, v_ref, o_ref, do_ref, lse_ref, dq_ref, dk_ref, dv_ref, ds_ref):
        p, i = pl.program_id(0), pl.program_id(1)
        fresh = (i == 0) & (p % 2 == 0) if shared_kv else i == 0

        @pl.when(fresh)
        def _():
            dk_ref[...] = jnp.zeros_like(dk_ref)
            dv_ref[...] = jnp.zeros_like(dv_ref)

        q0 = i * tq
        seg_lo = (q0 // seg_len) * seg_len
        low = lax.broadcasted_iota(jnp.int32, (1, LANES), 1) < HEAD_DIM
        rel = lax.broadcasted_iota(jnp.int32, (sb, win), 1) - lax.broadcasted_iota(jnp.int32, (sb, win), 0)
        col = lax.broadcasted_iota(jnp.int32, (1, win), 1)
        dsink = [jnp.zeros((1, 1), F32), jnp.zeros((1, 1), F32)]
        for j in range(tq // sb):
            rows = pl.ds(j * sb, sb)
            ks, valid = _sub_band(q0 + j * sb, hw, win, seg_lo, seg_len, t, rel, col)
            kw = k_ref[pl.ds(ks, win), :]
            vw = v_ref[pl.ds(ks, win), :]
            dov = do_ref[rows, :]
            prod = dov.astype(F32) * o_ref[rows, :].astype(F32)
            lse_t = lse_ref[rows, :]
            q2 = _split_heads(q_ref[rows, :], low)
            do2 = _split_heads(dov, low)
            s = _dot_nt(q2, kw)
            dpr = _dot_nt(do2, vw)
            prs, dss = [], []
            for a in range(2):
                mine = low if a == 0 else jnp.logical_not(low)
                lse_a = jnp.max(jnp.where(mine, lse_t, -jnp.inf), axis=1, keepdims=True)
                delta = jnp.sum(jnp.where(mine, prod, 0.0), axis=1, keepdims=True)
                pr = jnp.exp(jnp.where(valid, s[a * sb:(a + 1) * sb], NEG) - lse_a)
                prs.append(pr.astype(BF16))
                dss.append((pr * (dpr[a * sb:(a + 1) * sb] - delta)).astype(BF16))
                if has_sink:
                    dsink[a] = dsink[a] - jnp.sum(jnp.exp(sink_ref[2 * p + a] - lse_a) * delta, axis=0, keepdims=True)
            ds2 = jnp.concatenate(dss, axis=0)
            dv_ref[pl.ds(ks, win), :] += _dot_tn(jnp.concatenate(prs, axis=0), do2)
            dk_ref[pl.ds(ks, win), :] += _dot_tn(ds2, q2)
            dq2 = _dot_nn(ds2, kw)
            dq_ref[rows, :] = jnp.where(low, dq2[0:sb], dq2[sb:2 * sb])
        ds_ref[...] = jnp.broadcast_to(jnp.where(low, dsink[0], dsink[1]), ds_ref.shape)

    tile = pl.BlockSpec((tq, LANES), lambda p, i: (i, p))
    kv_shape = jax.ShapeDtypeStruct(k.shape, F32)
    return pl.pallas_call(
        body, name=name, grid=(width // LANES, n_q),
        out_shape=[jax.ShapeDtypeStruct((t, width), F32), kv_shape, kv_shape,
                   jax.ShapeDtypeStruct((width // LANES, n_q, 8, LANES), F32)],
        in_specs=[pl.BlockSpec(memory_space=pltpu.SMEM), tile, _kv_spec(k, t), _kv_spec(v, t), tile, tile, tile],
        out_specs=[tile, _kv_spec(k, t), _kv_spec(v, t), pl.BlockSpec((None, None, 8, LANES), lambda p, i: (p, i, 0, 0))],
        compiler_params=_cp(("arbitrary", "arbitrary")),
    )(sink, q, k, v, o, do, lse)


def _final_norm_loss(x, gn, target, name, tm):
    t, d = x.shape

    def body(x_ref, gn_ref, tg_ref, dx_ref, dgn_ref, sq_ref):
        i = pl.program_id(0)
        xv, gnv = x_ref[...], gn_ref[...]
        r = _rstd(xv)
        diff = (xv * r) * gnv - tg_ref[...]
        dxn, dgn = _norm_bwd(diff * (1.0 / d), xv, r, gnv)
        dx_ref[...] = dxn
        _accumulate(dgn_ref, dgn, i == 0)
        _accumulate(sq_ref, jnp.sum(diff * diff, axis=0, keepdims=True), i == 0)

    tok = pl.BlockSpec((tm, d), lambda i: (i, 0))
    row = pl.BlockSpec((1, d), lambda i: (0, 0))
    return pl.pallas_call(
        body, name=name, grid=(t // tm,),
        out_shape=[jax.ShapeDtypeStruct((t, d), F32), jax.ShapeDtypeStruct((1, d), F32), jax.ShapeDtypeStruct((1, d), F32)],
        in_specs=[tok, row, tok], out_specs=[tok, row, row],
        compiler_params=_cp(("arbitrary",)),
    )(x, gn, target)


def _adamw(w, g, m, v, name):
    def body(w_ref, g_ref, m_ref, v_ref, d_ref, nm_ref, nv_ref):
        gv = g_ref[...]
        nm = ADAM_B1 * m_ref[...] + (1.0 - ADAM_B1) * gv
        nv = ADAM_B2 * v_ref[...] + (1.0 - ADAM_B2) * (gv * gv)
        m_hat = nm / (1.0 - ADAM_B1 ** ADAM_STEP)
        v_hat = nv / (1.0 - ADAM_B2 ** ADAM_STEP)
        d_ref[...] = -ADAM_LR * (m_hat / (jnp.sqrt(v_hat) + ADAM_EPS) + ADAM_WD * w_ref[...])
        nm_ref[...] = nm
        nv_ref[...] = nv

    shape = jax.ShapeDtypeStruct(w.shape, F32)
    return pl.pallas_call(body, name=name, out_shape=[shape, shape, shape], compiler_params=_cp())(w, g, m, v)


def _sum_small(land, name):
    def body(l_ref, o_ref):
        acc = l_ref[0]
        for s in range(1, N_DEV):
            acc = acc + l_ref[s]
        o_ref[...] = acc

    return pl.pallas_call(body, name=name, out_shape=jax.ShapeDtypeStruct(land.shape[1:], F32), compiler_params=_cp())(land)


def _to_segments(a, dil):
    if dil == 1:
        return a
    t, w = a.shape
    return a.reshape(t // dil, dil, w).transpose(1, 0, 2).reshape(t, w)


def _from_segments(a, dil):
    if dil == 1:
        return a
    t, w = a.shape
    return a.reshape(dil, t // dil, w).transpose(1, 0, 2).reshape(t, w)


def _rope_lanes(positions):
    inv_freq = 1.0 / (ROPE_THETA ** (jnp.arange(0, HEAD_DIM, 2, dtype=F32) / HEAD_DIM))
    ang = positions.astype(F32)[:, None] * inv_freq
    cos, sin = jnp.cos(ang), jnp.sin(ang)
    return jnp.concatenate([cos, cos, cos, cos], axis=1), jnp.concatenate([-sin, sin, -sin, sin], axis=1)


def kernel(x, positions, norm_ffn1, w_gate1, w_up1, w_down1, norm_mix, w_in, a_sink, w_out, norm_ffn2, w_gate2, w_up2, w_down2, norm_final, loss_target, m_norm_ffn1, m_w_gate1, m_w_up1, m_w_down1, m_norm_mix, m_w_in, m_a_sink, m_w_out, m_norm_ffn2, m_w_gate2, m_w_up2, m_w_down2, m_norm_final, v_norm_ffn1, v_w_gate1, v_w_up1, v_w_down1, v_norm_mix, v_w_in, v_a_sink, v_w_out, v_norm_ffn2, v_w_gate2, v_w_up2, v_w_down2, v_norm_final):
    x = x[0]
    target = loss_target[0]
    t, d = x.shape
    tm = min(1024, t)
    tm_mix = min(512, t)
    tf = 256
    tq = min(512, t // 16)

    shards = [w_gate1[0].T, w_up1[0].T, w_down1[0], w_in[0].T, w_out[0], w_gate2[0].T, w_up2[0].T, w_down2[0]]
    rows_list = [s.shape[0] for s in shards]
    packed = jnp.concatenate([s.astype(BF16) for s in shards], axis=0)
    wg1, wu1, wd1, win, wout, wg2, wu2, wd2 = _all_gather_rows(packed, rows_list, "gather_weights")

    cos, sin_signed = _rope_lanes(positions[0])
    sink = a_sink[0]
    no_sink = jnp.zeros_like(sink)

    x1, h1, gp1, up1 = _ffn_fwd(x, norm_ffn1, wg1, wu1, wd1, "ffn1_fwd", tm, tf)
    h_mix, aq, akx, avx, bq, bk, bv = _in_proj_fwd(x1, norm_mix, win, cos, sin_signed, "in_proj_fwd", tm_mix)
    a_out, a_lse = _attn_fwd(aq, akx, avx, sink, "attn_a_fwd", A_HALF_WINDOW, t, tq, True)
    b_seg, b_outs, b_lses = [], [], []
    for window, dil in B_PATTERNS:
        qs, ks, vs = (_to_segments(a, dil) for a in (bq, bk, bv))
        o_seg, lse_seg = _attn_fwd(qs, ks, vs, no_sink, f"attn_b{dil}_fwd", window // (2 * dil), t // dil, tq, False)
        b_seg.append((qs, ks, vs))
        b_outs.append(_from_segments(o_seg, dil))
        b_lses.append(_from_segments(lse_seg, dil))
    x2, b_out, b_lse = _merge_out_proj_fwd(x1, a_out, b_outs, b_lses, wout, "out_proj_fwd", tm_mix)
    x3, h2, gp2, up2 = _ffn_fwd(x2, norm_ffn2, wg2, wu2, wd2, "ffn2_fwd", tm, tf)

    gfinal = norm_final.reshape(1, d)
    dx3, dg_final, sq = _final_norm_loss(x3, gfinal, target, "final_norm_loss", tm)

    dx2, dg_ffn2, dgt2, dut2, hid2, dout2 = _ffn_bwd_act(dx3, x2, norm_ffn2, gp2, up2, wg2, wu2, wd2, "ffn2_bwd_act", tm, tf)
    gwg2, gwu2, gwd2 = _ffn_bwd_weights(dgt2, dut2, hid2, h2, dout2, "ffn2_bwd_weights", tm, tf)

    da, db, gwout = _out_proj_bwd(dx2, a_out, b_out, wout, "out_proj_bwd", tm_mix)
    daq, dakx, davx, dsink_parts = _attn_bwd(aq, akx, avx, a_out, da, a_lse, sink, "attn_a_bwd", A_HALF_WINDOW, t, tq, True)
    dbq = dbk = dbv = None
    for (window, dil), (qs, ks, vs) in zip(B_PATTERNS, b_seg):
        os_, dos, ls = (_to_segments(a, dil) for a in (b_out, db, b_lse))
        dq_s, dk_s, dv_s, _ = _attn_bwd(qs, ks, vs, os_, dos, ls, no_sink, f"attn_b{dil}_bwd", window // (2 * dil), t // dil, tq, False)
        parts = [_from_segments(a, dil) for a in (dq_s, dk_s, dv_s)]
        dbq, dbk, dbv = parts if dbq is None else (dbq + parts[0], dbk + parts[1], dbv + parts[2])
    dx1, dg_mix, gwin = _in_proj_bwd(dx2, x1, norm_mix, win, h_mix, cos, sin_signed, daq, dakx, davx, dbq, dbk, dbv, "in_proj_bwd", tm_mix)

    grad_x, dg_ffn1, dgt1, dut1, hid1, dout1 = _ffn_bwd_act(dx1, x, norm_ffn1, gp1, up1, wg1, wu1, wd1, "ffn1_bwd_act", tm, tf)
    gwg1, gwu1, gwd1 = _ffn_bwd_weights(dgt1, dut1, hid1, h1, dout1, "ffn1_bwd_weights", tm, tf)

    dsink_pairs = jnp.sum(dsink_parts[:, :, 0, :], axis=1)
    dsink = jnp.stack([dsink_pairs[:, 0], dsink_pairs[:, HEAD_DIM]], axis=1).reshape(1, -1)
    small = jnp.concatenate([dg_ffn1, dg_mix, dg_ffn2, dg_final, jnp.pad(dsink, ((0, 0), (0, d - dsink.shape[1]))),
                             sq, jnp.zeros((2, d), F32)], axis=0)
    land_ffn, land_in, land_out, land_small = _exchange_partials(
        [[gwg1, gwu1, gwd1, gwg2, gwu2, gwd2], [gwin], [gwout], [jnp.tile(small, (N_DEV, 1))]], "scatter_gradients")
    red_ffn = _sum_slots(land_ffn, 6, "sum_ffn_grads")
    red_in = _sum_slots(land_in, 1, "sum_in_grads")
    red_out = _sum_slots(land_out, 1, "sum_out_grads")
    red_small = _sum_small(land_small, "sum_small_grads")

    rf = rows_list[0]
    ffn_block = lambda j: red_ffn[j * rf:(j + 1) * rf]
    n_sink = a_sink.shape[1]
    grads = {
        "norm_ffn1": red_small[0:1], "w_gate1": ffn_block(0).T[None], "w_up1": ffn_block(1).T[None], "w_down1": ffn_block(2)[None],
        "norm_mix": red_small[1:2], "w_in": red_in.T[None], "a_sink": red_small[4:5, :n_sink], "w_out": red_out[None],
        "norm_ffn2": red_small[2:3], "w_gate2": ffn_block(3).T[None], "w_up2": ffn_block(4).T[None], "w_down2": ffn_block(5)[None],
        "norm_final": red_small[3],
    }
    params = {
        "norm_ffn1": (norm_ffn1, m_norm_ffn1, v_norm_ffn1), "w_gate1": (w_gate1, m_w_gate1, v_w_gate1),
        "w_up1": (w_up1, m_w_up1, v_w_up1), "w_down1": (w_down1, m_w_down1, v_w_down1),
        "norm_mix": (norm_mix, m_norm_mix, v_norm_mix), "w_in": (w_in, m_w_in, v_w_in),
        "a_sink": (a_sink, m_a_sink, v_a_sink), "w_out": (w_out, m_w_out, v_w_out),
        "norm_ffn2": (norm_ffn2, m_norm_ffn2, v_norm_ffn2), "w_gate2": (w_gate2, m_w_gate2, v_w_gate2),
        "w_up2": (w_up2, m_w_up2, v_w_up2), "w_down2": (w_down2, m_w_down2, v_w_down2),
        "norm_final": (norm_final, m_norm_final, v_norm_final),
    }
    loss = 0.5 * jnp.sum(red_small[5]) / d
    deltas, new_ms, new_vs = [], [], []
    for name, (w, m, v) in params.items():
        as_block = (lambda a: a.reshape(1, -1)) if w.ndim == 1 else (lambda a: a)
        dl, nm, nv = _adamw(as_block(w), as_block(grads[name]), as_block(m), as_block(v), f"adamw_{name}")
        deltas.append(dl.reshape(w.shape))
        new_ms.append(nm.reshape(w.shape))
        new_vs.append(nv.reshape(w.shape))
    grad_list = [grads[name] for name in params]
    return (loss, grad_x[None], *grad_list, *deltas, *new_ms, *new_vs)
```

```python
import functools
import itertools

import numpy as np
import jax
import jax.numpy as jnp
from jax import lax
from jax.experimental import pallas as pl
from jax.experimental.pallas import tpu as pltpu

F32 = jnp.float32
BF16 = jnp.bfloat16

N_DEV = 8
HEAD_DIM = 64
LANES = 128
A_Q_W, A_KV_W, B_W = 512, 128, 512
A_HALF_WINDOW = 128
B_PATTERNS = ((128, 1), (512, 4), (2048, 16))
ROPE_THETA = 10000.0
NORM_EPS = 1e-6
FFN_RES_WEIGHT = 0.5
QK_SCALE = HEAD_DIM ** -0.5
NEG = -1e30

ADAM_LR = 0.001
ADAM_B1 = 0.9
ADAM_B2 = 0.999
ADAM_EPS = 1e-08
ADAM_WD = 0.01
ADAM_STEP = 10

MESH_T = pl.DeviceIdType.MESH
VMEM_LIMIT = 56 * 1024 * 1024


def _cp(sem=None, vmem=VMEM_LIMIT):
    return pltpu.CompilerParams(dimension_semantics=sem, vmem_limit_bytes=vmem)


def _dot_nn(a, b):
    return jnp.dot(a, b, preferred_element_type=F32)


def _dot_nt(a, b):
    return lax.dot_general(a, b, (((1,), (1,)), ((), ())), preferred_element_type=F32)


def _dot_tn(a, b):
    return lax.dot_general(a, b, (((0,), (0,)), ((), ())), preferred_element_type=F32)


def _rstd(xv):
    return lax.rsqrt(jnp.mean(xv * xv, axis=-1, keepdims=True) + NORM_EPS)


def _norm_bwd(dh, xv, r, gn):
    gy = dh * gn
    c = jnp.sum(gy * xv, axis=-1, keepdims=True) * (1.0 / xv.shape[-1])
    dx = r * gy - xv * (r * r * r * c)
    dgn = jnp.sum(dh * (xv * r), axis=0, keepdims=True)
    return dx, dgn


def _accumulate(ref, val, first):
    @pl.when(first)
    def _():
        ref[...] = val

    @pl.when(jnp.logical_not(first))
    def _():
        ref[...] += val


def _mesh_pos():
    return lax.axis_index("x"), lax.axis_index("y"), lax.axis_index("c")


def _dev_index(d):
    return 4 * d[0] + 2 * d[1] + d[2]


class _Comm:
    def __init__(self, inputs, out_shape, scratch, phases):
        self.inputs, self.out_shape, self.scratch, self.phases = inputs, out_shape, scratch, phases

    def specs(self):
        any_spec = pl.BlockSpec(memory_space=pl.ANY)
        return [any_spec] * len(self.inputs), [any_spec] * len(self.out_shape)


def _run_alone(comm, name):
    n_in, n_out = len(comm.inputs), len(comm.out_shape)

    def body(*refs):
        for phase in comm.phases:
            phase(refs[:n_in], refs[n_in:n_in + n_out], refs[n_in + n_out:])

    in_specs, out_specs = comm.specs()
    return pl.pallas_call(body, name=name, out_shape=comm.out_shape, in_specs=in_specs, out_specs=out_specs,
                          scratch_shapes=comm.scratch)(*comm.inputs)


def _run_hosted(comm, at, step, ins, outs, scr):
    for phase, when in zip(comm.phases, at):
        @pl.when(step == when)
        def _(phase=phase):
            phase(ins, outs, scr)


def _split(refs, *counts):
    parts, o = [], 0
    for n in counts:
        parts.append(refs[o:o + n])
        o += n
    return parts + [refs[o:]]


def _gather_plan(packed, rows_list):
    n_w = len(rows_list)
    offs = [int(o) for o in np.cumsum([0] + list(rows_list[:-1]))]
    d = packed.shape[1]

    def tools(ins, outs, scr):
        p_ref = ins[0]
        send_sems, recv_sems, local_sem = scr
        x, y, c = _mesh_pos()
        me, sibling = (x, y, c), (x, y, 1 - c)
        chips = [(1 - x, y), (x, 1 - y), (1 - x, 1 - y)]

        def rows(w, dev):
            start = pl.multiple_of(_dev_index(dev) * rows_list[w], 16)
            return outs[w].at[pl.ds(start, rows_list[w]), :]

        def mine(w):
            return p_ref.at[pl.ds(offs[w], rows_list[w]), :]

        def copy(k, w, block, to, own):
            return pltpu.make_async_remote_copy(
                src_ref=mine(w) if own else rows(w, block), dst_ref=rows(w, block),
                send_sem=send_sems.at[k], recv_sem=recv_sems.at[k], device_id=to, device_id_type=MESH_T)

        def all_blocks(k):
            return pltpu.make_async_remote_copy(
                src_ref=p_ref, dst_ref=p_ref, send_sem=send_sems.at[k], recv_sem=recv_sems.at[k],
                device_id=me, device_id_type=MESH_T)

        return p_ref, local_sem, me, sibling, chips, c, rows, mine, copy, all_blocks

    def start(ins, outs, scr):
        _, local_sem, me, sibling, chips, c, rows, mine, copy, _ = tools(ins, outs, scr)
        for w in range(n_w):
            pltpu.make_async_copy(mine(w), rows(w, me), local_sem).start()
        for w in range(n_w):
            copy(0, w, me, sibling, True).start()
        for j, chip in enumerate(chips):
            for w in range(n_w):
                copy(1 + j, w, me, (*chip, c), True).start()

    def relay(ins, outs, scr):
        _, _, _, sibling, chips, c, _, _, copy, all_blocks = tools(ins, outs, scr)
        for j, chip in enumerate(chips):
            all_blocks(1 + j).wait_recv()
            for w in range(n_w):
                copy(4 + j, w, (*chip, c), sibling, False).start()

    def finish(ins, outs, scr):
        p_ref, local_sem, _, _, _, _, _, _, _, all_blocks = tools(ins, outs, scr)
        all_blocks(0).wait_recv()
        for j in range(3):
            all_blocks(4 + j).wait_recv()
        for k in range(7):
            all_blocks(k).wait_send()
        pltpu.make_async_copy(p_ref, p_ref, local_sem).wait()

    return _Comm(
        [packed], [jax.ShapeDtypeStruct((N_DEV * r, d), packed.dtype) for r in rows_list],
        [pltpu.SemaphoreType.DMA((7,)), pltpu.SemaphoreType.DMA((7,)), pltpu.SemaphoreType.DMA], [start, relay, finish])


def _exchange_plan(groups):
    flat = [a for g in groups for a in g]
    n_g = len(groups)
    sizes = [len(g) for g in groups]
    rows = [g[0].shape[0] // N_DEV for g in groups]
    first = [int(o) for o in np.cumsum([0] + sizes[:-1])]

    def start(srcs, lands, scr):
        send_sems, recv_sems, local_sems = scr
        x, y, c = _mesh_pos()
        me = (x, y, c)
        me_idx = _dev_index(me)

        def block(g, i, dev):
            start_row = pl.multiple_of(_dev_index(dev) * rows[g], 8)
            return srcs[first[g] + i].at[pl.ds(start_row, rows[g]), :]

        def slot(g, i):
            return lands[g].at[me_idx, pl.ds(i * rows[g], rows[g]), :]

        for g in range(n_g):
            for i in range(sizes[g]):
                pltpu.make_async_copy(block(g, i, me), slot(g, i), local_sems.at[g]).start()
        flips = [f for f in itertools.product((0, 1), repeat=3) if any(f)]
        for k, (fx, fy, fc) in enumerate(flips):
            peer = (1 - x if fx else x, 1 - y if fy else y, 1 - c if fc else c)
            for g in range(n_g):
                for i in range(sizes[g]):
                    pltpu.make_async_remote_copy(
                        src_ref=block(g, i, peer), dst_ref=slot(g, i), send_sem=send_sems.at[g, k],
                        recv_sem=recv_sems.at[g, k], device_id=peer, device_id_type=MESH_T).start()

    def finish(srcs, lands, scr):
        send_sems, recv_sems, local_sems = scr
        me = _mesh_pos()
        for k in range(7):
            for g in range(n_g):
                pltpu.make_async_remote_copy(
                    src_ref=lands[g].at[0], dst_ref=lands[g].at[0], send_sem=send_sems.at[g, k],
                    recv_sem=recv_sems.at[g, k], device_id=me, device_id_type=MESH_T).wait()
        for g in range(n_g):
            pltpu.make_async_copy(lands[g].at[0], lands[g].at[0], local_sems.at[g]).wait()

    return _Comm(
        flat, [jax.ShapeDtypeStruct((N_DEV, sizes[g] * rows[g], groups[g][0].shape[1]), groups[g][0].dtype) for g in range(n_g)],
        [pltpu.SemaphoreType.DMA((n_g, 7)), pltpu.SemaphoreType.DMA((n_g, 7)), pltpu.SemaphoreType.DMA((n_g,))],
        [start, finish])


def _sum_slots(land, n_blocks, name):
    _, total, d = land.shape
    rows = total // n_blocks

    def body(l_ref, o_ref):
        s = pl.program_id(1)
        _accumulate(o_ref, l_ref[...].astype(F32), s == 0)

    return pl.pallas_call(
        body, name=name, grid=(n_blocks, N_DEV),
        out_shape=jax.ShapeDtypeStruct((total, d), F32),
        in_specs=[pl.BlockSpec((None, rows, d), lambda w, s: (s, w, 0))],
        out_specs=pl.BlockSpec((rows, d), lambda w, s: (w, 0)),
        compiler_params=_cp(("arbitrary", "arbitrary")),
    )(land)


def _ffn_fwd(x, gn, wg_t, wu_t, wd, name, tm, tf, comm=None, comm_at=None):
    t, d = x.shape
    f_all = wg_t.shape[0]
    n_f = f_all // tf
    n_ci, n_co = (len(comm.inputs), len(comm.out_shape)) if comm else (0, 0)

    def body(*refs):
        (x_ref, gn_ref, wg_ref, wu_ref, wd_ref), c_in, (y_ref, h_ref, gp_ref, up_ref), c_out, (h_s, acc_s), c_scr = _split(
            refs, 5, n_ci, 4, n_co, 2)
        f = pl.program_id(1)
        if comm:
            _run_hosted(comm, comm_at, pl.program_id(0) * n_f + f, c_in, c_out, c_scr)

        @pl.when(f == 0)
        def _():
            xv = x_ref[...]
            h = ((xv * _rstd(xv)) * gn_ref[...]).astype(BF16)
            h_s[...] = h
            h_ref[...] = h
            acc_s[...] = jnp.zeros_like(acc_s)

        h = h_s[...]
        g = _dot_nt(h, wg_ref[...])
        u = _dot_nt(h, wu_ref[...])
        gp_ref[...] = g.astype(BF16)
        up_ref[...] = u.astype(BF16)
        hid = (g * jax.nn.sigmoid(g)) * u
        acc_s[...] += _dot_nn(hid.astype(BF16), wd_ref[...])

        @pl.when(f == n_f - 1)
        def _():
            y_ref[...] = x_ref[...] + FFN_RES_WEIGHT * acc_s[...]

    tok = pl.BlockSpec((tm, d), lambda i, f: (i, 0))
    wblk = pl.BlockSpec((tf, d), lambda i, f: (f, 0))
    act = pl.BlockSpec((tm, tf), lambda i, f: (i, f))
    c_in_specs, c_out_specs = comm.specs() if comm else ([], [])
    res = pl.pallas_call(
        body, name=name, grid=(t // tm, n_f),
        out_shape=[jax.ShapeDtypeStruct((t, d), F32), jax.ShapeDtypeStruct((t, d), BF16),
                   jax.ShapeDtypeStruct((t, f_all), BF16), jax.ShapeDtypeStruct((t, f_all), BF16)]
        + (comm.out_shape if comm else []),
        in_specs=[tok, pl.BlockSpec((1, d), lambda i, f: (0, 0)), wblk, wblk, wblk] + c_in_specs,
        out_specs=[tok, tok, act, act] + c_out_specs,
        scratch_shapes=[pltpu.VMEM((tm, d), BF16), pltpu.VMEM((tm, d), F32)] + (comm.scratch if comm else []),
        compiler_params=_cp(("arbitrary", "arbitrary")),
    )(x, gn, wg_t, wu_t, wd, *(comm.inputs if comm else []))
    return res[:4], res[4:]


def _ffn_bwd_act(dy, x, gn, gp, up, wg_t, wu_t, wd, name, tm, tf, comm=None, comm_at=None):
    t, d = x.shape
    f_all = wg_t.shape[0]
    n_f = f_all // tf
    n_ci, n_co = (len(comm.inputs), len(comm.out_shape)) if comm else (0, 0)

    def body(*refs):
        ((dy_ref, x_ref, gn_ref, gp_ref, up_ref, wg_ref, wu_ref, wd_ref), c_in,
         (dx_ref, dgn_ref, dg_ref, du_ref, hid_ref, dout_ref), c_out, (dout_s, dh_s), c_scr) = _split(refs, 8, n_ci, 6, n_co, 2)
        i, f = pl.program_id(0), pl.program_id(1)
        if comm:
            _run_hosted(comm, comm_at, i * n_f + f, c_in, c_out, c_scr)

        @pl.when(f == 0)
        def _():
            dout = (FFN_RES_WEIGHT * dy_ref[...]).astype(BF16)
            dout_s[...] = dout
            dout_ref[...] = dout
            dh_s[...] = jnp.zeros_like(dh_s)

        dhid = _dot_nt(dout_s[...], wd_ref[...])
        g = gp_ref[...].astype(F32)
        u = up_ref[...].astype(F32)
        sg = jax.nn.sigmoid(g)
        silu = g * sg
        dg = ((dhid * u) * (sg * (1.0 + g * (1.0 - sg)))).astype(BF16)
        du = (dhid * silu).astype(BF16)
        dg_ref[...] = dg
        du_ref[...] = du
        hid_ref[...] = (silu * u).astype(BF16)
        dh_s[...] += _dot_nn(dg, wg_ref[...]) + _dot_nn(du, wu_ref[...])

        @pl.when(f == n_f - 1)
        def _():
            xv = x_ref[...]
            dxn, dgn = _norm_bwd(dh_s[...], xv, _rstd(xv), gn_ref[...])
            dx_ref[...] = dy_ref[...] + dxn
            _accumulate(dgn_ref, dgn, i == 0)

    tok = pl.BlockSpec((tm, d), lambda i, f: (i, 0))
    row = pl.BlockSpec((1, d), lambda i, f: (0, 0))
    wblk = pl.BlockSpec((tf, d), lambda i, f: (f, 0))
    act = pl.BlockSpec((tm, tf), lambda i, f: (i, f))
    act_shape = jax.ShapeDtypeStruct((t, f_all), BF16)
    c_in_specs, c_out_specs = comm.specs() if comm else ([], [])
    res = pl.pallas_call(
        body, name=name, grid=(t // tm, n_f),
        out_shape=[jax.ShapeDtypeStruct((t, d), F32), jax.ShapeDtypeStruct((1, d), F32),
                   act_shape, act_shape, act_shape, jax.ShapeDtypeStruct((t, d), BF16)] + (comm.out_shape if comm else []),
        in_specs=[tok, tok, row, act, act, wblk, wblk, wblk] + c_in_specs,
        out_specs=[tok, row, act, act, act, tok] + c_out_specs,
        scratch_shapes=[pltpu.VMEM((tm, d), BF16), pltpu.VMEM((tm, d), F32)] + (comm.scratch if comm else []),
        compiler_params=_cp(("arbitrary", "arbitrary")),
    )(dy, x, gn, gp, up, wg_t, wu_t, wd, *(comm.inputs if comm else []))
    return res[:6], res[6:]


def _ffn_bwd_weights(dg, du, hid, h, dout, name, tt, tf):
    t, f_all = dg.shape
    d = h.shape[1]
    n_t = t // tt

    def body(dg_ref, du_ref, hid_ref, h_ref, dout_ref, gg_ref, gu_ref, gd_ref, a_g, a_u, a_d):
        s = pl.program_id(1)

        @pl.when(s == 0)
        def _():
            a_g[...] = jnp.zeros_like(a_g)
            a_u[...] = jnp.zeros_like(a_u)
            a_d[...] = jnp.zeros_like(a_d)

        hv = h_ref[...]
        a_g[...] += _dot_tn(dg_ref[...], hv)
        a_u[...] += _dot_tn(du_ref[...], hv)
        a_d[...] += _dot_tn(hid_ref[...], dout_ref[...])

        @pl.when(s == n_t - 1)
        def _():
            gg_ref[...] = a_g[...].astype(BF16)
            gu_ref[...] = a_u[...].astype(BF16)
            gd_ref[...] = a_d[...].astype(BF16)

    act = pl.BlockSpec((tt, tf), lambda f, s: (s, f))
    tok = pl.BlockSpec((tt, d), lambda f, s: (s, 0))
    wblk = pl.BlockSpec((tf, d), lambda f, s: (f, 0))
    w_shape = jax.ShapeDtypeStruct((f_all, d), BF16)
    return pl.pallas_call(
        body, name=name, grid=(f_all // tf, n_t),
        out_shape=[w_shape, w_shape, w_shape],
        in_specs=[act, act, act, tok, tok], out_specs=[wblk, wblk, wblk],
        scratch_shapes=[pltpu.VMEM((tf, d), F32)] * 3,
        compiler_params=_cp(("arbitrary", "arbitrary")),
    )(dg, du, hid, h, dout)


def _swap_halves(t):
    w = t.shape[-1]
    lane = lax.broadcasted_iota(jnp.int32, (1, w), 1)
    return jnp.where((lane % HEAD_DIM) < HEAD_DIM // 2, pltpu.roll(t, w - HEAD_DIM // 2, 1), pltpu.roll(t, HEAD_DIM // 2, 1))


def _rope(t, cos, sin_signed):
    reps = t.shape[-1] // LANES
    return t * jnp.tile(cos, (1, reps)) + _swap_halves(t) * jnp.tile(sin_signed, (1, reps))


def _rope_bwd(dt, cos, sin_signed):
    reps = dt.shape[-1] // LANES
    return dt * jnp.tile(cos, (1, reps)) + _swap_halves(dt * jnp.tile(sin_signed, (1, reps)))


def _in_proj_fwd(x, gn, win_t, cos, sin_signed, name, tm):
    t, d = x.shape
    in_w = win_t.shape[0]

    def body(x_ref, gn_ref, w_ref, cos_ref, sin_ref, h_ref, aq_ref, akx_ref, avx_ref, bq_ref, bk_ref, bv_ref):
        xv = x_ref[...]
        h = ((xv * _rstd(xv)) * gn_ref[...]).astype(BF16)
        h_ref[...] = h
        p = _dot_nt(h, w_ref[...])
        cs, sn = cos_ref[...], sin_ref[...]
        o = 0
        aq_ref[...] = (_rope(p[:, o:o + A_Q_W], cs, sn) * QK_SCALE).astype(BF16)
        o += A_Q_W
        ak = _rope(p[:, o:o + A_KV_W], cs, sn)
        o += A_KV_W
        av = p[:, o:o + A_KV_W]
        o += A_KV_W
        low = lax.broadcasted_iota(jnp.int32, (1, LANES), 1) < HEAD_DIM
        for src, dst in ((ak, akx_ref), (av, avx_ref)):
            other = pltpu.roll(src, HEAD_DIM, 1)
            dst[0] = jnp.where(low, src, other).astype(BF16)
            dst[1] = jnp.where(low, other, src).astype(BF16)
        bq_ref[...] = (_rope(p[:, o:o + B_W], cs, sn) * QK_SCALE).astype(BF16)
        o += B_W
        bk_ref[...] = _rope(p[:, o:o + B_W], cs, sn).astype(BF16)
        o += B_W
        bv_ref[...] = p[:, o:o + B_W].astype(BF16)

    tok = lambda w: pl.BlockSpec((tm, w), lambda i: (i, 0))
    kvx = pl.BlockSpec((2, tm, LANES), lambda i: (0, i, 0))
    sd = lambda *s: jax.ShapeDtypeStruct(s, BF16)
    return pl.pallas_call(
        body, name=name, grid=(t // tm,),
        out_shape=[sd(t, d), sd(t, A_Q_W), sd(2, t, LANES), sd(2, t, LANES), sd(t, B_W), sd(t, B_W), sd(t, B_W)],
        in_specs=[tok(d), pl.BlockSpec((1, d), lambda i: (0, 0)), pl.BlockSpec((in_w, d), lambda i: (0, 0)),
                  tok(LANES), tok(LANES)],
        out_specs=[tok(d), tok(A_Q_W), kvx, kvx, tok(B_W), tok(B_W), tok(B_W)],
        compiler_params=_cp(("arbitrary",)),
    )(x, gn, win_t, cos, sin_signed)


def _in_proj_bwd(dres, x, gn, win_t, h, cos, sin_signed, daq, dakx, davx, dbq, dbk, dbv, name, tm):
    t, d = x.shape
    in_w = win_t.shape[0]
    n_t = t // tm

    def body(dres_ref, x_ref, gn_ref, w_ref, h_ref, cos_ref, sin_ref, daq_ref, dakx_ref, davx_ref, dbq_ref, dbk_ref,
             dbv_ref, dx_ref, dgn_ref, gw_ref, dp_s, gw_s):
        i = pl.program_id(0)
        cs, sn = cos_ref[...], sin_ref[...]
        low = lax.broadcasted_iota(jnp.int32, (1, LANES), 1) < HEAD_DIM

        def fold(ref):
            a, b = ref[0], ref[1]
            return jnp.where(low, a + pltpu.roll(a, HEAD_DIM, 1), b + pltpu.roll(b, HEAD_DIM, 1))

        o = 0
        dp_s[:, o:o + A_Q_W] = _rope_bwd(daq_ref[...] * QK_SCALE, cs, sn).astype(BF16)
        o += A_Q_W
        dp_s[:, o:o + A_KV_W] = _rope_bwd(fold(dakx_ref), cs, sn).astype(BF16)
        o += A_KV_W
        dp_s[:, o:o + A_KV_W] = fold(davx_ref).astype(BF16)
        o += A_KV_W
        dp_s[:, o:o + B_W] = _rope_bwd(dbq_ref[...] * QK_SCALE, cs, sn).astype(BF16)
        o += B_W
        dp_s[:, o:o + B_W] = _rope_bwd(dbk_ref[...], cs, sn).astype(BF16)
        o += B_W
        dp_s[:, o:o + B_W] = dbv_ref[...].astype(BF16)
        dh = _dot_nn(dp_s[...], w_ref[...])
        hv = h_ref[...]
        for c0 in range(0, in_w, 2 * LANES):
            _accumulate(gw_s.at[pl.ds(c0, 2 * LANES), :], _dot_tn(dp_s[:, c0:c0 + 2 * LANES], hv), i == 0)
        xv = x_ref[...]
        dxn, dgn = _norm_bwd(dh, xv, _rstd(xv), gn_ref[...])
        dx_ref[...] = dres_ref[...] + dxn
        _accumulate(dgn_ref, dgn, i == 0)

        @pl.when(i == n_t - 1)
        def _():
            gw_ref[...] = gw_s[...].astype(BF16)

    tok = lambda w: pl.BlockSpec((tm, w), lambda i: (i, 0))
    row = pl.BlockSpec((1, d), lambda i: (0, 0))
    whole = pl.BlockSpec((in_w, d), lambda i: (0, 0))
    kvx = pl.BlockSpec((2, tm, LANES), lambda i: (0, i, 0))
    return pl.pallas_call(
        body, name=name, grid=(n_t,),
        out_shape=[jax.ShapeDtypeStruct((t, d), F32), jax.ShapeDtypeStruct((1, d), F32),
                   jax.ShapeDtypeStruct((in_w, d), BF16)],
        in_specs=[tok(d), tok(d), row, whole, tok(d), tok(LANES), tok(LANES), tok(A_Q_W), kvx, kvx,
                  tok(B_W), tok(B_W), tok(B_W)],
        out_specs=[tok(d), row, whole],
        scratch_shapes=[pltpu.VMEM((tm, in_w), BF16), pltpu.VMEM((in_w, d), F32)],
        compiler_params=_cp(("arbitrary",)),
    )(dres, x, gn, win_t, h, cos, sin_signed, daq, dakx, davx, dbq, dbk, dbv)


def _merge_out_proj_fwd(x, a_out, outs, lses, wout, name, tm):
    t, d = x.shape

    def body(x_ref, a_ref, o0, o1, o2, l0, l1, l2, w_ref, y_ref, b_ref, lt_ref):
        ls = [l0[...], l1[...], l2[...]]
        mx = jnp.maximum(jnp.maximum(ls[0], ls[1]), ls[2])
        es = [jnp.exp(l - mx) for l in ls]
        den = es[0] + es[1] + es[2]
        b = (es[0] * o0[...].astype(F32) + es[1] * o1[...].astype(F32) + es[2] * o2[...].astype(F32)) / den
        bb = b.astype(BF16)
        b_ref[...] = bb
        lt_ref[...] = mx + jnp.log(den)
        y_ref[...] = x_ref[...] + _dot_nn(a_ref[...], w_ref[0:A_Q_W, :]) + _dot_nn(bb, w_ref[A_Q_W:A_Q_W + B_W, :])

    tok = lambda w: pl.BlockSpec((tm, w), lambda i: (i, 0))
    return pl.pallas_call(
        body, name=name, grid=(t // tm,),
        out_shape=[jax.ShapeDtypeStruct((t, d), F32), jax.ShapeDtypeStruct((t, B_W), BF16),
                   jax.ShapeDtypeStruct((t, B_W), F32)],
        in_specs=[tok(d), tok(A_Q_W)] + [tok(B_W)] * 6 + [pl.BlockSpec(wout.shape, lambda i: (0, 0))],
        out_specs=[tok(d), tok(B_W), tok(B_W)],
        compiler_params=_cp(("arbitrary",)),
    )(x, a_out, *outs, *lses, wout)


def _out_proj_bwd(dy, a_out, b_out, wout, name, tm):
    t, d = dy.shape
    n_t = t // tm

    def body(dy_ref, a_ref, b_ref, w_ref, da_ref, db_ref, gw_ref, gw_s):
        i = pl.program_id(0)
        dyb = dy_ref[...].astype(BF16)
        da_ref[...] = _dot_nt(dyb, w_ref[0:A_Q_W, :]).astype(BF16)
        db_ref[...] = _dot_nt(dyb, w_ref[A_Q_W:A_Q_W + B_W, :]).astype(BF16)
        ga = _dot_tn(a_ref[...], dyb)
        gb = _dot_tn(b_ref[...], dyb)

        @pl.when(i == 0)
        def _():
            gw_s[0:A_Q_W, :] = ga
            gw_s[A_Q_W:A_Q_W + B_W, :] = gb

        @pl.when(i > 0)
        def _():
            gw_s[0:A_Q_W, :] += ga
            gw_s[A_Q_W:A_Q_W + B_W, :] += gb

        @pl.when(i == n_t - 1)
        def _():
            gw_ref[...] = gw_s[...].astype(BF16)

    tok = lambda w: pl.BlockSpec((tm, w), lambda i: (i, 0))
    whole = pl.BlockSpec(wout.shape, lambda i: (0, 0))
    return pl.pallas_call(
        body, name=name, grid=(n_t,),
        out_shape=[jax.ShapeDtypeStruct((t, A_Q_W), BF16), jax.ShapeDtypeStruct((t, B_W), BF16),
                   jax.ShapeDtypeStruct(wout.shape, BF16)],
        in_specs=[tok(d), tok(A_Q_W), tok(B_W), whole], out_specs=[tok(A_Q_W), tok(B_W), whole],
        scratch_shapes=[pltpu.VMEM(wout.shape, F32)],
        compiler_params=_cp(("arbitrary",)),
    )(dy, a_out, b_out, wout)


SUB_ROWS = 64


def _sub_band(r0, hw, win, seg_lo, seg_len, t, rel, col):
    ks = pl.multiple_of(jnp.clip(r0 - hw, 0, t - win), SUB_ROWS)
    kpos = col + ks
    valid = (jnp.abs(rel + (ks - r0)) <= hw) & (kpos >= seg_lo) & (kpos < seg_lo + seg_len)
    return ks, valid


def _split_heads(v, low):
    zero = jnp.zeros_like(v)
    return jnp.concatenate([jnp.where(low, v, zero), jnp.where(low, zero, v)], axis=0)


def _kv_spec(kv, t):
    if kv.ndim == 3:
        return pl.BlockSpec((None, t, LANES), lambda p, i: (p // 2, 0, 0))
    return pl.BlockSpec((t, LANES), lambda p, i: (0, p))


def _attn_fwd(q, k, v, sink, name, hw, seg_len, tq, has_sink):
    t, width = q.shape
    sb = SUB_ROWS
    win = 2 * hw + LANES

    def body(sink_ref, q_ref, k_ref, v_ref, o_ref, lse_ref):
        p, i = pl.program_id(0), pl.program_id(1)
        q0 = i * tq
        seg_lo = (q0 // seg_len) * seg_len
        low = lax.broadcasted_iota(jnp.int32, (1, LANES), 1) < HEAD_DIM
        rel = lax.broadcasted_iota(jnp.int32, (sb, win), 1) - lax.broadcasted_iota(jnp.int32, (sb, win), 0)
        col = lax.broadcasted_iota(jnp.int32, (1, win), 1)
        for j in range(tq // sb):
            rows = pl.ds(j * sb, sb)
            ks, valid = _sub_band(q0 + j * sb, hw, win, seg_lo, seg_len, t, rel, col)
            kw = k_ref[pl.ds(ks, win), :]
            vw = v_ref[pl.ds(ks, win), :]
            s = _dot_nt(_split_heads(q_ref[rows, :], low), kw)
            es, inv, lses = [], [], []
            for a in range(2):
                sa = jnp.where(valid, s[a * sb:(a + 1) * sb], NEG)
                m = jnp.max(sa, axis=1, keepdims=True)
                if has_sink:
                    sk = sink_ref[2 * p + a]
                    m = jnp.maximum(m, sk)
                e = jnp.exp(sa - m)
                den = jnp.sum(e, axis=1, keepdims=True)
                if has_sink:
                    den = den + jnp.exp(sk - m)
                es.append(e.astype(BF16))
                inv.append(1.0 / den)
                lses.append(m + jnp.log(den))
            pv = _dot_nn(jnp.concatenate(es, axis=0), vw)
            o_ref[rows, :] = jnp.where(low, pv[0:sb] * inv[0], pv[sb:2 * sb] * inv[1]).astype(BF16)
            lse_ref[rows, :] = jnp.where(low, lses[0], lses[1])

    tile = pl.BlockSpec((tq, LANES), lambda p, i: (i, p))
    return pl.pallas_call(
        body, name=name, grid=(width // LANES, t // tq),
        out_shape=[jax.ShapeDtypeStruct((t, width), BF16), jax.ShapeDtypeStruct((t, width), F32)],
        in_specs=[pl.BlockSpec(memory_space=pltpu.SMEM), tile, _kv_spec(k, t), _kv_spec(v, t)],
        out_specs=[tile, tile],
        compiler_params=_cp(("arbitrary", "arbitrary")),
    )(sink, q, k, v)


def _attn_bwd(q, k, v, o, do, lse, sink, name, hw, seg_len, tq, has_sink):
    t, width = q.shape
    sb = SUB_ROWS
    win = 2 * hw + LANES
    n_q = t // tq
    shared_kv = k.ndim == 3

    def body(sink_ref, q_ref, k_ref, v_ref, o_ref, do_ref, lse_ref, dq_ref, dk_ref, dv_ref, ds_ref):
        p, i = pl.program_id(0), pl.program_id(1)
        fresh = (i == 0) & (p % 2 == 0) if shared_kv else i == 0

        @pl.when(fresh)
        def _():
            dk_ref[...] = jnp.zeros_like(dk_ref)
            dv_ref[...] = jnp.zeros_like(dv_ref)

        q0 = i * tq
        seg_lo = (q0 // seg_len) * seg_len
        low = lax.broadcasted_iota(jnp.int32, (1, LANES), 1) < HEAD_DIM
        rel = lax.broadcasted_iota(jnp.int32, (sb, win), 1) - lax.broadcasted_iota(jnp.int32, (sb, win), 0)
        col = lax.broadcasted_iota(jnp.int32, (1, win), 1)
        dsink = [jnp.zeros((1, 1), F32), jnp.zeros((1, 1), F32)]
        for j in range(tq // sb):
            rows = pl.ds(j * sb, sb)
            ks, valid = _sub_band(q0 + j * sb, hw, win, seg_lo, seg_len, t, rel, col)
            kw = k_ref[pl.ds(ks, win), :]
            vw = v_ref[pl.ds(ks, win), :]
            dov = do_ref[rows, :]
            prod = dov.astype(F32) * o_ref[rows, :].astype(F32)
            lse_t = lse_ref[rows, :]
            q2 = _split_heads(q_ref[rows, :], low)
            do2 = _split_heads(dov, low)
            s = _dot_nt(q2, kw)
            dpr = _dot_nt(do2, vw)
            prs, dss = [], []
            for a in range(2):
                mine = low if a == 0 else jnp.logical_not(low)
                lse_a = jnp.max(jnp.where(mine, lse_t, -jnp.inf), axis=1, keepdims=True)
                delta = jnp.sum(jnp.where(mine, prod, 0.0), axis=1, keepdims=True)
                pr = jnp.exp(jnp.where(valid, s[a * sb:(a + 1) * sb], NEG) - lse_a)
                prs.append(pr.astype(BF16))
                dss.append((pr * (dpr[a * sb:(a + 1) * sb] - delta)).astype(BF16))
                if has_sink:
                    dsink[a] = dsink[a] - jnp.sum(jnp.exp(sink_ref[2 * p + a] - lse_a) * delta, axis=0, keepdims=True)
            ds2 = jnp.concatenate(dss, axis=0)
            dv_ref[pl.ds(ks, win), :] += _dot_tn(jnp.concatenate(prs, axis=0), do2)
            dk_ref[pl.ds(ks, win), :] += _dot_tn(ds2, q2)
            dq2 = _dot_nn(ds2, kw)
            dq_ref[rows, :] = jnp.where(low, dq2[0:sb], dq2[sb:2 * sb])
        ds_ref[...] = jnp.broadcast_to(jnp.where(low, dsink[0], dsink[1]), ds_ref.shape)

    tile = pl.BlockSpec((tq, LANES), lambda p, i: (i, p))
    kv_shape = jax.ShapeDtypeStruct(k.shape, F32)
    return pl.pallas_call(
        body, name=name, grid=(width // LANES, n_q),
        out_shape=[jax.ShapeDtypeStruct((t, width), F32), kv_shape, kv_shape,
                   jax.ShapeDtypeStruct((width // LANES, n_q, 8, LANES), F32)],
        in_specs=[pl.BlockSpec(memory_space=pltpu.SMEM), tile, _kv_spec(k, t), _kv_spec(v, t), tile, tile, tile],
        out_specs=[tile, _kv_spec(k, t), _kv_spec(v, t), pl.BlockSpec((None, None, 8, LANES), lambda p, i: (p, i, 0, 0))],
        compiler_params=_cp(("arbitrary", "arbitrary")),
    )(sink, q, k, v, o, do, lse)


def _final_norm_loss(x, gn, target, name, tm):
    t, d = x.shape

    def body(x_ref, gn_ref, tg_ref, dx_ref, dgn_ref, sq_ref):
        i = pl.program_id(0)
        xv, gnv = x_ref[...], gn_ref[...]
        r = _rstd(xv)
        diff = (xv * r) * gnv - tg_ref[...]
        dxn, dgn = _norm_bwd(diff * (1.0 / d), xv, r, gnv)
        dx_ref[...] = dxn
        _accumulate(dgn_ref, dgn, i == 0)
        _accumulate(sq_ref, jnp.sum(diff * diff, axis=0, keepdims=True), i == 0)

    tok = pl.BlockSpec((tm, d), lambda i: (i, 0))
    row = pl.BlockSpec((1, d), lambda i: (0, 0))
    return pl.pallas_call(
        body, name=name, grid=(t // tm,),
        out_shape=[jax.ShapeDtypeStruct((t, d), F32), jax.ShapeDtypeStruct((1, d), F32), jax.ShapeDtypeStruct((1, d), F32)],
        in_specs=[tok, row, tok], out_specs=[tok, row, row],
        compiler_params=_cp(("arbitrary",)),
    )(x, gn, target)


def _adamw(w, g, m, v, name):
    def body(w_ref, g_ref, m_ref, v_ref, d_ref, nm_ref, nv_ref):
        gv = g_ref[...]
        nm = ADAM_B1 * m_ref[...] + (1.0 - ADAM_B1) * gv
        nv = ADAM_B2 * v_ref[...] + (1.0 - ADAM_B2) * (gv * gv)
        m_hat = nm / (1.0 - ADAM_B1 ** ADAM_STEP)
        v_hat = nv / (1.0 - ADAM_B2 ** ADAM_STEP)
        d_ref[...] = -ADAM_LR * (m_hat / (jnp.sqrt(v_hat) + ADAM_EPS) + ADAM_WD * w_ref[...])
        nm_ref[...] = nm
        nv_ref[...] = nv

    shape = jax.ShapeDtypeStruct(w.shape, F32)
    return pl.pallas_call(body, name=name, out_shape=[shape, shape, shape], compiler_params=_cp())(w, g, m, v)


def _sum_small(land, name):
    def body(l_ref, o_ref):
        acc = l_ref[0]
        for s in range(1, N_DEV):
            acc = acc + l_ref[s]
        o_ref[...] = acc

    return pl.pallas_call(body, name=name, out_shape=jax.ShapeDtypeStruct(land.shape[1:], F32), compiler_params=_cp())(land)


def _to_segments(a, dil):
    if dil == 1:
        return a
    t, w = a.shape
    return a.reshape(t // dil, dil, w).transpose(1, 0, 2).reshape(t, w)


def _from_segments(a, dil):
    if dil == 1:
        return a
    t, w = a.shape
    return a.reshape(dil, t // dil, w).transpose(1, 0, 2).reshape(t, w)


def _rope_lanes(positions):
    inv_freq = 1.0 / (ROPE_THETA ** (jnp.arange(0, HEAD_DIM, 2, dtype=F32) / HEAD_DIM))
    ang = positions.astype(F32)[:, None] * inv_freq
    cos, sin = jnp.cos(ang), jnp.sin(ang)
    return jnp.concatenate([cos, cos, cos, cos], axis=1), jnp.concatenate([-sin, sin, -sin, sin], axis=1)


def kernel(x, positions, norm_ffn1, w_gate1, w_up1, w_down1, norm_mix, w_in, a_sink, w_out, norm_ffn2, w_gate2, w_up2, w_down2, norm_final, loss_target, m_norm_ffn1, m_w_gate1, m_w_up1, m_w_down1, m_norm_mix, m_w_in, m_a_sink, m_w_out, m_norm_ffn2, m_w_gate2, m_w_up2, m_w_down2, m_norm_final, v_norm_ffn1, v_w_gate1, v_w_up1, v_w_down1, v_norm_mix, v_w_in, v_a_sink, v_w_out, v_norm_ffn2, v_w_gate2, v_w_up2, v_w_down2, v_norm_final):
    x = x[0]
    target = loss_target[0]
    t, d = x.shape
    tm = min(1024, t)
    tm_mix = min(512, t)
    tf = 256
    tq = min(512, t // 16)

    n_steps = (t // tm) * (w_gate1.shape[2] * N_DEV // tf)

    def stacked(shards):
        return jnp.concatenate([s.astype(BF16) for s in shards], axis=0), [s.shape[0] for s in shards]

    packed1, rows1 = stacked([w_gate1[0].T, w_up1[0].T, w_down1[0]])
    packed2, rows2 = stacked([w_in[0].T, w_out[0], w_gate2[0].T, w_up2[0].T, w_down2[0]])
    wg1, wu1, wd1 = _run_alone(_gather_plan(packed1, rows1), "gather_ffn1_weights")

    cos, sin_signed = _rope_lanes(positions[0])
    sink = a_sink[0]
    no_sink = jnp.zeros_like(sink)

    (x1, h1, gp1, up1), (win, wout, wg2, wu2, wd2) = _ffn_fwd(
        x, norm_ffn1, wg1, wu1, wd1, "ffn1_fwd", tm, tf, _gather_plan(packed2, rows2), [0, (3 * n_steps) // 4, n_steps - 1])
    h_mix, aq, akx, avx, bq, bk, bv = _in_proj_fwd(x1, norm_mix, win, cos, sin_signed, "in_proj_fwd", tm_mix)
    a_out, a_lse = _attn_fwd(aq, akx, avx, sink, "attn_a_fwd", A_HALF_WINDOW, t, tq, True)
    b_seg, b_outs, b_lses = [], [], []
    for window, dil in B_PATTERNS:
        qs, ks, vs = (_to_segments(a, dil) for a in (bq, bk, bv))
        o_seg, lse_seg = _attn_fwd(qs, ks, vs, no_sink, f"attn_b{dil}_fwd", window // (2 * dil), t // dil, tq, False)
        b_seg.append((qs, ks, vs))
        b_outs.append(_from_segments(o_seg, dil))
        b_lses.append(_from_segments(lse_seg, dil))
    x2, b_out, b_lse = _merge_out_proj_fwd(x1, a_out, b_outs, b_lses, wout, "out_proj_fwd", tm_mix)
    (x3, h2, gp2, up2), _ = _ffn_fwd(x2, norm_ffn2, wg2, wu2, wd2, "ffn2_fwd", tm, tf)

    gfinal = norm_final.reshape(1, d)
    dx3, dg_final, sq = _final_norm_loss(x3, gfinal, target, "final_norm_loss", tm)

    (dx2, dg_ffn2, dgt2, dut2, hid2, dout2), _ = _ffn_bwd_act(dx3, x2, norm_ffn2, gp2, up2, wg2, wu2, wd2, "ffn2_bwd_act", tm, tf)
    gwg2, gwu2, gwd2 = _ffn_bwd_weights(dgt2, dut2, hid2, h2, dout2, "ffn2_bwd_weights", tm, tf)

    da, db, gwout = _out_proj_bwd(dx2, a_out, b_out, wout, "out_proj_bwd", tm_mix)
    daq, dakx, davx, dsink_parts = _attn_bwd(aq, akx, avx, a_out, da, a_lse, sink, "attn_a_bwd", A_HALF_WINDOW, t, tq, True)
    dbq = dbk = dbv = None
    for (window, dil), (qs, ks, vs) in zip(B_PATTERNS, b_seg):
        os_, dos, ls = (_to_segments(a, dil) for a in (b_out, db, b_lse))
        dq_s, dk_s, dv_s, _ = _attn_bwd(qs, ks, vs, os_, dos, ls, no_sink, f"attn_b{dil}_bwd", window // (2 * dil), t // dil, tq, False)
        parts = [_from_segments(a, dil) for a in (dq_s, dk_s, dv_s)]
        dbq, dbk, dbv = parts if dbq is None else (dbq + parts[0], dbk + parts[1], dbv + parts[2])
    dx1, dg_mix, gwin = _in_proj_bwd(dx2, x1, norm_mix, win, h_mix, cos, sin_signed, daq, dakx, davx, dbq, dbk, dbv, "in_proj_bwd", tm_mix)

    (grad_x, dg_ffn1, dgt1, dut1, hid1, dout1), (land_ffn2, land_in, land_out) = _ffn_bwd_act(
        dx1, x, norm_ffn1, gp1, up1, wg1, wu1, wd1, "ffn1_bwd_act", tm, tf,
        _exchange_plan([[gwg2, gwu2, gwd2], [gwin], [gwout]]), [0, n_steps - 1])
    gwg1, gwu1, gwd1 = _ffn_bwd_weights(dgt1, dut1, hid1, h1, dout1, "ffn1_bwd_weights", tm, tf)

    dsink_pairs = jnp.sum(dsink_parts[:, :, 0, :], axis=1)
    dsink = jnp.stack([dsink_pairs[:, 0], dsink_pairs[:, HEAD_DIM]], axis=1).reshape(1, -1)
    small = jnp.concatenate([dg_ffn1, dg_mix, dg_ffn2, dg_final, jnp.pad(dsink, ((0, 0), (0, d - dsink.shape[1]))),
                             sq, jnp.zeros((2, d), F32)], axis=0)
    land_ffn1, land_small = _run_alone(
        _exchange_plan([[gwg1, gwu1, gwd1], [jnp.tile(small, (N_DEV, 1))]]), "scatter_ffn1_gradients")
    red_ffn = [_sum_slots(land_ffn1, 3, "sum_ffn1_grads"), _sum_slots(land_ffn2, 3, "sum_ffn2_grads")]
    red_in = _sum_slots(land_in, 1, "sum_in_grads")
    red_out = _sum_slots(land_out, 1, "sum_out_grads")
    red_small = _sum_small(land_small, "sum_small_grads")

    rf = rows1[0]
    ffn_block = lambda j: red_ffn[j // 3][(j % 3) * rf:(j % 3 + 1) * rf]
    n_sink = a_sink.shape[1]
    grads = {
        "norm_ffn1": red_small[0:1], "w_gate1": ffn_block(0).T[None], "w_up1": ffn_block(1).T[None], "w_down1": ffn_block(2)[None],
        "norm_mix": red_small[1:2], "w_in": red_in.T[None], "a_sink": red_small[4:5, :n_sink], "w_out": red_out[None],
        "norm_ffn2": red_small[2:3], "w_gate2": ffn_block(3).T[None], "w_up2": ffn_block(4).T[None], "w_down2": ffn_block(5)[None],
        "norm_final": red_small[3],
    }
    params = {
        "norm_ffn1": (norm_ffn1, m_norm_ffn1, v_norm_ffn1), "w_gate1": (w_gate1, m_w_gate1, v_w_gate1),
        "w_up1": (w_up1, m_w_up1, v_w_up1), "w_down1": (w_down1, m_w_down1, v_w_down1),
        "norm_mix": (norm_mix, m_norm_mix, v_norm_mix), "w_in": (w_in, m_w_in, v_w_in),
        "a_sink": (a_sink, m_a_sink, v_a_sink), "w_out": (w_out, m_w_out, v_w_out),
        "norm_ffn2": (norm_ffn2, m_norm_ffn2, v_norm_ffn2), "w_gate2": (w_gate2, m_w_gate2, v_w_gate2),
        "w_up2": (w_up2, m_w_up2, v_w_up2), "w_down2": (w_down2, m_w_down2, v_w_down2),
        "norm_final": (norm_final, m_norm_final, v_norm_final),
    }
    loss = 0.5 * jnp.sum(red_small[5]) / d
    deltas, new_ms, new_vs = [], [], []
    for name, (w, m, v) in params.items():
        as_block = (lambda a: a.reshape(1, -1)) if w.ndim == 1 else (lambda a: a)
        dl, nm, nv = _adamw(as_block(w), as_block(grads[name]), as_block(m), as_block(v), f"adamw_{name}")
        deltas.append(dl.reshape(w.shape))
        new_ms.append(nm.reshape(w.shape))
        new_vs.append(nv.reshape(w.shape))
    grad_list = [grads[name] for name in params]
    return (loss, grad_x[None], *grad_list, *deltas, *new_ms, *new_vs)
```

```python
import functools
import itertools

import numpy as np
import jax
import jax.numpy as jnp
from jax import lax
from jax.experimental import pallas as pl
from jax.experimental.pallas import tpu as pltpu

F32 = jnp.float32
BF16 = jnp.bfloat16

N_DEV = 8
HEAD_DIM = 64
LANES = 128
A_Q_W, A_KV_W, B_W = 512, 128, 512
A_HALF_WINDOW = 128
B_PATTERNS = ((128, 1), (512, 4), (2048, 16))
ROPE_THETA = 10000.0
NORM_EPS = 1e-6
FFN_RES_WEIGHT = 0.5
QK_SCALE = HEAD_DIM ** -0.5
NEG = -1e30

ADAM_LR = 0.001
ADAM_B1 = 0.9
ADAM_B2 = 0.999
ADAM_EPS = 1e-08
ADAM_WD = 0.01
ADAM_STEP = 10

MESH_T = pl.DeviceIdType.MESH
VMEM_LIMIT = 56 * 1024 * 1024


def _cp(sem=None, vmem=VMEM_LIMIT):
    return pltpu.CompilerParams(dimension_semantics=sem, vmem_limit_bytes=vmem)


def _dot_nn(a, b):
    return jnp.dot(a, b, preferred_element_type=F32)


def _dot_nt(a, b):
    return lax.dot_general(a, b, (((1,), (1,)), ((), ())), preferred_element_type=F32)


def _dot_tn(a, b):
    return lax.dot_general(a, b, (((0,), (0,)), ((), ())), preferred_element_type=F32)


def _rstd(xv):
    return lax.rsqrt(jnp.mean(xv * xv, axis=-1, keepdims=True) + NORM_EPS)


def _norm_bwd(dh, xv, r, gn):
    gy = dh * gn
    c = jnp.sum(gy * xv, axis=-1, keepdims=True) * (1.0 / xv.shape[-1])
    dx = r * gy - xv * (r * r * r * c)
    dgn = jnp.sum(dh * (xv * r), axis=0, keepdims=True)
    return dx, dgn


def _accumulate(ref, val, first):
    @pl.when(first)
    def _():
        ref[...] = val

    @pl.when(jnp.logical_not(first))
    def _():
        ref[...] += val


def _mesh_pos():
    return lax.axis_index("x"), lax.axis_index("y"), lax.axis_index("c")


def _dev_index(d):
    return 4 * d[0] + 2 * d[1] + d[2]


class _Comm:
    def __init__(self, inputs, out_shape, scratch, phases):
        self.inputs, self.out_shape, self.scratch, self.phases = inputs, out_shape, scratch, phases

    def specs(self):
        any_spec = pl.BlockSpec(memory_space=pl.ANY)
        return [any_spec] * len(self.inputs), [any_spec] * len(self.out_shape)


def _run_alone(comm, name):
    n_in, n_out = len(comm.inputs), len(comm.out_shape)

    def body(*refs):
        for phase in comm.phases:
            phase(refs[:n_in], refs[n_in:n_in + n_out], refs[n_in + n_out:])

    in_specs, out_specs = comm.specs()
    return pl.pallas_call(body, name=name, out_shape=comm.out_shape, in_specs=in_specs, out_specs=out_specs,
                          scratch_shapes=comm.scratch)(*comm.inputs)


def _run_hosted(comm, at, step, ins, outs, scr):
    for phase, when in zip(comm.phases, at):
        @pl.when(step == when)
        def _(phase=phase):
            phase(ins, outs, scr)


def _split(refs, *counts):
    parts, o = [], 0
    for n in counts:
        parts.append(refs[o:o + n])
        o += n
    return parts + [refs[o:]]


def _gather_plan(packed, rows_list):
    n_w = len(rows_list)
    offs = [int(o) for o in np.cumsum([0] + list(rows_list[:-1]))]
    d = packed.shape[1]

    def tools(ins, outs, scr):
        p_ref = ins[0]
        send_sems, recv_sems, local_sem = scr
        x, y, c = _mesh_pos()
        me, sibling = (x, y, c), (x, y, 1 - c)
        chips = [(1 - x, y), (x, 1 - y), (1 - x, 1 - y)]

        def rows(w, dev):
            start = pl.multiple_of(_dev_index(dev) * rows_list[w], 16)
            return outs[w].at[pl.ds(start, rows_list[w]), :]

        def mine(w):
            return p_ref.at[pl.ds(offs[w], rows_list[w]), :]

        def copy(k, w, block, to, own):
            return pltpu.make_async_remote_copy(
                src_ref=mine(w) if own else rows(w, block), dst_ref=rows(w, block),
                send_sem=send_sems.at[k], recv_sem=recv_sems.at[k], device_id=to, device_id_type=MESH_T)

        def all_blocks(k):
            return pltpu.make_async_remote_copy(
                src_ref=p_ref, dst_ref=p_ref, send_sem=send_sems.at[k], recv_sem=recv_sems.at[k],
                device_id=me, device_id_type=MESH_T)

        return p_ref, local_sem, me, sibling, chips, c, rows, mine, copy, all_blocks

    def start(ins, outs, scr):
        _, local_sem, me, sibling, chips, c, rows, mine, copy, _ = tools(ins, outs, scr)
        for w in range(n_w):
            pltpu.make_async_copy(mine(w), rows(w, me), local_sem).start()
        for w in range(n_w):
            copy(0, w, me, sibling, True).start()
        for j, chip in enumerate(chips):
            for w in range(n_w):
                copy(1 + j, w, me, (*chip, c), True).start()

    def relay(ins, outs, scr):
        _, _, _, sibling, chips, c, _, _, copy, all_blocks = tools(ins, outs, scr)
        for j, chip in enumerate(chips):
            all_blocks(1 + j).wait_recv()
            for w in range(n_w):
                copy(4 + j, w, (*chip, c), sibling, False).start()

    def finish(ins, outs, scr):
        p_ref, local_sem, _, _, _, _, _, _, _, all_blocks = tools(ins, outs, scr)
        all_blocks(0).wait_recv()
        for j in range(3):
            all_blocks(4 + j).wait_recv()
        for k in range(7):
            all_blocks(k).wait_send()
        pltpu.make_async_copy(p_ref, p_ref, local_sem).wait()

    return _Comm(
        [packed], [jax.ShapeDtypeStruct((N_DEV * r, d), packed.dtype) for r in rows_list],
        [pltpu.SemaphoreType.DMA((7,)), pltpu.SemaphoreType.DMA((7,)), pltpu.SemaphoreType.DMA], [start, relay, finish])


def _exchange_plan(groups):
    flat = [a for g in groups for a in g]
    n_g = len(groups)
    sizes = [len(g) for g in groups]
    rows = [g[0].shape[0] // N_DEV for g in groups]
    first = [int(o) for o in np.cumsum([0] + sizes[:-1])]

    def start(srcs, lands, scr):
        send_sems, recv_sems, local_sems = scr
        x, y, c = _mesh_pos()
        me = (x, y, c)
        me_idx = _dev_index(me)

        def block(g, i, dev):
            start_row = pl.multiple_of(_dev_index(dev) * rows[g], 8)
            return srcs[first[g] + i].at[pl.ds(start_row, rows[g]), :]

        def slot(g, i):
            return lands[g].at[me_idx, pl.ds(i * rows[g], rows[g]), :]

        for g in range(n_g):
            for i in range(sizes[g]):
                pltpu.make_async_copy(block(g, i, me), slot(g, i), local_sems.at[g]).start()
        flips = [f for f in itertools.product((0, 1), repeat=3) if any(f)]
        for k, (fx, fy, fc) in enumerate(flips):
            peer = (1 - x if fx else x, 1 - y if fy else y, 1 - c if fc else c)
            for g in range(n_g):
                for i in range(sizes[g]):
                    pltpu.make_async_remote_copy(
                        src_ref=block(g, i, peer), dst_ref=slot(g, i), send_sem=send_sems.at[g, k],
                        recv_sem=recv_sems.at[g, k], device_id=peer, device_id_type=MESH_T).start()

    def finish(srcs, lands, scr):
        send_sems, recv_sems, local_sems = scr
        me = _mesh_pos()
        for k in range(7):
            for g in range(n_g):
                pltpu.make_async_remote_copy(
                    src_ref=lands[g].at[0], dst_ref=lands[g].at[0], send_sem=send_sems.at[g, k],
                    recv_sem=recv_sems.at[g, k], device_id=me, device_id_type=MESH_T).wait()
        for g in range(n_g):
            pltpu.make_async_copy(lands[g].at[0], lands[g].at[0], local_sems.at[g]).wait()

    return _Comm(
        flat, [jax.ShapeDtypeStruct((N_DEV, sizes[g] * rows[g], groups[g][0].shape[1]), groups[g][0].dtype) for g in range(n_g)],
        [pltpu.SemaphoreType.DMA((n_g, 7)), pltpu.SemaphoreType.DMA((n_g, 7)), pltpu.SemaphoreType.DMA((n_g,))],
        [start, finish])


def _sum_slots(land, n_blocks, name):
    _, total, d = land.shape
    rows = total // n_blocks

    def body(l_ref, o_ref):
        s = pl.program_id(1)
        _accumulate(o_ref, l_ref[...].astype(F32), s == 0)

    return pl.pallas_call(
        body, name=name, grid=(n_blocks, N_DEV),
        out_shape=jax.ShapeDtypeStruct((total, d), F32),
        in_specs=[pl.BlockSpec((None, rows, d), lambda w, s: (s, w, 0))],
        out_specs=pl.BlockSpec((rows, d), lambda w, s: (w, 0)),
        compiler_params=_cp(("arbitrary", "arbitrary")),
    )(land)


def _comm_parts(comm):
    if comm is None:
        return 0, 0, [], [], [], [], []
    in_specs, out_specs = comm.specs()
    return len(comm.inputs), len(comm.out_shape), in_specs, out_specs, comm.out_shape, comm.scratch, comm.inputs


def _ffn_fwd(x, gn, wg_t, wu_t, wd, name, tm, tf, comm=None, comm_at=None):
    t, d = x.shape
    f_all = wg_t.shape[0]
    n_f = f_all // tf
    n_ci, n_co, c_in_specs, c_out_specs, c_shapes, c_scratch, c_inputs = _comm_parts(comm)

    def body(*refs):
        ((x_ref, gn_ref, wg_ref, wu_ref, wd_ref), c_in, (y_ref, h_ref, gp_ref, up_ref, hid_ref), c_out,
         (h_s, hid_s), c_scr) = _split(refs, 5, n_ci, 5, n_co, 2)
        f = pl.program_id(1)
        if comm:
            _run_hosted(comm, comm_at, pl.program_id(0) * n_f + f, c_in, c_out, c_scr)

        @pl.when(f == 0)
        def _():
            xv = x_ref[...]
            h = ((xv * _rstd(xv)) * gn_ref[...]).astype(BF16)
            h_s[...] = h
            h_ref[...] = h

        h = h_s[...]
        g = _dot_nt(h, wg_ref[...])
        u = _dot_nt(h, wu_ref[...])
        gp_ref[...] = g.astype(BF16)
        up_ref[...] = u.astype(BF16)
        hid = ((g * jax.nn.sigmoid(g)) * u).astype(BF16)
        hid_ref[...] = hid
        for f0 in range(n_f):
            @pl.when(f == f0)
            def _(f0=f0):
                hid_s[:, f0 * tf:(f0 + 1) * tf] = hid

        @pl.when(f == n_f - 1)
        def _():
            y_ref[...] = x_ref[...] + FFN_RES_WEIGHT * _dot_nn(hid_s[...], wd_ref[...])

    tok = pl.BlockSpec((tm, d), lambda i, f: (i, 0))
    wblk = pl.BlockSpec((tf, d), lambda i, f: (f, 0))
    whole = pl.BlockSpec((f_all, d), lambda i, f: (0, 0))
    act = pl.BlockSpec((tm, tf), lambda i, f: (i, f))
    act_shape = jax.ShapeDtypeStruct((t, f_all), BF16)
    res = pl.pallas_call(
        body, name=name, grid=(t // tm, n_f),
        out_shape=[jax.ShapeDtypeStruct((t, d), F32), jax.ShapeDtypeStruct((t, d), BF16), act_shape, act_shape, act_shape] + c_shapes,
        in_specs=[tok, pl.BlockSpec((1, d), lambda i, f: (0, 0)), wblk, wblk, whole] + c_in_specs,
        out_specs=[tok, tok, act, act, act] + c_out_specs,
        scratch_shapes=[pltpu.VMEM((tm, d), BF16), pltpu.VMEM((tm, f_all), BF16)] + c_scratch,
        compiler_params=_cp(("arbitrary", "arbitrary")),
    )(x, gn, wg_t, wu_t, wd, *c_inputs)
    return res[:5], res[5:]


def _ffn_bwd_hidden(dout, gp, up, wd, name, tm, tf):
    t, d = dout.shape
    f_all = wd.shape[0]

    def body(dout_ref, gp_ref, up_ref, wd_ref, dg_ref, du_ref):
        dhid = _dot_nt(dout_ref[...], wd_ref[...])
        g = gp_ref[...].astype(F32)
        u = up_ref[...].astype(F32)
        sg = jax.nn.sigmoid(g)
        dg_ref[...] = ((dhid * u) * (sg * (1.0 + g * (1.0 - sg)))).astype(BF16)
        du_ref[...] = (dhid * (g * sg)).astype(BF16)

    tok = pl.BlockSpec((tm, d), lambda i, f: (i, 0))
    wblk = pl.BlockSpec((tf, d), lambda i, f: (f, 0))
    act = pl.BlockSpec((tm, tf), lambda i, f: (i, f))
    act_shape = jax.ShapeDtypeStruct((t, f_all), BF16)
    return pl.pallas_call(
        body, name=name, grid=(t // tm, f_all // tf),
        out_shape=[act_shape, act_shape], in_specs=[tok, act, act, wblk], out_specs=[act, act],
        compiler_params=_cp(("arbitrary", "arbitrary")),
    )(dout, gp, up, wd)


def _ffn_bwd_input(dy, x, gn, dg, du, wg_t, wu_t, name, tm, comm=None, comm_at=None):
    t, d = x.shape
    f_all = wg_t.shape[0]
    n_ci, n_co, c_in_specs, c_out_specs, c_shapes, c_scratch, c_inputs = _comm_parts(comm)

    def body(*refs):
        (dy_ref, x_ref, gn_ref, dg_ref, du_ref, wg_ref, wu_ref), c_in, (dx_ref, dgn_ref), c_out, c_scr = _split(
            refs, 7, n_ci, 2, n_co)
        i = pl.program_id(0)
        if comm:
            _run_hosted(comm, comm_at, i, c_in, c_out, c_scr)
        dh = _dot_nn(dg_ref[...], wg_ref[...]) + _dot_nn(du_ref[...], wu_ref[...])
        xv = x_ref[...]
        dxn, dgn = _norm_bwd(dh, xv, _rstd(xv), gn_ref[...])
        dx_ref[...] = dy_ref[...] + dxn
        _accumulate(dgn_ref, dgn, i == 0)

    tok = pl.BlockSpec((tm, d), lambda i: (i, 0))
    row = pl.BlockSpec((1, d), lambda i: (0, 0))
    act = pl.BlockSpec((tm, f_all), lambda i: (i, 0))
    whole = pl.BlockSpec((f_all, d), lambda i: (0, 0))
    res = pl.pallas_call(
        body, name=name, grid=(t // tm,),
        out_shape=[jax.ShapeDtypeStruct((t, d), F32), jax.ShapeDtypeStruct((1, d), F32)] + c_shapes,
        in_specs=[tok, tok, row, act, act, whole, whole] + c_in_specs,
        out_specs=[tok, row] + c_out_specs,
        scratch_shapes=c_scratch,
        compiler_params=_cp(("arbitrary",)),
    )(dy, x, gn, dg, du, wg_t, wu_t, *c_inputs)
    return res[:2], res[2:]


def _token_products(lhs_list, rhs, name, tt, tf, comm=None, comm_at=None):
    n_l = len(lhs_list)
    t, f_all = lhs_list[0].shape
    d = rhs.shape[1]
    n_t = t // tt
    n_ci, n_co, c_in_specs, c_out_specs, c_shapes, c_scratch, c_inputs = _comm_parts(comm)

    def body(*refs):
        lhs_refs, (rhs_ref,), c_in, out_refs, c_out, accs, c_scr = _split(refs, n_l, 1, n_ci, n_l, n_co, n_l)
        s = pl.program_id(1)
        if comm:
            _run_hosted(comm, comm_at, pl.program_id(0) * n_t + s, c_in, c_out, c_scr)
        rv = rhs_ref[...]
        for l_ref, acc in zip(lhs_refs, accs):
            _accumulate(acc, _dot_tn(l_ref[...], rv), s == 0)

        @pl.when(s == n_t - 1)
        def _():
            for o_ref, acc in zip(out_refs, accs):
                o_ref[...] = acc[...].astype(BF16)

    act = pl.BlockSpec((tt, tf), lambda f, s: (s, f))
    tok = pl.BlockSpec((tt, d), lambda f, s: (s, 0))
    wblk = pl.BlockSpec((tf, d), lambda f, s: (f, 0))
    res = pl.pallas_call(
        body, name=name, grid=(f_all // tf, n_t),
        out_shape=[jax.ShapeDtypeStruct((f_all, d), BF16)] * n_l + c_shapes,
        in_specs=[act] * n_l + [tok] + c_in_specs, out_specs=[wblk] * n_l + c_out_specs,
        scratch_shapes=[pltpu.VMEM((tf, d), F32)] * n_l + c_scratch,
        compiler_params=_cp(("arbitrary", "arbitrary")),
    )(*lhs_list, rhs, *c_inputs)
    return res[:n_l], res[n_l:]


def _swap_halves(t):
    w = t.shape[-1]
    lane = lax.broadcasted_iota(jnp.int32, (1, w), 1)
    return jnp.where((lane % HEAD_DIM) < HEAD_DIM // 2, pltpu.roll(t, w - HEAD_DIM // 2, 1), pltpu.roll(t, HEAD_DIM // 2, 1))


def _rope(t, cos, sin_signed):
    reps = t.shape[-1] // LANES
    return t * jnp.tile(cos, (1, reps)) + _swap_halves(t) * jnp.tile(sin_signed, (1, reps))


def _rope_bwd(dt, cos, sin_signed):
    reps = dt.shape[-1] // LANES
    return dt * jnp.tile(cos, (1, reps)) + _swap_halves(dt * jnp.tile(sin_signed, (1, reps)))


def _in_proj_fwd(x, gn, win_t, cos, sin_signed, name, tm):
    t, d = x.shape
    in_w = win_t.shape[0]

    def body(x_ref, gn_ref, w_ref, cos_ref, sin_ref, h_ref, aq_ref, akx_ref, avx_ref, bq_ref, bk_ref, bv_ref):
        xv = x_ref[...]
        h = ((xv * _rstd(xv)) * gn_ref[...]).astype(BF16)
        h_ref[...] = h
        p = _dot_nt(h, w_ref[...])
        cs, sn = cos_ref[...], sin_ref[...]
        o = 0
        aq_ref[...] = (_rope(p[:, o:o + A_Q_W], cs, sn) * QK_SCALE).astype(BF16)
        o += A_Q_W
        ak = _rope(p[:, o:o + A_KV_W], cs, sn)
        o += A_KV_W
        av = p[:, o:o + A_KV_W]
        o += A_KV_W
        low = lax.broadcasted_iota(jnp.int32, (1, LANES), 1) < HEAD_DIM
        for src, dst in ((ak, akx_ref), (av, avx_ref)):
            other = pltpu.roll(src, HEAD_DIM, 1)
            dst[0] = jnp.where(low, src, other).astype(BF16)
            dst[1] = jnp.where(low, other, src).astype(BF16)
        bq_ref[...] = (_rope(p[:, o:o + B_W], cs, sn) * QK_SCALE).astype(BF16)
        o += B_W
        bk_ref[...] = _rope(p[:, o:o + B_W], cs, sn).astype(BF16)
        o += B_W
        bv_ref[...] = p[:, o:o + B_W].astype(BF16)

    tok = lambda w: pl.BlockSpec((tm, w), lambda i: (i, 0))
    kvx = pl.BlockSpec((2, tm, LANES), lambda i: (0, i, 0))
    sd = lambda *s: jax.ShapeDtypeStruct(s, BF16)
    return pl.pallas_call(
        body, name=name, grid=(t // tm,),
        out_shape=[sd(t, d), sd(t, A_Q_W), sd(2, t, LANES), sd(2, t, LANES), sd(t, B_W), sd(t, B_W), sd(t, B_W)],
        in_specs=[tok(d), pl.BlockSpec((1, d), lambda i: (0, 0)), pl.BlockSpec((in_w, d), lambda i: (0, 0)),
                  tok(LANES), tok(LANES)],
        out_specs=[tok(d), tok(A_Q_W), kvx, kvx, tok(B_W), tok(B_W), tok(B_W)],
        compiler_params=_cp(("arbitrary",)),
    )(x, gn, win_t, cos, sin_signed)


def _in_proj_bwd(dres, x, gn, win_t, h, cos, sin_signed, daq, dakx, davx, dbq, dbk, dbv, name, tm):
    t, d = x.shape
    in_w = win_t.shape[0]
    n_t = t // tm

    def body(dres_ref, x_ref, gn_ref, w_ref, h_ref, cos_ref, sin_ref, daq_ref, dakx_ref, davx_ref, dbq_ref, dbk_ref,
             dbv_ref, dx_ref, dgn_ref, gw_ref, half_ref, dp_s, gw_s):
        i = pl.program_id(0)
        cs, sn = cos_ref[...], sin_ref[...]
        low = lax.broadcasted_iota(jnp.int32, (1, LANES), 1) < HEAD_DIM

        def fold(ref):
            a, b = ref[0], ref[1]
            return jnp.where(low, a + pltpu.roll(a, HEAD_DIM, 1), b + pltpu.roll(b, HEAD_DIM, 1))

        o = 0
        dp_s[:, o:o + A_Q_W] = _rope_bwd(daq_ref[...] * QK_SCALE, cs, sn).astype(BF16)
        o += A_Q_W
        dp_s[:, o:o + A_KV_W] = _rope_bwd(fold(dakx_ref), cs, sn).astype(BF16)
        o += A_KV_W
        dp_s[:, o:o + A_KV_W] = fold(davx_ref).astype(BF16)
        o += A_KV_W
        dp_s[:, o:o + B_W] = _rope_bwd(dbq_ref[...] * QK_SCALE, cs, sn).astype(BF16)
        o += B_W
        dp_s[:, o:o + B_W] = _rope_bwd(dbk_ref[...], cs, sn).astype(BF16)
        o += B_W
        dp_s[:, o:o + B_W] = dbv_ref[...].astype(BF16)
        dh = _dot_nn(dp_s[...], w_ref[...])
        hv = h_ref[...]
        for c0 in range(0, in_w, 2 * LANES):
            _accumulate(gw_s.at[pl.ds(c0, 2 * LANES), :], _dot_tn(dp_s[:, c0:c0 + 2 * LANES], hv), i == 0)
        xv = x_ref[...]
        dxn, dgn = _norm_bwd(dh, xv, _rstd(xv), gn_ref[...])
        dx = dres_ref[...] + dxn
        dx_ref[...] = dx
        half_ref[...] = (FFN_RES_WEIGHT * dx).astype(BF16)
        _accumulate(dgn_ref, dgn, i == 0)

        @pl.when(i == n_t - 1)
        def _():
            gw_ref[...] = gw_s[...].astype(BF16)

    tok = lambda w: pl.BlockSpec((tm, w), lambda i: (i, 0))
    row = pl.BlockSpec((1, d), lambda i: (0, 0))
    whole = pl.BlockSpec((in_w, d), lambda i: (0, 0))
    kvx = pl.BlockSpec((2, tm, LANES), lambda i: (0, i, 0))
    return pl.pallas_call(
        body, name=name, grid=(n_t,),
        out_shape=[jax.ShapeDtypeStruct((t, d), F32), jax.ShapeDtypeStruct((1, d), F32),
                   jax.ShapeDtypeStruct((in_w, d), BF16), jax.ShapeDtypeStruct((t, d), BF16)],
        in_specs=[tok(d), tok(d), row, whole, tok(d), tok(LANES), tok(LANES), tok(A_Q_W), kvx, kvx,
                  tok(B_W), tok(B_W), tok(B_W)],
        out_specs=[tok(d), row, whole, tok(d)],
        scratch_shapes=[pltpu.VMEM((tm, in_w), BF16), pltpu.VMEM((in_w, d), F32)],
        compiler_params=_cp(("arbitrary",)),
    )(dres, x, gn, win_t, h, cos, sin_signed, daq, dakx, davx, dbq, dbk, dbv)


def _merge_out_proj_fwd(x, a_out, outs, lses, wout, name, tm):
    t, d = x.shape

    def body(x_ref, a_ref, o0, o1, o2, l0, l1, l2, w_ref, y_ref, b_ref, lt_ref):
        ls = [l0[...], l1[...], l2[...]]
        mx = jnp.maximum(jnp.maximum(ls[0], ls[1]), ls[2])
        es = [jnp.exp(l - mx) for l in ls]
        den = es[0] + es[1] + es[2]
        b = (es[0] * o0[...].astype(F32) + es[1] * o1[...].astype(F32) + es[2] * o2[...].astype(F32)) / den
        bb = b.astype(BF16)
        b_ref[...] = bb
        lt_ref[...] = mx + jnp.log(den)
        y_ref[...] = x_ref[...] + _dot_nn(a_ref[...], w_ref[0:A_Q_W, :]) + _dot_nn(bb, w_ref[A_Q_W:A_Q_W + B_W, :])

    tok = lambda w: pl.BlockSpec((tm, w), lambda i: (i, 0))
    return pl.pallas_call(
        body, name=name, grid=(t // tm,),
        out_shape=[jax.ShapeDtypeStruct((t, d), F32), jax.ShapeDtypeStruct((t, B_W), BF16),
                   jax.ShapeDtypeStruct((t, B_W), F32)],
        in_specs=[tok(d), tok(A_Q_W)] + [tok(B_W)] * 6 + [pl.BlockSpec(wout.shape, lambda i: (0, 0))],
        out_specs=[tok(d), tok(B_W), tok(B_W)],
        compiler_params=_cp(("arbitrary",)),
    )(x, a_out, *outs, *lses, wout)


def _out_proj_bwd(dy, a_out, b_out, wout, name, tm):
    t, d = dy.shape
    n_t = t // tm

    def body(dy_ref, a_ref, b_ref, w_ref, da_ref, db_ref, gw_ref, gw_s):
        i = pl.program_id(0)
        dyb = dy_ref[...].astype(BF16)
        da_ref[...] = _dot_nt(dyb, w_ref[0:A_Q_W, :]).astype(BF16)
        db_ref[...] = _dot_nt(dyb, w_ref[A_Q_W:A_Q_W + B_W, :]).astype(BF16)
        ga = _dot_tn(a_ref[...], dyb)
        gb = _dot_tn(b_ref[...], dyb)

        @pl.when(i == 0)
        def _():
            gw_s[0:A_Q_W, :] = ga
            gw_s[A_Q_W:A_Q_W + B_W, :] = gb

        @pl.when(i > 0)
        def _():
            gw_s[0:A_Q_W, :] += ga
            gw_s[A_Q_W:A_Q_W + B_W, :] += gb

        @pl.when(i == n_t - 1)
        def _():
            gw_ref[...] = gw_s[...].astype(BF16)

    tok = lambda w: pl.BlockSpec((tm, w), lambda i: (i, 0))
    whole = pl.BlockSpec(wout.shape, lambda i: (0, 0))
    return pl.pallas_call(
        body, name=name, grid=(n_t,),
        out_shape=[jax.ShapeDtypeStruct((t, A_Q_W), BF16), jax.ShapeDtypeStruct((t, B_W), BF16),
                   jax.ShapeDtypeStruct(wout.shape, BF16)],
        in_specs=[tok(d), tok(A_Q_W), tok(B_W), whole], out_specs=[tok(A_Q_W), tok(B_W), whole],
        scratch_shapes=[pltpu.VMEM(wout.shape, F32)],
        compiler_params=_cp(("arbitrary",)),
    )(dy, a_out, b_out, wout)


SUB_ROWS = 64


def _sub_band(r0, hw, win, seg_lo, seg_len, t, rel, col):
    ks = pl.multiple_of(jnp.clip(r0 - hw, 0, t - win), SUB_ROWS)
    kpos = col + ks
    valid = (jnp.abs(rel + (ks - r0)) <= hw) & (kpos >= seg_lo) & (kpos < seg_lo + seg_len)
    return ks, valid


def _split_heads(v, low):
    zero = jnp.zeros_like(v)
    return jnp.concatenate([jnp.where(low, v, zero), jnp.where(low, zero, v)], axis=0)


def _kv_spec(kv, t):
    if kv.ndim == 3:
        return pl.BlockSpec((None, t, LANES), lambda p, i: (p // 2, 0, 0))
    return pl.BlockSpec((t, LANES), lambda p, i: (0, p))


def _attn_fwd(q, k, v, sink, name, hw, seg_len, tq, has_sink):
    t, width = q.shape
    sb = SUB_ROWS
    win = 2 * hw + LANES

    def body(sink_ref, q_ref, k_ref, v_ref, o_ref, lse_ref):
        p, i = pl.program_id(0), pl.program_id(1)
        q0 = i * tq
        seg_lo = (q0 // seg_len) * seg_len
        low = lax.broadcasted_iota(jnp.int32, (1, LANES), 1) < HEAD_DIM
        rel = lax.broadcasted_iota(jnp.int32, (sb, win), 1) - lax.broadcasted_iota(jnp.int32, (sb, win), 0)
        col = lax.broadcasted_iota(jnp.int32, (1, win), 1)
        for j in range(tq // sb):
            rows = pl.ds(j * sb, sb)
            ks, valid = _sub_band(q0 + j * sb, hw, win, seg_lo, seg_len, t, rel, col)
            kw = k_ref[pl.ds(ks, win), :]
            vw = v_ref[pl.ds(ks, win), :]
            s = _dot_nt(_split_heads(q_ref[rows, :], low), kw)
            es, inv, lses = [], [], []
            for a in range(2):
                sa = jnp.where(valid, s[a * sb:(a + 1) * sb], NEG)
                m = jnp.max(sa, axis=1, keepdims=True)
                if has_sink:
                    sk = sink_ref[2 * p + a]
                    m = jnp.maximum(m, sk)
                e = jnp.exp(sa - m)
                den = jnp.sum(e, axis=1, keepdims=True)
                if has_sink:
                    den = den + jnp.exp(sk - m)
                es.append(e.astype(BF16))
                inv.append(1.0 / den)
                lses.append(m + jnp.log(den))
            pv = _dot_nn(jnp.concatenate(es, axis=0), vw)
            o_ref[rows, :] = jnp.where(low, pv[0:sb] * inv[0], pv[sb:2 * sb] * inv[1]).astype(BF16)
            lse_ref[rows, :] = jnp.where(low, lses[0], lses[1])

    tile = pl.BlockSpec((tq, LANES), lambda p, i: (i, p))
    return pl.pallas_call(
        body, name=name, grid=(width // LANES, t // tq),
        out_shape=[jax.ShapeDtypeStruct((t, width), BF16), jax.ShapeDtypeStruct((t, width), F32)],
        in_specs=[pl.BlockSpec(memory_space=pltpu.SMEM), tile, _kv_spec(k, t), _kv_spec(v, t)],
        out_specs=[tile, tile],
        compiler_params=_cp(("arbitrary", "arbitrary")),
    )(sink, q, k, v)


def _attn_bwd(q, k, v, o, do, lse, sink, name, hw, seg_len, tq, has_sink, comm=None, comm_at=None):
    t, width = q.shape
    sb = SUB_ROWS
    win = 2 * hw + LANES
    n_q = t // tq
    shared_kv = k.ndim == 3
    n_ci, n_co, c_in_specs, c_out_specs, c_shapes, c_scratch, c_inputs = _comm_parts(comm)

    def body(*refs):
        ((sink_ref, q_ref, k_ref, v_ref, o_ref, do_ref, lse_ref), c_in, (dq_ref, dk_ref, dv_ref, ds_ref), c_out, c_scr) = _split(
            refs, 7, n_ci, 4, n_co)
        p, i = pl.program_id(0), pl.program_id(1)
        if comm:
            _run_hosted(comm, comm_at, p * n_q + i, c_in, c_out, c_scr)
        fresh = (i == 0) & (p % 2 == 0) if shared_kv else i == 0

        @pl.when(fresh)
        def _():
            dk_ref[...] = jnp.zeros_like(dk_ref)
            dv_ref[...] = jnp.zeros_like(dv_ref)

        q0 = i * tq
        seg_lo = (q0 // seg_len) * seg_len
        low = lax.broadcasted_iota(jnp.int32, (1, LANES), 1) < HEAD_DIM
        rel = lax.broadcasted_iota(jnp.int32, (sb, win), 1) - lax.broadcasted_iota(jnp.int32, (sb, win), 0)
        col = lax.broadcasted_iota(jnp.int32, (1, win), 1)
        dsink = [jnp.zeros((1, 1), F32), jnp.zeros((1, 1), F32)]
        for j in range(tq // sb):
            rows = pl.ds(j * sb, sb)
            ks, valid = _sub_band(q0 + j * sb, hw, win, seg_lo, seg_len, t, rel, col)
            kw = k_ref[pl.ds(ks, win), :]
            vw = v_ref[pl.ds(ks, win), :]
            dov = do_ref[rows, :]
            prod = dov.astype(F32) * o_ref[rows, :].astype(F32)
            lse_t = lse_ref[rows, :]
            q2 = _split_heads(q_ref[rows, :], low)
            do2 = _split_heads(dov, low)
            s = _dot_nt(q2, kw)
            dpr = _dot_nt(do2, vw)
            prs, dss = [], []
            for a in range(2):
                mine = low if a == 0 else jnp.logical_not(low)
                lse_a = jnp.max(jnp.where(mine, lse_t, -jnp.inf), axis=1, keepdims=True)
                delta = jnp.sum(jnp.where(mine, prod, 0.0), axis=1, keepdims=True)
                pr = jnp.exp(jnp.where(valid, s[a * sb:(a + 1) * sb], NEG) - lse_a)
                prs.append(pr.astype(BF16))
                dss.append((pr * (dpr[a * sb:(a + 1) * sb] - delta)).astype(BF16))
                if has_sink:
                    dsink[a] = dsink[a] - jnp.sum(jnp.exp(sink_ref[2 * p + a] - lse_a) * delta, axis=0, keepdims=True)
            ds2 = jnp.concatenate(dss, axis=0)
            dv_ref[pl.ds(ks, win), :] += _dot_tn(jnp.concatenate(prs, axis=0), do2)
            dk_ref[pl.ds(ks, win), :] += _dot_tn(ds2, q2)
            dq2 = _dot_nn(ds2, kw)
            dq_ref[rows, :] = jnp.where(low, dq2[0:sb], dq2[sb:2 * sb])
        ds_ref[...] = jnp.broadcast_to(jnp.where(low, dsink[0], dsink[1]), ds_ref.shape)

    tile = pl.BlockSpec((tq, LANES), lambda p, i: (i, p))
    kv_shape = jax.ShapeDtypeStruct(k.shape, F32)
    res = pl.pallas_call(
        body, name=name, grid=(width // LANES, n_q),
        out_shape=[jax.ShapeDtypeStruct((t, width), F32), kv_shape, kv_shape,
                   jax.ShapeDtypeStruct((width // LANES, n_q, 8, LANES), F32)] + c_shapes,
        in_specs=[pl.BlockSpec(memory_space=pltpu.SMEM), tile, _kv_spec(k, t), _kv_spec(v, t), tile, tile, tile] + c_in_specs,
        out_specs=[tile, _kv_spec(k, t), _kv_spec(v, t), pl.BlockSpec((None, None, 8, LANES), lambda p, i: (p, i, 0, 0))] + c_out_specs,
        scratch_shapes=c_scratch,
        compiler_params=_cp(("arbitrary", "arbitrary")),
    )(sink, q, k, v, o, do, lse, *c_inputs)
    return res[:4], res[4:]


def _final_norm_loss(x, gn, target, name, tm):
    t, d = x.shape

    def body(x_ref, gn_ref, tg_ref, dx_ref, dgn_ref, sq_ref, half_ref):
        i = pl.program_id(0)
        xv, gnv = x_ref[...], gn_ref[...]
        r = _rstd(xv)
        diff = (xv * r) * gnv - tg_ref[...]
        dxn, dgn = _norm_bwd(diff * (1.0 / d), xv, r, gnv)
        dx_ref[...] = dxn
        half_ref[...] = (FFN_RES_WEIGHT * dxn).astype(BF16)
        _accumulate(dgn_ref, dgn, i == 0)
        _accumulate(sq_ref, jnp.sum(diff * diff, axis=0, keepdims=True), i == 0)

    tok = pl.BlockSpec((tm, d), lambda i: (i, 0))
    row = pl.BlockSpec((1, d), lambda i: (0, 0))
    return pl.pallas_call(
        body, name=name, grid=(t // tm,),
        out_shape=[jax.ShapeDtypeStruct((t, d), F32), jax.ShapeDtypeStruct((1, d), F32), jax.ShapeDtypeStruct((1, d), F32),
                   jax.ShapeDtypeStruct((t, d), BF16)],
        in_specs=[tok, row, tok], out_specs=[tok, row, row, tok],
        compiler_params=_cp(("arbitrary",)),
    )(x, gn, target)


def _adamw(w, g, m, v, name):
    def body(w_ref, g_ref, m_ref, v_ref, d_ref, nm_ref, nv_ref):
        gv = g_ref[...]
        nm = ADAM_B1 * m_ref[...] + (1.0 - ADAM_B1) * gv
        nv = ADAM_B2 * v_ref[...] + (1.0 - ADAM_B2) * (gv * gv)
        m_hat = nm / (1.0 - ADAM_B1 ** ADAM_STEP)
        v_hat = nv / (1.0 - ADAM_B2 ** ADAM_STEP)
        d_ref[...] = -ADAM_LR * (m_hat / (jnp.sqrt(v_hat) + ADAM_EPS) + ADAM_WD * w_ref[...])
        nm_ref[...] = nm
        nv_ref[...] = nv

    shape = jax.ShapeDtypeStruct(w.shape, F32)
    return pl.pallas_call(body, name=name, out_shape=[shape, shape, shape], compiler_params=_cp())(w, g, m, v)


def _sum_small(land, name):
    def body(l_ref, o_ref):
        acc = l_ref[0]
        for s in range(1, N_DEV):
            acc = acc + l_ref[s]
        o_ref[...] = acc

    return pl.pallas_call(body, name=name, out_shape=jax.ShapeDtypeStruct(land.shape[1:], F32), compiler_params=_cp())(land)


def _to_segments(a, dil):
    if dil == 1:
        return a
    t, w = a.shape
    return a.reshape(t // dil, dil, w).transpose(1, 0, 2).reshape(t, w)


def _from_segments(a, dil):
    if dil == 1:
        return a
    t, w = a.shape
    return a.reshape(dil, t // dil, w).transpose(1, 0, 2).reshape(t, w)


def _rope_lanes(positions):
    inv_freq = 1.0 / (ROPE_THETA ** (jnp.arange(0, HEAD_DIM, 2, dtype=F32) / HEAD_DIM))
    ang = positions.astype(F32)[:, None] * inv_freq
    cos, sin = jnp.cos(ang), jnp.sin(ang)
    return jnp.concatenate([cos, cos, cos, cos], axis=1), jnp.concatenate([-sin, sin, -sin, sin], axis=1)


def kernel(x, positions, norm_ffn1, w_gate1, w_up1, w_down1, norm_mix, w_in, a_sink, w_out, norm_ffn2, w_gate2, w_up2, w_down2, norm_final, loss_target, m_norm_ffn1, m_w_gate1, m_w_up1, m_w_down1, m_norm_mix, m_w_in, m_a_sink, m_w_out, m_norm_ffn2, m_w_gate2, m_w_up2, m_w_down2, m_norm_final, v_norm_ffn1, v_w_gate1, v_w_up1, v_w_down1, v_norm_mix, v_w_in, v_a_sink, v_w_out, v_norm_ffn2, v_w_gate2, v_w_up2, v_w_down2, v_norm_final):
    x = x[0]
    target = loss_target[0]
    t, d = x.shape
    tm = min(1024, t)
    tm_mix = min(512, t)
    tf = 256
    tq = min(512, t // 16)

    tt = min(2048, t)
    f_all = w_gate1.shape[2] * N_DEV
    n_steps = (t // tm) * (f_all // tf)

    def stacked(shards):
        return jnp.concatenate([s.astype(BF16) for s in shards], axis=0), [s.shape[0] for s in shards]

    packed1, rows1 = stacked([w_gate1[0].T, w_up1[0].T, w_down1[0]])
    packed2, rows2 = stacked([w_in[0].T, w_out[0], w_gate2[0].T, w_up2[0].T, w_down2[0]])
    wg1, wu1, wd1 = _run_alone(_gather_plan(packed1, rows1), "gather_ffn1_weights")

    cos, sin_signed = _rope_lanes(positions[0])
    sink = a_sink[0]
    no_sink = jnp.zeros_like(sink)

    (x1, h1, gp1, up1, hid1), (win, wout, wg2, wu2, wd2) = _ffn_fwd(
        x, norm_ffn1, wg1, wu1, wd1, "ffn1_fwd", tm, tf, _gather_plan(packed2, rows2), [0, (3 * n_steps) // 4, n_steps - 1])
    h_mix, aq, akx, avx, bq, bk, bv = _in_proj_fwd(x1, norm_mix, win, cos, sin_signed, "in_proj_fwd", tm_mix)
    a_out, a_lse = _attn_fwd(aq, akx, avx, sink, "attn_a_fwd", A_HALF_WINDOW, t, tq, True)
    b_seg, b_outs, b_lses = [], [], []
    for window, dil in B_PATTERNS:
        qs, ks, vs = (_to_segments(a, dil) for a in (bq, bk, bv))
        o_seg, lse_seg = _attn_fwd(qs, ks, vs, no_sink, f"attn_b{dil}_fwd", window // (2 * dil), t // dil, tq, False)
        b_seg.append((qs, ks, vs))
        b_outs.append(_from_segments(o_seg, dil))
        b_lses.append(_from_segments(lse_seg, dil))
    x2, b_out, b_lse = _merge_out_proj_fwd(x1, a_out, b_outs, b_lses, wout, "out_proj_fwd", tm_mix)
    (x3, h2, gp2, up2, hid2), _ = _ffn_fwd(x2, norm_ffn2, wg2, wu2, wd2, "ffn2_fwd", tm, tf)

    gfinal = norm_final.reshape(1, d)
    dx3, dg_final, sq, dout2 = _final_norm_loss(x3, gfinal, target, "final_norm_loss", tm)

    dgt2, dut2 = _ffn_bwd_hidden(dout2, gp2, up2, wd2, "ffn2_bwd_hidden", tm, tf)
    (dx2, dg_ffn2), _ = _ffn_bwd_input(dx3, x2, norm_ffn2, dgt2, dut2, wg2, wu2, "ffn2_bwd_input", tm_mix)
    (gwg2, gwu2), _ = _token_products([dgt2, dut2], h2, "ffn2_bwd_gate_up", tt, tf)
    (gwd2,), _ = _token_products([hid2], dout2, "ffn2_bwd_down", tt, tf)

    n_att = (A_Q_W // LANES) * (t // tq)
    n_prod = (f_all // tf) * (t // tt)
    da, db, gwout = _out_proj_bwd(dx2, a_out, b_out, wout, "out_proj_bwd", tm_mix)
    (daq, dakx, davx, dsink_parts), (land_ffn2,) = _attn_bwd(
        aq, akx, avx, a_out, da, a_lse, sink, "attn_a_bwd", A_HALF_WINDOW, t, tq, True,
        _exchange_plan([[gwg2, gwu2, gwd2]]), [0, n_att - 1])
    dbq = dbk = dbv = None
    for (window, dil), (qs, ks, vs) in zip(B_PATTERNS, b_seg):
        os_, dos, ls = (_to_segments(a, dil) for a in (b_out, db, b_lse))
        (dq_s, dk_s, dv_s, _), _ = _attn_bwd(qs, ks, vs, os_, dos, ls, no_sink, f"attn_b{dil}_bwd", window // (2 * dil), t // dil, tq, False)
        parts = [_from_segments(a, dil) for a in (dq_s, dk_s, dv_s)]
        dbq, dbk, dbv = parts if dbq is None else (dbq + parts[0], dbk + parts[1], dbv + parts[2])
    dx1, dg_mix, gwin, dout1 = _in_proj_bwd(dx2, x1, norm_mix, win, h_mix, cos, sin_signed, daq, dakx, davx, dbq, dbk, dbv, "in_proj_bwd", tm_mix)

    (gwd1,), (land_in, land_out) = _token_products(
        [hid1], dout1, "ffn1_bwd_down", tt, tf, _exchange_plan([[gwin], [gwout]]), [0, n_prod - 1])
    dgt1, dut1 = _ffn_bwd_hidden(dout1, gp1, up1, wd1, "ffn1_bwd_hidden", tm, tf)
    (gwg1, gwu1), (land_wd1,) = _token_products(
        [dgt1, dut1], h1, "ffn1_bwd_gate_up", tt, tf, _exchange_plan([[gwd1]]), [0, n_prod - 1])
    (grad_x, dg_ffn1), (land_wgu1,) = _ffn_bwd_input(
        dx1, x, norm_ffn1, dgt1, dut1, wg1, wu1, "ffn1_bwd_input", tm_mix, _exchange_plan([[gwg1, gwu1]]), [0, t // tm_mix - 1])

    dsink_pairs = jnp.sum(dsink_parts[:, :, 0, :], axis=1)
    dsink = jnp.stack([dsink_pairs[:, 0], dsink_pairs[:, HEAD_DIM]], axis=1).reshape(1, -1)
    small = jnp.concatenate([dg_ffn1, dg_mix, dg_ffn2, dg_final, jnp.pad(dsink, ((0, 0), (0, d - dsink.shape[1]))),
                             sq, jnp.zeros((2, d), F32)], axis=0)
    (land_small,) = _run_alone(_exchange_plan([[jnp.tile(small, (N_DEV, 1))]]), "gather_small_gradients")
    red_ffn2 = _sum_slots(land_ffn2, 3, "sum_ffn2_grads")
    red_wgu1 = _sum_slots(land_wgu1, 2, "sum_ffn1_gate_up_grads")
    red_wd1 = _sum_slots(land_wd1, 1, "sum_ffn1_down_grads")
    red_in = _sum_slots(land_in, 1, "sum_in_grads")
    red_out = _sum_slots(land_out, 1, "sum_out_grads")
    red_small = _sum_small(land_small, "sum_small_grads")

    rf = rows1[0]
    ffn_block = lambda j: [red_wgu1[:rf], red_wgu1[rf:], red_wd1, red_ffn2[:rf], red_ffn2[rf:2 * rf], red_ffn2[2 * rf:]][j]
    n_sink = a_sink.shape[1]
    grads = {
        "norm_ffn1": red_small[0:1], "w_gate1": ffn_block(0).T[None], "w_up1": ffn_block(1).T[None], "w_down1": ffn_block(2)[None],
        "norm_mix": red_small[1:2], "w_in": red_in.T[None], "a_sink": red_small[4:5, :n_sink], "w_out": red_out[None],
        "norm_ffn2": red_small[2:3], "w_gate2": ffn_block(3).T[None], "w_up2": ffn_block(4).T[None], "w_down2": ffn_block(5)[None],
        "norm_final": red_small[3],
    }
    params = {
        "norm_ffn1": (norm_ffn1, m_norm_ffn1, v_norm_ffn1), "w_gate1": (w_gate1, m_w_gate1, v_w_gate1),
        "w_up1": (w_up1, m_w_up1, v_w_up1), "w_down1": (w_down1, m_w_down1, v_w_down1),
        "norm_mix": (norm_mix, m_norm_mix, v_norm_mix), "w_in": (w_in, m_w_in, v_w_in),
        "a_sink": (a_sink, m_a_sink, v_a_sink), "w_out": (w_out, m_w_out, v_w_out),
        "norm_ffn2": (norm_ffn2, m_norm_ffn2, v_norm_ffn2), "w_gate2": (w_gate2, m_w_gate2, v_w_gate2),
        "w_up2": (w_up2, m_w_up2, v_w_up2), "w_down2": (w_down2, m_w_down2, v_w_down2),
        "norm_final": (norm_final, m_norm_final, v_norm_final),
    }
    loss = 0.5 * jnp.sum(red_small[5]) / d
    deltas, new_ms, new_vs = [], [], []
    for name, (w, m, v) in params.items():
        as_block = (lambda a: a.reshape(1, -1)) if w.ndim == 1 else (lambda a: a)
        dl, nm, nv = _adamw(as_block(w), as_block(grads[name]), as_block(m), as_block(v), f"adamw_{name}")
        deltas.append(dl.reshape(w.shape))
        new_ms.append(nm.reshape(w.shape))
        new_vs.append(nv.reshape(w.shape))
    grad_list = [grads[name] for name in params]
    return (loss, grad_x[None], *grad_list, *deltas, *new_ms, *new_vs)
```

```python
import functools
import itertools

import numpy as np
import jax
import jax.numpy as jnp
from jax import lax
from jax.experimental import pallas as pl
from jax.experimental.pallas import tpu as pltpu

F32 = jnp.float32
BF16 = jnp.bfloat16

N_DEV = 8
HEAD_DIM = 64
LANES = 128
A_Q_W, A_KV_W, B_W = 512, 128, 512
A_HALF_WINDOW = 128
B_PATTERNS = ((128, 1), (512, 4), (2048, 16))
ROPE_THETA = 10000.0
NORM_EPS = 1e-6
FFN_RES_WEIGHT = 0.5
QK_SCALE = HEAD_DIM ** -0.5
NEG = -1e30

ADAM_LR = 0.001
ADAM_B1 = 0.9
ADAM_B2 = 0.999
ADAM_EPS = 1e-08
ADAM_WD = 0.01
ADAM_STEP = 10

MESH_T = pl.DeviceIdType.MESH
VMEM_LIMIT = 56 * 1024 * 1024


def _cp(sem=None, vmem=VMEM_LIMIT):
    return pltpu.CompilerParams(dimension_semantics=sem, vmem_limit_bytes=vmem)


def _dot_nn(a, b):
    return jnp.dot(a, b, preferred_element_type=F32)


def _dot_nt(a, b):
    return lax.dot_general(a, b, (((1,), (1,)), ((), ())), preferred_element_type=F32)


def _dot_tn(a, b):
    return lax.dot_general(a, b, (((0,), (0,)), ((), ())), preferred_element_type=F32)


def _rstd(xv):
    return lax.rsqrt(jnp.mean(xv * xv, axis=-1, keepdims=True) + NORM_EPS)


def _norm_bwd(dh, xv, r, gn):
    gy = dh * gn
    c = jnp.sum(gy * xv, axis=-1, keepdims=True) * (1.0 / xv.shape[-1])
    dx = r * gy - xv * (r * r * r * c)
    dgn = jnp.sum(dh * (xv * r), axis=0, keepdims=True)
    return dx, dgn


def _accumulate(ref, val, first):
    @pl.when(first)
    def _():
        ref[...] = val

    @pl.when(jnp.logical_not(first))
    def _():
        ref[...] += val


def _mesh_pos():
    return lax.axis_index("x"), lax.axis_index("y"), lax.axis_index("c")


def _dev_index(d):
    return 4 * d[0] + 2 * d[1] + d[2]


class _Comm:
    def __init__(self, inputs, out_shape, scratch, phases):
        self.inputs, self.out_shape, self.scratch, self.phases = inputs, out_shape, scratch, phases

    def specs(self):
        any_spec = pl.BlockSpec(memory_space=pl.ANY)
        return [any_spec] * len(self.inputs), [any_spec] * len(self.out_shape)


def _run_alone(comm, name):
    n_in, n_out = len(comm.inputs), len(comm.out_shape)

    def body(*refs):
        for phase in comm.phases:
            phase(refs[:n_in], refs[n_in:n_in + n_out], refs[n_in + n_out:])

    in_specs, out_specs = comm.specs()
    return pl.pallas_call(body, name=name, out_shape=comm.out_shape, in_specs=in_specs, out_specs=out_specs,
                          scratch_shapes=comm.scratch)(*comm.inputs)


def _run_hosted(comm, at, step, ins, outs, scr):
    for phase, when in zip(comm.phases, at):
        @pl.when(step == when)
        def _(phase=phase):
            phase(ins, outs, scr)


def _split(refs, *counts):
    parts, o = [], 0
    for n in counts:
        parts.append(refs[o:o + n])
        o += n
    return parts + [refs[o:]]


def _gather_plan(packed, rows_list):
    n_w = len(rows_list)
    offs = [int(o) for o in np.cumsum([0] + list(rows_list[:-1]))]
    d = packed.shape[1]

    def tools(ins, outs, scr):
        p_ref = ins[0]
        send_sems, recv_sems, local_sem = scr
        x, y, c = _mesh_pos()
        me, sibling = (x, y, c), (x, y, 1 - c)
        chips = [(1 - x, y), (x, 1 - y), (1 - x, 1 - y)]

        def rows(w, dev):
            start = pl.multiple_of(_dev_index(dev) * rows_list[w], 16)
            return outs[w].at[pl.ds(start, rows_list[w]), :]

        def mine(w):
            return p_ref.at[pl.ds(offs[w], rows_list[w]), :]

        def copy(k, w, block, to, own):
            return pltpu.make_async_remote_copy(
                src_ref=mine(w) if own else rows(w, block), dst_ref=rows(w, block),
                send_sem=send_sems.at[k], recv_sem=recv_sems.at[k], device_id=to, device_id_type=MESH_T)

        def all_blocks(k):
            return pltpu.make_async_remote_copy(
                src_ref=p_ref, dst_ref=p_ref, send_sem=send_sems.at[k], recv_sem=recv_sems.at[k],
                device_id=me, device_id_type=MESH_T)

        return p_ref, local_sem, me, sibling, chips, c, rows, mine, copy, all_blocks

    def start(ins, outs, scr):
        _, local_sem, me, sibling, chips, c, rows, mine, copy, _ = tools(ins, outs, scr)
        for w in range(n_w):
            pltpu.make_async_copy(mine(w), rows(w, me), local_sem).start()
        for w in range(n_w):
            copy(0, w, me, sibling, True).start()
        for j, chip in enumerate(chips):
            for w in range(n_w):
                copy(1 + j, w, me, (*chip, c), True).start()

    def relay(ins, outs, scr):
        _, _, _, sibling, chips, c, _, _, copy, all_blocks = tools(ins, outs, scr)
        for j, chip in enumerate(chips):
            all_blocks(1 + j).wait_recv()
            for w in range(n_w):
                copy(4 + j, w, (*chip, c), sibling, False).start()

    def finish(ins, outs, scr):
        p_ref, local_sem, _, _, _, _, _, _, _, all_blocks = tools(ins, outs, scr)
        all_blocks(0).wait_recv()
        for j in range(3):
            all_blocks(4 + j).wait_recv()
        for k in range(7):
            all_blocks(k).wait_send()
        pltpu.make_async_copy(p_ref, p_ref, local_sem).wait()

    return _Comm(
        [packed], [jax.ShapeDtypeStruct((N_DEV * r, d), packed.dtype) for r in rows_list],
        [pltpu.SemaphoreType.DMA((7,)), pltpu.SemaphoreType.DMA((7,)), pltpu.SemaphoreType.DMA], [start, relay, finish])


def _exchange_plan(groups):
    flat = [a for g in groups for a in g]
    n_g = len(groups)
    sizes = [len(g) for g in groups]
    rows = [g[0].shape[0] // N_DEV for g in groups]
    first = [int(o) for o in np.cumsum([0] + sizes[:-1])]

    def start(srcs, lands, scr):
        send_sems, recv_sems, local_sems = scr
        x, y, c = _mesh_pos()
        me = (x, y, c)
        me_idx = _dev_index(me)

        def block(g, i, dev):
            start_row = pl.multiple_of(_dev_index(dev) * rows[g], 8)
            return srcs[first[g] + i].at[pl.ds(start_row, rows[g]), :]

        def slot(g, i):
            return lands[g].at[me_idx, pl.ds(i * rows[g], rows[g]), :]

        for g in range(n_g):
            for i in range(sizes[g]):
                pltpu.make_async_copy(block(g, i, me), slot(g, i), local_sems.at[g]).start()
        flips = [f for f in itertools.product((0, 1), repeat=3) if any(f)]
        for k, (fx, fy, fc) in enumerate(flips):
            peer = (1 - x if fx else x, 1 - y if fy else y, 1 - c if fc else c)
            for g in range(n_g):
                for i in range(sizes[g]):
                    pltpu.make_async_remote_copy(
                        src_ref=block(g, i, peer), dst_ref=slot(g, i), send_sem=send_sems.at[g, k],
                        recv_sem=recv_sems.at[g, k], device_id=peer, device_id_type=MESH_T).start()

    def finish(srcs, lands, scr):
        send_sems, recv_sems, local_sems = scr
        me = _mesh_pos()
        for k in range(7):
            for g in range(n_g):
                pltpu.make_async_remote_copy(
                    src_ref=lands[g].at[0], dst_ref=lands[g].at[0], send_sem=send_sems.at[g, k],
                    recv_sem=recv_sems.at[g, k], device_id=me, device_id_type=MESH_T).wait()
        for g in range(n_g):
            pltpu.make_async_copy(lands[g].at[0], lands[g].at[0], local_sems.at[g]).wait()

    return _Comm(
        flat, [jax.ShapeDtypeStruct((N_DEV, sizes[g] * rows[g], groups[g][0].shape[1]), groups[g][0].dtype) for g in range(n_g)],
        [pltpu.SemaphoreType.DMA((n_g, 7)), pltpu.SemaphoreType.DMA((n_g, 7)), pltpu.SemaphoreType.DMA((n_g,))],
        [start, finish])


def _sum_slots(land, n_blocks, name):
    _, total, d = land.shape
    rows = total // n_blocks

    def body(l_ref, o_ref):
        s = pl.program_id(1)
        _accumulate(o_ref, l_ref[...].astype(F32), s == 0)

    return pl.pallas_call(
        body, name=name, grid=(n_blocks, N_DEV),
        out_shape=jax.ShapeDtypeStruct((total, d), F32),
        in_specs=[pl.BlockSpec((None, rows, d), lambda w, s: (s, w, 0))],
        out_specs=pl.BlockSpec((rows, d), lambda w, s: (w, 0)),
        compiler_params=_cp(("arbitrary", "arbitrary")),
    )(land)


def _comm_parts(comm):
    if comm is None:
        return 0, 0, [], [], [], [], []
    in_specs, out_specs = comm.specs()
    return len(comm.inputs), len(comm.out_shape), in_specs, out_specs, comm.out_shape, comm.scratch, comm.inputs


def _ffn_fwd(x, gn, wg_t, wu_t, wd, name, tm, tf, comm=None, comm_at=None):
    t, d = x.shape
    f_all = wg_t.shape[0]
    n_f = f_all // tf
    n_ci, n_co, c_in_specs, c_out_specs, c_shapes, c_scratch, c_inputs = _comm_parts(comm)

    def body(*refs):
        ((x_ref, gn_ref, wg_ref, wu_ref, wd_ref), c_in, (y_ref, h_ref, gp_ref, up_ref, hid_ref), c_out,
         (h_s, hid_s), c_scr) = _split(refs, 5, n_ci, 5, n_co, 2)
        f = pl.program_id(1)
        if comm:
            _run_hosted(comm, comm_at, pl.program_id(0) * n_f + f, c_in, c_out, c_scr)

        @pl.when(f == 0)
        def _():
            xv = x_ref[...]
            h = ((xv * _rstd(xv)) * gn_ref[...]).astype(BF16)
            h_s[...] = h
            h_ref[...] = h

        h = h_s[...]
        g = _dot_nt(h, wg_ref[...])
        u = _dot_nt(h, wu_ref[...])
        gp_ref[...] = g.astype(BF16)
        up_ref[...] = u.astype(BF16)
        hid = ((g * jax.nn.sigmoid(g)) * u).astype(BF16)
        hid_ref[...] = hid
        for f0 in range(n_f):
            @pl.when(f == f0)
            def _(f0=f0):
                hid_s[:, f0 * tf:(f0 + 1) * tf] = hid

        @pl.when(f == n_f - 1)
        def _():
            y_ref[...] = x_ref[...] + FFN_RES_WEIGHT * _dot_nn(hid_s[...], wd_ref[...])

    tok = pl.BlockSpec((tm, d), lambda i, f: (i, 0))
    wblk = pl.BlockSpec((tf, d), lambda i, f: (f, 0))
    whole = pl.BlockSpec((f_all, d), lambda i, f: (0, 0))
    act = pl.BlockSpec((tm, tf), lambda i, f: (i, f))
    act_shape = jax.ShapeDtypeStruct((t, f_all), BF16)
    res = pl.pallas_call(
        body, name=name, grid=(t // tm, n_f),
        out_shape=[jax.ShapeDtypeStruct((t, d), F32), jax.ShapeDtypeStruct((t, d), BF16), act_shape, act_shape, act_shape] + c_shapes,
        in_specs=[tok, pl.BlockSpec((1, d), lambda i, f: (0, 0)), wblk, wblk, whole] + c_in_specs,
        out_specs=[tok, tok, act, act, act] + c_out_specs,
        scratch_shapes=[pltpu.VMEM((tm, d), BF16), pltpu.VMEM((tm, f_all), BF16)] + c_scratch,
        compiler_params=_cp(("arbitrary", "arbitrary")),
    )(x, gn, wg_t, wu_t, wd, *c_inputs)
    return res[:5], res[5:]


def _ffn_bwd_hidden(dout, gp, up, wd, name, tm, tf):
    t, d = dout.shape
    f_all = wd.shape[0]

    def body(dout_ref, gp_ref, up_ref, wd_ref, dg_ref, du_ref):
        dhid = _dot_nt(dout_ref[...], wd_ref[...])
        g = gp_ref[...].astype(F32)
        u = up_ref[...].astype(F32)
        sg = jax.nn.sigmoid(g)
        dg_ref[...] = ((dhid * u) * (sg * (1.0 + g * (1.0 - sg)))).astype(BF16)
        du_ref[...] = (dhid * (g * sg)).astype(BF16)

    tok = pl.BlockSpec((tm, d), lambda i, f: (i, 0))
    wblk = pl.BlockSpec((tf, d), lambda i, f: (f, 0))
    act = pl.BlockSpec((tm, tf), lambda i, f: (i, f))
    act_shape = jax.ShapeDtypeStruct((t, f_all), BF16)
    return pl.pallas_call(
        body, name=name, grid=(t // tm, f_all // tf),
        out_shape=[act_shape, act_shape], in_specs=[tok, act, act, wblk], out_specs=[act, act],
        compiler_params=_cp(("arbitrary", "arbitrary")),
    )(dout, gp, up, wd)


def _ffn_bwd_input(dy, x, gn, dg, du, wg_t, wu_t, name, tm, comm=None, comm_at=None):
    t, d = x.shape
    f_all = wg_t.shape[0]
    n_ci, n_co, c_in_specs, c_out_specs, c_shapes, c_scratch, c_inputs = _comm_parts(comm)

    def body(*refs):
        (dy_ref, x_ref, gn_ref, dg_ref, du_ref, wg_ref, wu_ref), c_in, (dx_ref, dgn_ref), c_out, c_scr = _split(
            refs, 7, n_ci, 2, n_co)
        i = pl.program_id(0)
        if comm:
            _run_hosted(comm, comm_at, i, c_in, c_out, c_scr)
        dh = _dot_nn(dg_ref[...], wg_ref[...]) + _dot_nn(du_ref[...], wu_ref[...])
        xv = x_ref[...]
        dxn, dgn = _norm_bwd(dh, xv, _rstd(xv), gn_ref[...])
        dx_ref[...] = dy_ref[...] + dxn
        _accumulate(dgn_ref, dgn, i == 0)

    tok = pl.BlockSpec((tm, d), lambda i: (i, 0))
    row = pl.BlockSpec((1, d), lambda i: (0, 0))
    act = pl.BlockSpec((tm, f_all), lambda i: (i, 0))
    whole = pl.BlockSpec((f_all, d), lambda i: (0, 0))
    res = pl.pallas_call(
        body, name=name, grid=(t // tm,),
        out_shape=[jax.ShapeDtypeStruct((t, d), F32), jax.ShapeDtypeStruct((1, d), F32)] + c_shapes,
        in_specs=[tok, tok, row, act, act, whole, whole] + c_in_specs,
        out_specs=[tok, row] + c_out_specs,
        scratch_shapes=c_scratch,
        compiler_params=_cp(("arbitrary",)),
    )(dy, x, gn, dg, du, wg_t, wu_t, *c_inputs)
    return res[:2], res[2:]


def _token_products(lhs_list, rhs, name, tt, tf, comm=None, comm_at=None):
    n_l = len(lhs_list)
    t, f_all = lhs_list[0].shape
    d = rhs.shape[1]
    n_t = t // tt
    n_ci, n_co, c_in_specs, c_out_specs, c_shapes, c_scratch, c_inputs = _comm_parts(comm)

    def body(*refs):
        lhs_refs, (rhs_ref,), c_in, out_refs, c_out, accs, c_scr = _split(refs, n_l, 1, n_ci, n_l, n_co, n_l)
        s = pl.program_id(1)
        if comm:
            _run_hosted(comm, comm_at, pl.program_id(0) * n_t + s, c_in, c_out, c_scr)
        rv = rhs_ref[...]
        for l_ref, acc in zip(lhs_refs, accs):
            _accumulate(acc, _dot_tn(l_ref[...], rv), s == 0)

        @pl.when(s == n_t - 1)
        def _():
            for o_ref, acc in zip(out_refs, accs):
                o_ref[...] = acc[...].astype(BF16)

    act = pl.BlockSpec((tt, tf), lambda f, s: (s, f))
    tok = pl.BlockSpec((tt, d), lambda f, s: (s, 0))
    wblk = pl.BlockSpec((tf, d), lambda f, s: (f, 0))
    res = pl.pallas_call(
        body, name=name, grid=(f_all // tf, n_t),
        out_shape=[jax.ShapeDtypeStruct((f_all, d), BF16)] * n_l + c_shapes,
        in_specs=[act] * n_l + [tok] + c_in_specs, out_specs=[wblk] * n_l + c_out_specs,
        scratch_shapes=[pltpu.VMEM((tf, d), F32)] * n_l + c_scratch,
        compiler_params=_cp(("arbitrary", "arbitrary")),
    )(*lhs_list, rhs, *c_inputs)
    return res[:n_l], res[n_l:]


def _swap_halves(t):
    w = t.shape[-1]
    lane = lax.broadcasted_iota(jnp.int32, (1, w), 1)
    return jnp.where((lane % HEAD_DIM) < HEAD_DIM // 2, pltpu.roll(t, w - HEAD_DIM // 2, 1), pltpu.roll(t, HEAD_DIM // 2, 1))


def _rope(t, cos, sin_signed):
    reps = t.shape[-1] // LANES
    return t * jnp.tile(cos, (1, reps)) + _swap_halves(t) * jnp.tile(sin_signed, (1, reps))


def _rope_bwd(dt, cos, sin_signed):
    reps = dt.shape[-1] // LANES
    return dt * jnp.tile(cos, (1, reps)) + _swap_halves(dt * jnp.tile(sin_signed, (1, reps)))


def _in_proj_fwd(x, gn, win_t, cos, sin_signed, name, tm):
    t, d = x.shape
    in_w = win_t.shape[0]

    def body(x_ref, gn_ref, w_ref, cos_ref, sin_ref, h_ref, aq_ref, akx_ref, avx_ref, bq_ref, bk_ref, bv_ref):
        xv = x_ref[...]
        h = ((xv * _rstd(xv)) * gn_ref[...]).astype(BF16)
        h_ref[...] = h
        p = _dot_nt(h, w_ref[...])
        cs, sn = cos_ref[...], sin_ref[...]
        o = 0
        aq_ref[...] = (_rope(p[:, o:o + A_Q_W], cs, sn) * QK_SCALE).astype(BF16)
        o += A_Q_W
        ak = _rope(p[:, o:o + A_KV_W], cs, sn)
        o += A_KV_W
        av = p[:, o:o + A_KV_W]
        o += A_KV_W
        low = lax.broadcasted_iota(jnp.int32, (1, LANES), 1) < HEAD_DIM
        for src, dst in ((ak, akx_ref), (av, avx_ref)):
            other = pltpu.roll(src, HEAD_DIM, 1)
            dst[0] = jnp.where(low, src, other).astype(BF16)
            dst[1] = jnp.where(low, other, src).astype(BF16)
        bq_ref[...] = (_rope(p[:, o:o + B_W], cs, sn) * QK_SCALE).astype(BF16)
        o += B_W
        bk_ref[...] = _rope(p[:, o:o + B_W], cs, sn).astype(BF16)
        o += B_W
        bv_ref[...] = p[:, o:o + B_W].astype(BF16)

    tok = lambda w: pl.BlockSpec((tm, w), lambda i: (i, 0))
    kvx = pl.BlockSpec((2, tm, LANES), lambda i: (0, i, 0))
    sd = lambda *s: jax.ShapeDtypeStruct(s, BF16)
    return pl.pallas_call(
        body, name=name, grid=(t // tm,),
        out_shape=[sd(t, d), sd(t, A_Q_W), sd(2, t, LANES), sd(2, t, LANES), sd(t, B_W), sd(t, B_W), sd(t, B_W)],
        in_specs=[tok(d), pl.BlockSpec((1, d), lambda i: (0, 0)), pl.BlockSpec((in_w, d), lambda i: (0, 0)),
                  tok(LANES), tok(LANES)],
        out_specs=[tok(d), tok(A_Q_W), kvx, kvx, tok(B_W), tok(B_W), tok(B_W)],
        compiler_params=_cp(("arbitrary",)),
    )(x, gn, win_t, cos, sin_signed)


def _in_proj_bwd(dres, x, gn, win_t, h, cos, sin_signed, daq, dakx, davx, dbq, dbk, dbv, name, tm):
    t, d = x.shape
    in_w = win_t.shape[0]
    n_t = t // tm
    chunk = in_w // 3

    def body(dres_ref, x_ref, gn_ref, w_ref, h_ref, cos_ref, sin_ref, daq_ref, dakx_ref, davx_ref, dbq_ref, dbk_ref,
             dbv_ref, dx_ref, dgn_ref, gw_ref, half_ref, dp_s, gw_s):
        i = pl.program_id(0)
        cs, sn = cos_ref[...], sin_ref[...]
        low = lax.broadcasted_iota(jnp.int32, (1, LANES), 1) < HEAD_DIM

        def fold(ref):
            a, b = ref[0], ref[1]
            return jnp.where(low, a + pltpu.roll(a, HEAD_DIM, 1), b + pltpu.roll(b, HEAD_DIM, 1))

        o = 0
        dp_s[:, o:o + A_Q_W] = _rope_bwd(daq_ref[...] * QK_SCALE, cs, sn).astype(BF16)
        o += A_Q_W
        dp_s[:, o:o + A_KV_W] = _rope_bwd(fold(dakx_ref), cs, sn).astype(BF16)
        o += A_KV_W
        dp_s[:, o:o + A_KV_W] = fold(davx_ref).astype(BF16)
        o += A_KV_W
        dp_s[:, o:o + B_W] = _rope_bwd(dbq_ref[...] * QK_SCALE, cs, sn).astype(BF16)
        o += B_W
        dp_s[:, o:o + B_W] = _rope_bwd(dbk_ref[...], cs, sn).astype(BF16)
        o += B_W
        dp_s[:, o:o + B_W] = dbv_ref[...].astype(BF16)
        dh = _dot_nn(dp_s[...], w_ref[...])
        hv = h_ref[...]
        for c0 in range(0, in_w, chunk):
            _accumulate(gw_s.at[pl.ds(c0, chunk), :], _dot_tn(dp_s[:, c0:c0 + chunk], hv), i == 0)
        xv = x_ref[...]
        dxn, dgn = _norm_bwd(dh, xv, _rstd(xv), gn_ref[...])
        dx = dres_ref[...] + dxn
        dx_ref[...] = dx
        half_ref[...] = (FFN_RES_WEIGHT * dx).astype(BF16)
        _accumulate(dgn_ref, dgn, i == 0)

        @pl.when(i == n_t - 1)
        def _():
            gw_ref[...] = gw_s[...].astype(BF16)

    tok = lambda w: pl.BlockSpec((tm, w), lambda i: (i, 0))
    row = pl.BlockSpec((1, d), lambda i: (0, 0))
    whole = pl.BlockSpec((in_w, d), lambda i: (0, 0))
    kvx = pl.BlockSpec((2, tm, LANES), lambda i: (0, i, 0))
    return pl.pallas_call(
        body, name=name, grid=(n_t,),
        out_shape=[jax.ShapeDtypeStruct((t, d), F32), jax.ShapeDtypeStruct((1, d), F32),
                   jax.ShapeDtypeStruct((in_w, d), BF16), jax.ShapeDtypeStruct((t, d), BF16)],
        in_specs=[tok(d), tok(d), row, whole, tok(d), tok(LANES), tok(LANES), tok(A_Q_W), kvx, kvx,
                  tok(B_W), tok(B_W), tok(B_W)],
        out_specs=[tok(d), row, whole, tok(d)],
        scratch_shapes=[pltpu.VMEM((tm, in_w), BF16), pltpu.VMEM((in_w, d), F32)],
        compiler_params=_cp(("arbitrary",)),
    )(dres, x, gn, win_t, h, cos, sin_signed, daq, dakx, davx, dbq, dbk, dbv)


def _merge_out_proj_fwd(x, a_out, outs, lses, wout, name, tm):
    t, d = x.shape

    def body(x_ref, a_ref, o0, o1, o2, l0, l1, l2, w_ref, y_ref, b_ref, lt_ref):
        ls = [l0[...], l1[...], l2[...]]
        mx = jnp.maximum(jnp.maximum(ls[0], ls[1]), ls[2])
        es = [jnp.exp(l - mx) for l in ls]
        den = es[0] + es[1] + es[2]
        b = (es[0] * o0[...].astype(F32) + es[1] * o1[...].astype(F32) + es[2] * o2[...].astype(F32)) / den
        bb = b.astype(BF16)
        b_ref[...] = bb
        lt_ref[...] = mx + jnp.log(den)
        y_ref[...] = x_ref[...] + _dot_nn(a_ref[...], w_ref[0:A_Q_W, :]) + _dot_nn(bb, w_ref[A_Q_W:A_Q_W + B_W, :])

    tok = lambda w: pl.BlockSpec((tm, w), lambda i: (i, 0))
    return pl.pallas_call(
        body, name=name, grid=(t // tm,),
        out_shape=[jax.ShapeDtypeStruct((t, d), F32), jax.ShapeDtypeStruct((t, B_W), BF16),
                   jax.ShapeDtypeStruct((t, B_W), F32)],
        in_specs=[tok(d), tok(A_Q_W)] + [tok(B_W)] * 6 + [pl.BlockSpec(wout.shape, lambda i: (0, 0))],
        out_specs=[tok(d), tok(B_W), tok(B_W)],
        compiler_params=_cp(("arbitrary",)),
    )(x, a_out, *outs, *lses, wout)


def _out_proj_bwd(dy, a_out, b_out, wout, name, tm):
    t, d = dy.shape
    n_t = t // tm

    def body(dy_ref, a_ref, b_ref, w_ref, da_ref, db_ref, gw_ref, gw_s):
        i = pl.program_id(0)
        dyb = dy_ref[...].astype(BF16)
        da_ref[...] = _dot_nt(dyb, w_ref[0:A_Q_W, :]).astype(BF16)
        db_ref[...] = _dot_nt(dyb, w_ref[A_Q_W:A_Q_W + B_W, :]).astype(BF16)
        ga = _dot_tn(a_ref[...], dyb)
        gb = _dot_tn(b_ref[...], dyb)

        @pl.when(i == 0)
        def _():
            gw_s[0:A_Q_W, :] = ga
            gw_s[A_Q_W:A_Q_W + B_W, :] = gb

        @pl.when(i > 0)
        def _():
            gw_s[0:A_Q_W, :] += ga
            gw_s[A_Q_W:A_Q_W + B_W, :] += gb

        @pl.when(i == n_t - 1)
        def _():
            gw_ref[...] = gw_s[...].astype(BF16)

    tok = lambda w: pl.BlockSpec((tm, w), lambda i: (i, 0))
    whole = pl.BlockSpec(wout.shape, lambda i: (0, 0))
    return pl.pallas_call(
        body, name=name, grid=(n_t,),
        out_shape=[jax.ShapeDtypeStruct((t, A_Q_W), BF16), jax.ShapeDtypeStruct((t, B_W), BF16),
                   jax.ShapeDtypeStruct(wout.shape, BF16)],
        in_specs=[tok(d), tok(A_Q_W), tok(B_W), whole], out_specs=[tok(A_Q_W), tok(B_W), whole],
        scratch_shapes=[pltpu.VMEM(wout.shape, F32)],
        compiler_params=_cp(("arbitrary",)),
    )(dy, a_out, b_out, wout)


SUB_ROWS = 64


def _sub_band(r0, hw, win, seg_lo, seg_len, t, rel, col):
    ks = pl.multiple_of(jnp.clip(r0 - hw, 0, t - win), SUB_ROWS)
    kpos = col + ks
    valid = (jnp.abs(rel + (ks - r0)) <= hw) & (kpos >= seg_lo) & (kpos < seg_lo + seg_len)
    return ks, valid


def _split_heads(v, low):
    zero = jnp.zeros_like(v)
    return jnp.concatenate([jnp.where(low, v, zero), jnp.where(low, zero, v)], axis=0)


def _kv_spec(kv, t):
    if kv.ndim == 3:
        return pl.BlockSpec((None, t, LANES), lambda p, i: (p // 2, 0, 0))
    return pl.BlockSpec((t, LANES), lambda p, i: (0, p))


def _attn_fwd(q, k, v, sink, name, hw, seg_len, tq, has_sink):
    t, width = q.shape
    sb = SUB_ROWS
    win = 2 * hw + LANES

    def body(sink_ref, q_ref, k_ref, v_ref, o_ref, lse_ref):
        p, i = pl.program_id(0), pl.program_id(1)
        q0 = i * tq
        seg_lo = (q0 // seg_len) * seg_len
        low = lax.broadcasted_iota(jnp.int32, (1, LANES), 1) < HEAD_DIM
        rel = lax.broadcasted_iota(jnp.int32, (sb, win), 1) - lax.broadcasted_iota(jnp.int32, (sb, win), 0)
        col = lax.broadcasted_iota(jnp.int32, (1, win), 1)
        for j in range(tq // sb):
            rows = pl.ds(j * sb, sb)
            ks, valid = _sub_band(q0 + j * sb, hw, win, seg_lo, seg_len, t, rel, col)
            kw = k_ref[pl.ds(ks, win), :]
            vw = v_ref[pl.ds(ks, win), :]
            s = _dot_nt(_split_heads(q_ref[rows, :], low), kw)
            es, inv, lses = [], [], []
            for a in range(2):
                sa = jnp.where(valid, s[a * sb:(a + 1) * sb], NEG)
                m = jnp.max(sa, axis=1, keepdims=True)
                if has_sink:
                    sk = sink_ref[2 * p + a]
                    m = jnp.maximum(m, sk)
                e = jnp.exp(sa - m)
                den = jnp.sum(e, axis=1, keepdims=True)
                if has_sink:
                    den = den + jnp.exp(sk - m)
                es.append(e.astype(BF16))
                inv.append(1.0 / den)
                lses.append(m + jnp.log(den))
            pv = _dot_nn(jnp.concatenate(es, axis=0), vw)
            o_ref[rows, :] = jnp.where(low, pv[0:sb] * inv[0], pv[sb:2 * sb] * inv[1]).astype(BF16)
            lse_ref[rows, :] = jnp.where(low, lses[0], lses[1])

    tile = pl.BlockSpec((tq, LANES), lambda p, i: (i, p))
    return pl.pallas_call(
        body, name=name, grid=(width // LANES, t // tq),
        out_shape=[jax.ShapeDtypeStruct((t, width), BF16), jax.ShapeDtypeStruct((t, width), F32)],
        in_specs=[pl.BlockSpec(memory_space=pltpu.SMEM), tile, _kv_spec(k, t), _kv_spec(v, t)],
        out_specs=[tile, tile],
        compiler_params=_cp(("arbitrary", "arbitrary")),
    )(sink, q, k, v)


def _attn_bwd(q, k, v, o, do, lse, sink, name, hw, seg_len, tq, has_sink, comm=None, comm_at=None):
    t, width = q.shape
    sb = SUB_ROWS
    win = 2 * hw + LANES
    n_q = t // tq
    shared_kv = k.ndim == 3
    n_ci, n_co, c_in_specs, c_out_specs, c_shapes, c_scratch, c_inputs = _comm_parts(comm)

    def body(*refs):
        ((sink_ref, q_ref, k_ref, v_ref, o_ref, do_ref, lse_ref), c_in, (dq_ref, dk_ref, dv_ref, ds_ref), c_out, c_scr) = _split(
            refs, 7, n_ci, 4, n_co)
        p, i = pl.program_id(0), pl.program_id(1)
        if comm:
            _run_hosted(comm, comm_at, p * n_q + i, c_in, c_out, c_scr)
        fresh = (i == 0) & (p % 2 == 0) if shared_kv else i == 0

        @pl.when(fresh)
        def _():
            dk_ref[...] = jnp.zeros_like(dk_ref)
            dv_ref[...] = jnp.zeros_like(dv_ref)

        q0 = i * tq
        seg_lo = (q0 // seg_len) * seg_len
        low = lax.broadcasted_iota(jnp.int32, (1, LANES), 1) < HEAD_DIM
        rel = lax.broadcasted_iota(jnp.int32, (sb, win), 1) - lax.broadcasted_iota(jnp.int32, (sb, win), 0)
        col = lax.broadcasted_iota(jnp.int32, (1, win), 1)
        dsink = [jnp.zeros((1, 1), F32), jnp.zeros((1, 1), F32)]
        for j in range(tq // sb):
            rows = pl.ds(j * sb, sb)
            ks, valid = _sub_band(q0 + j * sb, hw, win, seg_lo, seg_len, t, rel, col)
            kw = k_ref[pl.ds(ks, win), :]
            vw = v_ref[pl.ds(ks, win), :]
            dov = do_ref[rows, :]
            prod = dov.astype(F32) * o_ref[rows, :].astype(F32)
            lse_t = lse_ref[rows, :]
            q2 = _split_heads(q_ref[rows, :], low)
            do2 = _split_heads(dov, low)
            s = _dot_nt(q2, kw)
            dpr = _dot_nt(do2, vw)
            prs, dss = [], []
            for a in range(2):
                mine = low if a == 0 else jnp.logical_not(low)
                lse_a = jnp.max(jnp.where(mine, lse_t, -jnp.inf), axis=1, keepdims=True)
                delta = jnp.sum(jnp.where(mine, prod, 0.0), axis=1, keepdims=True)
                pr = jnp.exp(jnp.where(valid, s[a * sb:(a + 1) * sb], NEG) - lse_a)
                prs.append(pr.astype(BF16))
                dss.append((pr * (dpr[a * sb:(a + 1) * sb] - delta)).astype(BF16))
                if has_sink:
                    dsink[a] = dsink[a] - jnp.sum(jnp.exp(sink_ref[2 * p + a] - lse_a) * delta, axis=0, keepdims=True)
            ds2 = jnp.concatenate(dss, axis=0)
            dv_ref[pl.ds(ks, win), :] += _dot_tn(jnp.concatenate(prs, axis=0), do2)
            dk_ref[pl.ds(ks, win), :] += _dot_tn(ds2, q2)
            dq2 = _dot_nn(ds2, kw)
            dq_ref[rows, :] = jnp.where(low, dq2[0:sb], dq2[sb:2 * sb])
        ds_ref[...] = jnp.broadcast_to(jnp.where(low, dsink[0], dsink[1]), ds_ref.shape)

    tile = pl.BlockSpec((tq, LANES), lambda p, i: (i, p))
    kv_shape = jax.ShapeDtypeStruct(k.shape, F32)
    res = pl.pallas_call(
        body, name=name, grid=(width // LANES, n_q),
        out_shape=[jax.ShapeDtypeStruct((t, width), F32), kv_shape, kv_shape,
                   jax.ShapeDtypeStruct((width // LANES, n_q, 8, LANES), F32)] + c_shapes,
        in_specs=[pl.BlockSpec(memory_space=pltpu.SMEM), tile, _kv_spec(k, t), _kv_spec(v, t), tile, tile, tile] + c_in_specs,
        out_specs=[tile, _kv_spec(k, t), _kv_spec(v, t), pl.BlockSpec((None, None, 8, LANES), lambda p, i: (p, i, 0, 0))] + c_out_specs,
        scratch_shapes=c_scratch,
        compiler_params=_cp(("arbitrary", "arbitrary")),
    )(sink, q, k, v, o, do, lse, *c_inputs)
    return res[:4], res[4:]


def _final_norm_loss(x, gn, target, name, tm):
    t, d = x.shape

    def body(x_ref, gn_ref, tg_ref, dx_ref, dgn_ref, sq_ref, half_ref):
        i = pl.program_id(0)
        xv, gnv = x_ref[...], gn_ref[...]
        r = _rstd(xv)
        diff = (xv * r) * gnv - tg_ref[...]
        dxn, dgn = _norm_bwd(diff * (1.0 / d), xv, r, gnv)
        dx_ref[...] = dxn
        half_ref[...] = (FFN_RES_WEIGHT * dxn).astype(BF16)
        _accumulate(dgn_ref, dgn, i == 0)
        _accumulate(sq_ref, jnp.sum(diff * diff, axis=0, keepdims=True), i == 0)

    tok = pl.BlockSpec((tm, d), lambda i: (i, 0))
    row = pl.BlockSpec((1, d), lambda i: (0, 0))
    return pl.pallas_call(
        body, name=name, grid=(t // tm,),
        out_shape=[jax.ShapeDtypeStruct((t, d), F32), jax.ShapeDtypeStruct((1, d), F32), jax.ShapeDtypeStruct((1, d), F32),
                   jax.ShapeDtypeStruct((t, d), BF16)],
        in_specs=[tok, row, tok], out_specs=[tok, row, row, tok],
        compiler_params=_cp(("arbitrary",)),
    )(x, gn, target)


def _adamw(w, g, m, v, name):
    def body(w_ref, g_ref, m_ref, v_ref, d_ref, nm_ref, nv_ref):
        gv = g_ref[...]
        nm = ADAM_B1 * m_ref[...] + (1.0 - ADAM_B1) * gv
        nv = ADAM_B2 * v_ref[...] + (1.0 - ADAM_B2) * (gv * gv)
        m_hat = nm / (1.0 - ADAM_B1 ** ADAM_STEP)
        v_hat = nv / (1.0 - ADAM_B2 ** ADAM_STEP)
        d_ref[...] = -ADAM_LR * (m_hat / (jnp.sqrt(v_hat) + ADAM_EPS) + ADAM_WD * w_ref[...])
        nm_ref[...] = nm
        nv_ref[...] = nv

    shape = jax.ShapeDtypeStruct(w.shape, F32)
    return pl.pallas_call(body, name=name, out_shape=[shape, shape, shape], compiler_params=_cp())(w, g, m, v)


def _sum_small(land, name):
    def body(l_ref, o_ref):
        acc = l_ref[0]
        for s in range(1, N_DEV):
            acc = acc + l_ref[s]
        o_ref[...] = acc

    return pl.pallas_call(body, name=name, out_shape=jax.ShapeDtypeStruct(land.shape[1:], F32), compiler_params=_cp())(land)


def _to_segments(a, dil):
    if dil == 1:
        return a
    t, w = a.shape
    return a.reshape(t // dil, dil, w).transpose(1, 0, 2).reshape(t, w)


def _from_segments(a, dil):
    if dil == 1:
        return a
    t, w = a.shape
    return a.reshape(dil, t // dil, w).transpose(1, 0, 2).reshape(t, w)


def _rope_lanes(positions):
    inv_freq = 1.0 / (ROPE_THETA ** (jnp.arange(0, HEAD_DIM, 2, dtype=F32) / HEAD_DIM))
    ang = positions.astype(F32)[:, None] * inv_freq
    cos, sin = jnp.cos(ang), jnp.sin(ang)
    return jnp.concatenate([cos, cos, cos, cos], axis=1), jnp.concatenate([-sin, sin, -sin, sin], axis=1)


def kernel(x, positions, norm_ffn1, w_gate1, w_up1, w_down1, norm_mix, w_in, a_sink, w_out, norm_ffn2, w_gate2, w_up2, w_down2, norm_final, loss_target, m_norm_ffn1, m_w_gate1, m_w_up1, m_w_down1, m_norm_mix, m_w_in, m_a_sink, m_w_out, m_norm_ffn2, m_w_gate2, m_w_up2, m_w_down2, m_norm_final, v_norm_ffn1, v_w_gate1, v_w_up1, v_w_down1, v_norm_mix, v_w_in, v_a_sink, v_w_out, v_norm_ffn2, v_w_gate2, v_w_up2, v_w_down2, v_norm_final):
    x = x[0]
    target = loss_target[0]
    t, d = x.shape
    tm = min(1024, t)
    tm_mix = min(512, t)
    tf = 256
    tq = min(512, t // 16)

    tt = min(1024, t)
    f_all = w_gate1.shape[2] * N_DEV
    tfp = f_all // 2
    n_steps = (t // tm) * (f_all // tf)

    def stacked(shards):
        return jnp.concatenate([s.astype(BF16) for s in shards], axis=0), [s.shape[0] for s in shards]

    packed1, rows1 = stacked([w_gate1[0].T, w_up1[0].T, w_down1[0]])
    packed2, rows2 = stacked([w_in[0].T, w_out[0], w_gate2[0].T, w_up2[0].T, w_down2[0]])
    wg1, wu1, wd1 = _run_alone(_gather_plan(packed1, rows1), "gather_ffn1_weights")

    cos, sin_signed = _rope_lanes(positions[0])
    sink = a_sink[0]
    no_sink = jnp.zeros_like(sink)

    (x1, h1, gp1, up1, hid1), (win, wout, wg2, wu2, wd2) = _ffn_fwd(
        x, norm_ffn1, wg1, wu1, wd1, "ffn1_fwd", tm, tf, _gather_plan(packed2, rows2), [0, (3 * n_steps) // 4, n_steps - 1])
    h_mix, aq, akx, avx, bq, bk, bv = _in_proj_fwd(x1, norm_mix, win, cos, sin_signed, "in_proj_fwd", tm_mix)
    a_out, a_lse = _attn_fwd(aq, akx, avx, sink, "attn_a_fwd", A_HALF_WINDOW, t, tq, True)
    b_seg, b_outs, b_lses = [], [], []
    for window, dil in B_PATTERNS:
        qs, ks, vs = (_to_segments(a, dil) for a in (bq, bk, bv))
        o_seg, lse_seg = _attn_fwd(qs, ks, vs, no_sink, f"attn_b{dil}_fwd", window // (2 * dil), t // dil, tq, False)
        b_seg.append((qs, ks, vs))
        b_outs.append(_from_segments(o_seg, dil))
        b_lses.append(_from_segments(lse_seg, dil))
    x2, b_out, b_lse = _merge_out_proj_fwd(x1, a_out, b_outs, b_lses, wout, "out_proj_fwd", tm_mix)
    (x3, h2, gp2, up2, hid2), _ = _ffn_fwd(x2, norm_ffn2, wg2, wu2, wd2, "ffn2_fwd", tm, tf)

    gfinal = norm_final.reshape(1, d)
    dx3, dg_final, sq, dout2 = _final_norm_loss(x3, gfinal, target, "final_norm_loss", tm)

    dgt2, dut2 = _ffn_bwd_hidden(dout2, gp2, up2, wd2, "ffn2_bwd_hidden", tm, tf)
    (dx2, dg_ffn2), _ = _ffn_bwd_input(dx3, x2, norm_ffn2, dgt2, dut2, wg2, wu2, "ffn2_bwd_input", tm_mix)
    (gwg2, gwu2), _ = _token_products([dgt2, dut2], h2, "ffn2_bwd_gate_up", tt, tfp)
    (gwd2,), _ = _token_products([hid2], dout2, "ffn2_bwd_down", tt, tfp)

    n_att = (A_Q_W // LANES) * (t // tq)
    n_prod = (f_all // tfp) * (t // tt)
    da, db, gwout = _out_proj_bwd(dx2, a_out, b_out, wout, "out_proj_bwd", tm_mix)
    (daq, dakx, davx, dsink_parts), (land_ffn2,) = _attn_bwd(
        aq, akx, avx, a_out, da, a_lse, sink, "attn_a_bwd", A_HALF_WINDOW, t, tq, True,
        _exchange_plan([[gwg2, gwu2, gwd2]]), [0, n_att - 1])
    dbq = dbk = dbv = None
    for (window, dil), (qs, ks, vs) in zip(B_PATTERNS, b_seg):
        os_, dos, ls = (_to_segments(a, dil) for a in (b_out, db, b_lse))
        (dq_s, dk_s, dv_s, _), _ = _attn_bwd(qs, ks, vs, os_, dos, ls, no_sink, f"attn_b{dil}_bwd", window // (2 * dil), t // dil, tq, False)
        parts = [_from_segments(a, dil) for a in (dq_s, dk_s, dv_s)]
        dbq, dbk, dbv = parts if dbq is None else (dbq + parts[0], dbk + parts[1], dbv + parts[2])
    dx1, dg_mix, gwin, dout1 = _in_proj_bwd(dx2, x1, norm_mix, win, h_mix, cos, sin_signed, daq, dakx, davx, dbq, dbk, dbv, "in_proj_bwd", tm_mix)

    (gwd1,), (land_in, land_out) = _token_products(
        [hid1], dout1, "ffn1_bwd_down", tt, tfp, _exchange_plan([[gwin], [gwout]]), [0, n_prod - 1])
    dgt1, dut1 = _ffn_bwd_hidden(dout1, gp1, up1, wd1, "ffn1_bwd_hidden", tm, tf)
    (gwg1, gwu1), (land_wd1,) = _token_products(
        [dgt1, dut1], h1, "ffn1_bwd_gate_up", tt, tfp, _exchange_plan([[gwd1]]), [0, n_prod - 1])
    (grad_x, dg_ffn1), (land_wgu1,) = _ffn_bwd_input(
        dx1, x, norm_ffn1, dgt1, dut1, wg1, wu1, "ffn1_bwd_input", tm_mix, _exchange_plan([[gwg1, gwu1]]), [0, t // tm_mix - 1])

    dsink_pairs = jnp.sum(dsink_parts[:, :, 0, :], axis=1)
    dsink = jnp.stack([dsink_pairs[:, 0], dsink_pairs[:, HEAD_DIM]], axis=1).reshape(1, -1)
    small = jnp.concatenate([dg_ffn1, dg_mix, dg_ffn2, dg_final, jnp.pad(dsink, ((0, 0), (0, d - dsink.shape[1]))),
                             sq, jnp.zeros((2, d), F32)], axis=0)
    (land_small,) = _run_alone(_exchange_plan([[jnp.tile(small, (N_DEV, 1))]]), "gather_small_gradients")
    red_ffn2 = _sum_slots(land_ffn2, 3, "sum_ffn2_grads")
    red_wgu1 = _sum_slots(land_wgu1, 2, "sum_ffn1_gate_up_grads")
    red_wd1 = _sum_slots(land_wd1, 1, "sum_ffn1_down_grads")
    red_in = _sum_slots(land_in, 1, "sum_in_grads")
    red_out = _sum_slots(land_out, 1, "sum_out_grads")
    red_small = _sum_small(land_small, "sum_small_grads")

    rf = rows1[0]
    ffn_block = lambda j: [red_wgu1[:rf], red_wgu1[rf:], red_wd1, red_ffn2[:rf], red_ffn2[rf:2 * rf], red_ffn2[2 * rf:]][j]
    n_sink = a_sink.shape[1]
    grads = {
        "norm_ffn1": red_small[0:1], "w_gate1": ffn_block(0).T[None], "w_up1": ffn_block(1).T[None], "w_down1": ffn_block(2)[None],
        "norm_mix": red_small[1:2], "w_in": red_in.T[None], "a_sink": red_small[4:5, :n_sink], "w_out": red_out[None],
        "norm_ffn2": red_small[2:3], "w_gate2": ffn_block(3).T[None], "w_up2": ffn_block(4).T[None], "w_down2": ffn_block(5)[None],
        "norm_final": red_small[3],
    }
    params = {
        "norm_ffn1": (norm_ffn1, m_norm_ffn1, v_norm_ffn1), "w_gate1": (w_gate1, m_w_gate1, v_w_gate1),
        "w_up1": (w_up1, m_w_up1, v_w_up1), "w_down1": (w_down1, m_w_down1, v_w_down1),
        "norm_mix": (norm_mix, m_norm_mix, v_norm_mix), "w_in": (w_in, m_w_in, v_w_in),
        "a_sink": (a_sink, m_a_sink, v_a_sink), "w_out": (w_out, m_w_out, v_w_out),
        "norm_ffn2": (norm_ffn2, m_norm_ffn2, v_norm_ffn2), "w_gate2": (w_gate2, m_w_gate2, v_w_gate2),
        "w_up2": (w_up2, m_w_up2, v_w_up2), "w_down2": (w_down2, m_w_down2, v_w_down2),
        "norm_final": (norm_final, m_norm_final, v_norm_final),
    }
    loss = 0.5 * jnp.sum(red_small[5]) / d
    deltas, new_ms, new_vs = [], [], []
    for name, (w, m, v) in params.items():
        as_block = (lambda a: a.reshape(1, -1)) if w.ndim == 1 else (lambda a: a)
        dl, nm, nv = _adamw(as_block(w), as_block(grads[name]), as_block(m), as_block(v), f"adamw_{name}")
        deltas.append(dl.reshape(w.shape))
        new_ms.append(nm.reshape(w.shape))
        new_vs.append(nv.reshape(w.shape))
    grad_list = [grads[name] for name in params]
    return (loss, grad_x[None], *grad_list, *deltas, *new_ms, *new_vs)
```

```python
import functools
import itertools

import numpy as np
import jax
import jax.numpy as jnp
from jax import lax
from jax.experimental import pallas as pl
from jax.experimental.pallas import tpu as pltpu

F32 = jnp.float32
BF16 = jnp.bfloat16

N_DEV = 8
HEAD_DIM = 64
LANES = 128
A_Q_W, A_KV_W, B_W = 512, 128, 512
A_HALF_WINDOW = 128
B_PATTERNS = ((128, 1), (512, 4), (2048, 16))
ROPE_THETA = 10000.0
NORM_EPS = 1e-6
FFN_RES_WEIGHT = 0.5
QK_SCALE = HEAD_DIM ** -0.5
NEG = -1e30

ADAM_LR = 0.001
ADAM_B1 = 0.9
ADAM_B2 = 0.999
ADAM_EPS = 1e-08
ADAM_WD = 0.01
ADAM_STEP = 10

MESH_T = pl.DeviceIdType.MESH
VMEM_LIMIT = 56 * 1024 * 1024


def _cp(sem=None, vmem=VMEM_LIMIT):
    return pltpu.CompilerParams(dimension_semantics=sem, vmem_limit_bytes=vmem)


def _dot_nn(a, b):
    return jnp.dot(a, b, preferred_element_type=F32)


def _dot_nt(a, b):
    return lax.dot_general(a, b, (((1,), (1,)), ((), ())), preferred_element_type=F32)


def _dot_tn(a, b):
    return lax.dot_general(a, b, (((0,), (0,)), ((), ())), preferred_element_type=F32)


def _rstd(xv):
    return lax.rsqrt(jnp.mean(xv * xv, axis=-1, keepdims=True) + NORM_EPS)


def _norm_bwd(dh, xv, r, gn):
    gy = dh * gn
    c = jnp.sum(gy * xv, axis=-1, keepdims=True) * (1.0 / xv.shape[-1])
    dx = r * gy - xv * (r * r * r * c)
    dgn = jnp.sum(dh * (xv * r), axis=0, keepdims=True)
    return dx, dgn


def _accumulate(ref, val, first):
    @pl.when(first)
    def _():
        ref[...] = val

    @pl.when(jnp.logical_not(first))
    def _():
        ref[...] += val


def _mesh_pos():
    return lax.axis_index("x"), lax.axis_index("y"), lax.axis_index("c")


def _dev_index(d):
    return 4 * d[0] + 2 * d[1] + d[2]


class _Comm:
    def __init__(self, inputs, out_shape, scratch, phases):
        self.inputs, self.out_shape, self.scratch, self.phases = inputs, out_shape, scratch, phases

    def specs(self):
        any_spec = pl.BlockSpec(memory_space=pl.ANY)
        return [any_spec] * len(self.inputs), [any_spec] * len(self.out_shape)


def _run_alone(comm, name):
    n_in, n_out = len(comm.inputs), len(comm.out_shape)

    def body(*refs):
        for phase in comm.phases:
            phase(refs[:n_in], refs[n_in:n_in + n_out], refs[n_in + n_out:])

    in_specs, out_specs = comm.specs()
    return pl.pallas_call(body, name=name, out_shape=comm.out_shape, in_specs=in_specs, out_specs=out_specs,
                          scratch_shapes=comm.scratch)(*comm.inputs)


def _run_hosted(comm, at, step, ins, outs, scr):
    for phase, when in zip(comm.phases, at):
        @pl.when(step == when)
        def _(phase=phase):
            phase(ins, outs, scr)


def _split(refs, *counts):
    parts, o = [], 0
    for n in counts:
        parts.append(refs[o:o + n])
        o += n
    return parts + [refs[o:]]


def _gather_plan(packed, rows_list):
    n_w = len(rows_list)
    offs = [int(o) for o in np.cumsum([0] + list(rows_list[:-1]))]
    d = packed.shape[1]

    def tools(ins, outs, scr):
        p_ref = ins[0]
        send_sems, recv_sems, local_sem = scr
        x, y, c = _mesh_pos()
        me, sibling = (x, y, c), (x, y, 1 - c)
        chips = [(1 - x, y), (x, 1 - y), (1 - x, 1 - y)]

        def rows(w, dev):
            start = pl.multiple_of(_dev_index(dev) * rows_list[w], 16)
            return outs[w].at[pl.ds(start, rows_list[w]), :]

        def mine(w):
            return p_ref.at[pl.ds(offs[w], rows_list[w]), :]

        def copy(k, w, block, to, own):
            return pltpu.make_async_remote_copy(
                src_ref=mine(w) if own else rows(w, block), dst_ref=rows(w, block),
                send_sem=send_sems.at[k], recv_sem=recv_sems.at[k], device_id=to, device_id_type=MESH_T)

        def all_blocks(k):
            return pltpu.make_async_remote_copy(
                src_ref=p_ref, dst_ref=p_ref, send_sem=send_sems.at[k], recv_sem=recv_sems.at[k],
                device_id=me, device_id_type=MESH_T)

        return p_ref, local_sem, me, sibling, chips, c, rows, mine, copy, all_blocks

    def start(ins, outs, scr):
        _, local_sem, me, sibling, chips, c, rows, mine, copy, _ = tools(ins, outs, scr)
        for w in range(n_w):
            pltpu.make_async_copy(mine(w), rows(w, me), local_sem).start()
        for w in range(n_w):
            copy(0, w, me, sibling, True).start()
        for j, chip in enumerate(chips):
            for w in range(n_w):
                copy(1 + j, w, me, (*chip, c), True).start()

    def relay(ins, outs, scr):
        _, _, _, sibling, chips, c, _, _, copy, all_blocks = tools(ins, outs, scr)
        for j, chip in enumerate(chips):
            all_blocks(1 + j).wait_recv()
            for w in range(n_w):
                copy(4 + j, w, (*chip, c), sibling, False).start()

    def finish(ins, outs, scr):
        p_ref, local_sem, _, _, _, _, _, _, _, all_blocks = tools(ins, outs, scr)
        all_blocks(0).wait_recv()
        for j in range(3):
            all_blocks(4 + j).wait_recv()
        for k in range(7):
            all_blocks(k).wait_send()
        pltpu.make_async_copy(p_ref, p_ref, local_sem).wait()

    return _Comm(
        [packed], [jax.ShapeDtypeStruct((N_DEV * r, d), packed.dtype) for r in rows_list],
        [pltpu.SemaphoreType.DMA((7,)), pltpu.SemaphoreType.DMA((7,)), pltpu.SemaphoreType.DMA], [start, relay, finish])


def _exchange_plan(groups):
    flat = [a for g in groups for a in g]
    n_g = len(groups)
    sizes = [len(g) for g in groups]
    rows = [g[0].shape[0] // N_DEV for g in groups]
    first = [int(o) for o in np.cumsum([0] + sizes[:-1])]

    def start(srcs, lands, scr):
        send_sems, recv_sems, local_sems = scr
        x, y, c = _mesh_pos()
        me = (x, y, c)
        me_idx = _dev_index(me)

        def block(g, i, dev):
            start_row = pl.multiple_of(_dev_index(dev) * rows[g], 8)
            return srcs[first[g] + i].at[pl.ds(start_row, rows[g]), :]

        def slot(g, i):
            return lands[g].at[me_idx, pl.ds(i * rows[g], rows[g]), :]

        for g in range(n_g):
            for i in range(sizes[g]):
                pltpu.make_async_copy(block(g, i, me), slot(g, i), local_sems.at[g]).start()
        flips = [f for f in itertools.product((0, 1), repeat=3) if any(f)]
        for k, (fx, fy, fc) in enumerate(flips):
            peer = (1 - x if fx else x, 1 - y if fy else y, 1 - c if fc else c)
            for g in range(n_g):
                for i in range(sizes[g]):
                    pltpu.make_async_remote_copy(
                        src_ref=block(g, i, peer), dst_ref=slot(g, i), send_sem=send_sems.at[g, k],
                        recv_sem=recv_sems.at[g, k], device_id=peer, device_id_type=MESH_T).start()

    def finish(srcs, lands, scr):
        send_sems, recv_sems, local_sems = scr
        me = _mesh_pos()
        for k in range(7):
            for g in range(n_g):
                pltpu.make_async_remote_copy(
                    src_ref=lands[g].at[0], dst_ref=lands[g].at[0], send_sem=send_sems.at[g, k],
                    recv_sem=recv_sems.at[g, k], device_id=me, device_id_type=MESH_T).wait()
        for g in range(n_g):
            pltpu.make_async_copy(lands[g].at[0], lands[g].at[0], local_sems.at[g]).wait()

    return _Comm(
        flat, [jax.ShapeDtypeStruct((N_DEV, sizes[g] * rows[g], groups[g][0].shape[1]), groups[g][0].dtype) for g in range(n_g)],
        [pltpu.SemaphoreType.DMA((n_g, 7)), pltpu.SemaphoreType.DMA((n_g, 7)), pltpu.SemaphoreType.DMA((n_g,))],
        [start, finish])


def _comm_parts(comm):
    if comm is None:
        return 0, 0, [], [], [], [], []
    in_specs, out_specs = comm.specs()
    return len(comm.inputs), len(comm.out_shape), in_specs, out_specs, comm.out_shape, comm.scratch, comm.inputs


def _ffn_fwd(x, gn, wg_t, wu_t, wd, name, tm, tf, comm=None, comm_at=None):
    t, d = x.shape
    f_all = wg_t.shape[0]
    n_f = f_all // tf
    n_ci, n_co, c_in_specs, c_out_specs, c_shapes, c_scratch, c_inputs = _comm_parts(comm)

    def body(*refs):
        ((x_ref, gn_ref, wg_ref, wu_ref, wd_ref), c_in, (y_ref, h_ref, gp_ref, up_ref, hid_ref), c_out,
         (h_s, hid_s), c_scr) = _split(refs, 5, n_ci, 5, n_co, 2)
        f = pl.program_id(1)
        if comm:
            _run_hosted(comm, comm_at, pl.program_id(0) * n_f + f, c_in, c_out, c_scr)

        @pl.when(f == 0)
        def _():
            xv = x_ref[...]
            h = ((xv * _rstd(xv)) * gn_ref[...]).astype(BF16)
            h_s[...] = h
            h_ref[...] = h

        h = h_s[...]
        g = _dot_nt(h, wg_ref[...])
        u = _dot_nt(h, wu_ref[...])
        gp_ref[...] = g.astype(BF16)
        up_ref[...] = u.astype(BF16)
        hid = ((g * jax.nn.sigmoid(g)) * u).astype(BF16)
        hid_ref[...] = hid
        for f0 in range(n_f):
            @pl.when(f == f0)
            def _(f0=f0):
                hid_s[:, f0 * tf:(f0 + 1) * tf] = hid

        @pl.when(f == n_f - 1)
        def _():
            y_ref[...] = x_ref[...] + FFN_RES_WEIGHT * _dot_nn(hid_s[...], wd_ref[...])

    tok = pl.BlockSpec((tm, d), lambda i, f: (i, 0))
    wblk = pl.BlockSpec((tf, d), lambda i, f: (f, 0))
    whole = pl.BlockSpec((f_all, d), lambda i, f: (0, 0))
    act = pl.BlockSpec((tm, tf), lambda i, f: (i, f))
    act_shape = jax.ShapeDtypeStruct((t, f_all), BF16)
    res = pl.pallas_call(
        body, name=name, grid=(t // tm, n_f),
        out_shape=[jax.ShapeDtypeStruct((t, d), F32), jax.ShapeDtypeStruct((t, d), BF16), act_shape, act_shape, act_shape] + c_shapes,
        in_specs=[tok, pl.BlockSpec((1, d), lambda i, f: (0, 0)), wblk, wblk, whole] + c_in_specs,
        out_specs=[tok, tok, act, act, act] + c_out_specs,
        scratch_shapes=[pltpu.VMEM((tm, d), BF16), pltpu.VMEM((tm, f_all), BF16)] + c_scratch,
        compiler_params=_cp(("arbitrary", "arbitrary")),
    )(x, gn, wg_t, wu_t, wd, *c_inputs)
    return res[:5], res[5:]


def _ffn_bwd_hidden(dout, gp, up, wd, name, tm, tf):
    t, d = dout.shape
    f_all = wd.shape[0]

    def body(dout_ref, gp_ref, up_ref, wd_ref, dg_ref, du_ref):
        dhid = _dot_nt(dout_ref[...], wd_ref[...])
        g = gp_ref[...].astype(F32)
        u = up_ref[...].astype(F32)
        sg = jax.nn.sigmoid(g)
        dg_ref[...] = ((dhid * u) * (sg * (1.0 + g * (1.0 - sg)))).astype(BF16)
        du_ref[...] = (dhid * (g * sg)).astype(BF16)

    tok = pl.BlockSpec((tm, d), lambda i, f: (i, 0))
    wblk = pl.BlockSpec((tf, d), lambda i, f: (f, 0))
    act = pl.BlockSpec((tm, tf), lambda i, f: (i, f))
    act_shape = jax.ShapeDtypeStruct((t, f_all), BF16)
    return pl.pallas_call(
        body, name=name, grid=(t // tm, f_all // tf),
        out_shape=[act_shape, act_shape], in_specs=[tok, act, act, wblk], out_specs=[act, act],
        compiler_params=_cp(("arbitrary", "arbitrary")),
    )(dout, gp, up, wd)


def _ffn_bwd_input(dy, x, gn, dg, du, wg_t, wu_t, name, tm, comm=None, comm_at=None):
    t, d = x.shape
    f_all = wg_t.shape[0]
    n_ci, n_co, c_in_specs, c_out_specs, c_shapes, c_scratch, c_inputs = _comm_parts(comm)

    def body(*refs):
        (dy_ref, x_ref, gn_ref, dg_ref, du_ref, wg_ref, wu_ref), c_in, (dx_ref, dgn_ref), c_out, c_scr = _split(
            refs, 7, n_ci, 2, n_co)
        i = pl.program_id(0)
        if comm:
            _run_hosted(comm, comm_at, i, c_in, c_out, c_scr)
        dh = _dot_nn(dg_ref[...], wg_ref[...]) + _dot_nn(du_ref[...], wu_ref[...])
        xv = x_ref[...]
        dxn, dgn = _norm_bwd(dh, xv, _rstd(xv), gn_ref[...])
        dx_ref[...] = dy_ref[...] + dxn
        _accumulate(dgn_ref, dgn, i == 0)

    tok = pl.BlockSpec((tm, d), lambda i: (i, 0))
    row = pl.BlockSpec((1, d), lambda i: (0, 0))
    act = pl.BlockSpec((tm, f_all), lambda i: (i, 0))
    whole = pl.BlockSpec((f_all, d), lambda i: (0, 0))
    res = pl.pallas_call(
        body, name=name, grid=(t // tm,),
        out_shape=[jax.ShapeDtypeStruct((t, d), F32), jax.ShapeDtypeStruct((1, d), F32)] + c_shapes,
        in_specs=[tok, tok, row, act, act, whole, whole] + c_in_specs,
        out_specs=[tok, row] + c_out_specs,
        scratch_shapes=c_scratch,
        compiler_params=_cp(("arbitrary",)),
    )(dy, x, gn, dg, du, wg_t, wu_t, *c_inputs)
    return res[:2], res[2:]


def _token_products(lhs_list, rhs, name, tt, tf, comm=None, comm_at=None):
    n_l = len(lhs_list)
    t, f_all = lhs_list[0].shape
    d = rhs.shape[1]
    n_t = t // tt
    n_ci, n_co, c_in_specs, c_out_specs, c_shapes, c_scratch, c_inputs = _comm_parts(comm)

    def body(*refs):
        lhs_refs, (rhs_ref,), c_in, out_refs, c_out, accs, c_scr = _split(refs, n_l, 1, n_ci, n_l, n_co, n_l)
        s = pl.program_id(1)
        if comm:
            _run_hosted(comm, comm_at, pl.program_id(0) * n_t + s, c_in, c_out, c_scr)
        rv = rhs_ref[...]
        for l_ref, acc in zip(lhs_refs, accs):
            _accumulate(acc, _dot_tn(l_ref[...], rv), s == 0)

        @pl.when(s == n_t - 1)
        def _():
            for o_ref, acc in zip(out_refs, accs):
                o_ref[...] = acc[...].astype(BF16)

    act = pl.BlockSpec((tt, tf), lambda f, s: (s, f))
    tok = pl.BlockSpec((tt, d), lambda f, s: (s, 0))
    wblk = pl.BlockSpec((tf, d), lambda f, s: (f, 0))
    res = pl.pallas_call(
        body, name=name, grid=(f_all // tf, n_t),
        out_shape=[jax.ShapeDtypeStruct((f_all, d), BF16)] * n_l + c_shapes,
        in_specs=[act] * n_l + [tok] + c_in_specs, out_specs=[wblk] * n_l + c_out_specs,
        scratch_shapes=[pltpu.VMEM((tf, d), F32)] * n_l + c_scratch,
        compiler_params=_cp(("arbitrary", "arbitrary")),
    )(*lhs_list, rhs, *c_inputs)
    return res[:n_l], res[n_l:]


def _swap_halves(t):
    w = t.shape[-1]
    lane = lax.broadcasted_iota(jnp.int32, (1, w), 1)
    return jnp.where((lane % HEAD_DIM) < HEAD_DIM // 2, pltpu.roll(t, w - HEAD_DIM // 2, 1), pltpu.roll(t, HEAD_DIM // 2, 1))


def _rope(t, cos, sin_signed):
    reps = t.shape[-1] // LANES
    return t * jnp.tile(cos, (1, reps)) + _swap_halves(t) * jnp.tile(sin_signed, (1, reps))


def _rope_bwd(dt, cos, sin_signed):
    reps = dt.shape[-1] // LANES
    return dt * jnp.tile(cos, (1, reps)) + _swap_halves(dt * jnp.tile(sin_signed, (1, reps)))


DILATIONS = tuple(dil for _, dil in B_PATTERNS if dil > 1)


def _seg_shape(t, dil, w, dtype):
    return jax.ShapeDtypeStruct((dil, t // dil, w), dtype)


def _seg_spec(tm, dil, w):
    return pl.BlockSpec((dil, tm // dil, w), lambda i: (0, i, 0))


def _tile_scratch(tm, w):
    return [pltpu.VMEM((tm, LANES), F32)] * (w // LANES)


def _put_tile(tile, val):
    for c, ref in enumerate(tile):
        ref[...] = val[:, c * LANES:(c + 1) * LANES]


def _get_tile(tile):
    return jnp.concatenate([ref[...] for ref in tile], axis=1)


def _scatter_to_segments(tile, seg_refs):
    for seg_ref, dil in zip(seg_refs, DILATIONS):
        rows = tile[0].shape[0] // dil
        for r in range(dil):
            for c, ref in enumerate(tile):
                seg_ref[r, :, c * LANES:(c + 1) * LANES] = ref[pl.ds(r, rows, stride=dil), :].astype(seg_ref.dtype)


def _gather_from_segments(seg_ref, dil, tile, add=False):
    rows = tile[0].shape[0] // dil
    for r in range(dil):
        for c, ref in enumerate(tile):
            idx = (pl.ds(r, rows, stride=dil), slice(None))
            v = seg_ref[r, :, c * LANES:(c + 1) * LANES].astype(F32)
            ref[idx] = ref[idx] + v if add else v


def _in_proj_fwd(x, gn, win_t, cos, sin_signed, name, tm):
    t, d = x.shape
    in_w = win_t.shape[0]
    n_dil = len(DILATIONS)

    def body(x_ref, gn_ref, w_ref, cos_ref, sin_ref, h_ref, aq_ref, akx_ref, avx_ref, bq_ref, bk_ref, bv_ref, *rest):
        seg_refs, tile = rest[:3 * n_dil], rest[3 * n_dil:]
        xv = x_ref[...]
        h = ((xv * _rstd(xv)) * gn_ref[...]).astype(BF16)
        h_ref[...] = h
        p = _dot_nt(h, w_ref[...])
        cs, sn = cos_ref[...], sin_ref[...]
        o = 0
        aq_ref[...] = (_rope(p[:, o:o + A_Q_W], cs, sn) * QK_SCALE).astype(BF16)
        o += A_Q_W
        ak = _rope(p[:, o:o + A_KV_W], cs, sn)
        o += A_KV_W
        av = p[:, o:o + A_KV_W]
        o += A_KV_W
        low = lax.broadcasted_iota(jnp.int32, (1, LANES), 1) < HEAD_DIM
        for src, dst in ((ak, akx_ref), (av, avx_ref)):
            other = pltpu.roll(src, HEAD_DIM, 1)
            dst[0] = jnp.where(low, src, other).astype(BF16)
            dst[1] = jnp.where(low, other, src).astype(BF16)
        for k, nat_ref in enumerate((bq_ref, bk_ref, bv_ref)):
            val = p[:, o:o + B_W]
            o += B_W
            if k < 2:
                val = _rope(val, cs, sn)
            if k == 0:
                val = val * QK_SCALE
            nat_ref[...] = val.astype(BF16)
            _put_tile(tile, val)
            _scatter_to_segments(tile, seg_refs[k * n_dil:(k + 1) * n_dil])

    tok = lambda w: pl.BlockSpec((tm, w), lambda i: (i, 0))
    kvx = pl.BlockSpec((2, tm, LANES), lambda i: (0, i, 0))
    sd = lambda *s: jax.ShapeDtypeStruct(s, BF16)
    res = pl.pallas_call(
        body, name=name, grid=(t // tm,),
        out_shape=[sd(t, d), sd(t, A_Q_W), sd(2, t, LANES), sd(2, t, LANES), sd(t, B_W), sd(t, B_W), sd(t, B_W)]
        + [_seg_shape(t, dil, B_W, BF16) for _ in range(3) for dil in DILATIONS],
        in_specs=[tok(d), pl.BlockSpec((1, d), lambda i: (0, 0)), pl.BlockSpec((in_w, d), lambda i: (0, 0)),
                  tok(LANES), tok(LANES)],
        out_specs=[tok(d), tok(A_Q_W), kvx, kvx, tok(B_W), tok(B_W), tok(B_W)]
        + [_seg_spec(tm, dil, B_W) for _ in range(3) for dil in DILATIONS],
        scratch_shapes=_tile_scratch(tm, B_W),
        compiler_params=_cp(("arbitrary",)),
    )(x, gn, win_t, cos, sin_signed)
    return res[:7], [res[7 + k * n_dil:7 + (k + 1) * n_dil] for k in range(3)]


def _in_proj_bwd(dres, x, gn, win_t, h, cos, sin_signed, daq, dakx, davx, dbq, dbk, dbv, name, tm):
    t, d = x.shape
    in_w = win_t.shape[0]
    n_t = t // tm
    chunk = in_w // 3

    def body(dres_ref, x_ref, gn_ref, w_ref, h_ref, cos_ref, sin_ref, daq_ref, dakx_ref, davx_ref, dbq_ref, dbk_ref,
             dbv_ref, dx_ref, dgn_ref, gw_ref, half_ref, dp_s, gw_s):
        i = pl.program_id(0)
        cs, sn = cos_ref[...], sin_ref[...]
        low = lax.broadcasted_iota(jnp.int32, (1, LANES), 1) < HEAD_DIM

        def fold(ref):
            a, b = ref[0], ref[1]
            return jnp.where(low, a + pltpu.roll(a, HEAD_DIM, 1), b + pltpu.roll(b, HEAD_DIM, 1))

        o = 0
        dp_s[:, o:o + A_Q_W] = _rope_bwd(daq_ref[...] * QK_SCALE, cs, sn).astype(BF16)
        o += A_Q_W
        dp_s[:, o:o + A_KV_W] = _rope_bwd(fold(dakx_ref), cs, sn).astype(BF16)
        o += A_KV_W
        dp_s[:, o:o + A_KV_W] = fold(davx_ref).astype(BF16)
        o += A_KV_W
        dp_s[:, o:o + B_W] = _rope_bwd(dbq_ref[...] * QK_SCALE, cs, sn).astype(BF16)
        o += B_W
        dp_s[:, o:o + B_W] = _rope_bwd(dbk_ref[...], cs, sn).astype(BF16)
        o += B_W
        dp_s[:, o:o + B_W] = dbv_ref[...].astype(BF16)
        dh = _dot_nn(dp_s[...], w_ref[...])
        hv = h_ref[...]
        for c0 in range(0, in_w, chunk):
            _accumulate(gw_s.at[pl.ds(c0, chunk), :], _dot_tn(dp_s[:, c0:c0 + chunk], hv), i == 0)
        xv = x_ref[...]
        dxn, dgn = _norm_bwd(dh, xv, _rstd(xv), gn_ref[...])
        dx = dres_ref[...] + dxn
        dx_ref[...] = dx
        half_ref[...] = (FFN_RES_WEIGHT * dx).astype(BF16)
        _accumulate(dgn_ref, dgn, i == 0)

        @pl.when(i == n_t - 1)
        def _():
            gw_ref[...] = gw_s[...].astype(BF16)

    tok = lambda w: pl.BlockSpec((tm, w), lambda i: (i, 0))
    row = pl.BlockSpec((1, d), lambda i: (0, 0))
    whole = pl.BlockSpec((in_w, d), lambda i: (0, 0))
    kvx = pl.BlockSpec((2, tm, LANES), lambda i: (0, i, 0))
    return pl.pallas_call(
        body, name=name, grid=(n_t,),
        out_shape=[jax.ShapeDtypeStruct((t, d), F32), jax.ShapeDtypeStruct((1, d), F32),
                   jax.ShapeDtypeStruct((in_w, d), BF16), jax.ShapeDtypeStruct((t, d), BF16)],
        in_specs=[tok(d), tok(d), row, whole, tok(d), tok(LANES), tok(LANES), tok(A_Q_W), kvx, kvx,
                  tok(B_W), tok(B_W), tok(B_W)],
        out_specs=[tok(d), row, whole, tok(d)],
        scratch_shapes=[pltpu.VMEM((tm, in_w), BF16), pltpu.VMEM((in_w, d), F32)],
        compiler_params=_cp(("arbitrary",)),
    )(dres, x, gn, win_t, h, cos, sin_signed, daq, dakx, davx, dbq, dbk, dbv)


def _merge_out_proj_fwd(x, a_out, outs, lses, wout, name, tm):
    t, d = x.shape
    n_dil = len(DILATIONS)

    def body(x_ref, a_ref, *rest):
        o_refs, l_refs, (w_ref, y_ref, b_ref, lt_ref), b_segs, lt_segs, scratch = _split(rest, 1 + n_dil, 1 + n_dil, 4, n_dil, n_dil)
        n_c = B_W // LANES
        tiles = [scratch[j * n_c:(j + 1) * n_c] for j in range(2 * n_dil)]
        os_, ls = [o_refs[0][...].astype(F32)], [l_refs[0][...]]
        for k, dil in enumerate(DILATIONS):
            _gather_from_segments(o_refs[1 + k], dil, tiles[2 * k])
            _gather_from_segments(l_refs[1 + k], dil, tiles[2 * k + 1])
            os_.append(_get_tile(tiles[2 * k]))
            ls.append(_get_tile(tiles[2 * k + 1]))
        mx = functools.reduce(jnp.maximum, ls)
        es = [jnp.exp(l - mx) for l in ls]
        den = functools.reduce(jnp.add, es)
        b = functools.reduce(jnp.add, [e * o for e, o in zip(es, os_)]) / den
        lt = mx + jnp.log(den)
        bb = b.astype(BF16)
        b_ref[...] = bb
        lt_ref[...] = lt
        y_ref[...] = x_ref[...] + _dot_nn(a_ref[...], w_ref[0:A_Q_W, :]) + _dot_nn(bb, w_ref[A_Q_W:A_Q_W + B_W, :])
        _put_tile(tiles[0], b)
        _scatter_to_segments(tiles[0], b_segs)
        _put_tile(tiles[1], lt)
        _scatter_to_segments(tiles[1], lt_segs)

    tok = lambda w: pl.BlockSpec((tm, w), lambda i: (i, 0))
    segs = [_seg_spec(tm, dil, B_W) for dil in DILATIONS]
    res = pl.pallas_call(
        body, name=name, grid=(t // tm,),
        out_shape=[jax.ShapeDtypeStruct((t, d), F32), jax.ShapeDtypeStruct((t, B_W), BF16), jax.ShapeDtypeStruct((t, B_W), F32)]
        + [_seg_shape(t, dil, B_W, BF16) for dil in DILATIONS] + [_seg_shape(t, dil, B_W, F32) for dil in DILATIONS],
        in_specs=[tok(d), tok(A_Q_W)] + ([tok(B_W)] + segs) * 2 + [pl.BlockSpec(wout.shape, lambda i: (0, 0))],
        out_specs=[tok(d), tok(B_W), tok(B_W)] + segs * 2,
        scratch_shapes=_tile_scratch(tm, B_W) * (2 * n_dil),
        compiler_params=_cp(("arbitrary",)),
    )(x, a_out, *outs, *lses, wout)
    return res[0], [res[1]] + list(res[3:3 + n_dil]), [res[2]] + list(res[3 + n_dil:])


def _out_proj_bwd(dy, a_out, b_out, wout, name, tm):
    t, d = dy.shape
    n_t = t // tm
    n_dil = len(DILATIONS)

    def body(dy_ref, a_ref, b_ref, w_ref, da_ref, db_ref, gw_ref, *rest):
        db_segs, gw_s, tile = rest[:n_dil], rest[n_dil], rest[n_dil + 1:]
        i = pl.program_id(0)
        dyb = dy_ref[...].astype(BF16)
        da_ref[...] = _dot_nt(dyb, w_ref[0:A_Q_W, :]).astype(BF16)
        db = _dot_nt(dyb, w_ref[A_Q_W:A_Q_W + B_W, :])
        db_ref[...] = db.astype(BF16)
        _put_tile(tile, db)
        _scatter_to_segments(tile, db_segs)
        ga = _dot_tn(a_ref[...], dyb)
        gb = _dot_tn(b_ref[...], dyb)

        @pl.when(i == 0)
        def _():
            gw_s[0:A_Q_W, :] = ga
            gw_s[A_Q_W:A_Q_W + B_W, :] = gb

        @pl.when(i > 0)
        def _():
            gw_s[0:A_Q_W, :] += ga
            gw_s[A_Q_W:A_Q_W + B_W, :] += gb

        @pl.when(i == n_t - 1)
        def _():
            gw_ref[...] = gw_s[...].astype(BF16)

    tok = lambda w: pl.BlockSpec((tm, w), lambda i: (i, 0))
    whole = pl.BlockSpec(wout.shape, lambda i: (0, 0))
    res = pl.pallas_call(
        body, name=name, grid=(n_t,),
        out_shape=[jax.ShapeDtypeStruct((t, A_Q_W), BF16), jax.ShapeDtypeStruct((t, B_W), BF16),
                   jax.ShapeDtypeStruct(wout.shape, BF16)] + [_seg_shape(t, dil, B_W, BF16) for dil in DILATIONS],
        in_specs=[tok(d), tok(A_Q_W), tok(B_W), whole],
        out_specs=[tok(A_Q_W), tok(B_W), whole] + [_seg_spec(tm, dil, B_W) for dil in DILATIONS],
        scratch_shapes=[pltpu.VMEM(wout.shape, F32)] + _tile_scratch(tm, B_W),
        compiler_params=_cp(("arbitrary",)),
    )(dy, a_out, b_out, wout)
    return res[0], [res[1]] + list(res[3:]), res[2]


def _sum_pattern_grads(per_pattern, name, tm):
    t = per_pattern[0][0].shape[0]
    n_dil = len(DILATIONS)

    def body(*refs):
        ins, outs, tile = _split(refs, 3 * (1 + n_dil), 3)
        for j, o_ref in enumerate(outs):
            _put_tile(tile, ins[j][...])
            for k, dil in enumerate(DILATIONS):
                _gather_from_segments(ins[3 * (1 + k) + j], dil, tile, add=True)
            o_ref[...] = _get_tile(tile)

    tok = pl.BlockSpec((tm, B_W), lambda i: (i, 0))
    flat = [a if k == 0 else a.reshape(DILATIONS[k - 1], -1, B_W) for k, grads in enumerate(per_pattern) for a in grads]
    return pl.pallas_call(
        body, name=name, grid=(t // tm,),
        out_shape=[jax.ShapeDtypeStruct((t, B_W), F32)] * 3,
        in_specs=[tok] * 3 + [_seg_spec(tm, dil, B_W) for dil in DILATIONS for _ in range(3)],
        out_specs=[tok] * 3, scratch_shapes=_tile_scratch(tm, B_W),
        compiler_params=_cp(("arbitrary",)),
    )(*flat)


SUB_ROWS = 64


def _sub_band(r0, hw, win, seg_lo, seg_len, t, rel, col):
    ks = pl.multiple_of(jnp.clip(r0 - hw, 0, t - win), SUB_ROWS)
    kpos = col + ks
    valid = (jnp.abs(rel + (ks - r0)) <= hw) & (kpos >= seg_lo) & (kpos < seg_lo + seg_len)
    return ks, valid


def _split_heads(v, low):
    zero = jnp.zeros_like(v)
    return jnp.concatenate([jnp.where(low, v, zero), jnp.where(low, zero, v)], axis=0)


def _kv_spec(kv, t):
    if kv.ndim == 3:
        return pl.BlockSpec((None, t, LANES), lambda p, i: (p // 2, 0, 0))
    return pl.BlockSpec((t, LANES), lambda p, i: (0, p))


def _attn_fwd(q, k, v, sink, name, hw, seg_len, tq, has_sink):
    t, width = q.shape
    sb = SUB_ROWS
    win = 2 * hw + LANES

    def body(sink_ref, q_ref, k_ref, v_ref, o_ref, lse_ref):
        p, i = pl.program_id(0), pl.program_id(1)
        q0 = i * tq
        seg_lo = (q0 // seg_len) * seg_len
        low = lax.broadcasted_iota(jnp.int32, (1, LANES), 1) < HEAD_DIM
        rel = lax.broadcasted_iota(jnp.int32, (sb, win), 1) - lax.broadcasted_iota(jnp.int32, (sb, win), 0)
        col = lax.broadcasted_iota(jnp.int32, (1, win), 1)
        for j in range(tq // sb):
            rows = pl.ds(j * sb, sb)
            ks, valid = _sub_band(q0 + j * sb, hw, win, seg_lo, seg_len, t, rel, col)
            kw = k_ref[pl.ds(ks, win), :]
            vw = v_ref[pl.ds(ks, win), :]
            s = _dot_nt(_split_heads(q_ref[rows, :], low), kw)
            es, inv, lses = [], [], []
            for a in range(2):
                sa = jnp.where(valid, s[a * sb:(a + 1) * sb], NEG)
                m = jnp.max(sa, axis=1, keepdims=True)
                if has_sink:
                    sk = sink_ref[2 * p + a]
                    m = jnp.maximum(m, sk)
                e = jnp.exp(sa - m)
                den = jnp.sum(e, axis=1, keepdims=True)
                if has_sink:
                    den = den + jnp.exp(sk - m)
                es.append(e.astype(BF16))
                inv.append(1.0 / den)
                lses.append(m + jnp.log(den))
            pv = _dot_nn(jnp.concatenate(es, axis=0), vw)
            o_ref[rows, :] = jnp.where(low, pv[0:sb] * inv[0], pv[sb:2 * sb] * inv[1]).astype(BF16)
            lse_ref[rows, :] = jnp.where(low, lses[0], lses[1])

    tile = pl.BlockSpec((tq, LANES), lambda p, i: (i, p))
    return pl.pallas_call(
        body, name=name, grid=(width // LANES, t // tq),
        out_shape=[jax.ShapeDtypeStruct((t, width), BF16), jax.ShapeDtypeStruct((t, width), F32)],
        in_specs=[pl.BlockSpec(memory_space=pltpu.SMEM), tile, _kv_spec(k, t), _kv_spec(v, t)],
        out_specs=[tile, tile],
        compiler_params=_cp(("arbitrary", "arbitrary")),
    )(sink, q, k, v)


def _attn_bwd(q, k, v, o, do, lse, sink, name, hw, seg_len, tq, has_sink, comm=None, comm_at=None):
    t, width = q.shape
    sb = SUB_ROWS
    win = 2 * hw + LANES
    n_q = t // tq
    shared_kv = k.ndim == 3
    n_ci, n_co, c_in_specs, c_out_specs, c_shapes, c_scratch, c_inputs = _comm_parts(comm)

    def body(*refs):
        ((sink_ref, q_ref, k_ref, v_ref, o_ref, do_ref, lse_ref), c_in, (dq_ref, dk_ref, dv_ref, ds_ref), c_out, c_scr) = _split(
            refs, 7, n_ci, 4, n_co)
        p, i = pl.program_id(0), pl.program_id(1)
        if comm:
            _run_hosted(comm, comm_at, p * n_q + i, c_in, c_out, c_scr)
        fresh = (i == 0) & (p % 2 == 0) if shared_kv else i == 0

        @pl.when(fresh)
        def _():
            dk_ref[...] = jnp.zeros_like(dk_ref)
            dv_ref[...] = jnp.zeros_like(dv_ref)

        q0 = i * tq
        seg_lo = (q0 // seg_len) * seg_len
        low = lax.broadcasted_iota(jnp.int32, (1, LANES), 1) < HEAD_DIM
        rel = lax.broadcasted_iota(jnp.int32, (sb, win), 1) - lax.broadcasted_iota(jnp.int32, (sb, win), 0)
        col = lax.broadcasted_iota(jnp.int32, (1, win), 1)
        dsink = [jnp.zeros((1, 1), F32), jnp.zeros((1, 1), F32)]
        for j in range(tq // sb):
            rows = pl.ds(j * sb, sb)
            ks, valid = _sub_band(q0 + j * sb, hw, win, seg_lo, seg_len, t, rel, col)
            kw = k_ref[pl.ds(ks, win), :]
            vw = v_ref[pl.ds(ks, win), :]
            dov = do_ref[rows, :]
            prod = dov.astype(F32) * o_ref[rows, :].astype(F32)
            lse_t = lse_ref[rows, :]
            q2 = _split_heads(q_ref[rows, :], low)
            do2 = _split_heads(dov, low)
            s = _dot_nt(q2, kw)
            dpr = _dot_nt(do2, vw)
            prs, dss = [], []
            for a in range(2):
                mine = low if a == 0 else jnp.logical_not(low)
                lse_a = jnp.max(jnp.where(mine, lse_t, -jnp.inf), axis=1, keepdims=True)
                delta = jnp.sum(jnp.where(mine, prod, 0.0), axis=1, keepdims=True)
                pr = jnp.exp(jnp.where(valid, s[a * sb:(a + 1) * sb], NEG) - lse_a)
                prs.append(pr.astype(BF16))
                dss.append((pr * (dpr[a * sb:(a + 1) * sb] - delta)).astype(BF16))
                if has_sink:
                    dsink[a] = dsink[a] - jnp.sum(jnp.exp(sink_ref[2 * p + a] - lse_a) * delta, axis=0, keepdims=True)
            ds2 = jnp.concatenate(dss, axis=0)
            dv_ref[pl.ds(ks, win), :] += _dot_tn(jnp.concatenate(prs, axis=0), do2)
            dk_ref[pl.ds(ks, win), :] += _dot_tn(ds2, q2)
            dq2 = _dot_nn(ds2, kw)
            dq_ref[rows, :] = jnp.where(low, dq2[0:sb], dq2[sb:2 * sb])
        ds_ref[...] = jnp.broadcast_to(jnp.where(low, dsink[0], dsink[1]), ds_ref.shape)

    tile = pl.BlockSpec((tq, LANES), lambda p, i: (i, p))
    kv_shape = jax.ShapeDtypeStruct(k.shape, F32)
    res = pl.pallas_call(
        body, name=name, grid=(width // LANES, n_q),
        out_shape=[jax.ShapeDtypeStruct((t, width), F32), kv_shape, kv_shape,
                   jax.ShapeDtypeStruct((width // LANES, n_q, 8, LANES), F32)] + c_shapes,
        in_specs=[pl.BlockSpec(memory_space=pltpu.SMEM), tile, _kv_spec(k, t), _kv_spec(v, t), tile, tile, tile] + c_in_specs,
        out_specs=[tile, _kv_spec(k, t), _kv_spec(v, t), pl.BlockSpec((None, None, 8, LANES), lambda p, i: (p, i, 0, 0))] + c_out_specs,
        scratch_shapes=c_scratch,
        compiler_params=_cp(("arbitrary", "arbitrary")),
    )(sink, q, k, v, o, do, lse, *c_inputs)
    return res[:4], res[4:]


def _final_norm_loss(x, gn, target, name, tm):
    t, d = x.shape

    def body(x_ref, gn_ref, tg_ref, dx_ref, dgn_ref, sq_ref, half_ref):
        i = pl.program_id(0)
        xv, gnv = x_ref[...], gn_ref[...]
        r = _rstd(xv)
        diff = (xv * r) * gnv - tg_ref[...]
        dxn, dgn = _norm_bwd(diff * (1.0 / d), xv, r, gnv)
        dx_ref[...] = dxn
        half_ref[...] = (FFN_RES_WEIGHT * dxn).astype(BF16)
        _accumulate(dgn_ref, dgn, i == 0)
        _accumulate(sq_ref, jnp.sum(diff * diff, axis=0, keepdims=True), i == 0)

    tok = pl.BlockSpec((tm, d), lambda i: (i, 0))
    row = pl.BlockSpec((1, d), lambda i: (0, 0))
    return pl.pallas_call(
        body, name=name, grid=(t // tm,),
        out_shape=[jax.ShapeDtypeStruct((t, d), F32), jax.ShapeDtypeStruct((1, d), F32), jax.ShapeDtypeStruct((1, d), F32),
                   jax.ShapeDtypeStruct((t, d), BF16)],
        in_specs=[tok, row, tok], out_specs=[tok, row, row, tok],
        compiler_params=_cp(("arbitrary",)),
    )(x, gn, target)


def _adamw(w, g, m, v, name):
    def body(w_ref, g_ref, m_ref, v_ref, d_ref, nm_ref, nv_ref):
        gv = g_ref[...]
        nm = ADAM_B1 * m_ref[...] + (1.0 - ADAM_B1) * gv
        nv = ADAM_B2 * v_ref[...] + (1.0 - ADAM_B2) * (gv * gv)
        m_hat = nm / (1.0 - ADAM_B1 ** ADAM_STEP)
        v_hat = nv / (1.0 - ADAM_B2 ** ADAM_STEP)
        d_ref[...] = -ADAM_LR * (m_hat / (jnp.sqrt(v_hat) + ADAM_EPS) + ADAM_WD * w_ref[...])
        nm_ref[...] = nm
        nv_ref[...] = nv

    shape = jax.ShapeDtypeStruct(w.shape, F32)
    return pl.pallas_call(body, name=name, out_shape=[shape, shape, shape], compiler_params=_cp())(w, g, m, v)


def _adamw_rows(land, j, rows, w, m, v, name):
    d = w.shape[1]

    def body(l_ref, w_ref, m_ref, v_ref, g_ref, d_ref, nm_ref, nv_ref):
        gv = l_ref[0].astype(F32)
        for s in range(1, N_DEV):
            gv = gv + l_ref[s].astype(F32)
        g_ref[...] = gv
        nm = ADAM_B1 * m_ref[...] + (1.0 - ADAM_B1) * gv
        nv = ADAM_B2 * v_ref[...] + (1.0 - ADAM_B2) * (gv * gv)
        m_hat = nm / (1.0 - ADAM_B1 ** ADAM_STEP)
        v_hat = nv / (1.0 - ADAM_B2 ** ADAM_STEP)
        d_ref[...] = -ADAM_LR * (m_hat / (jnp.sqrt(v_hat) + ADAM_EPS) + ADAM_WD * w_ref[...])
        nm_ref[...] = nm
        nv_ref[...] = nv

    whole = pl.BlockSpec((rows, d), lambda i: (0, 0))
    shape = jax.ShapeDtypeStruct((rows, d), F32)
    return pl.pallas_call(
        body, name=name, grid=(1,), out_shape=[shape] * 4,
        in_specs=[pl.BlockSpec((N_DEV, rows, d), lambda i: (0, j, 0)), whole, whole, whole], out_specs=[whole] * 4,
        compiler_params=_cp(("arbitrary",)),
    )(land, w, m, v)


def _sum_small(land, name):
    def body(l_ref, o_ref):
        acc = l_ref[0]
        for s in range(1, N_DEV):
            acc = acc + l_ref[s]
        o_ref[...] = acc

    return pl.pallas_call(body, name=name, out_shape=jax.ShapeDtypeStruct(land.shape[1:], F32), compiler_params=_cp())(land)


def _rope_lanes(positions):
    inv_freq = 1.0 / (ROPE_THETA ** (jnp.arange(0, HEAD_DIM, 2, dtype=F32) / HEAD_DIM))
    ang = positions.astype(F32)[:, None] * inv_freq
    cos, sin = jnp.cos(ang), jnp.sin(ang)
    return jnp.concatenate([cos, cos, cos, cos], axis=1), jnp.concatenate([-sin, sin, -sin, sin], axis=1)


def kernel(x, positions, norm_ffn1, w_gate1, w_up1, w_down1, norm_mix, w_in, a_sink, w_out, norm_ffn2, w_gate2, w_up2, w_down2, norm_final, loss_target, m_norm_ffn1, m_w_gate1, m_w_up1, m_w_down1, m_norm_mix, m_w_in, m_a_sink, m_w_out, m_norm_ffn2, m_w_gate2, m_w_up2, m_w_down2, m_norm_final, v_norm_ffn1, v_w_gate1, v_w_up1, v_w_down1, v_norm_mix, v_w_in, v_a_sink, v_w_out, v_norm_ffn2, v_w_gate2, v_w_up2, v_w_down2, v_norm_final):
    x = x[0]
    target = loss_target[0]
    t, d = x.shape
    tm = min(1024, t)
    tm_mix = min(512, t)
    tf = 256
    tq = min(512, t // 16)

    tt = min(1024, t)
    f_all = w_gate1.shape[2] * N_DEV
    tfp = f_all // 2
    n_steps = (t // tm) * (f_all // tf)

    def stacked(shards):
        return jnp.concatenate([s.astype(BF16) for s in shards], axis=0), [s.shape[0] for s in shards]

    packed1, rows1 = stacked([w_gate1[0].T, w_up1[0].T, w_down1[0]])
    packed2, rows2 = stacked([w_in[0].T, w_out[0], w_gate2[0].T, w_up2[0].T, w_down2[0]])
    wg1, wu1, wd1 = _run_alone(_gather_plan(packed1, rows1), "gather_ffn1_weights")

    cos, sin_signed = _rope_lanes(positions[0])
    sink = a_sink[0]
    no_sink = jnp.zeros_like(sink)

    (x1, h1, gp1, up1, hid1), (win, wout, wg2, wu2, wd2) = _ffn_fwd(
        x, norm_ffn1, wg1, wu1, wd1, "ffn1_fwd", tm, tf, _gather_plan(packed2, rows2), [0, (3 * n_steps) // 4, n_steps - 1])
    (h_mix, aq, akx, avx, bq, bk, bv), seg_qkv = _in_proj_fwd(x1, norm_mix, win, cos, sin_signed, "in_proj_fwd", tm_mix)
    a_out, a_lse = _attn_fwd(aq, akx, avx, sink, "attn_a_fwd", A_HALF_WINDOW, t, tq, True)
    rows_of = lambda a: a.reshape(t, B_W)
    segments_of = lambda a, dil: a if dil == 1 else a.reshape(dil, t // dil, B_W)
    b_qkv, b_outs, b_lses = [], [], []
    for n, (window, dil) in enumerate(B_PATTERNS):
        qs, ks, vs = (bq, bk, bv) if dil == 1 else (rows_of(seg_qkv[k][n - 1]) for k in range(3))
        o_seg, lse_seg = _attn_fwd(qs, ks, vs, no_sink, f"attn_b{dil}_fwd", window // (2 * dil), t // dil, tq, False)
        b_qkv.append((qs, ks, vs))
        b_outs.append(segments_of(o_seg, dil))
        b_lses.append(segments_of(lse_seg, dil))
    x2, b_out, b_lse = _merge_out_proj_fwd(x1, a_out, b_outs, b_lses, wout, "out_proj_fwd", tm_mix)
    (x3, h2, gp2, up2, hid2), _ = _ffn_fwd(x2, norm_ffn2, wg2, wu2, wd2, "ffn2_fwd", tm, tf)

    gfinal = norm_final.reshape(1, d)
    dx3, dg_final, sq, dout2 = _final_norm_loss(x3, gfinal, target, "final_norm_loss", tm)

    dgt2, dut2 = _ffn_bwd_hidden(dout2, gp2, up2, wd2, "ffn2_bwd_hidden", tm, tf)
    (dx2, dg_ffn2), _ = _ffn_bwd_input(dx3, x2, norm_ffn2, dgt2, dut2, wg2, wu2, "ffn2_bwd_input", tm_mix)
    (gwg2, gwu2), _ = _token_products([dgt2, dut2], h2, "ffn2_bwd_gate_up", tt, tfp)
    (gwd2,), _ = _token_products([hid2], dout2, "ffn2_bwd_down", tt, tfp)

    n_att = (A_Q_W // LANES) * (t // tq)
    n_prod = (f_all // tfp) * (t // tt)
    da, db, gwout = _out_proj_bwd(dx2, a_out, b_out[0], wout, "out_proj_bwd", tm_mix)
    (daq, dakx, davx, dsink_parts), (land_ffn2,) = _attn_bwd(
        aq, akx, avx, a_out, da, a_lse, sink, "attn_a_bwd", A_HALF_WINDOW, t, tq, True,
        _exchange_plan([[gwg2, gwu2, gwd2]]), [0, n_att - 1])
    pattern_grads = []
    for n, ((window, dil), (qs, ks, vs)) in enumerate(zip(B_PATTERNS, b_qkv)):
        grads, _ = _attn_bwd(qs, ks, vs, rows_of(b_out[n]), rows_of(db[n]), rows_of(b_lse[n]), no_sink, f"attn_b{dil}_bwd",
                             window // (2 * dil), t // dil, tq, False)
        pattern_grads.append(grads[:3])
    dbq, dbk, dbv = _sum_pattern_grads(pattern_grads, "sum_pattern_grads", tm_mix)
    dx1, dg_mix, gwin, dout1 = _in_proj_bwd(dx2, x1, norm_mix, win, h_mix, cos, sin_signed, daq, dakx, davx, dbq, dbk, dbv, "in_proj_bwd", tm_mix)

    (gwd1,), (land_in, land_out) = _token_products(
        [hid1], dout1, "ffn1_bwd_down", tt, tfp, _exchange_plan([[gwin], [gwout]]), [0, n_prod - 1])
    dgt1, dut1 = _ffn_bwd_hidden(dout1, gp1, up1, wd1, "ffn1_bwd_hidden", tm, tf)
    (gwg1, gwu1), (land_wd1,) = _token_products(
        [dgt1, dut1], h1, "ffn1_bwd_gate_up", tt, tfp, _exchange_plan([[gwd1]]), [0, n_prod - 1])
    (grad_x, dg_ffn1), (land_wgu1,) = _ffn_bwd_input(
        dx1, x, norm_ffn1, dgt1, dut1, wg1, wu1, "ffn1_bwd_input", tm_mix, _exchange_plan([[gwg1, gwu1]]), [0, t // tm_mix - 1])

    dsink_pairs = jnp.sum(dsink_parts[:, :, 0, :], axis=1)
    dsink = jnp.stack([dsink_pairs[:, 0], dsink_pairs[:, HEAD_DIM]], axis=1).reshape(1, -1)
    small = jnp.concatenate([dg_ffn1, dg_mix, dg_ffn2, dg_final, jnp.pad(dsink, ((0, 0), (0, d - dsink.shape[1]))),
                             sq, jnp.zeros((2, d), F32)], axis=0)
    (land_small,) = _run_alone(_exchange_plan([[jnp.tile(small, (N_DEV, 1))]]), "gather_small_gradients")
    red_small = _sum_small(land_small, "sum_small_grads")
    loss = 0.5 * jnp.sum(red_small[5]) / d

    rf = rows1[0]
    sharded = {"w_gate1": (land_wgu1, 0, rf, True), "w_up1": (land_wgu1, 1, rf, True), "w_down1": (land_wd1, 0, rf, False),
               "w_in": (land_in, 0, rows2[0], True), "w_out": (land_out, 0, rows2[1], False),
               "w_gate2": (land_ffn2, 0, rf, True), "w_up2": (land_ffn2, 1, rf, True), "w_down2": (land_ffn2, 2, rf, False)}
    n_sink = a_sink.shape[1]
    small_grads = {"norm_ffn1": red_small[0:1], "norm_mix": red_small[1:2], "norm_ffn2": red_small[2:3], "norm_final": red_small[3],
                   "a_sink": red_small[4:5, :n_sink]}
    params = {
        "norm_ffn1": (norm_ffn1, m_norm_ffn1, v_norm_ffn1), "w_gate1": (w_gate1, m_w_gate1, v_w_gate1),
        "w_up1": (w_up1, m_w_up1, v_w_up1), "w_down1": (w_down1, m_w_down1, v_w_down1),
        "norm_mix": (norm_mix, m_norm_mix, v_norm_mix), "w_in": (w_in, m_w_in, v_w_in),
        "a_sink": (a_sink, m_a_sink, v_a_sink), "w_out": (w_out, m_w_out, v_w_out),
        "norm_ffn2": (norm_ffn2, m_norm_ffn2, v_norm_ffn2), "w_gate2": (w_gate2, m_w_gate2, v_w_gate2),
        "w_up2": (w_up2, m_w_up2, v_w_up2), "w_down2": (w_down2, m_w_down2, v_w_down2),
        "norm_final": (norm_final, m_norm_final, v_norm_final),
    }
    grad_list, deltas, new_ms, new_vs = [], [], [], []
    for name, (w, m, v) in params.items():
        if name in sharded:
            land, j, rows, is_transposed = sharded[name]
            view = (lambda a: a[0].T) if is_transposed else (lambda a: a[0])
            back = (lambda a: a.T[None]) if is_transposed else (lambda a: a[None])
            outs = [back(o) for o in _adamw_rows(land, j, rows, view(w), view(m), view(v), f"adamw_{name}")]
        else:
            as_block = (lambda a: a.reshape(1, -1)) if w.ndim == 1 else (lambda a: a)
            g = small_grads[name]
            outs = [g] + [o.reshape(w.shape) for o in _adamw(as_block(w), as_block(g), as_block(m), as_block(v), f"adamw_{name}")]
        for lst, o in zip((grad_list, deltas, new_ms, new_vs), outs):
            lst.append(o)
    return (loss, grad_x[None], *grad_list, *deltas, *new_ms, *new_vs)
```

```python
import functools
import itertools

import numpy as np
import jax
import jax.numpy as jnp
from jax import lax
from jax.experimental import pallas as pl
from jax.experimental.pallas import tpu as pltpu

F32 = jnp.float32
BF16 = jnp.bfloat16

N_DEV = 8
HEAD_DIM = 64
LANES = 128
A_Q_W, A_KV_W, B_W = 512, 128, 512
A_HALF_WINDOW = 128
B_PATTERNS = ((128, 1), (512, 4), (2048, 16))
ROPE_THETA = 10000.0
NORM_EPS = 1e-6
FFN_RES_WEIGHT = 0.5
QK_SCALE = HEAD_DIM ** -0.5
NEG = -1e30

ADAM_LR = 0.001
ADAM_B1 = 0.9
ADAM_B2 = 0.999
ADAM_EPS = 1e-08
ADAM_WD = 0.01
ADAM_STEP = 10

MESH_T = pl.DeviceIdType.MESH
VMEM_LIMIT = 56 * 1024 * 1024


def _cp(sem=None, vmem=VMEM_LIMIT):
    return pltpu.CompilerParams(dimension_semantics=sem, vmem_limit_bytes=vmem)


def _dot_nn(a, b):
    return jnp.dot(a, b, preferred_element_type=F32)


def _dot_nt(a, b):
    return lax.dot_general(a, b, (((1,), (1,)), ((), ())), preferred_element_type=F32)


def _dot_tn(a, b):
    return lax.dot_general(a, b, (((0,), (0,)), ((), ())), preferred_element_type=F32)


def _rstd(xv):
    return lax.rsqrt(jnp.mean(xv * xv, axis=-1, keepdims=True) + NORM_EPS)


def _norm_bwd(dh, xv, r, gn):
    gy = dh * gn
    c = jnp.sum(gy * xv, axis=-1, keepdims=True) * (1.0 / xv.shape[-1])
    dx = r * gy - xv * (r * r * r * c)
    dgn = jnp.sum(dh * (xv * r), axis=0, keepdims=True)
    return dx, dgn


def _accumulate(ref, val, first):
    @pl.when(first)
    def _():
        ref[...] = val

    @pl.when(jnp.logical_not(first))
    def _():
        ref[...] += val


def _mesh_pos():
    return lax.axis_index("x"), lax.axis_index("y"), lax.axis_index("c")


def _dev_index(d):
    return 4 * d[0] + 2 * d[1] + d[2]


class _Comm:
    def __init__(self, inputs, out_shape, scratch, phases):
        self.inputs, self.out_shape, self.scratch, self.phases = inputs, out_shape, scratch, phases

    def specs(self):
        any_spec = pl.BlockSpec(memory_space=pl.ANY)
        return [any_spec] * len(self.inputs), [any_spec] * len(self.out_shape)


def _run_alone(comm, name):
    n_in, n_out = len(comm.inputs), len(comm.out_shape)

    def body(*refs):
        for phase in comm.phases:
            phase(refs[:n_in], refs[n_in:n_in + n_out], refs[n_in + n_out:])

    in_specs, out_specs = comm.specs()
    return pl.pallas_call(body, name=name, out_shape=comm.out_shape, in_specs=in_specs, out_specs=out_specs,
                          scratch_shapes=comm.scratch)(*comm.inputs)


def _run_hosted(comm, at, step, ins, outs, scr):
    for phase, when in zip(comm.phases, at):
        @pl.when(step == when)
        def _(phase=phase):
            phase(ins, outs, scr)


def _split(refs, *counts):
    parts, o = [], 0
    for n in counts:
        parts.append(refs[o:o + n])
        o += n
    return parts + [refs[o:]]


def _gather_plan(packed, rows_list):
    n_w = len(rows_list)
    offs = [int(o) for o in np.cumsum([0] + list(rows_list[:-1]))]
    d = packed.shape[1]

    def tools(ins, outs, scr):
        p_ref = ins[0]
        send_sems, recv_sems, local_sem = scr
        x, y, c = _mesh_pos()
        me, sibling = (x, y, c), (x, y, 1 - c)
        chips = [(1 - x, y), (x, 1 - y), (1 - x, 1 - y)]

        def rows(w, dev):
            start = pl.multiple_of(_dev_index(dev) * rows_list[w], 16)
            return outs[w].at[pl.ds(start, rows_list[w]), :]

        def mine(w):
            return p_ref.at[pl.ds(offs[w], rows_list[w]), :]

        def copy(k, w, block, to, own):
            return pltpu.make_async_remote_copy(
                src_ref=mine(w) if own else rows(w, block), dst_ref=rows(w, block),
                send_sem=send_sems.at[k], recv_sem=recv_sems.at[k], device_id=to, device_id_type=MESH_T)

        def all_blocks(k):
            return pltpu.make_async_remote_copy(
                src_ref=p_ref, dst_ref=p_ref, send_sem=send_sems.at[k], recv_sem=recv_sems.at[k],
                device_id=me, device_id_type=MESH_T)

        return p_ref, local_sem, me, sibling, chips, c, rows, mine, copy, all_blocks

    def start(ins, outs, scr):
        _, local_sem, me, sibling, chips, c, rows, mine, copy, _ = tools(ins, outs, scr)
        for w in range(n_w):
            pltpu.make_async_copy(mine(w), rows(w, me), local_sem).start()
        for w in range(n_w):
            copy(0, w, me, sibling, True).start()
        for j, chip in enumerate(chips):
            for w in range(n_w):
                copy(1 + j, w, me, (*chip, c), True).start()

    def relay(ins, outs, scr):
        _, _, _, sibling, chips, c, _, _, copy, all_blocks = tools(ins, outs, scr)
        for j, chip in enumerate(chips):
            all_blocks(1 + j).wait_recv()
            for w in range(n_w):
                copy(4 + j, w, (*chip, c), sibling, False).start()

    def finish(ins, outs, scr):
        p_ref, local_sem, _, _, _, _, _, _, _, all_blocks = tools(ins, outs, scr)
        all_blocks(0).wait_recv()
        for j in range(3):
            all_blocks(4 + j).wait_recv()
        for k in range(7):
            all_blocks(k).wait_send()
        pltpu.make_async_copy(p_ref, p_ref, local_sem).wait()

    return _Comm(
        [packed], [jax.ShapeDtypeStruct((N_DEV * r, d), packed.dtype) for r in rows_list],
        [pltpu.SemaphoreType.DMA((7,)), pltpu.SemaphoreType.DMA((7,)), pltpu.SemaphoreType.DMA], [start, relay, finish])


def _exchange_plan(groups):
    flat = [a for g in groups for a in g]
    n_g = len(groups)
    sizes = [len(g) for g in groups]
    rows = [g[0].shape[0] // N_DEV for g in groups]
    first = [int(o) for o in np.cumsum([0] + sizes[:-1])]

    def start(srcs, lands, scr):
        send_sems, recv_sems, local_sems = scr
        x, y, c = _mesh_pos()
        me = (x, y, c)
        me_idx = _dev_index(me)

        def block(g, i, dev):
            start_row = pl.multiple_of(_dev_index(dev) * rows[g], 8)
            return srcs[first[g] + i].at[pl.ds(start_row, rows[g]), :]

        def slot(g, i):
            return lands[g].at[me_idx, pl.ds(i * rows[g], rows[g]), :]

        for g in range(n_g):
            for i in range(sizes[g]):
                pltpu.make_async_copy(block(g, i, me), slot(g, i), local_sems.at[g]).start()
        flips = [f for f in itertools.product((0, 1), repeat=3) if any(f)]
        for k, (fx, fy, fc) in enumerate(flips):
            peer = (1 - x if fx else x, 1 - y if fy else y, 1 - c if fc else c)
            for g in range(n_g):
                for i in range(sizes[g]):
                    pltpu.make_async_remote_copy(
                        src_ref=block(g, i, peer), dst_ref=slot(g, i), send_sem=send_sems.at[g, k],
                        recv_sem=recv_sems.at[g, k], device_id=peer, device_id_type=MESH_T).start()

    def finish(srcs, lands, scr):
        send_sems, recv_sems, local_sems = scr
        me = _mesh_pos()
        for k in range(7):
            for g in range(n_g):
                pltpu.make_async_remote_copy(
                    src_ref=lands[g].at[0], dst_ref=lands[g].at[0], send_sem=send_sems.at[g, k],
                    recv_sem=recv_sems.at[g, k], device_id=me, device_id_type=MESH_T).wait()
        for g in range(n_g):
            pltpu.make_async_copy(lands[g].at[0], lands[g].at[0], local_sems.at[g]).wait()

    return _Comm(
        flat, [jax.ShapeDtypeStruct((N_DEV, sizes[g] * rows[g], groups[g][0].shape[1]), groups[g][0].dtype) for g in range(n_g)],
        [pltpu.SemaphoreType.DMA((n_g, 7)), pltpu.SemaphoreType.DMA((n_g, 7)), pltpu.SemaphoreType.DMA((n_g,))],
        [start, finish])


def _comm_parts(comm):
    if comm is None:
        return 0, 0, [], [], [], [], []
    in_specs, out_specs = comm.specs()
    return len(comm.inputs), len(comm.out_shape), in_specs, out_specs, comm.out_shape, comm.scratch, comm.inputs


def _ffn_fwd(x, gn, wg_t, wu_t, wd, name, tm, tf, comm=None, comm_at=None):
    t, d = x.shape
    f_all = wg_t.shape[0]
    n_f = f_all // tf
    n_ci, n_co, c_in_specs, c_out_specs, c_shapes, c_scratch, c_inputs = _comm_parts(comm)

    def body(*refs):
        ((x_ref, gn_ref, wg_ref, wu_ref, wd_ref), c_in, (y_ref, h_ref, gp_ref, up_ref, hid_ref), c_out,
         (h_s, hid_s), c_scr) = _split(refs, 5, n_ci, 5, n_co, 2)
        f = pl.program_id(1)
        if comm:
            _run_hosted(comm, comm_at, pl.program_id(0) * n_f + f, c_in, c_out, c_scr)

        @pl.when(f == 0)
        def _():
            xv = x_ref[...]
            h = ((xv * _rstd(xv)) * gn_ref[...]).astype(BF16)
            h_s[...] = h
            h_ref[...] = h

        h = h_s[...]
        g = _dot_nt(h, wg_ref[...])
        u = _dot_nt(h, wu_ref[...])
        gp_ref[...] = g.astype(BF16)
        up_ref[...] = u.astype(BF16)
        hid = ((g * jax.nn.sigmoid(g)) * u).astype(BF16)
        hid_ref[...] = hid
        for f0 in range(n_f):
            @pl.when(f == f0)
            def _(f0=f0):
                hid_s[:, f0 * tf:(f0 + 1) * tf] = hid

        @pl.when(f == n_f - 1)
        def _():
            y_ref[...] = x_ref[...] + FFN_RES_WEIGHT * _dot_nn(hid_s[...], wd_ref[...])

    tok = pl.BlockSpec((tm, d), lambda i, f: (i, 0))
    wblk = pl.BlockSpec((tf, d), lambda i, f: (f, 0))
    whole = pl.BlockSpec((f_all, d), lambda i, f: (0, 0))
    act = pl.BlockSpec((tm, tf), lambda i, f: (i, f))
    act_shape = jax.ShapeDtypeStruct((t, f_all), BF16)
    res = pl.pallas_call(
        body, name=name, grid=(t // tm, n_f),
        out_shape=[jax.ShapeDtypeStruct((t, d), F32), jax.ShapeDtypeStruct((t, d), BF16), act_shape, act_shape, act_shape] + c_shapes,
        in_specs=[tok, pl.BlockSpec((1, d), lambda i, f: (0, 0)), wblk, wblk, whole] + c_in_specs,
        out_specs=[tok, tok, act, act, act] + c_out_specs,
        scratch_shapes=[pltpu.VMEM((tm, d), BF16), pltpu.VMEM((tm, f_all), BF16)] + c_scratch,
        compiler_params=_cp(("arbitrary", "arbitrary")),
    )(x, gn, wg_t, wu_t, wd, *c_inputs)
    return res[:5], res[5:]


def _ffn_bwd_hidden(dout, gp, up, wd, name, tm, tf):
    t, d = dout.shape
    f_all = wd.shape[0]

    def body(dout_ref, gp_ref, up_ref, wd_ref, dg_ref, du_ref):
        dhid = _dot_nt(dout_ref[...], wd_ref[...])
        g = gp_ref[...].astype(F32)
        u = up_ref[...].astype(F32)
        sg = jax.nn.sigmoid(g)
        dg_ref[...] = ((dhid * u) * (sg * (1.0 + g * (1.0 - sg)))).astype(BF16)
        du_ref[...] = (dhid * (g * sg)).astype(BF16)

    tok = pl.BlockSpec((tm, d), lambda i, f: (i, 0))
    wblk = pl.BlockSpec((tf, d), lambda i, f: (f, 0))
    act = pl.BlockSpec((tm, tf), lambda i, f: (i, f))
    act_shape = jax.ShapeDtypeStruct((t, f_all), BF16)
    return pl.pallas_call(
        body, name=name, grid=(t // tm, f_all // tf),
        out_shape=[act_shape, act_shape], in_specs=[tok, act, act, wblk], out_specs=[act, act],
        compiler_params=_cp(("arbitrary", "arbitrary")),
    )(dout, gp, up, wd)


def _ffn_bwd_input(dy, x, gn, dg, du, wg_t, wu_t, name, tm, comm=None, comm_at=None):
    t, d = x.shape
    f_all = wg_t.shape[0]
    n_ci, n_co, c_in_specs, c_out_specs, c_shapes, c_scratch, c_inputs = _comm_parts(comm)

    def body(*refs):
        (dy_ref, x_ref, gn_ref, dg_ref, du_ref, wg_ref, wu_ref), c_in, (dx_ref, dgn_ref), c_out, c_scr = _split(
            refs, 7, n_ci, 2, n_co)
        i = pl.program_id(0)
        if comm:
            _run_hosted(comm, comm_at, i, c_in, c_out, c_scr)
        dh = _dot_nn(dg_ref[...], wg_ref[...]) + _dot_nn(du_ref[...], wu_ref[...])
        xv = x_ref[...]
        dxn, dgn = _norm_bwd(dh, xv, _rstd(xv), gn_ref[...])
        dx_ref[...] = dy_ref[...] + dxn
        _accumulate(dgn_ref, dgn, i == 0)

    tok = pl.BlockSpec((tm, d), lambda i: (i, 0))
    row = pl.BlockSpec((1, d), lambda i: (0, 0))
    act = pl.BlockSpec((tm, f_all), lambda i: (i, 0))
    whole = pl.BlockSpec((f_all, d), lambda i: (0, 0))
    res = pl.pallas_call(
        body, name=name, grid=(t // tm,),
        out_shape=[jax.ShapeDtypeStruct((t, d), F32), jax.ShapeDtypeStruct((1, d), F32)] + c_shapes,
        in_specs=[tok, tok, row, act, act, whole, whole] + c_in_specs,
        out_specs=[tok, row] + c_out_specs,
        scratch_shapes=c_scratch,
        compiler_params=_cp(("arbitrary",)),
    )(dy, x, gn, dg, du, wg_t, wu_t, *c_inputs)
    return res[:2], res[2:]


def _token_products(lhs_list, rhs, name, tt, tf, comm=None, comm_at=None):
    n_l = len(lhs_list)
    t, f_all = lhs_list[0].shape
    d = rhs.shape[1]
    n_t = t // tt
    n_ci, n_co, c_in_specs, c_out_specs, c_shapes, c_scratch, c_inputs = _comm_parts(comm)

    def body(*refs):
        lhs_refs, (rhs_ref,), c_in, out_refs, c_out, accs, c_scr = _split(refs, n_l, 1, n_ci, n_l, n_co, n_l)
        s = pl.program_id(1)
        if comm:
            _run_hosted(comm, comm_at, pl.program_id(0) * n_t + s, c_in, c_out, c_scr)
        rv = rhs_ref[...]
        for l_ref, acc in zip(lhs_refs, accs):
            _accumulate(acc, _dot_tn(l_ref[...], rv), s == 0)

        @pl.when(s == n_t - 1)
        def _():
            for o_ref, acc in zip(out_refs, accs):
                o_ref[...] = acc[...].astype(BF16)

    act = pl.BlockSpec((tt, tf), lambda f, s: (s, f))
    tok = pl.BlockSpec((tt, d), lambda f, s: (s, 0))
    wblk = pl.BlockSpec((tf, d), lambda f, s: (f, 0))
    res = pl.pallas_call(
        body, name=name, grid=(f_all // tf, n_t),
        out_shape=[jax.ShapeDtypeStruct((f_all, d), BF16)] * n_l + c_shapes,
        in_specs=[act] * n_l + [tok] + c_in_specs, out_specs=[wblk] * n_l + c_out_specs,
        scratch_shapes=[pltpu.VMEM((tf, d), F32)] * n_l + c_scratch,
        compiler_params=_cp(("arbitrary", "arbitrary")),
    )(*lhs_list, rhs, *c_inputs)
    return res[:n_l], res[n_l:]


def _swap_halves(t):
    w = t.shape[-1]
    lane = lax.broadcasted_iota(jnp.int32, (1, w), 1)
    return jnp.where((lane % HEAD_DIM) < HEAD_DIM // 2, pltpu.roll(t, w - HEAD_DIM // 2, 1), pltpu.roll(t, HEAD_DIM // 2, 1))


def _rope(t, cos, sin_signed):
    reps = t.shape[-1] // LANES
    return t * jnp.tile(cos, (1, reps)) + _swap_halves(t) * jnp.tile(sin_signed, (1, reps))


def _rope_bwd(dt, cos, sin_signed):
    reps = dt.shape[-1] // LANES
    return dt * jnp.tile(cos, (1, reps)) + _swap_halves(dt * jnp.tile(sin_signed, (1, reps)))


DILATIONS = tuple(dil for _, dil in B_PATTERNS if dil > 1)


def _seg_shape(t, dil, w, dtype):
    return jax.ShapeDtypeStruct((dil, t // dil, w), dtype)


def _seg_spec(tm, dil, w):
    return pl.BlockSpec((dil, tm // dil, w), lambda i: (0, i, 0))


def _tile_scratch(tm, w):
    return [pltpu.VMEM((tm, LANES), F32)] * (w // LANES)


def _put_tile(tile, val):
    for c, ref in enumerate(tile):
        ref[...] = val[:, c * LANES:(c + 1) * LANES]


def _get_tile(tile):
    return jnp.concatenate([ref[...] for ref in tile], axis=1)


def _scatter_to_segments(tile, seg_refs):
    for seg_ref, dil in zip(seg_refs, DILATIONS):
        rows = tile[0].shape[0] // dil
        for r in range(dil):
            for c, ref in enumerate(tile):
                seg_ref[r, :, c * LANES:(c + 1) * LANES] = ref[pl.ds(r, rows, stride=dil), :].astype(seg_ref.dtype)


def _gather_from_segments(seg_ref, dil, tile, add=False):
    rows = tile[0].shape[0] // dil
    for r in range(dil):
        for c, ref in enumerate(tile):
            idx = (pl.ds(r, rows, stride=dil), slice(None))
            v = seg_ref[r, :, c * LANES:(c + 1) * LANES].astype(F32)
            ref[idx] = ref[idx] + v if add else v


def _in_proj_fwd(x, gn, win_t, cos, sin_signed, name, tm):
    t, d = x.shape
    in_w = win_t.shape[0]
    n_dil = len(DILATIONS)

    def body(x_ref, gn_ref, w_ref, cos_ref, sin_ref, h_ref, aq_ref, akx_ref, avx_ref, bq_ref, bk_ref, bv_ref, *rest):
        seg_refs, tile = rest[:3 * n_dil], rest[3 * n_dil:]
        xv = x_ref[...]
        h = ((xv * _rstd(xv)) * gn_ref[...]).astype(BF16)
        h_ref[...] = h
        p = _dot_nt(h, w_ref[...])
        cs, sn = cos_ref[...], sin_ref[...]
        o = 0
        aq_ref[...] = (_rope(p[:, o:o + A_Q_W], cs, sn) * QK_SCALE).astype(BF16)
        o += A_Q_W
        ak = _rope(p[:, o:o + A_KV_W], cs, sn)
        o += A_KV_W
        av = p[:, o:o + A_KV_W]
        o += A_KV_W
        low = lax.broadcasted_iota(jnp.int32, (1, LANES), 1) < HEAD_DIM
        for src, dst in ((ak, akx_ref), (av, avx_ref)):
            other = pltpu.roll(src, HEAD_DIM, 1)
            dst[0] = jnp.where(low, src, other).astype(BF16)
            dst[1] = jnp.where(low, other, src).astype(BF16)
        for k, nat_ref in enumerate((bq_ref, bk_ref, bv_ref)):
            val = p[:, o:o + B_W]
            o += B_W
            if k < 2:
                val = _rope(val, cs, sn)
            if k == 0:
                val = val * QK_SCALE
            nat_ref[...] = val.astype(BF16)
            _put_tile(tile, val)
            _scatter_to_segments(tile, seg_refs[k * n_dil:(k + 1) * n_dil])

    tok = lambda w: pl.BlockSpec((tm, w), lambda i: (i, 0))
    kvx = pl.BlockSpec((2, tm, LANES), lambda i: (0, i, 0))
    sd = lambda *s: jax.ShapeDtypeStruct(s, BF16)
    res = pl.pallas_call(
        body, name=name, grid=(t // tm,),
        out_shape=[sd(t, d), sd(t, A_Q_W), sd(2, t, LANES), sd(2, t, LANES), sd(t, B_W), sd(t, B_W), sd(t, B_W)]
        + [_seg_shape(t, dil, B_W, BF16) for _ in range(3) for dil in DILATIONS],
        in_specs=[tok(d), pl.BlockSpec((1, d), lambda i: (0, 0)), pl.BlockSpec((in_w, d), lambda i: (0, 0)),
                  tok(LANES), tok(LANES)],
        out_specs=[tok(d), tok(A_Q_W), kvx, kvx, tok(B_W), tok(B_W), tok(B_W)]
        + [_seg_spec(tm, dil, B_W) for _ in range(3) for dil in DILATIONS],
        scratch_shapes=_tile_scratch(tm, B_W),
        compiler_params=_cp(("arbitrary",)),
    )(x, gn, win_t, cos, sin_signed)
    return res[:7], [res[7 + k * n_dil:7 + (k + 1) * n_dil] for k in range(3)]


def _in_proj_bwd(dres, x, gn, win_t, h, cos, sin_signed, daq, dakx, davx, dbq, dbk, dbv, name, tm):
    t, d = x.shape
    in_w = win_t.shape[0]
    n_t = t // tm
    chunk = in_w // 3

    def body(dres_ref, x_ref, gn_ref, w_ref, h_ref, cos_ref, sin_ref, daq_ref, dakx_ref, davx_ref, dbq_ref, dbk_ref,
             dbv_ref, dx_ref, dgn_ref, gw_ref, half_ref, dp_s, gw_s):
        i = pl.program_id(0)
        cs, sn = cos_ref[...], sin_ref[...]
        low = lax.broadcasted_iota(jnp.int32, (1, LANES), 1) < HEAD_DIM

        def fold(ref):
            a, b = ref[0], ref[1]
            return jnp.where(low, a + pltpu.roll(a, HEAD_DIM, 1), b + pltpu.roll(b, HEAD_DIM, 1))

        o = 0
        dp_s[:, o:o + A_Q_W] = _rope_bwd(daq_ref[...] * QK_SCALE, cs, sn).astype(BF16)
        o += A_Q_W
        dp_s[:, o:o + A_KV_W] = _rope_bwd(fold(dakx_ref), cs, sn).astype(BF16)
        o += A_KV_W
        dp_s[:, o:o + A_KV_W] = fold(davx_ref).astype(BF16)
        o += A_KV_W
        dp_s[:, o:o + B_W] = _rope_bwd(dbq_ref[...] * QK_SCALE, cs, sn).astype(BF16)
        o += B_W
        dp_s[:, o:o + B_W] = _rope_bwd(dbk_ref[...], cs, sn).astype(BF16)
        o += B_W
        dp_s[:, o:o + B_W] = dbv_ref[...].astype(BF16)
        dh = _dot_nn(dp_s[...], w_ref[...])
        hv = h_ref[...]
        for c0 in range(0, in_w, chunk):
            _accumulate(gw_s.at[pl.ds(c0, chunk), :], _dot_tn(dp_s[:, c0:c0 + chunk], hv), i == 0)
        xv = x_ref[...]
        dxn, dgn = _norm_bwd(dh, xv, _rstd(xv), gn_ref[...])
        dx = dres_ref[...] + dxn
        dx_ref[...] = dx
        half_ref[...] = (FFN_RES_WEIGHT * dx).astype(BF16)
        _accumulate(dgn_ref, dgn, i == 0)

        @pl.when(i == n_t - 1)
        def _():
            gw_ref[...] = gw_s[...].astype(BF16)

    tok = lambda w: pl.BlockSpec((tm, w), lambda i: (i, 0))
    row = pl.BlockSpec((1, d), lambda i: (0, 0))
    whole = pl.BlockSpec((in_w, d), lambda i: (0, 0))
    kvx = pl.BlockSpec((2, tm, LANES), lambda i: (0, i, 0))
    return pl.pallas_call(
        body, name=name, grid=(n_t,),
        out_shape=[jax.ShapeDtypeStruct((t, d), F32), jax.ShapeDtypeStruct((1, d), F32),
                   jax.ShapeDtypeStruct((in_w, d), BF16), jax.ShapeDtypeStruct((t, d), BF16)],
        in_specs=[tok(d), tok(d), row, whole, tok(d), tok(LANES), tok(LANES), tok(A_Q_W), kvx, kvx,
                  tok(B_W), tok(B_W), tok(B_W)],
        out_specs=[tok(d), row, whole, tok(d)],
        scratch_shapes=[pltpu.VMEM((tm, in_w), BF16), pltpu.VMEM((in_w, d), F32)],
        compiler_params=_cp(("arbitrary",)),
    )(dres, x, gn, win_t, h, cos, sin_signed, daq, dakx, davx, dbq, dbk, dbv)


def _merge_out_proj_fwd(x, a_out, outs, lses, wout, name, tm):
    t, d = x.shape
    n_dil = len(DILATIONS)

    def body(x_ref, a_ref, *rest):
        o_refs, l_refs, (w_ref, y_ref, b_ref, lt_ref), b_segs, lt_segs, scratch = _split(rest, 1 + n_dil, 1 + n_dil, 4, n_dil, n_dil)
        n_c = B_W // LANES
        tiles = [scratch[j * n_c:(j + 1) * n_c] for j in range(2 * n_dil)]
        os_, ls = [o_refs[0][...].astype(F32)], [l_refs[0][...]]
        for k, dil in enumerate(DILATIONS):
            _gather_from_segments(o_refs[1 + k], dil, tiles[2 * k])
            _gather_from_segments(l_refs[1 + k], dil, tiles[2 * k + 1])
            os_.append(_get_tile(tiles[2 * k]))
            ls.append(_get_tile(tiles[2 * k + 1]))
        mx = functools.reduce(jnp.maximum, ls)
        es = [jnp.exp(l - mx) for l in ls]
        den = functools.reduce(jnp.add, es)
        b = functools.reduce(jnp.add, [e * o for e, o in zip(es, os_)]) / den
        lt = mx + jnp.log(den)
        bb = b.astype(BF16)
        b_ref[...] = bb
        lt_ref[...] = lt
        y_ref[...] = x_ref[...] + _dot_nn(a_ref[...], w_ref[0:A_Q_W, :]) + _dot_nn(bb, w_ref[A_Q_W:A_Q_W + B_W, :])
        _put_tile(tiles[0], b)
        _scatter_to_segments(tiles[0], b_segs)
        _put_tile(tiles[1], lt)
        _scatter_to_segments(tiles[1], lt_segs)

    tok = lambda w: pl.BlockSpec((tm, w), lambda i: (i, 0))
    segs = [_seg_spec(tm, dil, B_W) for dil in DILATIONS]
    res = pl.pallas_call(
        body, name=name, grid=(t // tm,),
        out_shape=[jax.ShapeDtypeStruct((t, d), F32), jax.ShapeDtypeStruct((t, B_W), BF16), jax.ShapeDtypeStruct((t, B_W), F32)]
        + [_seg_shape(t, dil, B_W, BF16) for dil in DILATIONS] + [_seg_shape(t, dil, B_W, F32) for dil in DILATIONS],
        in_specs=[tok(d), tok(A_Q_W)] + ([tok(B_W)] + segs) * 2 + [pl.BlockSpec(wout.shape, lambda i: (0, 0))],
        out_specs=[tok(d), tok(B_W), tok(B_W)] + segs * 2,
        scratch_shapes=_tile_scratch(tm, B_W) * (2 * n_dil),
        compiler_params=_cp(("arbitrary",)),
    )(x, a_out, *outs, *lses, wout)
    return res[0], [res[1]] + list(res[3:3 + n_dil]), [res[2]] + list(res[3 + n_dil:])


def _out_proj_bwd(dy, a_out, b_out, wout, name, tm):
    t, d = dy.shape
    n_t = t // tm
    n_dil = len(DILATIONS)

    def body(dy_ref, a_ref, b_ref, w_ref, da_ref, db_ref, gw_ref, *rest):
        db_segs, gw_s, tile = rest[:n_dil], rest[n_dil], rest[n_dil + 1:]
        i = pl.program_id(0)
        dyb = dy_ref[...].astype(BF16)
        da_ref[...] = _dot_nt(dyb, w_ref[0:A_Q_W, :]).astype(BF16)
        db = _dot_nt(dyb, w_ref[A_Q_W:A_Q_W + B_W, :])
        db_ref[...] = db.astype(BF16)
        _put_tile(tile, db)
        _scatter_to_segments(tile, db_segs)
        ga = _dot_tn(a_ref[...], dyb)
        gb = _dot_tn(b_ref[...], dyb)

        @pl.when(i == 0)
        def _():
            gw_s[0:A_Q_W, :] = ga
            gw_s[A_Q_W:A_Q_W + B_W, :] = gb

        @pl.when(i > 0)
        def _():
            gw_s[0:A_Q_W, :] += ga
            gw_s[A_Q_W:A_Q_W + B_W, :] += gb

        @pl.when(i == n_t - 1)
        def _():
            gw_ref[...] = gw_s[...].astype(BF16)

    tok = lambda w: pl.BlockSpec((tm, w), lambda i: (i, 0))
    whole = pl.BlockSpec(wout.shape, lambda i: (0, 0))
    res = pl.pallas_call(
        body, name=name, grid=(n_t,),
        out_shape=[jax.ShapeDtypeStruct((t, A_Q_W), BF16), jax.ShapeDtypeStruct((t, B_W), BF16),
                   jax.ShapeDtypeStruct(wout.shape, BF16)] + [_seg_shape(t, dil, B_W, BF16) for dil in DILATIONS],
        in_specs=[tok(d), tok(A_Q_W), tok(B_W), whole],
        out_specs=[tok(A_Q_W), tok(B_W), whole] + [_seg_spec(tm, dil, B_W) for dil in DILATIONS],
        scratch_shapes=[pltpu.VMEM(wout.shape, F32)] + _tile_scratch(tm, B_W),
        compiler_params=_cp(("arbitrary",)),
    )(dy, a_out, b_out, wout)
    return res[0], [res[1]] + list(res[3:]), res[2]


def _sum_pattern_grads(per_pattern, name, tm):
    t = per_pattern[0][0].shape[0]
    n_dil = len(DILATIONS)

    def body(*refs):
        ins, outs, tile = _split(refs, 3 * (1 + n_dil), 3)
        for j, o_ref in enumerate(outs):
            _put_tile(tile, ins[j][...])
            for k, dil in enumerate(DILATIONS):
                _gather_from_segments(ins[3 * (1 + k) + j], dil, tile, add=True)
            o_ref[...] = _get_tile(tile)

    tok = pl.BlockSpec((tm, B_W), lambda i: (i, 0))
    flat = [a if k == 0 else a.reshape(DILATIONS[k - 1], -1, B_W) for k, grads in enumerate(per_pattern) for a in grads]
    return pl.pallas_call(
        body, name=name, grid=(t // tm,),
        out_shape=[jax.ShapeDtypeStruct((t, B_W), F32)] * 3,
        in_specs=[tok] * 3 + [_seg_spec(tm, dil, B_W) for dil in DILATIONS for _ in range(3)],
        out_specs=[tok] * 3, scratch_shapes=_tile_scratch(tm, B_W),
        compiler_params=_cp(("arbitrary",)),
    )(*flat)


SUB_ROWS = 64


def _sub_band(r0, hw, win, seg_lo, seg_len, t, rel, col):
    ks = pl.multiple_of(jnp.clip(r0 - hw, 0, t - win), SUB_ROWS)
    kpos = col + ks
    valid = (jnp.abs(rel + (ks - r0)) <= hw) & (kpos >= seg_lo) & (kpos < seg_lo + seg_len)
    return ks, valid


def _split_heads(v, low):
    zero = jnp.zeros_like(v)
    return jnp.concatenate([jnp.where(low, v, zero), jnp.where(low, zero, v)], axis=0)


def _kv_spec(kv, t):
    if kv.ndim == 3:
        return pl.BlockSpec((None, t, LANES), lambda p, i: (p // 2, 0, 0))
    return pl.BlockSpec((t, LANES), lambda p, i: (0, p))


def _attn_fwd(q, k, v, sink, name, hw, seg_len, tq, has_sink):
    t, width = q.shape
    sb = SUB_ROWS
    win = 2 * hw + LANES

    def body(sink_ref, q_ref, k_ref, v_ref, o_ref, lse_ref):
        p, i = pl.program_id(0), pl.program_id(1)
        q0 = i * tq
        seg_lo = (q0 // seg_len) * seg_len
        low = lax.broadcasted_iota(jnp.int32, (1, LANES), 1) < HEAD_DIM
        rel = lax.broadcasted_iota(jnp.int32, (sb, win), 1) - lax.broadcasted_iota(jnp.int32, (sb, win), 0)
        col = lax.broadcasted_iota(jnp.int32, (1, win), 1)
        subs = []
        for j in range(tq // sb):
            rows = pl.ds(j * sb, sb)
            ks, valid = _sub_band(q0 + j * sb, hw, win, seg_lo, seg_len, t, rel, col)
            subs.append((rows, ks, valid, _dot_nt(_split_heads(q_ref[rows, :], low), k_ref[pl.ds(ks, win), :])))
        for rows, ks, valid, s in subs:
            vw = v_ref[pl.ds(ks, win), :]
            es, inv, lses = [], [], []
            for a in range(2):
                sa = jnp.where(valid, s[a * sb:(a + 1) * sb], NEG)
                m = jnp.max(sa, axis=1, keepdims=True)
                if has_sink:
                    sk = sink_ref[2 * p + a]
                    m = jnp.maximum(m, sk)
                e = jnp.exp(sa - m)
                den = jnp.sum(e, axis=1, keepdims=True)
                if has_sink:
                    den = den + jnp.exp(sk - m)
                es.append(e.astype(BF16))
                inv.append(1.0 / den)
                lses.append(m + jnp.log(den))
            pv = _dot_nn(jnp.concatenate(es, axis=0), vw)
            o_ref[rows, :] = jnp.where(low, pv[0:sb] * inv[0], pv[sb:2 * sb] * inv[1]).astype(BF16)
            lse_ref[rows, :] = jnp.where(low, lses[0], lses[1])

    tile = pl.BlockSpec((tq, LANES), lambda p, i: (i, p))
    return pl.pallas_call(
        body, name=name, grid=(width // LANES, t // tq),
        out_shape=[jax.ShapeDtypeStruct((t, width), BF16), jax.ShapeDtypeStruct((t, width), F32)],
        in_specs=[pl.BlockSpec(memory_space=pltpu.SMEM), tile, _kv_spec(k, t), _kv_spec(v, t)],
        out_specs=[tile, tile],
        compiler_params=_cp(("arbitrary", "arbitrary")),
    )(sink, q, k, v)


def _attn_bwd(q, k, v, o, do, lse, sink, name, hw, seg_len, tq, has_sink, comm=None, comm_at=None):
    t, width = q.shape
    sb = SUB_ROWS
    win = 2 * hw + LANES
    n_q = t // tq
    shared_kv = k.ndim == 3
    n_ci, n_co, c_in_specs, c_out_specs, c_shapes, c_scratch, c_inputs = _comm_parts(comm)

    def body(*refs):
        ((sink_ref, q_ref, k_ref, v_ref, o_ref, do_ref, lse_ref), c_in, (dq_ref, dk_ref, dv_ref, ds_ref), c_out, c_scr) = _split(
            refs, 7, n_ci, 4, n_co)
        p, i = pl.program_id(0), pl.program_id(1)
        if comm:
            _run_hosted(comm, comm_at, p * n_q + i, c_in, c_out, c_scr)
        fresh = (i == 0) & (p % 2 == 0) if shared_kv else i == 0

        @pl.when(fresh)
        def _():
            dk_ref[...] = jnp.zeros_like(dk_ref)
            dv_ref[...] = jnp.zeros_like(dv_ref)

        q0 = i * tq
        seg_lo = (q0 // seg_len) * seg_len
        low = lax.broadcasted_iota(jnp.int32, (1, LANES), 1) < HEAD_DIM
        rel = lax.broadcasted_iota(jnp.int32, (sb, win), 1) - lax.broadcasted_iota(jnp.int32, (sb, win), 0)
        col = lax.broadcasted_iota(jnp.int32, (1, win), 1)
        dsink = [jnp.zeros((1, 1), F32), jnp.zeros((1, 1), F32)]
        subs = []
        for j in range(tq // sb):
            rows = pl.ds(j * sb, sb)
            ks, valid = _sub_band(q0 + j * sb, hw, win, seg_lo, seg_len, t, rel, col)
            kw = k_ref[pl.ds(ks, win), :]
            dov = do_ref[rows, :]
            q2 = _split_heads(q_ref[rows, :], low)
            do2 = _split_heads(dov, low)
            subs.append((rows, ks, valid, kw, dov, q2, do2, _dot_nt(q2, kw), _dot_nt(do2, v_ref[pl.ds(ks, win), :])))
        probs = []
        for rows, ks, valid, kw, dov, q2, do2, s, dpr in subs:
            prod = dov.astype(F32) * o_ref[rows, :].astype(F32)
            lse_t = lse_ref[rows, :]
            prs, dss = [], []
            for a in range(2):
                mine = low if a == 0 else jnp.logical_not(low)
                lse_a = jnp.max(jnp.where(mine, lse_t, -jnp.inf), axis=1, keepdims=True)
                delta = jnp.sum(jnp.where(mine, prod, 0.0), axis=1, keepdims=True)
                pr = jnp.exp(jnp.where(valid, s[a * sb:(a + 1) * sb], NEG) - lse_a)
                prs.append(pr.astype(BF16))
                dss.append((pr * (dpr[a * sb:(a + 1) * sb] - delta)).astype(BF16))
                if has_sink:
                    dsink[a] = dsink[a] - jnp.sum(jnp.exp(sink_ref[2 * p + a] - lse_a) * delta, axis=0, keepdims=True)
            probs.append((jnp.concatenate(prs, axis=0), jnp.concatenate(dss, axis=0)))
        for (rows, ks, valid, kw, dov, q2, do2, s, dpr), (pr2, ds2) in zip(subs, probs):
            dv_ref[pl.ds(ks, win), :] += _dot_tn(pr2, do2)
            dk_ref[pl.ds(ks, win), :] += _dot_tn(ds2, q2)
            dq2 = _dot_nn(ds2, kw)
            dq_ref[rows, :] = jnp.where(low, dq2[0:sb], dq2[sb:2 * sb])
        ds_ref[...] = jnp.broadcast_to(jnp.where(low, dsink[0], dsink[1]), ds_ref.shape)

    tile = pl.BlockSpec((tq, LANES), lambda p, i: (i, p))
    kv_shape = jax.ShapeDtypeStruct(k.shape, F32)
    res = pl.pallas_call(
        body, name=name, grid=(width // LANES, n_q),
        out_shape=[jax.ShapeDtypeStruct((t, width), F32), kv_shape, kv_shape,
                   jax.ShapeDtypeStruct((width // LANES, n_q, 8, LANES), F32)] + c_shapes,
        in_specs=[pl.BlockSpec(memory_space=pltpu.SMEM), tile, _kv_spec(k, t), _kv_spec(v, t), tile, tile, tile] + c_in_specs,
        out_specs=[tile, _kv_spec(k, t), _kv_spec(v, t), pl.BlockSpec((None, None, 8, LANES), lambda p, i: (p, i, 0, 0))] + c_out_specs,
        scratch_shapes=c_scratch,
        compiler_params=_cp(("arbitrary", "arbitrary")),
    )(sink, q, k, v, o, do, lse, *c_inputs)
    return res[:4], res[4:]


def _final_norm_loss(x, gn, target, name, tm):
    t, d = x.shape

    def body(x_ref, gn_ref, tg_ref, dx_ref, dgn_ref, sq_ref, half_ref):
        i = pl.program_id(0)
        xv, gnv = x_ref[...], gn_ref[...]
        r = _rstd(xv)
        diff = (xv * r) * gnv - tg_ref[...]
        dxn, dgn = _norm_bwd(diff * (1.0 / d), xv, r, gnv)
        dx_ref[...] = dxn
        half_ref[...] = (FFN_RES_WEIGHT * dxn).astype(BF16)
        _accumulate(dgn_ref, dgn, i == 0)
        _accumulate(sq_ref, jnp.sum(diff * diff, axis=0, keepdims=True), i == 0)

    tok = pl.BlockSpec((tm, d), lambda i: (i, 0))
    row = pl.BlockSpec((1, d), lambda i: (0, 0))
    return pl.pallas_call(
        body, name=name, grid=(t // tm,),
        out_shape=[jax.ShapeDtypeStruct((t, d), F32), jax.ShapeDtypeStruct((1, d), F32), jax.ShapeDtypeStruct((1, d), F32),
                   jax.ShapeDtypeStruct((t, d), BF16)],
        in_specs=[tok, row, tok], out_specs=[tok, row, row, tok],
        compiler_params=_cp(("arbitrary",)),
    )(x, gn, target)


def _adamw(w, g, m, v, name):
    def body(w_ref, g_ref, m_ref, v_ref, d_ref, nm_ref, nv_ref):
        gv = g_ref[...]
        nm = ADAM_B1 * m_ref[...] + (1.0 - ADAM_B1) * gv
        nv = ADAM_B2 * v_ref[...] + (1.0 - ADAM_B2) * (gv * gv)
        m_hat = nm / (1.0 - ADAM_B1 ** ADAM_STEP)
        v_hat = nv / (1.0 - ADAM_B2 ** ADAM_STEP)
        d_ref[...] = -ADAM_LR * (m_hat / (jnp.sqrt(v_hat) + ADAM_EPS) + ADAM_WD * w_ref[...])
        nm_ref[...] = nm
        nv_ref[...] = nv

    shape = jax.ShapeDtypeStruct(w.shape, F32)
    return pl.pallas_call(body, name=name, out_shape=[shape, shape, shape], compiler_params=_cp())(w, g, m, v)


def _adamw_rows(land, j, rows, w, m, v, name):
    d = w.shape[1]

    def body(l_ref, w_ref, m_ref, v_ref, g_ref, d_ref, nm_ref, nv_ref):
        gv = l_ref[0].astype(F32)
        for s in range(1, N_DEV):
            gv = gv + l_ref[s].astype(F32)
        g_ref[...] = gv
        nm = ADAM_B1 * m_ref[...] + (1.0 - ADAM_B1) * gv
        nv = ADAM_B2 * v_ref[...] + (1.0 - ADAM_B2) * (gv * gv)
        m_hat = nm / (1.0 - ADAM_B1 ** ADAM_STEP)
        v_hat = nv / (1.0 - ADAM_B2 ** ADAM_STEP)
        d_ref[...] = -ADAM_LR * (m_hat / (jnp.sqrt(v_hat) + ADAM_EPS) + ADAM_WD * w_ref[...])
        nm_ref[...] = nm
        nv_ref[...] = nv

    whole = pl.BlockSpec((rows, d), lambda i: (0, 0))
    shape = jax.ShapeDtypeStruct((rows, d), F32)
    return pl.pallas_call(
        body, name=name, grid=(1,), out_shape=[shape] * 4,
        in_specs=[pl.BlockSpec((N_DEV, rows, d), lambda i: (0, j, 0)), whole, whole, whole], out_specs=[whole] * 4,
        compiler_params=_cp(("arbitrary",)),
    )(land, w, m, v)


def _sum_small(land, name):
    def body(l_ref, o_ref):
        acc = l_ref[0]
        for s in range(1, N_DEV):
            acc = acc + l_ref[s]
        o_ref[...] = acc

    return pl.pallas_call(body, name=name, out_shape=jax.ShapeDtypeStruct(land.shape[1:], F32), compiler_params=_cp())(land)


def _rope_lanes(positions):
    inv_freq = 1.0 / (ROPE_THETA ** (jnp.arange(0, HEAD_DIM, 2, dtype=F32) / HEAD_DIM))
    ang = positions.astype(F32)[:, None] * inv_freq
    cos, sin = jnp.cos(ang), jnp.sin(ang)
    return jnp.concatenate([cos, cos, cos, cos], axis=1), jnp.concatenate([-sin, sin, -sin, sin], axis=1)


def kernel(x, positions, norm_ffn1, w_gate1, w_up1, w_down1, norm_mix, w_in, a_sink, w_out, norm_ffn2, w_gate2, w_up2, w_down2, norm_final, loss_target, m_norm_ffn1, m_w_gate1, m_w_up1, m_w_down1, m_norm_mix, m_w_in, m_a_sink, m_w_out, m_norm_ffn2, m_w_gate2, m_w_up2, m_w_down2, m_norm_final, v_norm_ffn1, v_w_gate1, v_w_up1, v_w_down1, v_norm_mix, v_w_in, v_a_sink, v_w_out, v_norm_ffn2, v_w_gate2, v_w_up2, v_w_down2, v_norm_final):
    x = x[0]
    target = loss_target[0]
    t, d = x.shape
    tm = min(1024, t)
    tm_mix = min(512, t)
    tf = 256
    tq = min(512, t // 16)

    tt = min(1024, t)
    f_all = w_gate1.shape[2] * N_DEV
    tfp = f_all // 2
    n_steps = (t // tm) * (f_all // tf)

    def stacked(shards):
        return jnp.concatenate([s.astype(BF16) for s in shards], axis=0), [s.shape[0] for s in shards]

    packed1, rows1 = stacked([w_gate1[0].T, w_up1[0].T, w_down1[0]])
    packed2, rows2 = stacked([w_in[0].T, w_out[0], w_gate2[0].T, w_up2[0].T, w_down2[0]])
    wg1, wu1, wd1 = _run_alone(_gather_plan(packed1, rows1), "gather_ffn1_weights")

    cos, sin_signed = _rope_lanes(positions[0])
    sink = a_sink[0]
    no_sink = jnp.zeros_like(sink)

    (x1, h1, gp1, up1, hid1), (win, wout, wg2, wu2, wd2) = _ffn_fwd(
        x, norm_ffn1, wg1, wu1, wd1, "ffn1_fwd", tm, tf, _gather_plan(packed2, rows2), [0, (3 * n_steps) // 4, n_steps - 1])
    (h_mix, aq, akx, avx, bq, bk, bv), seg_qkv = _in_proj_fwd(x1, norm_mix, win, cos, sin_signed, "in_proj_fwd", tm_mix)
    a_out, a_lse = _attn_fwd(aq, akx, avx, sink, "attn_a_fwd", A_HALF_WINDOW, t, tq, True)
    rows_of = lambda a: a.reshape(t, B_W)
    segments_of = lambda a, dil: a if dil == 1 else a.reshape(dil, t // dil, B_W)
    b_qkv, b_outs, b_lses = [], [], []
    for n, (window, dil) in enumerate(B_PATTERNS):
        qs, ks, vs = (bq, bk, bv) if dil == 1 else (rows_of(seg_qkv[k][n - 1]) for k in range(3))
        o_seg, lse_seg = _attn_fwd(qs, ks, vs, no_sink, f"attn_b{dil}_fwd", window // (2 * dil), t // dil, tq, False)
        b_qkv.append((qs, ks, vs))
        b_outs.append(segments_of(o_seg, dil))
        b_lses.append(segments_of(lse_seg, dil))
    x2, b_out, b_lse = _merge_out_proj_fwd(x1, a_out, b_outs, b_lses, wout, "out_proj_fwd", tm_mix)
    (x3, h2, gp2, up2, hid2), _ = _ffn_fwd(x2, norm_ffn2, wg2, wu2, wd2, "ffn2_fwd", tm, tf)

    gfinal = norm_final.reshape(1, d)
    dx3, dg_final, sq, dout2 = _final_norm_loss(x3, gfinal, target, "final_norm_loss", tm)

    dgt2, dut2 = _ffn_bwd_hidden(dout2, gp2, up2, wd2, "ffn2_bwd_hidden", tm, tf)
    (dx2, dg_ffn2), _ = _ffn_bwd_input(dx3, x2, norm_ffn2, dgt2, dut2, wg2, wu2, "ffn2_bwd_input", tm_mix)
    (gwg2, gwu2), _ = _token_products([dgt2, dut2], h2, "ffn2_bwd_gate_up", tt, tfp)
    (gwd2,), _ = _token_products([hid2], dout2, "ffn2_bwd_down", tt, tfp)

    n_att = (A_Q_W // LANES) * (t // tq)
    n_prod = (f_all // tfp) * (t // tt)
    da, db, gwout = _out_proj_bwd(dx2, a_out, b_out[0], wout, "out_proj_bwd", tm_mix)
    (daq, dakx, davx, dsink_parts), (land_ffn2,) = _attn_bwd(
        aq, akx, avx, a_out, da, a_lse, sink, "attn_a_bwd", A_HALF_WINDOW, t, tq, True,
        _exchange_plan([[gwg2, gwu2, gwd2]]), [0, n_att - 1])
    pattern_grads = []
    for n, ((window, dil), (qs, ks, vs)) in enumerate(zip(B_PATTERNS, b_qkv)):
        grads, _ = _attn_bwd(qs, ks, vs, rows_of(b_out[n]), rows_of(db[n]), rows_of(b_lse[n]), no_sink, f"attn_b{dil}_bwd",
                             window // (2 * dil), t // dil, tq, False)
        pattern_grads.append(grads[:3])
    dbq, dbk, dbv = _sum_pattern_grads(pattern_grads, "sum_pattern_grads", tm_mix)
    dx1, dg_mix, gwin, dout1 = _in_proj_bwd(dx2, x1, norm_mix, win, h_mix, cos, sin_signed, daq, dakx, davx, dbq, dbk, dbv, "in_proj_bwd", tm_mix)

    (gwd1,), (land_in, land_out) = _token_products(
        [hid1], dout1, "ffn1_bwd_down", tt, tfp, _exchange_plan([[gwin], [gwout]]), [0, n_prod - 1])
    dgt1, dut1 = _ffn_bwd_hidden(dout1, gp1, up1, wd1, "ffn1_bwd_hidden", tm, tf)
    (gwg1, gwu1), (land_wd1,) = _token_products(
        [dgt1, dut1], h1, "ffn1_bwd_gate_up", tt, tfp, _exchange_plan([[gwd1]]), [0, n_prod - 1])
    (grad_x, dg_ffn1), (land_wgu1,) = _ffn_bwd_input(
        dx1, x, norm_ffn1, dgt1, dut1, wg1, wu1, "ffn1_bwd_input", tm_mix, _exchange_plan([[gwg1, gwu1]]), [0, t // tm_mix - 1])

    dsink_pairs = jnp.sum(dsink_parts[:, :, 0, :], axis=1)
    dsink = jnp.stack([dsink_pairs[:, 0], dsink_pairs[:, HEAD_DIM]], axis=1).reshape(1, -1)
    small = jnp.concatenate([dg_ffn1, dg_mix, dg_ffn2, dg_final, jnp.pad(dsink, ((0, 0), (0, d - dsink.shape[1]))),
                             sq, jnp.zeros((2, d), F32)], axis=0)
    (land_small,) = _run_alone(_exchange_plan([[jnp.tile(small, (N_DEV, 1))]]), "gather_small_gradients")
    red_small = _sum_small(land_small, "sum_small_grads")
    loss = 0.5 * jnp.sum(red_small[5]) / d

    rf = rows1[0]
    sharded = {"w_gate1": (land_wgu1, 0, rf, True), "w_up1": (land_wgu1, 1, rf, True), "w_down1": (land_wd1, 0, rf, False),
               "w_in": (land_in, 0, rows2[0], True), "w_out": (land_out, 0, rows2[1], False),
               "w_gate2": (land_ffn2, 0, rf, True), "w_up2": (land_ffn2, 1, rf, True), "w_down2": (land_ffn2, 2, rf, False)}
    n_sink = a_sink.shape[1]
    small_grads = {"norm_ffn1": red_small[0:1], "norm_mix": red_small[1:2], "norm_ffn2": red_small[2:3], "norm_final": red_small[3],
                   "a_sink": red_small[4:5, :n_sink]}
    params = {
        "norm_ffn1": (norm_ffn1, m_norm_ffn1, v_norm_ffn1), "w_gate1": (w_gate1, m_w_gate1, v_w_gate1),
        "w_up1": (w_up1, m_w_up1, v_w_up1), "w_down1": (w_down1, m_w_down1, v_w_down1),
        "norm_mix": (norm_mix, m_norm_mix, v_norm_mix), "w_in": (w_in, m_w_in, v_w_in),
        "a_sink": (a_sink, m_a_sink, v_a_sink), "w_out": (w_out, m_w_out, v_w_out),
        "norm_ffn2": (norm_ffn2, m_norm_ffn2, v_norm_ffn2), "w_gate2": (w_gate2, m_w_gate2, v_w_gate2),
        "w_up2": (w_up2, m_w_up2, v_w_up2), "w_down2": (w_down2, m_w_down2, v_w_down2),
        "norm_final": (norm_final, m_norm_final, v_norm_final),
    }
    grad_list, deltas, new_ms, new_vs = [], [], [], []
    for name, (w, m, v) in params.items():
        if name in sharded:
            land, j, rows, is_transposed = sharded[name]
            view = (lambda a: a[0].T) if is_transposed else (lambda a: a[0])
            back = (lambda a: a.T[None]) if is_transposed else (lambda a: a[None])
            outs = [back(o) for o in _adamw_rows(land, j, rows, view(w), view(m), view(v), f"adamw_{name}")]
        else:
            as_block = (lambda a: a.reshape(1, -1)) if w.ndim == 1 else (lambda a: a)
            g = small_grads[name]
            outs = [g] + [o.reshape(w.shape) for o in _adamw(as_block(w), as_block(g), as_block(m), as_block(v), f"adamw_{name}")]
        for lst, o in zip((grad_list, deltas, new_ms, new_vs), outs):
            lst.append(o)
    return (loss, grad_x[None], *grad_list, *deltas, *new_ms, *new_vs)
```

```python
import functools
import itertools

import numpy as np
import jax
import jax.numpy as jnp
from jax import lax
from jax.experimental import pallas as pl
from jax.experimental.pallas import tpu as pltpu

F32 = jnp.float32
BF16 = jnp.bfloat16

N_DEV = 8
HEAD_DIM = 64
LANES = 128
A_Q_W, A_KV_W, B_W = 512, 128, 512
A_HALF_WINDOW = 128
B_PATTERNS = ((128, 1), (512, 4), (2048, 16))
ROPE_THETA = 10000.0
NORM_EPS = 1e-6
FFN_RES_WEIGHT = 0.5
QK_SCALE = HEAD_DIM ** -0.5
NEG = -1e30

ADAM_LR = 0.001
ADAM_B1 = 0.9
ADAM_B2 = 0.999
ADAM_EPS = 1e-08
ADAM_WD = 0.01
ADAM_STEP = 10

MESH_T = pl.DeviceIdType.MESH
VMEM_LIMIT = 56 * 1024 * 1024


def _cp(sem=None, vmem=VMEM_LIMIT):
    return pltpu.CompilerParams(dimension_semantics=sem, vmem_limit_bytes=vmem)


def _dot_nn(a, b):
    return jnp.dot(a, b, preferred_element_type=F32)


def _dot_nt(a, b):
    return lax.dot_general(a, b, (((1,), (1,)), ((), ())), preferred_element_type=F32)


def _dot_tn(a, b):
    return lax.dot_general(a, b, (((0,), (0,)), ((), ())), preferred_element_type=F32)


def _rstd(xv):
    return lax.rsqrt(jnp.mean(xv * xv, axis=-1, keepdims=True) + NORM_EPS)


def _norm_bwd(dh, xv, r, gn):
    gy = dh * gn
    c = jnp.sum(gy * xv, axis=-1, keepdims=True) * (1.0 / xv.shape[-1])
    dx = r * gy - xv * (r * r * r * c)
    dgn = jnp.sum(dh * (xv * r), axis=0, keepdims=True)
    return dx, dgn


def _accumulate(ref, val, first):
    @pl.when(first)
    def _():
        ref[...] = val

    @pl.when(jnp.logical_not(first))
    def _():
        ref[...] += val


def _mesh_pos():
    return lax.axis_index("x"), lax.axis_index("y"), lax.axis_index("c")


def _dev_index(d):
    return 4 * d[0] + 2 * d[1] + d[2]


class _Comm:
    def __init__(self, inputs, out_shape, scratch, phases):
        self.inputs, self.out_shape, self.scratch, self.phases = inputs, out_shape, scratch, phases

    def specs(self):
        any_spec = pl.BlockSpec(memory_space=pl.ANY)
        return [any_spec] * len(self.inputs), [any_spec] * len(self.out_shape)


def _run_alone(comm, name):
    n_in, n_out = len(comm.inputs), len(comm.out_shape)

    def body(*refs):
        for phase in comm.phases:
            phase(refs[:n_in], refs[n_in:n_in + n_out], refs[n_in + n_out:])

    in_specs, out_specs = comm.specs()
    return pl.pallas_call(body, name=name, out_shape=comm.out_shape, in_specs=in_specs, out_specs=out_specs,
                          scratch_shapes=comm.scratch)(*comm.inputs)


def _run_hosted(comm, at, step, ins, outs, scr):
    for phase, when in zip(comm.phases, at):
        @pl.when(step == when)
        def _(phase=phase):
            phase(ins, outs, scr)


def _split(refs, *counts):
    parts, o = [], 0
    for n in counts:
        parts.append(refs[o:o + n])
        o += n
    return parts + [refs[o:]]


def _gather_plan(packed, rows_list):
    n_w = len(rows_list)
    offs = [int(o) for o in np.cumsum([0] + list(rows_list[:-1]))]
    d = packed.shape[1]

    def tools(ins, outs, scr):
        p_ref = ins[0]
        send_sems, recv_sems, local_sem = scr
        x, y, c = _mesh_pos()
        me, sibling = (x, y, c), (x, y, 1 - c)
        chips = [(1 - x, y), (x, 1 - y), (1 - x, 1 - y)]

        def rows(w, dev):
            start = pl.multiple_of(_dev_index(dev) * rows_list[w], 16)
            return outs[w].at[pl.ds(start, rows_list[w]), :]

        def mine(w):
            return p_ref.at[pl.ds(offs[w], rows_list[w]), :]

        def copy(k, w, block, to, own):
            return pltpu.make_async_remote_copy(
                src_ref=mine(w) if own else rows(w, block), dst_ref=rows(w, block),
                send_sem=send_sems.at[k], recv_sem=recv_sems.at[k], device_id=to, device_id_type=MESH_T)

        def all_blocks(k):
            return pltpu.make_async_remote_copy(
                src_ref=p_ref, dst_ref=p_ref, send_sem=send_sems.at[k], recv_sem=recv_sems.at[k],
                device_id=me, device_id_type=MESH_T)

        return p_ref, local_sem, me, sibling, chips, c, rows, mine, copy, all_blocks

    def start(ins, outs, scr):
        _, local_sem, me, sibling, chips, c, rows, mine, copy, _ = tools(ins, outs, scr)
        for w in range(n_w):
            pltpu.make_async_copy(mine(w), rows(w, me), local_sem).start()
        for w in range(n_w):
            copy(0, w, me, sibling, True).start()
        for j, chip in enumerate(chips):
            for w in range(n_w):
                copy(1 + j, w, me, (*chip, c), True).start()

    def relay(ins, outs, scr):
        _, _, _, sibling, chips, c, _, _, copy, all_blocks = tools(ins, outs, scr)
        for j, chip in enumerate(chips):
            all_blocks(1 + j).wait_recv()
            for w in range(n_w):
                copy(4 + j, w, (*chip, c), sibling, False).start()

    def finish(ins, outs, scr):
        p_ref, local_sem, _, _, _, _, _, _, _, all_blocks = tools(ins, outs, scr)
        all_blocks(0).wait_recv()
        for j in range(3):
            all_blocks(4 + j).wait_recv()
        for k in range(7):
            all_blocks(k).wait_send()
        pltpu.make_async_copy(p_ref, p_ref, local_sem).wait()

    return _Comm(
        [packed], [jax.ShapeDtypeStruct((N_DEV * r, d), packed.dtype) for r in rows_list],
        [pltpu.SemaphoreType.DMA((7,)), pltpu.SemaphoreType.DMA((7,)), pltpu.SemaphoreType.DMA], [start, relay, finish])


def _exchange_plan(groups):
    flat = [a for g in groups for a in g]
    n_g = len(groups)
    sizes = [len(g) for g in groups]
    rows = [g[0].shape[0] // N_DEV for g in groups]
    first = [int(o) for o in np.cumsum([0] + sizes[:-1])]

    def start(srcs, lands, scr):
        send_sems, recv_sems, local_sems = scr
        x, y, c = _mesh_pos()
        me = (x, y, c)
        me_idx = _dev_index(me)

        def block(g, i, dev):
            start_row = pl.multiple_of(_dev_index(dev) * rows[g], 8)
            return srcs[first[g] + i].at[pl.ds(start_row, rows[g]), :]

        def slot(g, i):
            return lands[g].at[me_idx, pl.ds(i * rows[g], rows[g]), :]

        for g in range(n_g):
            for i in range(sizes[g]):
                pltpu.make_async_copy(block(g, i, me), slot(g, i), local_sems.at[g]).start()
        flips = [f for f in itertools.product((0, 1), repeat=3) if any(f)]
        for k, (fx, fy, fc) in enumerate(flips):
            peer = (1 - x if fx else x, 1 - y if fy else y, 1 - c if fc else c)
            for g in range(n_g):
                for i in range(sizes[g]):
                    pltpu.make_async_remote_copy(
                        src_ref=block(g, i, peer), dst_ref=slot(g, i), send_sem=send_sems.at[g, k],
                        recv_sem=recv_sems.at[g, k], device_id=peer, device_id_type=MESH_T).start()

    def finish(srcs, lands, scr):
        send_sems, recv_sems, local_sems = scr
        me = _mesh_pos()
        for k in range(7):
            for g in range(n_g):
                pltpu.make_async_remote_copy(
                    src_ref=lands[g].at[0], dst_ref=lands[g].at[0], send_sem=send_sems.at[g, k],
                    recv_sem=recv_sems.at[g, k], device_id=me, device_id_type=MESH_T).wait()
        for g in range(n_g):
            pltpu.make_async_copy(lands[g].at[0], lands[g].at[0], local_sems.at[g]).wait()

    return _Comm(
        flat, [jax.ShapeDtypeStruct((N_DEV, sizes[g] * rows[g], groups[g][0].shape[1]), groups[g][0].dtype) for g in range(n_g)],
        [pltpu.SemaphoreType.DMA((n_g, 7)), pltpu.SemaphoreType.DMA((n_g, 7)), pltpu.SemaphoreType.DMA((n_g,))],
        [start, finish])


def _comm_parts(comm):
    if comm is None:
        return 0, 0, [], [], [], [], []
    in_specs, out_specs = comm.specs()
    return len(comm.inputs), len(comm.out_shape), in_specs, out_specs, comm.out_shape, comm.scratch, comm.inputs


def _ffn_fwd(x, gn, wg_t, wu_t, wd, name, tm, tf, comm=None, comm_at=None):
    t, d = x.shape
    f_all = wg_t.shape[0]
    n_f = f_all // tf
    n_ci, n_co, c_in_specs, c_out_specs, c_shapes, c_scratch, c_inputs = _comm_parts(comm)

    def body(*refs):
        ((x_ref, gn_ref, wg_ref, wu_ref, wd_ref), c_in, (y_ref, h_ref, gp_ref, up_ref, hid_ref), c_out,
         (h_s, hid_s), c_scr) = _split(refs, 5, n_ci, 5, n_co, 2)
        f = pl.program_id(1)
        if comm:
            _run_hosted(comm, comm_at, pl.program_id(0) * n_f + f, c_in, c_out, c_scr)

        @pl.when(f == 0)
        def _():
            xv = x_ref[...]
            h = ((xv * _rstd(xv)) * gn_ref[...]).astype(BF16)
            h_s[...] = h
            h_ref[...] = h

        h = h_s[...]
        g = _dot_nt(h, wg_ref[...])
        u = _dot_nt(h, wu_ref[...])
        gp_ref[...] = g.astype(BF16)
        up_ref[...] = u.astype(BF16)
        hid = ((g * jax.nn.sigmoid(g)) * u).astype(BF16)
        hid_ref[...] = hid
        for f0 in range(n_f):
            @pl.when(f == f0)
            def _(f0=f0):
                hid_s[:, f0 * tf:(f0 + 1) * tf] = hid

        @pl.when(f == n_f - 1)
        def _():
            y_ref[...] = x_ref[...] + FFN_RES_WEIGHT * _dot_nn(hid_s[...], wd_ref[...])

    tok = pl.BlockSpec((tm, d), lambda i, f: (i, 0))
    wblk = pl.BlockSpec((tf, d), lambda i, f: (f, 0))
    whole = pl.BlockSpec((f_all, d), lambda i, f: (0, 0))
    act = pl.BlockSpec((tm, tf), lambda i, f: (i, f))
    act_shape = jax.ShapeDtypeStruct((t, f_all), BF16)
    res = pl.pallas_call(
        body, name=name, grid=(t // tm, n_f),
        out_shape=[jax.ShapeDtypeStruct((t, d), F32), jax.ShapeDtypeStruct((t, d), BF16), act_shape, act_shape, act_shape] + c_shapes,
        in_specs=[tok, pl.BlockSpec((1, d), lambda i, f: (0, 0)), wblk, wblk, whole] + c_in_specs,
        out_specs=[tok, tok, act, act, act] + c_out_specs,
        scratch_shapes=[pltpu.VMEM((tm, d), BF16), pltpu.VMEM((tm, f_all), BF16)] + c_scratch,
        compiler_params=_cp(("arbitrary", "arbitrary")),
    )(x, gn, wg_t, wu_t, wd, *c_inputs)
    return res[:5], res[5:]


def _ffn_bwd_hidden(dout, gp, up, wd, name, tm, tf):
    t, d = dout.shape
    f_all = wd.shape[0]

    def body(dout_ref, gp_ref, up_ref, wd_ref, dg_ref, du_ref):
        dhid = _dot_nt(dout_ref[...], wd_ref[...])
        g = gp_ref[...].astype(F32)
        u = up_ref[...].astype(F32)
        sg = jax.nn.sigmoid(g)
        dg_ref[...] = ((dhid * u) * (sg * (1.0 + g * (1.0 - sg)))).astype(BF16)
        du_ref[...] = (dhid * (g * sg)).astype(BF16)

    tok = pl.BlockSpec((tm, d), lambda i, f: (i, 0))
    wblk = pl.BlockSpec((tf, d), lambda i, f: (f, 0))
    act = pl.BlockSpec((tm, tf), lambda i, f: (i, f))
    act_shape = jax.ShapeDtypeStruct((t, f_all), BF16)
    return pl.pallas_call(
        body, name=name, grid=(t // tm, f_all // tf),
        out_shape=[act_shape, act_shape], in_specs=[tok, act, act, wblk], out_specs=[act, act],
        compiler_params=_cp(("arbitrary", "arbitrary")),
    )(dout, gp, up, wd)


def _ffn_bwd_input(dy, x, gn, dg, du, wg_t, wu_t, name, tm, comm=None, comm_at=None):
    t, d = x.shape
    f_all = wg_t.shape[0]
    n_ci, n_co, c_in_specs, c_out_specs, c_shapes, c_scratch, c_inputs = _comm_parts(comm)

    def body(*refs):
        (dy_ref, x_ref, gn_ref, dg_ref, du_ref, wg_ref, wu_ref), c_in, (dx_ref, dgn_ref), c_out, c_scr = _split(
            refs, 7, n_ci, 2, n_co)
        i = pl.program_id(0)
        if comm:
            _run_hosted(comm, comm_at, i, c_in, c_out, c_scr)
        dh = _dot_nn(dg_ref[...], wg_ref[...]) + _dot_nn(du_ref[...], wu_ref[...])
        xv = x_ref[...]
        dxn, dgn = _norm_bwd(dh, xv, _rstd(xv), gn_ref[...])
        dx_ref[...] = dy_ref[...] + dxn
        _accumulate(dgn_ref, dgn, i == 0)

    tok = pl.BlockSpec((tm, d), lambda i: (i, 0))
    row = pl.BlockSpec((1, d), lambda i: (0, 0))
    act = pl.BlockSpec((tm, f_all), lambda i: (i, 0))
    whole = pl.BlockSpec((f_all, d), lambda i: (0, 0))
    res = pl.pallas_call(
        body, name=name, grid=(t // tm,),
        out_shape=[jax.ShapeDtypeStruct((t, d), F32), jax.ShapeDtypeStruct((1, d), F32)] + c_shapes,
        in_specs=[tok, tok, row, act, act, whole, whole] + c_in_specs,
        out_specs=[tok, row] + c_out_specs,
        scratch_shapes=c_scratch,
        compiler_params=_cp(("arbitrary",)),
    )(dy, x, gn, dg, du, wg_t, wu_t, *c_inputs)
    return res[:2], res[2:]


def _token_products(lhs_list, rhs, name, tt, tf, comm=None, comm_at=None):
    n_l = len(lhs_list)
    t, f_all = lhs_list[0].shape
    d = rhs.shape[1]
    n_t = t // tt
    n_ci, n_co, c_in_specs, c_out_specs, c_shapes, c_scratch, c_inputs = _comm_parts(comm)

    def body(*refs):
        lhs_refs, (rhs_ref,), c_in, out_refs, c_out, accs, c_scr = _split(refs, n_l, 1, n_ci, n_l, n_co, n_l)
        s = pl.program_id(1)
        if comm:
            _run_hosted(comm, comm_at, pl.program_id(0) * n_t + s, c_in, c_out, c_scr)
        rv = rhs_ref[...]
        for l_ref, acc in zip(lhs_refs, accs):
            _accumulate(acc, _dot_tn(l_ref[...], rv), s == 0)

        @pl.when(s == n_t - 1)
        def _():
            for o_ref, acc in zip(out_refs, accs):
                o_ref[...] = acc[...].astype(BF16)

    act = pl.BlockSpec((tt, tf), lambda f, s: (s, f))
    tok = pl.BlockSpec((tt, d), lambda f, s: (s, 0))
    wblk = pl.BlockSpec((tf, d), lambda f, s: (f, 0))
    res = pl.pallas_call(
        body, name=name, grid=(f_all // tf, n_t),
        out_shape=[jax.ShapeDtypeStruct((f_all, d), BF16)] * n_l + c_shapes,
        in_specs=[act] * n_l + [tok] + c_in_specs, out_specs=[wblk] * n_l + c_out_specs,
        scratch_shapes=[pltpu.VMEM((tf, d), F32)] * n_l + c_scratch,
        compiler_params=_cp(("arbitrary", "arbitrary")),
    )(*lhs_list, rhs, *c_inputs)
    return res[:n_l], res[n_l:]


def _swap_halves(t):
    w = t.shape[-1]
    lane = lax.broadcasted_iota(jnp.int32, (1, w), 1)
    return jnp.where((lane % HEAD_DIM) < HEAD_DIM // 2, pltpu.roll(t, w - HEAD_DIM // 2, 1), pltpu.roll(t, HEAD_DIM // 2, 1))


def _rope(t, cos, sin_signed):
    reps = t.shape[-1] // LANES
    return t * jnp.tile(cos, (1, reps)) + _swap_halves(t) * jnp.tile(sin_signed, (1, reps))


def _rope_bwd(dt, cos, sin_signed):
    reps = dt.shape[-1] // LANES
    return dt * jnp.tile(cos, (1, reps)) + _swap_halves(dt * jnp.tile(sin_signed, (1, reps)))


DILATIONS = tuple(dil for _, dil in B_PATTERNS if dil > 1)


def _seg_shape(t, dil, w, dtype):
    return jax.ShapeDtypeStruct((dil, t // dil, w), dtype)


def _seg_spec(tm, dil, w):
    return pl.BlockSpec((dil, tm // dil, w), lambda i: (0, i, 0))


def _tile_scratch(tm, w):
    return [pltpu.VMEM((tm, LANES), F32)] * (w // LANES)


def _put_tile(tile, val):
    for c, ref in enumerate(tile):
        ref[...] = val[:, c * LANES:(c + 1) * LANES]


def _get_tile(tile):
    return jnp.concatenate([ref[...] for ref in tile], axis=1)


def _scatter_to_segments(tile, seg_refs):
    for seg_ref, dil in zip(seg_refs, DILATIONS):
        rows = tile[0].shape[0] // dil
        for r in range(dil):
            for c, ref in enumerate(tile):
                seg_ref[r, :, c * LANES:(c + 1) * LANES] = ref[pl.ds(r, rows, stride=dil), :].astype(seg_ref.dtype)


def _gather_from_segments(seg_ref, dil, tile, add=False):
    rows = tile[0].shape[0] // dil
    for r in range(dil):
        for c, ref in enumerate(tile):
            idx = (pl.ds(r, rows, stride=dil), slice(None))
            v = seg_ref[r, :, c * LANES:(c + 1) * LANES].astype(F32)
            ref[idx] = ref[idx] + v if add else v


def _in_proj_fwd(x, gn, win_t, cos, sin_signed, name, tm):
    t, d = x.shape
    in_w = win_t.shape[0]
    n_dil = len(DILATIONS)

    def body(x_ref, gn_ref, w_ref, cos_ref, sin_ref, h_ref, aq_ref, akx_ref, avx_ref, bq_ref, bk_ref, bv_ref, *rest):
        seg_refs, tile = rest[:3 * n_dil], rest[3 * n_dil:]
        xv = x_ref[...]
        h = ((xv * _rstd(xv)) * gn_ref[...]).astype(BF16)
        h_ref[...] = h
        p = _dot_nt(h, w_ref[...])
        cs, sn = cos_ref[...], sin_ref[...]
        o = 0
        aq_ref[...] = (_rope(p[:, o:o + A_Q_W], cs, sn) * QK_SCALE).astype(BF16)
        o += A_Q_W
        ak = _rope(p[:, o:o + A_KV_W], cs, sn)
        o += A_KV_W
        av = p[:, o:o + A_KV_W]
        o += A_KV_W
        low = lax.broadcasted_iota(jnp.int32, (1, LANES), 1) < HEAD_DIM
        for src, dst in ((ak, akx_ref), (av, avx_ref)):
            other = pltpu.roll(src, HEAD_DIM, 1)
            dst[0] = jnp.where(low, src, other).astype(BF16)
            dst[1] = jnp.where(low, other, src).astype(BF16)
        for k, nat_ref in enumerate((bq_ref, bk_ref, bv_ref)):
            val = p[:, o:o + B_W]
            o += B_W
            if k < 2:
                val = _rope(val, cs, sn)
            if k == 0:
                val = val * QK_SCALE
            nat_ref[...] = val.astype(BF16)
            _put_tile(tile, val)
            _scatter_to_segments(tile, seg_refs[k * n_dil:(k + 1) * n_dil])

    tok = lambda w: pl.BlockSpec((tm, w), lambda i: (i, 0))
    kvx = pl.BlockSpec((2, tm, LANES), lambda i: (0, i, 0))
    sd = lambda *s: jax.ShapeDtypeStruct(s, BF16)
    res = pl.pallas_call(
        body, name=name, grid=(t // tm,),
        out_shape=[sd(t, d), sd(t, A_Q_W), sd(2, t, LANES), sd(2, t, LANES), sd(t, B_W), sd(t, B_W), sd(t, B_W)]
        + [_seg_shape(t, dil, B_W, BF16) for _ in range(3) for dil in DILATIONS],
        in_specs=[tok(d), pl.BlockSpec((1, d), lambda i: (0, 0)), pl.BlockSpec((in_w, d), lambda i: (0, 0)),
                  tok(LANES), tok(LANES)],
        out_specs=[tok(d), tok(A_Q_W), kvx, kvx, tok(B_W), tok(B_W), tok(B_W)]
        + [_seg_spec(tm, dil, B_W) for _ in range(3) for dil in DILATIONS],
        scratch_shapes=_tile_scratch(tm, B_W),
        compiler_params=_cp(("arbitrary",)),
    )(x, gn, win_t, cos, sin_signed)
    return res[:7], [res[7 + k * n_dil:7 + (k + 1) * n_dil] for k in range(3)]


def _in_proj_bwd(dres, x, gn, win_t, h, cos, sin_signed, daq, dakx, davx, dbq, dbk, dbv, name, tm):
    t, d = x.shape
    in_w = win_t.shape[0]
    n_t = t // tm
    chunk = in_w // 3

    def body(dres_ref, x_ref, gn_ref, w_ref, h_ref, cos_ref, sin_ref, daq_ref, dakx_ref, davx_ref, dbq_ref, dbk_ref,
             dbv_ref, dx_ref, dgn_ref, gw_ref, half_ref, dp_s, gw_s):
        i = pl.program_id(0)
        cs, sn = cos_ref[...], sin_ref[...]
        low = lax.broadcasted_iota(jnp.int32, (1, LANES), 1) < HEAD_DIM

        def fold(ref):
            a, b = ref[0], ref[1]
            return jnp.where(low, a + pltpu.roll(a, HEAD_DIM, 1), b + pltpu.roll(b, HEAD_DIM, 1))

        o = 0
        dp_s[:, o:o + A_Q_W] = _rope_bwd(daq_ref[...] * QK_SCALE, cs, sn).astype(BF16)
        o += A_Q_W
        dp_s[:, o:o + A_KV_W] = _rope_bwd(fold(dakx_ref), cs, sn).astype(BF16)
        o += A_KV_W
        dp_s[:, o:o + A_KV_W] = fold(davx_ref).astype(BF16)
        o += A_KV_W
        dp_s[:, o:o + B_W] = _rope_bwd(dbq_ref[...] * QK_SCALE, cs, sn).astype(BF16)
        o += B_W
        dp_s[:, o:o + B_W] = _rope_bwd(dbk_ref[...], cs, sn).astype(BF16)
        o += B_W
        dp_s[:, o:o + B_W] = dbv_ref[...].astype(BF16)
        dh = _dot_nn(dp_s[...], w_ref[...])
        hv = h_ref[...]
        for c0 in range(0, in_w, chunk):
            _accumulate(gw_s.at[pl.ds(c0, chunk), :], _dot_tn(dp_s[:, c0:c0 + chunk], hv), i == 0)
        xv = x_ref[...]
        dxn, dgn = _norm_bwd(dh, xv, _rstd(xv), gn_ref[...])
        dx = dres_ref[...] + dxn
        dx_ref[...] = dx
        half_ref[...] = (FFN_RES_WEIGHT * dx).astype(BF16)
        _accumulate(dgn_ref, dgn, i == 0)

        @pl.when(i == n_t - 1)
        def _():
            gw_ref[...] = gw_s[...].astype(BF16)

    tok = lambda w: pl.BlockSpec((tm, w), lambda i: (i, 0))
    row = pl.BlockSpec((1, d), lambda i: (0, 0))
    whole = pl.BlockSpec((in_w, d), lambda i: (0, 0))
    kvx = pl.BlockSpec((2, tm, LANES), lambda i: (0, i, 0))
    return pl.pallas_call(
        body, name=name, grid=(n_t,),
        out_shape=[jax.ShapeDtypeStruct((t, d), F32), jax.ShapeDtypeStruct((1, d), F32),
                   jax.ShapeDtypeStruct((in_w, d), BF16), jax.ShapeDtypeStruct((t, d), BF16)],
        in_specs=[tok(d), tok(d), row, whole, tok(d), tok(LANES), tok(LANES), tok(A_Q_W), kvx, kvx,
                  tok(B_W), tok(B_W), tok(B_W)],
        out_specs=[tok(d), row, whole, tok(d)],
        scratch_shapes=[pltpu.VMEM((tm, in_w), BF16), pltpu.VMEM((in_w, d), F32)],
        compiler_params=_cp(("arbitrary",)),
    )(dres, x, gn, win_t, h, cos, sin_signed, daq, dakx, davx, dbq, dbk, dbv)


def _merge_out_proj_fwd(x, a_out, outs, lses, wout, name, tm):
    t, d = x.shape
    n_dil = len(DILATIONS)

    def body(x_ref, a_ref, *rest):
        o_refs, l_refs, (w_ref, y_ref, b_ref, lt_ref), b_segs, lt_segs, scratch = _split(rest, 1 + n_dil, 1 + n_dil, 4, n_dil, n_dil)
        n_c = B_W // LANES
        tiles = [scratch[j * n_c:(j + 1) * n_c] for j in range(2 * n_dil)]
        os_, ls = [o_refs[0][...].astype(F32)], [l_refs[0][...]]
        for k, dil in enumerate(DILATIONS):
            _gather_from_segments(o_refs[1 + k], dil, tiles[2 * k])
            _gather_from_segments(l_refs[1 + k], dil, tiles[2 * k + 1])
            os_.append(_get_tile(tiles[2 * k]))
            ls.append(_get_tile(tiles[2 * k + 1]))
        mx = functools.reduce(jnp.maximum, ls)
        es = [jnp.exp(l - mx) for l in ls]
        den = functools.reduce(jnp.add, es)
        b = functools.reduce(jnp.add, [e * o for e, o in zip(es, os_)]) / den
        lt = mx + jnp.log(den)
        bb = b.astype(BF16)
        b_ref[...] = bb
        lt_ref[...] = lt
        y_ref[...] = x_ref[...] + _dot_nn(a_ref[...], w_ref[0:A_Q_W, :]) + _dot_nn(bb, w_ref[A_Q_W:A_Q_W + B_W, :])
        _put_tile(tiles[0], b)
        _scatter_to_segments(tiles[0], b_segs)
        _put_tile(tiles[1], lt)
        _scatter_to_segments(tiles[1], lt_segs)

    tok = lambda w: pl.BlockSpec((tm, w), lambda i: (i, 0))
    segs = [_seg_spec(tm, dil, B_W) for dil in DILATIONS]
    res = pl.pallas_call(
        body, name=name, grid=(t // tm,),
        out_shape=[jax.ShapeDtypeStruct((t, d), F32), jax.ShapeDtypeStruct((t, B_W), BF16), jax.ShapeDtypeStruct((t, B_W), F32)]
        + [_seg_shape(t, dil, B_W, BF16) for dil in DILATIONS] + [_seg_shape(t, dil, B_W, F32) for dil in DILATIONS],
        in_specs=[tok(d), tok(A_Q_W)] + ([tok(B_W)] + segs) * 2 + [pl.BlockSpec(wout.shape, lambda i: (0, 0))],
        out_specs=[tok(d), tok(B_W), tok(B_W)] + segs * 2,
        scratch_shapes=_tile_scratch(tm, B_W) * (2 * n_dil),
        compiler_params=_cp(("arbitrary",)),
    )(x, a_out, *outs, *lses, wout)
    return res[0], [res[1]] + list(res[3:3 + n_dil]), [res[2]] + list(res[3 + n_dil:])


def _out_proj_bwd(dy, a_out, b_out, wout, name, tm):
    t, d = dy.shape
    n_t = t // tm
    n_dil = len(DILATIONS)

    def body(dy_ref, a_ref, b_ref, w_ref, da_ref, db_ref, gw_ref, *rest):
        db_segs, gw_s, tile = rest[:n_dil], rest[n_dil], rest[n_dil + 1:]
        i = pl.program_id(0)
        dyb = dy_ref[...].astype(BF16)
        da_ref[...] = _dot_nt(dyb, w_ref[0:A_Q_W, :]).astype(BF16)
        db = _dot_nt(dyb, w_ref[A_Q_W:A_Q_W + B_W, :])
        db_ref[...] = db.astype(BF16)
        _put_tile(tile, db)
        _scatter_to_segments(tile, db_segs)
        ga = _dot_tn(a_ref[...], dyb)
        gb = _dot_tn(b_ref[...], dyb)

        @pl.when(i == 0)
        def _():
            gw_s[0:A_Q_W, :] = ga
            gw_s[A_Q_W:A_Q_W + B_W, :] = gb

        @pl.when(i > 0)
        def _():
            gw_s[0:A_Q_W, :] += ga
            gw_s[A_Q_W:A_Q_W + B_W, :] += gb

        @pl.when(i == n_t - 1)
        def _():
            gw_ref[...] = gw_s[...].astype(BF16)

    tok = lambda w: pl.BlockSpec((tm, w), lambda i: (i, 0))
    whole = pl.BlockSpec(wout.shape, lambda i: (0, 0))
    res = pl.pallas_call(
        body, name=name, grid=(n_t,),
        out_shape=[jax.ShapeDtypeStruct((t, A_Q_W), BF16), jax.ShapeDtypeStruct((t, B_W), BF16),
                   jax.ShapeDtypeStruct(wout.shape, BF16)] + [_seg_shape(t, dil, B_W, BF16) for dil in DILATIONS],
        in_specs=[tok(d), tok(A_Q_W), tok(B_W), whole],
        out_specs=[tok(A_Q_W), tok(B_W), whole] + [_seg_spec(tm, dil, B_W) for dil in DILATIONS],
        scratch_shapes=[pltpu.VMEM(wout.shape, F32)] + _tile_scratch(tm, B_W),
        compiler_params=_cp(("arbitrary",)),
    )(dy, a_out, b_out, wout)
    return res[0], [res[1]] + list(res[3:]), res[2]


def _sum_pattern_grads(per_pattern, name, tm):
    t = per_pattern[0][0].shape[0]
    n_dil = len(DILATIONS)

    def body(*refs):
        ins, outs, tile = _split(refs, 3 * (1 + n_dil), 3)
        for j, o_ref in enumerate(outs):
            _put_tile(tile, ins[j][...])
            for k, dil in enumerate(DILATIONS):
                _gather_from_segments(ins[3 * (1 + k) + j], dil, tile, add=True)
            o_ref[...] = _get_tile(tile)

    tok = pl.BlockSpec((tm, B_W), lambda i: (i, 0))
    flat = [a if k == 0 else a.reshape(DILATIONS[k - 1], -1, B_W) for k, grads in enumerate(per_pattern) for a in grads]
    return pl.pallas_call(
        body, name=name, grid=(t // tm,),
        out_shape=[jax.ShapeDtypeStruct((t, B_W), F32)] * 3,
        in_specs=[tok] * 3 + [_seg_spec(tm, dil, B_W) for dil in DILATIONS for _ in range(3)],
        out_specs=[tok] * 3, scratch_shapes=_tile_scratch(tm, B_W),
        compiler_params=_cp(("arbitrary",)),
    )(*flat)


SUB_ROWS = 64


def _sub_band(r0, hw, win, seg_lo, seg_len, t, rel, col):
    ks = pl.multiple_of(jnp.clip(r0 - hw, 0, t - win), SUB_ROWS)
    kpos = col + ks
    valid = (jnp.abs(rel + (ks - r0)) <= hw) & (kpos >= seg_lo) & (kpos < seg_lo + seg_len)
    return ks, valid


def _split_heads(v, low):
    zero = jnp.zeros_like(v)
    return jnp.concatenate([jnp.where(low, v, zero), jnp.where(low, zero, v)], axis=0)


def _kv_spec(kv, t):
    if kv.ndim == 3:
        return pl.BlockSpec((None, t, LANES), lambda p, i: (p // 2, 0, 0))
    return pl.BlockSpec((t, LANES), lambda p, i: (0, p))


def _attn_fwd(q, k, v, sink, name, hw, seg_len, tq, has_sink):
    t, width = q.shape
    sb = SUB_ROWS
    win = 2 * hw + LANES

    def body(sink_ref, q_ref, k_ref, v_ref, o_ref, lse_ref):
        p, i = pl.program_id(0), pl.program_id(1)
        q0 = i * tq
        seg_lo = (q0 // seg_len) * seg_len
        low = lax.broadcasted_iota(jnp.int32, (1, LANES), 1) < HEAD_DIM
        rel = lax.broadcasted_iota(jnp.int32, (sb, win), 1) - lax.broadcasted_iota(jnp.int32, (sb, win), 0)
        col = lax.broadcasted_iota(jnp.int32, (1, win), 1)
        subs = []
        for j in range(tq // sb):
            rows = pl.ds(j * sb, sb)
            ks, valid = _sub_band(q0 + j * sb, hw, win, seg_lo, seg_len, t, rel, col)
            subs.append((rows, ks, valid, _dot_nt(_split_heads(q_ref[rows, :], low), k_ref[pl.ds(ks, win), :])))
        for rows, ks, valid, s in subs:
            vw = v_ref[pl.ds(ks, win), :]
            es, inv, lses = [], [], []
            for a in range(2):
                sa = jnp.where(valid, s[a * sb:(a + 1) * sb], NEG)
                m = jnp.max(sa, axis=1, keepdims=True)
                if has_sink:
                    sk = sink_ref[2 * p + a]
                    m = jnp.maximum(m, sk)
                e = jnp.exp(sa - m)
                den = jnp.sum(e, axis=1, keepdims=True)
                if has_sink:
                    den = den + jnp.exp(sk - m)
                es.append(e.astype(BF16))
                inv.append(1.0 / den)
                lses.append(m + jnp.log(den))
            pv = _dot_nn(jnp.concatenate(es, axis=0), vw)
            o_ref[rows, :] = jnp.where(low, pv[0:sb] * inv[0], pv[sb:2 * sb] * inv[1]).astype(BF16)
            lse_ref[rows, :] = jnp.where(low, lses[0], lses[1])

    tile = pl.BlockSpec((tq, LANES), lambda p, i: (i, p))
    return pl.pallas_call(
        body, name=name, grid=(width // LANES, t // tq),
        out_shape=[jax.ShapeDtypeStruct((t, width), BF16), jax.ShapeDtypeStruct((t, width), F32)],
        in_specs=[pl.BlockSpec(memory_space=pltpu.SMEM), tile, _kv_spec(k, t), _kv_spec(v, t)],
        out_specs=[tile, tile],
        compiler_params=_cp(("arbitrary", "arbitrary")),
    )(sink, q, k, v)


def _attn_bwd(q, k, v, o, do, lse, sink, name, hw, seg_len, tq, has_sink, comm=None, comm_at=None):
    t, width = q.shape
    sb = SUB_ROWS
    win = 2 * hw + LANES
    n_q = t // tq
    shared_kv = k.ndim == 3
    n_ci, n_co, c_in_specs, c_out_specs, c_shapes, c_scratch, c_inputs = _comm_parts(comm)

    def body(*refs):
        ((sink_ref, q_ref, k_ref, v_ref, o_ref, do_ref, lse_ref), c_in, (dq_ref, dk_ref, dv_ref, ds_ref), c_out, c_scr) = _split(
            refs, 7, n_ci, 4, n_co)
        p, i = pl.program_id(0), pl.program_id(1)
        if comm:
            _run_hosted(comm, comm_at, p * n_q + i, c_in, c_out, c_scr)
        fresh = (i == 0) & (p % 2 == 0) if shared_kv else i == 0

        @pl.when(fresh)
        def _():
            dk_ref[...] = jnp.zeros_like(dk_ref)
            dv_ref[...] = jnp.zeros_like(dv_ref)

        q0 = i * tq
        seg_lo = (q0 // seg_len) * seg_len
        low = lax.broadcasted_iota(jnp.int32, (1, LANES), 1) < HEAD_DIM
        rel = lax.broadcasted_iota(jnp.int32, (sb, win), 1) - lax.broadcasted_iota(jnp.int32, (sb, win), 0)
        col = lax.broadcasted_iota(jnp.int32, (1, win), 1)
        dsink = [jnp.zeros((1, 1), F32), jnp.zeros((1, 1), F32)]
        subs = []
        for j in range(tq // sb):
            rows = pl.ds(j * sb, sb)
            ks, valid = _sub_band(q0 + j * sb, hw, win, seg_lo, seg_len, t, rel, col)
            kw = k_ref[pl.ds(ks, win), :]
            dov = do_ref[rows, :]
            q2 = _split_heads(q_ref[rows, :], low)
            do2 = _split_heads(dov, low)
            subs.append((rows, ks, valid, kw, dov, q2, do2, _dot_nt(q2, kw), _dot_nt(do2, v_ref[pl.ds(ks, win), :])))
        probs = []
        for rows, ks, valid, kw, dov, q2, do2, s, dpr in subs:
            prod = dov.astype(F32) * o_ref[rows, :].astype(F32)
            lse_t = lse_ref[rows, :]
            prs, dss = [], []
            for a in range(2):
                mine = low if a == 0 else jnp.logical_not(low)
                lse_a = jnp.max(jnp.where(mine, lse_t, -jnp.inf), axis=1, keepdims=True)
                delta = jnp.sum(jnp.where(mine, prod, 0.0), axis=1, keepdims=True)
                pr = jnp.exp(jnp.where(valid, s[a * sb:(a + 1) * sb], NEG) - lse_a)
                prs.append(pr.astype(BF16))
                dss.append((pr * (dpr[a * sb:(a + 1) * sb] - delta)).astype(BF16))
                if has_sink:
                    dsink[a] = dsink[a] - jnp.sum(jnp.exp(sink_ref[2 * p + a] - lse_a) * delta, axis=0, keepdims=True)
            probs.append((jnp.concatenate(prs, axis=0), jnp.concatenate(dss, axis=0)))
        for (rows, ks, valid, kw, dov, q2, do2, s, dpr), (pr2, ds2) in zip(subs, probs):
            dv_ref[pl.ds(ks, win), :] += _dot_tn(pr2, do2)
            dk_ref[pl.ds(ks, win), :] += _dot_tn(ds2, q2)
            dq2 = _dot_nn(ds2, kw)
            dq_ref[rows, :] = jnp.where(low, dq2[0:sb], dq2[sb:2 * sb])
        ds_ref[...] = jnp.broadcast_to(jnp.where(low, dsink[0], dsink[1]), ds_ref.shape)

    tile = pl.BlockSpec((tq, LANES), lambda p, i: (i, p))
    kv_shape = jax.ShapeDtypeStruct(k.shape, F32)
    res = pl.pallas_call(
        body, name=name, grid=(width // LANES, n_q),
        out_shape=[jax.ShapeDtypeStruct((t, width), F32), kv_shape, kv_shape,
                   jax.ShapeDtypeStruct((width // LANES, n_q, 8, LANES), F32)] + c_shapes,
        in_specs=[pl.BlockSpec(memory_space=pltpu.SMEM), tile, _kv_spec(k, t), _kv_spec(v, t), tile, tile, tile] + c_in_specs,
        out_specs=[tile, _kv_spec(k, t), _kv_spec(v, t), pl.BlockSpec((None, None, 8, LANES), lambda p, i: (p, i, 0, 0))] + c_out_specs,
        scratch_shapes=c_scratch,
        compiler_params=_cp(("arbitrary", "arbitrary")),
    )(sink, q, k, v, o, do, lse, *c_inputs)
    return res[:4], res[4:]


def _final_norm_loss(x, gn, target, name, tm):
    t, d = x.shape

    def body(x_ref, gn_ref, tg_ref, dx_ref, dgn_ref, sq_ref, half_ref):
        i = pl.program_id(0)
        xv, gnv = x_ref[...], gn_ref[...]
        r = _rstd(xv)
        diff = (xv * r) * gnv - tg_ref[...]
        dxn, dgn = _norm_bwd(diff * (1.0 / d), xv, r, gnv)
        dx_ref[...] = dxn
        half_ref[...] = (FFN_RES_WEIGHT * dxn).astype(BF16)
        _accumulate(dgn_ref, dgn, i == 0)
        _accumulate(sq_ref, jnp.sum(diff * diff, axis=0, keepdims=True), i == 0)

    tok = pl.BlockSpec((tm, d), lambda i: (i, 0))
    row = pl.BlockSpec((1, d), lambda i: (0, 0))
    return pl.pallas_call(
        body, name=name, grid=(t // tm,),
        out_shape=[jax.ShapeDtypeStruct((t, d), F32), jax.ShapeDtypeStruct((1, d), F32), jax.ShapeDtypeStruct((1, d), F32),
                   jax.ShapeDtypeStruct((t, d), BF16)],
        in_specs=[tok, row, tok], out_specs=[tok, row, row, tok],
        compiler_params=_cp(("arbitrary",)),
    )(x, gn, target)


def _adamw(w, g, m, v, name):
    def body(w_ref, g_ref, m_ref, v_ref, d_ref, nm_ref, nv_ref):
        gv = g_ref[...]
        nm = ADAM_B1 * m_ref[...] + (1.0 - ADAM_B1) * gv
        nv = ADAM_B2 * v_ref[...] + (1.0 - ADAM_B2) * (gv * gv)
        m_hat = nm / (1.0 - ADAM_B1 ** ADAM_STEP)
        v_hat = nv / (1.0 - ADAM_B2 ** ADAM_STEP)
        d_ref[...] = -ADAM_LR * (m_hat / (jnp.sqrt(v_hat) + ADAM_EPS) + ADAM_WD * w_ref[...])
        nm_ref[...] = nm
        nv_ref[...] = nv

    shape = jax.ShapeDtypeStruct(w.shape, F32)
    return pl.pallas_call(body, name=name, out_shape=[shape, shape, shape], compiler_params=_cp())(w, g, m, v)


def _adamw_rows(land, j, rows, w, m, v, name):
    d = w.shape[1]

    def body(l_ref, w_ref, m_ref, v_ref, g_ref, d_ref, nm_ref, nv_ref):
        gv = l_ref[0].astype(F32)
        for s in range(1, N_DEV):
            gv = gv + l_ref[s].astype(F32)
        g_ref[...] = gv
        nm = ADAM_B1 * m_ref[...] + (1.0 - ADAM_B1) * gv
        nv = ADAM_B2 * v_ref[...] + (1.0 - ADAM_B2) * (gv * gv)
        m_hat = nm / (1.0 - ADAM_B1 ** ADAM_STEP)
        v_hat = nv / (1.0 - ADAM_B2 ** ADAM_STEP)
        d_ref[...] = -ADAM_LR * (m_hat / (jnp.sqrt(v_hat) + ADAM_EPS) + ADAM_WD * w_ref[...])
        nm_ref[...] = nm
        nv_ref[...] = nv

    whole = pl.BlockSpec((rows, d), lambda i: (0, 0))
    shape = jax.ShapeDtypeStruct((rows, d), F32)
    return pl.pallas_call(
        body, name=name, grid=(1,), out_shape=[shape] * 4,
        in_specs=[pl.BlockSpec((N_DEV, rows, d), lambda i: (0, j, 0)), whole, whole, whole], out_specs=[whole] * 4,
        compiler_params=_cp(("arbitrary",)),
    )(land, w, m, v)


def _sum_small(land, name):
    def body(l_ref, o_ref):
        acc = l_ref[0]
        for s in range(1, N_DEV):
            acc = acc + l_ref[s]
        o_ref[...] = acc

    return pl.pallas_call(body, name=name, out_shape=jax.ShapeDtypeStruct(land.shape[1:], F32), compiler_params=_cp())(land)


def _rope_lanes(positions):
    inv_freq = 1.0 / (ROPE_THETA ** (jnp.arange(0, HEAD_DIM, 2, dtype=F32) / HEAD_DIM))
    ang = positions.astype(F32)[:, None] * inv_freq
    cos, sin = jnp.cos(ang), jnp.sin(ang)
    return jnp.concatenate([cos, cos, cos, cos], axis=1), jnp.concatenate([-sin, sin, -sin, sin], axis=1)


def kernel(x, positions, norm_ffn1, w_gate1, w_up1, w_down1, norm_mix, w_in, a_sink, w_out, norm_ffn2, w_gate2, w_up2, w_down2, norm_final, loss_target, m_norm_ffn1, m_w_gate1, m_w_up1, m_w_down1, m_norm_mix, m_w_in, m_a_sink, m_w_out, m_norm_ffn2, m_w_gate2, m_w_up2, m_w_down2, m_norm_final, v_norm_ffn1, v_w_gate1, v_w_up1, v_w_down1, v_norm_mix, v_w_in, v_a_sink, v_w_out, v_norm_ffn2, v_w_gate2, v_w_up2, v_w_down2, v_norm_final):
    x = x[0]
    target = loss_target[0]
    t, d = x.shape
    tm = min(1024, t)
    tm_mix = min(512, t)
    tf = 256
    tq = min(512, t // 16)

    tt = min(1024, t)
    f_all = w_gate1.shape[2] * N_DEV
    tfp = f_all // 2
    n_steps = (t // tm) * (f_all // tf)

    def stacked(shards):
        return jnp.concatenate([s.astype(BF16) for s in shards], axis=0), [s.shape[0] for s in shards]

    packed1, rows1 = stacked([w_gate1[0].T, w_up1[0].T, w_down1[0]])
    packed2, rows2 = stacked([w_in[0].T, w_out[0], w_gate2[0].T, w_up2[0].T, w_down2[0]])
    wg1, wu1, wd1 = _run_alone(_gather_plan(packed1, rows1), "gather_ffn1_weights")

    cos, sin_signed = _rope_lanes(positions[0])
    sink = a_sink[0]
    no_sink = jnp.zeros_like(sink)

    (x1, h1, gp1, up1, hid1), (win, wout, wg2, wu2, wd2) = _ffn_fwd(
        x, norm_ffn1, wg1, wu1, wd1, "ffn1_fwd", tm, tf, _gather_plan(packed2, rows2), [0, (3 * n_steps) // 4, n_steps - 1])
    (h_mix, aq, akx, avx, bq, bk, bv), seg_qkv = _in_proj_fwd(x1, norm_mix, win, cos, sin_signed, "in_proj_fwd", tm_mix)
    a_out, a_lse = _attn_fwd(aq, akx, avx, sink, "attn_a_fwd", A_HALF_WINDOW, t, tq, True)
    rows_of = lambda a: a.reshape(t, B_W)
    segments_of = lambda a, dil: a if dil == 1 else a.reshape(dil, t // dil, B_W)
    b_qkv, b_outs, b_lses = [], [], []
    for n, (window, dil) in enumerate(B_PATTERNS):
        qs, ks, vs = (bq, bk, bv) if dil == 1 else (rows_of(seg_qkv[k][n - 1]) for k in range(3))
        o_seg, lse_seg = _attn_fwd(qs, ks, vs, no_sink, f"attn_b{dil}_fwd", window // (2 * dil), t // dil, tq, False)
        b_qkv.append((qs, ks, vs))
        b_outs.append(segments_of(o_seg, dil))
        b_lses.append(segments_of(lse_seg, dil))
    x2, b_out, b_lse = _merge_out_proj_fwd(x1, a_out, b_outs, b_lses, wout, "out_proj_fwd", tm_mix)
    (x3, h2, gp2, up2, hid2), _ = _ffn_fwd(x2, norm_ffn2, wg2, wu2, wd2, "ffn2_fwd", tm, tf)

    gfinal = norm_final.reshape(1, d)
    dx3, dg_final, sq, dout2 = _final_norm_loss(x3, gfinal, target, "final_norm_loss", tm)

    dgt2, dut2 = _ffn_bwd_hidden(dout2, gp2, up2, wd2, "ffn2_bwd_hidden", tm, tf)
    (dx2, dg_ffn2), _ = _ffn_bwd_input(dx3, x2, norm_ffn2, dgt2, dut2, wg2, wu2, "ffn2_bwd_input", tm_mix)
    (gwg2, gwu2), _ = _token_products([dgt2, dut2], h2, "ffn2_bwd_gate_up", tt, tfp)
    (gwd2,), _ = _token_products([hid2], dout2, "ffn2_bwd_down", tt, tfp)

    n_att = (A_Q_W // LANES) * (t // tq)
    n_prod = (f_all // tfp) * (t // tt)
    da, db, gwout = _out_proj_bwd(dx2, a_out, b_out[0], wout, "out_proj_bwd", tm_mix)
    (daq, dakx, davx, dsink_parts), (land_wg2,) = _attn_bwd(
        aq, akx, avx, a_out, da, a_lse, sink, "attn_a_bwd", A_HALF_WINDOW, t, tq, True, _exchange_plan([[gwg2]]), [0, n_att - 1])
    pattern_grads, pattern_lands = [], []
    for n, ((window, dil), (qs, ks, vs)) in enumerate(zip(B_PATTERNS, b_qkv)):
        carried = [gwu2, gwd2][n:n + 1]
        grads, lands = _attn_bwd(qs, ks, vs, rows_of(b_out[n]), rows_of(db[n]), rows_of(b_lse[n]), no_sink, f"attn_b{dil}_bwd",
                                 window // (2 * dil), t // dil, tq, False,
                                 _exchange_plan([carried]) if carried else None, [0, (B_W // LANES) * (t // tq) - 1])
        pattern_grads.append(grads[:3])
        pattern_lands.extend(lands)
    land_wu2, land_wd2 = pattern_lands
    dbq, dbk, dbv = _sum_pattern_grads(pattern_grads, "sum_pattern_grads", tm_mix)
    dx1, dg_mix, gwin, dout1 = _in_proj_bwd(dx2, x1, norm_mix, win, h_mix, cos, sin_signed, daq, dakx, davx, dbq, dbk, dbv, "in_proj_bwd", tm_mix)

    (gwd1,), (land_in, land_out) = _token_products(
        [hid1], dout1, "ffn1_bwd_down", tt, tfp, _exchange_plan([[gwin], [gwout]]), [0, n_prod - 1])
    dgt1, dut1 = _ffn_bwd_hidden(dout1, gp1, up1, wd1, "ffn1_bwd_hidden", tm, tf)
    (gwg1, gwu1), (land_wd1,) = _token_products(
        [dgt1, dut1], h1, "ffn1_bwd_gate_up", tt, tfp, _exchange_plan([[gwd1]]), [0, n_prod - 1])
    (grad_x, dg_ffn1), (land_wgu1,) = _ffn_bwd_input(
        dx1, x, norm_ffn1, dgt1, dut1, wg1, wu1, "ffn1_bwd_input", tm_mix, _exchange_plan([[gwg1, gwu1]]), [0, t // tm_mix - 1])

    dsink_pairs = jnp.sum(dsink_parts[:, :, 0, :], axis=1)
    dsink = jnp.stack([dsink_pairs[:, 0], dsink_pairs[:, HEAD_DIM]], axis=1).reshape(1, -1)
    small = jnp.concatenate([dg_ffn1, dg_mix, dg_ffn2, dg_final, jnp.pad(dsink, ((0, 0), (0, d - dsink.shape[1]))),
                             sq, jnp.zeros((2, d), F32)], axis=0)
    (land_small,) = _run_alone(_exchange_plan([[jnp.tile(small, (N_DEV, 1))]]), "gather_small_gradients")
    red_small = _sum_small(land_small, "sum_small_grads")
    loss = 0.5 * jnp.sum(red_small[5]) / d

    rf = rows1[0]
    sharded = {"w_gate1": (land_wgu1, 0, rf, True), "w_up1": (land_wgu1, 1, rf, True), "w_down1": (land_wd1, 0, rf, False),
               "w_in": (land_in, 0, rows2[0], True), "w_out": (land_out, 0, rows2[1], False),
               "w_gate2": (land_wg2, 0, rf, True), "w_up2": (land_wu2, 0, rf, True), "w_down2": (land_wd2, 0, rf, False)}
    n_sink = a_sink.shape[1]
    small_grads = {"norm_ffn1": red_small[0:1], "norm_mix": red_small[1:2], "norm_ffn2": red_small[2:3], "norm_final": red_small[3],
                   "a_sink": red_small[4:5, :n_sink]}
    params = {
        "norm_ffn1": (norm_ffn1, m_norm_ffn1, v_norm_ffn1), "w_gate1": (w_gate1, m_w_gate1, v_w_gate1),
        "w_up1": (w_up1, m_w_up1, v_w_up1), "w_down1": (w_down1, m_w_down1, v_w_down1),
        "norm_mix": (norm_mix, m_norm_mix, v_norm_mix), "w_in": (w_in, m_w_in, v_w_in),
        "a_sink": (a_sink, m_a_sink, v_a_sink), "w_out": (w_out, m_w_out, v_w_out),
        "norm_ffn2": (norm_ffn2, m_norm_ffn2, v_norm_ffn2), "w_gate2": (w_gate2, m_w_gate2, v_w_gate2),
        "w_up2": (w_up2, m_w_up2, v_w_up2), "w_down2": (w_down2, m_w_down2, v_w_down2),
        "norm_final": (norm_final, m_norm_final, v_norm_final),
    }
    grad_list, deltas, new_ms, new_vs = [], [], [], []
    for name, (w, m, v) in params.items():
        if name in sharded:
            land, j, rows, is_transposed = sharded[name]
            view = (lambda a: a[0].T) if is_transposed else (lambda a: a[0])
            back = (lambda a: a.T[None]) if is_transposed else (lambda a: a[None])
            outs = [back(o) for o in _adamw_rows(land, j, rows, view(w), view(m), view(v), f"adamw_{name}")]
        else:
            as_block = (lambda a: a.reshape(1, -1)) if w.ndim == 1 else (lambda a: a)
            g = small_grads[name]
            outs = [g] + [o.reshape(w.shape) for o in _adamw(as_block(w), as_block(g), as_block(m), as_block(v), f"adamw_{name}")]
        for lst, o in zip((grad_list, deltas, new_ms, new_vs), outs):
            lst.append(o)
    return (loss, grad_x[None], *grad_list, *deltas, *new_ms, *new_vs)
```

```python
import functools
import itertools

import numpy as np
import jax
import jax.numpy as jnp
from jax import lax
from jax.experimental import pallas as pl
from jax.experimental.pallas import tpu as pltpu

F32 = jnp.float32
BF16 = jnp.bfloat16

N_DEV = 8
HEAD_DIM = 64
LANES = 128
A_Q_W, A_KV_W, B_W = 512, 128, 512
A_HALF_WINDOW = 128
B_PATTERNS = ((128, 1), (512, 4), (2048, 16))
ROPE_THETA = 10000.0
NORM_EPS = 1e-6
FFN_RES_WEIGHT = 0.5
QK_SCALE = HEAD_DIM ** -0.5
NEG = -1e30

ADAM_LR = 0.001
ADAM_B1 = 0.9
ADAM_B2 = 0.999
ADAM_EPS = 1e-08
ADAM_WD = 0.01
ADAM_STEP = 10

MESH_T = pl.DeviceIdType.MESH
VMEM_LIMIT = 56 * 1024 * 1024


def _cp(sem=None, vmem=VMEM_LIMIT):
    return pltpu.CompilerParams(dimension_semantics=sem, vmem_limit_bytes=vmem)


def _dot_nn(a, b):
    return jnp.dot(a, b, preferred_element_type=F32)


def _dot_nt(a, b):
    return lax.dot_general(a, b, (((1,), (1,)), ((), ())), preferred_element_type=F32)


def _dot_tn(a, b):
    return lax.dot_general(a, b, (((0,), (0,)), ((), ())), preferred_element_type=F32)


def _rstd(xv):
    return lax.rsqrt(jnp.mean(xv * xv, axis=-1, keepdims=True) + NORM_EPS)


def _norm_bwd(dh, xv, r, gn):
    gy = dh * gn
    c = jnp.sum(gy * xv, axis=-1, keepdims=True) * (1.0 / xv.shape[-1])
    dx = r * gy - xv * (r * r * r * c)
    dgn = jnp.sum(dh * (xv * r), axis=0, keepdims=True)
    return dx, dgn


def _accumulate(ref, val, first):
    @pl.when(first)
    def _():
        ref[...] = val

    @pl.when(jnp.logical_not(first))
    def _():
        ref[...] += val


def _mesh_pos():
    return lax.axis_index("x"), lax.axis_index("y"), lax.axis_index("c")


def _dev_index(d):
    return 4 * d[0] + 2 * d[1] + d[2]


class _Comm:
    def __init__(self, inputs, out_shape, scratch, phases):
        self.inputs, self.out_shape, self.scratch, self.phases = inputs, out_shape, scratch, phases

    def specs(self):
        any_spec = pl.BlockSpec(memory_space=pl.ANY)
        return [any_spec] * len(self.inputs), [any_spec] * len(self.out_shape)


def _run_alone(comm, name):
    n_in, n_out = len(comm.inputs), len(comm.out_shape)

    def body(*refs):
        for phase in comm.phases:
            phase(refs[:n_in], refs[n_in:n_in + n_out], refs[n_in + n_out:])

    in_specs, out_specs = comm.specs()
    return pl.pallas_call(body, name=name, out_shape=comm.out_shape, in_specs=in_specs, out_specs=out_specs,
                          scratch_shapes=comm.scratch)(*comm.inputs)


def _run_hosted(comm, at, step, ins, outs, scr):
    for phase, when in zip(comm.phases, at):
        @pl.when(step == when)
        def _(phase=phase):
            phase(ins, outs, scr)


def _split(refs, *counts):
    parts, o = [], 0
    for n in counts:
        parts.append(refs[o:o + n])
        o += n
    return parts + [refs[o:]]


def _gather_plan(packed, rows_list):
    n_w = len(rows_list)
    offs = [int(o) for o in np.cumsum([0] + list(rows_list[:-1]))]
    d = packed.shape[1]

    def tools(ins, outs, scr):
        p_ref = ins[0]
        send_sems, recv_sems, local_sem = scr
        x, y, c = _mesh_pos()
        me, sibling = (x, y, c), (x, y, 1 - c)
        chips = [(1 - x, y), (x, 1 - y), (1 - x, 1 - y)]

        def rows(w, dev):
            start = pl.multiple_of(_dev_index(dev) * rows_list[w], 16)
            return outs[w].at[pl.ds(start, rows_list[w]), :]

        def mine(w):
            return p_ref.at[pl.ds(offs[w], rows_list[w]), :]

        def copy(k, w, block, to, own):
            return pltpu.make_async_remote_copy(
                src_ref=mine(w) if own else rows(w, block), dst_ref=rows(w, block),
                send_sem=send_sems.at[k], recv_sem=recv_sems.at[k], device_id=to, device_id_type=MESH_T)

        def all_blocks(k):
            return pltpu.make_async_remote_copy(
                src_ref=p_ref, dst_ref=p_ref, send_sem=send_sems.at[k], recv_sem=recv_sems.at[k],
                device_id=me, device_id_type=MESH_T)

        return p_ref, local_sem, me, sibling, chips, c, rows, mine, copy, all_blocks

    def start(ins, outs, scr):
        _, local_sem, me, sibling, chips, c, rows, mine, copy, _ = tools(ins, outs, scr)
        for w in range(n_w):
            pltpu.make_async_copy(mine(w), rows(w, me), local_sem).start()
        for w in range(n_w):
            copy(0, w, me, sibling, True).start()
        for j, chip in enumerate(chips):
            for w in range(n_w):
                copy(1 + j, w, me, (*chip, c), True).start()

    def relay(ins, outs, scr):
        _, _, _, sibling, chips, c, _, _, copy, all_blocks = tools(ins, outs, scr)
        for j, chip in enumerate(chips):
            all_blocks(1 + j).wait_recv()
            for w in range(n_w):
                copy(4 + j, w, (*chip, c), sibling, False).start()

    def finish(ins, outs, scr):
        p_ref, local_sem, _, _, _, _, _, _, _, all_blocks = tools(ins, outs, scr)
        all_blocks(0).wait_recv()
        for j in range(3):
            all_blocks(4 + j).wait_recv()
        for k in range(7):
            all_blocks(k).wait_send()
        pltpu.make_async_copy(p_ref, p_ref, local_sem).wait()

    return _Comm(
        [packed], [jax.ShapeDtypeStruct((N_DEV * r, d), packed.dtype) for r in rows_list],
        [pltpu.SemaphoreType.DMA((7,)), pltpu.SemaphoreType.DMA((7,)), pltpu.SemaphoreType.DMA], [start, relay, finish])


def _exchange_plan(groups):
    flat = [a for g in groups for a in g]
    n_g = len(groups)
    sizes = [len(g) for g in groups]
    rows = [g[0].shape[0] // N_DEV for g in groups]
    first = [int(o) for o in np.cumsum([0] + sizes[:-1])]

    def start(srcs, lands, scr):
        send_sems, recv_sems, local_sems = scr
        x, y, c = _mesh_pos()
        me = (x, y, c)
        me_idx = _dev_index(me)

        def block(g, i, dev):
            start_row = pl.multiple_of(_dev_index(dev) * rows[g], 8)
            return srcs[first[g] + i].at[pl.ds(start_row, rows[g]), :]

        def slot(g, i):
            return lands[g].at[me_idx, pl.ds(i * rows[g], rows[g]), :]

        for g in range(n_g):
            for i in range(sizes[g]):
                pltpu.make_async_copy(block(g, i, me), slot(g, i), local_sems.at[g]).start()
        flips = [f for f in itertools.product((0, 1), repeat=3) if any(f)]
        for k, (fx, fy, fc) in enumerate(flips):
            peer = (1 - x if fx else x, 1 - y if fy else y, 1 - c if fc else c)
            for g in range(n_g):
                for i in range(sizes[g]):
                    pltpu.make_async_remote_copy(
                        src_ref=block(g, i, peer), dst_ref=slot(g, i), send_sem=send_sems.at[g, k],
                        recv_sem=recv_sems.at[g, k], device_id=peer, device_id_type=MESH_T).start()

    def finish(srcs, lands, scr):
        send_sems, recv_sems, local_sems = scr
        me = _mesh_pos()
        for k in range(7):
            for g in range(n_g):
                pltpu.make_async_remote_copy(
                    src_ref=lands[g].at[0], dst_ref=lands[g].at[0], send_sem=send_sems.at[g, k],
                    recv_sem=recv_sems.at[g, k], device_id=me, device_id_type=MESH_T).wait()
        for g in range(n_g):
            pltpu.make_async_copy(lands[g].at[0], lands[g].at[0], local_sems.at[g]).wait()

    return _Comm(
        flat, [jax.ShapeDtypeStruct((N_DEV, sizes[g] * rows[g], groups[g][0].shape[1]), groups[g][0].dtype) for g in range(n_g)],
        [pltpu.SemaphoreType.DMA((n_g, 7)), pltpu.SemaphoreType.DMA((n_g, 7)), pltpu.SemaphoreType.DMA((n_g,))],
        [start, finish])


def _comm_parts(comm):
    if comm is None:
        return 0, 0, [], [], [], [], []
    in_specs, out_specs = comm.specs()
    return len(comm.inputs), len(comm.out_shape), in_specs, out_specs, comm.out_shape, comm.scratch, comm.inputs


def _ffn_fwd(x, gn, wg_t, wu_t, wd, name, tm, tf, comm=None, comm_at=None):
    t, d = x.shape
    f_all = wg_t.shape[0]
    n_f = f_all // tf
    n_ci, n_co, c_in_specs, c_out_specs, c_shapes, c_scratch, c_inputs = _comm_parts(comm)

    def body(*refs):
        ((x_ref, gn_ref, wg_ref, wu_ref, wd_ref), c_in, (y_ref, h_ref, gp_ref, up_ref, hid_ref), c_out,
         (h_s, hid_s), c_scr) = _split(refs, 5, n_ci, 5, n_co, 2)
        f = pl.program_id(1)
        if comm:
            _run_hosted(comm, comm_at, pl.program_id(0) * n_f + f, c_in, c_out, c_scr)

        @pl.when(f == 0)
        def _():
            xv = x_ref[...]
            h = ((xv * _rstd(xv)) * gn_ref[...]).astype(BF16)
            h_s[...] = h
            h_ref[...] = h

        h = h_s[...]
        g = _dot_nt(h, wg_ref[...])
        u = _dot_nt(h, wu_ref[...])
        sg = jax.nn.sigmoid(g)
        silu = g * sg
        gp_ref[...] = (u * (sg * (1.0 + g * (1.0 - sg)))).astype(BF16)
        up_ref[...] = silu.astype(BF16)
        hid = (silu * u).astype(BF16)
        hid_ref[...] = hid
        for f0 in range(n_f):
            @pl.when(f == f0)
            def _(f0=f0):
                hid_s[:, f0 * tf:(f0 + 1) * tf] = hid

        @pl.when(f == n_f - 1)
        def _():
            y_ref[...] = x_ref[...] + FFN_RES_WEIGHT * _dot_nn(hid_s[...], wd_ref[...])

    tok = pl.BlockSpec((tm, d), lambda i, f: (i, 0))
    wblk = pl.BlockSpec((tf, d), lambda i, f: (f, 0))
    whole = pl.BlockSpec((f_all, d), lambda i, f: (0, 0))
    act = pl.BlockSpec((tm, tf), lambda i, f: (i, f))
    act_shape = jax.ShapeDtypeStruct((t, f_all), BF16)
    res = pl.pallas_call(
        body, name=name, grid=(t // tm, n_f),
        out_shape=[jax.ShapeDtypeStruct((t, d), F32), jax.ShapeDtypeStruct((t, d), BF16), act_shape, act_shape, act_shape] + c_shapes,
        in_specs=[tok, pl.BlockSpec((1, d), lambda i, f: (0, 0)), wblk, wblk, whole] + c_in_specs,
        out_specs=[tok, tok, act, act, act] + c_out_specs,
        scratch_shapes=[pltpu.VMEM((tm, d), BF16), pltpu.VMEM((tm, f_all), BF16)] + c_scratch,
        compiler_params=_cp(("arbitrary", "arbitrary")),
    )(x, gn, wg_t, wu_t, wd, *c_inputs)
    return res[:5], res[5:]


def _ffn_bwd_hidden(dout, gp, up, wd, name, tm, tf):
    t, d = dout.shape
    f_all = wd.shape[0]

    def body(dout_ref, gp_ref, up_ref, wd_ref, dg_ref, du_ref):
        dhid = _dot_nt(dout_ref[...], wd_ref[...])
        dg_ref[...] = (dhid * gp_ref[...].astype(F32)).astype(BF16)
        du_ref[...] = (dhid * up_ref[...].astype(F32)).astype(BF16)

    tok = pl.BlockSpec((tm, d), lambda i, f: (i, 0))
    wblk = pl.BlockSpec((tf, d), lambda i, f: (f, 0))
    act = pl.BlockSpec((tm, tf), lambda i, f: (i, f))
    act_shape = jax.ShapeDtypeStruct((t, f_all), BF16)
    return pl.pallas_call(
        body, name=name, grid=(t // tm, f_all // tf),
        out_shape=[act_shape, act_shape], in_specs=[tok, act, act, wblk], out_specs=[act, act],
        compiler_params=_cp(("arbitrary", "arbitrary")),
    )(dout, gp, up, wd)


def _ffn_bwd_input(dy, x, gn, dg, du, wg_t, wu_t, name, tm, comm=None, comm_at=None):
    t, d = x.shape
    f_all = wg_t.shape[0]
    n_ci, n_co, c_in_specs, c_out_specs, c_shapes, c_scratch, c_inputs = _comm_parts(comm)

    def body(*refs):
        (dy_ref, x_ref, gn_ref, dg_ref, du_ref, wg_ref, wu_ref), c_in, (dx_ref, dgn_ref), c_out, c_scr = _split(
            refs, 7, n_ci, 2, n_co)
        i = pl.program_id(0)
        if comm:
            _run_hosted(comm, comm_at, i, c_in, c_out, c_scr)
        dh = _dot_nn(dg_ref[...], wg_ref[...]) + _dot_nn(du_ref[...], wu_ref[...])
        xv = x_ref[...]
        dxn, dgn = _norm_bwd(dh, xv, _rstd(xv), gn_ref[...])
        dx_ref[...] = dy_ref[...] + dxn
        _accumulate(dgn_ref, dgn, i == 0)

    tok = pl.BlockSpec((tm, d), lambda i: (i, 0))
    row = pl.BlockSpec((1, d), lambda i: (0, 0))
    act = pl.BlockSpec((tm, f_all), lambda i: (i, 0))
    whole = pl.BlockSpec((f_all, d), lambda i: (0, 0))
    res = pl.pallas_call(
        body, name=name, grid=(t // tm,),
        out_shape=[jax.ShapeDtypeStruct((t, d), F32), jax.ShapeDtypeStruct((1, d), F32)] + c_shapes,
        in_specs=[tok, tok, row, act, act, whole, whole] + c_in_specs,
        out_specs=[tok, row] + c_out_specs,
        scratch_shapes=c_scratch,
        compiler_params=_cp(("arbitrary",)),
    )(dy, x, gn, dg, du, wg_t, wu_t, *c_inputs)
    return res[:2], res[2:]


def _token_products(lhs_list, rhs, name, tt, tf, comm=None, comm_at=None):
    n_l = len(lhs_list)
    t, f_all = lhs_list[0].shape
    d = rhs.shape[1]
    n_t = t // tt
    n_ci, n_co, c_in_specs, c_out_specs, c_shapes, c_scratch, c_inputs = _comm_parts(comm)

    def body(*refs):
        lhs_refs, (rhs_ref,), c_in, out_refs, c_out, accs, c_scr = _split(refs, n_l, 1, n_ci, n_l, n_co, n_l)
        s = pl.program_id(1)
        if comm:
            _run_hosted(comm, comm_at, pl.program_id(0) * n_t + s, c_in, c_out, c_scr)
        rv = rhs_ref[...]
        for l_ref, acc in zip(lhs_refs, accs):
            _accumulate(acc, _dot_tn(l_ref[...], rv), s == 0)

        @pl.when(s == n_t - 1)
        def _():
            for o_ref, acc in zip(out_refs, accs):
                o_ref[...] = acc[...].astype(BF16)

    act = pl.BlockSpec((tt, tf), lambda f, s: (s, f))
    tok = pl.BlockSpec((tt, d), lambda f, s: (s, 0))
    wblk = pl.BlockSpec((tf, d), lambda f, s: (f, 0))
    res = pl.pallas_call(
        body, name=name, grid=(f_all // tf, n_t),
        out_shape=[jax.ShapeDtypeStruct((f_all, d), BF16)] * n_l + c_shapes,
        in_specs=[act] * n_l + [tok] + c_in_specs, out_specs=[wblk] * n_l + c_out_specs,
        scratch_shapes=[pltpu.VMEM((tf, d), F32)] * n_l + c_scratch,
        compiler_params=_cp(("arbitrary", "arbitrary")),
    )(*lhs_list, rhs, *c_inputs)
    return res[:n_l], res[n_l:]


def _swap_halves(t):
    w = t.shape[-1]
    lane = lax.broadcasted_iota(jnp.int32, (1, w), 1)
    return jnp.where((lane % HEAD_DIM) < HEAD_DIM // 2, pltpu.roll(t, w - HEAD_DIM // 2, 1), pltpu.roll(t, HEAD_DIM // 2, 1))


def _rope(t, cos, sin_signed):
    reps = t.shape[-1] // LANES
    return t * jnp.tile(cos, (1, reps)) + _swap_halves(t) * jnp.tile(sin_signed, (1, reps))


def _rope_bwd(dt, cos, sin_signed):
    reps = dt.shape[-1] // LANES
    return dt * jnp.tile(cos, (1, reps)) + _swap_halves(dt * jnp.tile(sin_signed, (1, reps)))


DILATIONS = tuple(dil for _, dil in B_PATTERNS if dil > 1)


def _seg_shape(t, dil, w, dtype):
    return jax.ShapeDtypeStruct((dil, t // dil, w), dtype)


def _seg_spec(tm, dil, w):
    return pl.BlockSpec((dil, tm // dil, w), lambda i: (0, i, 0))


def _tile_scratch(tm, w):
    return [pltpu.VMEM((tm, LANES), F32)] * (w // LANES)


def _put_tile(tile, val):
    for c, ref in enumerate(tile):
        ref[...] = val[:, c * LANES:(c + 1) * LANES]


def _get_tile(tile):
    return jnp.concatenate([ref[...] for ref in tile], axis=1)


def _scatter_to_segments(tile, seg_refs):
    for seg_ref, dil in zip(seg_refs, DILATIONS):
        rows = tile[0].shape[0] // dil
        for r in range(dil):
            for c, ref in enumerate(tile):
                seg_ref[r, :, c * LANES:(c + 1) * LANES] = ref[pl.ds(r, rows, stride=dil), :].astype(seg_ref.dtype)


def _gather_from_segments(seg_ref, dil, tile, add=False):
    rows = tile[0].shape[0] // dil
    for r in range(dil):
        for c, ref in enumerate(tile):
            idx = (pl.ds(r, rows, stride=dil), slice(None))
            v = seg_ref[r, :, c * LANES:(c + 1) * LANES].astype(F32)
            ref[idx] = ref[idx] + v if add else v


def _in_proj_fwd(x, gn, win_t, cos, sin_signed, name, tm):
    t, d = x.shape
    in_w = win_t.shape[0]
    n_dil = len(DILATIONS)

    def body(x_ref, gn_ref, w_ref, cos_ref, sin_ref, h_ref, aq_ref, akx_ref, avx_ref, bq_ref, bk_ref, bv_ref, *rest):
        seg_refs, tile = rest[:3 * n_dil], rest[3 * n_dil:]
        xv = x_ref[...]
        h = ((xv * _rstd(xv)) * gn_ref[...]).astype(BF16)
        h_ref[...] = h
        p = _dot_nt(h, w_ref[...])
        cs, sn = cos_ref[...], sin_ref[...]
        o = 0
        aq_ref[...] = (_rope(p[:, o:o + A_Q_W], cs, sn) * QK_SCALE).astype(BF16)
        o += A_Q_W
        ak = _rope(p[:, o:o + A_KV_W], cs, sn)
        o += A_KV_W
        av = p[:, o:o + A_KV_W]
        o += A_KV_W
        low = lax.broadcasted_iota(jnp.int32, (1, LANES), 1) < HEAD_DIM
        for src, dst in ((ak, akx_ref), (av, avx_ref)):
            other = pltpu.roll(src, HEAD_DIM, 1)
            dst[0] = jnp.where(low, src, other).astype(BF16)
            dst[1] = jnp.where(low, other, src).astype(BF16)
        for k, nat_ref in enumerate((bq_ref, bk_ref, bv_ref)):
            val = p[:, o:o + B_W]
            o += B_W
            if k < 2:
                val = _rope(val, cs, sn)
            if k == 0:
                val = val * QK_SCALE
            nat_ref[...] = val.astype(BF16)
            _put_tile(tile, val)
            _scatter_to_segments(tile, seg_refs[k * n_dil:(k + 1) * n_dil])

    tok = lambda w: pl.BlockSpec((tm, w), lambda i: (i, 0))
    kvx = pl.BlockSpec((2, tm, LANES), lambda i: (0, i, 0))
    sd = lambda *s: jax.ShapeDtypeStruct(s, BF16)
    res = pl.pallas_call(
        body, name=name, grid=(t // tm,),
        out_shape=[sd(t, d), sd(t, A_Q_W), sd(2, t, LANES), sd(2, t, LANES), sd(t, B_W), sd(t, B_W), sd(t, B_W)]
        + [_seg_shape(t, dil, B_W, BF16) for _ in range(3) for dil in DILATIONS],
        in_specs=[tok(d), pl.BlockSpec((1, d), lambda i: (0, 0)), pl.BlockSpec((in_w, d), lambda i: (0, 0)),
                  tok(LANES), tok(LANES)],
        out_specs=[tok(d), tok(A_Q_W), kvx, kvx, tok(B_W), tok(B_W), tok(B_W)]
        + [_seg_spec(tm, dil, B_W) for _ in range(3) for dil in DILATIONS],
        scratch_shapes=_tile_scratch(tm, B_W),
        compiler_params=_cp(("arbitrary",)),
    )(x, gn, win_t, cos, sin_signed)
    return res[:7], [res[7 + k * n_dil:7 + (k + 1) * n_dil] for k in range(3)]


def _in_proj_bwd(dres, x, gn, win_t, h, cos, sin_signed, daq, dakx, davx, dbq, dbk, dbv, name, tm):
    t, d = x.shape
    in_w = win_t.shape[0]
    n_t = t // tm
    chunk = in_w // 3

    def body(dres_ref, x_ref, gn_ref, w_ref, h_ref, cos_ref, sin_ref, daq_ref, dakx_ref, davx_ref, dbq_ref, dbk_ref,
             dbv_ref, dx_ref, dgn_ref, gw_ref, half_ref, dp_s, gw_s):
        i = pl.program_id(0)
        cs, sn = cos_ref[...], sin_ref[...]
        low = lax.broadcasted_iota(jnp.int32, (1, LANES), 1) < HEAD_DIM

        def fold(ref):
            a, b = ref[0], ref[1]
            return jnp.where(low, a + pltpu.roll(a, HEAD_DIM, 1), b + pltpu.roll(b, HEAD_DIM, 1))

        o = 0
        dp_s[:, o:o + A_Q_W] = _rope_bwd(daq_ref[...] * QK_SCALE, cs, sn).astype(BF16)
        o += A_Q_W
        dp_s[:, o:o + A_KV_W] = _rope_bwd(fold(dakx_ref), cs, sn).astype(BF16)
        o += A_KV_W
        dp_s[:, o:o + A_KV_W] = fold(davx_ref).astype(BF16)
        o += A_KV_W
        dp_s[:, o:o + B_W] = _rope_bwd(dbq_ref[...] * QK_SCALE, cs, sn).astype(BF16)
        o += B_W
        dp_s[:, o:o + B_W] = _rope_bwd(dbk_ref[...], cs, sn).astype(BF16)
        o += B_W
        dp_s[:, o:o + B_W] = dbv_ref[...].astype(BF16)
        dh = _dot_nn(dp_s[...], w_ref[...])
        hv = h_ref[...]
        for c0 in range(0, in_w, chunk):
            _accumulate(gw_s.at[pl.ds(c0, chunk), :], _dot_tn(dp_s[:, c0:c0 + chunk], hv), i == 0)
        xv = x_ref[...]
        dxn, dgn = _norm_bwd(dh, xv, _rstd(xv), gn_ref[...])
        dx = dres_ref[...] + dxn
        dx_ref[...] = dx
        half_ref[...] = (FFN_RES_WEIGHT * dx).astype(BF16)
        _accumulate(dgn_ref, dgn, i == 0)

        @pl.when(i == n_t - 1)
        def _():
            gw_ref[...] = gw_s[...].astype(BF16)

    tok = lambda w: pl.BlockSpec((tm, w), lambda i: (i, 0))
    row = pl.BlockSpec((1, d), lambda i: (0, 0))
    whole = pl.BlockSpec((in_w, d), lambda i: (0, 0))
    kvx = pl.BlockSpec((2, tm, LANES), lambda i: (0, i, 0))
    return pl.pallas_call(
        body, name=name, grid=(n_t,),
        out_shape=[jax.ShapeDtypeStruct((t, d), F32), jax.ShapeDtypeStruct((1, d), F32),
                   jax.ShapeDtypeStruct((in_w, d), BF16), jax.ShapeDtypeStruct((t, d), BF16)],
        in_specs=[tok(d), tok(d), row, whole, tok(d), tok(LANES), tok(LANES), tok(A_Q_W), kvx, kvx,
                  tok(B_W), tok(B_W), tok(B_W)],
        out_specs=[tok(d), row, whole, tok(d)],
        scratch_shapes=[pltpu.VMEM((tm, in_w), BF16), pltpu.VMEM((in_w, d), F32)],
        compiler_params=_cp(("arbitrary",)),
    )(dres, x, gn, win_t, h, cos, sin_signed, daq, dakx, davx, dbq, dbk, dbv)


def _merge_out_proj_fwd(x, a_out, outs, lses, wout, name, tm):
    t, d = x.shape
    n_dil = len(DILATIONS)

    def body(x_ref, a_ref, *rest):
        o_refs, l_refs, (w_ref, y_ref, b_ref, lt_ref), b_segs, lt_segs, scratch = _split(rest, 1 + n_dil, 1 + n_dil, 4, n_dil, n_dil)
        n_c = B_W // LANES
        tiles = [scratch[j * n_c:(j + 1) * n_c] for j in range(2 * n_dil)]
        os_, ls = [o_refs[0][...].astype(F32)], [l_refs[0][...]]
        for k, dil in enumerate(DILATIONS):
            _gather_from_segments(o_refs[1 + k], dil, tiles[2 * k])
            _gather_from_segments(l_refs[1 + k], dil, tiles[2 * k + 1])
            os_.append(_get_tile(tiles[2 * k]))
            ls.append(_get_tile(tiles[2 * k + 1]))
        mx = functools.reduce(jnp.maximum, ls)
        es = [jnp.exp(l - mx) for l in ls]
        den = functools.reduce(jnp.add, es)
        b = functools.reduce(jnp.add, [e * o for e, o in zip(es, os_)]) / den
        lt = mx + jnp.log(den)
        bb = b.astype(BF16)
        b_ref[...] = bb
        lt_ref[...] = lt
        y_ref[...] = x_ref[...] + _dot_nn(a_ref[...], w_ref[0:A_Q_W, :]) + _dot_nn(bb, w_ref[A_Q_W:A_Q_W + B_W, :])
        _put_tile(tiles[0], b)
        _scatter_to_segments(tiles[0], b_segs)
        _put_tile(tiles[1], lt)
        _scatter_to_segments(tiles[1], lt_segs)

    tok = lambda w: pl.BlockSpec((tm, w), lambda i: (i, 0))
    segs = [_seg_spec(tm, dil, B_W) for dil in DILATIONS]
    res = pl.pallas_call(
        body, name=name, grid=(t // tm,),
        out_shape=[jax.ShapeDtypeStruct((t, d), F32), jax.ShapeDtypeStruct((t, B_W), BF16), jax.ShapeDtypeStruct((t, B_W), F32)]
        + [_seg_shape(t, dil, B_W, BF16) for dil in DILATIONS] + [_seg_shape(t, dil, B_W, F32) for dil in DILATIONS],
        in_specs=[tok(d), tok(A_Q_W)] + ([tok(B_W)] + segs) * 2 + [pl.BlockSpec(wout.shape, lambda i: (0, 0))],
        out_specs=[tok(d), tok(B_W), tok(B_W)] + segs * 2,
        scratch_shapes=_tile_scratch(tm, B_W) * (2 * n_dil),
        compiler_params=_cp(("arbitrary",)),
    )(x, a_out, *outs, *lses, wout)
    return res[0], [res[1]] + list(res[3:3 + n_dil]), [res[2]] + list(res[3 + n_dil:])


def _out_proj_bwd(dy, a_out, b_out, wout, name, tm):
    t, d = dy.shape
    n_t = t // tm
    n_dil = len(DILATIONS)

    def body(dy_ref, a_ref, b_ref, w_ref, da_ref, db_ref, gw_ref, *rest):
        db_segs, gw_s, tile = rest[:n_dil], rest[n_dil], rest[n_dil + 1:]
        i = pl.program_id(0)
        dyb = dy_ref[...].astype(BF16)
        da_ref[...] = _dot_nt(dyb, w_ref[0:A_Q_W, :]).astype(BF16)
        db = _dot_nt(dyb, w_ref[A_Q_W:A_Q_W + B_W, :])
        db_ref[...] = db.astype(BF16)
        _put_tile(tile, db)
        _scatter_to_segments(tile, db_segs)
        ga = _dot_tn(a_ref[...], dyb)
        gb = _dot_tn(b_ref[...], dyb)

        @pl.when(i == 0)
        def _():
            gw_s[0:A_Q_W, :] = ga
            gw_s[A_Q_W:A_Q_W + B_W, :] = gb

        @pl.when(i > 0)
        def _():
            gw_s[0:A_Q_W, :] += ga
            gw_s[A_Q_W:A_Q_W + B_W, :] += gb

        @pl.when(i == n_t - 1)
        def _():
            gw_ref[...] = gw_s[...].astype(BF16)

    tok = lambda w: pl.BlockSpec((tm, w), lambda i: (i, 0))
    whole = pl.BlockSpec(wout.shape, lambda i: (0, 0))
    res = pl.pallas_call(
        body, name=name, grid=(n_t,),
        out_shape=[jax.ShapeDtypeStruct((t, A_Q_W), BF16), jax.ShapeDtypeStruct((t, B_W), BF16),
                   jax.ShapeDtypeStruct(wout.shape, BF16)] + [_seg_shape(t, dil, B_W, BF16) for dil in DILATIONS],
        in_specs=[tok(d), tok(A_Q_W), tok(B_W), whole],
        out_specs=[tok(A_Q_W), tok(B_W), whole] + [_seg_spec(tm, dil, B_W) for dil in DILATIONS],
        scratch_shapes=[pltpu.VMEM(wout.shape, F32)] + _tile_scratch(tm, B_W),
        compiler_params=_cp(("arbitrary",)),
    )(dy, a_out, b_out, wout)
    return res[0], [res[1]] + list(res[3:]), res[2]


def _sum_pattern_grads(per_pattern, name, tm):
    t = per_pattern[0][0].shape[0]
    n_dil = len(DILATIONS)

    def body(*refs):
        ins, outs, tile = _split(refs, 3 * (1 + n_dil), 3)
        for j, o_ref in enumerate(outs):
            _put_tile(tile, ins[j][...])
            for k, dil in enumerate(DILATIONS):
                _gather_from_segments(ins[3 * (1 + k) + j], dil, tile, add=True)
            o_ref[...] = _get_tile(tile)

    tok = pl.BlockSpec((tm, B_W), lambda i: (i, 0))
    flat = [a if k == 0 else a.reshape(DILATIONS[k - 1], -1, B_W) for k, grads in enumerate(per_pattern) for a in grads]
    return pl.pallas_call(
        body, name=name, grid=(t // tm,),
        out_shape=[jax.ShapeDtypeStruct((t, B_W), F32)] * 3,
        in_specs=[tok] * 3 + [_seg_spec(tm, dil, B_W) for dil in DILATIONS for _ in range(3)],
        out_specs=[tok] * 3, scratch_shapes=_tile_scratch(tm, B_W),
        compiler_params=_cp(("arbitrary",)),
    )(*flat)


SUB_ROWS = 64


def _sub_band(r0, hw, win, seg_lo, seg_len, t, rel, col):
    ks = pl.multiple_of(jnp.clip(r0 - hw, 0, t - win), SUB_ROWS)
    kpos = col + ks
    valid = (jnp.abs(rel + (ks - r0)) <= hw) & (kpos >= seg_lo) & (kpos < seg_lo + seg_len)
    return ks, valid


def _split_heads(v, low):
    zero = jnp.zeros_like(v)
    return jnp.concatenate([jnp.where(low, v, zero), jnp.where(low, zero, v)], axis=0)


def _kv_spec(kv, t):
    if kv.ndim == 3:
        return pl.BlockSpec((None, t, LANES), lambda p, i: (p // 2, 0, 0))
    return pl.BlockSpec((t, LANES), lambda p, i: (0, p))


def _attn_fwd(q, k, v, sink, name, hw, seg_len, tq, has_sink):
    t, width = q.shape
    sb = SUB_ROWS
    win = 2 * hw + LANES

    def body(sink_ref, q_ref, k_ref, v_ref, o_ref, lse_ref):
        p, i = pl.program_id(0), pl.program_id(1)
        q0 = i * tq
        seg_lo = (q0 // seg_len) * seg_len
        low = lax.broadcasted_iota(jnp.int32, (1, LANES), 1) < HEAD_DIM
        rel = lax.broadcasted_iota(jnp.int32, (sb, win), 1) - lax.broadcasted_iota(jnp.int32, (sb, win), 0)
        col = lax.broadcasted_iota(jnp.int32, (1, win), 1)
        subs = []
        for j in range(tq // sb):
            rows = pl.ds(j * sb, sb)
            ks, valid = _sub_band(q0 + j * sb, hw, win, seg_lo, seg_len, t, rel, col)
            subs.append((rows, ks, valid, _dot_nt(_split_heads(q_ref[rows, :], low), k_ref[pl.ds(ks, win), :])))
        for rows, ks, valid, s in subs:
            vw = v_ref[pl.ds(ks, win), :]
            es, inv, lses = [], [], []
            for a in range(2):
                sa = jnp.where(valid, s[a * sb:(a + 1) * sb], NEG)
                m = jnp.max(sa, axis=1, keepdims=True)
                if has_sink:
                    sk = sink_ref[2 * p + a]
                    m = jnp.maximum(m, sk)
                e = jnp.exp(sa - m)
                den = jnp.sum(e, axis=1, keepdims=True)
                if has_sink:
                    den = den + jnp.exp(sk - m)
                es.append(e.astype(BF16))
                inv.append(1.0 / den)
                lses.append(m + jnp.log(den))
            pv = _dot_nn(jnp.concatenate(es, axis=0), vw)
            o_ref[rows, :] = jnp.where(low, pv[0:sb] * inv[0], pv[sb:2 * sb] * inv[1]).astype(BF16)
            lse_ref[rows, :] = jnp.where(low, lses[0], lses[1])

    tile = pl.BlockSpec((tq, LANES), lambda p, i: (i, p))
    return pl.pallas_call(
        body, name=name, grid=(width // LANES, t // tq),
        out_shape=[jax.ShapeDtypeStruct((t, width), BF16), jax.ShapeDtypeStruct((t, width), F32)],
        in_specs=[pl.BlockSpec(memory_space=pltpu.SMEM), tile, _kv_spec(k, t), _kv_spec(v, t)],
        out_specs=[tile, tile],
        compiler_params=_cp(("arbitrary", "arbitrary")),
    )(sink, q, k, v)


def _attn_bwd(q, k, v, o, do, lse, sink, name, hw, seg_len, tq, has_sink, comm=None, comm_at=None):
    t, width = q.shape
    sb = SUB_ROWS
    win = 2 * hw + LANES
    n_q = t // tq
    shared_kv = k.ndim == 3
    n_ci, n_co, c_in_specs, c_out_specs, c_shapes, c_scratch, c_inputs = _comm_parts(comm)

    def body(*refs):
        ((sink_ref, q_ref, k_ref, v_ref, o_ref, do_ref, lse_ref), c_in, (dq_ref, dk_ref, dv_ref, ds_ref), c_out, c_scr) = _split(
            refs, 7, n_ci, 4, n_co)
        p, i = pl.program_id(0), pl.program_id(1)
        if comm:
            _run_hosted(comm, comm_at, p * n_q + i, c_in, c_out, c_scr)
        fresh = (i == 0) & (p % 2 == 0) if shared_kv else i == 0

        @pl.when(fresh)
        def _():
            dk_ref[...] = jnp.zeros_like(dk_ref)
            dv_ref[...] = jnp.zeros_like(dv_ref)

        q0 = i * tq
        seg_lo = (q0 // seg_len) * seg_len
        low = lax.broadcasted_iota(jnp.int32, (1, LANES), 1) < HEAD_DIM
        rel = lax.broadcasted_iota(jnp.int32, (sb, win), 1) - lax.broadcasted_iota(jnp.int32, (sb, win), 0)
        col = lax.broadcasted_iota(jnp.int32, (1, win), 1)
        dsink = [jnp.zeros((1, 1), F32), jnp.zeros((1, 1), F32)]
        subs = []
        for j in range(tq // sb):
            rows = pl.ds(j * sb, sb)
            ks, valid = _sub_band(q0 + j * sb, hw, win, seg_lo, seg_len, t, rel, col)
            kw = k_ref[pl.ds(ks, win), :]
            dov = do_ref[rows, :]
            q2 = _split_heads(q_ref[rows, :], low)
            do2 = _split_heads(dov, low)
            subs.append((rows, ks, valid, kw, dov, q2, do2, _dot_nt(q2, kw), _dot_nt(do2, v_ref[pl.ds(ks, win), :])))
        probs = []
        for rows, ks, valid, kw, dov, q2, do2, s, dpr in subs:
            prod = dov.astype(F32) * o_ref[rows, :].astype(F32)
            lse_t = lse_ref[rows, :]
            prs, dss = [], []
            for a in range(2):
                mine = low if a == 0 else jnp.logical_not(low)
                lse_a = jnp.max(jnp.where(mine, lse_t, -jnp.inf), axis=1, keepdims=True)
                delta = jnp.sum(jnp.where(mine, prod, 0.0), axis=1, keepdims=True)
                pr = jnp.exp(jnp.where(valid, s[a * sb:(a + 1) * sb], NEG) - lse_a)
                prs.append(pr.astype(BF16))
                dss.append((pr * (dpr[a * sb:(a + 1) * sb] - delta)).astype(BF16))
                if has_sink:
                    dsink[a] = dsink[a] - jnp.sum(jnp.exp(sink_ref[2 * p + a] - lse_a) * delta, axis=0, keepdims=True)
            probs.append((jnp.concatenate(prs, axis=0), jnp.concatenate(dss, axis=0)))
        for (rows, ks, valid, kw, dov, q2, do2, s, dpr), (pr2, ds2) in zip(subs, probs):
            dv_ref[pl.ds(ks, win), :] += _dot_tn(pr2, do2)
            dk_ref[pl.ds(ks, win), :] += _dot_tn(ds2, q2)
            dq2 = _dot_nn(ds2, kw)
            dq_ref[rows, :] = jnp.where(low, dq2[0:sb], dq2[sb:2 * sb])
        ds_ref[...] = jnp.broadcast_to(jnp.where(low, dsink[0], dsink[1]), ds_ref.shape)

    tile = pl.BlockSpec((tq, LANES), lambda p, i: (i, p))
    kv_shape = jax.ShapeDtypeStruct(k.shape, F32)
    res = pl.pallas_call(
        body, name=name, grid=(width // LANES, n_q),
        out_shape=[jax.ShapeDtypeStruct((t, width), F32), kv_shape, kv_shape,
                   jax.ShapeDtypeStruct((width // LANES, n_q, 8, LANES), F32)] + c_shapes,
        in_specs=[pl.BlockSpec(memory_space=pltpu.SMEM), tile, _kv_spec(k, t), _kv_spec(v, t), tile, tile, tile] + c_in_specs,
        out_specs=[tile, _kv_spec(k, t), _kv_spec(v, t), pl.BlockSpec((None, None, 8, LANES), lambda p, i: (p, i, 0, 0))] + c_out_specs,
        scratch_shapes=c_scratch,
        compiler_params=_cp(("arbitrary", "arbitrary")),
    )(sink, q, k, v, o, do, lse, *c_inputs)
    return res[:4], res[4:]


def _final_norm_loss(x, gn, target, name, tm):
    t, d = x.shape

    def body(x_ref, gn_ref, tg_ref, dx_ref, dgn_ref, sq_ref, half_ref):
        i = pl.program_id(0)
        xv, gnv = x_ref[...], gn_ref[...]
        r = _rstd(xv)
        diff = (xv * r) * gnv - tg_ref[...]
        dxn, dgn = _norm_bwd(diff * (1.0 / d), xv, r, gnv)
        dx_ref[...] = dxn
        half_ref[...] = (FFN_RES_WEIGHT * dxn).astype(BF16)
        _accumulate(dgn_ref, dgn, i == 0)
        _accumulate(sq_ref, jnp.sum(diff * diff, axis=0, keepdims=True), i == 0)

    tok = pl.BlockSpec((tm, d), lambda i: (i, 0))
    row = pl.BlockSpec((1, d), lambda i: (0, 0))
    return pl.pallas_call(
        body, name=name, grid=(t // tm,),
        out_shape=[jax.ShapeDtypeStruct((t, d), F32), jax.ShapeDtypeStruct((1, d), F32), jax.ShapeDtypeStruct((1, d), F32),
                   jax.ShapeDtypeStruct((t, d), BF16)],
        in_specs=[tok, row, tok], out_specs=[tok, row, row, tok],
        compiler_params=_cp(("arbitrary",)),
    )(x, gn, target)


def _adamw(w, g, m, v, name):
    def body(w_ref, g_ref, m_ref, v_ref, d_ref, nm_ref, nv_ref):
        gv = g_ref[...]
        nm = ADAM_B1 * m_ref[...] + (1.0 - ADAM_B1) * gv
        nv = ADAM_B2 * v_ref[...] + (1.0 - ADAM_B2) * (gv * gv)
        m_hat = nm / (1.0 - ADAM_B1 ** ADAM_STEP)
        v_hat = nv / (1.0 - ADAM_B2 ** ADAM_STEP)
        d_ref[...] = -ADAM_LR * (m_hat / (jnp.sqrt(v_hat) + ADAM_EPS) + ADAM_WD * w_ref[...])
        nm_ref[...] = nm
        nv_ref[...] = nv

    shape = jax.ShapeDtypeStruct(w.shape, F32)
    return pl.pallas_call(body, name=name, out_shape=[shape, shape, shape], compiler_params=_cp())(w, g, m, v)


def _adamw_rows(land, j, rows, w, m, v, name):
    d = w.shape[1]

    def body(l_ref, w_ref, m_ref, v_ref, g_ref, d_ref, nm_ref, nv_ref):
        gv = l_ref[0].astype(F32)
        for s in range(1, N_DEV):
            gv = gv + l_ref[s].astype(F32)
        g_ref[...] = gv
        nm = ADAM_B1 * m_ref[...] + (1.0 - ADAM_B1) * gv
        nv = ADAM_B2 * v_ref[...] + (1.0 - ADAM_B2) * (gv * gv)
        m_hat = nm / (1.0 - ADAM_B1 ** ADAM_STEP)
        v_hat = nv / (1.0 - ADAM_B2 ** ADAM_STEP)
        d_ref[...] = -ADAM_LR * (m_hat / (jnp.sqrt(v_hat) + ADAM_EPS) + ADAM_WD * w_ref[...])
        nm_ref[...] = nm
        nv_ref[...] = nv

    whole = pl.BlockSpec((rows, d), lambda i: (0, 0))
    shape = jax.ShapeDtypeStruct((rows, d), F32)
    return pl.pallas_call(
        body, name=name, grid=(1,), out_shape=[shape] * 4,
        in_specs=[pl.BlockSpec((N_DEV, rows, d), lambda i: (0, j, 0)), whole, whole, whole], out_specs=[whole] * 4,
        compiler_params=_cp(("arbitrary",)),
    )(land, w, m, v)


def _sum_small(land, name):
    def body(l_ref, o_ref):
        acc = l_ref[0]
        for s in range(1, N_DEV):
            acc = acc + l_ref[s]
        o_ref[...] = acc

    return pl.pallas_call(body, name=name, out_shape=jax.ShapeDtypeStruct(land.shape[1:], F32), compiler_params=_cp())(land)


def _rope_lanes(positions):
    inv_freq = 1.0 / (ROPE_THETA ** (jnp.arange(0, HEAD_DIM, 2, dtype=F32) / HEAD_DIM))
    ang = positions.astype(F32)[:, None] * inv_freq
    cos, sin = jnp.cos(ang), jnp.sin(ang)
    return jnp.concatenate([cos, cos, cos, cos], axis=1), jnp.concatenate([-sin, sin, -sin, sin], axis=1)


def kernel(x, positions, norm_ffn1, w_gate1, w_up1, w_down1, norm_mix, w_in, a_sink, w_out, norm_ffn2, w_gate2, w_up2, w_down2, norm_final, loss_target, m_norm_ffn1, m_w_gate1, m_w_up1, m_w_down1, m_norm_mix, m_w_in, m_a_sink, m_w_out, m_norm_ffn2, m_w_gate2, m_w_up2, m_w_down2, m_norm_final, v_norm_ffn1, v_w_gate1, v_w_up1, v_w_down1, v_norm_mix, v_w_in, v_a_sink, v_w_out, v_norm_ffn2, v_w_gate2, v_w_up2, v_w_down2, v_norm_final):
    x = x[0]
    target = loss_target[0]
    t, d = x.shape
    tm = min(1024, t)
    tm_mix = min(512, t)
    tf = 256
    tq = min(512, t // 16)

    tm_hidden = min(2048, t)
    tt = min(1024, t)
    f_all = w_gate1.shape[2] * N_DEV
    tfp = f_all // 2
    n_steps = (t // tm) * (f_all // tf)

    def stacked(shards):
        return jnp.concatenate([s.astype(BF16) for s in shards], axis=0), [s.shape[0] for s in shards]

    packed1, rows1 = stacked([w_gate1[0].T, w_up1[0].T, w_down1[0]])
    packed2, rows2 = stacked([w_in[0].T, w_out[0], w_gate2[0].T, w_up2[0].T, w_down2[0]])
    wg1, wu1, wd1 = _run_alone(_gather_plan(packed1, rows1), "gather_ffn1_weights")

    cos, sin_signed = _rope_lanes(positions[0])
    sink = a_sink[0]
    no_sink = jnp.zeros_like(sink)

    (x1, h1, gp1, up1, hid1), (win, wout, wg2, wu2, wd2) = _ffn_fwd(
        x, norm_ffn1, wg1, wu1, wd1, "ffn1_fwd", tm, tf, _gather_plan(packed2, rows2), [0, (3 * n_steps) // 4, n_steps - 1])
    (h_mix, aq, akx, avx, bq, bk, bv), seg_qkv = _in_proj_fwd(x1, norm_mix, win, cos, sin_signed, "in_proj_fwd", tm_mix)
    a_out, a_lse = _attn_fwd(aq, akx, avx, sink, "attn_a_fwd", A_HALF_WINDOW, t, tq, True)
    rows_of = lambda a: a.reshape(t, B_W)
    segments_of = lambda a, dil: a if dil == 1 else a.reshape(dil, t // dil, B_W)
    b_qkv, b_outs, b_lses = [], [], []
    for n, (window, dil) in enumerate(B_PATTERNS):
        qs, ks, vs = (bq, bk, bv) if dil == 1 else (rows_of(seg_qkv[k][n - 1]) for k in range(3))
        o_seg, lse_seg = _attn_fwd(qs, ks, vs, no_sink, f"attn_b{dil}_fwd", window // (2 * dil), t // dil, tq, False)
        b_qkv.append((qs, ks, vs))
        b_outs.append(segments_of(o_seg, dil))
        b_lses.append(segments_of(lse_seg, dil))
    x2, b_out, b_lse = _merge_out_proj_fwd(x1, a_out, b_outs, b_lses, wout, "out_proj_fwd", tm_mix)
    (x3, h2, gp2, up2, hid2), _ = _ffn_fwd(x2, norm_ffn2, wg2, wu2, wd2, "ffn2_fwd", tm, tf)

    gfinal = norm_final.reshape(1, d)
    dx3, dg_final, sq, dout2 = _final_norm_loss(x3, gfinal, target, "final_norm_loss", tm)

    dgt2, dut2 = _ffn_bwd_hidden(dout2, gp2, up2, wd2, "ffn2_bwd_hidden", tm_hidden, tf)
    (dx2, dg_ffn2), _ = _ffn_bwd_input(dx3, x2, norm_ffn2, dgt2, dut2, wg2, wu2, "ffn2_bwd_input", tm_mix)
    (gwg2, gwu2), _ = _token_products([dgt2, dut2], h2, "ffn2_bwd_gate_up", tt, tfp)
    (gwd2,), _ = _token_products([hid2], dout2, "ffn2_bwd_down", tt, tfp)

    n_att = (A_Q_W // LANES) * (t // tq)
    n_prod = (f_all // tfp) * (t // tt)
    da, db, gwout = _out_proj_bwd(dx2, a_out, b_out[0], wout, "out_proj_bwd", tm_mix)
    (daq, dakx, davx, dsink_parts), (land_wg2,) = _attn_bwd(
        aq, akx, avx, a_out, da, a_lse, sink, "attn_a_bwd", A_HALF_WINDOW, t, tq, True, _exchange_plan([[gwg2]]), [0, n_att - 1])
    pattern_grads, pattern_lands = [], []
    for n, ((window, dil), (qs, ks, vs)) in enumerate(zip(B_PATTERNS, b_qkv)):
        carried = [gwu2, gwd2][n:n + 1]
        grads, lands = _attn_bwd(qs, ks, vs, rows_of(b_out[n]), rows_of(db[n]), rows_of(b_lse[n]), no_sink, f"attn_b{dil}_bwd",
                                 window // (2 * dil), t // dil, tq, False,
                                 _exchange_plan([carried]) if carried else None, [0, (B_W // LANES) * (t // tq) - 1])
        pattern_grads.append(grads[:3])
        pattern_lands.extend(lands)
    land_wu2, land_wd2 = pattern_lands
    dbq, dbk, dbv = _sum_pattern_grads(pattern_grads, "sum_pattern_grads", tm_mix)
    dx1, dg_mix, gwin, dout1 = _in_proj_bwd(dx2, x1, norm_mix, win, h_mix, cos, sin_signed, daq, dakx, davx, dbq, dbk, dbv, "in_proj_bwd", tm_mix)

    (gwd1,), (land_in, land_out) = _token_products(
        [hid1], dout1, "ffn1_bwd_down", tt, tfp, _exchange_plan([[gwin], [gwout]]), [0, n_prod - 1])
    dgt1, dut1 = _ffn_bwd_hidden(dout1, gp1, up1, wd1, "ffn1_bwd_hidden", tm_hidden, tf)
    (gwg1, gwu1), (land_wd1,) = _token_products(
        [dgt1, dut1], h1, "ffn1_bwd_gate_up", tt, tfp, _exchange_plan([[gwd1]]), [0, n_prod - 1])
    (grad_x, dg_ffn1), (land_wgu1,) = _ffn_bwd_input(
        dx1, x, norm_ffn1, dgt1, dut1, wg1, wu1, "ffn1_bwd_input", tm_mix, _exchange_plan([[gwg1, gwu1]]), [0, t // tm_mix - 1])

    dsink_pairs = jnp.sum(dsink_parts[:, :, 0, :], axis=1)
    dsink = jnp.stack([dsink_pairs[:, 0], dsink_pairs[:, HEAD_DIM]], axis=1).reshape(1, -1)
    small = jnp.concatenate([dg_ffn1, dg_mix, dg_ffn2, dg_final, jnp.pad(dsink, ((0, 0), (0, d - dsink.shape[1]))),
                             sq, jnp.zeros((2, d), F32)], axis=0)
    (land_small,) = _run_alone(_exchange_plan([[jnp.tile(small, (N_DEV, 1))]]), "gather_small_gradients")
    red_small = _sum_small(land_small, "sum_small_grads")
    loss = 0.5 * jnp.sum(red_small[5]) / d

    rf = rows1[0]
    sharded = {"w_gate1": (land_wgu1, 0, rf, True), "w_up1": (land_wgu1, 1, rf, True), "w_down1": (land_wd1, 0, rf, False),
               "w_in": (land_in, 0, rows2[0], True), "w_out": (land_out, 0, rows2[1], False),
               "w_gate2": (land_wg2, 0, rf, True), "w_up2": (land_wu2, 0, rf, True), "w_down2": (land_wd2, 0, rf, False)}
    n_sink = a_sink.shape[1]
    small_grads = {"norm_ffn1": red_small[0:1], "norm_mix": red_small[1:2], "norm_ffn2": red_small[2:3], "norm_final": red_small[3],
                   "a_sink": red_small[4:5, :n_sink]}
    params = {
        "norm_ffn1": (norm_ffn1, m_norm_ffn1, v_norm_ffn1), "w_gate1": (w_gate1, m_w_gate1, v_w_gate1),
        "w_up1": (w_up1, m_w_up1, v_w_up1), "w_down1": (w_down1, m_w_down1, v_w_down1),
        "norm_mix": (norm_mix, m_norm_mix, v_norm_mix), "w_in": (w_in, m_w_in, v_w_in),
        "a_sink": (a_sink, m_a_sink, v_a_sink), "w_out": (w_out, m_w_out, v_w_out),
        "norm_ffn2": (norm_ffn2, m_norm_ffn2, v_norm_ffn2), "w_gate2": (w_gate2, m_w_gate2, v_w_gate2),
        "w_up2": (w_up2, m_w_up2, v_w_up2), "w_down2": (w_down2, m_w_down2, v_w_down2),
        "norm_final": (norm_final, m_norm_final, v_norm_final),
    }
    grad_list, deltas, new_ms, new_vs = [], [], [], []
    for name, (w, m, v) in params.items():
        if name in sharded:
            land, j, rows, is_transposed = sharded[name]
            view = (lambda a: a[0].T) if is_transposed else (lambda a: a[0])
            back = (lambda a: a.T[None]) if is_transposed else (lambda a: a[None])
            outs = [back(o) for o in _adamw_rows(land, j, rows, view(w), view(m), view(v), f"adamw_{name}")]
        else:
            as_block = (lambda a: a.reshape(1, -1)) if w.ndim == 1 else (lambda a: a)
            g = small_grads[name]
            outs = [g] + [o.reshape(w.shape) for o in _adamw(as_block(w), as_block(g), as_block(m), as_block(v), f"adamw_{name}")]
        for lst, o in zip((grad_list, deltas, new_ms, new_vs), outs):
            lst.append(o)
    return (loss, grad_x[None], *grad_list, *deltas, *new_ms, *new_vs)
```

```python
import functools
import itertools

import numpy as np
import jax
import jax.numpy as jnp
from jax import lax
from jax.experimental import pallas as pl
from jax.experimental.pallas import tpu as pltpu

F32 = jnp.float32
BF16 = jnp.bfloat16

N_DEV = 8
HEAD_DIM = 64
LANES = 128
A_Q_W, A_KV_W, B_W = 512, 128, 512
A_HALF_WINDOW = 128
B_PATTERNS = ((128, 1), (512, 4), (2048, 16))
ROPE_THETA = 10000.0
NORM_EPS = 1e-6
FFN_RES_WEIGHT = 0.5
QK_SCALE = HEAD_DIM ** -0.5
NEG = -1e30

ADAM_LR = 0.001
ADAM_B1 = 0.9
ADAM_B2 = 0.999
ADAM_EPS = 1e-08
ADAM_WD = 0.01
ADAM_STEP = 10

MESH_T = pl.DeviceIdType.MESH
VMEM_LIMIT = 56 * 1024 * 1024


def _cp(sem=None, vmem=VMEM_LIMIT):
    return pltpu.CompilerParams(dimension_semantics=sem, vmem_limit_bytes=vmem)


def _dot_nn(a, b):
    return jnp.dot(a, b, preferred_element_type=F32)


def _dot_nt(a, b):
    return lax.dot_general(a, b, (((1,), (1,)), ((), ())), preferred_element_type=F32)


def _dot_tn(a, b):
    return lax.dot_general(a, b, (((0,), (0,)), ((), ())), preferred_element_type=F32)


def _rstd(xv):
    return lax.rsqrt(jnp.mean(xv * xv, axis=-1, keepdims=True) + NORM_EPS)


def _norm_bwd(dh, xv, r, gn):
    gy = dh * gn
    c = jnp.sum(gy * xv, axis=-1, keepdims=True) * (1.0 / xv.shape[-1])
    dx = r * gy - xv * (r * r * r * c)
    dgn = jnp.sum(dh * (xv * r), axis=0, keepdims=True)
    return dx, dgn


def _accumulate(ref, val, first):
    @pl.when(first)
    def _():
        ref[...] = val

    @pl.when(jnp.logical_not(first))
    def _():
        ref[...] += val


def _mesh_pos():
    return lax.axis_index("x"), lax.axis_index("y"), lax.axis_index("c")


def _dev_index(d):
    return 4 * d[0] + 2 * d[1] + d[2]


class _Comm:
    def __init__(self, inputs, out_shape, scratch, phases):
        self.inputs, self.out_shape, self.scratch, self.phases = inputs, out_shape, scratch, phases

    def specs(self):
        any_spec = pl.BlockSpec(memory_space=pl.ANY)
        return [any_spec] * len(self.inputs), [any_spec] * len(self.out_shape)


def _run_alone(comm, name):
    n_in, n_out = len(comm.inputs), len(comm.out_shape)

    def body(*refs):
        for phase in comm.phases:
            phase(refs[:n_in], refs[n_in:n_in + n_out], refs[n_in + n_out:])

    in_specs, out_specs = comm.specs()
    return pl.pallas_call(body, name=name, out_shape=comm.out_shape, in_specs=in_specs, out_specs=out_specs,
                          scratch_shapes=comm.scratch)(*comm.inputs)


def _run_hosted(comm, at, step, ins, outs, scr):
    for phase, when in zip(comm.phases, at):
        @pl.when(step == when)
        def _(phase=phase):
            phase(ins, outs, scr)


def _split(refs, *counts):
    parts, o = [], 0
    for n in counts:
        parts.append(refs[o:o + n])
        o += n
    return parts + [refs[o:]]


def _gather_plan(packed, rows_list):
    n_w = len(rows_list)
    offs = [int(o) for o in np.cumsum([0] + list(rows_list[:-1]))]
    d = packed.shape[1]

    def tools(ins, outs, scr):
        p_ref = ins[0]
        send_sems, recv_sems, local_sem = scr
        x, y, c = _mesh_pos()
        me, sibling = (x, y, c), (x, y, 1 - c)
        chips = [(1 - x, y), (x, 1 - y), (1 - x, 1 - y)]

        def rows(w, dev):
            start = pl.multiple_of(_dev_index(dev) * rows_list[w], 16)
            return outs[w].at[pl.ds(start, rows_list[w]), :]

        def mine(w):
            return p_ref.at[pl.ds(offs[w], rows_list[w]), :]

        def copy(k, w, block, to, own):
            return pltpu.make_async_remote_copy(
                src_ref=mine(w) if own else rows(w, block), dst_ref=rows(w, block),
                send_sem=send_sems.at[k], recv_sem=recv_sems.at[k], device_id=to, device_id_type=MESH_T)

        def all_blocks(k):
            return pltpu.make_async_remote_copy(
                src_ref=p_ref, dst_ref=p_ref, send_sem=send_sems.at[k], recv_sem=recv_sems.at[k],
                device_id=me, device_id_type=MESH_T)

        return p_ref, local_sem, me, sibling, chips, c, rows, mine, copy, all_blocks

    def start(ins, outs, scr):
        _, local_sem, me, sibling, chips, c, rows, mine, copy, _ = tools(ins, outs, scr)
        for w in range(n_w):
            pltpu.make_async_copy(mine(w), rows(w, me), local_sem).start()
        for w in range(n_w):
            copy(0, w, me, sibling, True).start()
        for j, chip in enumerate(chips):
            for w in range(n_w):
                copy(1 + j, w, me, (*chip, c), True).start()

    def relay(ins, outs, scr):
        _, _, _, sibling, chips, c, _, _, copy, all_blocks = tools(ins, outs, scr)
        for j, chip in enumerate(chips):
            all_blocks(1 + j).wait_recv()
            for w in range(n_w):
                copy(4 + j, w, (*chip, c), sibling, False).start()

    def finish(ins, outs, scr):
        p_ref, local_sem, _, _, _, _, _, _, _, all_blocks = tools(ins, outs, scr)
        all_blocks(0).wait_recv()
        for j in range(3):
            all_blocks(4 + j).wait_recv()
        for k in range(7):
            all_blocks(k).wait_send()
        pltpu.make_async_copy(p_ref, p_ref, local_sem).wait()

    return _Comm(
        [packed], [jax.ShapeDtypeStruct((N_DEV * r, d), packed.dtype) for r in rows_list],
        [pltpu.SemaphoreType.DMA((7,)), pltpu.SemaphoreType.DMA((7,)), pltpu.SemaphoreType.DMA], [start, relay, finish])


def _exchange_plan(groups):
    flat = [a for g in groups for a in g]
    n_g = len(groups)
    sizes = [len(g) for g in groups]
    rows = [g[0].shape[0] // N_DEV for g in groups]
    first = [int(o) for o in np.cumsum([0] + sizes[:-1])]

    def start(srcs, lands, scr):
        send_sems, recv_sems, local_sems = scr
        x, y, c = _mesh_pos()
        me = (x, y, c)
        me_idx = _dev_index(me)

        def block(g, i, dev):
            start_row = pl.multiple_of(_dev_index(dev) * rows[g], 8)
            return srcs[first[g] + i].at[pl.ds(start_row, rows[g]), :]

        def slot(g, i):
            return lands[g].at[me_idx, pl.ds(i * rows[g], rows[g]), :]

        for g in range(n_g):
            for i in range(sizes[g]):
                pltpu.make_async_copy(block(g, i, me), slot(g, i), local_sems.at[g]).start()
        flips = [f for f in itertools.product((0, 1), repeat=3) if any(f)]
        for k, (fx, fy, fc) in enumerate(flips):
            peer = (1 - x if fx else x, 1 - y if fy else y, 1 - c if fc else c)
            for g in range(n_g):
                for i in range(sizes[g]):
                    pltpu.make_async_remote_copy(
                        src_ref=block(g, i, peer), dst_ref=slot(g, i), send_sem=send_sems.at[g, k],
                        recv_sem=recv_sems.at[g, k], device_id=peer, device_id_type=MESH_T).start()

    def finish(srcs, lands, scr):
        send_sems, recv_sems, local_sems = scr
        me = _mesh_pos()
        for k in range(7):
            for g in range(n_g):
                pltpu.make_async_remote_copy(
                    src_ref=lands[g].at[0], dst_ref=lands[g].at[0], send_sem=send_sems.at[g, k],
                    recv_sem=recv_sems.at[g, k], device_id=me, device_id_type=MESH_T).wait()
        for g in range(n_g):
            pltpu.make_async_copy(lands[g].at[0], lands[g].at[0], local_sems.at[g]).wait()

    return _Comm(
        flat, [jax.ShapeDtypeStruct((N_DEV, sizes[g] * rows[g], groups[g][0].shape[1]), groups[g][0].dtype) for g in range(n_g)],
        [pltpu.SemaphoreType.DMA((n_g, 7)), pltpu.SemaphoreType.DMA((n_g, 7)), pltpu.SemaphoreType.DMA((n_g,))],
        [start, finish])


def _comm_parts(comm):
    if comm is None:
        return 0, 0, [], [], [], [], []
    in_specs, out_specs = comm.specs()
    return len(comm.inputs), len(comm.out_shape), in_specs, out_specs, comm.out_shape, comm.scratch, comm.inputs


def _ffn_fwd(x, gn, wg_t, wu_t, wd, name, tm, tf, comm=None, comm_at=None):
    t, d = x.shape
    f_all = wg_t.shape[0]
    n_f = f_all // tf
    n_ci, n_co, c_in_specs, c_out_specs, c_shapes, c_scratch, c_inputs = _comm_parts(comm)

    def body(*refs):
        ((x_ref, gn_ref, wg_ref, wu_ref, wd_ref), c_in, (y_ref, h_ref, gp_ref, up_ref, hid_ref), c_out,
         (h_s, hid_s), c_scr) = _split(refs, 5, n_ci, 5, n_co, 2)
        f = pl.program_id(1)
        if comm:
            _run_hosted(comm, comm_at, pl.program_id(0) * n_f + f, c_in, c_out, c_scr)

        @pl.when(f == 0)
        def _():
            xv = x_ref[...]
            h = ((xv * _rstd(xv)) * gn_ref[...]).astype(BF16)
            h_s[...] = h
            h_ref[...] = h

        h = h_s[...]
        g = _dot_nt(h, wg_ref[...])
        u = _dot_nt(h, wu_ref[...])
        sg = jax.nn.sigmoid(g)
        silu = g * sg
        gp_ref[...] = (u * (sg * (1.0 + g * (1.0 - sg)))).astype(BF16)
        up_ref[...] = silu.astype(BF16)
        hid = (silu * u).astype(BF16)
        hid_ref[...] = hid
        for f0 in range(n_f):
            @pl.when(f == f0)
            def _(f0=f0):
                hid_s[:, f0 * tf:(f0 + 1) * tf] = hid

        @pl.when(f == n_f - 1)
        def _():
            y_ref[...] = x_ref[...] + FFN_RES_WEIGHT * _dot_nn(hid_s[...], wd_ref[...])

    tok = pl.BlockSpec((tm, d), lambda i, f: (i, 0))
    wblk = pl.BlockSpec((tf, d), lambda i, f: (f, 0))
    whole = pl.BlockSpec((f_all, d), lambda i, f: (0, 0))
    act = pl.BlockSpec((tm, tf), lambda i, f: (i, f))
    act_shape = jax.ShapeDtypeStruct((t, f_all), BF16)
    res = pl.pallas_call(
        body, name=name, grid=(t // tm, n_f),
        out_shape=[jax.ShapeDtypeStruct((t, d), F32), jax.ShapeDtypeStruct((t, d), BF16), act_shape, act_shape, act_shape] + c_shapes,
        in_specs=[tok, pl.BlockSpec((1, d), lambda i, f: (0, 0)), wblk, wblk, whole] + c_in_specs,
        out_specs=[tok, tok, act, act, act] + c_out_specs,
        scratch_shapes=[pltpu.VMEM((tm, d), BF16), pltpu.VMEM((tm, f_all), BF16)] + c_scratch,
        compiler_params=_cp(("arbitrary", "arbitrary")),
    )(x, gn, wg_t, wu_t, wd, *c_inputs)
    return res[:5], res[5:]


def _ffn_bwd_hidden(dout, gp, up, wd, name, tm, tf):
    t, d = dout.shape
    f_all = wd.shape[0]

    def body(dout_ref, gp_ref, up_ref, wd_ref, dg_ref, du_ref):
        dhid = _dot_nt(dout_ref[...], wd_ref[...])
        dg_ref[...] = (dhid * gp_ref[...].astype(F32)).astype(BF16)
        du_ref[...] = (dhid * up_ref[...].astype(F32)).astype(BF16)

    tok = pl.BlockSpec((tm, d), lambda i, f: (i, 0))
    wblk = pl.BlockSpec((tf, d), lambda i, f: (f, 0))
    act = pl.BlockSpec((tm, tf), lambda i, f: (i, f))
    act_shape = jax.ShapeDtypeStruct((t, f_all), BF16)
    return pl.pallas_call(
        body, name=name, grid=(t // tm, f_all // tf),
        out_shape=[act_shape, act_shape], in_specs=[tok, act, act, wblk], out_specs=[act, act],
        compiler_params=_cp(("arbitrary", "arbitrary")),
    )(dout, gp, up, wd)


def _ffn_bwd_input(dy, x, gn, dg, du, wg_t, wu_t, name, tm, comm=None, comm_at=None):
    t, d = x.shape
    f_all = wg_t.shape[0]
    n_ci, n_co, c_in_specs, c_out_specs, c_shapes, c_scratch, c_inputs = _comm_parts(comm)

    def body(*refs):
        (dy_ref, x_ref, gn_ref, dg_ref, du_ref, wg_ref, wu_ref), c_in, (dx_ref, dgn_ref), c_out, c_scr = _split(
            refs, 7, n_ci, 2, n_co)
        i = pl.program_id(0)
        if comm:
            _run_hosted(comm, comm_at, i, c_in, c_out, c_scr)
        dh = _dot_nn(dg_ref[...], wg_ref[...]) + _dot_nn(du_ref[...], wu_ref[...])
        xv = x_ref[...]
        dxn, dgn = _norm_bwd(dh, xv, _rstd(xv), gn_ref[...])
        dx_ref[...] = dy_ref[...] + dxn
        _accumulate(dgn_ref, dgn, i == 0)

    tok = pl.BlockSpec((tm, d), lambda i: (i, 0))
    row = pl.BlockSpec((1, d), lambda i: (0, 0))
    act = pl.BlockSpec((tm, f_all), lambda i: (i, 0))
    whole = pl.BlockSpec((f_all, d), lambda i: (0, 0))
    res = pl.pallas_call(
        body, name=name, grid=(t // tm,),
        out_shape=[jax.ShapeDtypeStruct((t, d), F32), jax.ShapeDtypeStruct((1, d), F32)] + c_shapes,
        in_specs=[tok, tok, row, act, act, whole, whole] + c_in_specs,
        out_specs=[tok, row] + c_out_specs,
        scratch_shapes=c_scratch,
        compiler_params=_cp(("arbitrary",)),
    )(dy, x, gn, dg, du, wg_t, wu_t, *c_inputs)
    return res[:2], res[2:]


def _token_products(lhs_list, rhs, name, tt, tf, comm=None, comm_at=None):
    n_l = len(lhs_list)
    t, f_all = lhs_list[0].shape
    d = rhs.shape[1]
    n_t = t // tt
    n_ci, n_co, c_in_specs, c_out_specs, c_shapes, c_scratch, c_inputs = _comm_parts(comm)

    def body(*refs):
        lhs_refs, (rhs_ref,), c_in, out_refs, c_out, accs, c_scr = _split(refs, n_l, 1, n_ci, n_l, n_co, n_l)
        s = pl.program_id(1)
        if comm:
            _run_hosted(comm, comm_at, pl.program_id(0) * n_t + s, c_in, c_out, c_scr)
        rv = rhs_ref[...]
        for l_ref, acc in zip(lhs_refs, accs):
            _accumulate(acc, _dot_tn(l_ref[...], rv), s == 0)

        @pl.when(s == n_t - 1)
        def _():
            for o_ref, acc in zip(out_refs, accs):
                o_ref[...] = acc[...].astype(BF16)

    act = pl.BlockSpec((tt, tf), lambda f, s: (s, f))
    tok = pl.BlockSpec((tt, d), lambda f, s: (s, 0))
    wblk = pl.BlockSpec((tf, d), lambda f, s: (f, 0))
    res = pl.pallas_call(
        body, name=name, grid=(f_all // tf, n_t),
        out_shape=[jax.ShapeDtypeStruct((f_all, d), BF16)] * n_l + c_shapes,
        in_specs=[act] * n_l + [tok] + c_in_specs, out_specs=[wblk] * n_l + c_out_specs,
        scratch_shapes=[pltpu.VMEM((tf, d), F32)] * n_l + c_scratch,
        compiler_params=_cp(("arbitrary", "arbitrary")),
    )(*lhs_list, rhs, *c_inputs)
    return res[:n_l], res[n_l:]


def _swap_halves(t):
    w = t.shape[-1]
    lane = lax.broadcasted_iota(jnp.int32, (1, w), 1)
    return jnp.where((lane % HEAD_DIM) < HEAD_DIM // 2, pltpu.roll(t, w - HEAD_DIM // 2, 1), pltpu.roll(t, HEAD_DIM // 2, 1))


def _rope(t, cos, sin_signed):
    reps = t.shape[-1] // LANES
    return t * jnp.tile(cos, (1, reps)) + _swap_halves(t) * jnp.tile(sin_signed, (1, reps))


def _rope_bwd(dt, cos, sin_signed):
    reps = dt.shape[-1] // LANES
    return dt * jnp.tile(cos, (1, reps)) + _swap_halves(dt * jnp.tile(sin_signed, (1, reps)))


DILATIONS = tuple(dil for _, dil in B_PATTERNS if dil > 1)


def _seg_shape(t, dil, w, dtype):
    return jax.ShapeDtypeStruct((dil, t // dil, w), dtype)


def _seg_spec(tm, dil, w):
    return pl.BlockSpec((dil, tm // dil, w), lambda i: (0, i, 0))


def _tile_scratch(tm, w):
    return [pltpu.VMEM((tm, LANES), F32)] * (w // LANES)


def _put_tile(tile, val):
    for c, ref in enumerate(tile):
        ref[...] = val[:, c * LANES:(c + 1) * LANES]


def _get_tile(tile):
    return jnp.concatenate([ref[...] for ref in tile], axis=1)


def _scatter_to_segments(tile, seg_refs):
    for seg_ref, dil in zip(seg_refs, DILATIONS):
        rows = tile[0].shape[0] // dil
        for r in range(dil):
            for c, ref in enumerate(tile):
                seg_ref[r, :, c * LANES:(c + 1) * LANES] = ref[pl.ds(r, rows, stride=dil), :].astype(seg_ref.dtype)


def _gather_from_segments(seg_ref, dil, tile, add=False):
    rows = tile[0].shape[0] // dil
    for r in range(dil):
        for c, ref in enumerate(tile):
            idx = (pl.ds(r, rows, stride=dil), slice(None))
            v = seg_ref[r, :, c * LANES:(c + 1) * LANES].astype(F32)
            ref[idx] = ref[idx] + v if add else v


def _in_proj_fwd(x, gn, win_t, cos, sin_signed, name, tm):
    t, d = x.shape
    in_w = win_t.shape[0]
    n_dil = len(DILATIONS)

    def body(x_ref, gn_ref, w_ref, cos_ref, sin_ref, h_ref, aq_ref, akx_ref, avx_ref, bq_ref, bk_ref, bv_ref, *rest):
        seg_refs, tile = rest[:3 * n_dil], rest[3 * n_dil:]
        xv = x_ref[...]
        h = ((xv * _rstd(xv)) * gn_ref[...]).astype(BF16)
        h_ref[...] = h
        p = _dot_nt(h, w_ref[...])
        cs, sn = cos_ref[...], sin_ref[...]
        o = 0
        aq_ref[...] = (_rope(p[:, o:o + A_Q_W], cs, sn) * QK_SCALE).astype(BF16)
        o += A_Q_W
        ak = _rope(p[:, o:o + A_KV_W], cs, sn)
        o += A_KV_W
        av = p[:, o:o + A_KV_W]
        o += A_KV_W
        low = lax.broadcasted_iota(jnp.int32, (1, LANES), 1) < HEAD_DIM
        for src, dst in ((ak, akx_ref), (av, avx_ref)):
            other = pltpu.roll(src, HEAD_DIM, 1)
            dst[0] = jnp.where(low, src, other).astype(BF16)
            dst[1] = jnp.where(low, other, src).astype(BF16)
        for k, nat_ref in enumerate((bq_ref, bk_ref, bv_ref)):
            val = p[:, o:o + B_W]
            o += B_W
            if k < 2:
                val = _rope(val, cs, sn)
            if k == 0:
                val = val * QK_SCALE
            nat_ref[...] = val.astype(BF16)
            _put_tile(tile, val)
            _scatter_to_segments(tile, seg_refs[k * n_dil:(k + 1) * n_dil])

    tok = lambda w: pl.BlockSpec((tm, w), lambda i: (i, 0))
    kvx = pl.BlockSpec((2, tm, LANES), lambda i: (0, i, 0))
    sd = lambda *s: jax.ShapeDtypeStruct(s, BF16)
    res = pl.pallas_call(
        body, name=name, grid=(t // tm,),
        out_shape=[sd(t, d), sd(t, A_Q_W), sd(2, t, LANES), sd(2, t, LANES), sd(t, B_W), sd(t, B_W), sd(t, B_W)]
        + [_seg_shape(t, dil, B_W, BF16) for _ in range(3) for dil in DILATIONS],
        in_specs=[tok(d), pl.BlockSpec((1, d), lambda i: (0, 0)), pl.BlockSpec((in_w, d), lambda i: (0, 0)),
                  tok(LANES), tok(LANES)],
        out_specs=[tok(d), tok(A_Q_W), kvx, kvx, tok(B_W), tok(B_W), tok(B_W)]
        + [_seg_spec(tm, dil, B_W) for _ in range(3) for dil in DILATIONS],
        scratch_shapes=_tile_scratch(tm, B_W),
        compiler_params=_cp(("arbitrary",)),
    )(x, gn, win_t, cos, sin_signed)
    return res[:7], [res[7 + k * n_dil:7 + (k + 1) * n_dil] for k in range(3)]


def _in_proj_bwd(dres, x, gn, win_t, h, cos, sin_signed, daq, dakx, davx, dbq, dbk, dbv, name, tm):
    t, d = x.shape
    in_w = win_t.shape[0]
    n_t = t // tm
    chunk = in_w // 3

    def body(dres_ref, x_ref, gn_ref, w_ref, h_ref, cos_ref, sin_ref, daq_ref, dakx_ref, davx_ref, dbq_ref, dbk_ref,
             dbv_ref, dx_ref, dgn_ref, gw_ref, half_ref, dp_s, gw_s):
        i = pl.program_id(0)
        cs, sn = cos_ref[...], sin_ref[...]
        low = lax.broadcasted_iota(jnp.int32, (1, LANES), 1) < HEAD_DIM

        def fold(ref):
            a, b = ref[0].astype(F32), ref[1].astype(F32)
            return jnp.where(low, a + pltpu.roll(a, HEAD_DIM, 1), b + pltpu.roll(b, HEAD_DIM, 1))

        o = 0
        dp_s[:, o:o + A_Q_W] = _rope_bwd(daq_ref[...].astype(F32) * QK_SCALE, cs, sn).astype(BF16)
        o += A_Q_W
        dp_s[:, o:o + A_KV_W] = _rope_bwd(fold(dakx_ref), cs, sn).astype(BF16)
        o += A_KV_W
        dp_s[:, o:o + A_KV_W] = fold(davx_ref).astype(BF16)
        o += A_KV_W
        dp_s[:, o:o + B_W] = _rope_bwd(dbq_ref[...].astype(F32) * QK_SCALE, cs, sn).astype(BF16)
        o += B_W
        dp_s[:, o:o + B_W] = _rope_bwd(dbk_ref[...].astype(F32), cs, sn).astype(BF16)
        o += B_W
        dp_s[:, o:o + B_W] = dbv_ref[...].astype(BF16)
        dh = _dot_nn(dp_s[...], w_ref[...])
        hv = h_ref[...]
        for c0 in range(0, in_w, chunk):
            _accumulate(gw_s.at[pl.ds(c0, chunk), :], _dot_tn(dp_s[:, c0:c0 + chunk], hv), i == 0)
        xv = x_ref[...]
        dxn, dgn = _norm_bwd(dh, xv, _rstd(xv), gn_ref[...])
        dx = dres_ref[...] + dxn
        dx_ref[...] = dx
        half_ref[...] = (FFN_RES_WEIGHT * dx).astype(BF16)
        _accumulate(dgn_ref, dgn, i == 0)

        @pl.when(i == n_t - 1)
        def _():
            gw_ref[...] = gw_s[...].astype(BF16)

    tok = lambda w: pl.BlockSpec((tm, w), lambda i: (i, 0))
    row = pl.BlockSpec((1, d), lambda i: (0, 0))
    whole = pl.BlockSpec((in_w, d), lambda i: (0, 0))
    kvx = pl.BlockSpec((2, tm, LANES), lambda i: (0, i, 0))
    return pl.pallas_call(
        body, name=name, grid=(n_t,),
        out_shape=[jax.ShapeDtypeStruct((t, d), F32), jax.ShapeDtypeStruct((1, d), F32),
                   jax.ShapeDtypeStruct((in_w, d), BF16), jax.ShapeDtypeStruct((t, d), BF16)],
        in_specs=[tok(d), tok(d), row, whole, tok(d), tok(LANES), tok(LANES), tok(A_Q_W), kvx, kvx,
                  tok(B_W), tok(B_W), tok(B_W)],
        out_specs=[tok(d), row, whole, tok(d)],
        scratch_shapes=[pltpu.VMEM((tm, in_w), BF16), pltpu.VMEM((in_w, d), F32)],
        compiler_params=_cp(("arbitrary",)),
    )(dres, x, gn, win_t, h, cos, sin_signed, daq, dakx, davx, dbq, dbk, dbv)


def _merge_out_proj_fwd(x, a_out, outs, lses, wout, name, tm):
    t, d = x.shape
    n_dil = len(DILATIONS)

    def body(x_ref, a_ref, *rest):
        o_refs, l_refs, (w_ref, y_ref, b_ref, lt_ref), b_segs, lt_segs, scratch = _split(rest, 1 + n_dil, 1 + n_dil, 4, n_dil, n_dil)
        n_c = B_W // LANES
        tiles = [scratch[j * n_c:(j + 1) * n_c] for j in range(2 * n_dil)]
        os_, ls = [o_refs[0][...].astype(F32)], [l_refs[0][...]]
        for k, dil in enumerate(DILATIONS):
            _gather_from_segments(o_refs[1 + k], dil, tiles[2 * k])
            _gather_from_segments(l_refs[1 + k], dil, tiles[2 * k + 1])
            os_.append(_get_tile(tiles[2 * k]))
            ls.append(_get_tile(tiles[2 * k + 1]))
        mx = functools.reduce(jnp.maximum, ls)
        es = [jnp.exp(l - mx) for l in ls]
        den = functools.reduce(jnp.add, es)
        b = functools.reduce(jnp.add, [e * o for e, o in zip(es, os_)]) / den
        lt = mx + jnp.log(den)
        bb = b.astype(BF16)
        b_ref[...] = bb
        lt_ref[...] = lt
        y_ref[...] = x_ref[...] + _dot_nn(a_ref[...], w_ref[0:A_Q_W, :]) + _dot_nn(bb, w_ref[A_Q_W:A_Q_W + B_W, :])
        _put_tile(tiles[0], b)
        _scatter_to_segments(tiles[0], b_segs)
        _put_tile(tiles[1], lt)
        _scatter_to_segments(tiles[1], lt_segs)

    tok = lambda w: pl.BlockSpec((tm, w), lambda i: (i, 0))
    segs = [_seg_spec(tm, dil, B_W) for dil in DILATIONS]
    res = pl.pallas_call(
        body, name=name, grid=(t // tm,),
        out_shape=[jax.ShapeDtypeStruct((t, d), F32), jax.ShapeDtypeStruct((t, B_W), BF16), jax.ShapeDtypeStruct((t, B_W), F32)]
        + [_seg_shape(t, dil, B_W, BF16) for dil in DILATIONS] + [_seg_shape(t, dil, B_W, F32) for dil in DILATIONS],
        in_specs=[tok(d), tok(A_Q_W)] + ([tok(B_W)] + segs) * 2 + [pl.BlockSpec(wout.shape, lambda i: (0, 0))],
        out_specs=[tok(d), tok(B_W), tok(B_W)] + segs * 2,
        scratch_shapes=_tile_scratch(tm, B_W) * (2 * n_dil),
        compiler_params=_cp(("arbitrary",)),
    )(x, a_out, *outs, *lses, wout)
    return res[0], [res[1]] + list(res[3:3 + n_dil]), [res[2]] + list(res[3 + n_dil:])


def _out_proj_bwd(dy, a_out, b_out, wout, name, tm):
    t, d = dy.shape
    n_t = t // tm
    n_dil = len(DILATIONS)

    def body(dy_ref, a_ref, b_ref, w_ref, da_ref, db_ref, gw_ref, *rest):
        db_segs, gw_s, tile = rest[:n_dil], rest[n_dil], rest[n_dil + 1:]
        i = pl.program_id(0)
        dyb = dy_ref[...].astype(BF16)
        da_ref[...] = _dot_nt(dyb, w_ref[0:A_Q_W, :]).astype(BF16)
        db = _dot_nt(dyb, w_ref[A_Q_W:A_Q_W + B_W, :])
        db_ref[...] = db.astype(BF16)
        _put_tile(tile, db)
        _scatter_to_segments(tile, db_segs)
        ga = _dot_tn(a_ref[...], dyb)
        gb = _dot_tn(b_ref[...], dyb)

        @pl.when(i == 0)
        def _():
            gw_s[0:A_Q_W, :] = ga
            gw_s[A_Q_W:A_Q_W + B_W, :] = gb

        @pl.when(i > 0)
        def _():
            gw_s[0:A_Q_W, :] += ga
            gw_s[A_Q_W:A_Q_W + B_W, :] += gb

        @pl.when(i == n_t - 1)
        def _():
            gw_ref[...] = gw_s[...].astype(BF16)

    tok = lambda w: pl.BlockSpec((tm, w), lambda i: (i, 0))
    whole = pl.BlockSpec(wout.shape, lambda i: (0, 0))
    res = pl.pallas_call(
        body, name=name, grid=(n_t,),
        out_shape=[jax.ShapeDtypeStruct((t, A_Q_W), BF16), jax.ShapeDtypeStruct((t, B_W), BF16),
                   jax.ShapeDtypeStruct(wout.shape, BF16)] + [_seg_shape(t, dil, B_W, BF16) for dil in DILATIONS],
        in_specs=[tok(d), tok(A_Q_W), tok(B_W), whole],
        out_specs=[tok(A_Q_W), tok(B_W), whole] + [_seg_spec(tm, dil, B_W) for dil in DILATIONS],
        scratch_shapes=[pltpu.VMEM(wout.shape, F32)] + _tile_scratch(tm, B_W),
        compiler_params=_cp(("arbitrary",)),
    )(dy, a_out, b_out, wout)
    return res[0], [res[1]] + list(res[3:]), res[2]


def _sum_pattern_grads(per_pattern, name, tm):
    t = per_pattern[0][0].shape[0]
    n_dil = len(DILATIONS)

    def body(*refs):
        ins, outs, tile = _split(refs, 3 * (1 + n_dil), 3)
        for j, o_ref in enumerate(outs):
            _put_tile(tile, ins[j][...].astype(F32))
            for k, dil in enumerate(DILATIONS):
                _gather_from_segments(ins[3 * (1 + k) + j], dil, tile, add=True)
            o_ref[...] = _get_tile(tile).astype(BF16)

    tok = pl.BlockSpec((tm, B_W), lambda i: (i, 0))
    flat = [a if k == 0 else a.reshape(DILATIONS[k - 1], -1, B_W) for k, grads in enumerate(per_pattern) for a in grads]
    return pl.pallas_call(
        body, name=name, grid=(t // tm,),
        out_shape=[jax.ShapeDtypeStruct((t, B_W), BF16)] * 3,
        in_specs=[tok] * 3 + [_seg_spec(tm, dil, B_W) for dil in DILATIONS for _ in range(3)],
        out_specs=[tok] * 3, scratch_shapes=_tile_scratch(tm, B_W),
        compiler_params=_cp(("arbitrary",)),
    )(*flat)


SUB_ROWS = 64


def _sub_band(r0, hw, win, seg_lo, seg_len, t, rel, col):
    ks = pl.multiple_of(jnp.clip(r0 - hw, 0, t - win), SUB_ROWS)
    kpos = col + ks
    valid = (jnp.abs(rel + (ks - r0)) <= hw) & (kpos >= seg_lo) & (kpos < seg_lo + seg_len)
    return ks, valid


def _split_heads(v, low):
    zero = jnp.zeros_like(v)
    return jnp.concatenate([jnp.where(low, v, zero), jnp.where(low, zero, v)], axis=0)


def _kv_spec(kv, t):
    if kv.ndim == 3:
        return pl.BlockSpec((None, t, LANES), lambda p, i: (p // 2, 0, 0))
    return pl.BlockSpec((t, LANES), lambda p, i: (0, p))


def _attn_fwd(q, k, v, sink, name, hw, seg_len, tq, has_sink):
    t, width = q.shape
    sb = SUB_ROWS
    win = 2 * hw + LANES

    def body(sink_ref, q_ref, k_ref, v_ref, o_ref, lse_ref):
        p, i = pl.program_id(0), pl.program_id(1)
        q0 = i * tq
        seg_lo = (q0 // seg_len) * seg_len
        low = lax.broadcasted_iota(jnp.int32, (1, LANES), 1) < HEAD_DIM
        rel = lax.broadcasted_iota(jnp.int32, (sb, win), 1) - lax.broadcasted_iota(jnp.int32, (sb, win), 0)
        col = lax.broadcasted_iota(jnp.int32, (1, win), 1)
        subs = []
        for j in range(tq // sb):
            rows = pl.ds(j * sb, sb)
            ks, valid = _sub_band(q0 + j * sb, hw, win, seg_lo, seg_len, t, rel, col)
            subs.append((rows, ks, valid, _dot_nt(_split_heads(q_ref[rows, :], low), k_ref[pl.ds(ks, win), :])))
        for rows, ks, valid, s in subs:
            vw = v_ref[pl.ds(ks, win), :]
            es, inv, lses = [], [], []
            for a in range(2):
                sa = jnp.where(valid, s[a * sb:(a + 1) * sb], NEG)
                m = jnp.max(sa, axis=1, keepdims=True)
                if has_sink:
                    sk = sink_ref[2 * p + a]
                    m = jnp.maximum(m, sk)
                e = jnp.exp(sa - m)
                den = jnp.sum(e, axis=1, keepdims=True)
                if has_sink:
                    den = den + jnp.exp(sk - m)
                es.append(e.astype(BF16))
                inv.append(1.0 / den)
                lses.append(m + jnp.log(den))
            pv = _dot_nn(jnp.concatenate(es, axis=0), vw)
            o_ref[rows, :] = jnp.where(low, pv[0:sb] * inv[0], pv[sb:2 * sb] * inv[1]).astype(BF16)
            lse_ref[rows, :] = jnp.where(low, lses[0], lses[1])

    tile = pl.BlockSpec((tq, LANES), lambda p, i: (i, p))
    return pl.pallas_call(
        body, name=name, grid=(width // LANES, t // tq),
        out_shape=[jax.ShapeDtypeStruct((t, width), BF16), jax.ShapeDtypeStruct((t, width), F32)],
        in_specs=[pl.BlockSpec(memory_space=pltpu.SMEM), tile, _kv_spec(k, t), _kv_spec(v, t)],
        out_specs=[tile, tile],
        compiler_params=_cp(("arbitrary", "arbitrary")),
    )(sink, q, k, v)


def _attn_bwd(q, k, v, o, do, lse, sink, name, hw, seg_len, tq, has_sink, comm=None, comm_at=None):
    t, width = q.shape
    sb = SUB_ROWS
    win = 2 * hw + LANES
    n_q = t // tq
    shared_kv = k.ndim == 3
    n_ci, n_co, c_in_specs, c_out_specs, c_shapes, c_scratch, c_inputs = _comm_parts(comm)

    def body(*refs):
        ((sink_ref, q_ref, k_ref, v_ref, o_ref, do_ref, lse_ref), c_in, (dq_ref, dk_ref, dv_ref, ds_ref), c_out,
         (dk_s, dv_s), c_scr) = _split(refs, 7, n_ci, 4, n_co, 2)
        p, i = pl.program_id(0), pl.program_id(1)
        if comm:
            _run_hosted(comm, comm_at, p * n_q + i, c_in, c_out, c_scr)
        fresh = (i == 0) & (p % 2 == 0) if shared_kv else i == 0
        last = (i == n_q - 1) & (p % 2 == 1) if shared_kv else i == n_q - 1

        @pl.when(fresh)
        def _():
            dk_s[...] = jnp.zeros_like(dk_s)
            dv_s[...] = jnp.zeros_like(dv_s)

        q0 = i * tq
        seg_lo = (q0 // seg_len) * seg_len
        low = lax.broadcasted_iota(jnp.int32, (1, LANES), 1) < HEAD_DIM
        rel = lax.broadcasted_iota(jnp.int32, (sb, win), 1) - lax.broadcasted_iota(jnp.int32, (sb, win), 0)
        col = lax.broadcasted_iota(jnp.int32, (1, win), 1)
        dsink = [jnp.zeros((1, 1), F32), jnp.zeros((1, 1), F32)]
        subs = []
        for j in range(tq // sb):
            rows = pl.ds(j * sb, sb)
            ks, valid = _sub_band(q0 + j * sb, hw, win, seg_lo, seg_len, t, rel, col)
            kw = k_ref[pl.ds(ks, win), :]
            dov = do_ref[rows, :]
            q2 = _split_heads(q_ref[rows, :], low)
            do2 = _split_heads(dov, low)
            subs.append((rows, ks, valid, kw, dov, q2, do2, _dot_nt(q2, kw), _dot_nt(do2, v_ref[pl.ds(ks, win), :])))
        probs = []
        for rows, ks, valid, kw, dov, q2, do2, s, dpr in subs:
            prod = dov.astype(F32) * o_ref[rows, :].astype(F32)
            lse_t = lse_ref[rows, :]
            prs, dss = [], []
            for a in range(2):
                mine = low if a == 0 else jnp.logical_not(low)
                lse_a = jnp.max(jnp.where(mine, lse_t, -jnp.inf), axis=1, keepdims=True)
                delta = jnp.sum(jnp.where(mine, prod, 0.0), axis=1, keepdims=True)
                pr = jnp.exp(jnp.where(valid, s[a * sb:(a + 1) * sb], NEG) - lse_a)
                prs.append(pr.astype(BF16))
                dss.append((pr * (dpr[a * sb:(a + 1) * sb] - delta)).astype(BF16))
                if has_sink:
                    dsink[a] = dsink[a] - jnp.sum(jnp.exp(sink_ref[2 * p + a] - lse_a) * delta, axis=0, keepdims=True)
            probs.append((jnp.concatenate(prs, axis=0), jnp.concatenate(dss, axis=0)))
        for (rows, ks, valid, kw, dov, q2, do2, s, dpr), (pr2, ds2) in zip(subs, probs):
            dv_s[pl.ds(ks, win), :] += _dot_tn(pr2, do2)
            dk_s[pl.ds(ks, win), :] += _dot_tn(ds2, q2)
            dq2 = _dot_nn(ds2, kw)
            dq_ref[rows, :] = jnp.where(low, dq2[0:sb], dq2[sb:2 * sb]).astype(BF16)
        ds_ref[...] = jnp.broadcast_to(jnp.where(low, dsink[0], dsink[1]), ds_ref.shape)

        @pl.when(last)
        def _():
            dk_ref[...] = dk_s[...].astype(BF16)
            dv_ref[...] = dv_s[...].astype(BF16)

    tile = pl.BlockSpec((tq, LANES), lambda p, i: (i, p))
    kv_shape = jax.ShapeDtypeStruct(k.shape, BF16)
    res = pl.pallas_call(
        body, name=name, grid=(width // LANES, n_q),
        out_shape=[jax.ShapeDtypeStruct((t, width), BF16), kv_shape, kv_shape,
                   jax.ShapeDtypeStruct((width // LANES, n_q, 8, LANES), F32)] + c_shapes,
        in_specs=[pl.BlockSpec(memory_space=pltpu.SMEM), tile, _kv_spec(k, t), _kv_spec(v, t), tile, tile, tile] + c_in_specs,
        out_specs=[tile, _kv_spec(k, t), _kv_spec(v, t), pl.BlockSpec((None, None, 8, LANES), lambda p, i: (p, i, 0, 0))] + c_out_specs,
        scratch_shapes=[pltpu.VMEM((t, LANES), F32)] * 2 + c_scratch,
        compiler_params=_cp(("arbitrary", "arbitrary")),
    )(sink, q, k, v, o, do, lse, *c_inputs)
    return res[:4], res[4:]


def _final_norm_loss(x, gn, target, name, tm):
    t, d = x.shape

    def body(x_ref, gn_ref, tg_ref, dx_ref, dgn_ref, sq_ref, half_ref):
        i = pl.program_id(0)
        xv, gnv = x_ref[...], gn_ref[...]
        r = _rstd(xv)
        diff = (xv * r) * gnv - tg_ref[...]
        dxn, dgn = _norm_bwd(diff * (1.0 / d), xv, r, gnv)
        dx_ref[...] = dxn
        half_ref[...] = (FFN_RES_WEIGHT * dxn).astype(BF16)
        _accumulate(dgn_ref, dgn, i == 0)
        _accumulate(sq_ref, jnp.sum(diff * diff, axis=0, keepdims=True), i == 0)

    tok = pl.BlockSpec((tm, d), lambda i: (i, 0))
    row = pl.BlockSpec((1, d), lambda i: (0, 0))
    return pl.pallas_call(
        body, name=name, grid=(t // tm,),
        out_shape=[jax.ShapeDtypeStruct((t, d), F32), jax.ShapeDtypeStruct((1, d), F32), jax.ShapeDtypeStruct((1, d), F32),
                   jax.ShapeDtypeStruct((t, d), BF16)],
        in_specs=[tok, row, tok], out_specs=[tok, row, row, tok],
        compiler_params=_cp(("arbitrary",)),
    )(x, gn, target)


def _adamw(w, g, m, v, name):
    def body(w_ref, g_ref, m_ref, v_ref, d_ref, nm_ref, nv_ref):
        gv = g_ref[...]
        nm = ADAM_B1 * m_ref[...] + (1.0 - ADAM_B1) * gv
        nv = ADAM_B2 * v_ref[...] + (1.0 - ADAM_B2) * (gv * gv)
        m_hat = nm / (1.0 - ADAM_B1 ** ADAM_STEP)
        v_hat = nv / (1.0 - ADAM_B2 ** ADAM_STEP)
        d_ref[...] = -ADAM_LR * (m_hat / (jnp.sqrt(v_hat) + ADAM_EPS) + ADAM_WD * w_ref[...])
        nm_ref[...] = nm
        nv_ref[...] = nv

    shape = jax.ShapeDtypeStruct(w.shape, F32)
    return pl.pallas_call(body, name=name, out_shape=[shape, shape, shape], compiler_params=_cp())(w, g, m, v)


def _adamw_rows(land, j, rows, w, m, v, name):
    d = w.shape[1]

    def body(l_ref, w_ref, m_ref, v_ref, g_ref, d_ref, nm_ref, nv_ref):
        gv = l_ref[0].astype(F32)
        for s in range(1, N_DEV):
            gv = gv + l_ref[s].astype(F32)
        g_ref[...] = gv
        nm = ADAM_B1 * m_ref[...] + (1.0 - ADAM_B1) * gv
        nv = ADAM_B2 * v_ref[...] + (1.0 - ADAM_B2) * (gv * gv)
        m_hat = nm / (1.0 - ADAM_B1 ** ADAM_STEP)
        v_hat = nv / (1.0 - ADAM_B2 ** ADAM_STEP)
        d_ref[...] = -ADAM_LR * (m_hat / (jnp.sqrt(v_hat) + ADAM_EPS) + ADAM_WD * w_ref[...])
        nm_ref[...] = nm
        nv_ref[...] = nv

    whole = pl.BlockSpec((rows, d), lambda i: (0, 0))
    shape = jax.ShapeDtypeStruct((rows, d), F32)
    return pl.pallas_call(
        body, name=name, grid=(1,), out_shape=[shape] * 4,
        in_specs=[pl.BlockSpec((N_DEV, rows, d), lambda i: (0, j, 0)), whole, whole, whole], out_specs=[whole] * 4,
        compiler_params=_cp(("arbitrary",)),
    )(land, w, m, v)


def _sum_small(land, name):
    def body(l_ref, o_ref):
        acc = l_ref[0]
        for s in range(1, N_DEV):
            acc = acc + l_ref[s]
        o_ref[...] = acc

    return pl.pallas_call(body, name=name, out_shape=jax.ShapeDtypeStruct(land.shape[1:], F32), compiler_params=_cp())(land)


def _rope_lanes(positions):
    inv_freq = 1.0 / (ROPE_THETA ** (jnp.arange(0, HEAD_DIM, 2, dtype=F32) / HEAD_DIM))
    ang = positions.astype(F32)[:, None] * inv_freq
    cos, sin = jnp.cos(ang), jnp.sin(ang)
    return jnp.concatenate([cos, cos, cos, cos], axis=1), jnp.concatenate([-sin, sin, -sin, sin], axis=1)


def kernel(x, positions, norm_ffn1, w_gate1, w_up1, w_down1, norm_mix, w_in, a_sink, w_out, norm_ffn2, w_gate2, w_up2, w_down2, norm_final, loss_target, m_norm_ffn1, m_w_gate1, m_w_up1, m_w_down1, m_norm_mix, m_w_in, m_a_sink, m_w_out, m_norm_ffn2, m_w_gate2, m_w_up2, m_w_down2, m_norm_final, v_norm_ffn1, v_w_gate1, v_w_up1, v_w_down1, v_norm_mix, v_w_in, v_a_sink, v_w_out, v_norm_ffn2, v_w_gate2, v_w_up2, v_w_down2, v_norm_final):
    x = x[0]
    target = loss_target[0]
    t, d = x.shape
    tm = min(1024, t)
    tm_mix = min(512, t)
    tf = 256
    tq = min(512, t // 16)

    tm_hidden = min(2048, t)
    tt = min(1024, t)
    f_all = w_gate1.shape[2] * N_DEV
    tfp = f_all // 2
    n_steps = (t // tm) * (f_all // tf)

    def stacked(shards):
        return jnp.concatenate([s.astype(BF16) for s in shards], axis=0), [s.shape[0] for s in shards]

    packed1, rows1 = stacked([w_gate1[0].T, w_up1[0].T, w_down1[0]])
    packed2, rows2 = stacked([w_in[0].T, w_out[0], w_gate2[0].T, w_up2[0].T, w_down2[0]])
    wg1, wu1, wd1 = _run_alone(_gather_plan(packed1, rows1), "gather_ffn1_weights")

    cos, sin_signed = _rope_lanes(positions[0])
    sink = a_sink[0]
    no_sink = jnp.zeros_like(sink)

    (x1, h1, gp1, up1, hid1), (win, wout, wg2, wu2, wd2) = _ffn_fwd(
        x, norm_ffn1, wg1, wu1, wd1, "ffn1_fwd", tm, tf, _gather_plan(packed2, rows2), [0, (3 * n_steps) // 4, n_steps - 1])
    (h_mix, aq, akx, avx, bq, bk, bv), seg_qkv = _in_proj_fwd(x1, norm_mix, win, cos, sin_signed, "in_proj_fwd", tm_mix)
    a_out, a_lse = _attn_fwd(aq, akx, avx, sink, "attn_a_fwd", A_HALF_WINDOW, t, tq, True)
    rows_of = lambda a: a.reshape(t, B_W)
    segments_of = lambda a, dil: a if dil == 1 else a.reshape(dil, t // dil, B_W)
    b_qkv, b_outs, b_lses = [], [], []
    for n, (window, dil) in enumerate(B_PATTERNS):
        qs, ks, vs = (bq, bk, bv) if dil == 1 else (rows_of(seg_qkv[k][n - 1]) for k in range(3))
        o_seg, lse_seg = _attn_fwd(qs, ks, vs, no_sink, f"attn_b{dil}_fwd", window // (2 * dil), t // dil, tq, False)
        b_qkv.append((qs, ks, vs))
        b_outs.append(segments_of(o_seg, dil))
        b_lses.append(segments_of(lse_seg, dil))
    x2, b_out, b_lse = _merge_out_proj_fwd(x1, a_out, b_outs, b_lses, wout, "out_proj_fwd", tm_mix)
    (x3, h2, gp2, up2, hid2), _ = _ffn_fwd(x2, norm_ffn2, wg2, wu2, wd2, "ffn2_fwd", tm, tf)

    gfinal = norm_final.reshape(1, d)
    dx3, dg_final, sq, dout2 = _final_norm_loss(x3, gfinal, target, "final_norm_loss", tm)

    dgt2, dut2 = _ffn_bwd_hidden(dout2, gp2, up2, wd2, "ffn2_bwd_hidden", tm_hidden, tf)
    (dx2, dg_ffn2), _ = _ffn_bwd_input(dx3, x2, norm_ffn2, dgt2, dut2, wg2, wu2, "ffn2_bwd_input", tm_mix)
    (gwg2, gwu2), _ = _token_products([dgt2, dut2], h2, "ffn2_bwd_gate_up", tt, tfp)
    (gwd2,), _ = _token_products([hid2], dout2, "ffn2_bwd_down", tt, tfp)

    n_att = (A_Q_W // LANES) * (t // tq)
    n_prod = (f_all // tfp) * (t // tt)
    da, db, gwout = _out_proj_bwd(dx2, a_out, b_out[0], wout, "out_proj_bwd", tm_mix)
    (daq, dakx, davx, dsink_parts), (land_wg2,) = _attn_bwd(
        aq, akx, avx, a_out, da, a_lse, sink, "attn_a_bwd", A_HALF_WINDOW, t, tq, True, _exchange_plan([[gwg2]]), [0, n_att - 1])
    pattern_grads, pattern_lands = [], []
    for n, ((window, dil), (qs, ks, vs)) in enumerate(zip(B_PATTERNS, b_qkv)):
        carried = [gwu2, gwd2][n:n + 1]
        grads, lands = _attn_bwd(qs, ks, vs, rows_of(b_out[n]), rows_of(db[n]), rows_of(b_lse[n]), no_sink, f"attn_b{dil}_bwd",
                                 window // (2 * dil), t // dil, tq, False,
                                 _exchange_plan([carried]) if carried else None, [0, (B_W // LANES) * (t // tq) - 1])
        pattern_grads.append(grads[:3])
        pattern_lands.extend(lands)
    land_wu2, land_wd2 = pattern_lands
    dbq, dbk, dbv = _sum_pattern_grads(pattern_grads, "sum_pattern_grads", tm_mix)
    dx1, dg_mix, gwin, dout1 = _in_proj_bwd(dx2, x1, norm_mix, win, h_mix, cos, sin_signed, daq, dakx, davx, dbq, dbk, dbv, "in_proj_bwd", tm_mix)

    (gwd1,), (land_in, land_out) = _token_products(
        [hid1], dout1, "ffn1_bwd_down", tt, tfp, _exchange_plan([[gwin], [gwout]]), [0, n_prod - 1])
    dgt1, dut1 = _ffn_bwd_hidden(dout1, gp1, up1, wd1, "ffn1_bwd_hidden", tm_hidden, tf)
    (gwg1, gwu1), (land_wd1,) = _token_products(
        [dgt1, dut1], h1, "ffn1_bwd_gate_up", tt, tfp, _exchange_plan([[gwd1]]), [0, n_prod - 1])
    (grad_x, dg_ffn1), (land_wgu1,) = _ffn_bwd_input(
        dx1, x, norm_ffn1, dgt1, dut1, wg1, wu1, "ffn1_bwd_input", tm_mix, _exchange_plan([[gwg1, gwu1]]), [0, t // tm_mix - 1])

    dsink_pairs = jnp.sum(dsink_parts[:, :, 0, :], axis=1)
    dsink = jnp.stack([dsink_pairs[:, 0], dsink_pairs[:, HEAD_DIM]], axis=1).reshape(1, -1)
    small = jnp.concatenate([dg_ffn1, dg_mix, dg_ffn2, dg_final, jnp.pad(dsink, ((0, 0), (0, d - dsink.shape[1]))),
                             sq, jnp.zeros((2, d), F32)], axis=0)
    (land_small,) = _run_alone(_exchange_plan([[jnp.tile(small, (N_DEV, 1))]]), "gather_small_gradients")
    red_small = _sum_small(land_small, "sum_small_grads")
    loss = 0.5 * jnp.sum(red_small[5]) / d

    rf = rows1[0]
    sharded = {"w_gate1": (land_wgu1, 0, rf, True), "w_up1": (land_wgu1, 1, rf, True), "w_down1": (land_wd1, 0, rf, False),
               "w_in": (land_in, 0, rows2[0], True), "w_out": (land_out, 0, rows2[1], False),
               "w_gate2": (land_wg2, 0, rf, True), "w_up2": (land_wu2, 0, rf, True), "w_down2": (land_wd2, 0, rf, False)}
    n_sink = a_sink.shape[1]
    small_grads = {"norm_ffn1": red_small[0:1], "norm_mix": red_small[1:2], "norm_ffn2": red_small[2:3], "norm_final": red_small[3],
                   "a_sink": red_small[4:5, :n_sink]}
    params = {
        "norm_ffn1": (norm_ffn1, m_norm_ffn1, v_norm_ffn1), "w_gate1": (w_gate1, m_w_gate1, v_w_gate1),
        "w_up1": (w_up1, m_w_up1, v_w_up1), "w_down1": (w_down1, m_w_down1, v_w_down1),
        "norm_mix": (norm_mix, m_norm_mix, v_norm_mix), "w_in": (w_in, m_w_in, v_w_in),
        "a_sink": (a_sink, m_a_sink, v_a_sink), "w_out": (w_out, m_w_out, v_w_out),
        "norm_ffn2": (norm_ffn2, m_norm_ffn2, v_norm_ffn2), "w_gate2": (w_gate2, m_w_gate2, v_w_gate2),
        "w_up2": (w_up2, m_w_up2, v_w_up2), "w_down2": (w_down2, m_w_down2, v_w_down2),
        "norm_final": (norm_final, m_norm_final, v_norm_final),
    }
    grad_list, deltas, new_ms, new_vs = [], [], [], []
    for name, (w, m, v) in params.items():
        if name in sharded:
            land, j, rows, is_transposed = sharded[name]
            view = (lambda a: a[0].T) if is_transposed else (lambda a: a[0])
            back = (lambda a: a.T[None]) if is_transposed else (lambda a: a[None])
            outs = [back(o) for o in _adamw_rows(land, j, rows, view(w), view(m), view(v), f"adamw_{name}")]
        else:
            as_block = (lambda a: a.reshape(1, -1)) if w.ndim == 1 else (lambda a: a)
            g = small_grads[name]
            outs = [g] + [o.reshape(w.shape) for o in _adamw(as_block(w), as_block(g), as_block(m), as_block(v), f"adamw_{name}")]
        for lst, o in zip((grad_list, deltas, new_ms, new_vs), outs):
            lst.append(o)
    return (loss, grad_x[None], *grad_list, *deltas, *new_ms, *new_vs)
```

```python
import functools
import itertools

import numpy as np
import jax
import jax.numpy as jnp
from jax import lax
from jax.experimental import pallas as pl
from jax.experimental.pallas import tpu as pltpu

F32 = jnp.float32
BF16 = jnp.bfloat16

N_DEV = 8
HEAD_DIM = 64
LANES = 128
A_Q_W, A_KV_W, B_W = 512, 128, 512
A_HALF_WINDOW = 128
B_PATTERNS = ((128, 1), (512, 4), (2048, 16))
ROPE_THETA = 10000.0
NORM_EPS = 1e-6
FFN_RES_WEIGHT = 0.5
QK_SCALE = HEAD_DIM ** -0.5
NEG = -1e30

ADAM_LR = 0.001
ADAM_B1 = 0.9
ADAM_B2 = 0.999
ADAM_EPS = 1e-08
ADAM_WD = 0.01
ADAM_STEP = 10

MESH_T = pl.DeviceIdType.MESH
VMEM_LIMIT = 56 * 1024 * 1024


def _cp(sem=None, vmem=VMEM_LIMIT):
    return pltpu.CompilerParams(dimension_semantics=sem, vmem_limit_bytes=vmem)


def _dot_nn(a, b):
    return jnp.dot(a, b, preferred_element_type=F32)


def _dot_nt(a, b):
    return lax.dot_general(a, b, (((1,), (1,)), ((), ())), preferred_element_type=F32)


def _dot_tn(a, b):
    return lax.dot_general(a, b, (((0,), (0,)), ((), ())), preferred_element_type=F32)


def _rstd(xv):
    return lax.rsqrt(jnp.mean(xv * xv, axis=-1, keepdims=True) + NORM_EPS)


def _norm_bwd(dh, xv, r, gn):
    gy = dh * gn
    c = jnp.sum(gy * xv, axis=-1, keepdims=True) * (1.0 / xv.shape[-1])
    dx = r * gy - xv * (r * r * r * c)
    dgn = jnp.sum(dh * (xv * r), axis=0, keepdims=True)
    return dx, dgn


def _accumulate(ref, val, first):
    @pl.when(first)
    def _():
        ref[...] = val

    @pl.when(jnp.logical_not(first))
    def _():
        ref[...] += val


def _mesh_pos():
    return lax.axis_index("x"), lax.axis_index("y"), lax.axis_index("c")


def _dev_index(d):
    return 4 * d[0] + 2 * d[1] + d[2]


class _Comm:
    def __init__(self, inputs, out_shape, scratch, phases):
        self.inputs, self.out_shape, self.scratch, self.phases = inputs, out_shape, scratch, phases

    def specs(self):
        any_spec = pl.BlockSpec(memory_space=pl.ANY)
        return [any_spec] * len(self.inputs), [any_spec] * len(self.out_shape)


def _run_alone(comm, name):
    n_in, n_out = len(comm.inputs), len(comm.out_shape)

    def body(*refs):
        for phase in comm.phases:
            phase(refs[:n_in], refs[n_in:n_in + n_out], refs[n_in + n_out:])

    in_specs, out_specs = comm.specs()
    return pl.pallas_call(body, name=name, out_shape=comm.out_shape, in_specs=in_specs, out_specs=out_specs,
                          scratch_shapes=comm.scratch)(*comm.inputs)


def _run_hosted(comm, at, step, ins, outs, scr):
    for phase, when in zip(comm.phases, at):
        @pl.when(step == when)
        def _(phase=phase):
            phase(ins, outs, scr)


def _split(refs, *counts):
    parts, o = [], 0
    for n in counts:
        parts.append(refs[o:o + n])
        o += n
    return parts + [refs[o:]]


def _gather_plan(packed, rows_list):
    n_w = len(rows_list)
    offs = [int(o) for o in np.cumsum([0] + list(rows_list[:-1]))]
    d = packed.shape[1]

    def tools(ins, outs, scr):
        p_ref = ins[0]
        send_sems, recv_sems, local_sem = scr
        x, y, c = _mesh_pos()
        me, sibling = (x, y, c), (x, y, 1 - c)
        chips = [(1 - x, y), (x, 1 - y), (1 - x, 1 - y)]

        def rows(w, dev):
            start = pl.multiple_of(_dev_index(dev) * rows_list[w], 16)
            return outs[w].at[pl.ds(start, rows_list[w]), :]

        def mine(w):
            return p_ref.at[pl.ds(offs[w], rows_list[w]), :]

        def copy(k, w, block, to, own):
            return pltpu.make_async_remote_copy(
                src_ref=mine(w) if own else rows(w, block), dst_ref=rows(w, block),
                send_sem=send_sems.at[k], recv_sem=recv_sems.at[k], device_id=to, device_id_type=MESH_T)

        def all_blocks(k):
            return pltpu.make_async_remote_copy(
                src_ref=p_ref, dst_ref=p_ref, send_sem=send_sems.at[k], recv_sem=recv_sems.at[k],
                device_id=me, device_id_type=MESH_T)

        return p_ref, local_sem, me, sibling, chips, c, rows, mine, copy, all_blocks

    def start(ins, outs, scr):
        _, local_sem, me, sibling, chips, c, rows, mine, copy, _ = tools(ins, outs, scr)
        for w in range(n_w):
            pltpu.make_async_copy(mine(w), rows(w, me), local_sem).start()
        for w in range(n_w):
            copy(0, w, me, sibling, True).start()
        for j, chip in enumerate(chips):
            for w in range(n_w):
                copy(1 + j, w, me, (*chip, c), True).start()

    def relay(ins, outs, scr):
        _, _, _, sibling, chips, c, _, _, copy, all_blocks = tools(ins, outs, scr)
        for j, chip in enumerate(chips):
            all_blocks(1 + j).wait_recv()
            for w in range(n_w):
                copy(4 + j, w, (*chip, c), sibling, False).start()

    def finish(ins, outs, scr):
        p_ref, local_sem, _, _, _, _, _, _, _, all_blocks = tools(ins, outs, scr)
        all_blocks(0).wait_recv()
        for j in range(3):
            all_blocks(4 + j).wait_recv()
        for k in range(7):
            all_blocks(k).wait_send()
        pltpu.make_async_copy(p_ref, p_ref, local_sem).wait()

    return _Comm(
        [packed], [jax.ShapeDtypeStruct((N_DEV * r, d), packed.dtype) for r in rows_list],
        [pltpu.SemaphoreType.DMA((7,)), pltpu.SemaphoreType.DMA((7,)), pltpu.SemaphoreType.DMA], [start, relay, finish])


def _exchange_plan(groups):
    flat = [a for g in groups for a in g]
    n_g = len(groups)
    sizes = [len(g) for g in groups]
    rows = [g[0].shape[0] // N_DEV for g in groups]
    first = [int(o) for o in np.cumsum([0] + sizes[:-1])]

    def start(srcs, lands, scr):
        send_sems, recv_sems, local_sems = scr
        x, y, c = _mesh_pos()
        me = (x, y, c)
        me_idx = _dev_index(me)

        def block(g, i, dev):
            start_row = pl.multiple_of(_dev_index(dev) * rows[g], 8)
            return srcs[first[g] + i].at[pl.ds(start_row, rows[g]), :]

        def slot(g, i):
            return lands[g].at[me_idx, pl.ds(i * rows[g], rows[g]), :]

        for g in range(n_g):
            for i in range(sizes[g]):
                pltpu.make_async_copy(block(g, i, me), slot(g, i), local_sems.at[g]).start()
        flips = [f for f in itertools.product((0, 1), repeat=3) if any(f)]
        for k, (fx, fy, fc) in enumerate(flips):
            peer = (1 - x if fx else x, 1 - y if fy else y, 1 - c if fc else c)
            for g in range(n_g):
                for i in range(sizes[g]):
                    pltpu.make_async_remote_copy(
                        src_ref=block(g, i, peer), dst_ref=slot(g, i), send_sem=send_sems.at[g, k],
                        recv_sem=recv_sems.at[g, k], device_id=peer, device_id_type=MESH_T).start()

    def finish(srcs, lands, scr):
        send_sems, recv_sems, local_sems = scr
        me = _mesh_pos()
        for k in range(7):
            for g in range(n_g):
                pltpu.make_async_remote_copy(
                    src_ref=lands[g].at[0], dst_ref=lands[g].at[0], send_sem=send_sems.at[g, k],
                    recv_sem=recv_sems.at[g, k], device_id=me, device_id_type=MESH_T).wait()
        for g in range(n_g):
            pltpu.make_async_copy(lands[g].at[0], lands[g].at[0], local_sems.at[g]).wait()

    return _Comm(
        flat, [jax.ShapeDtypeStruct((N_DEV, sizes[g] * rows[g], groups[g][0].shape[1]), groups[g][0].dtype) for g in range(n_g)],
        [pltpu.SemaphoreType.DMA((n_g, 7)), pltpu.SemaphoreType.DMA((n_g, 7)), pltpu.SemaphoreType.DMA((n_g,))],
        [start, finish])


def _comm_parts(comm):
    if comm is None:
        return 0, 0, [], [], [], [], []
    in_specs, out_specs = comm.specs()
    return len(comm.inputs), len(comm.out_shape), in_specs, out_specs, comm.out_shape, comm.scratch, comm.inputs


def _ffn_fwd(x, gn, wg_t, wu_t, wd, name, tm, tf, comm=None, comm_at=None):
    t, d = x.shape
    f_all = wg_t.shape[0]
    n_f = f_all // tf
    n_ci, n_co, c_in_specs, c_out_specs, c_shapes, c_scratch, c_inputs = _comm_parts(comm)

    def body(*refs):
        ((x_ref, gn_ref, wg_ref, wu_ref, wd_ref), c_in, (y_ref, h_ref, gp_ref, up_ref, hid_ref), c_out,
         (h_s, hid_s), c_scr) = _split(refs, 5, n_ci, 5, n_co, 2)
        f = pl.program_id(1)
        if comm:
            _run_hosted(comm, comm_at, pl.program_id(0) * n_f + f, c_in, c_out, c_scr)

        @pl.when(f == 0)
        def _():
            xv = x_ref[...]
            h = ((xv * _rstd(xv)) * gn_ref[...]).astype(BF16)
            h_s[...] = h
            h_ref[...] = h

        h = h_s[...]
        g = _dot_nt(h, wg_ref[...])
        u = _dot_nt(h, wu_ref[...])
        sg = jax.nn.sigmoid(g)
        silu = g * sg
        gp_ref[...] = (u * (sg * (1.0 + g * (1.0 - sg)))).astype(BF16)
        up_ref[...] = silu.astype(BF16)
        hid = (silu * u).astype(BF16)
        hid_ref[...] = hid
        for f0 in range(n_f):
            @pl.when(f == f0)
            def _(f0=f0):
                hid_s[:, f0 * tf:(f0 + 1) * tf] = hid

        @pl.when(f == n_f - 1)
        def _():
            y_ref[...] = x_ref[...] + FFN_RES_WEIGHT * _dot_nn(hid_s[...], wd_ref[...])

    tok = pl.BlockSpec((tm, d), lambda i, f: (i, 0))
    wblk = pl.BlockSpec((tf, d), lambda i, f: (f, 0))
    whole = pl.BlockSpec((f_all, d), lambda i, f: (0, 0))
    act = pl.BlockSpec((tm, tf), lambda i, f: (i, f))
    act_shape = jax.ShapeDtypeStruct((t, f_all), BF16)
    res = pl.pallas_call(
        body, name=name, grid=(t // tm, n_f),
        out_shape=[jax.ShapeDtypeStruct((t, d), F32), jax.ShapeDtypeStruct((t, d), BF16), act_shape, act_shape, act_shape] + c_shapes,
        in_specs=[tok, pl.BlockSpec((1, d), lambda i, f: (0, 0)), wblk, wblk, whole] + c_in_specs,
        out_specs=[tok, tok, act, act, act] + c_out_specs,
        scratch_shapes=[pltpu.VMEM((tm, d), BF16), pltpu.VMEM((tm, f_all), BF16)] + c_scratch,
        compiler_params=_cp(("arbitrary", "arbitrary")),
    )(x, gn, wg_t, wu_t, wd, *c_inputs)
    return res[:5], res[5:]


def _ffn_bwd_hidden(dout, gp, up, wd, name, tm, tf):
    t, d = dout.shape
    f_all = wd.shape[0]

    def body(dout_ref, gp_ref, up_ref, wd_ref, dg_ref, du_ref):
        dhid = _dot_nt(dout_ref[...], wd_ref[...])
        dg_ref[...] = (dhid * gp_ref[...].astype(F32)).astype(BF16)
        du_ref[...] = (dhid * up_ref[...].astype(F32)).astype(BF16)

    tok = pl.BlockSpec((tm, d), lambda i, f: (i, 0))
    wblk = pl.BlockSpec((tf, d), lambda i, f: (f, 0))
    act = pl.BlockSpec((tm, tf), lambda i, f: (i, f))
    act_shape = jax.ShapeDtypeStruct((t, f_all), BF16)
    return pl.pallas_call(
        body, name=name, grid=(t // tm, f_all // tf),
        out_shape=[act_shape, act_shape], in_specs=[tok, act, act, wblk], out_specs=[act, act],
        compiler_params=_cp(("arbitrary", "arbitrary")),
    )(dout, gp, up, wd)


def _ffn_bwd_input(dy, x, gn, dg, du, wg_t, wu_t, name, tm, comm=None, comm_at=None):
    t, d = x.shape
    f_all = wg_t.shape[0]
    n_ci, n_co, c_in_specs, c_out_specs, c_shapes, c_scratch, c_inputs = _comm_parts(comm)

    def body(*refs):
        (dy_ref, x_ref, gn_ref, dg_ref, du_ref, wg_ref, wu_ref), c_in, (dx_ref, dgn_ref), c_out, c_scr = _split(
            refs, 7, n_ci, 2, n_co)
        i = pl.program_id(0)
        if comm:
            _run_hosted(comm, comm_at, i, c_in, c_out, c_scr)
        dh = _dot_nn(dg_ref[...], wg_ref[...]) + _dot_nn(du_ref[...], wu_ref[...])
        xv = x_ref[...]
        dxn, dgn = _norm_bwd(dh, xv, _rstd(xv), gn_ref[...])
        dx_ref[...] = dy_ref[...] + dxn
        _accumulate(dgn_ref, dgn, i == 0)

    tok = pl.BlockSpec((tm, d), lambda i: (i, 0))
    row = pl.BlockSpec((1, d), lambda i: (0, 0))
    act = pl.BlockSpec((tm, f_all), lambda i: (i, 0))
    whole = pl.BlockSpec((f_all, d), lambda i: (0, 0))
    res = pl.pallas_call(
        body, name=name, grid=(t // tm,),
        out_shape=[jax.ShapeDtypeStruct((t, d), F32), jax.ShapeDtypeStruct((1, d), F32)] + c_shapes,
        in_specs=[tok, tok, row, act, act, whole, whole] + c_in_specs,
        out_specs=[tok, row] + c_out_specs,
        scratch_shapes=c_scratch,
        compiler_params=_cp(("arbitrary",)),
    )(dy, x, gn, dg, du, wg_t, wu_t, *c_inputs)
    return res[:2], res[2:]


def _token_products(lhs_list, rhs, name, tt, tf, comm=None, comm_at=None):
    n_l = len(lhs_list)
    t, f_all = lhs_list[0].shape
    d = rhs.shape[1]
    n_t = t // tt
    n_ci, n_co, c_in_specs, c_out_specs, c_shapes, c_scratch, c_inputs = _comm_parts(comm)

    def body(*refs):
        lhs_refs, (rhs_ref,), c_in, out_refs, c_out, accs, c_scr = _split(refs, n_l, 1, n_ci, n_l, n_co, n_l)
        s = pl.program_id(1)
        if comm:
            _run_hosted(comm, comm_at, pl.program_id(0) * n_t + s, c_in, c_out, c_scr)
        rv = rhs_ref[...]
        for l_ref, acc in zip(lhs_refs, accs):
            _accumulate(acc, _dot_tn(l_ref[...], rv), s == 0)

        @pl.when(s == n_t - 1)
        def _():
            for o_ref, acc in zip(out_refs, accs):
                o_ref[...] = acc[...].astype(BF16)

    act = pl.BlockSpec((tt, tf), lambda f, s: (s, f))
    tok = pl.BlockSpec((tt, d), lambda f, s: (s, 0))
    wblk = pl.BlockSpec((tf, d), lambda f, s: (f, 0))
    res = pl.pallas_call(
        body, name=name, grid=(f_all // tf, n_t),
        out_shape=[jax.ShapeDtypeStruct((f_all, d), BF16)] * n_l + c_shapes,
        in_specs=[act] * n_l + [tok] + c_in_specs, out_specs=[wblk] * n_l + c_out_specs,
        scratch_shapes=[pltpu.VMEM((tf, d), F32)] * n_l + c_scratch,
        compiler_params=_cp(("arbitrary", "arbitrary")),
    )(*lhs_list, rhs, *c_inputs)
    return res[:n_l], res[n_l:]


def _swap_halves(t):
    w = t.shape[-1]
    lane = lax.broadcasted_iota(jnp.int32, (1, w), 1)
    return jnp.where((lane % HEAD_DIM) < HEAD_DIM // 2, pltpu.roll(t, w - HEAD_DIM // 2, 1), pltpu.roll(t, HEAD_DIM // 2, 1))


def _rope(t, cos, sin_signed):
    reps = t.shape[-1] // LANES
    return t * jnp.tile(cos, (1, reps)) + _swap_halves(t) * jnp.tile(sin_signed, (1, reps))


def _rope_bwd(dt, cos, sin_signed):
    reps = dt.shape[-1] // LANES
    return dt * jnp.tile(cos, (1, reps)) + _swap_halves(dt * jnp.tile(sin_signed, (1, reps)))


DILATIONS = tuple(dil for _, dil in B_PATTERNS if dil > 1)


def _seg_shape(t, dil, w, dtype):
    return jax.ShapeDtypeStruct((dil, t // dil, w), dtype)


def _seg_spec(tm, dil, w):
    return pl.BlockSpec((dil, tm // dil, w), lambda i: (0, i, 0))


def _tile_scratch(tm, w):
    return [pltpu.VMEM((tm, LANES), F32)] * (w // LANES)


def _put_tile(tile, val):
    for c, ref in enumerate(tile):
        ref[...] = val[:, c * LANES:(c + 1) * LANES]


def _get_tile(tile):
    return jnp.concatenate([ref[...] for ref in tile], axis=1)


def _scatter_to_segments(tile, seg_refs):
    for seg_ref, dil in zip(seg_refs, DILATIONS):
        rows = tile[0].shape[0] // dil
        for r in range(dil):
            for c, ref in enumerate(tile):
                seg_ref[r, :, c * LANES:(c + 1) * LANES] = ref[pl.ds(r, rows, stride=dil), :].astype(seg_ref.dtype)


def _gather_from_segments(seg_ref, dil, tile, add=False):
    rows = tile[0].shape[0] // dil
    for r in range(dil):
        for c, ref in enumerate(tile):
            idx = (pl.ds(r, rows, stride=dil), slice(None))
            v = seg_ref[r, :, c * LANES:(c + 1) * LANES].astype(F32)
            ref[idx] = ref[idx] + v if add else v


def _in_proj_fwd(x, gn, win_t, cos, sin_signed, name, tm):
    t, d = x.shape
    in_w = win_t.shape[0]
    n_dil = len(DILATIONS)

    def body(x_ref, gn_ref, w_ref, cos_ref, sin_ref, h_ref, aq_ref, akx_ref, avx_ref, bq_ref, bk_ref, bv_ref, *rest):
        seg_refs, tile = rest[:3 * n_dil], rest[3 * n_dil:]
        xv = x_ref[...]
        h = ((xv * _rstd(xv)) * gn_ref[...]).astype(BF16)
        h_ref[...] = h
        p = _dot_nt(h, w_ref[...])
        cs, sn = cos_ref[...], sin_ref[...]
        o = 0
        aq_ref[...] = (_rope(p[:, o:o + A_Q_W], cs, sn) * QK_SCALE).astype(BF16)
        o += A_Q_W
        ak = _rope(p[:, o:o + A_KV_W], cs, sn)
        o += A_KV_W
        av = p[:, o:o + A_KV_W]
        o += A_KV_W
        low = lax.broadcasted_iota(jnp.int32, (1, LANES), 1) < HEAD_DIM
        for src, dst in ((ak, akx_ref), (av, avx_ref)):
            other = pltpu.roll(src, HEAD_DIM, 1)
            dst[0] = jnp.where(low, src, other).astype(BF16)
            dst[1] = jnp.where(low, other, src).astype(BF16)
        for k, nat_ref in enumerate((bq_ref, bk_ref, bv_ref)):
            val = p[:, o:o + B_W]
            o += B_W
            if k < 2:
                val = _rope(val, cs, sn)
            if k == 0:
                val = val * QK_SCALE
            nat_ref[...] = val.astype(BF16)
            _put_tile(tile, val)
            _scatter_to_segments(tile, seg_refs[k * n_dil:(k + 1) * n_dil])

    tok = lambda w: pl.BlockSpec((tm, w), lambda i: (i, 0))
    kvx = pl.BlockSpec((2, tm, LANES), lambda i: (0, i, 0))
    sd = lambda *s: jax.ShapeDtypeStruct(s, BF16)
    res = pl.pallas_call(
        body, name=name, grid=(t // tm,),
        out_shape=[sd(t, d), sd(t, A_Q_W), sd(2, t, LANES), sd(2, t, LANES), sd(t, B_W), sd(t, B_W), sd(t, B_W)]
        + [_seg_shape(t, dil, B_W, BF16) for _ in range(3) for dil in DILATIONS],
        in_specs=[tok(d), pl.BlockSpec((1, d), lambda i: (0, 0)), pl.BlockSpec((in_w, d), lambda i: (0, 0)),
                  tok(LANES), tok(LANES)],
        out_specs=[tok(d), tok(A_Q_W), kvx, kvx, tok(B_W), tok(B_W), tok(B_W)]
        + [_seg_spec(tm, dil, B_W) for _ in range(3) for dil in DILATIONS],
        scratch_shapes=_tile_scratch(tm, B_W),
        compiler_params=_cp(("arbitrary",)),
    )(x, gn, win_t, cos, sin_signed)
    return res[:7], [res[7 + k * n_dil:7 + (k + 1) * n_dil] for k in range(3)]


def _in_proj_bwd(dres, x, gn, win_t, h, cos, sin_signed, daq, dakx, davx, dbq, dbk, dbv, name, tm):
    t, d = x.shape
    in_w = win_t.shape[0]
    n_t = t // tm
    chunk = in_w // 3

    def body(dres_ref, x_ref, gn_ref, w_ref, h_ref, cos_ref, sin_ref, daq_ref, dakx_ref, davx_ref, dbq_ref, dbk_ref,
             dbv_ref, dx_ref, dgn_ref, gw_ref, half_ref, dp_s, gw_s):
        i = pl.program_id(0)
        cs, sn = cos_ref[...], sin_ref[...]
        low = lax.broadcasted_iota(jnp.int32, (1, LANES), 1) < HEAD_DIM

        def fold(ref):
            a, b = ref[0].astype(F32), ref[1].astype(F32)
            return jnp.where(low, a + pltpu.roll(a, HEAD_DIM, 1), b + pltpu.roll(b, HEAD_DIM, 1))

        o = 0
        dp_s[:, o:o + A_Q_W] = _rope_bwd(daq_ref[...].astype(F32) * QK_SCALE, cs, sn).astype(BF16)
        o += A_Q_W
        dp_s[:, o:o + A_KV_W] = _rope_bwd(fold(dakx_ref), cs, sn).astype(BF16)
        o += A_KV_W
        dp_s[:, o:o + A_KV_W] = fold(davx_ref).astype(BF16)
        o += A_KV_W
        dp_s[:, o:o + B_W] = _rope_bwd(dbq_ref[...].astype(F32) * QK_SCALE, cs, sn).astype(BF16)
        o += B_W
        dp_s[:, o:o + B_W] = _rope_bwd(dbk_ref[...].astype(F32), cs, sn).astype(BF16)
        o += B_W
        dp_s[:, o:o + B_W] = dbv_ref[...].astype(BF16)
        dh = _dot_nn(dp_s[...], w_ref[...])
        hv = h_ref[...]
        for c0 in range(0, in_w, chunk):
            _accumulate(gw_s.at[pl.ds(c0, chunk), :], _dot_tn(dp_s[:, c0:c0 + chunk], hv), i == 0)
        xv = x_ref[...]
        dxn, dgn = _norm_bwd(dh, xv, _rstd(xv), gn_ref[...])
        dx = dres_ref[...] + dxn
        dx_ref[...] = dx
        half_ref[...] = (FFN_RES_WEIGHT * dx).astype(BF16)
        _accumulate(dgn_ref, dgn, i == 0)

        @pl.when(i == n_t - 1)
        def _():
            gw_ref[...] = gw_s[...].astype(BF16)

    tok = lambda w: pl.BlockSpec((tm, w), lambda i: (i, 0))
    row = pl.BlockSpec((1, d), lambda i: (0, 0))
    whole = pl.BlockSpec((in_w, d), lambda i: (0, 0))
    kvx = pl.BlockSpec((2, tm, LANES), lambda i: (0, i, 0))
    return pl.pallas_call(
        body, name=name, grid=(n_t,),
        out_shape=[jax.ShapeDtypeStruct((t, d), F32), jax.ShapeDtypeStruct((1, d), F32),
                   jax.ShapeDtypeStruct((in_w, d), BF16), jax.ShapeDtypeStruct((t, d), BF16)],
        in_specs=[tok(d), tok(d), row, whole, tok(d), tok(LANES), tok(LANES), tok(A_Q_W), kvx, kvx,
                  tok(B_W), tok(B_W), tok(B_W)],
        out_specs=[tok(d), row, whole, tok(d)],
        scratch_shapes=[pltpu.VMEM((tm, in_w), BF16), pltpu.VMEM((in_w, d), F32)],
        compiler_params=_cp(("arbitrary",)),
    )(dres, x, gn, win_t, h, cos, sin_signed, daq, dakx, davx, dbq, dbk, dbv)


def _merge_out_proj_fwd(x, a_out, outs, lses, wout, name, tm):
    t, d = x.shape
    n_dil = len(DILATIONS)

    def body(x_ref, a_ref, *rest):
        o_refs, l_refs, (w_ref, y_ref, b_ref, lt_ref), b_segs, lt_segs, scratch = _split(rest, 1 + n_dil, 1 + n_dil, 4, n_dil, n_dil)
        n_c = B_W // LANES
        tiles = [scratch[j * n_c:(j + 1) * n_c] for j in range(2 * n_dil)]
        os_, ls = [o_refs[0][...].astype(F32)], [l_refs[0][...]]
        for k, dil in enumerate(DILATIONS):
            _gather_from_segments(o_refs[1 + k], dil, tiles[2 * k])
            _gather_from_segments(l_refs[1 + k], dil, tiles[2 * k + 1])
            os_.append(_get_tile(tiles[2 * k]))
            ls.append(_get_tile(tiles[2 * k + 1]))
        mx = functools.reduce(jnp.maximum, ls)
        es = [jnp.exp(l - mx) for l in ls]
        den = functools.reduce(jnp.add, es)
        b = functools.reduce(jnp.add, [e * o for e, o in zip(es, os_)]) / den
        lt = mx + jnp.log(den)
        bb = b.astype(BF16)
        b_ref[...] = bb
        lt_ref[...] = lt
        y_ref[...] = x_ref[...] + _dot_nn(a_ref[...], w_ref[0:A_Q_W, :]) + _dot_nn(bb, w_ref[A_Q_W:A_Q_W + B_W, :])
        _put_tile(tiles[0], b)
        _scatter_to_segments(tiles[0], b_segs)
        _put_tile(tiles[1], lt)
        _scatter_to_segments(tiles[1], lt_segs)

    tok = lambda w: pl.BlockSpec((tm, w), lambda i: (i, 0))
    segs = [_seg_spec(tm, dil, B_W) for dil in DILATIONS]
    res = pl.pallas_call(
        body, name=name, grid=(t // tm,),
        out_shape=[jax.ShapeDtypeStruct((t, d), F32), jax.ShapeDtypeStruct((t, B_W), BF16), jax.ShapeDtypeStruct((t, B_W), F32)]
        + [_seg_shape(t, dil, B_W, BF16) for dil in DILATIONS] + [_seg_shape(t, dil, B_W, F32) for dil in DILATIONS],
        in_specs=[tok(d), tok(A_Q_W)] + ([tok(B_W)] + segs) * 2 + [pl.BlockSpec(wout.shape, lambda i: (0, 0))],
        out_specs=[tok(d), tok(B_W), tok(B_W)] + segs * 2,
        scratch_shapes=_tile_scratch(tm, B_W) * (2 * n_dil),
        compiler_params=_cp(("arbitrary",)),
    )(x, a_out, *outs, *lses, wout)
    return res[0], [res[1]] + list(res[3:3 + n_dil]), [res[2]] + list(res[3 + n_dil:])


def _out_proj_bwd(dy, a_out, b_out, wout, name, tm):
    t, d = dy.shape
    n_t = t // tm
    n_dil = len(DILATIONS)

    def body(dy_ref, a_ref, b_ref, w_ref, da_ref, db_ref, gw_ref, *rest):
        db_segs, gw_s, tile = rest[:n_dil], rest[n_dil], rest[n_dil + 1:]
        i = pl.program_id(0)
        dyb = dy_ref[...].astype(BF16)
        da_ref[...] = _dot_nt(dyb, w_ref[0:A_Q_W, :]).astype(BF16)
        db = _dot_nt(dyb, w_ref[A_Q_W:A_Q_W + B_W, :])
        db_ref[...] = db.astype(BF16)
        _put_tile(tile, db)
        _scatter_to_segments(tile, db_segs)
        ga = _dot_tn(a_ref[...], dyb)
        gb = _dot_tn(b_ref[...], dyb)

        @pl.when(i == 0)
        def _():
            gw_s[0:A_Q_W, :] = ga
            gw_s[A_Q_W:A_Q_W + B_W, :] = gb

        @pl.when(i > 0)
        def _():
            gw_s[0:A_Q_W, :] += ga
            gw_s[A_Q_W:A_Q_W + B_W, :] += gb

        @pl.when(i == n_t - 1)
        def _():
            gw_ref[...] = gw_s[...].astype(BF16)

    tok = lambda w: pl.BlockSpec((tm, w), lambda i: (i, 0))
    whole = pl.BlockSpec(wout.shape, lambda i: (0, 0))
    res = pl.pallas_call(
        body, name=name, grid=(n_t,),
        out_shape=[jax.ShapeDtypeStruct((t, A_Q_W), BF16), jax.ShapeDtypeStruct((t, B_W), BF16),
                   jax.ShapeDtypeStruct(wout.shape, BF16)] + [_seg_shape(t, dil, B_W, BF16) for dil in DILATIONS],
        in_specs=[tok(d), tok(A_Q_W), tok(B_W), whole],
        out_specs=[tok(A_Q_W), tok(B_W), whole] + [_seg_spec(tm, dil, B_W) for dil in DILATIONS],
        scratch_shapes=[pltpu.VMEM(wout.shape, F32)] + _tile_scratch(tm, B_W),
        compiler_params=_cp(("arbitrary",)),
    )(dy, a_out, b_out, wout)
    return res[0], [res[1]] + list(res[3:]), res[2]


def _sum_pattern_grads(per_pattern, name, tm):
    t = per_pattern[0][0].shape[0]
    n_dil = len(DILATIONS)

    def body(*refs):
        ins, outs, tile = _split(refs, 3 * (1 + n_dil), 3)
        for j, o_ref in enumerate(outs):
            _put_tile(tile, ins[j][...].astype(F32))
            for k, dil in enumerate(DILATIONS):
                _gather_from_segments(ins[3 * (1 + k) + j], dil, tile, add=True)
            o_ref[...] = _get_tile(tile).astype(BF16)

    tok = pl.BlockSpec((tm, B_W), lambda i: (i, 0))
    flat = [a if k == 0 else a.reshape(DILATIONS[k - 1], -1, B_W) for k, grads in enumerate(per_pattern) for a in grads]
    return pl.pallas_call(
        body, name=name, grid=(t // tm,),
        out_shape=[jax.ShapeDtypeStruct((t, B_W), BF16)] * 3,
        in_specs=[tok] * 3 + [_seg_spec(tm, dil, B_W) for dil in DILATIONS for _ in range(3)],
        out_specs=[tok] * 3, scratch_shapes=_tile_scratch(tm, B_W),
        compiler_params=_cp(("arbitrary",)),
    )(*flat)


SUB_ROWS = 64


def _sub_band(r0, hw, win, seg_lo, seg_len, t, rel, col):
    ks = pl.multiple_of(jnp.clip(r0 - hw, 0, t - win), SUB_ROWS)
    kpos = col + ks
    valid = (jnp.abs(rel + (ks - r0)) <= hw) & (kpos >= seg_lo) & (kpos < seg_lo + seg_len)
    return ks, valid


def _split_heads(v, low):
    zero = jnp.zeros_like(v)
    return jnp.concatenate([jnp.where(low, v, zero), jnp.where(low, zero, v)], axis=0)


def _kv_spec(kv, t):
    if kv.ndim == 3:
        return pl.BlockSpec((None, t, LANES), lambda p, i: (p // 2, 0, 0))
    return pl.BlockSpec((t, LANES), lambda p, i: (0, p))


def _attn_fwd(q, k, v, sink, name, hw, seg_len, tq, has_sink):
    t, width = q.shape
    sb = SUB_ROWS
    win = 2 * hw + LANES

    def body(sink_ref, q_ref, k_ref, v_ref, o_ref, lse_ref):
        p, i = pl.program_id(0), pl.program_id(1)
        q0 = i * tq
        seg_lo = (q0 // seg_len) * seg_len
        low = lax.broadcasted_iota(jnp.int32, (1, LANES), 1) < HEAD_DIM
        rel = lax.broadcasted_iota(jnp.int32, (sb, win), 1) - lax.broadcasted_iota(jnp.int32, (sb, win), 0)
        col = lax.broadcasted_iota(jnp.int32, (1, win), 1)
        subs = []
        for j in range(tq // sb):
            rows = pl.ds(j * sb, sb)
            ks, valid = _sub_band(q0 + j * sb, hw, win, seg_lo, seg_len, t, rel, col)
            subs.append((rows, ks, valid, _dot_nt(_split_heads(q_ref[rows, :], low), k_ref[pl.ds(ks, win), :])))
        for rows, ks, valid, s in subs:
            vw = v_ref[pl.ds(ks, win), :]
            es, inv, lses = [], [], []
            for a in range(2):
                sa = jnp.where(valid, s[a * sb:(a + 1) * sb], NEG)
                m = jnp.max(sa, axis=1, keepdims=True)
                if has_sink:
                    sk = sink_ref[2 * p + a]
                    m = jnp.maximum(m, sk)
                e = jnp.exp(sa - m)
                den = jnp.sum(e, axis=1, keepdims=True)
                if has_sink:
                    den = den + jnp.exp(sk - m)
                es.append(e.astype(BF16))
                inv.append(1.0 / den)
                lses.append(m + jnp.log(den))
            pv = _dot_nn(jnp.concatenate(es, axis=0), vw)
            o_ref[rows, :] = jnp.where(low, pv[0:sb] * inv[0], pv[sb:2 * sb] * inv[1]).astype(BF16)
            lse_ref[rows, :] = jnp.where(low, lses[0], lses[1])

    tile = pl.BlockSpec((tq, LANES), lambda p, i: (i, p))
    return pl.pallas_call(
        body, name=name, grid=(width // LANES, t // tq),
        out_shape=[jax.ShapeDtypeStruct((t, width), BF16), jax.ShapeDtypeStruct((t, width), F32)],
        in_specs=[pl.BlockSpec(memory_space=pltpu.SMEM), tile, _kv_spec(k, t), _kv_spec(v, t)],
        out_specs=[tile, tile],
        compiler_params=_cp(("arbitrary", "arbitrary")),
    )(sink, q, k, v)


def _attn_bwd(q, k, v, o, do, lse, sink, name, hw, seg_len, tq, has_sink, comm=None, comm_at=None):
    t, width = q.shape
    sb = SUB_ROWS
    win = 2 * hw + LANES
    n_q = t // tq
    shared_kv = k.ndim == 3
    n_ci, n_co, c_in_specs, c_out_specs, c_shapes, c_scratch, c_inputs = _comm_parts(comm)

    def body(*refs):
        ((sink_ref, q_ref, k_ref, v_ref, o_ref, do_ref, lse_ref), c_in, (dq_ref, dk_ref, dv_ref, ds_ref), c_out,
         (dk_s, dv_s), c_scr) = _split(refs, 7, n_ci, 4, n_co, 2)
        p, i = pl.program_id(0), pl.program_id(1)
        if comm:
            _run_hosted(comm, comm_at, p * n_q + i, c_in, c_out, c_scr)
        fresh = (i == 0) & (p % 2 == 0) if shared_kv else i == 0
        last = (i == n_q - 1) & (p % 2 == 1) if shared_kv else i == n_q - 1

        @pl.when(fresh)
        def _():
            dk_s[...] = jnp.zeros_like(dk_s)
            dv_s[...] = jnp.zeros_like(dv_s)

        q0 = i * tq
        seg_lo = (q0 // seg_len) * seg_len
        low = lax.broadcasted_iota(jnp.int32, (1, LANES), 1) < HEAD_DIM
        rel = lax.broadcasted_iota(jnp.int32, (sb, win), 1) - lax.broadcasted_iota(jnp.int32, (sb, win), 0)
        col = lax.broadcasted_iota(jnp.int32, (1, win), 1)
        dsink = [jnp.zeros((1, 1), F32), jnp.zeros((1, 1), F32)]
        subs = []
        for j in range(tq // sb):
            rows = pl.ds(j * sb, sb)
            ks, valid = _sub_band(q0 + j * sb, hw, win, seg_lo, seg_len, t, rel, col)
            kw = k_ref[pl.ds(ks, win), :]
            dov = do_ref[rows, :]
            q2 = _split_heads(q_ref[rows, :], low)
            do2 = _split_heads(dov, low)
            subs.append((rows, ks, valid, kw, dov, q2, do2, _dot_nt(q2, kw), _dot_nt(do2, v_ref[pl.ds(ks, win), :])))
        probs = []
        for rows, ks, valid, kw, dov, q2, do2, s, dpr in subs:
            prod = dov.astype(F32) * o_ref[rows, :].astype(F32)
            lse_t = lse_ref[rows, :]
            prs, dss = [], []
            for a in range(2):
                mine = low if a == 0 else jnp.logical_not(low)
                lse_a = jnp.max(jnp.where(mine, lse_t, -jnp.inf), axis=1, keepdims=True)
                delta = jnp.sum(jnp.where(mine, prod, 0.0), axis=1, keepdims=True)
                pr = jnp.exp(jnp.where(valid, s[a * sb:(a + 1) * sb], NEG) - lse_a)
                prs.append(pr.astype(BF16))
                dss.append((pr * (dpr[a * sb:(a + 1) * sb] - delta)).astype(BF16))
                if has_sink:
                    dsink[a] = dsink[a] - jnp.sum(jnp.exp(sink_ref[2 * p + a] - lse_a) * delta, axis=0, keepdims=True)
            probs.append((jnp.concatenate(prs, axis=0), jnp.concatenate(dss, axis=0)))
        for (rows, ks, valid, kw, dov, q2, do2, s, dpr), (pr2, ds2) in zip(subs, probs):
            dv_s[pl.ds(ks, win), :] += _dot_tn(pr2, do2)
            dk_s[pl.ds(ks, win), :] += _dot_tn(ds2, q2)
            dq2 = _dot_nn(ds2, kw)
            dq_ref[rows, :] = jnp.where(low, dq2[0:sb], dq2[sb:2 * sb]).astype(BF16)
        ds_ref[...] = jnp.broadcast_to(jnp.where(low, dsink[0], dsink[1]), ds_ref.shape)

        @pl.when(last)
        def _():
            dk_ref[...] = dk_s[...].astype(BF16)
            dv_ref[...] = dv_s[...].astype(BF16)

    tile = pl.BlockSpec((tq, LANES), lambda p, i: (i, p))
    kv_shape = jax.ShapeDtypeStruct(k.shape, BF16)
    res = pl.pallas_call(
        body, name=name, grid=(width // LANES, n_q),
        out_shape=[jax.ShapeDtypeStruct((t, width), BF16), kv_shape, kv_shape,
                   jax.ShapeDtypeStruct((width // LANES, n_q, 8, LANES), F32)] + c_shapes,
        in_specs=[pl.BlockSpec(memory_space=pltpu.SMEM), tile, _kv_spec(k, t), _kv_spec(v, t), tile, tile, tile] + c_in_specs,
        out_specs=[tile, _kv_spec(k, t), _kv_spec(v, t), pl.BlockSpec((None, None, 8, LANES), lambda p, i: (p, i, 0, 0))] + c_out_specs,
        scratch_shapes=[pltpu.VMEM((t, LANES), F32)] * 2 + c_scratch,
        compiler_params=_cp(("arbitrary", "arbitrary")),
    )(sink, q, k, v, o, do, lse, *c_inputs)
    return res[:4], res[4:]


def _final_norm_loss(x, gn, target, name, tm):
    t, d = x.shape

    def body(x_ref, gn_ref, tg_ref, dx_ref, dgn_ref, sq_ref, half_ref):
        i = pl.program_id(0)
        xv, gnv = x_ref[...], gn_ref[...]
        r = _rstd(xv)
        diff = (xv * r) * gnv - tg_ref[...]
        dxn, dgn = _norm_bwd(diff * (1.0 / d), xv, r, gnv)
        dx_ref[...] = dxn
        half_ref[...] = (FFN_RES_WEIGHT * dxn).astype(BF16)
        _accumulate(dgn_ref, dgn, i == 0)
        _accumulate(sq_ref, jnp.sum(diff * diff, axis=0, keepdims=True), i == 0)

    tok = pl.BlockSpec((tm, d), lambda i: (i, 0))
    row = pl.BlockSpec((1, d), lambda i: (0, 0))
    return pl.pallas_call(
        body, name=name, grid=(t // tm,),
        out_shape=[jax.ShapeDtypeStruct((t, d), F32), jax.ShapeDtypeStruct((1, d), F32), jax.ShapeDtypeStruct((1, d), F32),
                   jax.ShapeDtypeStruct((t, d), BF16)],
        in_specs=[tok, row, tok], out_specs=[tok, row, row, tok],
        compiler_params=_cp(("arbitrary",)),
    )(x, gn, target)


def _adamw(w, g, m, v, name):
    def body(w_ref, g_ref, m_ref, v_ref, d_ref, nm_ref, nv_ref):
        gv = g_ref[...]
        nm = ADAM_B1 * m_ref[...] + (1.0 - ADAM_B1) * gv
        nv = ADAM_B2 * v_ref[...] + (1.0 - ADAM_B2) * (gv * gv)
        m_hat = nm / (1.0 - ADAM_B1 ** ADAM_STEP)
        v_hat = nv / (1.0 - ADAM_B2 ** ADAM_STEP)
        d_ref[...] = -ADAM_LR * (m_hat / (jnp.sqrt(v_hat) + ADAM_EPS) + ADAM_WD * w_ref[...])
        nm_ref[...] = nm
        nv_ref[...] = nv

    shape = jax.ShapeDtypeStruct(w.shape, F32)
    return pl.pallas_call(body, name=name, out_shape=[shape, shape, shape], compiler_params=_cp())(w, g, m, v)


def _adamw_rows(land, j, rows, w, m, v, name):
    d = w.shape[1]

    def body(l_ref, w_ref, m_ref, v_ref, g_ref, d_ref, nm_ref, nv_ref):
        gv = l_ref[0].astype(F32)
        for s in range(1, N_DEV):
            gv = gv + l_ref[s].astype(F32)
        g_ref[...] = gv
        nm = ADAM_B1 * m_ref[...] + (1.0 - ADAM_B1) * gv
        nv = ADAM_B2 * v_ref[...] + (1.0 - ADAM_B2) * (gv * gv)
        m_hat = nm / (1.0 - ADAM_B1 ** ADAM_STEP)
        v_hat = nv / (1.0 - ADAM_B2 ** ADAM_STEP)
        d_ref[...] = -ADAM_LR * (m_hat / (jnp.sqrt(v_hat) + ADAM_EPS) + ADAM_WD * w_ref[...])
        nm_ref[...] = nm
        nv_ref[...] = nv

    whole = pl.BlockSpec((rows, d), lambda i: (0, 0))
    shape = jax.ShapeDtypeStruct((rows, d), F32)
    return pl.pallas_call(
        body, name=name, grid=(1,), out_shape=[shape] * 4,
        in_specs=[pl.BlockSpec((N_DEV, rows, d), lambda i: (0, j, 0)), whole, whole, whole], out_specs=[whole] * 4,
        compiler_params=_cp(("arbitrary",)),
    )(land, w, m, v)


def _sum_small(land, name):
    def body(l_ref, o_ref):
        acc = l_ref[0]
        for s in range(1, N_DEV):
            acc = acc + l_ref[s]
        o_ref[...] = acc

    return pl.pallas_call(body, name=name, out_shape=jax.ShapeDtypeStruct(land.shape[1:], F32), compiler_params=_cp())(land)


def _rope_lanes(positions):
    inv_freq = 1.0 / (ROPE_THETA ** (jnp.arange(0, HEAD_DIM, 2, dtype=F32) / HEAD_DIM))
    ang = positions.astype(F32)[:, None] * inv_freq
    cos, sin = jnp.cos(ang), jnp.sin(ang)
    return jnp.concatenate([cos, cos, cos, cos], axis=1), jnp.concatenate([-sin, sin, -sin, sin], axis=1)


def kernel(x, positions, norm_ffn1, w_gate1, w_up1, w_down1, norm_mix, w_in, a_sink, w_out, norm_ffn2, w_gate2, w_up2, w_down2, norm_final, loss_target, m_norm_ffn1, m_w_gate1, m_w_up1, m_w_down1, m_norm_mix, m_w_in, m_a_sink, m_w_out, m_norm_ffn2, m_w_gate2, m_w_up2, m_w_down2, m_norm_final, v_norm_ffn1, v_w_gate1, v_w_up1, v_w_down1, v_norm_mix, v_w_in, v_a_sink, v_w_out, v_norm_ffn2, v_w_gate2, v_w_up2, v_w_down2, v_norm_final):
    x = x[0]
    target = loss_target[0]
    t, d = x.shape
    tm = min(1024, t)
    tm_mix = min(512, t)
    tf = 256
    tq = min(512, t // 16)

    tm_hidden = min(2048, t)
    tt = min(1024, t)
    f_all = w_gate1.shape[2] * N_DEV
    tfp = f_all // 2
    tm_fwd, tf_fwd = tm_mix, tfp
    n_steps = (t // tm_fwd) * (f_all // tf_fwd)

    def stacked(shards):
        return jnp.concatenate([s.astype(BF16) for s in shards], axis=0), [s.shape[0] for s in shards]

    packed1, rows1 = stacked([w_gate1[0].T, w_up1[0].T, w_down1[0]])
    packed2, rows2 = stacked([w_in[0].T, w_out[0], w_gate2[0].T, w_up2[0].T, w_down2[0]])
    wg1, wu1, wd1 = _run_alone(_gather_plan(packed1, rows1), "gather_ffn1_weights")

    cos, sin_signed = _rope_lanes(positions[0])
    sink = a_sink[0]
    no_sink = jnp.zeros_like(sink)

    (x1, h1, gp1, up1, hid1), (win, wout, wg2, wu2, wd2) = _ffn_fwd(
        x, norm_ffn1, wg1, wu1, wd1, "ffn1_fwd", tm_fwd, tf_fwd, _gather_plan(packed2, rows2), [0, (3 * n_steps) // 4, n_steps - 1])
    (h_mix, aq, akx, avx, bq, bk, bv), seg_qkv = _in_proj_fwd(x1, norm_mix, win, cos, sin_signed, "in_proj_fwd", tm_mix)
    a_out, a_lse = _attn_fwd(aq, akx, avx, sink, "attn_a_fwd", A_HALF_WINDOW, t, tq, True)
    rows_of = lambda a: a.reshape(t, B_W)
    segments_of = lambda a, dil: a if dil == 1 else a.reshape(dil, t // dil, B_W)
    b_qkv, b_outs, b_lses = [], [], []
    for n, (window, dil) in enumerate(B_PATTERNS):
        qs, ks, vs = (bq, bk, bv) if dil == 1 else (rows_of(seg_qkv[k][n - 1]) for k in range(3))
        o_seg, lse_seg = _attn_fwd(qs, ks, vs, no_sink, f"attn_b{dil}_fwd", window // (2 * dil), t // dil, tq, False)
        b_qkv.append((qs, ks, vs))
        b_outs.append(segments_of(o_seg, dil))
        b_lses.append(segments_of(lse_seg, dil))
    x2, b_out, b_lse = _merge_out_proj_fwd(x1, a_out, b_outs, b_lses, wout, "out_proj_fwd", tm_mix)
    (x3, h2, gp2, up2, hid2), _ = _ffn_fwd(x2, norm_ffn2, wg2, wu2, wd2, "ffn2_fwd", tm_fwd, tf_fwd)

    gfinal = norm_final.reshape(1, d)
    dx3, dg_final, sq, dout2 = _final_norm_loss(x3, gfinal, target, "final_norm_loss", tm)

    dgt2, dut2 = _ffn_bwd_hidden(dout2, gp2, up2, wd2, "ffn2_bwd_hidden", tm_hidden, tf)
    (dx2, dg_ffn2), _ = _ffn_bwd_input(dx3, x2, norm_ffn2, dgt2, dut2, wg2, wu2, "ffn2_bwd_input", tm_mix)
    (gwg2, gwu2), _ = _token_products([dgt2, dut2], h2, "ffn2_bwd_gate_up", tt, tfp)
    (gwd2,), _ = _token_products([hid2], dout2, "ffn2_bwd_down", tt, tfp)

    n_att = (A_Q_W // LANES) * (t // tq)
    n_prod = (f_all // tfp) * (t // tt)
    da, db, gwout = _out_proj_bwd(dx2, a_out, b_out[0], wout, "out_proj_bwd", tm_mix)
    (daq, dakx, davx, dsink_parts), (land_wg2,) = _attn_bwd(
        aq, akx, avx, a_out, da, a_lse, sink, "attn_a_bwd", A_HALF_WINDOW, t, tq, True, _exchange_plan([[gwg2]]), [0, n_att - 1])
    pattern_grads, pattern_lands = [], []
    for n, ((window, dil), (qs, ks, vs)) in enumerate(zip(B_PATTERNS, b_qkv)):
        carried = [gwu2, gwd2][n:n + 1]
        grads, lands = _attn_bwd(qs, ks, vs, rows_of(b_out[n]), rows_of(db[n]), rows_of(b_lse[n]), no_sink, f"attn_b{dil}_bwd",
                                 window // (2 * dil), t // dil, tq, False,
                                 _exchange_plan([carried]) if carried else None, [0, (B_W // LANES) * (t // tq) - 1])
        pattern_grads.append(grads[:3])
        pattern_lands.extend(lands)
    land_wu2, land_wd2 = pattern_lands
    dbq, dbk, dbv = _sum_pattern_grads(pattern_grads, "sum_pattern_grads", tm_mix)
    dx1, dg_mix, gwin, dout1 = _in_proj_bwd(dx2, x1, norm_mix, win, h_mix, cos, sin_signed, daq, dakx, davx, dbq, dbk, dbv, "in_proj_bwd", tm_mix)

    (gwd1,), (land_in, land_out) = _token_products(
        [hid1], dout1, "ffn1_bwd_down", tt, tfp, _exchange_plan([[gwin], [gwout]]), [0, n_prod - 1])
    dgt1, dut1 = _ffn_bwd_hidden(dout1, gp1, up1, wd1, "ffn1_bwd_hidden", tm_hidden, tf)
    (gwg1, gwu1), (land_wd1,) = _token_products(
        [dgt1, dut1], h1, "ffn1_bwd_gate_up", tt, tfp, _exchange_plan([[gwd1]]), [0, n_prod - 1])
    (grad_x, dg_ffn1), (land_wgu1,) = _ffn_bwd_input(
        dx1, x, norm_ffn1, dgt1, dut1, wg1, wu1, "ffn1_bwd_input", tm_mix, _exchange_plan([[gwg1, gwu1]]), [0, t // tm_mix - 1])

    dsink_pairs = jnp.sum(dsink_parts[:, :, 0, :], axis=1)
    dsink = jnp.stack([dsink_pairs[:, 0], dsink_pairs[:, HEAD_DIM]], axis=1).reshape(1, -1)
    small = jnp.concatenate([dg_ffn1, dg_mix, dg_ffn2, dg_final, jnp.pad(dsink, ((0, 0), (0, d - dsink.shape[1]))),
                             sq, jnp.zeros((2, d), F32)], axis=0)
    (land_small,) = _run_alone(_exchange_plan([[jnp.tile(small, (N_DEV, 1))]]), "gather_small_gradients")
    red_small = _sum_small(land_small, "sum_small_grads")
    loss = 0.5 * jnp.sum(red_small[5]) / d

    rf = rows1[0]
    sharded = {"w_gate1": (land_wgu1, 0, rf, True), "w_up1": (land_wgu1, 1, rf, True), "w_down1": (land_wd1, 0, rf, False),
               "w_in": (land_in, 0, rows2[0], True), "w_out": (land_out, 0, rows2[1], False),
               "w_gate2": (land_wg2, 0, rf, True), "w_up2": (land_wu2, 0, rf, True), "w_down2": (land_wd2, 0, rf, False)}
    n_sink = a_sink.shape[1]
    small_grads = {"norm_ffn1": red_small[0:1], "norm_mix": red_small[1:2], "norm_ffn2": red_small[2:3], "norm_final": red_small[3],
                   "a_sink": red_small[4:5, :n_sink]}
    params = {
        "norm_ffn1": (norm_ffn1, m_norm_ffn1, v_norm_ffn1), "w_gate1": (w_gate1, m_w_gate1, v_w_gate1),
        "w_up1": (w_up1, m_w_up1, v_w_up1), "w_down1": (w_down1, m_w_down1, v_w_down1),
        "norm_mix": (norm_mix, m_norm_mix, v_norm_mix), "w_in": (w_in, m_w_in, v_w_in),
        "a_sink": (a_sink, m_a_sink, v_a_sink), "w_out": (w_out, m_w_out, v_w_out),
        "norm_ffn2": (norm_ffn2, m_norm_ffn2, v_norm_ffn2), "w_gate2": (w_gate2, m_w_gate2, v_w_gate2),
        "w_up2": (w_up2, m_w_up2, v_w_up2), "w_down2": (w_down2, m_w_down2, v_w_down2),
        "norm_final": (norm_final, m_norm_final, v_norm_final),
    }
    grad_list, deltas, new_ms, new_vs = [], [], [], []
    for name, (w, m, v) in params.items():
        if name in sharded:
            land, j, rows, is_transposed = sharded[name]
            view = (lambda a: a[0].T) if is_transposed else (lambda a: a[0])
            back = (lambda a: a.T[None]) if is_transposed else (lambda a: a[None])
            outs = [back(o) for o in _adamw_rows(land, j, rows, view(w), view(m), view(v), f"adamw_{name}")]
        else:
            as_block = (lambda a: a.reshape(1, -1)) if w.ndim == 1 else (lambda a: a)
            g = small_grads[name]
            outs = [g] + [o.reshape(w.shape) for o in _adamw(as_block(w), as_block(g), as_block(m), as_block(v), f"adamw_{name}")]
        for lst, o in zip((grad_list, deltas, new_ms, new_vs), outs):
            lst.append(o)
    return (loss, grad_x[None], *grad_list, *deltas, *new_ms, *new_vs)
```

```python
import functools
import itertools

import numpy as np
import jax
import jax.numpy as jnp
from jax import lax
from jax.experimental import pallas as pl
from jax.experimental.pallas import tpu as pltpu

F32 = jnp.float32
BF16 = jnp.bfloat16

N_DEV = 8
HEAD_DIM = 64
LANES = 128
A_Q_W, A_KV_W, B_W = 512, 128, 512
A_HALF_WINDOW = 128
B_PATTERNS = ((128, 1), (512, 4), (2048, 16))
ROPE_THETA = 10000.0
NORM_EPS = 1e-6
FFN_RES_WEIGHT = 0.5
QK_SCALE = HEAD_DIM ** -0.5
NEG = -1e30

ADAM_LR = 0.001
ADAM_B1 = 0.9
ADAM_B2 = 0.999
ADAM_EPS = 1e-08
ADAM_WD = 0.01
ADAM_STEP = 10

MESH_T = pl.DeviceIdType.MESH
VMEM_LIMIT = 56 * 1024 * 1024


def _cp(sem=None, vmem=VMEM_LIMIT):
    return pltpu.CompilerParams(dimension_semantics=sem, vmem_limit_bytes=vmem)


def _dot_nn(a, b):
    return jnp.dot(a, b, preferred_element_type=F32)


def _dot_nt(a, b):
    return lax.dot_general(a, b, (((1,), (1,)), ((), ())), preferred_element_type=F32)


def _dot_tn(a, b):
    return lax.dot_general(a, b, (((0,), (0,)), ((), ())), preferred_element_type=F32)


def _rstd(xv):
    return lax.rsqrt(jnp.mean(xv * xv, axis=-1, keepdims=True) + NORM_EPS)


def _norm_bwd(dh, xv, r, gn):
    gy = dh * gn
    c = jnp.sum(gy * xv, axis=-1, keepdims=True) * (1.0 / xv.shape[-1])
    dx = r * gy - xv * (r * r * r * c)
    dgn = jnp.sum(dh * (xv * r), axis=0, keepdims=True)
    return dx, dgn


def _accumulate(ref, val, first):
    @pl.when(first)
    def _():
        ref[...] = val

    @pl.when(jnp.logical_not(first))
    def _():
        ref[...] += val


def _mesh_pos():
    return lax.axis_index("x"), lax.axis_index("y"), lax.axis_index("c")


def _dev_index(d):
    return 4 * d[0] + 2 * d[1] + d[2]


class _Comm:
    def __init__(self, inputs, out_shape, scratch, phases):
        self.inputs, self.out_shape, self.scratch, self.phases = inputs, out_shape, scratch, phases

    def specs(self):
        any_spec = pl.BlockSpec(memory_space=pl.ANY)
        return [any_spec] * len(self.inputs), [any_spec] * len(self.out_shape)


def _run_alone(comm, name):
    n_in, n_out = len(comm.inputs), len(comm.out_shape)

    def body(*refs):
        for phase in comm.phases:
            phase(refs[:n_in], refs[n_in:n_in + n_out], refs[n_in + n_out:])

    in_specs, out_specs = comm.specs()
    return pl.pallas_call(body, name=name, out_shape=comm.out_shape, in_specs=in_specs, out_specs=out_specs,
                          scratch_shapes=comm.scratch)(*comm.inputs)


def _run_hosted(comm, at, step, ins, outs, scr):
    for phase, when in zip(comm.phases, at):
        @pl.when(step == when)
        def _(phase=phase):
            phase(ins, outs, scr)


def _split(refs, *counts):
    parts, o = [], 0
    for n in counts:
        parts.append(refs[o:o + n])
        o += n
    return parts + [refs[o:]]


def _gather_plan(packed, rows_list):
    n_w = len(rows_list)
    offs = [int(o) for o in np.cumsum([0] + list(rows_list[:-1]))]
    d = packed.shape[1]

    def tools(ins, outs, scr):
        p_ref = ins[0]
        send_sems, recv_sems, local_sem = scr
        x, y, c = _mesh_pos()
        me, sibling = (x, y, c), (x, y, 1 - c)
        chips = [(1 - x, y), (x, 1 - y), (1 - x, 1 - y)]

        def rows(w, dev):
            start = pl.multiple_of(_dev_index(dev) * rows_list[w], 16)
            return outs[w].at[pl.ds(start, rows_list[w]), :]

        def mine(w):
            return p_ref.at[pl.ds(offs[w], rows_list[w]), :]

        def copy(k, w, block, to, own):
            return pltpu.make_async_remote_copy(
                src_ref=mine(w) if own else rows(w, block), dst_ref=rows(w, block),
                send_sem=send_sems.at[k], recv_sem=recv_sems.at[k], device_id=to, device_id_type=MESH_T)

        def all_blocks(k):
            return pltpu.make_async_remote_copy(
                src_ref=p_ref, dst_ref=p_ref, send_sem=send_sems.at[k], recv_sem=recv_sems.at[k],
                device_id=me, device_id_type=MESH_T)

        return p_ref, local_sem, me, sibling, chips, c, rows, mine, copy, all_blocks

    def start(ins, outs, scr):
        _, local_sem, me, sibling, chips, c, rows, mine, copy, _ = tools(ins, outs, scr)
        for w in range(n_w):
            pltpu.make_async_copy(mine(w), rows(w, me), local_sem).start()
        for w in range(n_w):
            copy(0, w, me, sibling, True).start()
        for j, chip in enumerate(chips):
            for w in range(n_w):
                copy(1 + j, w, me, (*chip, c), True).start()

    def relay(ins, outs, scr):
        _, _, _, sibling, chips, c, _, _, copy, all_blocks = tools(ins, outs, scr)
        for j, chip in enumerate(chips):
            all_blocks(1 + j).wait_recv()
            for w in range(n_w):
                copy(4 + j, w, (*chip, c), sibling, False).start()

    def finish(ins, outs, scr):
        p_ref, local_sem, _, _, _, _, _, _, _, all_blocks = tools(ins, outs, scr)
        all_blocks(0).wait_recv()
        for j in range(3):
            all_blocks(4 + j).wait_recv()
        for k in range(7):
            all_blocks(k).wait_send()
        pltpu.make_async_copy(p_ref, p_ref, local_sem).wait()

    return _Comm(
        [packed], [jax.ShapeDtypeStruct((N_DEV * r, d), packed.dtype) for r in rows_list],
        [pltpu.SemaphoreType.DMA((7,)), pltpu.SemaphoreType.DMA((7,)), pltpu.SemaphoreType.DMA], [start, relay, finish])


def _exchange_plan(groups):
    flat = [a for g in groups for a in g]
    n_g = len(groups)
    sizes = [len(g) for g in groups]
    rows = [g[0].shape[0] // N_DEV for g in groups]
    first = [int(o) for o in np.cumsum([0] + sizes[:-1])]

    def start(srcs, lands, scr):
        send_sems, recv_sems, local_sems = scr
        x, y, c = _mesh_pos()
        me = (x, y, c)
        me_idx = _dev_index(me)

        def block(g, i, dev):
            start_row = pl.multiple_of(_dev_index(dev) * rows[g], 8)
            return srcs[first[g] + i].at[pl.ds(start_row, rows[g]), :]

        def slot(g, i):
            return lands[g].at[me_idx, pl.ds(i * rows[g], rows[g]), :]

        for g in range(n_g):
            for i in range(sizes[g]):
                pltpu.make_async_copy(block(g, i, me), slot(g, i), local_sems.at[g]).start()
        flips = [f for f in itertools.product((0, 1), repeat=3) if any(f)]
        for k, (fx, fy, fc) in enumerate(flips):
            peer = (1 - x if fx else x, 1 - y if fy else y, 1 - c if fc else c)
            for g in range(n_g):
                for i in range(sizes[g]):
                    pltpu.make_async_remote_copy(
                        src_ref=block(g, i, peer), dst_ref=slot(g, i), send_sem=send_sems.at[g, k],
                        recv_sem=recv_sems.at[g, k], device_id=peer, device_id_type=MESH_T).start()

    def finish(srcs, lands, scr):
        send_sems, recv_sems, local_sems = scr
        me = _mesh_pos()
        for k in range(7):
            for g in range(n_g):
                pltpu.make_async_remote_copy(
                    src_ref=lands[g].at[0], dst_ref=lands[g].at[0], send_sem=send_sems.at[g, k],
                    recv_sem=recv_sems.at[g, k], device_id=me, device_id_type=MESH_T).wait()
        for g in range(n_g):
            pltpu.make_async_copy(lands[g].at[0], lands[g].at[0], local_sems.at[g]).wait()

    return _Comm(
        flat, [jax.ShapeDtypeStruct((N_DEV, sizes[g] * rows[g], groups[g][0].shape[1]), groups[g][0].dtype) for g in range(n_g)],
        [pltpu.SemaphoreType.DMA((n_g, 7)), pltpu.SemaphoreType.DMA((n_g, 7)), pltpu.SemaphoreType.DMA((n_g,))],
        [start, finish])


def _comm_parts(comm):
    if comm is None:
        return 0, 0, [], [], [], [], []
    in_specs, out_specs = comm.specs()
    return len(comm.inputs), len(comm.out_shape), in_specs, out_specs, comm.out_shape, comm.scratch, comm.inputs


def _ffn_fwd(x, gn, wg_t, wu_t, wd, name, tm, tf, comm=None, comm_at=None):
    t, d = x.shape
    f_all = wg_t.shape[0]
    n_f = f_all // tf
    n_ci, n_co, c_in_specs, c_out_specs, c_shapes, c_scratch, c_inputs = _comm_parts(comm)

    def body(*refs):
        ((x_ref, gn_ref, wg_ref, wu_ref, wd_ref), c_in, (y_ref, h_ref, gp_ref, up_ref, hid_ref), c_out,
         (h_s, hid_s), c_scr) = _split(refs, 5, n_ci, 5, n_co, 2)
        f = pl.program_id(1)
        if comm:
            _run_hosted(comm, comm_at, pl.program_id(0) * n_f + f, c_in, c_out, c_scr)

        @pl.when(f == 0)
        def _():
            xv = x_ref[...]
            h = ((xv * _rstd(xv)) * gn_ref[...]).astype(BF16)
            h_s[...] = h
            h_ref[...] = h

        h = h_s[...]
        g = _dot_nt(h, wg_ref[...])
        u = _dot_nt(h, wu_ref[...])
        sg = jax.nn.sigmoid(g)
        silu = g * sg
        gp_ref[...] = (u * (sg * (1.0 + g * (1.0 - sg)))).astype(BF16)
        up_ref[...] = silu.astype(BF16)
        hid = (silu * u).astype(BF16)
        hid_ref[...] = hid
        for f0 in range(n_f):
            @pl.when(f == f0)
            def _(f0=f0):
                hid_s[:, f0 * tf:(f0 + 1) * tf] = hid

        @pl.when(f == n_f - 1)
        def _():
            y_ref[...] = x_ref[...] + FFN_RES_WEIGHT * _dot_nn(hid_s[...], wd_ref[...])

    tok = pl.BlockSpec((tm, d), lambda i, f: (i, 0))
    wblk = pl.BlockSpec((tf, d), lambda i, f: (f, 0))
    whole = pl.BlockSpec((f_all, d), lambda i, f: (0, 0))
    act = pl.BlockSpec((tm, tf), lambda i, f: (i, f))
    act_shape = jax.ShapeDtypeStruct((t, f_all), BF16)
    res = pl.pallas_call(
        body, name=name, grid=(t // tm, n_f),
        out_shape=[jax.ShapeDtypeStruct((t, d), F32), jax.ShapeDtypeStruct((t, d), BF16), act_shape, act_shape, act_shape] + c_shapes,
        in_specs=[tok, pl.BlockSpec((1, d), lambda i, f: (0, 0)), wblk, wblk, whole] + c_in_specs,
        out_specs=[tok, tok, act, act, act] + c_out_specs,
        scratch_shapes=[pltpu.VMEM((tm, d), BF16), pltpu.VMEM((tm, f_all), BF16)] + c_scratch,
        compiler_params=_cp(("arbitrary", "arbitrary")),
    )(x, gn, wg_t, wu_t, wd, *c_inputs)
    return res[:5], res[5:]


def _ffn_bwd_hidden(dout, gp, up, wd, name, tm, tf):
    t, d = dout.shape
    f_all = wd.shape[0]

    def body(dout_ref, gp_ref, up_ref, wd_ref, dg_ref, du_ref):
        dhid = _dot_nt(dout_ref[...], wd_ref[...])
        dg_ref[...] = (dhid * gp_ref[...].astype(F32)).astype(BF16)
        du_ref[...] = (dhid * up_ref[...].astype(F32)).astype(BF16)

    tok = pl.BlockSpec((tm, d), lambda i, f: (i, 0))
    wblk = pl.BlockSpec((tf, d), lambda i, f: (f, 0))
    act = pl.BlockSpec((tm, tf), lambda i, f: (i, f))
    act_shape = jax.ShapeDtypeStruct((t, f_all), BF16)
    return pl.pallas_call(
        body, name=name, grid=(t // tm, f_all // tf),
        out_shape=[act_shape, act_shape], in_specs=[tok, act, act, wblk], out_specs=[act, act],
        compiler_params=_cp(("arbitrary", "arbitrary")),
    )(dout, gp, up, wd)


def _ffn_bwd_input(dy, x, gn, dg, du, wg_t, wu_t, name, tm, comm=None, comm_at=None):
    t, d = x.shape
    f_all = wg_t.shape[0]
    n_ci, n_co, c_in_specs, c_out_specs, c_shapes, c_scratch, c_inputs = _comm_parts(comm)

    def body(*refs):
        (dy_ref, x_ref, gn_ref, dg_ref, du_ref, wg_ref, wu_ref), c_in, (dx_ref, dgn_ref), c_out, c_scr = _split(
            refs, 7, n_ci, 2, n_co)
        i = pl.program_id(0)
        if comm:
            _run_hosted(comm, comm_at, i, c_in, c_out, c_scr)
        dh = _dot_nn(dg_ref[...], wg_ref[...]) + _dot_nn(du_ref[...], wu_ref[...])
        xv = x_ref[...]
        dxn, dgn = _norm_bwd(dh, xv, _rstd(xv), gn_ref[...])
        dx_ref[...] = dy_ref[...] + dxn
        _accumulate(dgn_ref, dgn, i == 0)

    tok = pl.BlockSpec((tm, d), lambda i: (i, 0))
    row = pl.BlockSpec((1, d), lambda i: (0, 0))
    act = pl.BlockSpec((tm, f_all), lambda i: (i, 0))
    whole = pl.BlockSpec((f_all, d), lambda i: (0, 0))
    res = pl.pallas_call(
        body, name=name, grid=(t // tm,),
        out_shape=[jax.ShapeDtypeStruct((t, d), F32), jax.ShapeDtypeStruct((1, d), F32)] + c_shapes,
        in_specs=[tok, tok, row, act, act, whole, whole] + c_in_specs,
        out_specs=[tok, row] + c_out_specs,
        scratch_shapes=c_scratch,
        compiler_params=_cp(("arbitrary",)),
    )(dy, x, gn, dg, du, wg_t, wu_t, *c_inputs)
    return res[:2], res[2:]


def _token_products(lhs_list, rhs, name, tt, tf, comm=None, comm_at=None):
    n_l = len(lhs_list)
    t, f_all = lhs_list[0].shape
    d = rhs.shape[1]
    n_t = t // tt
    n_ci, n_co, c_in_specs, c_out_specs, c_shapes, c_scratch, c_inputs = _comm_parts(comm)

    def body(*refs):
        lhs_refs, (rhs_ref,), c_in, out_refs, c_out, accs, c_scr = _split(refs, n_l, 1, n_ci, n_l, n_co, n_l)
        s = pl.program_id(1)
        if comm:
            _run_hosted(comm, comm_at, pl.program_id(0) * n_t + s, c_in, c_out, c_scr)
        rv = rhs_ref[...]
        for l_ref, acc in zip(lhs_refs, accs):
            _accumulate(acc, _dot_tn(l_ref[...], rv), s == 0)

        @pl.when(s == n_t - 1)
        def _():
            for o_ref, acc in zip(out_refs, accs):
                o_ref[...] = acc[...].astype(BF16)

    act = pl.BlockSpec((tt, tf), lambda f, s: (s, f))
    tok = pl.BlockSpec((tt, d), lambda f, s: (s, 0))
    wblk = pl.BlockSpec((tf, d), lambda f, s: (f, 0))
    res = pl.pallas_call(
        body, name=name, grid=(f_all // tf, n_t),
        out_shape=[jax.ShapeDtypeStruct((f_all, d), BF16)] * n_l + c_shapes,
        in_specs=[act] * n_l + [tok] + c_in_specs, out_specs=[wblk] * n_l + c_out_specs,
        scratch_shapes=[pltpu.VMEM((tf, d), F32)] * n_l + c_scratch,
        compiler_params=_cp(("arbitrary", "arbitrary")),
    )(*lhs_list, rhs, *c_inputs)
    return res[:n_l], res[n_l:]


def _swap_halves(t):
    w = t.shape[-1]
    lane = lax.broadcasted_iota(jnp.int32, (1, w), 1)
    return jnp.where((lane % HEAD_DIM) < HEAD_DIM // 2, pltpu.roll(t, w - HEAD_DIM // 2, 1), pltpu.roll(t, HEAD_DIM // 2, 1))


def _rope(t, cos, sin_signed):
    reps = t.shape[-1] // LANES
    return t * jnp.tile(cos, (1, reps)) + _swap_halves(t) * jnp.tile(sin_signed, (1, reps))


def _rope_bwd(dt, cos, sin_signed):
    reps = dt.shape[-1] // LANES
    return dt * jnp.tile(cos, (1, reps)) + _swap_halves(dt * jnp.tile(sin_signed, (1, reps)))


DILATIONS = tuple(dil for _, dil in B_PATTERNS if dil > 1)


def _seg_shape(t, dil, w, dtype):
    return jax.ShapeDtypeStruct((dil, t // dil, w), dtype)


def _seg_spec(tm, dil, w):
    return pl.BlockSpec((dil, tm // dil, w), lambda i: (0, i, 0))


def _tile_scratch(tm, w):
    return [pltpu.VMEM((tm, LANES), F32)] * (w // LANES)


def _put_tile(tile, val):
    for c, ref in enumerate(tile):
        ref[...] = val[:, c * LANES:(c + 1) * LANES]


def _get_tile(tile):
    return jnp.concatenate([ref[...] for ref in tile], axis=1)


def _scatter_to_segments(tile, seg_refs):
    for seg_ref, dil in zip(seg_refs, DILATIONS):
        rows = tile[0].shape[0] // dil
        for r in range(dil):
            for c, ref in enumerate(tile):
                seg_ref[r, :, c * LANES:(c + 1) * LANES] = ref[pl.ds(r, rows, stride=dil), :].astype(seg_ref.dtype)


def _gather_from_segments(seg_ref, dil, tile, add=False):
    rows = tile[0].shape[0] // dil
    for r in range(dil):
        for c, ref in enumerate(tile):
            idx = (pl.ds(r, rows, stride=dil), slice(None))
            v = seg_ref[r, :, c * LANES:(c + 1) * LANES].astype(F32)
            ref[idx] = ref[idx] + v if add else v


def _in_proj_fwd(x, gn, win_t, cos, sin_signed, name, tm):
    t, d = x.shape
    in_w = win_t.shape[0]
    n_dil = len(DILATIONS)

    def body(x_ref, gn_ref, w_ref, cos_ref, sin_ref, h_ref, aq_ref, akx_ref, avx_ref, bq_ref, bk_ref, bv_ref, *rest):
        seg_refs, tile = rest[:3 * n_dil], rest[3 * n_dil:]
        xv = x_ref[...]
        h = ((xv * _rstd(xv)) * gn_ref[...]).astype(BF16)
        h_ref[...] = h
        p = _dot_nt(h, w_ref[...])
        cs, sn = cos_ref[...], sin_ref[...]
        o = 0
        aq_ref[...] = (_rope(p[:, o:o + A_Q_W], cs, sn) * QK_SCALE).astype(BF16)
        o += A_Q_W
        ak = _rope(p[:, o:o + A_KV_W], cs, sn)
        o += A_KV_W
        av = p[:, o:o + A_KV_W]
        o += A_KV_W
        low = lax.broadcasted_iota(jnp.int32, (1, LANES), 1) < HEAD_DIM
        for src, dst in ((ak, akx_ref), (av, avx_ref)):
            other = pltpu.roll(src, HEAD_DIM, 1)
            dst[0] = jnp.where(low, src, other).astype(BF16)
            dst[1] = jnp.where(low, other, src).astype(BF16)
        for k, nat_ref in enumerate((bq_ref, bk_ref, bv_ref)):
            val = p[:, o:o + B_W]
            o += B_W
            if k < 2:
                val = _rope(val, cs, sn)
            if k == 0:
                val = val * QK_SCALE
            nat_ref[...] = val.astype(BF16)
            _put_tile(tile, val)
            _scatter_to_segments(tile, seg_refs[k * n_dil:(k + 1) * n_dil])

    tok = lambda w: pl.BlockSpec((tm, w), lambda i: (i, 0))
    kvx = pl.BlockSpec((2, tm, LANES), lambda i: (0, i, 0))
    sd = lambda *s: jax.ShapeDtypeStruct(s, BF16)
    res = pl.pallas_call(
        body, name=name, grid=(t // tm,),
        out_shape=[sd(t, d), sd(t, A_Q_W), sd(2, t, LANES), sd(2, t, LANES), sd(t, B_W), sd(t, B_W), sd(t, B_W)]
        + [_seg_shape(t, dil, B_W, BF16) for _ in range(3) for dil in DILATIONS],
        in_specs=[tok(d), pl.BlockSpec((1, d), lambda i: (0, 0)), pl.BlockSpec((in_w, d), lambda i: (0, 0)),
                  tok(LANES), tok(LANES)],
        out_specs=[tok(d), tok(A_Q_W), kvx, kvx, tok(B_W), tok(B_W), tok(B_W)]
        + [_seg_spec(tm, dil, B_W) for _ in range(3) for dil in DILATIONS],
        scratch_shapes=_tile_scratch(tm, B_W),
        compiler_params=_cp(("arbitrary",)),
    )(x, gn, win_t, cos, sin_signed)
    return res[:7], [res[7 + k * n_dil:7 + (k + 1) * n_dil] for k in range(3)]


def _in_proj_bwd(dres, x, gn, win_t, h, cos, sin_signed, daq, dakx, davx, dbq, dbk, dbv, name, tm):
    t, d = x.shape
    in_w = win_t.shape[0]
    n_t = t // tm
    chunk = in_w // 3

    def body(dres_ref, x_ref, gn_ref, w_ref, h_ref, cos_ref, sin_ref, daq_ref, dakx_ref, davx_ref, dbq_ref, dbk_ref,
             dbv_ref, dx_ref, dgn_ref, gw_ref, half_ref, dp_s, gw_s):
        i = pl.program_id(0)
        cs, sn = cos_ref[...], sin_ref[...]
        low = lax.broadcasted_iota(jnp.int32, (1, LANES), 1) < HEAD_DIM

        def fold(ref):
            a, b = ref[0].astype(F32), ref[1].astype(F32)
            return jnp.where(low, a + pltpu.roll(a, HEAD_DIM, 1), b + pltpu.roll(b, HEAD_DIM, 1))

        o = 0
        dp_s[:, o:o + A_Q_W] = _rope_bwd(daq_ref[...].astype(F32) * QK_SCALE, cs, sn).astype(BF16)
        o += A_Q_W
        dp_s[:, o:o + A_KV_W] = _rope_bwd(fold(dakx_ref), cs, sn).astype(BF16)
        o += A_KV_W
        dp_s[:, o:o + A_KV_W] = fold(davx_ref).astype(BF16)
        o += A_KV_W
        dp_s[:, o:o + B_W] = _rope_bwd(dbq_ref[...].astype(F32) * QK_SCALE, cs, sn).astype(BF16)
        o += B_W
        dp_s[:, o:o + B_W] = _rope_bwd(dbk_ref[...].astype(F32), cs, sn).astype(BF16)
        o += B_W
        dp_s[:, o:o + B_W] = dbv_ref[...].astype(BF16)
        dh = _dot_nn(dp_s[...], w_ref[...])
        hv = h_ref[...]
        for c0 in range(0, in_w, chunk):
            _accumulate(gw_s.at[pl.ds(c0, chunk), :], _dot_tn(dp_s[:, c0:c0 + chunk], hv), i == 0)
        xv = x_ref[...]
        dxn, dgn = _norm_bwd(dh, xv, _rstd(xv), gn_ref[...])
        dx = dres_ref[...] + dxn
        dx_ref[...] = dx
        half_ref[...] = (FFN_RES_WEIGHT * dx).astype(BF16)
        _accumulate(dgn_ref, dgn, i == 0)

        @pl.when(i == n_t - 1)
        def _():
            gw_ref[...] = gw_s[...].astype(BF16)

    tok = lambda w: pl.BlockSpec((tm, w), lambda i: (i, 0))
    row = pl.BlockSpec((1, d), lambda i: (0, 0))
    whole = pl.BlockSpec((in_w, d), lambda i: (0, 0))
    kvx = pl.BlockSpec((2, tm, LANES), lambda i: (0, i, 0))
    return pl.pallas_call(
        body, name=name, grid=(n_t,),
        out_shape=[jax.ShapeDtypeStruct((t, d), F32), jax.ShapeDtypeStruct((1, d), F32),
                   jax.ShapeDtypeStruct((in_w, d), BF16), jax.ShapeDtypeStruct((t, d), BF16)],
        in_specs=[tok(d), tok(d), row, whole, tok(d), tok(LANES), tok(LANES), tok(A_Q_W), kvx, kvx,
                  tok(B_W), tok(B_W), tok(B_W)],
        out_specs=[tok(d), row, whole, tok(d)],
        scratch_shapes=[pltpu.VMEM((tm, in_w), BF16), pltpu.VMEM((in_w, d), F32)],
        compiler_params=_cp(("arbitrary",)),
    )(dres, x, gn, win_t, h, cos, sin_signed, daq, dakx, davx, dbq, dbk, dbv)


def _merge_out_proj_fwd(x, a_out, outs, lses, wout, name, tm):
    t, d = x.shape
    n_dil = len(DILATIONS)

    def body(x_ref, a_ref, *rest):
        o_refs, l_refs, (w_ref, y_ref, b_ref, lt_ref), b_segs, lt_segs, scratch = _split(rest, 1 + n_dil, 1 + n_dil, 4, n_dil, n_dil)
        n_c = B_W // LANES
        tiles = [scratch[j * n_c:(j + 1) * n_c] for j in range(2 * n_dil)]
        os_, ls = [o_refs[0][...].astype(F32)], [l_refs[0][...]]
        for k, dil in enumerate(DILATIONS):
            _gather_from_segments(o_refs[1 + k], dil, tiles[2 * k])
            _gather_from_segments(l_refs[1 + k], dil, tiles[2 * k + 1])
            os_.append(_get_tile(tiles[2 * k]))
            ls.append(_get_tile(tiles[2 * k + 1]))
        mx = functools.reduce(jnp.maximum, ls)
        es = [jnp.exp(l - mx) for l in ls]
        den = functools.reduce(jnp.add, es)
        b = functools.reduce(jnp.add, [e * o for e, o in zip(es, os_)]) / den
        lt = mx + jnp.log(den)
        bb = b.astype(BF16)
        b_ref[...] = bb
        lt_ref[...] = lt
        y_ref[...] = x_ref[...] + _dot_nn(a_ref[...], w_ref[0:A_Q_W, :]) + _dot_nn(bb, w_ref[A_Q_W:A_Q_W + B_W, :])
        _put_tile(tiles[0], b)
        _scatter_to_segments(tiles[0], b_segs)
        _put_tile(tiles[1], lt)
        _scatter_to_segments(tiles[1], lt_segs)

    tok = lambda w: pl.BlockSpec((tm, w), lambda i: (i, 0))
    segs = [_seg_spec(tm, dil, B_W) for dil in DILATIONS]
    res = pl.pallas_call(
        body, name=name, grid=(t // tm,),
        out_shape=[jax.ShapeDtypeStruct((t, d), F32), jax.ShapeDtypeStruct((t, B_W), BF16), jax.ShapeDtypeStruct((t, B_W), F32)]
        + [_seg_shape(t, dil, B_W, BF16) for dil in DILATIONS] + [_seg_shape(t, dil, B_W, F32) for dil in DILATIONS],
        in_specs=[tok(d), tok(A_Q_W)] + ([tok(B_W)] + segs) * 2 + [pl.BlockSpec(wout.shape, lambda i: (0, 0))],
        out_specs=[tok(d), tok(B_W), tok(B_W)] + segs * 2,
        scratch_shapes=_tile_scratch(tm, B_W) * (2 * n_dil),
        compiler_params=_cp(("arbitrary",)),
    )(x, a_out, *outs, *lses, wout)
    return res[0], [res[1]] + list(res[3:3 + n_dil]), [res[2]] + list(res[3 + n_dil:])


def _out_proj_bwd(dy, a_out, b_out, wout, name, tm):
    t, d = dy.shape
    n_t = t // tm
    n_dil = len(DILATIONS)

    def body(dy_ref, a_ref, b_ref, w_ref, da_ref, db_ref, gw_ref, *rest):
        db_segs, gw_s, tile = rest[:n_dil], rest[n_dil], rest[n_dil + 1:]
        i = pl.program_id(0)
        dyb = dy_ref[...].astype(BF16)
        da_ref[...] = _dot_nt(dyb, w_ref[0:A_Q_W, :]).astype(BF16)
        db = _dot_nt(dyb, w_ref[A_Q_W:A_Q_W + B_W, :])
        db_ref[...] = db.astype(BF16)
        _put_tile(tile, db)
        _scatter_to_segments(tile, db_segs)
        ga = _dot_tn(a_ref[...], dyb)
        gb = _dot_tn(b_ref[...], dyb)

        @pl.when(i == 0)
        def _():
            gw_s[0:A_Q_W, :] = ga
            gw_s[A_Q_W:A_Q_W + B_W, :] = gb

        @pl.when(i > 0)
        def _():
            gw_s[0:A_Q_W, :] += ga
            gw_s[A_Q_W:A_Q_W + B_W, :] += gb

        @pl.when(i == n_t - 1)
        def _():
            gw_ref[...] = gw_s[...].astype(BF16)

    tok = lambda w: pl.BlockSpec((tm, w), lambda i: (i, 0))
    whole = pl.BlockSpec(wout.shape, lambda i: (0, 0))
    res = pl.pallas_call(
        body, name=name, grid=(n_t,),
        out_shape=[jax.ShapeDtypeStruct((t, A_Q_W), BF16), jax.ShapeDtypeStruct((t, B_W), BF16),
                   jax.ShapeDtypeStruct(wout.shape, BF16)] + [_seg_shape(t, dil, B_W, BF16) for dil in DILATIONS],
        in_specs=[tok(d), tok(A_Q_W), tok(B_W), whole],
        out_specs=[tok(A_Q_W), tok(B_W), whole] + [_seg_spec(tm, dil, B_W) for dil in DILATIONS],
        scratch_shapes=[pltpu.VMEM(wout.shape, F32)] + _tile_scratch(tm, B_W),
        compiler_params=_cp(("arbitrary",)),
    )(dy, a_out, b_out, wout)
    return res[0], [res[1]] + list(res[3:]), res[2]


def _sum_pattern_grads(per_pattern, name, tm):
    t = per_pattern[0][0].shape[0]
    n_dil = len(DILATIONS)

    def body(*refs):
        ins, outs, tile = _split(refs, 3 * (1 + n_dil), 3)
        for j, o_ref in enumerate(outs):
            _put_tile(tile, ins[j][...].astype(F32))
            for k, dil in enumerate(DILATIONS):
                _gather_from_segments(ins[3 * (1 + k) + j], dil, tile, add=True)
            o_ref[...] = _get_tile(tile).astype(BF16)

    tok = pl.BlockSpec((tm, B_W), lambda i: (i, 0))
    flat = [a if k == 0 else a.reshape(DILATIONS[k - 1], -1, B_W) for k, grads in enumerate(per_pattern) for a in grads]
    return pl.pallas_call(
        body, name=name, grid=(t // tm,),
        out_shape=[jax.ShapeDtypeStruct((t, B_W), BF16)] * 3,
        in_specs=[tok] * 3 + [_seg_spec(tm, dil, B_W) for dil in DILATIONS for _ in range(3)],
        out_specs=[tok] * 3, scratch_shapes=_tile_scratch(tm, B_W),
        compiler_params=_cp(("arbitrary",)),
    )(*flat)


SUB_ROWS = 64


def _sub_band(r0, hw, win, seg_lo, seg_len, t, rel, col):
    ks = pl.multiple_of(jnp.clip(r0 - hw, 0, t - win), SUB_ROWS)
    kpos = col + ks
    valid = (jnp.abs(rel + (ks - r0)) <= hw) & (kpos >= seg_lo) & (kpos < seg_lo + seg_len)
    return ks, valid


def _split_heads(v, low):
    zero = jnp.zeros_like(v)
    return jnp.concatenate([jnp.where(low, v, zero), jnp.where(low, zero, v)], axis=0)


def _kv_spec(kv, t):
    if kv.ndim == 3:
        return pl.BlockSpec((None, t, LANES), lambda p, i: (p // 2, 0, 0))
    return pl.BlockSpec((t, LANES), lambda p, i: (0, p))


def _attn_fwd(q, k, v, sink, name, hw, seg_len, tq, has_sink):
    t, width = q.shape
    sb = SUB_ROWS
    win = 2 * hw + LANES

    def body(sink_ref, q_ref, k_ref, v_ref, o_ref, lse_ref):
        p, i = pl.program_id(0), pl.program_id(1)
        q0 = i * tq
        seg_lo = (q0 // seg_len) * seg_len
        low = lax.broadcasted_iota(jnp.int32, (1, LANES), 1) < HEAD_DIM
        rel = lax.broadcasted_iota(jnp.int32, (sb, win), 1) - lax.broadcasted_iota(jnp.int32, (sb, win), 0)
        col = lax.broadcasted_iota(jnp.int32, (1, win), 1)
        subs = []
        for j in range(tq // sb):
            rows = pl.ds(j * sb, sb)
            ks, valid = _sub_band(q0 + j * sb, hw, win, seg_lo, seg_len, t, rel, col)
            subs.append((rows, ks, valid, _dot_nt(_split_heads(q_ref[rows, :], low), k_ref[pl.ds(ks, win), :])))
        for rows, ks, valid, s in subs:
            vw = v_ref[pl.ds(ks, win), :]
            es, inv, lses = [], [], []
            for a in range(2):
                sa = jnp.where(valid, s[a * sb:(a + 1) * sb], NEG)
                m = jnp.max(sa, axis=1, keepdims=True)
                if has_sink:
                    sk = sink_ref[2 * p + a]
                    m = jnp.maximum(m, sk)
                e = jnp.exp(sa - m)
                den = jnp.sum(e, axis=1, keepdims=True)
                if has_sink:
                    den = den + jnp.exp(sk - m)
                es.append(e.astype(BF16))
                inv.append(1.0 / den)
                lses.append(m + jnp.log(den))
            pv = _dot_nn(jnp.concatenate(es, axis=0), vw)
            o_ref[rows, :] = jnp.where(low, pv[0:sb] * inv[0], pv[sb:2 * sb] * inv[1]).astype(BF16)
            lse_ref[rows, :] = jnp.where(low, lses[0], lses[1])

    tile = pl.BlockSpec((tq, LANES), lambda p, i: (i, p))
    return pl.pallas_call(
        body, name=name, grid=(width // LANES, t // tq),
        out_shape=[jax.ShapeDtypeStruct((t, width), BF16), jax.ShapeDtypeStruct((t, width), F32)],
        in_specs=[pl.BlockSpec(memory_space=pltpu.SMEM), tile, _kv_spec(k, t), _kv_spec(v, t)],
        out_specs=[tile, tile],
        compiler_params=_cp(("arbitrary", "arbitrary")),
    )(sink, q, k, v)


def _attn_bwd(q, k, v, o, do, lse, sink, name, hw, seg_len, tq, has_sink, comm=None, comm_at=None):
    t, width = q.shape
    sb = SUB_ROWS
    win = 2 * hw + LANES
    n_q = t // tq
    shared_kv = k.ndim == 3
    n_ci, n_co, c_in_specs, c_out_specs, c_shapes, c_scratch, c_inputs = _comm_parts(comm)

    def body(*refs):
        ((sink_ref, q_ref, k_ref, v_ref, o_ref, do_ref, lse_ref), c_in, (dq_ref, dk_ref, dv_ref, ds_ref), c_out,
         (dk_s, dv_s), c_scr) = _split(refs, 7, n_ci, 4, n_co, 2)
        p, i = pl.program_id(0), pl.program_id(1)
        if comm:
            _run_hosted(comm, comm_at, p * n_q + i, c_in, c_out, c_scr)
        fresh = (i == 0) & (p % 2 == 0) if shared_kv else i == 0
        last = (i == n_q - 1) & (p % 2 == 1) if shared_kv else i == n_q - 1

        @pl.when(fresh)
        def _():
            dk_s[...] = jnp.zeros_like(dk_s)
            dv_s[...] = jnp.zeros_like(dv_s)

        q0 = i * tq
        seg_lo = (q0 // seg_len) * seg_len
        low = lax.broadcasted_iota(jnp.int32, (1, LANES), 1) < HEAD_DIM
        rel = lax.broadcasted_iota(jnp.int32, (sb, win), 1) - lax.broadcasted_iota(jnp.int32, (sb, win), 0)
        col = lax.broadcasted_iota(jnp.int32, (1, win), 1)
        dsink = [jnp.zeros((1, 1), F32), jnp.zeros((1, 1), F32)]
        subs = []
        for j in range(tq // sb):
            rows = pl.ds(j * sb, sb)
            ks, valid = _sub_band(q0 + j * sb, hw, win, seg_lo, seg_len, t, rel, col)
            kw = k_ref[pl.ds(ks, win), :]
            dov = do_ref[rows, :]
            q2 = _split_heads(q_ref[rows, :], low)
            do2 = _split_heads(dov, low)
            subs.append((rows, ks, valid, kw, dov, q2, do2, _dot_nt(q2, kw), _dot_nt(do2, v_ref[pl.ds(ks, win), :])))
        probs = []
        for rows, ks, valid, kw, dov, q2, do2, s, dpr in subs:
            prod = dov.astype(F32) * o_ref[rows, :].astype(F32)
            lse_t = lse_ref[rows, :]
            prs, dss = [], []
            for a in range(2):
                mine = low if a == 0 else jnp.logical_not(low)
                lse_a = jnp.max(jnp.where(mine, lse_t, -jnp.inf), axis=1, keepdims=True)
                delta = jnp.sum(jnp.where(mine, prod, 0.0), axis=1, keepdims=True)
                pr = jnp.exp(jnp.where(valid, s[a * sb:(a + 1) * sb], NEG) - lse_a)
                prs.append(pr.astype(BF16))
                dss.append((pr * (dpr[a * sb:(a + 1) * sb] - delta)).astype(BF16))
                if has_sink:
                    dsink[a] = dsink[a] - jnp.sum(jnp.exp(sink_ref[2 * p + a] - lse_a) * delta, axis=0, keepdims=True)
            probs.append((jnp.concatenate(prs, axis=0), jnp.concatenate(dss, axis=0)))
        for (rows, ks, valid, kw, dov, q2, do2, s, dpr), (pr2, ds2) in zip(subs, probs):
            dv_s[pl.ds(ks, win), :] += _dot_tn(pr2, do2)
            dk_s[pl.ds(ks, win), :] += _dot_tn(ds2, q2)
            dq2 = _dot_nn(ds2, kw)
            dq_ref[rows, :] = jnp.where(low, dq2[0:sb], dq2[sb:2 * sb]).astype(BF16)
        ds_ref[...] = jnp.broadcast_to(jnp.where(low, dsink[0], dsink[1]), ds_ref.shape)

        @pl.when(last)
        def _():
            dk_ref[...] = dk_s[...].astype(BF16)
            dv_ref[...] = dv_s[...].astype(BF16)

    tile = pl.BlockSpec((tq, LANES), lambda p, i: (i, p))
    kv_shape = jax.ShapeDtypeStruct(k.shape, BF16)
    res = pl.pallas_call(
        body, name=name, grid=(width // LANES, n_q),
        out_shape=[jax.ShapeDtypeStruct((t, width), BF16), kv_shape, kv_shape,
                   jax.ShapeDtypeStruct((width // LANES, n_q, 8, LANES), F32)] + c_shapes,
        in_specs=[pl.BlockSpec(memory_space=pltpu.SMEM), tile, _kv_spec(k, t), _kv_spec(v, t), tile, tile, tile] + c_in_specs,
        out_specs=[tile, _kv_spec(k, t), _kv_spec(v, t), pl.BlockSpec((None, None, 8, LANES), lambda p, i: (p, i, 0, 0))] + c_out_specs,
        scratch_shapes=[pltpu.VMEM((t, LANES), F32)] * 2 + c_scratch,
        compiler_params=_cp(("arbitrary", "arbitrary")),
    )(sink, q, k, v, o, do, lse, *c_inputs)
    return res[:4], res[4:]


def _final_norm_loss(x, gn, target, name, tm):
    t, d = x.shape

    def body(x_ref, gn_ref, tg_ref, dx_ref, dgn_ref, sq_ref, half_ref):
        i = pl.program_id(0)
        xv, gnv = x_ref[...], gn_ref[...]
        r = _rstd(xv)
        diff = (xv * r) * gnv - tg_ref[...]
        dxn, dgn = _norm_bwd(diff * (1.0 / d), xv, r, gnv)
        dx_ref[...] = dxn
        half_ref[...] = (FFN_RES_WEIGHT * dxn).astype(BF16)
        _accumulate(dgn_ref, dgn, i == 0)
        _accumulate(sq_ref, jnp.sum(diff * diff, axis=0, keepdims=True), i == 0)

    tok = pl.BlockSpec((tm, d), lambda i: (i, 0))
    row = pl.BlockSpec((1, d), lambda i: (0, 0))
    return pl.pallas_call(
        body, name=name, grid=(t // tm,),
        out_shape=[jax.ShapeDtypeStruct((t, d), F32), jax.ShapeDtypeStruct((1, d), F32), jax.ShapeDtypeStruct((1, d), F32),
                   jax.ShapeDtypeStruct((t, d), BF16)],
        in_specs=[tok, row, tok], out_specs=[tok, row, row, tok],
        compiler_params=_cp(("arbitrary",)),
    )(x, gn, target)


def _adamw(w, g, m, v, name):
    def body(w_ref, g_ref, m_ref, v_ref, d_ref, nm_ref, nv_ref):
        gv = g_ref[...]
        nm = ADAM_B1 * m_ref[...] + (1.0 - ADAM_B1) * gv
        nv = ADAM_B2 * v_ref[...] + (1.0 - ADAM_B2) * (gv * gv)
        m_hat = nm / (1.0 - ADAM_B1 ** ADAM_STEP)
        v_hat = nv / (1.0 - ADAM_B2 ** ADAM_STEP)
        d_ref[...] = -ADAM_LR * (m_hat / (jnp.sqrt(v_hat) + ADAM_EPS) + ADAM_WD * w_ref[...])
        nm_ref[...] = nm
        nv_ref[...] = nv

    shape = jax.ShapeDtypeStruct(w.shape, F32)
    return pl.pallas_call(body, name=name, out_shape=[shape, shape, shape], compiler_params=_cp())(w, g, m, v)


def _adamw_rows(land, j, rows, w, m, v, name):
    d = w.shape[1]

    def body(l_ref, w_ref, m_ref, v_ref, g_ref, d_ref, nm_ref, nv_ref):
        gv = l_ref[0].astype(F32)
        for s in range(1, N_DEV):
            gv = gv + l_ref[s].astype(F32)
        g_ref[...] = gv
        nm = ADAM_B1 * m_ref[...] + (1.0 - ADAM_B1) * gv
        nv = ADAM_B2 * v_ref[...] + (1.0 - ADAM_B2) * (gv * gv)
        m_hat = nm / (1.0 - ADAM_B1 ** ADAM_STEP)
        v_hat = nv / (1.0 - ADAM_B2 ** ADAM_STEP)
        d_ref[...] = -ADAM_LR * (m_hat / (jnp.sqrt(v_hat) + ADAM_EPS) + ADAM_WD * w_ref[...])
        nm_ref[...] = nm
        nv_ref[...] = nv

    halves = 2
    half = pl.BlockSpec((rows // halves, d), lambda i: (i, 0))
    shape = jax.ShapeDtypeStruct((rows, d), F32)
    return pl.pallas_call(
        body, name=name, grid=(halves,), out_shape=[shape] * 4,
        in_specs=[pl.BlockSpec((N_DEV, rows // halves, d), lambda i: (0, halves * j + i, 0)), half, half, half],
        out_specs=[half] * 4,
        compiler_params=_cp(("arbitrary",)),
    )(land, w, m, v)


def _sum_small(land, name):
    def body(l_ref, o_ref):
        acc = l_ref[0]
        for s in range(1, N_DEV):
            acc = acc + l_ref[s]
        o_ref[...] = acc

    return pl.pallas_call(body, name=name, out_shape=jax.ShapeDtypeStruct(land.shape[1:], F32), compiler_params=_cp())(land)


def _rope_lanes(positions):
    inv_freq = 1.0 / (ROPE_THETA ** (jnp.arange(0, HEAD_DIM, 2, dtype=F32) / HEAD_DIM))
    ang = positions.astype(F32)[:, None] * inv_freq
    cos, sin = jnp.cos(ang), jnp.sin(ang)
    return jnp.concatenate([cos, cos, cos, cos], axis=1), jnp.concatenate([-sin, sin, -sin, sin], axis=1)


def kernel(x, positions, norm_ffn1, w_gate1, w_up1, w_down1, norm_mix, w_in, a_sink, w_out, norm_ffn2, w_gate2, w_up2, w_down2, norm_final, loss_target, m_norm_ffn1, m_w_gate1, m_w_up1, m_w_down1, m_norm_mix, m_w_in, m_a_sink, m_w_out, m_norm_ffn2, m_w_gate2, m_w_up2, m_w_down2, m_norm_final, v_norm_ffn1, v_w_gate1, v_w_up1, v_w_down1, v_norm_mix, v_w_in, v_a_sink, v_w_out, v_norm_ffn2, v_w_gate2, v_w_up2, v_w_down2, v_norm_final):
    x = x[0]
    target = loss_target[0]
    t, d = x.shape
    tm = min(1024, t)
    tm_mix = min(512, t)
    tf = 256
    tq = min(512, t // 16)

    tm_hidden = min(2048, t)
    tt = min(1024, t)
    f_all = w_gate1.shape[2] * N_DEV
    tfp = f_all // 2
    tm_fwd, tf_fwd = tm_mix, tfp
    n_steps = (t // tm_fwd) * (f_all // tf_fwd)

    def stacked(shards):
        return jnp.concatenate([s.astype(BF16) for s in shards], axis=0), [s.shape[0] for s in shards]

    packed1, rows1 = stacked([w_gate1[0].T, w_up1[0].T, w_down1[0]])
    packed2, rows2 = stacked([w_in[0].T, w_out[0], w_gate2[0].T, w_up2[0].T, w_down2[0]])
    wg1, wu1, wd1 = _run_alone(_gather_plan(packed1, rows1), "gather_ffn1_weights")

    cos, sin_signed = _rope_lanes(positions[0])
    sink = a_sink[0]
    no_sink = jnp.zeros_like(sink)

    (x1, h1, gp1, up1, hid1), (win, wout, wg2, wu2, wd2) = _ffn_fwd(
        x, norm_ffn1, wg1, wu1, wd1, "ffn1_fwd", tm_fwd, tf_fwd, _gather_plan(packed2, rows2), [0, (3 * n_steps) // 4, n_steps - 1])
    (h_mix, aq, akx, avx, bq, bk, bv), seg_qkv = _in_proj_fwd(x1, norm_mix, win, cos, sin_signed, "in_proj_fwd", tm_mix)
    a_out, a_lse = _attn_fwd(aq, akx, avx, sink, "attn_a_fwd", A_HALF_WINDOW, t, tq, True)
    rows_of = lambda a: a.reshape(t, B_W)
    segments_of = lambda a, dil: a if dil == 1 else a.reshape(dil, t // dil, B_W)
    b_qkv, b_outs, b_lses = [], [], []
    for n, (window, dil) in enumerate(B_PATTERNS):
        qs, ks, vs = (bq, bk, bv) if dil == 1 else (rows_of(seg_qkv[k][n - 1]) for k in range(3))
        o_seg, lse_seg = _attn_fwd(qs, ks, vs, no_sink, f"attn_b{dil}_fwd", window // (2 * dil), t // dil, tq, False)
        b_qkv.append((qs, ks, vs))
        b_outs.append(segments_of(o_seg, dil))
        b_lses.append(segments_of(lse_seg, dil))
    x2, b_out, b_lse = _merge_out_proj_fwd(x1, a_out, b_outs, b_lses, wout, "out_proj_fwd", tm_mix)
    (x3, h2, gp2, up2, hid2), _ = _ffn_fwd(x2, norm_ffn2, wg2, wu2, wd2, "ffn2_fwd", tm_fwd, tf_fwd)

    gfinal = norm_final.reshape(1, d)
    dx3, dg_final, sq, dout2 = _final_norm_loss(x3, gfinal, target, "final_norm_loss", tm)

    dgt2, dut2 = _ffn_bwd_hidden(dout2, gp2, up2, wd2, "ffn2_bwd_hidden", tm_hidden, tf)
    (dx2, dg_ffn2), _ = _ffn_bwd_input(dx3, x2, norm_ffn2, dgt2, dut2, wg2, wu2, "ffn2_bwd_input", tm_mix)
    (gwg2, gwu2), _ = _token_products([dgt2, dut2], h2, "ffn2_bwd_gate_up", tt, tfp)
    (gwd2,), _ = _token_products([hid2], dout2, "ffn2_bwd_down", tt, tfp)

    n_att = (A_Q_W // LANES) * (t // tq)
    n_prod = (f_all // tfp) * (t // tt)
    da, db, gwout = _out_proj_bwd(dx2, a_out, b_out[0], wout, "out_proj_bwd", tm_mix)
    (daq, dakx, davx, dsink_parts), (land_wg2,) = _attn_bwd(
        aq, akx, avx, a_out, da, a_lse, sink, "attn_a_bwd", A_HALF_WINDOW, t, tq, True, _exchange_plan([[gwg2]]), [0, n_att - 1])
    pattern_grads, pattern_lands = [], []
    for n, ((window, dil), (qs, ks, vs)) in enumerate(zip(B_PATTERNS, b_qkv)):
        carried = [gwu2, gwd2][n:n + 1]
        grads, lands = _attn_bwd(qs, ks, vs, rows_of(b_out[n]), rows_of(db[n]), rows_of(b_lse[n]), no_sink, f"attn_b{dil}_bwd",
                                 window // (2 * dil), t // dil, tq, False,
                                 _exchange_plan([carried]) if carried else None, [0, (B_W // LANES) * (t // tq) - 1])
        pattern_grads.append(grads[:3])
        pattern_lands.extend(lands)
    land_wu2, land_wd2 = pattern_lands
    dbq, dbk, dbv = _sum_pattern_grads(pattern_grads, "sum_pattern_grads", tm_mix)
    dx1, dg_mix, gwin, dout1 = _in_proj_bwd(dx2, x1, norm_mix, win, h_mix, cos, sin_signed, daq, dakx, davx, dbq, dbk, dbv, "in_proj_bwd", tm_mix)

    (gwd1,), (land_in, land_out) = _token_products(
        [hid1], dout1, "ffn1_bwd_down", tt, tfp, _exchange_plan([[gwin], [gwout]]), [0, n_prod - 1])
    dgt1, dut1 = _ffn_bwd_hidden(dout1, gp1, up1, wd1, "ffn1_bwd_hidden", tm_hidden, tf)
    (gwg1,), (land_wd1,) = _token_products([dgt1], h1, "ffn1_bwd_gate", tt, tfp, _exchange_plan([[gwd1]]), [0, n_prod - 1])
    (gwu1,), (land_wg1,) = _token_products([dut1], h1, "ffn1_bwd_up", tt, tfp, _exchange_plan([[gwg1]]), [0, n_prod - 1])
    (grad_x, dg_ffn1), (land_wu1,) = _ffn_bwd_input(
        dx1, x, norm_ffn1, dgt1, dut1, wg1, wu1, "ffn1_bwd_input", tm_mix, _exchange_plan([[gwu1]]), [0, t // tm_mix - 1])

    dsink_pairs = jnp.sum(dsink_parts[:, :, 0, :], axis=1)
    dsink = jnp.stack([dsink_pairs[:, 0], dsink_pairs[:, HEAD_DIM]], axis=1).reshape(1, -1)
    small = jnp.concatenate([dg_ffn1, dg_mix, dg_ffn2, dg_final, jnp.pad(dsink, ((0, 0), (0, d - dsink.shape[1]))),
                             sq, jnp.zeros((2, d), F32)], axis=0)
    (land_small,) = _run_alone(_exchange_plan([[jnp.tile(small, (N_DEV, 1))]]), "gather_small_gradients")
    red_small = _sum_small(land_small, "sum_small_grads")
    loss = 0.5 * jnp.sum(red_small[5]) / d

    rf = rows1[0]
    sharded = {"w_gate1": (land_wg1, 0, rf, True), "w_up1": (land_wu1, 0, rf, True), "w_down1": (land_wd1, 0, rf, False),
               "w_in": (land_in, 0, rows2[0], True), "w_out": (land_out, 0, rows2[1], False),
               "w_gate2": (land_wg2, 0, rf, True), "w_up2": (land_wu2, 0, rf, True), "w_down2": (land_wd2, 0, rf, False)}
    n_sink = a_sink.shape[1]
    small_grads = {"norm_ffn1": red_small[0:1], "norm_mix": red_small[1:2], "norm_ffn2": red_small[2:3], "norm_final": red_small[3],
                   "a_sink": red_small[4:5, :n_sink]}
    params = {
        "norm_ffn1": (norm_ffn1, m_norm_ffn1, v_norm_ffn1), "w_gate1": (w_gate1, m_w_gate1, v_w_gate1),
        "w_up1": (w_up1, m_w_up1, v_w_up1), "w_down1": (w_down1, m_w_down1, v_w_down1),
        "norm_mix": (norm_mix, m_norm_mix, v_norm_mix), "w_in": (w_in, m_w_in, v_w_in),
        "a_sink": (a_sink, m_a_sink, v_a_sink), "w_out": (w_out, m_w_out, v_w_out),
        "norm_ffn2": (norm_ffn2, m_norm_ffn2, v_norm_ffn2), "w_gate2": (w_gate2, m_w_gate2, v_w_gate2),
        "w_up2": (w_up2, m_w_up2, v_w_up2), "w_down2": (w_down2, m_w_down2, v_w_down2),
        "norm_final": (norm_final, m_norm_final, v_norm_final),
    }
    grad_list, deltas, new_ms, new_vs = [], [], [], []
    for name, (w, m, v) in params.items():
        if name in sharded:
            land, j, rows, is_transposed = sharded[name]
            view = (lambda a: a[0].T) if is_transposed else (lambda a: a[0])
            back = (lambda a: a.T[None]) if is_transposed else (lambda a: a[None])
            outs = [back(o) for o in _adamw_rows(land, j, rows, view(w), view(m), view(v), f"adamw_{name}")]
        else:
            as_block = (lambda a: a.reshape(1, -1)) if w.ndim == 1 else (lambda a: a)
            g = small_grads[name]
            outs = [g] + [o.reshape(w.shape) for o in _adamw(as_block(w), as_block(g), as_block(m), as_block(v), f"adamw_{name}")]
        for lst, o in zip((grad_list, deltas, new_ms, new_vs), outs):
            lst.append(o)
    return (loss, grad_x[None], *grad_list, *deltas, *new_ms, *new_vs)
```

```python
import functools
import itertools

import numpy as np
import jax
import jax.numpy as jnp
from jax import lax
from jax.experimental import pallas as pl
from jax.experimental.pallas import tpu as pltpu

F32 = jnp.float32
BF16 = jnp.bfloat16

N_DEV = 8
HEAD_DIM = 64
LANES = 128
A_Q_W, A_KV_W, B_W = 512, 128, 512
A_HALF_WINDOW = 128
B_PATTERNS = ((128, 1), (512, 4), (2048, 16))
ROPE_THETA = 10000.0
NORM_EPS = 1e-6
FFN_RES_WEIGHT = 0.5
QK_SCALE = HEAD_DIM ** -0.5
NEG = -1e30

ADAM_LR = 0.001
ADAM_B1 = 0.9
ADAM_B2 = 0.999
ADAM_EPS = 1e-08
ADAM_WD = 0.01
ADAM_STEP = 10

MESH_T = pl.DeviceIdType.MESH
VMEM_LIMIT = 56 * 1024 * 1024


def _cp(sem=None, vmem=VMEM_LIMIT):
    return pltpu.CompilerParams(dimension_semantics=sem, vmem_limit_bytes=vmem)


def _dot_nn(a, b):
    return jnp.dot(a, b, preferred_element_type=F32)


def _dot_nt(a, b):
    return lax.dot_general(a, b, (((1,), (1,)), ((), ())), preferred_element_type=F32)


def _dot_tn(a, b):
    return lax.dot_general(a, b, (((0,), (0,)), ((), ())), preferred_element_type=F32)


def _rstd(xv):
    return lax.rsqrt(jnp.mean(xv * xv, axis=-1, keepdims=True) + NORM_EPS)


def _norm_bwd(dh, xv, r, gn):
    gy = dh * gn
    c = jnp.sum(gy * xv, axis=-1, keepdims=True) * (1.0 / xv.shape[-1])
    dx = r * gy - xv * (r * r * r * c)
    dgn = jnp.sum(dh * (xv * r), axis=0, keepdims=True)
    return dx, dgn


def _accumulate(ref, val, first):
    @pl.when(first)
    def _():
        ref[...] = val

    @pl.when(jnp.logical_not(first))
    def _():
        ref[...] += val


def _mesh_pos():
    return lax.axis_index("x"), lax.axis_index("y"), lax.axis_index("c")


def _dev_index(d):
    return 4 * d[0] + 2 * d[1] + d[2]


class _Comm:
    def __init__(self, inputs, out_shape, scratch, phases):
        self.inputs, self.out_shape, self.scratch, self.phases = inputs, out_shape, scratch, phases

    def specs(self):
        any_spec = pl.BlockSpec(memory_space=pl.ANY)
        return [any_spec] * len(self.inputs), [any_spec] * len(self.out_shape)


def _run_alone(comm, name):
    n_in, n_out = len(comm.inputs), len(comm.out_shape)

    def body(*refs):
        for phase in comm.phases:
            phase(refs[:n_in], refs[n_in:n_in + n_out], refs[n_in + n_out:])

    in_specs, out_specs = comm.specs()
    return pl.pallas_call(body, name=name, out_shape=comm.out_shape, in_specs=in_specs, out_specs=out_specs,
                          scratch_shapes=comm.scratch)(*comm.inputs)


def _run_hosted(comm, at, step, ins, outs, scr):
    for phase, when in zip(comm.phases, at):
        @pl.when(step == when)
        def _(phase=phase):
            phase(ins, outs, scr)


def _split(refs, *counts):
    parts, o = [], 0
    for n in counts:
        parts.append(refs[o:o + n])
        o += n
    return parts + [refs[o:]]


def _gather_plan(packed, rows_list):
    n_w = len(rows_list)
    offs = [int(o) for o in np.cumsum([0] + list(rows_list[:-1]))]
    d = packed.shape[1]

    def tools(ins, outs, scr):
        p_ref = ins[0]
        send_sems, recv_sems, local_sem = scr
        x, y, c = _mesh_pos()
        me, sibling = (x, y, c), (x, y, 1 - c)
        chips = [(1 - x, y), (x, 1 - y), (1 - x, 1 - y)]

        def rows(w, dev):
            start = pl.multiple_of(_dev_index(dev) * rows_list[w], 16)
            return outs[w].at[pl.ds(start, rows_list[w]), :]

        def mine(w):
            return p_ref.at[pl.ds(offs[w], rows_list[w]), :]

        def copy(k, w, block, to, own):
            return pltpu.make_async_remote_copy(
                src_ref=mine(w) if own else rows(w, block), dst_ref=rows(w, block),
                send_sem=send_sems.at[k], recv_sem=recv_sems.at[k], device_id=to, device_id_type=MESH_T)

        def all_blocks(k):
            return pltpu.make_async_remote_copy(
                src_ref=p_ref, dst_ref=p_ref, send_sem=send_sems.at[k], recv_sem=recv_sems.at[k],
                device_id=me, device_id_type=MESH_T)

        return p_ref, local_sem, me, sibling, chips, c, rows, mine, copy, all_blocks

    def start(ins, outs, scr):
        _, local_sem, me, sibling, chips, c, rows, mine, copy, _ = tools(ins, outs, scr)
        for w in range(n_w):
            pltpu.make_async_copy(mine(w), rows(w, me), local_sem).start()
        for w in range(n_w):
            copy(0, w, me, sibling, True).start()
        for j, chip in enumerate(chips):
            for w in range(n_w):
                copy(1 + j, w, me, (*chip, c), True).start()

    def relay(ins, outs, scr):
        _, _, _, sibling, chips, c, _, _, copy, all_blocks = tools(ins, outs, scr)
        for j, chip in enumerate(chips):
            all_blocks(1 + j).wait_recv()
            for w in range(n_w):
                copy(4 + j, w, (*chip, c), sibling, False).start()

    def finish(ins, outs, scr):
        p_ref, local_sem, _, _, _, _, _, _, _, all_blocks = tools(ins, outs, scr)
        all_blocks(0).wait_recv()
        for j in range(3):
            all_blocks(4 + j).wait_recv()
        for k in range(7):
            all_blocks(k).wait_send()
        pltpu.make_async_copy(p_ref, p_ref, local_sem).wait()

    return _Comm(
        [packed], [jax.ShapeDtypeStruct((N_DEV * r, d), packed.dtype) for r in rows_list],
        [pltpu.SemaphoreType.DMA((7,)), pltpu.SemaphoreType.DMA((7,)), pltpu.SemaphoreType.DMA], [start, relay, finish])


def _exchange_plan(groups):
    flat = [a for g in groups for a in g]
    n_g = len(groups)
    sizes = [len(g) for g in groups]
    rows = [g[0].shape[0] // N_DEV for g in groups]
    first = [int(o) for o in np.cumsum([0] + sizes[:-1])]

    def start(srcs, lands, scr):
        send_sems, recv_sems, local_sems = scr
        x, y, c = _mesh_pos()
        me = (x, y, c)
        me_idx = _dev_index(me)

        def block(g, i, dev):
            start_row = pl.multiple_of(_dev_index(dev) * rows[g], 8)
            return srcs[first[g] + i].at[pl.ds(start_row, rows[g]), :]

        def slot(g, i):
            return lands[g].at[me_idx, pl.ds(i * rows[g], rows[g]), :]

        for g in range(n_g):
            for i in range(sizes[g]):
                pltpu.make_async_copy(block(g, i, me), slot(g, i), local_sems.at[g]).start()
        flips = [f for f in itertools.product((0, 1), repeat=3) if any(f)]
        for k, (fx, fy, fc) in enumerate(flips):
            peer = (1 - x if fx else x, 1 - y if fy else y, 1 - c if fc else c)
            for g in range(n_g):
                for i in range(sizes[g]):
                    pltpu.make_async_remote_copy(
                        src_ref=block(g, i, peer), dst_ref=slot(g, i), send_sem=send_sems.at[g, k],
                        recv_sem=recv_sems.at[g, k], device_id=peer, device_id_type=MESH_T).start()

    def finish(srcs, lands, scr):
        send_sems, recv_sems, local_sems = scr
        me = _mesh_pos()
        for k in range(7):
            for g in range(n_g):
                pltpu.make_async_remote_copy(
                    src_ref=lands[g].at[0], dst_ref=lands[g].at[0], send_sem=send_sems.at[g, k],
                    recv_sem=recv_sems.at[g, k], device_id=me, device_id_type=MESH_T).wait()
        for g in range(n_g):
            pltpu.make_async_copy(lands[g].at[0], lands[g].at[0], local_sems.at[g]).wait()

    return _Comm(
        flat, [jax.ShapeDtypeStruct((N_DEV, sizes[g] * rows[g], groups[g][0].shape[1]), groups[g][0].dtype) for g in range(n_g)],
        [pltpu.SemaphoreType.DMA((n_g, 7)), pltpu.SemaphoreType.DMA((n_g, 7)), pltpu.SemaphoreType.DMA((n_g,))],
        [start, finish])


def _comm_parts(comm):
    if comm is None:
        return 0, 0, [], [], [], [], []
    in_specs, out_specs = comm.specs()
    return len(comm.inputs), len(comm.out_shape), in_specs, out_specs, comm.out_shape, comm.scratch, comm.inputs


def _ffn_fwd(x, gn, wg_t, wu_t, wd, name, tm, tf, comm=None, comm_at=None):
    t, d = x.shape
    f_all = wg_t.shape[0]
    n_f = f_all // tf
    n_ci, n_co, c_in_specs, c_out_specs, c_shapes, c_scratch, c_inputs = _comm_parts(comm)

    def body(*refs):
        ((x_ref, gn_ref, wg_ref, wu_ref, wd_ref), c_in, (y_ref, h_ref, gp_ref, up_ref, hid_ref), c_out,
         (h_s, hid_s), c_scr) = _split(refs, 5, n_ci, 5, n_co, 2)
        f = pl.program_id(1)
        if comm:
            _run_hosted(comm, comm_at, pl.program_id(0) * n_f + f, c_in, c_out, c_scr)

        @pl.when(f == 0)
        def _():
            xv = x_ref[...]
            h = ((xv * _rstd(xv)) * gn_ref[...]).astype(BF16)
            h_s[...] = h
            h_ref[...] = h

        h = h_s[...]
        g = _dot_nt(h, wg_ref[...])
        u = _dot_nt(h, wu_ref[...])
        sg = jax.nn.sigmoid(g)
        silu = g * sg
        gp_ref[...] = (u * (sg * (1.0 + g * (1.0 - sg)))).astype(BF16)
        up_ref[...] = silu.astype(BF16)
        hid = (silu * u).astype(BF16)
        hid_ref[...] = hid
        for f0 in range(n_f):
            @pl.when(f == f0)
            def _(f0=f0):
                hid_s[:, f0 * tf:(f0 + 1) * tf] = hid

        @pl.when(f == n_f - 1)
        def _():
            y_ref[...] = x_ref[...] + FFN_RES_WEIGHT * _dot_nn(hid_s[...], wd_ref[...])

    tok = pl.BlockSpec((tm, d), lambda i, f: (i, 0))
    wblk = pl.BlockSpec((tf, d), lambda i, f: (f, 0))
    whole = pl.BlockSpec((f_all, d), lambda i, f: (0, 0))
    act = pl.BlockSpec((tm, tf), lambda i, f: (i, f))
    act_shape = jax.ShapeDtypeStruct((t, f_all), BF16)
    res = pl.pallas_call(
        body, name=name, grid=(t // tm, n_f),
        out_shape=[jax.ShapeDtypeStruct((t, d), F32), jax.ShapeDtypeStruct((t, d), BF16), act_shape, act_shape, act_shape] + c_shapes,
        in_specs=[tok, pl.BlockSpec((1, d), lambda i, f: (0, 0)), wblk, wblk, whole] + c_in_specs,
        out_specs=[tok, tok, act, act, act] + c_out_specs,
        scratch_shapes=[pltpu.VMEM((tm, d), BF16), pltpu.VMEM((tm, f_all), BF16)] + c_scratch,
        compiler_params=_cp(("arbitrary", "arbitrary")),
    )(x, gn, wg_t, wu_t, wd, *c_inputs)
    return res[:5], res[5:]


def _ffn_bwd_hidden(dout, gp, up, hid, wd, name, tt, tf, comm=None, comm_at=None):
    t, d = dout.shape
    f_all = wd.shape[0]
    n_t = t // tt
    n_ci, n_co, c_in_specs, c_out_specs, c_shapes, c_scratch, c_inputs = _comm_parts(comm)

    def body(*refs):
        (dout_ref, gp_ref, up_ref, hid_ref, wd_ref), c_in, (dg_ref, du_ref, gwd_ref), c_out, (acc,), c_scr = _split(
            refs, 5, n_ci, 3, n_co, 1)
        s = pl.program_id(1)
        if comm:
            _run_hosted(comm, comm_at, pl.program_id(0) * n_t + s, c_in, c_out, c_scr)
        doutv = dout_ref[...]
        dhid = _dot_nt(doutv, wd_ref[...])
        dg_ref[...] = (dhid * gp_ref[...].astype(F32)).astype(BF16)
        du_ref[...] = (dhid * up_ref[...].astype(F32)).astype(BF16)
        _accumulate(acc, _dot_tn(hid_ref[...], doutv), s == 0)

        @pl.when(s == n_t - 1)
        def _():
            gwd_ref[...] = acc[...].astype(BF16)

    tok = pl.BlockSpec((tt, d), lambda f, s: (s, 0))
    wblk = pl.BlockSpec((tf, d), lambda f, s: (f, 0))
    act = pl.BlockSpec((tt, tf), lambda f, s: (s, f))
    act_shape = jax.ShapeDtypeStruct((t, f_all), BF16)
    res = pl.pallas_call(
        body, name=name, grid=(f_all // tf, n_t),
        out_shape=[act_shape, act_shape, jax.ShapeDtypeStruct((f_all, d), BF16)] + c_shapes,
        in_specs=[tok, act, act, act, wblk] + c_in_specs, out_specs=[act, act, wblk] + c_out_specs,
        scratch_shapes=[pltpu.VMEM((tf, d), F32)] + c_scratch,
        compiler_params=_cp(("arbitrary", "arbitrary")),
    )(dout, gp, up, hid, wd, *c_inputs)
    return res[:3], res[3:]


def _ffn_bwd_input(dy, x, gn, dg, du, wg_t, wu_t, name, tm, comm=None, comm_at=None):
    t, d = x.shape
    f_all = wg_t.shape[0]
    n_ci, n_co, c_in_specs, c_out_specs, c_shapes, c_scratch, c_inputs = _comm_parts(comm)

    def body(*refs):
        (dy_ref, x_ref, gn_ref, dg_ref, du_ref, wg_ref, wu_ref), c_in, (dx_ref, dgn_ref), c_out, c_scr = _split(
            refs, 7, n_ci, 2, n_co)
        i = pl.program_id(0)
        if comm:
            _run_hosted(comm, comm_at, i, c_in, c_out, c_scr)
        dh = _dot_nn(dg_ref[...], wg_ref[...]) + _dot_nn(du_ref[...], wu_ref[...])
        xv = x_ref[...]
        dxn, dgn = _norm_bwd(dh, xv, _rstd(xv), gn_ref[...])
        dx_ref[...] = dy_ref[...] + dxn
        _accumulate(dgn_ref, dgn, i == 0)

    tok = pl.BlockSpec((tm, d), lambda i: (i, 0))
    row = pl.BlockSpec((1, d), lambda i: (0, 0))
    act = pl.BlockSpec((tm, f_all), lambda i: (i, 0))
    whole = pl.BlockSpec((f_all, d), lambda i: (0, 0))
    res = pl.pallas_call(
        body, name=name, grid=(t // tm,),
        out_shape=[jax.ShapeDtypeStruct((t, d), F32), jax.ShapeDtypeStruct((1, d), F32)] + c_shapes,
        in_specs=[tok, tok, row, act, act, whole, whole] + c_in_specs,
        out_specs=[tok, row] + c_out_specs,
        scratch_shapes=c_scratch,
        compiler_params=_cp(("arbitrary",)),
    )(dy, x, gn, dg, du, wg_t, wu_t, *c_inputs)
    return res[:2], res[2:]


def _token_products(lhs_list, rhs, name, tt, tf, comm=None, comm_at=None):
    n_l = len(lhs_list)
    t, f_all = lhs_list[0].shape
    d = rhs.shape[1]
    n_t = t // tt
    n_ci, n_co, c_in_specs, c_out_specs, c_shapes, c_scratch, c_inputs = _comm_parts(comm)

    def body(*refs):
        lhs_refs, (rhs_ref,), c_in, out_refs, c_out, accs, c_scr = _split(refs, n_l, 1, n_ci, n_l, n_co, n_l)
        s = pl.program_id(1)
        if comm:
            _run_hosted(comm, comm_at, pl.program_id(0) * n_t + s, c_in, c_out, c_scr)
        rv = rhs_ref[...]
        for l_ref, acc in zip(lhs_refs, accs):
            _accumulate(acc, _dot_tn(l_ref[...], rv), s == 0)

        @pl.when(s == n_t - 1)
        def _():
            for o_ref, acc in zip(out_refs, accs):
                o_ref[...] = acc[...].astype(BF16)

    act = pl.BlockSpec((tt, tf), lambda f, s: (s, f))
    tok = pl.BlockSpec((tt, d), lambda f, s: (s, 0))
    wblk = pl.BlockSpec((tf, d), lambda f, s: (f, 0))
    res = pl.pallas_call(
        body, name=name, grid=(f_all // tf, n_t),
        out_shape=[jax.ShapeDtypeStruct((f_all, d), BF16)] * n_l + c_shapes,
        in_specs=[act] * n_l + [tok] + c_in_specs, out_specs=[wblk] * n_l + c_out_specs,
        scratch_shapes=[pltpu.VMEM((tf, d), F32)] * n_l + c_scratch,
        compiler_params=_cp(("arbitrary", "arbitrary")),
    )(*lhs_list, rhs, *c_inputs)
    return res[:n_l], res[n_l:]


def _swap_halves(t):
    w = t.shape[-1]
    lane = lax.broadcasted_iota(jnp.int32, (1, w), 1)
    return jnp.where((lane % HEAD_DIM) < HEAD_DIM // 2, pltpu.roll(t, w - HEAD_DIM // 2, 1), pltpu.roll(t, HEAD_DIM // 2, 1))


def _rope(t, cos, sin_signed):
    reps = t.shape[-1] // LANES
    return t * jnp.tile(cos, (1, reps)) + _swap_halves(t) * jnp.tile(sin_signed, (1, reps))


def _rope_bwd(dt, cos, sin_signed):
    reps = dt.shape[-1] // LANES
    return dt * jnp.tile(cos, (1, reps)) + _swap_halves(dt * jnp.tile(sin_signed, (1, reps)))


DILATIONS = tuple(dil for _, dil in B_PATTERNS if dil > 1)


def _seg_shape(t, dil, w, dtype):
    return jax.ShapeDtypeStruct((dil, t // dil, w), dtype)


def _seg_spec(tm, dil, w):
    return pl.BlockSpec((dil, tm // dil, w), lambda i: (0, i, 0))


def _tile_scratch(tm, w):
    return [pltpu.VMEM((tm, LANES), F32)] * (w // LANES)


def _put_tile(tile, val):
    for c, ref in enumerate(tile):
        ref[...] = val[:, c * LANES:(c + 1) * LANES]


def _get_tile(tile):
    return jnp.concatenate([ref[...] for ref in tile], axis=1)


def _scatter_to_segments(tile, seg_refs):
    for seg_ref, dil in zip(seg_refs, DILATIONS):
        rows = tile[0].shape[0] // dil
        for r in range(dil):
            for c, ref in enumerate(tile):
                seg_ref[r, :, c * LANES:(c + 1) * LANES] = ref[pl.ds(r, rows, stride=dil), :].astype(seg_ref.dtype)


def _gather_from_segments(seg_ref, dil, tile, add=False):
    rows = tile[0].shape[0] // dil
    for r in range(dil):
        for c, ref in enumerate(tile):
            idx = (pl.ds(r, rows, stride=dil), slice(None))
            v = seg_ref[r, :, c * LANES:(c + 1) * LANES].astype(F32)
            ref[idx] = ref[idx] + v if add else v


def _in_proj_fwd(x, gn, win_t, cos, sin_signed, name, tm):
    t, d = x.shape
    in_w = win_t.shape[0]
    n_dil = len(DILATIONS)

    def body(x_ref, gn_ref, w_ref, cos_ref, sin_ref, h_ref, aq_ref, akx_ref, avx_ref, bq_ref, bk_ref, bv_ref, *rest):
        seg_refs, tile = rest[:3 * n_dil], rest[3 * n_dil:]
        xv = x_ref[...]
        h = ((xv * _rstd(xv)) * gn_ref[...]).astype(BF16)
        h_ref[...] = h
        p = _dot_nt(h, w_ref[...])
        cs, sn = cos_ref[...], sin_ref[...]
        o = 0
        aq_ref[...] = (_rope(p[:, o:o + A_Q_W], cs, sn) * QK_SCALE).astype(BF16)
        o += A_Q_W
        ak = _rope(p[:, o:o + A_KV_W], cs, sn)
        o += A_KV_W
        av = p[:, o:o + A_KV_W]
        o += A_KV_W
        low = lax.broadcasted_iota(jnp.int32, (1, LANES), 1) < HEAD_DIM
        for src, dst in ((ak, akx_ref), (av, avx_ref)):
            other = pltpu.roll(src, HEAD_DIM, 1)
            dst[0] = jnp.where(low, src, other).astype(BF16)
            dst[1] = jnp.where(low, other, src).astype(BF16)
        for k, nat_ref in enumerate((bq_ref, bk_ref, bv_ref)):
            val = p[:, o:o + B_W]
            o += B_W
            if k < 2:
                val = _rope(val, cs, sn)
            if k == 0:
                val = val * QK_SCALE
            nat_ref[...] = val.astype(BF16)
            _put_tile(tile, val)
            _scatter_to_segments(tile, seg_refs[k * n_dil:(k + 1) * n_dil])

    tok = lambda w: pl.BlockSpec((tm, w), lambda i: (i, 0))
    kvx = pl.BlockSpec((2, tm, LANES), lambda i: (0, i, 0))
    sd = lambda *s: jax.ShapeDtypeStruct(s, BF16)
    res = pl.pallas_call(
        body, name=name, grid=(t // tm,),
        out_shape=[sd(t, d), sd(t, A_Q_W), sd(2, t, LANES), sd(2, t, LANES), sd(t, B_W), sd(t, B_W), sd(t, B_W)]
        + [_seg_shape(t, dil, B_W, BF16) for _ in range(3) for dil in DILATIONS],
        in_specs=[tok(d), pl.BlockSpec((1, d), lambda i: (0, 0)), pl.BlockSpec((in_w, d), lambda i: (0, 0)),
                  tok(LANES), tok(LANES)],
        out_specs=[tok(d), tok(A_Q_W), kvx, kvx, tok(B_W), tok(B_W), tok(B_W)]
        + [_seg_spec(tm, dil, B_W) for _ in range(3) for dil in DILATIONS],
        scratch_shapes=_tile_scratch(tm, B_W),
        compiler_params=_cp(("arbitrary",)),
    )(x, gn, win_t, cos, sin_signed)
    return res[:7], [res[7 + k * n_dil:7 + (k + 1) * n_dil] for k in range(3)]


def _in_proj_bwd(dres, x, gn, win_t, h, cos, sin_signed, daq, dakx, davx, dbq, dbk, dbv, name, tm):
    t, d = x.shape
    in_w = win_t.shape[0]
    n_t = t // tm
    chunk = in_w // 3

    def body(dres_ref, x_ref, gn_ref, w_ref, h_ref, cos_ref, sin_ref, daq_ref, dakx_ref, davx_ref, dbq_ref, dbk_ref,
             dbv_ref, dx_ref, dgn_ref, gw_ref, half_ref, dp_s, gw_s):
        i = pl.program_id(0)
        cs, sn = cos_ref[...], sin_ref[...]
        low = lax.broadcasted_iota(jnp.int32, (1, LANES), 1) < HEAD_DIM

        def fold(ref):
            a, b = ref[0].astype(F32), ref[1].astype(F32)
            return jnp.where(low, a + pltpu.roll(a, HEAD_DIM, 1), b + pltpu.roll(b, HEAD_DIM, 1))

        o = 0
        dp_s[:, o:o + A_Q_W] = _rope_bwd(daq_ref[...].astype(F32) * QK_SCALE, cs, sn).astype(BF16)
        o += A_Q_W
        dp_s[:, o:o + A_KV_W] = _rope_bwd(fold(dakx_ref), cs, sn).astype(BF16)
        o += A_KV_W
        dp_s[:, o:o + A_KV_W] = fold(davx_ref).astype(BF16)
        o += A_KV_W
        dp_s[:, o:o + B_W] = _rope_bwd(dbq_ref[...].astype(F32) * QK_SCALE, cs, sn).astype(BF16)
        o += B_W
        dp_s[:, o:o + B_W] = _rope_bwd(dbk_ref[...].astype(F32), cs, sn).astype(BF16)
        o += B_W
        dp_s[:, o:o + B_W] = dbv_ref[...].astype(BF16)
        dh = _dot_nn(dp_s[...], w_ref[...])
        hv = h_ref[...]
        for c0 in range(0, in_w, chunk):
            _accumulate(gw_s.at[pl.ds(c0, chunk), :], _dot_tn(dp_s[:, c0:c0 + chunk], hv), i == 0)
        xv = x_ref[...]
        dxn, dgn = _norm_bwd(dh, xv, _rstd(xv), gn_ref[...])
        dx = dres_ref[...] + dxn
        dx_ref[...] = dx
        half_ref[...] = (FFN_RES_WEIGHT * dx).astype(BF16)
        _accumulate(dgn_ref, dgn, i == 0)

        @pl.when(i == n_t - 1)
        def _():
            gw_ref[...] = gw_s[...].astype(BF16)

    tok = lambda w: pl.BlockSpec((tm, w), lambda i: (i, 0))
    row = pl.BlockSpec((1, d), lambda i: (0, 0))
    whole = pl.BlockSpec((in_w, d), lambda i: (0, 0))
    kvx = pl.BlockSpec((2, tm, LANES), lambda i: (0, i, 0))
    return pl.pallas_call(
        body, name=name, grid=(n_t,),
        out_shape=[jax.ShapeDtypeStruct((t, d), F32), jax.ShapeDtypeStruct((1, d), F32),
                   jax.ShapeDtypeStruct((in_w, d), BF16), jax.ShapeDtypeStruct((t, d), BF16)],
        in_specs=[tok(d), tok(d), row, whole, tok(d), tok(LANES), tok(LANES), tok(A_Q_W), kvx, kvx,
                  tok(B_W), tok(B_W), tok(B_W)],
        out_specs=[tok(d), row, whole, tok(d)],
        scratch_shapes=[pltpu.VMEM((tm, in_w), BF16), pltpu.VMEM((in_w, d), F32)],
        compiler_params=_cp(("arbitrary",)),
    )(dres, x, gn, win_t, h, cos, sin_signed, daq, dakx, davx, dbq, dbk, dbv)


def _merge_out_proj_fwd(x, a_out, outs, lses, wout, name, tm):
    t, d = x.shape
    n_dil = len(DILATIONS)

    def body(x_ref, a_ref, *rest):
        o_refs, l_refs, (w_ref, y_ref, b_ref, lt_ref), b_segs, lt_segs, scratch = _split(rest, 1 + n_dil, 1 + n_dil, 4, n_dil, n_dil)
        n_c = B_W // LANES
        tiles = [scratch[j * n_c:(j + 1) * n_c] for j in range(2 * n_dil)]
        os_, ls = [o_refs[0][...].astype(F32)], [l_refs[0][...]]
        for k, dil in enumerate(DILATIONS):
            _gather_from_segments(o_refs[1 + k], dil, tiles[2 * k])
            _gather_from_segments(l_refs[1 + k], dil, tiles[2 * k + 1])
            os_.append(_get_tile(tiles[2 * k]))
            ls.append(_get_tile(tiles[2 * k + 1]))
        mx = functools.reduce(jnp.maximum, ls)
        es = [jnp.exp(l - mx) for l in ls]
        den = functools.reduce(jnp.add, es)
        b = functools.reduce(jnp.add, [e * o for e, o in zip(es, os_)]) / den
        lt = mx + jnp.log(den)
        bb = b.astype(BF16)
        b_ref[...] = bb
        lt_ref[...] = lt
        y_ref[...] = x_ref[...] + _dot_nn(a_ref[...], w_ref[0:A_Q_W, :]) + _dot_nn(bb, w_ref[A_Q_W:A_Q_W + B_W, :])
        _put_tile(tiles[0], b)
        _scatter_to_segments(tiles[0], b_segs)
        _put_tile(tiles[1], lt)
        _scatter_to_segments(tiles[1], lt_segs)

    tok = lambda w: pl.BlockSpec((tm, w), lambda i: (i, 0))
    segs = [_seg_spec(tm, dil, B_W) for dil in DILATIONS]
    res = pl.pallas_call(
        body, name=name, grid=(t // tm,),
        out_shape=[jax.ShapeDtypeStruct((t, d), F32), jax.ShapeDtypeStruct((t, B_W), BF16), jax.ShapeDtypeStruct((t, B_W), F32)]
        + [_seg_shape(t, dil, B_W, BF16) for dil in DILATIONS] + [_seg_shape(t, dil, B_W, F32) for dil in DILATIONS],
        in_specs=[tok(d), tok(A_Q_W)] + ([tok(B_W)] + segs) * 2 + [pl.BlockSpec(wout.shape, lambda i: (0, 0))],
        out_specs=[tok(d), tok(B_W), tok(B_W)] + segs * 2,
        scratch_shapes=_tile_scratch(tm, B_W) * (2 * n_dil),
        compiler_params=_cp(("arbitrary",)),
    )(x, a_out, *outs, *lses, wout)
    return res[0], [res[1]] + list(res[3:3 + n_dil]), [res[2]] + list(res[3 + n_dil:])


def _out_proj_bwd(dy, a_out, b_out, wout, name, tm):
    t, d = dy.shape
    n_t = t // tm
    n_dil = len(DILATIONS)

    def body(dy_ref, a_ref, b_ref, w_ref, da_ref, db_ref, gw_ref, *rest):
        db_segs, gw_s, tile = rest[:n_dil], rest[n_dil], rest[n_dil + 1:]
        i = pl.program_id(0)
        dyb = dy_ref[...].astype(BF16)
        da_ref[...] = _dot_nt(dyb, w_ref[0:A_Q_W, :]).astype(BF16)
        db = _dot_nt(dyb, w_ref[A_Q_W:A_Q_W + B_W, :])
        db_ref[...] = db.astype(BF16)
        _put_tile(tile, db)
        _scatter_to_segments(tile, db_segs)
        ga = _dot_tn(a_ref[...], dyb)
        gb = _dot_tn(b_ref[...], dyb)

        @pl.when(i == 0)
        def _():
            gw_s[0:A_Q_W, :] = ga
            gw_s[A_Q_W:A_Q_W + B_W, :] = gb

        @pl.when(i > 0)
        def _():
            gw_s[0:A_Q_W, :] += ga
            gw_s[A_Q_W:A_Q_W + B_W, :] += gb

        @pl.when(i == n_t - 1)
        def _():
            gw_ref[...] = gw_s[...].astype(BF16)

    tok = lambda w: pl.BlockSpec((tm, w), lambda i: (i, 0))
    whole = pl.BlockSpec(wout.shape, lambda i: (0, 0))
    res = pl.pallas_call(
        body, name=name, grid=(n_t,),
        out_shape=[jax.ShapeDtypeStruct((t, A_Q_W), BF16), jax.ShapeDtypeStruct((t, B_W), BF16),
                   jax.ShapeDtypeStruct(wout.shape, BF16)] + [_seg_shape(t, dil, B_W, BF16) for dil in DILATIONS],
        in_specs=[tok(d), tok(A_Q_W), tok(B_W), whole],
        out_specs=[tok(A_Q_W), tok(B_W), whole] + [_seg_spec(tm, dil, B_W) for dil in DILATIONS],
        scratch_shapes=[pltpu.VMEM(wout.shape, F32)] + _tile_scratch(tm, B_W),
        compiler_params=_cp(("arbitrary",)),
    )(dy, a_out, b_out, wout)
    return res[0], [res[1]] + list(res[3:]), res[2]


def _sum_pattern_grads(per_pattern, name, tm):
    t = per_pattern[0][0].shape[0]
    n_dil = len(DILATIONS)

    def body(*refs):
        ins, outs, tile = _split(refs, 3 * (1 + n_dil), 3)
        for j, o_ref in enumerate(outs):
            _put_tile(tile, ins[j][...].astype(F32))
            for k, dil in enumerate(DILATIONS):
                _gather_from_segments(ins[3 * (1 + k) + j], dil, tile, add=True)
            o_ref[...] = _get_tile(tile).astype(BF16)

    tok = pl.BlockSpec((tm, B_W), lambda i: (i, 0))
    flat = [a if k == 0 else a.reshape(DILATIONS[k - 1], -1, B_W) for k, grads in enumerate(per_pattern) for a in grads]
    return pl.pallas_call(
        body, name=name, grid=(t // tm,),
        out_shape=[jax.ShapeDtypeStruct((t, B_W), BF16)] * 3,
        in_specs=[tok] * 3 + [_seg_spec(tm, dil, B_W) for dil in DILATIONS for _ in range(3)],
        out_specs=[tok] * 3, scratch_shapes=_tile_scratch(tm, B_W),
        compiler_params=_cp(("arbitrary",)),
    )(*flat)


SUB_ROWS = 64


def _sub_band(r0, hw, win, seg_lo, seg_len, t, rel, col):
    ks = pl.multiple_of(jnp.clip(r0 - hw, 0, t - win), SUB_ROWS)
    kpos = col + ks
    valid = (jnp.abs(rel + (ks - r0)) <= hw) & (kpos >= seg_lo) & (kpos < seg_lo + seg_len)
    return ks, valid


def _split_heads(v, low):
    zero = jnp.zeros_like(v)
    return jnp.concatenate([jnp.where(low, v, zero), jnp.where(low, zero, v)], axis=0)


def _kv_spec(kv, t):
    if kv.ndim == 3:
        return pl.BlockSpec((None, t, LANES), lambda p, i: (p // 2, 0, 0))
    return pl.BlockSpec((t, LANES), lambda p, i: (0, p))


def _attn_fwd(q, k, v, sink, name, hw, seg_len, tq, has_sink):
    t, width = q.shape
    sb = SUB_ROWS
    win = 2 * hw + LANES

    def body(sink_ref, q_ref, k_ref, v_ref, o_ref, lse_ref):
        p, i = pl.program_id(0), pl.program_id(1)
        q0 = i * tq
        seg_lo = (q0 // seg_len) * seg_len
        low = lax.broadcasted_iota(jnp.int32, (1, LANES), 1) < HEAD_DIM
        rel = lax.broadcasted_iota(jnp.int32, (sb, win), 1) - lax.broadcasted_iota(jnp.int32, (sb, win), 0)
        col = lax.broadcasted_iota(jnp.int32, (1, win), 1)
        subs = []
        for j in range(tq // sb):
            rows = pl.ds(j * sb, sb)
            ks, valid = _sub_band(q0 + j * sb, hw, win, seg_lo, seg_len, t, rel, col)
            subs.append((rows, ks, valid, _dot_nt(_split_heads(q_ref[rows, :], low), k_ref[pl.ds(ks, win), :])))
        for rows, ks, valid, s in subs:
            vw = v_ref[pl.ds(ks, win), :]
            es, inv, lses = [], [], []
            for a in range(2):
                sa = jnp.where(valid, s[a * sb:(a + 1) * sb], NEG)
                m = jnp.max(sa, axis=1, keepdims=True)
                if has_sink:
                    sk = sink_ref[2 * p + a]
                    m = jnp.maximum(m, sk)
                e = jnp.exp(sa - m)
                den = jnp.sum(e, axis=1, keepdims=True)
                if has_sink:
                    den = den + jnp.exp(sk - m)
                es.append(e.astype(BF16))
                inv.append(1.0 / den)
                lses.append(m + jnp.log(den))
            pv = _dot_nn(jnp.concatenate(es, axis=0), vw)
            o_ref[rows, :] = jnp.where(low, pv[0:sb] * inv[0], pv[sb:2 * sb] * inv[1]).astype(BF16)
            lse_ref[rows, :] = jnp.where(low, lses[0], lses[1])

    tile = pl.BlockSpec((tq, LANES), lambda p, i: (i, p))
    return pl.pallas_call(
        body, name=name, grid=(width // LANES, t // tq),
        out_shape=[jax.ShapeDtypeStruct((t, width), BF16), jax.ShapeDtypeStruct((t, width), F32)],
        in_specs=[pl.BlockSpec(memory_space=pltpu.SMEM), tile, _kv_spec(k, t), _kv_spec(v, t)],
        out_specs=[tile, tile],
        compiler_params=_cp(("arbitrary", "arbitrary")),
    )(sink, q, k, v)


def _attn_bwd(q, k, v, o, do, lse, sink, name, hw, seg_len, tq, has_sink, comm=None, comm_at=None):
    t, width = q.shape
    sb = SUB_ROWS
    win = 2 * hw + LANES
    n_q = t // tq
    shared_kv = k.ndim == 3
    n_ci, n_co, c_in_specs, c_out_specs, c_shapes, c_scratch, c_inputs = _comm_parts(comm)

    def body(*refs):
        ((sink_ref, q_ref, k_ref, v_ref, o_ref, do_ref, lse_ref), c_in, (dq_ref, dk_ref, dv_ref, ds_ref), c_out,
         (dk_s, dv_s), c_scr) = _split(refs, 7, n_ci, 4, n_co, 2)
        p, i = pl.program_id(0), pl.program_id(1)
        if comm:
            _run_hosted(comm, comm_at, p * n_q + i, c_in, c_out, c_scr)
        fresh = (i == 0) & (p % 2 == 0) if shared_kv else i == 0
        last = (i == n_q - 1) & (p % 2 == 1) if shared_kv else i == n_q - 1

        @pl.when(fresh)
        def _():
            dk_s[...] = jnp.zeros_like(dk_s)
            dv_s[...] = jnp.zeros_like(dv_s)

        q0 = i * tq
        seg_lo = (q0 // seg_len) * seg_len
        low = lax.broadcasted_iota(jnp.int32, (1, LANES), 1) < HEAD_DIM
        rel = lax.broadcasted_iota(jnp.int32, (sb, win), 1) - lax.broadcasted_iota(jnp.int32, (sb, win), 0)
        col = lax.broadcasted_iota(jnp.int32, (1, win), 1)
        dsink = [jnp.zeros((1, 1), F32), jnp.zeros((1, 1), F32)]
        subs = []
        for j in range(tq // sb):
            rows = pl.ds(j * sb, sb)
            ks, valid = _sub_band(q0 + j * sb, hw, win, seg_lo, seg_len, t, rel, col)
            kw = k_ref[pl.ds(ks, win), :]
            dov = do_ref[rows, :]
            q2 = _split_heads(q_ref[rows, :], low)
            do2 = _split_heads(dov, low)
            subs.append((rows, ks, valid, kw, dov, q2, do2, _dot_nt(q2, kw), _dot_nt(do2, v_ref[pl.ds(ks, win), :])))
        probs = []
        for rows, ks, valid, kw, dov, q2, do2, s, dpr in subs:
            prod = dov.astype(F32) * o_ref[rows, :].astype(F32)
            lse_t = lse_ref[rows, :]
            prs, dss = [], []
            for a in range(2):
                mine = low if a == 0 else jnp.logical_not(low)
                lse_a = jnp.max(jnp.where(mine, lse_t, -jnp.inf), axis=1, keepdims=True)
                delta = jnp.sum(jnp.where(mine, prod, 0.0), axis=1, keepdims=True)
                pr = jnp.exp(jnp.where(valid, s[a * sb:(a + 1) * sb], NEG) - lse_a)
                prs.append(pr.astype(BF16))
                dss.append((pr * (dpr[a * sb:(a + 1) * sb] - delta)).astype(BF16))
                if has_sink:
                    dsink[a] = dsink[a] - jnp.sum(jnp.exp(sink_ref[2 * p + a] - lse_a) * delta, axis=0, keepdims=True)
            probs.append((jnp.concatenate(prs, axis=0), jnp.concatenate(dss, axis=0)))
        for (rows, ks, valid, kw, dov, q2, do2, s, dpr), (pr2, ds2) in zip(subs, probs):
            dv_s[pl.ds(ks, win), :] += _dot_tn(pr2, do2)
            dk_s[pl.ds(ks, win), :] += _dot_tn(ds2, q2)
            dq2 = _dot_nn(ds2, kw)
            dq_ref[rows, :] = jnp.where(low, dq2[0:sb], dq2[sb:2 * sb]).astype(BF16)
        ds_ref[...] = jnp.broadcast_to(jnp.where(low, dsink[0], dsink[1]), ds_ref.shape)

        @pl.when(last)
        def _():
            dk_ref[...] = dk_s[...].astype(BF16)
            dv_ref[...] = dv_s[...].astype(BF16)

    tile = pl.BlockSpec((tq, LANES), lambda p, i: (i, p))
    kv_shape = jax.ShapeDtypeStruct(k.shape, BF16)
    res = pl.pallas_call(
        body, name=name, grid=(width // LANES, n_q),
        out_shape=[jax.ShapeDtypeStruct((t, width), BF16), kv_shape, kv_shape,
                   jax.ShapeDtypeStruct((width // LANES, n_q, 8, LANES), F32)] + c_shapes,
        in_specs=[pl.BlockSpec(memory_space=pltpu.SMEM), tile, _kv_spec(k, t), _kv_spec(v, t), tile, tile, tile] + c_in_specs,
        out_specs=[tile, _kv_spec(k, t), _kv_spec(v, t), pl.BlockSpec((None, None, 8, LANES), lambda p, i: (p, i, 0, 0))] + c_out_specs,
        scratch_shapes=[pltpu.VMEM((t, LANES), F32)] * 2 + c_scratch,
        compiler_params=_cp(("arbitrary", "arbitrary")),
    )(sink, q, k, v, o, do, lse, *c_inputs)
    return res[:4], res[4:]


def _final_norm_loss(x, gn, target, name, tm):
    t, d = x.shape

    def body(x_ref, gn_ref, tg_ref, dx_ref, dgn_ref, sq_ref, half_ref):
        i = pl.program_id(0)
        xv, gnv = x_ref[...], gn_ref[...]
        r = _rstd(xv)
        diff = (xv * r) * gnv - tg_ref[...]
        dxn, dgn = _norm_bwd(diff * (1.0 / d), xv, r, gnv)
        dx_ref[...] = dxn
        half_ref[...] = (FFN_RES_WEIGHT * dxn).astype(BF16)
        _accumulate(dgn_ref, dgn, i == 0)
        _accumulate(sq_ref, jnp.sum(diff * diff, axis=0, keepdims=True), i == 0)

    tok = pl.BlockSpec((tm, d), lambda i: (i, 0))
    row = pl.BlockSpec((1, d), lambda i: (0, 0))
    return pl.pallas_call(
        body, name=name, grid=(t // tm,),
        out_shape=[jax.ShapeDtypeStruct((t, d), F32), jax.ShapeDtypeStruct((1, d), F32), jax.ShapeDtypeStruct((1, d), F32),
                   jax.ShapeDtypeStruct((t, d), BF16)],
        in_specs=[tok, row, tok], out_specs=[tok, row, row, tok],
        compiler_params=_cp(("arbitrary",)),
    )(x, gn, target)


def _adamw(w, g, m, v, name):
    def body(w_ref, g_ref, m_ref, v_ref, d_ref, nm_ref, nv_ref):
        gv = g_ref[...]
        nm = ADAM_B1 * m_ref[...] + (1.0 - ADAM_B1) * gv
        nv = ADAM_B2 * v_ref[...] + (1.0 - ADAM_B2) * (gv * gv)
        m_hat = nm / (1.0 - ADAM_B1 ** ADAM_STEP)
        v_hat = nv / (1.0 - ADAM_B2 ** ADAM_STEP)
        d_ref[...] = -ADAM_LR * (m_hat / (jnp.sqrt(v_hat) + ADAM_EPS) + ADAM_WD * w_ref[...])
        nm_ref[...] = nm
        nv_ref[...] = nv

    shape = jax.ShapeDtypeStruct(w.shape, F32)
    return pl.pallas_call(body, name=name, out_shape=[shape, shape, shape], compiler_params=_cp())(w, g, m, v)


def _adamw_rows(land, j, rows, w, m, v, name):
    d = w.shape[1]

    def body(l_ref, w_ref, m_ref, v_ref, g_ref, d_ref, nm_ref, nv_ref):
        gv = l_ref[0].astype(F32)
        for s in range(1, N_DEV):
            gv = gv + l_ref[s].astype(F32)
        g_ref[...] = gv
        nm = ADAM_B1 * m_ref[...] + (1.0 - ADAM_B1) * gv
        nv = ADAM_B2 * v_ref[...] + (1.0 - ADAM_B2) * (gv * gv)
        m_hat = nm / (1.0 - ADAM_B1 ** ADAM_STEP)
        v_hat = nv / (1.0 - ADAM_B2 ** ADAM_STEP)
        d_ref[...] = -ADAM_LR * (m_hat / (jnp.sqrt(v_hat) + ADAM_EPS) + ADAM_WD * w_ref[...])
        nm_ref[...] = nm
        nv_ref[...] = nv

    halves = 2
    half = pl.BlockSpec((rows // halves, d), lambda i: (i, 0))
    shape = jax.ShapeDtypeStruct((rows, d), F32)
    return pl.pallas_call(
        body, name=name, grid=(halves,), out_shape=[shape] * 4,
        in_specs=[pl.BlockSpec((N_DEV, rows // halves, d), lambda i: (0, halves * j + i, 0)), half, half, half],
        out_specs=[half] * 4,
        compiler_params=_cp(("arbitrary",)),
    )(land, w, m, v)


def _sum_small(land, name):
    def body(l_ref, o_ref):
        acc = l_ref[0]
        for s in range(1, N_DEV):
            acc = acc + l_ref[s]
        o_ref[...] = acc

    return pl.pallas_call(body, name=name, out_shape=jax.ShapeDtypeStruct(land.shape[1:], F32), compiler_params=_cp())(land)


def _rope_lanes(positions):
    inv_freq = 1.0 / (ROPE_THETA ** (jnp.arange(0, HEAD_DIM, 2, dtype=F32) / HEAD_DIM))
    ang = positions.astype(F32)[:, None] * inv_freq
    cos, sin = jnp.cos(ang), jnp.sin(ang)
    return jnp.concatenate([cos, cos, cos, cos], axis=1), jnp.concatenate([-sin, sin, -sin, sin], axis=1)


def kernel(x, positions, norm_ffn1, w_gate1, w_up1, w_down1, norm_mix, w_in, a_sink, w_out, norm_ffn2, w_gate2, w_up2, w_down2, norm_final, loss_target, m_norm_ffn1, m_w_gate1, m_w_up1, m_w_down1, m_norm_mix, m_w_in, m_a_sink, m_w_out, m_norm_ffn2, m_w_gate2, m_w_up2, m_w_down2, m_norm_final, v_norm_ffn1, v_w_gate1, v_w_up1, v_w_down1, v_norm_mix, v_w_in, v_a_sink, v_w_out, v_norm_ffn2, v_w_gate2, v_w_up2, v_w_down2, v_norm_final):
    x = x[0]
    target = loss_target[0]
    t, d = x.shape
    tm = min(1024, t)
    tm_mix = min(512, t)
    tq = min(512, t // 16)

    tt = min(1024, t)
    f_all = w_gate1.shape[2] * N_DEV
    tfp = f_all // 2
    tm_fwd, tf_fwd = tm_mix, tfp
    n_steps = (t // tm_fwd) * (f_all // tf_fwd)

    def stacked(shards):
        return jnp.concatenate([s.astype(BF16) for s in shards], axis=0), [s.shape[0] for s in shards]

    packed1, rows1 = stacked([w_gate1[0].T, w_up1[0].T, w_down1[0]])
    packed2, rows2 = stacked([w_in[0].T, w_out[0], w_gate2[0].T, w_up2[0].T, w_down2[0]])
    wg1, wu1, wd1 = _run_alone(_gather_plan(packed1, rows1), "gather_ffn1_weights")

    cos, sin_signed = _rope_lanes(positions[0])
    sink = a_sink[0]
    no_sink = jnp.zeros_like(sink)

    (x1, h1, gp1, up1, hid1), (win, wout, wg2, wu2, wd2) = _ffn_fwd(
        x, norm_ffn1, wg1, wu1, wd1, "ffn1_fwd", tm_fwd, tf_fwd, _gather_plan(packed2, rows2), [0, (3 * n_steps) // 4, n_steps - 1])
    (h_mix, aq, akx, avx, bq, bk, bv), seg_qkv = _in_proj_fwd(x1, norm_mix, win, cos, sin_signed, "in_proj_fwd", tm_mix)
    a_out, a_lse = _attn_fwd(aq, akx, avx, sink, "attn_a_fwd", A_HALF_WINDOW, t, tq, True)
    rows_of = lambda a: a.reshape(t, B_W)
    segments_of = lambda a, dil: a if dil == 1 else a.reshape(dil, t // dil, B_W)
    b_qkv, b_outs, b_lses = [], [], []
    for n, (window, dil) in enumerate(B_PATTERNS):
        qs, ks, vs = (bq, bk, bv) if dil == 1 else (rows_of(seg_qkv[k][n - 1]) for k in range(3))
        o_seg, lse_seg = _attn_fwd(qs, ks, vs, no_sink, f"attn_b{dil}_fwd", window // (2 * dil), t // dil, tq, False)
        b_qkv.append((qs, ks, vs))
        b_outs.append(segments_of(o_seg, dil))
        b_lses.append(segments_of(lse_seg, dil))
    x2, b_out, b_lse = _merge_out_proj_fwd(x1, a_out, b_outs, b_lses, wout, "out_proj_fwd", tm_mix)
    (x3, h2, gp2, up2, hid2), _ = _ffn_fwd(x2, norm_ffn2, wg2, wu2, wd2, "ffn2_fwd", tm_fwd, tf_fwd)

    gfinal = norm_final.reshape(1, d)
    dx3, dg_final, sq, dout2 = _final_norm_loss(x3, gfinal, target, "final_norm_loss", tm)

    (dgt2, dut2, gwd2), _ = _ffn_bwd_hidden(dout2, gp2, up2, hid2, wd2, "ffn2_bwd_hidden", tm_mix, tfp)
    (dx2, dg_ffn2), _ = _ffn_bwd_input(dx3, x2, norm_ffn2, dgt2, dut2, wg2, wu2, "ffn2_bwd_input", tm_mix)
    (gwg2, gwu2), _ = _token_products([dgt2, dut2], h2, "ffn2_bwd_gate_up", tt, tfp)

    n_att = (A_Q_W // LANES) * (t // tq)
    n_prod = (f_all // tfp) * (t // tt)
    da, db, gwout = _out_proj_bwd(dx2, a_out, b_out[0], wout, "out_proj_bwd", tm_mix)
    (daq, dakx, davx, dsink_parts), (land_wg2,) = _attn_bwd(
        aq, akx, avx, a_out, da, a_lse, sink, "attn_a_bwd", A_HALF_WINDOW, t, tq, True, _exchange_plan([[gwg2]]), [0, n_att - 1])
    pattern_grads, pattern_lands = [], []
    for n, ((window, dil), (qs, ks, vs)) in enumerate(zip(B_PATTERNS, b_qkv)):
        carried = [gwu2, gwd2][n:n + 1]
        grads, lands = _attn_bwd(qs, ks, vs, rows_of(b_out[n]), rows_of(db[n]), rows_of(b_lse[n]), no_sink, f"attn_b{dil}_bwd",
                                 window // (2 * dil), t // dil, tq, False,
                                 _exchange_plan([carried]) if carried else None, [0, (B_W // LANES) * (t // tq) - 1])
        pattern_grads.append(grads[:3])
        pattern_lands.extend(lands)
    land_wu2, land_wd2 = pattern_lands
    dbq, dbk, dbv = _sum_pattern_grads(pattern_grads, "sum_pattern_grads", tm_mix)
    dx1, dg_mix, gwin, dout1 = _in_proj_bwd(dx2, x1, norm_mix, win, h_mix, cos, sin_signed, daq, dakx, davx, dbq, dbk, dbv, "in_proj_bwd", tm_mix)

    (dgt1, dut1, gwd1), (land_in, land_out) = _ffn_bwd_hidden(
        dout1, gp1, up1, hid1, wd1, "ffn1_bwd_hidden", tm_mix, tfp, _exchange_plan([[gwin], [gwout]]),
        [0, (f_all // tfp) * (t // tm_mix) - 1])
    (gwg1,), (land_wd1,) = _token_products([dgt1], h1, "ffn1_bwd_gate", tt, tfp, _exchange_plan([[gwd1]]), [0, n_prod - 1])
    (gwu1,), (land_wg1,) = _token_products([dut1], h1, "ffn1_bwd_up", tt, tfp, _exchange_plan([[gwg1]]), [0, n_prod - 1])
    (grad_x, dg_ffn1), (land_wu1,) = _ffn_bwd_input(
        dx1, x, norm_ffn1, dgt1, dut1, wg1, wu1, "ffn1_bwd_input", tm_mix, _exchange_plan([[gwu1]]), [0, t // tm_mix - 1])

    dsink_pairs = jnp.sum(dsink_parts[:, :, 0, :], axis=1)
    dsink = jnp.stack([dsink_pairs[:, 0], dsink_pairs[:, HEAD_DIM]], axis=1).reshape(1, -1)
    small = jnp.concatenate([dg_ffn1, dg_mix, dg_ffn2, dg_final, jnp.pad(dsink, ((0, 0), (0, d - dsink.shape[1]))),
                             sq, jnp.zeros((2, d), F32)], axis=0)
    (land_small,) = _run_alone(_exchange_plan([[jnp.tile(small, (N_DEV, 1))]]), "gather_small_gradients")
    red_small = _sum_small(land_small, "sum_small_grads")
    loss = 0.5 * jnp.sum(red_small[5]) / d

    rf = rows1[0]
    sharded = {"w_gate1": (land_wg1, 0, rf, True), "w_up1": (land_wu1, 0, rf, True), "w_down1": (land_wd1, 0, rf, False),
               "w_in": (land_in, 0, rows2[0], True), "w_out": (land_out, 0, rows2[1], False),
               "w_gate2": (land_wg2, 0, rf, True), "w_up2": (land_wu2, 0, rf, True), "w_down2": (land_wd2, 0, rf, False)}
    n_sink = a_sink.shape[1]
    small_grads = {"norm_ffn1": red_small[0:1], "norm_mix": red_small[1:2], "norm_ffn2": red_small[2:3], "norm_final": red_small[3],
                   "a_sink": red_small[4:5, :n_sink]}
    params = {
        "norm_ffn1": (norm_ffn1, m_norm_ffn1, v_norm_ffn1), "w_gate1": (w_gate1, m_w_gate1, v_w_gate1),
        "w_up1": (w_up1, m_w_up1, v_w_up1), "w_down1": (w_down1, m_w_down1, v_w_down1),
        "norm_mix": (norm_mix, m_norm_mix, v_norm_mix), "w_in": (w_in, m_w_in, v_w_in),
        "a_sink": (a_sink, m_a_sink, v_a_sink), "w_out": (w_out, m_w_out, v_w_out),
        "norm_ffn2": (norm_ffn2, m_norm_ffn2, v_norm_ffn2), "w_gate2": (w_gate2, m_w_gate2, v_w_gate2),
        "w_up2": (w_up2, m_w_up2, v_w_up2), "w_down2": (w_down2, m_w_down2, v_w_down2),
        "norm_final": (norm_final, m_norm_final, v_norm_final),
    }
    grad_list, deltas, new_ms, new_vs = [], [], [], []
    for name, (w, m, v) in params.items():
        if name in sharded:
            land, j, rows, is_transposed = sharded[name]
            view = (lambda a: a[0].T) if is_transposed else (lambda a: a[0])
            back = (lambda a: a.T[None]) if is_transposed else (lambda a: a[None])
            outs = [back(o) for o in _adamw_rows(land, j, rows, view(w), view(m), view(v), f"adamw_{name}")]
        else:
            as_block = (lambda a: a.reshape(1, -1)) if w.ndim == 1 else (lambda a: a)
            g = small_grads[name]
            outs = [g] + [o.reshape(w.shape) for o in _adamw(as_block(w), as_block(g), as_block(m), as_block(v), f"adamw_{name}")]
        for lst, o in zip((grad_list, deltas, new_ms, new_vs), outs):
            lst.append(o)
    return (loss, grad_x[None], *grad_list, *deltas, *new_ms, *new_vs)
```

```python
import functools
import itertools

import numpy as np
import jax
import jax.numpy as jnp
from jax import lax
from jax.experimental import pallas as pl
from jax.experimental.pallas import tpu as pltpu

F32 = jnp.float32
BF16 = jnp.bfloat16

N_DEV = 8
HEAD_DIM = 64
LANES = 128
A_Q_W, A_KV_W, B_W = 512, 128, 512
A_HALF_WINDOW = 128
B_PATTERNS = ((128, 1), (512, 4), (2048, 16))
ROPE_THETA = 10000.0
NORM_EPS = 1e-6
FFN_RES_WEIGHT = 0.5
QK_SCALE = HEAD_DIM ** -0.5
NEG = -1e30

ADAM_LR = 0.001
ADAM_B1 = 0.9
ADAM_B2 = 0.999
ADAM_EPS = 1e-08
ADAM_WD = 0.01
ADAM_STEP = 10

MESH_T = pl.DeviceIdType.MESH
VMEM_LIMIT = 56 * 1024 * 1024


def _cp(sem=None, vmem=VMEM_LIMIT):
    return pltpu.CompilerParams(dimension_semantics=sem, vmem_limit_bytes=vmem)


def _dot_nn(a, b):
    return jnp.dot(a, b, preferred_element_type=F32)


def _dot_nt(a, b):
    return lax.dot_general(a, b, (((1,), (1,)), ((), ())), preferred_element_type=F32)


def _dot_tn(a, b):
    return lax.dot_general(a, b, (((0,), (0,)), ((), ())), preferred_element_type=F32)


def _rstd(xv):
    return lax.rsqrt(jnp.mean(xv * xv, axis=-1, keepdims=True) + NORM_EPS)


def _norm_bwd(dh, xv, r, gn):
    gy = dh * gn
    c = jnp.sum(gy * xv, axis=-1, keepdims=True) * (1.0 / xv.shape[-1])
    dx = r * gy - xv * (r * r * r * c)
    dgn = jnp.sum(dh * (xv * r), axis=0, keepdims=True)
    return dx, dgn


def _accumulate(ref, val, first):
    @pl.when(first)
    def _():
        ref[...] = val

    @pl.when(jnp.logical_not(first))
    def _():
        ref[...] += val


def _mesh_pos():
    return lax.axis_index("x"), lax.axis_index("y"), lax.axis_index("c")


def _dev_index(d):
    return 4 * d[0] + 2 * d[1] + d[2]


class _Comm:
    def __init__(self, inputs, out_shape, scratch, phases):
        self.inputs, self.out_shape, self.scratch, self.phases = inputs, out_shape, scratch, phases

    def specs(self):
        any_spec = pl.BlockSpec(memory_space=pl.ANY)
        return [any_spec] * len(self.inputs), [any_spec] * len(self.out_shape)


def _run_alone(comm, name):
    n_in, n_out = len(comm.inputs), len(comm.out_shape)

    def body(*refs):
        for phase in comm.phases:
            phase(refs[:n_in], refs[n_in:n_in + n_out], refs[n_in + n_out:])

    in_specs, out_specs = comm.specs()
    return pl.pallas_call(body, name=name, out_shape=comm.out_shape, in_specs=in_specs, out_specs=out_specs,
                          scratch_shapes=comm.scratch)(*comm.inputs)


def _run_hosted(comm, at, step, ins, outs, scr):
    for phase, when in zip(comm.phases, at):
        @pl.when(step == when)
        def _(phase=phase):
            phase(ins, outs, scr)


def _split(refs, *counts):
    parts, o = [], 0
    for n in counts:
        parts.append(refs[o:o + n])
        o += n
    return parts + [refs[o:]]


def _gather_plan(packed, rows_list):
    n_w = len(rows_list)
    offs = [int(o) for o in np.cumsum([0] + list(rows_list[:-1]))]
    d = packed.shape[1]

    def tools(ins, outs, scr):
        p_ref = ins[0]
        send_sems, recv_sems, local_sem = scr
        x, y, c = _mesh_pos()
        me, sibling = (x, y, c), (x, y, 1 - c)
        chips = [(1 - x, y), (x, 1 - y), (1 - x, 1 - y)]

        def rows(w, dev):
            start = pl.multiple_of(_dev_index(dev) * rows_list[w], 16)
            return outs[w].at[pl.ds(start, rows_list[w]), :]

        def mine(w):
            return p_ref.at[pl.ds(offs[w], rows_list[w]), :]

        def copy(k, w, block, to, own):
            return pltpu.make_async_remote_copy(
                src_ref=mine(w) if own else rows(w, block), dst_ref=rows(w, block),
                send_sem=send_sems.at[k], recv_sem=recv_sems.at[k], device_id=to, device_id_type=MESH_T)

        def all_blocks(k):
            return pltpu.make_async_remote_copy(
                src_ref=p_ref, dst_ref=p_ref, send_sem=send_sems.at[k], recv_sem=recv_sems.at[k],
                device_id=me, device_id_type=MESH_T)

        return p_ref, local_sem, me, sibling, chips, c, rows, mine, copy, all_blocks

    def start(ins, outs, scr):
        _, local_sem, me, sibling, chips, c, rows, mine, copy, _ = tools(ins, outs, scr)
        for w in range(n_w):
            pltpu.make_async_copy(mine(w), rows(w, me), local_sem).start()
        for w in range(n_w):
            copy(0, w, me, sibling, True).start()
        for j, chip in enumerate(chips):
            for w in range(n_w):
                copy(1 + j, w, me, (*chip, c), True).start()

    def relay(ins, outs, scr):
        _, _, _, sibling, chips, c, _, _, copy, all_blocks = tools(ins, outs, scr)
        for j, chip in enumerate(chips):
            all_blocks(1 + j).wait_recv()
            for w in range(n_w):
                copy(4 + j, w, (*chip, c), sibling, False).start()

    def finish(ins, outs, scr):
        p_ref, local_sem, _, _, _, _, _, _, _, all_blocks = tools(ins, outs, scr)
        all_blocks(0).wait_recv()
        for j in range(3):
            all_blocks(4 + j).wait_recv()
        for k in range(7):
            all_blocks(k).wait_send()
        pltpu.make_async_copy(p_ref, p_ref, local_sem).wait()

    return _Comm(
        [packed], [jax.ShapeDtypeStruct((N_DEV * r, d), packed.dtype) for r in rows_list],
        [pltpu.SemaphoreType.DMA((7,)), pltpu.SemaphoreType.DMA((7,)), pltpu.SemaphoreType.DMA], [start, relay, finish])


def _exchange_plan(groups):
    flat = [a for g in groups for a in g]
    n_g = len(groups)
    sizes = [len(g) for g in groups]
    rows = [g[0].shape[0] // N_DEV for g in groups]
    first = [int(o) for o in np.cumsum([0] + sizes[:-1])]

    def start(srcs, lands, scr):
        send_sems, recv_sems, local_sems = scr
        x, y, c = _mesh_pos()
        me = (x, y, c)
        me_idx = _dev_index(me)

        def block(g, i, dev):
            start_row = pl.multiple_of(_dev_index(dev) * rows[g], 8)
            return srcs[first[g] + i].at[pl.ds(start_row, rows[g]), :]

        def slot(g, i):
            return lands[g].at[me_idx, pl.ds(i * rows[g], rows[g]), :]

        for g in range(n_g):
            for i in range(sizes[g]):
                pltpu.make_async_copy(block(g, i, me), slot(g, i), local_sems.at[g]).start()
        flips = [f for f in itertools.product((0, 1), repeat=3) if any(f)]
        for k, (fx, fy, fc) in enumerate(flips):
            peer = (1 - x if fx else x, 1 - y if fy else y, 1 - c if fc else c)
            for g in range(n_g):
                for i in range(sizes[g]):
                    pltpu.make_async_remote_copy(
                        src_ref=block(g, i, peer), dst_ref=slot(g, i), send_sem=send_sems.at[g, k],
                        recv_sem=recv_sems.at[g, k], device_id=peer, device_id_type=MESH_T).start()

    def finish(srcs, lands, scr):
        send_sems, recv_sems, local_sems = scr
        me = _mesh_pos()
        for k in range(7):
            for g in range(n_g):
                pltpu.make_async_remote_copy(
                    src_ref=lands[g].at[0], dst_ref=lands[g].at[0], send_sem=send_sems.at[g, k],
                    recv_sem=recv_sems.at[g, k], device_id=me, device_id_type=MESH_T).wait()
        for g in range(n_g):
            pltpu.make_async_copy(lands[g].at[0], lands[g].at[0], local_sems.at[g]).wait()

    return _Comm(
        flat, [jax.ShapeDtypeStruct((N_DEV, sizes[g] * rows[g], groups[g][0].shape[1]), groups[g][0].dtype) for g in range(n_g)],
        [pltpu.SemaphoreType.DMA((n_g, 7)), pltpu.SemaphoreType.DMA((n_g, 7)), pltpu.SemaphoreType.DMA((n_g,))],
        [start, finish])


def _comm_parts(comm):
    if comm is None:
        return 0, 0, [], [], [], [], []
    in_specs, out_specs = comm.specs()
    return len(comm.inputs), len(comm.out_shape), in_specs, out_specs, comm.out_shape, comm.scratch, comm.inputs


def _ffn_fwd(x, gn, wg_t, wu_t, wd, name, tm, tf, comm=None, comm_at=None):
    t, d = x.shape
    f_all = wg_t.shape[0]
    n_f = f_all // tf
    n_ci, n_co, c_in_specs, c_out_specs, c_shapes, c_scratch, c_inputs = _comm_parts(comm)

    def body(*refs):
        ((x_ref, gn_ref, wg_ref, wu_ref, wd_ref), c_in, (y_ref, h_ref, gp_ref, up_ref, hid_ref), c_out,
         (h_s, hid_s), c_scr) = _split(refs, 5, n_ci, 5, n_co, 2)
        f = pl.program_id(1)
        if comm:
            _run_hosted(comm, comm_at, pl.program_id(0) * n_f + f, c_in, c_out, c_scr)

        @pl.when(f == 0)
        def _():
            xv = x_ref[...]
            h = ((xv * _rstd(xv)) * gn_ref[...]).astype(BF16)
            h_s[...] = h
            h_ref[...] = h

        h = h_s[...]
        cols = pl.ds(pl.multiple_of(f * tf, tf), tf)
        g = _dot_nt(h, wg_ref[cols, :])
        u = _dot_nt(h, wu_ref[cols, :])
        sg = jax.nn.sigmoid(g)
        silu = g * sg
        gp_ref[...] = (u * (sg * (1.0 + g * (1.0 - sg)))).astype(BF16)
        up_ref[...] = silu.astype(BF16)
        hid = (silu * u).astype(BF16)
        hid_ref[...] = hid
        for f0 in range(n_f):
            @pl.when(f == f0)
            def _(f0=f0):
                hid_s[:, f0 * tf:(f0 + 1) * tf] = hid

        @pl.when(f == n_f - 1)
        def _():
            y_ref[...] = x_ref[...] + FFN_RES_WEIGHT * _dot_nn(hid_s[...], wd_ref[...])

    tok = pl.BlockSpec((tm, d), lambda i, f: (i, 0))
    whole = pl.BlockSpec((f_all, d), lambda i, f: (0, 0), pipeline_mode=pl.Buffered(1))
    act = pl.BlockSpec((tm, tf), lambda i, f: (i, f))
    act_shape = jax.ShapeDtypeStruct((t, f_all), BF16)
    res = pl.pallas_call(
        body, name=name, grid=(t // tm, n_f),
        out_shape=[jax.ShapeDtypeStruct((t, d), F32), jax.ShapeDtypeStruct((t, d), BF16), act_shape, act_shape, act_shape] + c_shapes,
        in_specs=[tok, pl.BlockSpec((1, d), lambda i, f: (0, 0)), whole, whole, whole] + c_in_specs,
        out_specs=[tok, tok, act, act, act] + c_out_specs,
        scratch_shapes=[pltpu.VMEM((tm, d), BF16), pltpu.VMEM((tm, f_all), BF16)] + c_scratch,
        compiler_params=_cp(("arbitrary", "arbitrary")),
    )(x, gn, wg_t, wu_t, wd, *c_inputs)
    return res[:5], res[5:]


def _ffn_bwd_hidden(dout, gp, up, hid, wd, name, tt, tf, comm=None, comm_at=None):
    t, d = dout.shape
    f_all = wd.shape[0]
    n_t = t // tt
    n_ci, n_co, c_in_specs, c_out_specs, c_shapes, c_scratch, c_inputs = _comm_parts(comm)

    def body(*refs):
        (dout_ref, gp_ref, up_ref, hid_ref, wd_ref), c_in, (dg_ref, du_ref, gwd_ref), c_out, (acc,), c_scr = _split(
            refs, 5, n_ci, 3, n_co, 1)
        s = pl.program_id(1)
        if comm:
            _run_hosted(comm, comm_at, pl.program_id(0) * n_t + s, c_in, c_out, c_scr)
        doutv = dout_ref[...]
        dhid = _dot_nt(doutv, wd_ref[...])
        dg_ref[...] = (dhid * gp_ref[...].astype(F32)).astype(BF16)
        du_ref[...] = (dhid * up_ref[...].astype(F32)).astype(BF16)
        _accumulate(acc, _dot_tn(hid_ref[...], doutv), s == 0)

        @pl.when(s == n_t - 1)
        def _():
            gwd_ref[...] = acc[...].astype(BF16)

    tok = pl.BlockSpec((tt, d), lambda f, s: (s, 0))
    wblk = pl.BlockSpec((tf, d), lambda f, s: (f, 0))
    act = pl.BlockSpec((tt, tf), lambda f, s: (s, f))
    act_shape = jax.ShapeDtypeStruct((t, f_all), BF16)
    res = pl.pallas_call(
        body, name=name, grid=(f_all // tf, n_t),
        out_shape=[act_shape, act_shape, jax.ShapeDtypeStruct((f_all, d), BF16)] + c_shapes,
        in_specs=[tok, act, act, act, wblk] + c_in_specs, out_specs=[act, act, wblk] + c_out_specs,
        scratch_shapes=[pltpu.VMEM((tf, d), F32)] + c_scratch,
        compiler_params=_cp(("arbitrary", "arbitrary")),
    )(dout, gp, up, hid, wd, *c_inputs)
    return res[:3], res[3:]


def _ffn_bwd_input(dy, x, gn, dg, du, wg_t, wu_t, name, tm, comm=None, comm_at=None):
    t, d = x.shape
    f_all = wg_t.shape[0]
    n_ci, n_co, c_in_specs, c_out_specs, c_shapes, c_scratch, c_inputs = _comm_parts(comm)

    def body(*refs):
        (dy_ref, x_ref, gn_ref, dg_ref, du_ref, wg_ref, wu_ref), c_in, (dx_ref, dgn_ref), c_out, c_scr = _split(
            refs, 7, n_ci, 2, n_co)
        i = pl.program_id(0)
        if comm:
            _run_hosted(comm, comm_at, i, c_in, c_out, c_scr)
        dh = _dot_nn(dg_ref[...], wg_ref[...]) + _dot_nn(du_ref[...], wu_ref[...])
        xv = x_ref[...]
        dxn, dgn = _norm_bwd(dh, xv, _rstd(xv), gn_ref[...])
        dx_ref[...] = dy_ref[...] + dxn
        _accumulate(dgn_ref, dgn, i == 0)

    tok = pl.BlockSpec((tm, d), lambda i: (i, 0))
    row = pl.BlockSpec((1, d), lambda i: (0, 0))
    act = pl.BlockSpec((tm, f_all), lambda i: (i, 0))
    whole = pl.BlockSpec((f_all, d), lambda i: (0, 0))
    res = pl.pallas_call(
        body, name=name, grid=(t // tm,),
        out_shape=[jax.ShapeDtypeStruct((t, d), F32), jax.ShapeDtypeStruct((1, d), F32)] + c_shapes,
        in_specs=[tok, tok, row, act, act, whole, whole] + c_in_specs,
        out_specs=[tok, row] + c_out_specs,
        scratch_shapes=c_scratch,
        compiler_params=_cp(("arbitrary",)),
    )(dy, x, gn, dg, du, wg_t, wu_t, *c_inputs)
    return res[:2], res[2:]


def _token_products(lhs_list, rhs, name, tt, tf, comm=None, comm_at=None):
    n_l = len(lhs_list)
    t, f_all = lhs_list[0].shape
    d = rhs.shape[1]
    n_t = t // tt
    n_ci, n_co, c_in_specs, c_out_specs, c_shapes, c_scratch, c_inputs = _comm_parts(comm)

    def body(*refs):
        lhs_refs, (rhs_ref,), c_in, out_refs, c_out, accs, c_scr = _split(refs, n_l, 1, n_ci, n_l, n_co, n_l)
        s = pl.program_id(1)
        if comm:
            _run_hosted(comm, comm_at, pl.program_id(0) * n_t + s, c_in, c_out, c_scr)
        rv = rhs_ref[...]
        for l_ref, acc in zip(lhs_refs, accs):
            _accumulate(acc, _dot_tn(l_ref[...], rv), s == 0)

        @pl.when(s == n_t - 1)
        def _():
            for o_ref, acc in zip(out_refs, accs):
                o_ref[...] = acc[...].astype(BF16)

    act = pl.BlockSpec((tt, tf), lambda f, s: (s, f))
    tok = pl.BlockSpec((tt, d), lambda f, s: (s, 0))
    wblk = pl.BlockSpec((tf, d), lambda f, s: (f, 0))
    res = pl.pallas_call(
        body, name=name, grid=(f_all // tf, n_t),
        out_shape=[jax.ShapeDtypeStruct((f_all, d), BF16)] * n_l + c_shapes,
        in_specs=[act] * n_l + [tok] + c_in_specs, out_specs=[wblk] * n_l + c_out_specs,
        scratch_shapes=[pltpu.VMEM((tf, d), F32)] * n_l + c_scratch,
        compiler_params=_cp(("arbitrary", "arbitrary")),
    )(*lhs_list, rhs, *c_inputs)
    return res[:n_l], res[n_l:]


def _swap_halves(t):
    w = t.shape[-1]
    lane = lax.broadcasted_iota(jnp.int32, (1, w), 1)
    return jnp.where((lane % HEAD_DIM) < HEAD_DIM // 2, pltpu.roll(t, w - HEAD_DIM // 2, 1), pltpu.roll(t, HEAD_DIM // 2, 1))


def _rope(t, cos, sin_signed):
    reps = t.shape[-1] // LANES
    return t * jnp.tile(cos, (1, reps)) + _swap_halves(t) * jnp.tile(sin_signed, (1, reps))


def _rope_bwd(dt, cos, sin_signed):
    reps = dt.shape[-1] // LANES
    return dt * jnp.tile(cos, (1, reps)) + _swap_halves(dt * jnp.tile(sin_signed, (1, reps)))


DILATIONS = tuple(dil for _, dil in B_PATTERNS if dil > 1)


def _seg_shape(t, dil, w, dtype):
    return jax.ShapeDtypeStruct((dil, t // dil, w), dtype)


def _seg_spec(tm, dil, w):
    return pl.BlockSpec((dil, tm // dil, w), lambda i: (0, i, 0))


def _tile_scratch(tm, w):
    return [pltpu.VMEM((tm, LANES), F32)] * (w // LANES)


def _put_tile(tile, val):
    for c, ref in enumerate(tile):
        ref[...] = val[:, c * LANES:(c + 1) * LANES]


def _get_tile(tile):
    return jnp.concatenate([ref[...] for ref in tile], axis=1)


def _scatter_to_segments(tile, seg_refs):
    for seg_ref, dil in zip(seg_refs, DILATIONS):
        rows = tile[0].shape[0] // dil
        for r in range(dil):
            for c, ref in enumerate(tile):
                seg_ref[r, :, c * LANES:(c + 1) * LANES] = ref[pl.ds(r, rows, stride=dil), :].astype(seg_ref.dtype)


def _gather_from_segments(seg_ref, dil, tile, add=False):
    rows = tile[0].shape[0] // dil
    for r in range(dil):
        for c, ref in enumerate(tile):
            idx = (pl.ds(r, rows, stride=dil), slice(None))
            v = seg_ref[r, :, c * LANES:(c + 1) * LANES].astype(F32)
            ref[idx] = ref[idx] + v if add else v


def _in_proj_fwd(x, gn, win_t, cos, sin_signed, name, tm):
    t, d = x.shape
    in_w = win_t.shape[0]
    n_dil = len(DILATIONS)

    def body(x_ref, gn_ref, w_ref, cos_ref, sin_ref, h_ref, aq_ref, akx_ref, avx_ref, bq_ref, bk_ref, bv_ref, *rest):
        seg_refs, tile = rest[:3 * n_dil], rest[3 * n_dil:]
        xv = x_ref[...]
        h = ((xv * _rstd(xv)) * gn_ref[...]).astype(BF16)
        h_ref[...] = h
        p = _dot_nt(h, w_ref[...])
        cs, sn = cos_ref[...], sin_ref[...]
        o = 0
        aq_ref[...] = (_rope(p[:, o:o + A_Q_W], cs, sn) * QK_SCALE).astype(BF16)
        o += A_Q_W
        ak = _rope(p[:, o:o + A_KV_W], cs, sn)
        o += A_KV_W
        av = p[:, o:o + A_KV_W]
        o += A_KV_W
        low = lax.broadcasted_iota(jnp.int32, (1, LANES), 1) < HEAD_DIM
        for src, dst in ((ak, akx_ref), (av, avx_ref)):
            other = pltpu.roll(src, HEAD_DIM, 1)
            dst[0] = jnp.where(low, src, other).astype(BF16)
            dst[1] = jnp.where(low, other, src).astype(BF16)
        for k, nat_ref in enumerate((bq_ref, bk_ref, bv_ref)):
            val = p[:, o:o + B_W]
            o += B_W
            if k < 2:
                val = _rope(val, cs, sn)
            if k == 0:
                val = val * QK_SCALE
            nat_ref[...] = val.astype(BF16)
            _put_tile(tile, val)
            _scatter_to_segments(tile, seg_refs[k * n_dil:(k + 1) * n_dil])

    tok = lambda w: pl.BlockSpec((tm, w), lambda i: (i, 0))
    kvx = pl.BlockSpec((2, tm, LANES), lambda i: (0, i, 0))
    sd = lambda *s: jax.ShapeDtypeStruct(s, BF16)
    res = pl.pallas_call(
        body, name=name, grid=(t // tm,),
        out_shape=[sd(t, d), sd(t, A_Q_W), sd(2, t, LANES), sd(2, t, LANES), sd(t, B_W), sd(t, B_W), sd(t, B_W)]
        + [_seg_shape(t, dil, B_W, BF16) for _ in range(3) for dil in DILATIONS],
        in_specs=[tok(d), pl.BlockSpec((1, d), lambda i: (0, 0)), pl.BlockSpec((in_w, d), lambda i: (0, 0)),
                  tok(LANES), tok(LANES)],
        out_specs=[tok(d), tok(A_Q_W), kvx, kvx, tok(B_W), tok(B_W), tok(B_W)]
        + [_seg_spec(tm, dil, B_W) for _ in range(3) for dil in DILATIONS],
        scratch_shapes=_tile_scratch(tm, B_W),
        compiler_params=_cp(("arbitrary",)),
    )(x, gn, win_t, cos, sin_signed)
    return res[:7], [res[7 + k * n_dil:7 + (k + 1) * n_dil] for k in range(3)]


def _in_proj_bwd(dres, x, gn, win_t, h, cos, sin_signed, daq, dakx, davx, dbq, dbk, dbv, name, tm):
    t, d = x.shape
    in_w = win_t.shape[0]
    n_t = t // tm
    chunk = in_w // 3

    def body(dres_ref, x_ref, gn_ref, w_ref, h_ref, cos_ref, sin_ref, daq_ref, dakx_ref, davx_ref, dbq_ref, dbk_ref,
             dbv_ref, dx_ref, dgn_ref, gw_ref, half_ref, dp_s, gw_s):
        i = pl.program_id(0)
        cs, sn = cos_ref[...], sin_ref[...]
        low = lax.broadcasted_iota(jnp.int32, (1, LANES), 1) < HEAD_DIM

        def fold(ref):
            a, b = ref[0].astype(F32), ref[1].astype(F32)
            return jnp.where(low, a + pltpu.roll(a, HEAD_DIM, 1), b + pltpu.roll(b, HEAD_DIM, 1))

        o = 0
        dp_s[:, o:o + A_Q_W] = _rope_bwd(daq_ref[...].astype(F32) * QK_SCALE, cs, sn).astype(BF16)
        o += A_Q_W
        dp_s[:, o:o + A_KV_W] = _rope_bwd(fold(dakx_ref), cs, sn).astype(BF16)
        o += A_KV_W
        dp_s[:, o:o + A_KV_W] = fold(davx_ref).astype(BF16)
        o += A_KV_W
        dp_s[:, o:o + B_W] = _rope_bwd(dbq_ref[...].astype(F32) * QK_SCALE, cs, sn).astype(BF16)
        o += B_W
        dp_s[:, o:o + B_W] = _rope_bwd(dbk_ref[...].astype(F32), cs, sn).astype(BF16)
        o += B_W
        dp_s[:, o:o + B_W] = dbv_ref[...].astype(BF16)
        dh = _dot_nn(dp_s[...], w_ref[...])
        hv = h_ref[...]
        for c0 in range(0, in_w, chunk):
            _accumulate(gw_s.at[pl.ds(c0, chunk), :], _dot_tn(dp_s[:, c0:c0 + chunk], hv), i == 0)
        xv = x_ref[...]
        dxn, dgn = _norm_bwd(dh, xv, _rstd(xv), gn_ref[...])
        dx = dres_ref[...] + dxn
        dx_ref[...] = dx
        half_ref[...] = (FFN_RES_WEIGHT * dx).astype(BF16)
        _accumulate(dgn_ref, dgn, i == 0)

        @pl.when(i == n_t - 1)
        def _():
            gw_ref[...] = gw_s[...].astype(BF16)

    tok = lambda w: pl.BlockSpec((tm, w), lambda i: (i, 0))
    row = pl.BlockSpec((1, d), lambda i: (0, 0))
    whole = pl.BlockSpec((in_w, d), lambda i: (0, 0))
    kvx = pl.BlockSpec((2, tm, LANES), lambda i: (0, i, 0))
    return pl.pallas_call(
        body, name=name, grid=(n_t,),
        out_shape=[jax.ShapeDtypeStruct((t, d), F32), jax.ShapeDtypeStruct((1, d), F32),
                   jax.ShapeDtypeStruct((in_w, d), BF16), jax.ShapeDtypeStruct((t, d), BF16)],
        in_specs=[tok(d), tok(d), row, whole, tok(d), tok(LANES), tok(LANES), tok(A_Q_W), kvx, kvx,
                  tok(B_W), tok(B_W), tok(B_W)],
        out_specs=[tok(d), row, whole, tok(d)],
        scratch_shapes=[pltpu.VMEM((tm, in_w), BF16), pltpu.VMEM((in_w, d), F32)],
        compiler_params=_cp(("arbitrary",)),
    )(dres, x, gn, win_t, h, cos, sin_signed, daq, dakx, davx, dbq, dbk, dbv)


def _merge_out_proj_fwd(x, a_out, outs, lses, wout, name, tm):
    t, d = x.shape
    n_dil = len(DILATIONS)

    def body(x_ref, a_ref, *rest):
        o_refs, l_refs, (w_ref, y_ref, b_ref, lt_ref), b_segs, lt_segs, scratch = _split(rest, 1 + n_dil, 1 + n_dil, 4, n_dil, n_dil)
        n_c = B_W // LANES
        tiles = [scratch[j * n_c:(j + 1) * n_c] for j in range(2 * n_dil)]
        os_, ls = [o_refs[0][...].astype(F32)], [l_refs[0][...]]
        for k, dil in enumerate(DILATIONS):
            _gather_from_segments(o_refs[1 + k], dil, tiles[2 * k])
            _gather_from_segments(l_refs[1 + k], dil, tiles[2 * k + 1])
            os_.append(_get_tile(tiles[2 * k]))
            ls.append(_get_tile(tiles[2 * k + 1]))
        mx = functools.reduce(jnp.maximum, ls)
        es = [jnp.exp(l - mx) for l in ls]
        den = functools.reduce(jnp.add, es)
        b = functools.reduce(jnp.add, [e * o for e, o in zip(es, os_)]) / den
        lt = mx + jnp.log(den)
        bb = b.astype(BF16)
        b_ref[...] = bb
        lt_ref[...] = lt
        y_ref[...] = x_ref[...] + _dot_nn(a_ref[...], w_ref[0:A_Q_W, :]) + _dot_nn(bb, w_ref[A_Q_W:A_Q_W + B_W, :])
        _put_tile(tiles[0], b)
        _scatter_to_segments(tiles[0], b_segs)
        _put_tile(tiles[1], lt)
        _scatter_to_segments(tiles[1], lt_segs)

    tok = lambda w: pl.BlockSpec((tm, w), lambda i: (i, 0))
    segs = [_seg_spec(tm, dil, B_W) for dil in DILATIONS]
    res = pl.pallas_call(
        body, name=name, grid=(t // tm,),
        out_shape=[jax.ShapeDtypeStruct((t, d), F32), jax.ShapeDtypeStruct((t, B_W), BF16), jax.ShapeDtypeStruct((t, B_W), F32)]
        + [_seg_shape(t, dil, B_W, BF16) for dil in DILATIONS] + [_seg_shape(t, dil, B_W, F32) for dil in DILATIONS],
        in_specs=[tok(d), tok(A_Q_W)] + ([tok(B_W)] + segs) * 2 + [pl.BlockSpec(wout.shape, lambda i: (0, 0))],
        out_specs=[tok(d), tok(B_W), tok(B_W)] + segs * 2,
        scratch_shapes=_tile_scratch(tm, B_W) * (2 * n_dil),
        compiler_params=_cp(("arbitrary",)),
    )(x, a_out, *outs, *lses, wout)
    return res[0], [res[1]] + list(res[3:3 + n_dil]), [res[2]] + list(res[3 + n_dil:])


def _out_proj_bwd(dy, a_out, b_out, wout, name, tm):
    t, d = dy.shape
    n_t = t // tm
    n_dil = len(DILATIONS)

    def body(dy_ref, a_ref, b_ref, w_ref, da_ref, db_ref, gw_ref, *rest):
        db_segs, gw_s, tile = rest[:n_dil], rest[n_dil], rest[n_dil + 1:]
        i = pl.program_id(0)
        dyb = dy_ref[...].astype(BF16)
        da_ref[...] = _dot_nt(dyb, w_ref[0:A_Q_W, :]).astype(BF16)
        db = _dot_nt(dyb, w_ref[A_Q_W:A_Q_W + B_W, :])
        db_ref[...] = db.astype(BF16)
        _put_tile(tile, db)
        _scatter_to_segments(tile, db_segs)
        ga = _dot_tn(a_ref[...], dyb)
        gb = _dot_tn(b_ref[...], dyb)

        @pl.when(i == 0)
        def _():
            gw_s[0:A_Q_W, :] = ga
            gw_s[A_Q_W:A_Q_W + B_W, :] = gb

        @pl.when(i > 0)
        def _():
            gw_s[0:A_Q_W, :] += ga
            gw_s[A_Q_W:A_Q_W + B_W, :] += gb

        @pl.when(i == n_t - 1)
        def _():
            gw_ref[...] = gw_s[...].astype(BF16)

    tok = lambda w: pl.BlockSpec((tm, w), lambda i: (i, 0))
    whole = pl.BlockSpec(wout.shape, lambda i: (0, 0))
    res = pl.pallas_call(
        body, name=name, grid=(n_t,),
        out_shape=[jax.ShapeDtypeStruct((t, A_Q_W), BF16), jax.ShapeDtypeStruct((t, B_W), BF16),
                   jax.ShapeDtypeStruct(wout.shape, BF16)] + [_seg_shape(t, dil, B_W, BF16) for dil in DILATIONS],
        in_specs=[tok(d), tok(A_Q_W), tok(B_W), whole],
        out_specs=[tok(A_Q_W), tok(B_W), whole] + [_seg_spec(tm, dil, B_W) for dil in DILATIONS],
        scratch_shapes=[pltpu.VMEM(wout.shape, F32)] + _tile_scratch(tm, B_W),
        compiler_params=_cp(("arbitrary",)),
    )(dy, a_out, b_out, wout)
    return res[0], [res[1]] + list(res[3:]), res[2]


def _sum_pattern_grads(per_pattern, name, tm):
    t = per_pattern[0][0].shape[0]
    n_dil = len(DILATIONS)

    def body(*refs):
        ins, outs, tile = _split(refs, 3 * (1 + n_dil), 3)
        for j, o_ref in enumerate(outs):
            _put_tile(tile, ins[j][...].astype(F32))
            for k, dil in enumerate(DILATIONS):
                _gather_from_segments(ins[3 * (1 + k) + j], dil, tile, add=True)
            o_ref[...] = _get_tile(tile).astype(BF16)

    tok = pl.BlockSpec((tm, B_W), lambda i: (i, 0))
    flat = [a if k == 0 else a.reshape(DILATIONS[k - 1], -1, B_W) for k, grads in enumerate(per_pattern) for a in grads]
    return pl.pallas_call(
        body, name=name, grid=(t // tm,),
        out_shape=[jax.ShapeDtypeStruct((t, B_W), BF16)] * 3,
        in_specs=[tok] * 3 + [_seg_spec(tm, dil, B_W) for dil in DILATIONS for _ in range(3)],
        out_specs=[tok] * 3, scratch_shapes=_tile_scratch(tm, B_W),
        compiler_params=_cp(("arbitrary",)),
    )(*flat)


SUB_ROWS = 64


def _sub_band(r0, hw, win, seg_lo, seg_len, t, rel, col):
    ks = pl.multiple_of(jnp.clip(r0 - hw, 0, t - win), SUB_ROWS)
    kpos = col + ks
    valid = (jnp.abs(rel + (ks - r0)) <= hw) & (kpos >= seg_lo) & (kpos < seg_lo + seg_len)
    return ks, valid


def _split_heads(v, low):
    zero = jnp.zeros_like(v)
    return jnp.concatenate([jnp.where(low, v, zero), jnp.where(low, zero, v)], axis=0)


def _kv_spec(kv, t):
    if kv.ndim == 3:
        return pl.BlockSpec((None, t, LANES), lambda p, i: (p // 2, 0, 0))
    return pl.BlockSpec((t, LANES), lambda p, i: (0, p))


def _attn_fwd(q, k, v, sink, name, hw, seg_len, tq, has_sink):
    t, width = q.shape
    sb = SUB_ROWS
    win = 2 * hw + LANES

    def body(sink_ref, q_ref, k_ref, v_ref, o_ref, lse_ref):
        p, i = pl.program_id(0), pl.program_id(1)
        q0 = i * tq
        seg_lo = (q0 // seg_len) * seg_len
        low = lax.broadcasted_iota(jnp.int32, (1, LANES), 1) < HEAD_DIM
        rel = lax.broadcasted_iota(jnp.int32, (sb, win), 1) - lax.broadcasted_iota(jnp.int32, (sb, win), 0)
        col = lax.broadcasted_iota(jnp.int32, (1, win), 1)
        subs = []
        for j in range(tq // sb):
            rows = pl.ds(j * sb, sb)
            ks, valid = _sub_band(q0 + j * sb, hw, win, seg_lo, seg_len, t, rel, col)
            subs.append((rows, ks, valid, _dot_nt(_split_heads(q_ref[rows, :], low), k_ref[pl.ds(ks, win), :])))
        for rows, ks, valid, s in subs:
            vw = v_ref[pl.ds(ks, win), :]
            es, inv, lses = [], [], []
            for a in range(2):
                sa = jnp.where(valid, s[a * sb:(a + 1) * sb], NEG)
                m = jnp.max(sa, axis=1, keepdims=True)
                if has_sink:
                    sk = sink_ref[2 * p + a]
                    m = jnp.maximum(m, sk)
                e = jnp.exp(sa - m)
                den = jnp.sum(e, axis=1, keepdims=True)
                if has_sink:
                    den = den + jnp.exp(sk - m)
                es.append(e.astype(BF16))
                inv.append(1.0 / den)
                lses.append(m + jnp.log(den))
            pv = _dot_nn(jnp.concatenate(es, axis=0), vw)
            o_ref[rows, :] = jnp.where(low, pv[0:sb] * inv[0], pv[sb:2 * sb] * inv[1]).astype(BF16)
            lse_ref[rows, :] = jnp.where(low, lses[0], lses[1])

    tile = pl.BlockSpec((tq, LANES), lambda p, i: (i, p))
    return pl.pallas_call(
        body, name=name, grid=(width // LANES, t // tq),
        out_shape=[jax.ShapeDtypeStruct((t, width), BF16), jax.ShapeDtypeStruct((t, width), F32)],
        in_specs=[pl.BlockSpec(memory_space=pltpu.SMEM), tile, _kv_spec(k, t), _kv_spec(v, t)],
        out_specs=[tile, tile],
        compiler_params=_cp(("arbitrary", "arbitrary")),
    )(sink, q, k, v)


def _attn_bwd(q, k, v, o, do, lse, sink, name, hw, seg_len, tq, has_sink, comm=None, comm_at=None):
    t, width = q.shape
    sb = SUB_ROWS
    win = 2 * hw + LANES
    n_q = t // tq
    shared_kv = k.ndim == 3
    n_ci, n_co, c_in_specs, c_out_specs, c_shapes, c_scratch, c_inputs = _comm_parts(comm)

    def body(*refs):
        ((sink_ref, q_ref, k_ref, v_ref, o_ref, do_ref, lse_ref), c_in, (dq_ref, dk_ref, dv_ref, ds_ref), c_out,
         (dk_s, dv_s), c_scr) = _split(refs, 7, n_ci, 4, n_co, 2)
        p, i = pl.program_id(0), pl.program_id(1)
        if comm:
            _run_hosted(comm, comm_at, p * n_q + i, c_in, c_out, c_scr)
        fresh = (i == 0) & (p % 2 == 0) if shared_kv else i == 0
        last = (i == n_q - 1) & (p % 2 == 1) if shared_kv else i == n_q - 1

        @pl.when(fresh)
        def _():
            dk_s[...] = jnp.zeros_like(dk_s)
            dv_s[...] = jnp.zeros_like(dv_s)

        q0 = i * tq
        seg_lo = (q0 // seg_len) * seg_len
        low = lax.broadcasted_iota(jnp.int32, (1, LANES), 1) < HEAD_DIM
        rel = lax.broadcasted_iota(jnp.int32, (sb, win), 1) - lax.broadcasted_iota(jnp.int32, (sb, win), 0)
        col = lax.broadcasted_iota(jnp.int32, (1, win), 1)
        dsink = [jnp.zeros((1, 1), F32), jnp.zeros((1, 1), F32)]
        subs = []
        for j in range(tq // sb):
            rows = pl.ds(j * sb, sb)
            ks, valid = _sub_band(q0 + j * sb, hw, win, seg_lo, seg_len, t, rel, col)
            kw = k_ref[pl.ds(ks, win), :]
            dov = do_ref[rows, :]
            q2 = _split_heads(q_ref[rows, :], low)
            do2 = _split_heads(dov, low)
            subs.append((rows, ks, valid, kw, dov, q2, do2, _dot_nt(q2, kw), _dot_nt(do2, v_ref[pl.ds(ks, win), :])))
        probs = []
        for rows, ks, valid, kw, dov, q2, do2, s, dpr in subs:
            prod = dov.astype(F32) * o_ref[rows, :].astype(F32)
            lse_t = lse_ref[rows, :]
            prs, dss = [], []
            for a in range(2):
                mine = low if a == 0 else jnp.logical_not(low)
                lse_a = jnp.max(jnp.where(mine, lse_t, -jnp.inf), axis=1, keepdims=True)
                delta = jnp.sum(jnp.where(mine, prod, 0.0), axis=1, keepdims=True)
                pr = jnp.exp(jnp.where(valid, s[a * sb:(a + 1) * sb], NEG) - lse_a)
                prs.append(pr.astype(BF16))
                dss.append((pr * (dpr[a * sb:(a + 1) * sb] - delta)).astype(BF16))
                if has_sink:
                    dsink[a] = dsink[a] - jnp.sum(jnp.exp(sink_ref[2 * p + a] - lse_a) * delta, axis=0, keepdims=True)
            probs.append((jnp.concatenate(prs, axis=0), jnp.concatenate(dss, axis=0)))
        for (rows, ks, valid, kw, dov, q2, do2, s, dpr), (pr2, ds2) in zip(subs, probs):
            dv_s[pl.ds(ks, win), :] += _dot_tn(pr2, do2)
            dk_s[pl.ds(ks, win), :] += _dot_tn(ds2, q2)
            dq2 = _dot_nn(ds2, kw)
            dq_ref[rows, :] = jnp.where(low, dq2[0:sb], dq2[sb:2 * sb]).astype(BF16)
        ds_ref[...] = jnp.broadcast_to(jnp.where(low, dsink[0], dsink[1]), ds_ref.shape)

        @pl.when(last)
        def _():
            dk_ref[...] = dk_s[...].astype(BF16)
            dv_ref[...] = dv_s[...].astype(BF16)

    tile = pl.BlockSpec((tq, LANES), lambda p, i: (i, p))
    kv_shape = jax.ShapeDtypeStruct(k.shape, BF16)
    res = pl.pallas_call(
        body, name=name, grid=(width // LANES, n_q),
        out_shape=[jax.ShapeDtypeStruct((t, width), BF16), kv_shape, kv_shape,
                   jax.ShapeDtypeStruct((width // LANES, n_q, 8, LANES), F32)] + c_shapes,
        in_specs=[pl.BlockSpec(memory_space=pltpu.SMEM), tile, _kv_spec(k, t), _kv_spec(v, t), tile, tile, tile] + c_in_specs,
        out_specs=[tile, _kv_spec(k, t), _kv_spec(v, t), pl.BlockSpec((None, None, 8, LANES), lambda p, i: (p, i, 0, 0))] + c_out_specs,
        scratch_shapes=[pltpu.VMEM((t, LANES), F32)] * 2 + c_scratch,
        compiler_params=_cp(("arbitrary", "arbitrary")),
    )(sink, q, k, v, o, do, lse, *c_inputs)
    return res[:4], res[4:]


def _final_norm_loss(x, gn, target, name, tm):
    t, d = x.shape

    def body(x_ref, gn_ref, tg_ref, dx_ref, dgn_ref, sq_ref, half_ref):
        i = pl.program_id(0)
        xv, gnv = x_ref[...], gn_ref[...]
        r = _rstd(xv)
        diff = (xv * r) * gnv - tg_ref[...]
        dxn, dgn = _norm_bwd(diff * (1.0 / d), xv, r, gnv)
        dx_ref[...] = dxn
        half_ref[...] = (FFN_RES_WEIGHT * dxn).astype(BF16)
        _accumulate(dgn_ref, dgn, i == 0)
        _accumulate(sq_ref, jnp.sum(diff * diff, axis=0, keepdims=True), i == 0)

    tok = pl.BlockSpec((tm, d), lambda i: (i, 0))
    row = pl.BlockSpec((1, d), lambda i: (0, 0))
    return pl.pallas_call(
        body, name=name, grid=(t // tm,),
        out_shape=[jax.ShapeDtypeStruct((t, d), F32), jax.ShapeDtypeStruct((1, d), F32), jax.ShapeDtypeStruct((1, d), F32),
                   jax.ShapeDtypeStruct((t, d), BF16)],
        in_specs=[tok, row, tok], out_specs=[tok, row, row, tok],
        compiler_params=_cp(("arbitrary",)),
    )(x, gn, target)


def _adamw(w, g, m, v, name):
    def body(w_ref, g_ref, m_ref, v_ref, d_ref, nm_ref, nv_ref):
        gv = g_ref[...]
        nm = ADAM_B1 * m_ref[...] + (1.0 - ADAM_B1) * gv
        nv = ADAM_B2 * v_ref[...] + (1.0 - ADAM_B2) * (gv * gv)
        m_hat = nm / (1.0 - ADAM_B1 ** ADAM_STEP)
        v_hat = nv / (1.0 - ADAM_B2 ** ADAM_STEP)
        d_ref[...] = -ADAM_LR * (m_hat / (jnp.sqrt(v_hat) + ADAM_EPS) + ADAM_WD * w_ref[...])
        nm_ref[...] = nm
        nv_ref[...] = nv

    shape = jax.ShapeDtypeStruct(w.shape, F32)
    return pl.pallas_call(body, name=name, out_shape=[shape, shape, shape], compiler_params=_cp())(w, g, m, v)


def _adamw_rows(land, j, rows, w, m, v, name):
    d = w.shape[1]

    def body(l_ref, w_ref, m_ref, v_ref, g_ref, d_ref, nm_ref, nv_ref):
        gv = l_ref[0].astype(F32)
        for s in range(1, N_DEV):
            gv = gv + l_ref[s].astype(F32)
        g_ref[...] = gv
        nm = ADAM_B1 * m_ref[...] + (1.0 - ADAM_B1) * gv
        nv = ADAM_B2 * v_ref[...] + (1.0 - ADAM_B2) * (gv * gv)
        m_hat = nm / (1.0 - ADAM_B1 ** ADAM_STEP)
        v_hat = nv / (1.0 - ADAM_B2 ** ADAM_STEP)
        d_ref[...] = -ADAM_LR * (m_hat / (jnp.sqrt(v_hat) + ADAM_EPS) + ADAM_WD * w_ref[...])
        nm_ref[...] = nm
        nv_ref[...] = nv

    halves = 2
    half = pl.BlockSpec((rows // halves, d), lambda i: (i, 0))
    shape = jax.ShapeDtypeStruct((rows, d), F32)
    return pl.pallas_call(
        body, name=name, grid=(halves,), out_shape=[shape] * 4,
        in_specs=[pl.BlockSpec((N_DEV, rows // halves, d), lambda i: (0, halves * j + i, 0)), half, half, half],
        out_specs=[half] * 4,
        compiler_params=_cp(("arbitrary",)),
    )(land, w, m, v)


def _sum_small(land, name):
    def body(l_ref, o_ref):
        acc = l_ref[0]
        for s in range(1, N_DEV):
            acc = acc + l_ref[s]
        o_ref[...] = acc

    return pl.pallas_call(body, name=name, out_shape=jax.ShapeDtypeStruct(land.shape[1:], F32), compiler_params=_cp())(land)


def _rope_lanes(positions):
    inv_freq = 1.0 / (ROPE_THETA ** (jnp.arange(0, HEAD_DIM, 2, dtype=F32) / HEAD_DIM))
    ang = positions.astype(F32)[:, None] * inv_freq
    cos, sin = jnp.cos(ang), jnp.sin(ang)
    return jnp.concatenate([cos, cos, cos, cos], axis=1), jnp.concatenate([-sin, sin, -sin, sin], axis=1)


def kernel(x, positions, norm_ffn1, w_gate1, w_up1, w_down1, norm_mix, w_in, a_sink, w_out, norm_ffn2, w_gate2, w_up2, w_down2, norm_final, loss_target, m_norm_ffn1, m_w_gate1, m_w_up1, m_w_down1, m_norm_mix, m_w_in, m_a_sink, m_w_out, m_norm_ffn2, m_w_gate2, m_w_up2, m_w_down2, m_norm_final, v_norm_ffn1, v_w_gate1, v_w_up1, v_w_down1, v_norm_mix, v_w_in, v_a_sink, v_w_out, v_norm_ffn2, v_w_gate2, v_w_up2, v_w_down2, v_norm_final):
    x = x[0]
    target = loss_target[0]
    t, d = x.shape
    tm = min(1024, t)
    tm_mix = min(512, t)
    tq = min(512, t // 16)

    tt = min(1024, t)
    f_all = w_gate1.shape[2] * N_DEV
    tfp = f_all // 2
    tm_fwd, tf_fwd = tm_mix, tfp
    n_steps = (t // tm_fwd) * (f_all // tf_fwd)

    def stacked(shards):
        return jnp.concatenate([s.astype(BF16) for s in shards], axis=0), [s.shape[0] for s in shards]

    packed1, rows1 = stacked([w_gate1[0].T, w_up1[0].T, w_down1[0]])
    packed2, rows2 = stacked([w_in[0].T, w_out[0], w_gate2[0].T, w_up2[0].T, w_down2[0]])
    wg1, wu1, wd1 = _run_alone(_gather_plan(packed1, rows1), "gather_ffn1_weights")

    cos, sin_signed = _rope_lanes(positions[0])
    sink = a_sink[0]
    no_sink = jnp.zeros_like(sink)

    (x1, h1, gp1, up1, hid1), (win, wout, wg2, wu2, wd2) = _ffn_fwd(
        x, norm_ffn1, wg1, wu1, wd1, "ffn1_fwd", tm_fwd, tf_fwd, _gather_plan(packed2, rows2), [0, (3 * n_steps) // 4, n_steps - 1])
    (h_mix, aq, akx, avx, bq, bk, bv), seg_qkv = _in_proj_fwd(x1, norm_mix, win, cos, sin_signed, "in_proj_fwd", tm_mix)
    a_out, a_lse = _attn_fwd(aq, akx, avx, sink, "attn_a_fwd", A_HALF_WINDOW, t, tq, True)
    rows_of = lambda a: a.reshape(t, B_W)
    segments_of = lambda a, dil: a if dil == 1 else a.reshape(dil, t // dil, B_W)
    b_qkv, b_outs, b_lses = [], [], []
    for n, (window, dil) in enumerate(B_PATTERNS):
        qs, ks, vs = (bq, bk, bv) if dil == 1 else (rows_of(seg_qkv[k][n - 1]) for k in range(3))
        o_seg, lse_seg = _attn_fwd(qs, ks, vs, no_sink, f"attn_b{dil}_fwd", window // (2 * dil), t // dil, tq, False)
        b_qkv.append((qs, ks, vs))
        b_outs.append(segments_of(o_seg, dil))
        b_lses.append(segments_of(lse_seg, dil))
    x2, b_out, b_lse = _merge_out_proj_fwd(x1, a_out, b_outs, b_lses, wout, "out_proj_fwd", tm_mix)
    (x3, h2, gp2, up2, hid2), _ = _ffn_fwd(x2, norm_ffn2, wg2, wu2, wd2, "ffn2_fwd", tm_fwd, tf_fwd)

    gfinal = norm_final.reshape(1, d)
    dx3, dg_final, sq, dout2 = _final_norm_loss(x3, gfinal, target, "final_norm_loss", tm)

    (dgt2, dut2, gwd2), _ = _ffn_bwd_hidden(dout2, gp2, up2, hid2, wd2, "ffn2_bwd_hidden", tm_mix, tfp)
    (dx2, dg_ffn2), _ = _ffn_bwd_input(dx3, x2, norm_ffn2, dgt2, dut2, wg2, wu2, "ffn2_bwd_input", tm_mix)
    (gwg2, gwu2), _ = _token_products([dgt2, dut2], h2, "ffn2_bwd_gate_up", tt, tfp)

    n_att = (A_Q_W // LANES) * (t // tq)
    n_prod = (f_all // tfp) * (t // tt)
    da, db, gwout = _out_proj_bwd(dx2, a_out, b_out[0], wout, "out_proj_bwd", tm_mix)
    (daq, dakx, davx, dsink_parts), (land_wg2,) = _attn_bwd(
        aq, akx, avx, a_out, da, a_lse, sink, "attn_a_bwd", A_HALF_WINDOW, t, tq, True, _exchange_plan([[gwg2]]), [0, n_att - 1])
    pattern_grads, pattern_lands = [], []
    for n, ((window, dil), (qs, ks, vs)) in enumerate(zip(B_PATTERNS, b_qkv)):
        carried = [gwu2, gwd2][n:n + 1]
        grads, lands = _attn_bwd(qs, ks, vs, rows_of(b_out[n]), rows_of(db[n]), rows_of(b_lse[n]), no_sink, f"attn_b{dil}_bwd",
                                 window // (2 * dil), t // dil, tq, False,
                                 _exchange_plan([carried]) if carried else None, [0, (B_W // LANES) * (t // tq) - 1])
        pattern_grads.append(grads[:3])
        pattern_lands.extend(lands)
    land_wu2, land_wd2 = pattern_lands
    dbq, dbk, dbv = _sum_pattern_grads(pattern_grads, "sum_pattern_grads", tm_mix)
    dx1, dg_mix, gwin, dout1 = _in_proj_bwd(dx2, x1, norm_mix, win, h_mix, cos, sin_signed, daq, dakx, davx, dbq, dbk, dbv, "in_proj_bwd", tm_mix)

    (dgt1, dut1, gwd1), (land_in, land_out) = _ffn_bwd_hidden(
        dout1, gp1, up1, hid1, wd1, "ffn1_bwd_hidden", tm_mix, tfp, _exchange_plan([[gwin], [gwout]]),
        [0, (f_all // tfp) * (t // tm_mix) - 1])
    (gwg1,), (land_wd1,) = _token_products([dgt1], h1, "ffn1_bwd_gate", tt, tfp, _exchange_plan([[gwd1]]), [0, n_prod - 1])
    (gwu1,), (land_wg1,) = _token_products([dut1], h1, "ffn1_bwd_up", tt, tfp, _exchange_plan([[gwg1]]), [0, n_prod - 1])
    (grad_x, dg_ffn1), (land_wu1,) = _ffn_bwd_input(
        dx1, x, norm_ffn1, dgt1, dut1, wg1, wu1, "ffn1_bwd_input", tm_mix, _exchange_plan([[gwu1]]), [0, t // tm_mix - 1])

    dsink_pairs = jnp.sum(dsink_parts[:, :, 0, :], axis=1)
    dsink = jnp.stack([dsink_pairs[:, 0], dsink_pairs[:, HEAD_DIM]], axis=1).reshape(1, -1)
    small = jnp.concatenate([dg_ffn1, dg_mix, dg_ffn2, dg_final, jnp.pad(dsink, ((0, 0), (0, d - dsink.shape[1]))),
                             sq, jnp.zeros((2, d), F32)], axis=0)
    (land_small,) = _run_alone(_exchange_plan([[jnp.tile(small, (N_DEV, 1))]]), "gather_small_gradients")
    red_small = _sum_small(land_small, "sum_small_grads")
    loss = 0.5 * jnp.sum(red_small[5]) / d

    rf = rows1[0]
    sharded = {"w_gate1": (land_wg1, 0, rf, True), "w_up1": (land_wu1, 0, rf, True), "w_down1": (land_wd1, 0, rf, False),
               "w_in": (land_in, 0, rows2[0], True), "w_out": (land_out, 0, rows2[1], False),
               "w_gate2": (land_wg2, 0, rf, True), "w_up2": (land_wu2, 0, rf, True), "w_down2": (land_wd2, 0, rf, False)}
    n_sink = a_sink.shape[1]
    small_grads = {"norm_ffn1": red_small[0:1], "norm_mix": red_small[1:2], "norm_ffn2": red_small[2:3], "norm_final": red_small[3],
                   "a_sink": red_small[4:5, :n_sink]}
    params = {
        "norm_ffn1": (norm_ffn1, m_norm_ffn1, v_norm_ffn1), "w_gate1": (w_gate1, m_w_gate1, v_w_gate1),
        "w_up1": (w_up1, m_w_up1, v_w_up1), "w_down1": (w_down1, m_w_down1, v_w_down1),
        "norm_mix": (norm_mix, m_norm_mix, v_norm_mix), "w_in": (w_in, m_w_in, v_w_in),
        "a_sink": (a_sink, m_a_sink, v_a_sink), "w_out": (w_out, m_w_out, v_w_out),
        "norm_ffn2": (norm_ffn2, m_norm_ffn2, v_norm_ffn2), "w_gate2": (w_gate2, m_w_gate2, v_w_gate2),
        "w_up2": (w_up2, m_w_up2, v_w_up2), "w_down2": (w_down2, m_w_down2, v_w_down2),
        "norm_final": (norm_final, m_norm_final, v_norm_final),
    }
    grad_list, deltas, new_ms, new_vs = [], [], [], []
    for name, (w, m, v) in params.items():
        if name in sharded:
            land, j, rows, is_transposed = sharded[name]
            view = (lambda a: a[0].T) if is_transposed else (lambda a: a[0])
            back = (lambda a: a.T[None]) if is_transposed else (lambda a: a[None])
            outs = [back(o) for o in _adamw_rows(land, j, rows, view(w), view(m), view(v), f"adamw_{name}")]
        else:
            as_block = (lambda a: a.reshape(1, -1)) if w.ndim == 1 else (lambda a: a)
            g = small_grads[name]
            outs = [g] + [o.reshape(w.shape) for o in _adamw(as_block(w), as_block(g), as_block(m), as_block(v), f"adamw_{name}")]
        for lst, o in zip((grad_list, deltas, new_ms, new_vs), outs):
            lst.append(o)
    return (loss, grad_x[None], *grad_list, *deltas, *new_ms, *new_vs)
```

```python
import functools
import itertools

import numpy as np
import jax
import jax.numpy as jnp
from jax import lax
from jax.experimental import pallas as pl
from jax.experimental.pallas import tpu as pltpu

F32 = jnp.float32
BF16 = jnp.bfloat16

N_DEV = 8
HEAD_DIM = 64
LANES = 128
A_Q_W, A_KV_W, B_W = 512, 128, 512
A_HALF_WINDOW = 128
B_PATTERNS = ((128, 1), (512, 4), (2048, 16))
ROPE_THETA = 10000.0
NORM_EPS = 1e-6
FFN_RES_WEIGHT = 0.5
QK_SCALE = HEAD_DIM ** -0.5
NEG = -1e30

ADAM_LR = 0.001
ADAM_B1 = 0.9
ADAM_B2 = 0.999
ADAM_EPS = 1e-08
ADAM_WD = 0.01
ADAM_STEP = 10

MESH_T = pl.DeviceIdType.MESH
VMEM_LIMIT = 60 * 1024 * 1024


def _cp(sem=None, vmem=VMEM_LIMIT):
    return pltpu.CompilerParams(dimension_semantics=sem, vmem_limit_bytes=vmem)


def _dot_nn(a, b):
    return jnp.dot(a, b, preferred_element_type=F32)


def _dot_nt(a, b):
    return lax.dot_general(a, b, (((1,), (1,)), ((), ())), preferred_element_type=F32)


def _dot_tn(a, b):
    return lax.dot_general(a, b, (((0,), (0,)), ((), ())), preferred_element_type=F32)


def _rstd(xv):
    return lax.rsqrt(jnp.mean(xv * xv, axis=-1, keepdims=True) + NORM_EPS)


def _norm_bwd(dh, xv, r, gn):
    gy = dh * gn
    c = jnp.sum(gy * xv, axis=-1, keepdims=True) * (1.0 / xv.shape[-1])
    dx = r * gy - xv * (r * r * r * c)
    dgn = jnp.sum(dh * (xv * r), axis=0, keepdims=True)
    return dx, dgn


def _accumulate(ref, val, first):
    @pl.when(first)
    def _():
        ref[...] = val

    @pl.when(jnp.logical_not(first))
    def _():
        ref[...] += val


def _mesh_pos():
    return lax.axis_index("x"), lax.axis_index("y"), lax.axis_index("c")


def _dev_index(d):
    return 4 * d[0] + 2 * d[1] + d[2]


class _Comm:
    def __init__(self, inputs, out_shape, scratch, phases):
        self.inputs, self.out_shape, self.scratch, self.phases = inputs, out_shape, scratch, phases

    def specs(self):
        any_spec = pl.BlockSpec(memory_space=pl.ANY)
        return [any_spec] * len(self.inputs), [any_spec] * len(self.out_shape)


def _run_alone(comm, name):
    n_in, n_out = len(comm.inputs), len(comm.out_shape)

    def body(*refs):
        for phase in comm.phases:
            phase(refs[:n_in], refs[n_in:n_in + n_out], refs[n_in + n_out:])

    in_specs, out_specs = comm.specs()
    return pl.pallas_call(body, name=name, out_shape=comm.out_shape, in_specs=in_specs, out_specs=out_specs,
                          scratch_shapes=comm.scratch)(*comm.inputs)


def _run_hosted(comm, at, step, ins, outs, scr):
    for phase, when in zip(comm.phases, at):
        @pl.when(step == when)
        def _(phase=phase):
            phase(ins, outs, scr)


def _split(refs, *counts):
    parts, o = [], 0
    for n in counts:
        parts.append(refs[o:o + n])
        o += n
    return parts + [refs[o:]]


def _gather_plan(packed, rows_list):
    n_w = len(rows_list)
    offs = [int(o) for o in np.cumsum([0] + list(rows_list[:-1]))]
    d = packed.shape[1]

    def tools(ins, outs, scr):
        p_ref = ins[0]
        send_sems, recv_sems, local_sem = scr
        x, y, c = _mesh_pos()
        me, sibling = (x, y, c), (x, y, 1 - c)
        chips = [(1 - x, y), (x, 1 - y), (1 - x, 1 - y)]

        def rows(w, dev):
            start = pl.multiple_of(_dev_index(dev) * rows_list[w], 16)
            return outs[w].at[pl.ds(start, rows_list[w]), :]

        def mine(w):
            return p_ref.at[pl.ds(offs[w], rows_list[w]), :]

        def copy(k, w, block, to, own):
            return pltpu.make_async_remote_copy(
                src_ref=mine(w) if own else rows(w, block), dst_ref=rows(w, block),
                send_sem=send_sems.at[k], recv_sem=recv_sems.at[k], device_id=to, device_id_type=MESH_T)

        def all_blocks(k):
            return pltpu.make_async_remote_copy(
                src_ref=p_ref, dst_ref=p_ref, send_sem=send_sems.at[k], recv_sem=recv_sems.at[k],
                device_id=me, device_id_type=MESH_T)

        return p_ref, local_sem, me, sibling, chips, c, rows, mine, copy, all_blocks

    def start(ins, outs, scr):
        _, local_sem, me, sibling, chips, c, rows, mine, copy, _ = tools(ins, outs, scr)
        for w in range(n_w):
            pltpu.make_async_copy(mine(w), rows(w, me), local_sem).start()
        for w in range(n_w):
            copy(0, w, me, sibling, True).start()
        for j, chip in enumerate(chips):
            for w in range(n_w):
                copy(1 + j, w, me, (*chip, c), True).start()

    def relay(ins, outs, scr):
        _, _, _, sibling, chips, c, _, _, copy, all_blocks = tools(ins, outs, scr)
        for j, chip in enumerate(chips):
            all_blocks(1 + j).wait_recv()
            for w in range(n_w):
                copy(4 + j, w, (*chip, c), sibling, False).start()

    def finish(ins, outs, scr):
        p_ref, local_sem, _, _, _, _, _, _, _, all_blocks = tools(ins, outs, scr)
        all_blocks(0).wait_recv()
        for j in range(3):
            all_blocks(4 + j).wait_recv()
        for k in range(7):
            all_blocks(k).wait_send()
        pltpu.make_async_copy(p_ref, p_ref, local_sem).wait()

    return _Comm(
        [packed], [jax.ShapeDtypeStruct((N_DEV * r, d), packed.dtype) for r in rows_list],
        [pltpu.SemaphoreType.DMA((7,)), pltpu.SemaphoreType.DMA((7,)), pltpu.SemaphoreType.DMA], [start, relay, finish])


def _exchange_plan(groups):
    flat = [a for g in groups for a in g]
    n_g = len(groups)
    sizes = [len(g) for g in groups]
    rows = [g[0].shape[0] // N_DEV for g in groups]
    first = [int(o) for o in np.cumsum([0] + sizes[:-1])]

    def start(srcs, lands, scr):
        send_sems, recv_sems, local_sems = scr
        x, y, c = _mesh_pos()
        me = (x, y, c)
        me_idx = _dev_index(me)

        def block(g, i, dev):
            start_row = pl.multiple_of(_dev_index(dev) * rows[g], 8)
            return srcs[first[g] + i].at[pl.ds(start_row, rows[g]), :]

        def slot(g, i):
            return lands[g].at[me_idx, pl.ds(i * rows[g], rows[g]), :]

        for g in range(n_g):
            for i in range(sizes[g]):
                pltpu.make_async_copy(block(g, i, me), slot(g, i), local_sems.at[g]).start()
        flips = [f for f in itertools.product((0, 1), repeat=3) if any(f)]
        for k, (fx, fy, fc) in enumerate(flips):
            peer = (1 - x if fx else x, 1 - y if fy else y, 1 - c if fc else c)
            for g in range(n_g):
                for i in range(sizes[g]):
                    pltpu.make_async_remote_copy(
                        src_ref=block(g, i, peer), dst_ref=slot(g, i), send_sem=send_sems.at[g, k],
                        recv_sem=recv_sems.at[g, k], device_id=peer, device_id_type=MESH_T).start()

    def finish(srcs, lands, scr):
        send_sems, recv_sems, local_sems = scr
        me = _mesh_pos()
        for k in range(7):
            for g in range(n_g):
                pltpu.make_async_remote_copy(
                    src_ref=lands[g].at[0], dst_ref=lands[g].at[0], send_sem=send_sems.at[g, k],
                    recv_sem=recv_sems.at[g, k], device_id=me, device_id_type=MESH_T).wait()
        for g in range(n_g):
            pltpu.make_async_copy(lands[g].at[0], lands[g].at[0], local_sems.at[g]).wait()

    return _Comm(
        flat, [jax.ShapeDtypeStruct((N_DEV, sizes[g] * rows[g], groups[g][0].shape[1]), groups[g][0].dtype) for g in range(n_g)],
        [pltpu.SemaphoreType.DMA((n_g, 7)), pltpu.SemaphoreType.DMA((n_g, 7)), pltpu.SemaphoreType.DMA((n_g,))],
        [start, finish])


def _comm_parts(comm):
    if comm is None:
        return 0, 0, [], [], [], [], []
    in_specs, out_specs = comm.specs()
    return len(comm.inputs), len(comm.out_shape), in_specs, out_specs, comm.out_shape, comm.scratch, comm.inputs


def _ffn_fwd(x, gn, wg_t, wu_t, wd, name, tm, tf, comm=None, comm_at=None, loss_head=None):
    t, d = x.shape
    f_all = wg_t.shape[0]
    n_f = f_all // tf
    n_ci, n_co, c_in_specs, c_out_specs, c_shapes, c_scratch, c_inputs = _comm_parts(comm)
    n_head, n_y = (2, 4) if loss_head else (0, 1)

    def body(*refs):
        ((x_ref, gn_ref, wg_ref, wu_ref, wd_ref), head_in, c_in, y_outs, (h_ref, gp_ref, up_ref, hid_ref), c_out,
         (h_s, hid_s), c_scr) = _split(refs, 5, n_head, n_ci, n_y, 4, n_co, 2)
        f = pl.program_id(1)
        if comm:
            _run_hosted(comm, comm_at, pl.program_id(0) * n_f + f, c_in, c_out, c_scr)

        @pl.when(f == 0)
        def _():
            xv = x_ref[...]
            h = ((xv * _rstd(xv)) * gn_ref[...]).astype(BF16)
            h_s[...] = h
            h_ref[...] = h

        h = h_s[...]
        cols = pl.ds(pl.multiple_of(f * tf, tf), tf)
        g = _dot_nt(h, wg_ref[cols, :])
        u = _dot_nt(h, wu_ref[cols, :])
        sg = jax.nn.sigmoid(g)
        silu = g * sg
        gp_ref[...] = (u * (sg * (1.0 + g * (1.0 - sg)))).astype(BF16)
        up_ref[...] = silu.astype(BF16)
        hid = (silu * u).astype(BF16)
        hid_ref[...] = hid
        for f0 in range(n_f):
            @pl.when(f == f0)
            def _(f0=f0):
                hid_s[:, f0 * tf:(f0 + 1) * tf] = hid

        @pl.when(f == n_f - 1)
        def _():
            y = x_ref[...] + FFN_RES_WEIGHT * _dot_nn(hid_s[...], wd_ref[...])
            if not loss_head:
                y_outs[0][...] = y
            else:
                (gf_ref, tg_ref), (dy_ref, dgf_ref, sq_ref, half_ref) = head_in, y_outs
                gfv, r, first = gf_ref[...], _rstd(y), pl.program_id(0) == 0
                diff = (y * r) * gfv - tg_ref[...]
                dy, dgf = _norm_bwd(diff * (1.0 / d), y, r, gfv)
                dy_ref[...] = dy
                half_ref[...] = (FFN_RES_WEIGHT * dy).astype(BF16)
                _accumulate(dgf_ref, dgf, first)
                _accumulate(sq_ref, jnp.sum(diff * diff, axis=0, keepdims=True), first)

    tok = pl.BlockSpec((tm, d), lambda i, f: (i, 0))
    row = pl.BlockSpec((1, d), lambda i, f: (0, 0))
    whole = pl.BlockSpec((f_all, d), lambda i, f: (0, 0), pipeline_mode=pl.Buffered(1))
    act = pl.BlockSpec((tm, tf), lambda i, f: (i, f))
    act_shape = jax.ShapeDtypeStruct((t, f_all), BF16)
    tok_f32, tok_bf16, row_f32 = (jax.ShapeDtypeStruct((t, d), F32), jax.ShapeDtypeStruct((t, d), BF16),
                                  jax.ShapeDtypeStruct((1, d), F32))
    y_shapes, y_specs = ([tok_f32, row_f32, row_f32, tok_bf16], [tok, row, row, tok]) if loss_head else ([tok_f32], [tok])
    res = pl.pallas_call(
        body, name=name, grid=(t // tm, n_f),
        out_shape=y_shapes + [tok_bf16, act_shape, act_shape, act_shape] + c_shapes,
        in_specs=[tok, row, whole, whole, whole] + ([row, tok] if loss_head else []) + c_in_specs,
        out_specs=y_specs + [tok, act, act, act] + c_out_specs,
        scratch_shapes=[pltpu.VMEM((tm, d), BF16), pltpu.VMEM((tm, f_all), BF16)] + c_scratch,
        compiler_params=_cp(("arbitrary", "arbitrary")),
    )(x, gn, wg_t, wu_t, wd, *(loss_head or ()), *c_inputs)
    return res[:n_y + 4], res[n_y + 4:]


def _ffn_bwd_hidden(dout, gp, up, hid, wd, name, tt, tf, comm=None, comm_at=None):
    t, d = dout.shape
    f_all = wd.shape[0]
    n_t = t // tt
    n_ci, n_co, c_in_specs, c_out_specs, c_shapes, c_scratch, c_inputs = _comm_parts(comm)

    def body(*refs):
        (dout_ref, gp_ref, up_ref, hid_ref, wd_ref), c_in, (dg_ref, du_ref, gwd_ref), c_out, (acc,), c_scr = _split(
            refs, 5, n_ci, 3, n_co, 1)
        s = pl.program_id(1)
        if comm:
            _run_hosted(comm, comm_at, pl.program_id(0) * n_t + s, c_in, c_out, c_scr)
        doutv = dout_ref[...]
        dhid = _dot_nt(doutv, wd_ref[...])
        dg_ref[...] = (dhid * gp_ref[...].astype(F32)).astype(BF16)
        du_ref[...] = (dhid * up_ref[...].astype(F32)).astype(BF16)
        _accumulate(acc, _dot_tn(hid_ref[...], doutv), s == 0)

        @pl.when(s == n_t - 1)
        def _():
            gwd_ref[...] = acc[...].astype(BF16)

    tok = pl.BlockSpec((tt, d), lambda f, s: (s, 0))
    wblk = pl.BlockSpec((tf, d), lambda f, s: (f, 0))
    act = pl.BlockSpec((tt, tf), lambda f, s: (s, f))
    act_shape = jax.ShapeDtypeStruct((t, f_all), BF16)
    res = pl.pallas_call(
        body, name=name, grid=(f_all // tf, n_t),
        out_shape=[act_shape, act_shape, jax.ShapeDtypeStruct((f_all, d), BF16)] + c_shapes,
        in_specs=[tok, act, act, act, wblk] + c_in_specs, out_specs=[act, act, wblk] + c_out_specs,
        scratch_shapes=[pltpu.VMEM((tf, d), F32)] + c_scratch,
        compiler_params=_cp(("arbitrary", "arbitrary")),
    )(dout, gp, up, hid, wd, *c_inputs)
    return res[:3], res[3:]


def _ffn_bwd_input(dy, x, gn, dg, du, wg_t, wu_t, name, tm, comm=None, comm_at=None):
    t, d = x.shape
    f_all = wg_t.shape[0]
    n_ci, n_co, c_in_specs, c_out_specs, c_shapes, c_scratch, c_inputs = _comm_parts(comm)

    def body(*refs):
        (dy_ref, x_ref, gn_ref, dg_ref, du_ref, wg_ref, wu_ref), c_in, (dx_ref, dgn_ref), c_out, c_scr = _split(
            refs, 7, n_ci, 2, n_co)
        i = pl.program_id(0)
        if comm:
            _run_hosted(comm, comm_at, i, c_in, c_out, c_scr)
        dh = _dot_nn(dg_ref[...], wg_ref[...]) + _dot_nn(du_ref[...], wu_ref[...])
        xv = x_ref[...]
        dxn, dgn = _norm_bwd(dh, xv, _rstd(xv), gn_ref[...])
        dx_ref[...] = dy_ref[...] + dxn
        _accumulate(dgn_ref, dgn, i == 0)

    tok = pl.BlockSpec((tm, d), lambda i: (i, 0))
    row = pl.BlockSpec((1, d), lambda i: (0, 0))
    act = pl.BlockSpec((tm, f_all), lambda i: (i, 0))
    whole = pl.BlockSpec((f_all, d), lambda i: (0, 0))
    res = pl.pallas_call(
        body, name=name, grid=(t // tm,),
        out_shape=[jax.ShapeDtypeStruct((t, d), F32), jax.ShapeDtypeStruct((1, d), F32)] + c_shapes,
        in_specs=[tok, tok, row, act, act, whole, whole] + c_in_specs,
        out_specs=[tok, row] + c_out_specs,
        scratch_shapes=c_scratch,
        compiler_params=_cp(("arbitrary",)),
    )(dy, x, gn, dg, du, wg_t, wu_t, *c_inputs)
    return res[:2], res[2:]


def _token_products(lhs_list, rhs, name, tt, tf, comm=None, comm_at=None):
    n_l = len(lhs_list)
    t, f_all = lhs_list[0].shape
    d = rhs.shape[1]
    n_t = t // tt
    n_ci, n_co, c_in_specs, c_out_specs, c_shapes, c_scratch, c_inputs = _comm_parts(comm)

    def body(*refs):
        lhs_refs, (rhs_ref,), c_in, out_refs, c_out, accs, c_scr = _split(refs, n_l, 1, n_ci, n_l, n_co, n_l)
        s = pl.program_id(1)
        if comm:
            _run_hosted(comm, comm_at, pl.program_id(0) * n_t + s, c_in, c_out, c_scr)
        rv = rhs_ref[...]
        for l_ref, acc in zip(lhs_refs, accs):
            _accumulate(acc, _dot_tn(l_ref[...], rv), s == 0)

        @pl.when(s == n_t - 1)
        def _():
            for o_ref, acc in zip(out_refs, accs):
                o_ref[...] = acc[...].astype(BF16)

    act = pl.BlockSpec((tt, tf), lambda f, s: (s, f))
    tok = pl.BlockSpec((tt, d), lambda f, s: (s, 0))
    wblk = pl.BlockSpec((tf, d), lambda f, s: (f, 0))
    res = pl.pallas_call(
        body, name=name, grid=(f_all // tf, n_t),
        out_shape=[jax.ShapeDtypeStruct((f_all, d), BF16)] * n_l + c_shapes,
        in_specs=[act] * n_l + [tok] + c_in_specs, out_specs=[wblk] * n_l + c_out_specs,
        scratch_shapes=[pltpu.VMEM((tf, d), F32)] * n_l + c_scratch,
        compiler_params=_cp(("arbitrary", "arbitrary")),
    )(*lhs_list, rhs, *c_inputs)
    return res[:n_l], res[n_l:]


def _swap_halves(t):
    w = t.shape[-1]
    lane = lax.broadcasted_iota(jnp.int32, (1, w), 1)
    return jnp.where((lane % HEAD_DIM) < HEAD_DIM // 2, pltpu.roll(t, w - HEAD_DIM // 2, 1), pltpu.roll(t, HEAD_DIM // 2, 1))


def _rope(t, cos, sin_signed):
    reps = t.shape[-1] // LANES
    return t * jnp.tile(cos, (1, reps)) + _swap_halves(t) * jnp.tile(sin_signed, (1, reps))


def _rope_bwd(dt, cos, sin_signed):
    reps = dt.shape[-1] // LANES
    return dt * jnp.tile(cos, (1, reps)) + _swap_halves(dt * jnp.tile(sin_signed, (1, reps)))


DILATIONS = tuple(dil for _, dil in B_PATTERNS if dil > 1)


def _seg_shape(t, dil, w, dtype):
    return jax.ShapeDtypeStruct((dil, t // dil, w), dtype)


def _seg_spec(tm, dil, w):
    return pl.BlockSpec((dil, tm // dil, w), lambda i: (0, i, 0))


def _tile_scratch(tm, w):
    return [pltpu.VMEM((tm, LANES), F32)] * (w // LANES)


def _put_tile(tile, val):
    for c, ref in enumerate(tile):
        ref[...] = val[:, c * LANES:(c + 1) * LANES]


def _get_tile(tile):
    return jnp.concatenate([ref[...] for ref in tile], axis=1)


def _scatter_to_segments(tile, seg_refs):
    for seg_ref, dil in zip(seg_refs, DILATIONS):
        rows = tile[0].shape[0] // dil
        for r in range(dil):
            for c, ref in enumerate(tile):
                seg_ref[r, :, c * LANES:(c + 1) * LANES] = ref[pl.ds(r, rows, stride=dil), :].astype(seg_ref.dtype)


def _gather_from_segments(seg_ref, dil, tile, add=False):
    rows = tile[0].shape[0] // dil
    for r in range(dil):
        for c, ref in enumerate(tile):
            idx = (pl.ds(r, rows, stride=dil), slice(None))
            v = seg_ref[r, :, c * LANES:(c + 1) * LANES].astype(F32)
            ref[idx] = ref[idx] + v if add else v


def _in_proj_fwd(x, gn, win_t, cos, sin_signed, name, tm):
    t, d = x.shape
    in_w = win_t.shape[0]
    n_dil = len(DILATIONS)

    def body(x_ref, gn_ref, w_ref, cos_ref, sin_ref, h_ref, aq_ref, akx_ref, avx_ref, bq_ref, bk_ref, bv_ref, *rest):
        seg_refs, tile = rest[:3 * n_dil], rest[3 * n_dil:]
        xv = x_ref[...]
        h = ((xv * _rstd(xv)) * gn_ref[...]).astype(BF16)
        h_ref[...] = h
        p = _dot_nt(h, w_ref[...])
        cs, sn = cos_ref[...], sin_ref[...]
        o = 0
        aq_ref[...] = (_rope(p[:, o:o + A_Q_W], cs, sn) * QK_SCALE).astype(BF16)
        o += A_Q_W
        ak = _rope(p[:, o:o + A_KV_W], cs, sn)
        o += A_KV_W
        av = p[:, o:o + A_KV_W]
        o += A_KV_W
        low = lax.broadcasted_iota(jnp.int32, (1, LANES), 1) < HEAD_DIM
        for src, dst in ((ak, akx_ref), (av, avx_ref)):
            other = pltpu.roll(src, HEAD_DIM, 1)
            dst[0] = jnp.where(low, src, other).astype(BF16)
            dst[1] = jnp.where(low, other, src).astype(BF16)
        for k, nat_ref in enumerate((bq_ref, bk_ref, bv_ref)):
            val = p[:, o:o + B_W]
            o += B_W
            if k < 2:
                val = _rope(val, cs, sn)
            if k == 0:
                val = val * QK_SCALE
            nat_ref[...] = val.astype(BF16)
            _put_tile(tile, val)
            _scatter_to_segments(tile, seg_refs[k * n_dil:(k + 1) * n_dil])

    tok = lambda w: pl.BlockSpec((tm, w), lambda i: (i, 0))
    kvx = pl.BlockSpec((2, tm, LANES), lambda i: (0, i, 0))
    sd = lambda *s: jax.ShapeDtypeStruct(s, BF16)
    res = pl.pallas_call(
        body, name=name, grid=(t // tm,),
        out_shape=[sd(t, d), sd(t, A_Q_W), sd(2, t, LANES), sd(2, t, LANES), sd(t, B_W), sd(t, B_W), sd(t, B_W)]
        + [_seg_shape(t, dil, B_W, BF16) for _ in range(3) for dil in DILATIONS],
        in_specs=[tok(d), pl.BlockSpec((1, d), lambda i: (0, 0)), pl.BlockSpec((in_w, d), lambda i: (0, 0)),
                  tok(LANES), tok(LANES)],
        out_specs=[tok(d), tok(A_Q_W), kvx, kvx, tok(B_W), tok(B_W), tok(B_W)]
        + [_seg_spec(tm, dil, B_W) for _ in range(3) for dil in DILATIONS],
        scratch_shapes=_tile_scratch(tm, B_W),
        compiler_params=_cp(("arbitrary",)),
    )(x, gn, win_t, cos, sin_signed)
    return res[:7], [res[7 + k * n_dil:7 + (k + 1) * n_dil] for k in range(3)]


def _in_proj_bwd(dres, x, gn, win_t, h, cos, sin_signed, daq, dakx, davx, dbq, dbk, dbv, name, tm):
    t, d = x.shape
    in_w = win_t.shape[0]
    n_t = t // tm
    chunk = in_w // 3

    def body(dres_ref, x_ref, gn_ref, w_ref, h_ref, cos_ref, sin_ref, daq_ref, dakx_ref, davx_ref, dbq_ref, dbk_ref,
             dbv_ref, dx_ref, dgn_ref, gw_ref, half_ref, dp_s, gw_s):
        i = pl.program_id(0)
        cs, sn = cos_ref[...], sin_ref[...]
        low = lax.broadcasted_iota(jnp.int32, (1, LANES), 1) < HEAD_DIM

        def fold(ref):
            a, b = ref[0].astype(F32), ref[1].astype(F32)
            return jnp.where(low, a + pltpu.roll(a, HEAD_DIM, 1), b + pltpu.roll(b, HEAD_DIM, 1))

        o = 0
        dp_s[:, o:o + A_Q_W] = _rope_bwd(daq_ref[...].astype(F32) * QK_SCALE, cs, sn).astype(BF16)
        o += A_Q_W
        dp_s[:, o:o + A_KV_W] = _rope_bwd(fold(dakx_ref), cs, sn).astype(BF16)
        o += A_KV_W
        dp_s[:, o:o + A_KV_W] = fold(davx_ref).astype(BF16)
        o += A_KV_W
        dp_s[:, o:o + B_W] = _rope_bwd(dbq_ref[...].astype(F32) * QK_SCALE, cs, sn).astype(BF16)
        o += B_W
        dp_s[:, o:o + B_W] = _rope_bwd(dbk_ref[...].astype(F32), cs, sn).astype(BF16)
        o += B_W
        dp_s[:, o:o + B_W] = dbv_ref[...].astype(BF16)
        dh = _dot_nn(dp_s[...], w_ref[...])
        hv = h_ref[...]
        for c0 in range(0, in_w, chunk):
            _accumulate(gw_s.at[pl.ds(c0, chunk), :], _dot_tn(dp_s[:, c0:c0 + chunk], hv), i == 0)
        xv = x_ref[...]
        dxn, dgn = _norm_bwd(dh, xv, _rstd(xv), gn_ref[...])
        dx = dres_ref[...] + dxn
        dx_ref[...] = dx
        half_ref[...] = (FFN_RES_WEIGHT * dx).astype(BF16)
        _accumulate(dgn_ref, dgn, i == 0)

        @pl.when(i == n_t - 1)
        def _():
            gw_ref[...] = gw_s[...].astype(BF16)

    tok = lambda w: pl.BlockSpec((tm, w), lambda i: (i, 0))
    row = pl.BlockSpec((1, d), lambda i: (0, 0))
    whole = pl.BlockSpec((in_w, d), lambda i: (0, 0))
    kvx = pl.BlockSpec((2, tm, LANES), lambda i: (0, i, 0))
    return pl.pallas_call(
        body, name=name, grid=(n_t,),
        out_shape=[jax.ShapeDtypeStruct((t, d), F32), jax.ShapeDtypeStruct((1, d), F32),
                   jax.ShapeDtypeStruct((in_w, d), BF16), jax.ShapeDtypeStruct((t, d), BF16)],
        in_specs=[tok(d), tok(d), row, whole, tok(d), tok(LANES), tok(LANES), tok(A_Q_W), kvx, kvx,
                  tok(B_W), tok(B_W), tok(B_W)],
        out_specs=[tok(d), row, whole, tok(d)],
        scratch_shapes=[pltpu.VMEM((tm, in_w), BF16), pltpu.VMEM((in_w, d), F32)],
        compiler_params=_cp(("arbitrary",)),
    )(dres, x, gn, win_t, h, cos, sin_signed, daq, dakx, davx, dbq, dbk, dbv)


def _merge_out_proj_fwd(x, a_out, outs, lses, wout, name, tm):
    t, d = x.shape
    n_dil = len(DILATIONS)

    def body(x_ref, a_ref, *rest):
        o_refs, l_refs, (w_ref, y_ref, b_ref, lt_ref), b_segs, lt_segs, scratch = _split(rest, 1 + n_dil, 1 + n_dil, 4, n_dil, n_dil)
        n_c = B_W // LANES
        tiles = [scratch[j * n_c:(j + 1) * n_c] for j in range(2 * n_dil)]
        os_, ls = [o_refs[0][...].astype(F32)], [l_refs[0][...]]
        for k, dil in enumerate(DILATIONS):
            _gather_from_segments(o_refs[1 + k], dil, tiles[2 * k])
            _gather_from_segments(l_refs[1 + k], dil, tiles[2 * k + 1])
            os_.append(_get_tile(tiles[2 * k]))
            ls.append(_get_tile(tiles[2 * k + 1]))
        mx = functools.reduce(jnp.maximum, ls)
        es = [jnp.exp(l - mx) for l in ls]
        den = functools.reduce(jnp.add, es)
        b = functools.reduce(jnp.add, [e * o for e, o in zip(es, os_)]) / den
        lt = mx + jnp.log(den)
        bb = b.astype(BF16)
        b_ref[...] = bb
        lt_ref[...] = lt
        y_ref[...] = x_ref[...] + _dot_nn(a_ref[...], w_ref[0:A_Q_W, :]) + _dot_nn(bb, w_ref[A_Q_W:A_Q_W + B_W, :])
        _put_tile(tiles[0], b)
        _scatter_to_segments(tiles[0], b_segs)
        _put_tile(tiles[1], lt)
        _scatter_to_segments(tiles[1], lt_segs)

    tok = lambda w: pl.BlockSpec((tm, w), lambda i: (i, 0))
    segs = [_seg_spec(tm, dil, B_W) for dil in DILATIONS]
    res = pl.pallas_call(
        body, name=name, grid=(t // tm,),
        out_shape=[jax.ShapeDtypeStruct((t, d), F32), jax.ShapeDtypeStruct((t, B_W), BF16), jax.ShapeDtypeStruct((t, B_W), F32)]
        + [_seg_shape(t, dil, B_W, BF16) for dil in DILATIONS] + [_seg_shape(t, dil, B_W, F32) for dil in DILATIONS],
        in_specs=[tok(d), tok(A_Q_W)] + ([tok(B_W)] + segs) * 2 + [pl.BlockSpec(wout.shape, lambda i: (0, 0))],
        out_specs=[tok(d), tok(B_W), tok(B_W)] + segs * 2,
        scratch_shapes=_tile_scratch(tm, B_W) * (2 * n_dil),
        compiler_params=_cp(("arbitrary",)),
    )(x, a_out, *outs, *lses, wout)
    return res[0], [res[1]] + list(res[3:3 + n_dil]), [res[2]] + list(res[3 + n_dil:])


def _out_proj_bwd(dy, a_out, b_out, wout, name, tm):
    t, d = dy.shape
    n_t = t // tm
    n_dil = len(DILATIONS)

    def body(dy_ref, a_ref, b_ref, w_ref, da_ref, db_ref, gw_ref, *rest):
        db_segs, gw_s, tile = rest[:n_dil], rest[n_dil], rest[n_dil + 1:]
        i = pl.program_id(0)
        dyb = dy_ref[...].astype(BF16)
        da_ref[...] = _dot_nt(dyb, w_ref[0:A_Q_W, :]).astype(BF16)
        db = _dot_nt(dyb, w_ref[A_Q_W:A_Q_W + B_W, :])
        db_ref[...] = db.astype(BF16)
        _put_tile(tile, db)
        _scatter_to_segments(tile, db_segs)
        ga = _dot_tn(a_ref[...], dyb)
        gb = _dot_tn(b_ref[...], dyb)

        @pl.when(i == 0)
        def _():
            gw_s[0:A_Q_W, :] = ga
            gw_s[A_Q_W:A_Q_W + B_W, :] = gb

        @pl.when(i > 0)
        def _():
            gw_s[0:A_Q_W, :] += ga
            gw_s[A_Q_W:A_Q_W + B_W, :] += gb

        @pl.when(i == n_t - 1)
        def _():
            gw_ref[...] = gw_s[...].astype(BF16)

    tok = lambda w: pl.BlockSpec((tm, w), lambda i: (i, 0))
    whole = pl.BlockSpec(wout.shape, lambda i: (0, 0))
    res = pl.pallas_call(
        body, name=name, grid=(n_t,),
        out_shape=[jax.ShapeDtypeStruct((t, A_Q_W), BF16), jax.ShapeDtypeStruct((t, B_W), BF16),
                   jax.ShapeDtypeStruct(wout.shape, BF16)] + [_seg_shape(t, dil, B_W, BF16) for dil in DILATIONS],
        in_specs=[tok(d), tok(A_Q_W), tok(B_W), whole],
        out_specs=[tok(A_Q_W), tok(B_W), whole] + [_seg_spec(tm, dil, B_W) for dil in DILATIONS],
        scratch_shapes=[pltpu.VMEM(wout.shape, F32)] + _tile_scratch(tm, B_W),
        compiler_params=_cp(("arbitrary",)),
    )(dy, a_out, b_out, wout)
    return res[0], [res[1]] + list(res[3:]), res[2]


def _sum_pattern_grads(per_pattern, name, tm):
    t = per_pattern[0][0].shape[0]
    n_dil = len(DILATIONS)

    def body(*refs):
        ins, outs, tile = _split(refs, 3 * (1 + n_dil), 3)
        for j, o_ref in enumerate(outs):
            _put_tile(tile, ins[j][...].astype(F32))
            for k, dil in enumerate(DILATIONS):
                _gather_from_segments(ins[3 * (1 + k) + j], dil, tile, add=True)
            o_ref[...] = _get_tile(tile).astype(BF16)

    tok = pl.BlockSpec((tm, B_W), lambda i: (i, 0))
    flat = [a if k == 0 else a.reshape(DILATIONS[k - 1], -1, B_W) for k, grads in enumerate(per_pattern) for a in grads]
    return pl.pallas_call(
        body, name=name, grid=(t // tm,),
        out_shape=[jax.ShapeDtypeStruct((t, B_W), BF16)] * 3,
        in_specs=[tok] * 3 + [_seg_spec(tm, dil, B_W) for dil in DILATIONS for _ in range(3)],
        out_specs=[tok] * 3, scratch_shapes=_tile_scratch(tm, B_W),
        compiler_params=_cp(("arbitrary",)),
    )(*flat)


SUB_ROWS = 64
WINDOW_ALIGN = 64


def _sub_band(r0, hw, win, seg_lo, seg_len, t, rel, col):
    ks = pl.multiple_of(jnp.clip(r0 - hw, 0, t - win), WINDOW_ALIGN)
    kpos = col + ks
    valid = (jnp.abs(rel + (ks - r0)) <= hw) & (kpos >= seg_lo) & (kpos < seg_lo + seg_len)
    return ks, valid


def _split_heads(v, low):
    zero = jnp.zeros_like(v)
    return jnp.concatenate([jnp.where(low, v, zero), jnp.where(low, zero, v)], axis=0)


def _kv_spec(kv, t):
    if kv.ndim == 3:
        return pl.BlockSpec((None, t, LANES), lambda p, i: (p // 2, 0, 0))
    return pl.BlockSpec((t, LANES), lambda p, i: (0, p))


def _attn_fwd(q, k, v, sink, name, hw, seg_len, tq, has_sink):
    t, width = q.shape
    sb = min(SUB_ROWS, tq)
    win = 2 * hw + LANES

    def body(sink_ref, q_ref, k_ref, v_ref, o_ref, lse_ref):
        p, i = pl.program_id(0), pl.program_id(1)
        q0 = i * tq
        seg_lo = (q0 // seg_len) * seg_len
        low = lax.broadcasted_iota(jnp.int32, (1, LANES), 1) < HEAD_DIM
        rel = lax.broadcasted_iota(jnp.int32, (sb, win), 1) - lax.broadcasted_iota(jnp.int32, (sb, win), 0)
        col = lax.broadcasted_iota(jnp.int32, (1, win), 1)
        subs = []
        for j in range(tq // sb):
            rows = pl.ds(j * sb, sb)
            ks, valid = _sub_band(q0 + j * sb, hw, win, seg_lo, seg_len, t, rel, col)
            subs.append((rows, ks, valid, _dot_nt(_split_heads(q_ref[rows, :], low), k_ref[pl.ds(ks, win), :])))
        for rows, ks, valid, s in subs:
            vw = v_ref[pl.ds(ks, win), :]
            es, inv, lses = [], [], []
            for a in range(2):
                sa = jnp.where(valid, s[a * sb:(a + 1) * sb], NEG)
                m = jnp.max(sa, axis=1, keepdims=True)
                if has_sink:
                    sk = sink_ref[2 * p + a]
                    m = jnp.maximum(m, sk)
                e = jnp.exp(sa - m)
                den = jnp.sum(e, axis=1, keepdims=True)
                if has_sink:
                    den = den + jnp.exp(sk - m)
                es.append(e.astype(BF16))
                inv.append(1.0 / den)
                lses.append(m + jnp.log(den))
            pv = _dot_nn(jnp.concatenate(es, axis=0), vw)
            o_ref[rows, :] = jnp.where(low, pv[0:sb] * inv[0], pv[sb:2 * sb] * inv[1]).astype(BF16)
            lse_ref[rows, :] = jnp.where(low, lses[0], lses[1])

    tile = pl.BlockSpec((tq, LANES), lambda p, i: (i, p))
    return pl.pallas_call(
        body, name=name, grid=(width // LANES, t // tq),
        out_shape=[jax.ShapeDtypeStruct((t, width), BF16), jax.ShapeDtypeStruct((t, width), F32)],
        in_specs=[pl.BlockSpec(memory_space=pltpu.SMEM), tile, _kv_spec(k, t), _kv_spec(v, t)],
        out_specs=[tile, tile],
        compiler_params=_cp(("arbitrary", "arbitrary")),
    )(sink, q, k, v)


def _attn_bwd(q, k, v, o, do, lse, sink, name, hw, seg_len, tq, has_sink, comm=None, comm_at=None):
    t, width = q.shape
    sb = min(SUB_ROWS, tq)
    win = 2 * hw + LANES
    n_q = t // tq
    shared_kv = k.ndim == 3
    n_ci, n_co, c_in_specs, c_out_specs, c_shapes, c_scratch, c_inputs = _comm_parts(comm)

    def body(*refs):
        ((sink_ref, q_ref, k_ref, v_ref, o_ref, do_ref, lse_ref), c_in, (dq_ref, dk_ref, dv_ref, ds_ref), c_out,
         (dk_s, dv_s), c_scr) = _split(refs, 7, n_ci, 4, n_co, 2)
        p, i = pl.program_id(0), pl.program_id(1)
        if comm:
            _run_hosted(comm, comm_at, p * n_q + i, c_in, c_out, c_scr)
        fresh = (i == 0) & (p % 2 == 0) if shared_kv else i == 0
        last = (i == n_q - 1) & (p % 2 == 1) if shared_kv else i == n_q - 1

        @pl.when(fresh)
        def _():
            dk_s[...] = jnp.zeros_like(dk_s)
            dv_s[...] = jnp.zeros_like(dv_s)

        q0 = i * tq
        seg_lo = (q0 // seg_len) * seg_len
        low = lax.broadcasted_iota(jnp.int32, (1, LANES), 1) < HEAD_DIM
        rel = lax.broadcasted_iota(jnp.int32, (sb, win), 1) - lax.broadcasted_iota(jnp.int32, (sb, win), 0)
        col = lax.broadcasted_iota(jnp.int32, (1, win), 1)
        dsink = [jnp.zeros((1, 1), F32), jnp.zeros((1, 1), F32)]
        subs = []
        for j in range(tq // sb):
            rows = pl.ds(j * sb, sb)
            ks, valid = _sub_band(q0 + j * sb, hw, win, seg_lo, seg_len, t, rel, col)
            kw = k_ref[pl.ds(ks, win), :]
            dov = do_ref[rows, :]
            q2 = _split_heads(q_ref[rows, :], low)
            do2 = _split_heads(dov, low)
            subs.append((rows, ks, valid, kw, dov, q2, do2, _dot_nt(q2, kw), _dot_nt(do2, v_ref[pl.ds(ks, win), :])))
        probs = []
        for rows, ks, valid, kw, dov, q2, do2, s, dpr in subs:
            prod = dov.astype(F32) * o_ref[rows, :].astype(F32)
            lse_t = lse_ref[rows, :]
            prs, dss = [], []
            for a in range(2):
                mine = low if a == 0 else jnp.logical_not(low)
                lse_a = jnp.max(jnp.where(mine, lse_t, -jnp.inf), axis=1, keepdims=True)
                delta = jnp.sum(jnp.where(mine, prod, 0.0), axis=1, keepdims=True)
                pr = jnp.exp(jnp.where(valid, s[a * sb:(a + 1) * sb], NEG) - lse_a)
                prs.append(pr.astype(BF16))
                dss.append((pr * (dpr[a * sb:(a + 1) * sb] - delta)).astype(BF16))
                if has_sink:
                    dsink[a] = dsink[a] - jnp.sum(jnp.exp(sink_ref[2 * p + a] - lse_a) * delta, axis=0, keepdims=True)
            probs.append((jnp.concatenate(prs, axis=0), jnp.concatenate(dss, axis=0)))
        for (rows, ks, valid, kw, dov, q2, do2, s, dpr), (pr2, ds2) in zip(subs, probs):
            dv_s[pl.ds(ks, win), :] += _dot_tn(pr2, do2)
            dk_s[pl.ds(ks, win), :] += _dot_tn(ds2, q2)
            dq2 = _dot_nn(ds2, kw)
            dq_ref[rows, :] = jnp.where(low, dq2[0:sb], dq2[sb:2 * sb]).astype(BF16)
        ds_ref[...] = jnp.broadcast_to(jnp.where(low, dsink[0], dsink[1]), ds_ref.shape)

        @pl.when(last)
        def _():
            dk_ref[...] = dk_s[...].astype(BF16)
            dv_ref[...] = dv_s[...].astype(BF16)

    tile = pl.BlockSpec((tq, LANES), lambda p, i: (i, p))
    kv_shape = jax.ShapeDtypeStruct(k.shape, BF16)
    res = pl.pallas_call(
        body, name=name, grid=(width // LANES, n_q),
        out_shape=[jax.ShapeDtypeStruct((t, width), BF16), kv_shape, kv_shape,
                   jax.ShapeDtypeStruct((width // LANES, n_q, 8, LANES), F32)] + c_shapes,
        in_specs=[pl.BlockSpec(memory_space=pltpu.SMEM), tile, _kv_spec(k, t), _kv_spec(v, t), tile, tile, tile] + c_in_specs,
        out_specs=[tile, _kv_spec(k, t), _kv_spec(v, t), pl.BlockSpec((None, None, 8, LANES), lambda p, i: (p, i, 0, 0))] + c_out_specs,
        scratch_shapes=[pltpu.VMEM((t, LANES), F32)] * 2 + c_scratch,
        compiler_params=_cp(("arbitrary", "arbitrary")),
    )(sink, q, k, v, o, do, lse, *c_inputs)
    return res[:4], res[4:]


def _adamw(w, g, m, v, name):
    def body(w_ref, g_ref, m_ref, v_ref, d_ref, nm_ref, nv_ref):
        gv = g_ref[...]
        nm = ADAM_B1 * m_ref[...] + (1.0 - ADAM_B1) * gv
        nv = ADAM_B2 * v_ref[...] + (1.0 - ADAM_B2) * (gv * gv)
        m_hat = nm / (1.0 - ADAM_B1 ** ADAM_STEP)
        v_hat = nv / (1.0 - ADAM_B2 ** ADAM_STEP)
        d_ref[...] = -ADAM_LR * (m_hat / (jnp.sqrt(v_hat) + ADAM_EPS) + ADAM_WD * w_ref[...])
        nm_ref[...] = nm
        nv_ref[...] = nv

    shape = jax.ShapeDtypeStruct(w.shape, F32)
    return pl.pallas_call(body, name=name, out_shape=[shape, shape, shape], compiler_params=_cp())(w, g, m, v)


def _adamw_rows(land, j, rows, w, m, v, name):
    d = w.shape[1]

    def body(l_ref, w_ref, m_ref, v_ref, g_ref, d_ref, nm_ref, nv_ref):
        gv = l_ref[0].astype(F32)
        for s in range(1, N_DEV):
            gv = gv + l_ref[s].astype(F32)
        g_ref[...] = gv
        nm = ADAM_B1 * m_ref[...] + (1.0 - ADAM_B1) * gv
        nv = ADAM_B2 * v_ref[...] + (1.0 - ADAM_B2) * (gv * gv)
        m_hat = nm / (1.0 - ADAM_B1 ** ADAM_STEP)
        v_hat = nv / (1.0 - ADAM_B2 ** ADAM_STEP)
        d_ref[...] = -ADAM_LR * (m_hat / (jnp.sqrt(v_hat) + ADAM_EPS) + ADAM_WD * w_ref[...])
        nm_ref[...] = nm
        nv_ref[...] = nv

    halves = 2
    half = pl.BlockSpec((rows // halves, d), lambda i: (i, 0))
    shape = jax.ShapeDtypeStruct((rows, d), F32)
    return pl.pallas_call(
        body, name=name, grid=(halves,), out_shape=[shape] * 4,
        in_specs=[pl.BlockSpec((N_DEV, rows // halves, d), lambda i: (0, halves * j + i, 0)), half, half, half],
        out_specs=[half] * 4,
        compiler_params=_cp(("arbitrary",)),
    )(land, w, m, v)


def _sum_small(land, name):
    def body(l_ref, o_ref):
        acc = l_ref[0]
        for s in range(1, N_DEV):
            acc = acc + l_ref[s]
        o_ref[...] = acc

    return pl.pallas_call(body, name=name, out_shape=jax.ShapeDtypeStruct(land.shape[1:], F32), compiler_params=_cp())(land)


def _rope_lanes(positions):
    inv_freq = 1.0 / (ROPE_THETA ** (jnp.arange(0, HEAD_DIM, 2, dtype=F32) / HEAD_DIM))
    ang = positions.astype(F32)[:, None] * inv_freq
    cos, sin = jnp.cos(ang), jnp.sin(ang)
    return jnp.concatenate([cos, cos, cos, cos], axis=1), jnp.concatenate([-sin, sin, -sin, sin], axis=1)


def kernel(x, positions, norm_ffn1, w_gate1, w_up1, w_down1, norm_mix, w_in, a_sink, w_out, norm_ffn2, w_gate2, w_up2, w_down2, norm_final, loss_target, m_norm_ffn1, m_w_gate1, m_w_up1, m_w_down1, m_norm_mix, m_w_in, m_a_sink, m_w_out, m_norm_ffn2, m_w_gate2, m_w_up2, m_w_down2, m_norm_final, v_norm_ffn1, v_w_gate1, v_w_up1, v_w_down1, v_norm_mix, v_w_in, v_a_sink, v_w_out, v_norm_ffn2, v_w_gate2, v_w_up2, v_w_down2, v_norm_final):
    x = x[0]
    target = loss_target[0]
    t, d = x.shape
    tm_mix = min(512, t)
    tq = min(512, t // 16)

    tt = min(1024, t)
    f_all = w_gate1.shape[2] * N_DEV
    tfp = f_all // 2
    tm_fwd, tf_fwd = tm_mix, tfp
    n_steps = (t // tm_fwd) * (f_all // tf_fwd)

    def stacked(shards):
        return jnp.concatenate([s.astype(BF16) for s in shards], axis=0), [s.shape[0] for s in shards]

    packed1, rows1 = stacked([w_gate1[0].T, w_up1[0].T, w_down1[0]])
    packed2, rows2 = stacked([w_in[0].T, w_out[0], w_gate2[0].T, w_up2[0].T, w_down2[0]])
    wg1, wu1, wd1 = _run_alone(_gather_plan(packed1, rows1), "gather_ffn1_weights")

    cos, sin_signed = _rope_lanes(positions[0])
    sink = a_sink[0]
    no_sink = jnp.zeros_like(sink)

    (x1, h1, gp1, up1, hid1), (win, wout, wg2, wu2, wd2) = _ffn_fwd(
        x, norm_ffn1, wg1, wu1, wd1, "ffn1_fwd", tm_fwd, tf_fwd, _gather_plan(packed2, rows2), [0, (3 * n_steps) // 4, n_steps - 1])
    (h_mix, aq, akx, avx, bq, bk, bv), seg_qkv = _in_proj_fwd(x1, norm_mix, win, cos, sin_signed, "in_proj_fwd", tm_mix)
    a_out, a_lse = _attn_fwd(aq, akx, avx, sink, "attn_a_fwd", A_HALF_WINDOW, t, tq, True)
    rows_of = lambda a: a.reshape(t, B_W)
    segments_of = lambda a, dil: a if dil == 1 else a.reshape(dil, t // dil, B_W)
    b_qkv, b_outs, b_lses = [], [], []
    for n, (window, dil) in enumerate(B_PATTERNS):
        qs, ks, vs = (bq, bk, bv) if dil == 1 else (rows_of(seg_qkv[k][n - 1]) for k in range(3))
        o_seg, lse_seg = _attn_fwd(qs, ks, vs, no_sink, f"attn_b{dil}_fwd", window // (2 * dil), t // dil, tq, False)
        b_qkv.append((qs, ks, vs))
        b_outs.append(segments_of(o_seg, dil))
        b_lses.append(segments_of(lse_seg, dil))
    x2, b_out, b_lse = _merge_out_proj_fwd(x1, a_out, b_outs, b_lses, wout, "out_proj_fwd", tm_mix)
    (dx3, dg_final, sq, dout2, h2, gp2, up2, hid2), _ = _ffn_fwd(
        x2, norm_ffn2, wg2, wu2, wd2, "ffn2_fwd_loss", tm_fwd, tf_fwd, loss_head=(norm_final.reshape(1, d), target))


    (dgt2, dut2, gwd2), _ = _ffn_bwd_hidden(dout2, gp2, up2, hid2, wd2, "ffn2_bwd_hidden", tm_mix, tfp)
    (dx2, dg_ffn2), _ = _ffn_bwd_input(dx3, x2, norm_ffn2, dgt2, dut2, wg2, wu2, "ffn2_bwd_input", tm_mix)
    (gwg2, gwu2), _ = _token_products([dgt2, dut2], h2, "ffn2_bwd_gate_up", tt, tfp)

    n_att = (A_Q_W // LANES) * (t // tq)
    n_prod = (f_all // tfp) * (t // tt)
    da, db, gwout = _out_proj_bwd(dx2, a_out, b_out[0], wout, "out_proj_bwd", tm_mix)
    (daq, dakx, davx, dsink_parts), (land_wg2,) = _attn_bwd(
        aq, akx, avx, a_out, da, a_lse, sink, "attn_a_bwd", A_HALF_WINDOW, t, tq, True, _exchange_plan([[gwg2]]), [0, n_att - 1])
    pattern_grads, pattern_lands = [], []
    for n, ((window, dil), (qs, ks, vs)) in enumerate(zip(B_PATTERNS, b_qkv)):
        carried = [gwu2, gwd2][n:n + 1]
        grads, lands = _attn_bwd(qs, ks, vs, rows_of(b_out[n]), rows_of(db[n]), rows_of(b_lse[n]), no_sink, f"attn_b{dil}_bwd",
                                 window // (2 * dil), t // dil, tq, False,
                                 _exchange_plan([carried]) if carried else None, [0, (B_W // LANES) * (t // tq) - 1])
        pattern_grads.append(grads[:3])
        pattern_lands.extend(lands)
    land_wu2, land_wd2 = pattern_lands
    dbq, dbk, dbv = _sum_pattern_grads(pattern_grads, "sum_pattern_grads", tm_mix)
    dx1, dg_mix, gwin, dout1 = _in_proj_bwd(dx2, x1, norm_mix, win, h_mix, cos, sin_signed, daq, dakx, davx, dbq, dbk, dbv, "in_proj_bwd", tm_mix)

    (dgt1, dut1, gwd1), (land_in, land_out) = _ffn_bwd_hidden(
        dout1, gp1, up1, hid1, wd1, "ffn1_bwd_hidden", tm_mix, tfp, _exchange_plan([[gwin], [gwout]]),
        [0, (f_all // tfp) * (t // tm_mix) - 1])
    (gwg1,), (land_wd1,) = _token_products([dgt1], h1, "ffn1_bwd_gate", tt, tfp, _exchange_plan([[gwd1]]), [0, n_prod - 1])
    (gwu1,), (land_wg1,) = _token_products([dut1], h1, "ffn1_bwd_up", tt, tfp, _exchange_plan([[gwg1]]), [0, n_prod - 1])
    (grad_x, dg_ffn1), (land_wu1,) = _ffn_bwd_input(
        dx1, x, norm_ffn1, dgt1, dut1, wg1, wu1, "ffn1_bwd_input", tm_mix, _exchange_plan([[gwu1]]), [0, t // tm_mix - 1])

    dsink_pairs = jnp.sum(dsink_parts[:, :, 0, :], axis=1)
    dsink = jnp.stack([dsink_pairs[:, 0], dsink_pairs[:, HEAD_DIM]], axis=1).reshape(1, -1)
    small = jnp.concatenate([dg_ffn1, dg_mix, dg_ffn2, dg_final, jnp.pad(dsink, ((0, 0), (0, d - dsink.shape[1]))),
                             sq, jnp.zeros((2, d), F32)], axis=0)
    (land_small,) = _run_alone(_exchange_plan([[jnp.tile(small, (N_DEV, 1))]]), "gather_small_gradients")
    red_small = _sum_small(land_small, "sum_small_grads")
    loss = 0.5 * jnp.sum(red_small[5]) / d

    rf = rows1[0]
    sharded = {"w_gate1": (land_wg1, 0, rf, True), "w_up1": (land_wu1, 0, rf, True), "w_down1": (land_wd1, 0, rf, False),
               "w_in": (land_in, 0, rows2[0], True), "w_out": (land_out, 0, rows2[1], False),
               "w_gate2": (land_wg2, 0, rf, True), "w_up2": (land_wu2, 0, rf, True), "w_down2": (land_wd2, 0, rf, False)}
    n_sink = a_sink.shape[1]
    small_grads = {"norm_ffn1": red_small[0:1], "norm_mix": red_small[1:2], "norm_ffn2": red_small[2:3], "norm_final": red_small[3],
                   "a_sink": red_small[4:5, :n_sink]}
    params = {
        "norm_ffn1": (norm_ffn1, m_norm_ffn1, v_norm_ffn1), "w_gate1": (w_gate1, m_w_gate1, v_w_gate1),
        "w_up1": (w_up1, m_w_up1, v_w_up1), "w_down1": (w_down1, m_w_down1, v_w_down1),
        "norm_mix": (norm_mix, m_norm_mix, v_norm_mix), "w_in": (w_in, m_w_in, v_w_in),
        "a_sink": (a_sink, m_a_sink, v_a_sink), "w_out": (w_out, m_w_out, v_w_out),
        "norm_ffn2": (norm_ffn2, m_norm_ffn2, v_norm_ffn2), "w_gate2": (w_gate2, m_w_gate2, v_w_gate2),
        "w_up2": (w_up2, m_w_up2, v_w_up2), "w_down2": (w_down2, m_w_down2, v_w_down2),
        "norm_final": (norm_final, m_norm_final, v_norm_final),
    }
    grad_list, deltas, new_ms, new_vs = [], [], [], []
    for name, (w, m, v) in params.items():
        if name in sharded:
            land, j, rows, is_transposed = sharded[name]
            view = (lambda a: a[0].T) if is_transposed else (lambda a: a[0])
            back = (lambda a: a.T[None]) if is_transposed else (lambda a: a[None])
            outs = [back(o) for o in _adamw_rows(land, j, rows, view(w), view(m), view(v), f"adamw_{name}")]
        else:
            as_block = (lambda a: a.reshape(1, -1)) if w.ndim == 1 else (lambda a: a)
            g = small_grads[name]
            outs = [g] + [o.reshape(w.shape) for o in _adamw(as_block(w), as_block(g), as_block(m), as_block(v), f"adamw_{name}")]
        for lst, o in zip((grad_list, deltas, new_ms, new_vs), outs):
            lst.append(o)
    return (loss, grad_x[None], *grad_list, *deltas, *new_ms, *new_vs)
```

```python
import functools
import itertools

import numpy as np
import jax
import jax.numpy as jnp
from jax import lax
from jax.experimental import pallas as pl
from jax.experimental.pallas import tpu as pltpu

F32 = jnp.float32
BF16 = jnp.bfloat16

N_DEV = 8
HEAD_DIM = 64
LANES = 128
A_Q_W, A_KV_W, B_W = 512, 128, 512
A_HALF_WINDOW = 128
B_PATTERNS = ((128, 1), (512, 4), (2048, 16))
ROPE_THETA = 10000.0
NORM_EPS = 1e-6
FFN_RES_WEIGHT = 0.5
QK_SCALE = HEAD_DIM ** -0.5
NEG = -1e30

ADAM_LR = 0.001
ADAM_B1 = 0.9
ADAM_B2 = 0.999
ADAM_EPS = 1e-08
ADAM_WD = 0.01
ADAM_STEP = 10

MESH_T = pl.DeviceIdType.MESH
VMEM_LIMIT = 60 * 1024 * 1024


def _cp(sem=None, vmem=VMEM_LIMIT):
    return pltpu.CompilerParams(dimension_semantics=sem, vmem_limit_bytes=vmem)


def _dot_nn(a, b):
    return jnp.dot(a, b, preferred_element_type=F32)


def _dot_nt(a, b):
    return lax.dot_general(a, b, (((1,), (1,)), ((), ())), preferred_element_type=F32)


def _dot_tn(a, b):
    return lax.dot_general(a, b, (((0,), (0,)), ((), ())), preferred_element_type=F32)


def _rstd(xv):
    return lax.rsqrt(jnp.mean(xv * xv, axis=-1, keepdims=True) + NORM_EPS)


def _norm_bwd(dh, xv, r, gn):
    gy = dh * gn
    c = jnp.sum(gy * xv, axis=-1, keepdims=True) * (1.0 / xv.shape[-1])
    dx = r * gy - xv * (r * r * r * c)
    dgn = jnp.sum(dh * (xv * r), axis=0, keepdims=True)
    return dx, dgn


def _accumulate(ref, val, first):
    @pl.when(first)
    def _():
        ref[...] = val

    @pl.when(jnp.logical_not(first))
    def _():
        ref[...] += val


def _mesh_pos():
    return lax.axis_index("x"), lax.axis_index("y"), lax.axis_index("c")


def _dev_index(d):
    return 4 * d[0] + 2 * d[1] + d[2]


class _Comm:
    def __init__(self, inputs, out_shape, scratch, phases):
        self.inputs, self.out_shape, self.scratch, self.phases = inputs, out_shape, scratch, phases

    def specs(self):
        any_spec = pl.BlockSpec(memory_space=pl.ANY)
        return [any_spec] * len(self.inputs), [any_spec] * len(self.out_shape)


def _run_alone(comm, name):
    n_in, n_out = len(comm.inputs), len(comm.out_shape)

    def body(*refs):
        for phase in comm.phases:
            phase(refs[:n_in], refs[n_in:n_in + n_out], refs[n_in + n_out:])

    in_specs, out_specs = comm.specs()
    return pl.pallas_call(body, name=name, out_shape=comm.out_shape, in_specs=in_specs, out_specs=out_specs,
                          scratch_shapes=comm.scratch)(*comm.inputs)


def _run_hosted(comm, at, step, ins, outs, scr):
    for phase, when in zip(comm.phases, at):
        @pl.when(step == when)
        def _(phase=phase):
            phase(ins, outs, scr)


def _split(refs, *counts):
    parts, o = [], 0
    for n in counts:
        parts.append(refs[o:o + n])
        o += n
    return parts + [refs[o:]]


def _gather_plan(packed, rows_list):
    n_w = len(rows_list)
    offs = [int(o) for o in np.cumsum([0] + list(rows_list[:-1]))]
    d = packed.shape[1]

    def tools(ins, outs, scr):
        p_ref = ins[0]
        send_sems, recv_sems, local_sem = scr
        x, y, c = _mesh_pos()
        me, sibling = (x, y, c), (x, y, 1 - c)
        chips = [(1 - x, y), (x, 1 - y), (1 - x, 1 - y)]

        def rows(w, dev):
            start = pl.multiple_of(_dev_index(dev) * rows_list[w], 16)
            return outs[w].at[pl.ds(start, rows_list[w]), :]

        def mine(w):
            return p_ref.at[pl.ds(offs[w], rows_list[w]), :]

        def copy(k, w, block, to, own):
            return pltpu.make_async_remote_copy(
                src_ref=mine(w) if own else rows(w, block), dst_ref=rows(w, block),
                send_sem=send_sems.at[k], recv_sem=recv_sems.at[k], device_id=to, device_id_type=MESH_T)

        def all_blocks(k):
            return pltpu.make_async_remote_copy(
                src_ref=p_ref, dst_ref=p_ref, send_sem=send_sems.at[k], recv_sem=recv_sems.at[k],
                device_id=me, device_id_type=MESH_T)

        return p_ref, local_sem, me, sibling, chips, c, rows, mine, copy, all_blocks

    def start(ins, outs, scr):
        _, local_sem, me, sibling, chips, c, rows, mine, copy, _ = tools(ins, outs, scr)
        for w in range(n_w):
            pltpu.make_async_copy(mine(w), rows(w, me), local_sem).start()
        for w in range(n_w):
            copy(0, w, me, sibling, True).start()
        for j, chip in enumerate(chips):
            for w in range(n_w):
                copy(1 + j, w, me, (*chip, c), True).start()

    def relay(ins, outs, scr):
        _, _, _, sibling, chips, c, _, _, copy, all_blocks = tools(ins, outs, scr)
        for j, chip in enumerate(chips):
            all_blocks(1 + j).wait_recv()
            for w in range(n_w):
                copy(4 + j, w, (*chip, c), sibling, False).start()

    def finish(ins, outs, scr):
        p_ref, local_sem, _, _, _, _, _, _, _, all_blocks = tools(ins, outs, scr)
        all_blocks(0).wait_recv()
        for j in range(3):
            all_blocks(4 + j).wait_recv()
        for k in range(7):
            all_blocks(k).wait_send()
        pltpu.make_async_copy(p_ref, p_ref, local_sem).wait()

    return _Comm(
        [packed], [jax.ShapeDtypeStruct((N_DEV * r, d), packed.dtype) for r in rows_list],
        [pltpu.SemaphoreType.DMA((7,)), pltpu.SemaphoreType.DMA((7,)), pltpu.SemaphoreType.DMA], [start, relay, finish])


def _exchange_plan(groups):
    flat = [a for g in groups for a in g]
    n_g = len(groups)
    sizes = [len(g) for g in groups]
    rows = [g[0].shape[0] // N_DEV for g in groups]
    first = [int(o) for o in np.cumsum([0] + sizes[:-1])]

    def start(srcs, lands, scr):
        send_sems, recv_sems, local_sems = scr
        x, y, c = _mesh_pos()
        me = (x, y, c)
        me_idx = _dev_index(me)

        def block(g, i, dev):
            start_row = pl.multiple_of(_dev_index(dev) * rows[g], 8)
            return srcs[first[g] + i].at[pl.ds(start_row, rows[g]), :]

        def slot(g, i):
            return lands[g].at[me_idx, pl.ds(i * rows[g], rows[g]), :]

        for g in range(n_g):
            for i in range(sizes[g]):
                pltpu.make_async_copy(block(g, i, me), slot(g, i), local_sems.at[g]).start()
        flips = [f for f in itertools.product((0, 1), repeat=3) if any(f)]
        for k, (fx, fy, fc) in enumerate(flips):
            peer = (1 - x if fx else x, 1 - y if fy else y, 1 - c if fc else c)
            for g in range(n_g):
                for i in range(sizes[g]):
                    pltpu.make_async_remote_copy(
                        src_ref=block(g, i, peer), dst_ref=slot(g, i), send_sem=send_sems.at[g, k],
                        recv_sem=recv_sems.at[g, k], device_id=peer, device_id_type=MESH_T).start()

    def finish(srcs, lands, scr):
        send_sems, recv_sems, local_sems = scr
        me = _mesh_pos()
        for k in range(7):
            for g in range(n_g):
                pltpu.make_async_remote_copy(
                    src_ref=lands[g].at[0], dst_ref=lands[g].at[0], send_sem=send_sems.at[g, k],
                    recv_sem=recv_sems.at[g, k], device_id=me, device_id_type=MESH_T).wait()
        for g in range(n_g):
            pltpu.make_async_copy(lands[g].at[0], lands[g].at[0], local_sems.at[g]).wait()

    return _Comm(
        flat, [jax.ShapeDtypeStruct((N_DEV, sizes[g] * rows[g], groups[g][0].shape[1]), groups[g][0].dtype) for g in range(n_g)],
        [pltpu.SemaphoreType.DMA((n_g, 7)), pltpu.SemaphoreType.DMA((n_g, 7)), pltpu.SemaphoreType.DMA((n_g,))],
        [start, finish])


def _comm_parts(comm):
    if comm is None:
        return 0, 0, [], [], [], [], []
    in_specs, out_specs = comm.specs()
    return len(comm.inputs), len(comm.out_shape), in_specs, out_specs, comm.out_shape, comm.scratch, comm.inputs


def _ffn_fwd(x, gn, wg_t, wu_t, wd, name, tm, tf, comm=None, comm_at=None, loss_head=None):
    t, d = x.shape
    f_all = wg_t.shape[0]
    n_f = f_all // tf
    n_ci, n_co, c_in_specs, c_out_specs, c_shapes, c_scratch, c_inputs = _comm_parts(comm)
    n_head, n_y = (2, 4) if loss_head else (0, 1)

    def body(*refs):
        ((x_ref, gn_ref, wg_ref, wu_ref, wd_ref), head_in, c_in, y_outs, (h_ref, gp_ref, up_ref, hid_ref), c_out,
         (h_s, hid_s), c_scr) = _split(refs, 5, n_head, n_ci, n_y, 4, n_co, 2)
        f = pl.program_id(1)
        if comm:
            _run_hosted(comm, comm_at, pl.program_id(0) * n_f + f, c_in, c_out, c_scr)

        @pl.when(f == 0)
        def _():
            xv = x_ref[...]
            h = ((xv * _rstd(xv)) * gn_ref[...]).astype(BF16)
            h_s[...] = h
            h_ref[...] = h

        h = h_s[...]
        cols = pl.ds(pl.multiple_of(f * tf, tf), tf)
        g = _dot_nt(h, wg_ref[cols, :])
        u = _dot_nt(h, wu_ref[cols, :])
        sg = jax.nn.sigmoid(g)
        silu = g * sg
        gp_ref[...] = (u * (sg * (1.0 + g * (1.0 - sg)))).astype(BF16)
        up_ref[...] = silu.astype(BF16)
        hid = (silu * u).astype(BF16)
        hid_ref[...] = hid
        for f0 in range(n_f):
            @pl.when(f == f0)
            def _(f0=f0):
                hid_s[:, f0 * tf:(f0 + 1) * tf] = hid

        @pl.when(f == n_f - 1)
        def _():
            y = x_ref[...] + FFN_RES_WEIGHT * _dot_nn(hid_s[...], wd_ref[...])
            if not loss_head:
                y_outs[0][...] = y
            else:
                (gf_ref, tg_ref), (dy_ref, dgf_ref, sq_ref, half_ref) = head_in, y_outs
                gfv, r, first = gf_ref[...], _rstd(y), pl.program_id(0) == 0
                diff = (y * r) * gfv - tg_ref[...]
                dy, dgf = _norm_bwd(diff * (1.0 / d), y, r, gfv)
                dy_ref[...] = dy
                half_ref[...] = (FFN_RES_WEIGHT * dy).astype(BF16)
                _accumulate(dgf_ref, dgf, first)
                _accumulate(sq_ref, jnp.sum(diff * diff, axis=0, keepdims=True), first)

    tok = pl.BlockSpec((tm, d), lambda i, f: (i, 0))
    row = pl.BlockSpec((1, d), lambda i, f: (0, 0))
    whole = pl.BlockSpec((f_all, d), lambda i, f: (0, 0), pipeline_mode=pl.Buffered(1))
    act = pl.BlockSpec((tm, tf), lambda i, f: (i, f))
    act_shape = jax.ShapeDtypeStruct((t, f_all), BF16)
    tok_f32, tok_bf16, row_f32 = (jax.ShapeDtypeStruct((t, d), F32), jax.ShapeDtypeStruct((t, d), BF16),
                                  jax.ShapeDtypeStruct((1, d), F32))
    y_shapes, y_specs = ([tok_f32, row_f32, row_f32, tok_bf16], [tok, row, row, tok]) if loss_head else ([tok_f32], [tok])
    res = pl.pallas_call(
        body, name=name, grid=(t // tm, n_f),
        out_shape=y_shapes + [tok_bf16, act_shape, act_shape, act_shape] + c_shapes,
        in_specs=[tok, row, whole, whole, whole] + ([row, tok] if loss_head else []) + c_in_specs,
        out_specs=y_specs + [tok, act, act, act] + c_out_specs,
        scratch_shapes=[pltpu.VMEM((tm, d), BF16), pltpu.VMEM((tm, f_all), BF16)] + c_scratch,
        compiler_params=_cp(("arbitrary", "arbitrary")),
    )(x, gn, wg_t, wu_t, wd, *(loss_head or ()), *c_inputs)
    return res[:n_y + 4], res[n_y + 4:]


def _ffn_bwd_hidden(dout, gp, up, hid, wd, name, tt, tf, comm=None, comm_at=None):
    t, d = dout.shape
    f_all = wd.shape[0]
    n_t = t // tt
    n_ci, n_co, c_in_specs, c_out_specs, c_shapes, c_scratch, c_inputs = _comm_parts(comm)

    def body(*refs):
        (dout_ref, gp_ref, up_ref, hid_ref, wd_ref), c_in, (dg_ref, du_ref, gwd_ref), c_out, (acc,), c_scr = _split(
            refs, 5, n_ci, 3, n_co, 1)
        s = pl.program_id(1)
        if comm:
            _run_hosted(comm, comm_at, pl.program_id(0) * n_t + s, c_in, c_out, c_scr)
        doutv = dout_ref[...]
        dhid = _dot_nt(doutv, wd_ref[...])
        dg_ref[...] = (dhid * gp_ref[...].astype(F32)).astype(BF16)
        du_ref[...] = (dhid * up_ref[...].astype(F32)).astype(BF16)
        _accumulate(acc, _dot_tn(hid_ref[...], doutv), s == 0)

        @pl.when(s == n_t - 1)
        def _():
            gwd_ref[...] = acc[...].astype(BF16)

    tok = pl.BlockSpec((tt, d), lambda f, s: (s, 0))
    wblk = pl.BlockSpec((tf, d), lambda f, s: (f, 0))
    act = pl.BlockSpec((tt, tf), lambda f, s: (s, f))
    act_shape = jax.ShapeDtypeStruct((t, f_all), BF16)
    res = pl.pallas_call(
        body, name=name, grid=(f_all // tf, n_t),
        out_shape=[act_shape, act_shape, jax.ShapeDtypeStruct((f_all, d), BF16)] + c_shapes,
        in_specs=[tok, act, act, act, wblk] + c_in_specs, out_specs=[act, act, wblk] + c_out_specs,
        scratch_shapes=[pltpu.VMEM((tf, d), F32)] + c_scratch,
        compiler_params=_cp(("arbitrary", "arbitrary")),
    )(dout, gp, up, hid, wd, *c_inputs)
    return res[:3], res[3:]


def _ffn_bwd_input(dy, x, gn, dg, du, wg_t, wu_t, name, tm, comm=None, comm_at=None):
    t, d = x.shape
    f_all = wg_t.shape[0]
    n_ci, n_co, c_in_specs, c_out_specs, c_shapes, c_scratch, c_inputs = _comm_parts(comm)

    def body(*refs):
        (dy_ref, x_ref, gn_ref, dg_ref, du_ref, wg_ref, wu_ref), c_in, (dx_ref, dgn_ref), c_out, c_scr = _split(
            refs, 7, n_ci, 2, n_co)
        i = pl.program_id(0)
        if comm:
            _run_hosted(comm, comm_at, i, c_in, c_out, c_scr)
        dh = _dot_nn(dg_ref[...], wg_ref[...]) + _dot_nn(du_ref[...], wu_ref[...])
        xv = x_ref[...]
        dxn, dgn = _norm_bwd(dh, xv, _rstd(xv), gn_ref[...])
        dx_ref[...] = dy_ref[...] + dxn
        _accumulate(dgn_ref, dgn, i == 0)

    tok = pl.BlockSpec((tm, d), lambda i: (i, 0))
    row = pl.BlockSpec((1, d), lambda i: (0, 0))
    act = pl.BlockSpec((tm, f_all), lambda i: (i, 0))
    whole = pl.BlockSpec((f_all, d), lambda i: (0, 0))
    res = pl.pallas_call(
        body, name=name, grid=(t // tm,),
        out_shape=[jax.ShapeDtypeStruct((t, d), F32), jax.ShapeDtypeStruct((1, d), F32)] + c_shapes,
        in_specs=[tok, tok, row, act, act, whole, whole] + c_in_specs,
        out_specs=[tok, row] + c_out_specs,
        scratch_shapes=c_scratch,
        compiler_params=_cp(("arbitrary",)),
    )(dy, x, gn, dg, du, wg_t, wu_t, *c_inputs)
    return res[:2], res[2:]


def _token_products(lhs_list, rhs, name, tt, tf, comm=None, comm_at=None):
    n_l = len(lhs_list)
    t, f_all = lhs_list[0].shape
    d = rhs.shape[1]
    n_t = t // tt
    n_ci, n_co, c_in_specs, c_out_specs, c_shapes, c_scratch, c_inputs = _comm_parts(comm)

    def body(*refs):
        lhs_refs, (rhs_ref,), c_in, out_refs, c_out, accs, c_scr = _split(refs, n_l, 1, n_ci, n_l, n_co, n_l)
        s = pl.program_id(1)
        if comm:
            _run_hosted(comm, comm_at, pl.program_id(0) * n_t + s, c_in, c_out, c_scr)
        rv = rhs_ref[...]
        for l_ref, acc in zip(lhs_refs, accs):
            _accumulate(acc, _dot_tn(l_ref[...], rv), s == 0)

        @pl.when(s == n_t - 1)
        def _():
            for o_ref, acc in zip(out_refs, accs):
                o_ref[...] = acc[...].astype(BF16)

    act = pl.BlockSpec((tt, tf), lambda f, s: (s, f))
    tok = pl.BlockSpec((tt, d), lambda f, s: (s, 0))
    wblk = pl.BlockSpec((tf, d), lambda f, s: (f, 0))
    res = pl.pallas_call(
        body, name=name, grid=(f_all // tf, n_t),
        out_shape=[jax.ShapeDtypeStruct((f_all, d), BF16)] * n_l + c_shapes,
        in_specs=[act] * n_l + [tok] + c_in_specs, out_specs=[wblk] * n_l + c_out_specs,
        scratch_shapes=[pltpu.VMEM((tf, d), F32)] * n_l + c_scratch,
        compiler_params=_cp(("arbitrary", "arbitrary")),
    )(*lhs_list, rhs, *c_inputs)
    return res[:n_l], res[n_l:]


def _swap_halves(t):
    w = t.shape[-1]
    lane = lax.broadcasted_iota(jnp.int32, (1, w), 1)
    return jnp.where((lane % HEAD_DIM) < HEAD_DIM // 2, pltpu.roll(t, w - HEAD_DIM // 2, 1), pltpu.roll(t, HEAD_DIM // 2, 1))


def _rope(t, cos, sin_signed):
    reps = t.shape[-1] // LANES
    return t * jnp.tile(cos, (1, reps)) + _swap_halves(t) * jnp.tile(sin_signed, (1, reps))


def _rope_bwd(dt, cos, sin_signed):
    reps = dt.shape[-1] // LANES
    return dt * jnp.tile(cos, (1, reps)) + _swap_halves(dt * jnp.tile(sin_signed, (1, reps)))


DILATIONS = tuple(dil for _, dil in B_PATTERNS if dil > 1)


def _seg_shape(t, dil, w, dtype):
    return jax.ShapeDtypeStruct((dil, t // dil, w), dtype)


def _seg_spec(tm, dil, w):
    return pl.BlockSpec((dil, tm // dil, w), lambda i: (0, i, 0))


def _tile_scratch(tm, w):
    return [pltpu.VMEM((tm, LANES), F32)] * (w // LANES)


def _put_tile(tile, val):
    for c, ref in enumerate(tile):
        ref[...] = val[:, c * LANES:(c + 1) * LANES]


def _get_tile(tile):
    return jnp.concatenate([ref[...] for ref in tile], axis=1)


def _scatter_to_segments(tile, seg_refs):
    for seg_ref, dil in zip(seg_refs, DILATIONS):
        rows = tile[0].shape[0] // dil
        for r in range(dil):
            for c, ref in enumerate(tile):
                seg_ref[r, :, c * LANES:(c + 1) * LANES] = ref[pl.ds(r, rows, stride=dil), :].astype(seg_ref.dtype)


def _gather_from_segments(seg_ref, dil, tile, add=False):
    rows = tile[0].shape[0] // dil
    for r in range(dil):
        for c, ref in enumerate(tile):
            idx = (pl.ds(r, rows, stride=dil), slice(None))
            v = seg_ref[r, :, c * LANES:(c + 1) * LANES].astype(F32)
            ref[idx] = ref[idx] + v if add else v


def _in_proj_fwd(x, gn, win_t, cos, sin_signed, name, tm):
    t, d = x.shape
    in_w = win_t.shape[0]
    n_dil = len(DILATIONS)

    def body(x_ref, gn_ref, w_ref, cos_ref, sin_ref, h_ref, aq_ref, akx_ref, avx_ref, bq_ref, bk_ref, bv_ref, *rest):
        seg_refs, tile = rest[:3 * n_dil], rest[3 * n_dil:]
        xv = x_ref[...]
        h = ((xv * _rstd(xv)) * gn_ref[...]).astype(BF16)
        h_ref[...] = h
        p = _dot_nt(h, w_ref[...])
        cs, sn = cos_ref[...], sin_ref[...]
        o = 0
        aq_ref[...] = (_rope(p[:, o:o + A_Q_W], cs, sn) * QK_SCALE).astype(BF16)
        o += A_Q_W
        ak = _rope(p[:, o:o + A_KV_W], cs, sn)
        o += A_KV_W
        av = p[:, o:o + A_KV_W]
        o += A_KV_W
        low = lax.broadcasted_iota(jnp.int32, (1, LANES), 1) < HEAD_DIM
        for src, dst in ((ak, akx_ref), (av, avx_ref)):
            other = pltpu.roll(src, HEAD_DIM, 1)
            dst[0] = jnp.where(low, src, other).astype(BF16)
            dst[1] = jnp.where(low, other, src).astype(BF16)
        for k, nat_ref in enumerate((bq_ref, bk_ref, bv_ref)):
            val = p[:, o:o + B_W]
            o += B_W
            if k < 2:
                val = _rope(val, cs, sn)
            if k == 0:
                val = val * QK_SCALE
            nat_ref[...] = val.astype(BF16)
            _put_tile(tile, val)
            _scatter_to_segments(tile, seg_refs[k * n_dil:(k + 1) * n_dil])

    tok = lambda w: pl.BlockSpec((tm, w), lambda i: (i, 0))
    kvx = pl.BlockSpec((2, tm, LANES), lambda i: (0, i, 0))
    sd = lambda *s: jax.ShapeDtypeStruct(s, BF16)
    res = pl.pallas_call(
        body, name=name, grid=(t // tm,),
        out_shape=[sd(t, d), sd(t, A_Q_W), sd(2, t, LANES), sd(2, t, LANES), sd(t, B_W), sd(t, B_W), sd(t, B_W)]
        + [_seg_shape(t, dil, B_W, BF16) for _ in range(3) for dil in DILATIONS],
        in_specs=[tok(d), pl.BlockSpec((1, d), lambda i: (0, 0)), pl.BlockSpec((in_w, d), lambda i: (0, 0)),
                  tok(LANES), tok(LANES)],
        out_specs=[tok(d), tok(A_Q_W), kvx, kvx, tok(B_W), tok(B_W), tok(B_W)]
        + [_seg_spec(tm, dil, B_W) for _ in range(3) for dil in DILATIONS],
        scratch_shapes=_tile_scratch(tm, B_W),
        compiler_params=_cp(("arbitrary",)),
    )(x, gn, win_t, cos, sin_signed)
    return res[:7], [res[7 + k * n_dil:7 + (k + 1) * n_dil] for k in range(3)]


def _in_proj_bwd(dres, x, gn, win_t, h, cos, sin_signed, daq, dakx, davx, dbq, dbk, dbv, name, tm):
    t, d = x.shape
    in_w = win_t.shape[0]
    n_t = t // tm
    chunk = in_w // 3

    def body(dres_ref, x_ref, gn_ref, w_ref, h_ref, cos_ref, sin_ref, daq_ref, dakx_ref, davx_ref, dbq_ref, dbk_ref,
             dbv_ref, dx_ref, dgn_ref, gw_ref, half_ref, dp_s, gw_s):
        i = pl.program_id(0)
        cs, sn = cos_ref[...], sin_ref[...]
        low = lax.broadcasted_iota(jnp.int32, (1, LANES), 1) < HEAD_DIM

        def fold(ref):
            a, b = ref[0].astype(F32), ref[1].astype(F32)
            return jnp.where(low, a + pltpu.roll(a, HEAD_DIM, 1), b + pltpu.roll(b, HEAD_DIM, 1))

        o = 0
        dp_s[:, o:o + A_Q_W] = _rope_bwd(daq_ref[...].astype(F32) * QK_SCALE, cs, sn).astype(BF16)
        o += A_Q_W
        dp_s[:, o:o + A_KV_W] = _rope_bwd(fold(dakx_ref), cs, sn).astype(BF16)
        o += A_KV_W
        dp_s[:, o:o + A_KV_W] = fold(davx_ref).astype(BF16)
        o += A_KV_W
        dp_s[:, o:o + B_W] = _rope_bwd(dbq_ref[...].astype(F32) * QK_SCALE, cs, sn).astype(BF16)
        o += B_W
        dp_s[:, o:o + B_W] = _rope_bwd(dbk_ref[...].astype(F32), cs, sn).astype(BF16)
        o += B_W
        dp_s[:, o:o + B_W] = dbv_ref[...].astype(BF16)
        dh = _dot_nn(dp_s[...], w_ref[...])
        hv = h_ref[...]
        for c0 in range(0, in_w, chunk):
            _accumulate(gw_s.at[pl.ds(c0, chunk), :], _dot_tn(dp_s[:, c0:c0 + chunk], hv), i == 0)
        xv = x_ref[...]
        dxn, dgn = _norm_bwd(dh, xv, _rstd(xv), gn_ref[...])
        dx = dres_ref[...] + dxn
        dx_ref[...] = dx
        half_ref[...] = (FFN_RES_WEIGHT * dx).astype(BF16)
        _accumulate(dgn_ref, dgn, i == 0)

        @pl.when(i == n_t - 1)
        def _():
            gw_ref[...] = gw_s[...].astype(BF16)

    tok = lambda w: pl.BlockSpec((tm, w), lambda i: (i, 0))
    row = pl.BlockSpec((1, d), lambda i: (0, 0))
    whole = pl.BlockSpec((in_w, d), lambda i: (0, 0))
    kvx = pl.BlockSpec((2, tm, LANES), lambda i: (0, i, 0))
    return pl.pallas_call(
        body, name=name, grid=(n_t,),
        out_shape=[jax.ShapeDtypeStruct((t, d), F32), jax.ShapeDtypeStruct((1, d), F32),
                   jax.ShapeDtypeStruct((in_w, d), BF16), jax.ShapeDtypeStruct((t, d), BF16)],
        in_specs=[tok(d), tok(d), row, whole, tok(d), tok(LANES), tok(LANES), tok(A_Q_W), kvx, kvx,
                  tok(B_W), tok(B_W), tok(B_W)],
        out_specs=[tok(d), row, whole, tok(d)],
        scratch_shapes=[pltpu.VMEM((tm, in_w), BF16), pltpu.VMEM((in_w, d), F32)],
        compiler_params=_cp(("arbitrary",)),
    )(dres, x, gn, win_t, h, cos, sin_signed, daq, dakx, davx, dbq, dbk, dbv)


def _merge_out_proj_fwd(x, a_out, outs, lses, wout, name, tm):
    t, d = x.shape
    n_dil = len(DILATIONS)

    def body(x_ref, a_ref, *rest):
        o_refs, l_refs, (w_ref, y_ref, b_ref, lt_ref), b_segs, lt_segs, scratch = _split(rest, 1 + n_dil, 1 + n_dil, 4, n_dil, n_dil)
        n_c = B_W // LANES
        tiles = [scratch[j * n_c:(j + 1) * n_c] for j in range(2 * n_dil)]
        os_, ls = [o_refs[0][...].astype(F32)], [l_refs[0][...]]
        for k, dil in enumerate(DILATIONS):
            _gather_from_segments(o_refs[1 + k], dil, tiles[2 * k])
            _gather_from_segments(l_refs[1 + k], dil, tiles[2 * k + 1])
            os_.append(_get_tile(tiles[2 * k]))
            ls.append(_get_tile(tiles[2 * k + 1]))
        mx = functools.reduce(jnp.maximum, ls)
        es = [jnp.exp(l - mx) for l in ls]
        den = functools.reduce(jnp.add, es)
        b = functools.reduce(jnp.add, [e * o for e, o in zip(es, os_)]) / den
        lt = mx + jnp.log(den)
        bb = b.astype(BF16)
        b_ref[...] = bb
        lt_ref[...] = lt
        y_ref[...] = x_ref[...] + _dot_nn(a_ref[...], w_ref[0:A_Q_W, :]) + _dot_nn(bb, w_ref[A_Q_W:A_Q_W + B_W, :])
        _put_tile(tiles[0], b)
        _scatter_to_segments(tiles[0], b_segs)
        _put_tile(tiles[1], lt)
        _scatter_to_segments(tiles[1], lt_segs)

    tok = lambda w: pl.BlockSpec((tm, w), lambda i: (i, 0))
    segs = [_seg_spec(tm, dil, B_W) for dil in DILATIONS]
    res = pl.pallas_call(
        body, name=name, grid=(t // tm,),
        out_shape=[jax.ShapeDtypeStruct((t, d), F32), jax.ShapeDtypeStruct((t, B_W), BF16), jax.ShapeDtypeStruct((t, B_W), F32)]
        + [_seg_shape(t, dil, B_W, BF16) for dil in DILATIONS] + [_seg_shape(t, dil, B_W, F32) for dil in DILATIONS],
        in_specs=[tok(d), tok(A_Q_W)] + ([tok(B_W)] + segs) * 2 + [pl.BlockSpec(wout.shape, lambda i: (0, 0))],
        out_specs=[tok(d), tok(B_W), tok(B_W)] + segs * 2,
        scratch_shapes=_tile_scratch(tm, B_W) * (2 * n_dil),
        compiler_params=_cp(("arbitrary",)),
    )(x, a_out, *outs, *lses, wout)
    return res[0], [res[1]] + list(res[3:3 + n_dil]), [res[2]] + list(res[3 + n_dil:])


def _out_proj_bwd(dy, a_out, b_out, wout, name, tm):
    t, d = dy.shape
    n_t = t // tm
    n_dil = len(DILATIONS)

    def body(dy_ref, a_ref, b_ref, w_ref, da_ref, db_ref, gw_ref, *rest):
        db_segs, gw_s, tile = rest[:n_dil], rest[n_dil], rest[n_dil + 1:]
        i = pl.program_id(0)
        dyb = dy_ref[...].astype(BF16)
        da_ref[...] = _dot_nt(dyb, w_ref[0:A_Q_W, :]).astype(BF16)
        db = _dot_nt(dyb, w_ref[A_Q_W:A_Q_W + B_W, :])
        db_ref[...] = db.astype(BF16)
        _put_tile(tile, db)
        _scatter_to_segments(tile, db_segs)
        ga = _dot_tn(a_ref[...], dyb)
        gb = _dot_tn(b_ref[...], dyb)

        @pl.when(i == 0)
        def _():
            gw_s[0:A_Q_W, :] = ga
            gw_s[A_Q_W:A_Q_W + B_W, :] = gb

        @pl.when(i > 0)
        def _():
            gw_s[0:A_Q_W, :] += ga
            gw_s[A_Q_W:A_Q_W + B_W, :] += gb

        @pl.when(i == n_t - 1)
        def _():
            gw_ref[...] = gw_s[...].astype(BF16)

    tok = lambda w: pl.BlockSpec((tm, w), lambda i: (i, 0))
    whole = pl.BlockSpec(wout.shape, lambda i: (0, 0))
    res = pl.pallas_call(
        body, name=name, grid=(n_t,),
        out_shape=[jax.ShapeDtypeStruct((t, A_Q_W), BF16), jax.ShapeDtypeStruct((t, B_W), BF16),
                   jax.ShapeDtypeStruct(wout.shape, BF16)] + [_seg_shape(t, dil, B_W, BF16) for dil in DILATIONS],
        in_specs=[tok(d), tok(A_Q_W), tok(B_W), whole],
        out_specs=[tok(A_Q_W), tok(B_W), whole] + [_seg_spec(tm, dil, B_W) for dil in DILATIONS],
        scratch_shapes=[pltpu.VMEM(wout.shape, F32)] + _tile_scratch(tm, B_W),
        compiler_params=_cp(("arbitrary",)),
    )(dy, a_out, b_out, wout)
    return res[0], [res[1]] + list(res[3:]), res[2]


def _sum_pattern_grads(per_pattern, name, tm):
    t = per_pattern[0][0].shape[0]
    n_dil = len(DILATIONS)

    def body(*refs):
        ins, outs, tile = _split(refs, 3 * (1 + n_dil), 3)
        for j, o_ref in enumerate(outs):
            _put_tile(tile, ins[j][...].astype(F32))
            for k, dil in enumerate(DILATIONS):
                _gather_from_segments(ins[3 * (1 + k) + j], dil, tile, add=True)
            o_ref[...] = _get_tile(tile).astype(BF16)

    tok = pl.BlockSpec((tm, B_W), lambda i: (i, 0))
    flat = [a if k == 0 else a.reshape(DILATIONS[k - 1], -1, B_W) for k, grads in enumerate(per_pattern) for a in grads]
    return pl.pallas_call(
        body, name=name, grid=(t // tm,),
        out_shape=[jax.ShapeDtypeStruct((t, B_W), BF16)] * 3,
        in_specs=[tok] * 3 + [_seg_spec(tm, dil, B_W) for dil in DILATIONS for _ in range(3)],
        out_specs=[tok] * 3, scratch_shapes=_tile_scratch(tm, B_W),
        compiler_params=_cp(("arbitrary",)),
    )(*flat)


SUB_ROWS = 64
WINDOW_ALIGN = 64


def _sub_band(r0, hw, win, seg_lo, seg_len, t, rel, col):
    ks = pl.multiple_of(jnp.clip(r0 - hw, 0, t - win), WINDOW_ALIGN)
    kpos = col + ks
    valid = (jnp.abs(rel + (ks - r0)) <= hw) & (kpos >= seg_lo) & (kpos < seg_lo + seg_len)
    return ks, valid


def _split_heads(v, low):
    zero = jnp.zeros_like(v)
    return jnp.concatenate([jnp.where(low, v, zero), jnp.where(low, zero, v)], axis=0)


def _kv_spec(kv, t):
    if kv.ndim == 3:
        return pl.BlockSpec((None, t, LANES), lambda p, i: (p // 2, 0, 0))
    return pl.BlockSpec((t, LANES), lambda p, i: (0, p))


def _attn_fwd(q, k, v, sink, name, hw, seg_len, tq, has_sink):
    t, width = q.shape
    sb = min(SUB_ROWS, tq)
    win = 2 * hw + LANES

    def body(sink_ref, q_ref, k_ref, v_ref, o_ref, lse_ref):
        p, i = pl.program_id(0), pl.program_id(1)
        q0 = i * tq
        seg_lo = (q0 // seg_len) * seg_len
        low = lax.broadcasted_iota(jnp.int32, (1, LANES), 1) < HEAD_DIM
        rel = lax.broadcasted_iota(jnp.int32, (sb, win), 1) - lax.broadcasted_iota(jnp.int32, (sb, win), 0)
        col = lax.broadcasted_iota(jnp.int32, (1, win), 1)
        subs = []
        for j in range(tq // sb):
            rows = pl.ds(j * sb, sb)
            ks, valid = _sub_band(q0 + j * sb, hw, win, seg_lo, seg_len, t, rel, col)
            subs.append((rows, ks, valid, _dot_nt(_split_heads(q_ref[rows, :], low), k_ref[pl.ds(ks, win), :])))
        for rows, ks, valid, s in subs:
            vw = v_ref[pl.ds(ks, win), :]
            es, inv, lses = [], [], []
            for a in range(2):
                sa = jnp.where(valid, s[a * sb:(a + 1) * sb], NEG)
                m = jnp.max(sa, axis=1, keepdims=True)
                if has_sink:
                    sk = sink_ref[2 * p + a]
                    m = jnp.maximum(m, sk)
                e = jnp.exp(sa - m)
                den = jnp.sum(e, axis=1, keepdims=True)
                if has_sink:
                    den = den + jnp.exp(sk - m)
                es.append(e.astype(BF16))
                inv.append(1.0 / den)
                lses.append(m + jnp.log(den))
            pv = _dot_nn(jnp.concatenate(es, axis=0), vw)
            o_ref[rows, :] = jnp.where(low, pv[0:sb] * inv[0], pv[sb:2 * sb] * inv[1]).astype(BF16)
            lse_ref[rows, :] = jnp.where(low, lses[0], lses[1])

    tile = pl.BlockSpec((tq, LANES), lambda p, i: (i, p))
    return pl.pallas_call(
        body, name=name, grid=(width // LANES, t // tq),
        out_shape=[jax.ShapeDtypeStruct((t, width), BF16), jax.ShapeDtypeStruct((t, width), F32)],
        in_specs=[pl.BlockSpec(memory_space=pltpu.SMEM), tile, _kv_spec(k, t), _kv_spec(v, t)],
        out_specs=[tile, tile],
        compiler_params=_cp(("arbitrary", "arbitrary")),
    )(sink, q, k, v)


def _attn_bwd(q, k, v, o, do, lse, sink, name, hw, seg_len, tq, has_sink, comm=None, comm_at=None):
    t, width = q.shape
    sb = min(SUB_ROWS, tq)
    win = 2 * hw + LANES
    n_q = t // tq
    shared_kv = k.ndim == 3
    n_ci, n_co, c_in_specs, c_out_specs, c_shapes, c_scratch, c_inputs = _comm_parts(comm)

    def body(*refs):
        ((sink_ref, q_ref, k_ref, v_ref, o_ref, do_ref, lse_ref), c_in, (dq_ref, dk_ref, dv_ref, ds_ref), c_out,
         (dk_s, dv_s), c_scr) = _split(refs, 7, n_ci, 4, n_co, 2)
        p, i = pl.program_id(0), pl.program_id(1)
        if comm:
            _run_hosted(comm, comm_at, p * n_q + i, c_in, c_out, c_scr)
        fresh = (i == 0) & (p % 2 == 0) if shared_kv else i == 0
        last = (i == n_q - 1) & (p % 2 == 1) if shared_kv else i == n_q - 1

        @pl.when(fresh)
        def _():
            dk_s[...] = jnp.zeros_like(dk_s)
            dv_s[...] = jnp.zeros_like(dv_s)

        q0 = i * tq
        seg_lo = (q0 // seg_len) * seg_len
        low = lax.broadcasted_iota(jnp.int32, (1, LANES), 1) < HEAD_DIM
        rel = lax.broadcasted_iota(jnp.int32, (sb, win), 1) - lax.broadcasted_iota(jnp.int32, (sb, win), 0)
        col = lax.broadcasted_iota(jnp.int32, (1, win), 1)
        dsink = [jnp.zeros((1, 1), F32), jnp.zeros((1, 1), F32)]
        subs = []
        for j in range(tq // sb):
            rows = pl.ds(j * sb, sb)
            ks, valid = _sub_band(q0 + j * sb, hw, win, seg_lo, seg_len, t, rel, col)
            kw = k_ref[pl.ds(ks, win), :]
            dov = do_ref[rows, :]
            q2 = _split_heads(q_ref[rows, :], low)
            do2 = _split_heads(dov, low)
            subs.append((rows, ks, valid, kw, dov, q2, do2, _dot_nt(q2, kw), _dot_nt(do2, v_ref[pl.ds(ks, win), :])))
        probs = []
        for rows, ks, valid, kw, dov, q2, do2, s, dpr in subs:
            prod = dov.astype(F32) * o_ref[rows, :].astype(F32)
            lse_t = lse_ref[rows, :]
            prs, dss = [], []
            for a in range(2):
                mine = low if a == 0 else jnp.logical_not(low)
                lse_a = jnp.max(jnp.where(mine, lse_t, -jnp.inf), axis=1, keepdims=True)
                delta = jnp.sum(jnp.where(mine, prod, 0.0), axis=1, keepdims=True)
                pr = jnp.exp(jnp.where(valid, s[a * sb:(a + 1) * sb], NEG) - lse_a)
                prs.append(pr.astype(BF16))
                dss.append((pr * (dpr[a * sb:(a + 1) * sb] - delta)).astype(BF16))
                if has_sink:
                    dsink[a] = dsink[a] - jnp.sum(jnp.exp(sink_ref[2 * p + a] - lse_a) * delta, axis=0, keepdims=True)
            probs.append((jnp.concatenate(prs, axis=0), jnp.concatenate(dss, axis=0)))
        for (rows, ks, valid, kw, dov, q2, do2, s, dpr), (pr2, ds2) in zip(subs, probs):
            dv_s[pl.ds(ks, win), :] += _dot_tn(pr2, do2)
            dk_s[pl.ds(ks, win), :] += _dot_tn(ds2, q2)
            dq2 = _dot_nn(ds2, kw)
            dq_ref[rows, :] = jnp.where(low, dq2[0:sb], dq2[sb:2 * sb]).astype(BF16)
        ds_ref[...] = jnp.broadcast_to(jnp.where(low, dsink[0], dsink[1]), ds_ref.shape)

        @pl.when(last)
        def _():
            dk_ref[...] = dk_s[...].astype(BF16)
            dv_ref[...] = dv_s[...].astype(BF16)

    tile = pl.BlockSpec((tq, LANES), lambda p, i: (i, p))
    kv_shape = jax.ShapeDtypeStruct(k.shape, BF16)
    res = pl.pallas_call(
        body, name=name, grid=(width // LANES, n_q),
        out_shape=[jax.ShapeDtypeStruct((t, width), BF16), kv_shape, kv_shape,
                   jax.ShapeDtypeStruct((width // LANES, n_q, 8, LANES), F32)] + c_shapes,
        in_specs=[pl.BlockSpec(memory_space=pltpu.SMEM), tile, _kv_spec(k, t), _kv_spec(v, t), tile, tile, tile] + c_in_specs,
        out_specs=[tile, _kv_spec(k, t), _kv_spec(v, t), pl.BlockSpec((None, None, 8, LANES), lambda p, i: (p, i, 0, 0))] + c_out_specs,
        scratch_shapes=[pltpu.VMEM((t, LANES), F32)] * 2 + c_scratch,
        compiler_params=_cp(("arbitrary", "arbitrary")),
    )(sink, q, k, v, o, do, lse, *c_inputs)
    return res[:4], res[4:]


def _adamw(w, g, m, v, name):
    def body(w_ref, g_ref, m_ref, v_ref, d_ref, nm_ref, nv_ref):
        gv = g_ref[...]
        nm = ADAM_B1 * m_ref[...] + (1.0 - ADAM_B1) * gv
        nv = ADAM_B2 * v_ref[...] + (1.0 - ADAM_B2) * (gv * gv)
        m_hat = nm / (1.0 - ADAM_B1 ** ADAM_STEP)
        v_hat = nv / (1.0 - ADAM_B2 ** ADAM_STEP)
        d_ref[...] = -ADAM_LR * (m_hat / (jnp.sqrt(v_hat) + ADAM_EPS) + ADAM_WD * w_ref[...])
        nm_ref[...] = nm
        nv_ref[...] = nv

    shape = jax.ShapeDtypeStruct(w.shape, F32)
    return pl.pallas_call(body, name=name, out_shape=[shape, shape, shape], compiler_params=_cp())(w, g, m, v)


def _adamw_rows(land, j, rows, w, m, v, name):
    d = w.shape[1]

    def body(l_ref, w_ref, m_ref, v_ref, g_ref, d_ref, nm_ref, nv_ref):
        gv = l_ref[0].astype(F32)
        for s in range(1, N_DEV):
            gv = gv + l_ref[s].astype(F32)
        g_ref[...] = gv
        nm = ADAM_B1 * m_ref[...] + (1.0 - ADAM_B1) * gv
        nv = ADAM_B2 * v_ref[...] + (1.0 - ADAM_B2) * (gv * gv)
        m_hat = nm / (1.0 - ADAM_B1 ** ADAM_STEP)
        v_hat = nv / (1.0 - ADAM_B2 ** ADAM_STEP)
        d_ref[...] = -ADAM_LR * (m_hat / (jnp.sqrt(v_hat) + ADAM_EPS) + ADAM_WD * w_ref[...])
        nm_ref[...] = nm
        nv_ref[...] = nv

    halves = 2
    half = pl.BlockSpec((rows // halves, d), lambda i: (i, 0))
    shape = jax.ShapeDtypeStruct((rows, d), F32)
    return pl.pallas_call(
        body, name=name, grid=(halves,), out_shape=[shape] * 4,
        in_specs=[pl.BlockSpec((N_DEV, rows // halves, d), lambda i: (0, halves * j + i, 0)), half, half, half],
        out_specs=[half] * 4,
        compiler_params=_cp(("arbitrary",)),
    )(land, w, m, v)


def _sum_small(land, name):
    def body(l_ref, o_ref):
        acc = l_ref[0]
        for s in range(1, N_DEV):
            acc = acc + l_ref[s]
        o_ref[...] = acc

    return pl.pallas_call(body, name=name, out_shape=jax.ShapeDtypeStruct(land.shape[1:], F32), compiler_params=_cp())(land)


def _rope_lanes(positions):
    inv_freq = 1.0 / (ROPE_THETA ** (jnp.arange(0, HEAD_DIM, 2, dtype=F32) / HEAD_DIM))
    ang = positions.astype(F32)[:, None] * inv_freq
    cos, sin = jnp.cos(ang), jnp.sin(ang)
    return jnp.concatenate([cos, cos, cos, cos], axis=1), jnp.concatenate([-sin, sin, -sin, sin], axis=1)


def kernel(x, positions, norm_ffn1, w_gate1, w_up1, w_down1, norm_mix, w_in, a_sink, w_out, norm_ffn2, w_gate2, w_up2, w_down2, norm_final, loss_target, m_norm_ffn1, m_w_gate1, m_w_up1, m_w_down1, m_norm_mix, m_w_in, m_a_sink, m_w_out, m_norm_ffn2, m_w_gate2, m_w_up2, m_w_down2, m_norm_final, v_norm_ffn1, v_w_gate1, v_w_up1, v_w_down1, v_norm_mix, v_w_in, v_a_sink, v_w_out, v_norm_ffn2, v_w_gate2, v_w_up2, v_w_down2, v_norm_final):
    x = x[0]
    target = loss_target[0]
    t, d = x.shape
    tm_mix = min(512, t)
    tq_of = lambda seg_len: min(1024, seg_len)

    tt = min(1024, t)
    f_all = w_gate1.shape[2] * N_DEV
    tfp = f_all // 2
    tm_fwd, tf_fwd = tm_mix, tfp
    n_steps = (t // tm_fwd) * (f_all // tf_fwd)

    def stacked(shards):
        return jnp.concatenate([s.astype(BF16) for s in shards], axis=0), [s.shape[0] for s in shards]

    packed1, rows1 = stacked([w_gate1[0].T, w_up1[0].T, w_down1[0]])
    packed2, rows2 = stacked([w_in[0].T, w_out[0], w_gate2[0].T, w_up2[0].T, w_down2[0]])
    wg1, wu1, wd1 = _run_alone(_gather_plan(packed1, rows1), "gather_ffn1_weights")

    cos, sin_signed = _rope_lanes(positions[0])
    sink = a_sink[0]
    no_sink = jnp.zeros_like(sink)

    (x1, h1, gp1, up1, hid1), (win, wout, wg2, wu2, wd2) = _ffn_fwd(
        x, norm_ffn1, wg1, wu1, wd1, "ffn1_fwd", tm_fwd, tf_fwd, _gather_plan(packed2, rows2), [0, (3 * n_steps) // 4, n_steps - 1])
    (h_mix, aq, akx, avx, bq, bk, bv), seg_qkv = _in_proj_fwd(x1, norm_mix, win, cos, sin_signed, "in_proj_fwd", tm_mix)
    a_out, a_lse = _attn_fwd(aq, akx, avx, sink, "attn_a_fwd", A_HALF_WINDOW, t, tq_of(t), True)
    rows_of = lambda a: a.reshape(t, B_W)
    segments_of = lambda a, dil: a if dil == 1 else a.reshape(dil, t // dil, B_W)
    b_qkv, b_outs, b_lses = [], [], []
    for n, (window, dil) in enumerate(B_PATTERNS):
        qs, ks, vs = (bq, bk, bv) if dil == 1 else (rows_of(seg_qkv[k][n - 1]) for k in range(3))
        o_seg, lse_seg = _attn_fwd(qs, ks, vs, no_sink, f"attn_b{dil}_fwd", window // (2 * dil), t // dil, tq_of(t // dil), False)
        b_qkv.append((qs, ks, vs))
        b_outs.append(segments_of(o_seg, dil))
        b_lses.append(segments_of(lse_seg, dil))
    x2, b_out, b_lse = _merge_out_proj_fwd(x1, a_out, b_outs, b_lses, wout, "out_proj_fwd", tm_mix)
    (dx3, dg_final, sq, dout2, h2, gp2, up2, hid2), _ = _ffn_fwd(
        x2, norm_ffn2, wg2, wu2, wd2, "ffn2_fwd_loss", tm_fwd, tf_fwd, loss_head=(norm_final.reshape(1, d), target))


    (dgt2, dut2, gwd2), _ = _ffn_bwd_hidden(dout2, gp2, up2, hid2, wd2, "ffn2_bwd_hidden", tm_mix, tfp)
    (dx2, dg_ffn2), _ = _ffn_bwd_input(dx3, x2, norm_ffn2, dgt2, dut2, wg2, wu2, "ffn2_bwd_input", tm_mix)
    (gwg2, gwu2), _ = _token_products([dgt2, dut2], h2, "ffn2_bwd_gate_up", tt, tfp)

    n_att = lambda width, tq: (width // LANES) * (t // tq)
    n_prod = (f_all // tfp) * (t // tt)
    da, db, gwout = _out_proj_bwd(dx2, a_out, b_out[0], wout, "out_proj_bwd", tm_mix)
    (daq, dakx, davx, dsink_parts), (land_wg2,) = _attn_bwd(
        aq, akx, avx, a_out, da, a_lse, sink, "attn_a_bwd", A_HALF_WINDOW, t, tq_of(t), True,
        _exchange_plan([[gwg2]]), [0, n_att(A_Q_W, tq_of(t)) - 1])
    pattern_grads, pattern_lands = [], []
    for n, ((window, dil), (qs, ks, vs)) in enumerate(zip(B_PATTERNS, b_qkv)):
        carried = [gwu2, gwd2][n:n + 1]
        grads, lands = _attn_bwd(qs, ks, vs, rows_of(b_out[n]), rows_of(db[n]), rows_of(b_lse[n]), no_sink, f"attn_b{dil}_bwd",
                                 window // (2 * dil), t // dil, tq_of(t // dil), False,
                                 _exchange_plan([carried]) if carried else None, [0, n_att(B_W, tq_of(t // dil)) - 1])
        pattern_grads.append(grads[:3])
        pattern_lands.extend(lands)
    land_wu2, land_wd2 = pattern_lands
    dbq, dbk, dbv = _sum_pattern_grads(pattern_grads, "sum_pattern_grads", tm_mix)
    dx1, dg_mix, gwin, dout1 = _in_proj_bwd(dx2, x1, norm_mix, win, h_mix, cos, sin_signed, daq, dakx, davx, dbq, dbk, dbv, "in_proj_bwd", tm_mix)

    (dgt1, dut1, gwd1), (land_in, land_out) = _ffn_bwd_hidden(
        dout1, gp1, up1, hid1, wd1, "ffn1_bwd_hidden", tm_mix, tfp, _exchange_plan([[gwin], [gwout]]),
        [0, (f_all // tfp) * (t // tm_mix) - 1])
    (gwg1,), (land_wd1,) = _token_products([dgt1], h1, "ffn1_bwd_gate", tt, tfp, _exchange_plan([[gwd1]]), [0, n_prod - 1])
    (gwu1,), (land_wg1,) = _token_products([dut1], h1, "ffn1_bwd_up", tt, tfp, _exchange_plan([[gwg1]]), [0, n_prod - 1])
    (grad_x, dg_ffn1), (land_wu1,) = _ffn_bwd_input(
        dx1, x, norm_ffn1, dgt1, dut1, wg1, wu1, "ffn1_bwd_input", tm_mix, _exchange_plan([[gwu1]]), [0, t // tm_mix - 1])

    dsink_pairs = jnp.sum(dsink_parts[:, :, 0, :], axis=1)
    dsink = jnp.stack([dsink_pairs[:, 0], dsink_pairs[:, HEAD_DIM]], axis=1).reshape(1, -1)
    small = jnp.concatenate([dg_ffn1, dg_mix, dg_ffn2, dg_final, jnp.pad(dsink, ((0, 0), (0, d - dsink.shape[1]))),
                             sq, jnp.zeros((2, d), F32)], axis=0)
    (land_small,) = _run_alone(_exchange_plan([[jnp.tile(small, (N_DEV, 1))]]), "gather_small_gradients")
    red_small = _sum_small(land_small, "sum_small_grads")
    loss = 0.5 * jnp.sum(red_small[5]) / d

    rf = rows1[0]
    sharded = {"w_gate1": (land_wg1, 0, rf, True), "w_up1": (land_wu1, 0, rf, True), "w_down1": (land_wd1, 0, rf, False),
               "w_in": (land_in, 0, rows2[0], True), "w_out": (land_out, 0, rows2[1], False),
               "w_gate2": (land_wg2, 0, rf, True), "w_up2": (land_wu2, 0, rf, True), "w_down2": (land_wd2, 0, rf, False)}
    n_sink = a_sink.shape[1]
    small_grads = {"norm_ffn1": red_small[0:1], "norm_mix": red_small[1:2], "norm_ffn2": red_small[2:3], "norm_final": red_small[3],
                   "a_sink": red_small[4:5, :n_sink]}
    params = {
        "norm_ffn1": (norm_ffn1, m_norm_ffn1, v_norm_ffn1), "w_gate1": (w_gate1, m_w_gate1, v_w_gate1),
        "w_up1": (w_up1, m_w_up1, v_w_up1), "w_down1": (w_down1, m_w_down1, v_w_down1),
        "norm_mix": (norm_mix, m_norm_mix, v_norm_mix), "w_in": (w_in, m_w_in, v_w_in),
        "a_sink": (a_sink, m_a_sink, v_a_sink), "w_out": (w_out, m_w_out, v_w_out),
        "norm_ffn2": (norm_ffn2, m_norm_ffn2, v_norm_ffn2), "w_gate2": (w_gate2, m_w_gate2, v_w_gate2),
        "w_up2": (w_up2, m_w_up2, v_w_up2), "w_down2": (w_down2, m_w_down2, v_w_down2),
        "norm_final": (norm_final, m_norm_final, v_norm_final),
    }
    grad_list, deltas, new_ms, new_vs = [], [], [], []
    for name, (w, m, v) in params.items():
        if name in sharded:
            land, j, rows, is_transposed = sharded[name]
            view = (lambda a: a[0].T) if is_transposed else (lambda a: a[0])
            back = (lambda a: a.T[None]) if is_transposed else (lambda a: a[None])
            outs = [back(o) for o in _adamw_rows(land, j, rows, view(w), view(m), view(v), f"adamw_{name}")]
        else:
            as_block = (lambda a: a.reshape(1, -1)) if w.ndim == 1 else (lambda a: a)
            g = small_grads[name]
            outs = [g] + [o.reshape(w.shape) for o in _adamw(as_block(w), as_block(g), as_block(m), as_block(v), f"adamw_{name}")]
        for lst, o in zip((grad_list, deltas, new_ms, new_vs), outs):
            lst.append(o)
    return (loss, grad_x[None], *grad_list, *deltas, *new_ms, *new_vs)
```

```python
import functools
import itertools

import numpy as np
import jax
import jax.numpy as jnp
from jax import lax
from jax.experimental import pallas as pl
from jax.experimental.pallas import tpu as pltpu

F32 = jnp.float32
BF16 = jnp.bfloat16

N_DEV = 8
HEAD_DIM = 64
LANES = 128
A_Q_W, A_KV_W, B_W = 512, 128, 512
A_HALF_WINDOW = 128
B_PATTERNS = ((128, 1), (512, 4), (2048, 16))
ROPE_THETA = 10000.0
NORM_EPS = 1e-6
FFN_RES_WEIGHT = 0.5
QK_SCALE = HEAD_DIM ** -0.5
NEG = -1e30

ADAM_LR = 0.001
ADAM_B1 = 0.9
ADAM_B2 = 0.999
ADAM_EPS = 1e-08
ADAM_WD = 0.01
ADAM_STEP = 10

MESH_T = pl.DeviceIdType.MESH
VMEM_LIMIT = 60 * 1024 * 1024


def _cp(sem=None, vmem=VMEM_LIMIT):
    return pltpu.CompilerParams(dimension_semantics=sem, vmem_limit_bytes=vmem)


def _dot_nn(a, b):
    return jnp.dot(a, b, preferred_element_type=F32)


def _dot_nt(a, b):
    return lax.dot_general(a, b, (((1,), (1,)), ((), ())), preferred_element_type=F32)


def _dot_tn(a, b):
    return lax.dot_general(a, b, (((0,), (0,)), ((), ())), preferred_element_type=F32)


def _rstd(xv):
    return lax.rsqrt(jnp.mean(xv * xv, axis=-1, keepdims=True) + NORM_EPS)


def _norm_bwd(dh, xv, r, gn):
    gy = dh * gn
    c = jnp.sum(gy * xv, axis=-1, keepdims=True) * (1.0 / xv.shape[-1])
    dx = r * gy - xv * (r * r * r * c)
    dgn = jnp.sum(dh * (xv * r), axis=0, keepdims=True)
    return dx, dgn


def _accumulate(ref, val, first):
    @pl.when(first)
    def _():
        ref[...] = val

    @pl.when(jnp.logical_not(first))
    def _():
        ref[...] += val


def _mesh_pos():
    return lax.axis_index("x"), lax.axis_index("y"), lax.axis_index("c")


def _dev_index(d):
    return 4 * d[0] + 2 * d[1] + d[2]


class _Comm:
    def __init__(self, inputs, out_shape, scratch, phases):
        self.inputs, self.out_shape, self.scratch, self.phases = inputs, out_shape, scratch, phases

    def specs(self):
        any_spec = pl.BlockSpec(memory_space=pl.ANY)
        return [any_spec] * len(self.inputs), [any_spec] * len(self.out_shape)


def _run_alone(comm, name):
    n_in, n_out = len(comm.inputs), len(comm.out_shape)

    def body(*refs):
        for phase in comm.phases:
            phase(refs[:n_in], refs[n_in:n_in + n_out], refs[n_in + n_out:])

    in_specs, out_specs = comm.specs()
    return pl.pallas_call(body, name=name, out_shape=comm.out_shape, in_specs=in_specs, out_specs=out_specs,
                          scratch_shapes=comm.scratch)(*comm.inputs)


def _run_hosted(comm, at, step, ins, outs, scr):
    for phase, when in zip(comm.phases, at):
        @pl.when(step == when)
        def _(phase=phase):
            phase(ins, outs, scr)


def _split(refs, *counts):
    parts, o = [], 0
    for n in counts:
        parts.append(refs[o:o + n])
        o += n
    return parts + [refs[o:]]


def _gather_plan(packed, rows_list):
    n_w = len(rows_list)
    offs = [int(o) for o in np.cumsum([0] + list(rows_list[:-1]))]
    d = packed.shape[1]

    def tools(ins, outs, scr):
        p_ref = ins[0]
        send_sems, recv_sems, local_sem = scr
        x, y, c = _mesh_pos()
        me, sibling = (x, y, c), (x, y, 1 - c)
        chips = [(1 - x, y), (x, 1 - y), (1 - x, 1 - y)]

        def rows(w, dev):
            start = pl.multiple_of(_dev_index(dev) * rows_list[w], 16)
            return outs[w].at[pl.ds(start, rows_list[w]), :]

        def mine(w):
            return p_ref.at[pl.ds(offs[w], rows_list[w]), :]

        def copy(k, w, block, to, own):
            return pltpu.make_async_remote_copy(
                src_ref=mine(w) if own else rows(w, block), dst_ref=rows(w, block),
                send_sem=send_sems.at[k], recv_sem=recv_sems.at[k], device_id=to, device_id_type=MESH_T)

        def all_blocks(k):
            return pltpu.make_async_remote_copy(
                src_ref=p_ref, dst_ref=p_ref, send_sem=send_sems.at[k], recv_sem=recv_sems.at[k],
                device_id=me, device_id_type=MESH_T)

        return p_ref, local_sem, me, sibling, chips, c, rows, mine, copy, all_blocks

    def start(ins, outs, scr):
        _, local_sem, me, sibling, chips, c, rows, mine, copy, _ = tools(ins, outs, scr)
        for w in range(n_w):
            pltpu.make_async_copy(mine(w), rows(w, me), local_sem).start()
        for w in range(n_w):
            copy(0, w, me, sibling, True).start()
        for j, chip in enumerate(chips):
            for w in range(n_w):
                copy(1 + j, w, me, (*chip, c), True).start()

    def relay(ins, outs, scr):
        _, _, _, sibling, chips, c, _, _, copy, all_blocks = tools(ins, outs, scr)
        for j, chip in enumerate(chips):
            all_blocks(1 + j).wait_recv()
            for w in range(n_w):
                copy(4 + j, w, (*chip, c), sibling, False).start()

    def finish(ins, outs, scr):
        p_ref, local_sem, _, _, _, _, _, _, _, all_blocks = tools(ins, outs, scr)
        all_blocks(0).wait_recv()
        for j in range(3):
            all_blocks(4 + j).wait_recv()
        for k in range(7):
            all_blocks(k).wait_send()
        pltpu.make_async_copy(p_ref, p_ref, local_sem).wait()

    return _Comm(
        [packed], [jax.ShapeDtypeStruct((N_DEV * r, d), packed.dtype) for r in rows_list],
        [pltpu.SemaphoreType.DMA((7,)), pltpu.SemaphoreType.DMA((7,)), pltpu.SemaphoreType.DMA], [start, relay, finish])


def _exchange_plan(groups):
    flat = [a for g in groups for a in g]
    n_g = len(groups)
    sizes = [len(g) for g in groups]
    rows = [g[0].shape[0] // N_DEV for g in groups]
    first = [int(o) for o in np.cumsum([0] + sizes[:-1])]

    def start(srcs, lands, scr):
        send_sems, recv_sems, local_sems = scr
        x, y, c = _mesh_pos()
        me = (x, y, c)
        me_idx = _dev_index(me)

        def block(g, i, dev):
            start_row = pl.multiple_of(_dev_index(dev) * rows[g], 8)
            return srcs[first[g] + i].at[pl.ds(start_row, rows[g]), :]

        def slot(g, i):
            return lands[g].at[me_idx, pl.ds(i * rows[g], rows[g]), :]

        for g in range(n_g):
            for i in range(sizes[g]):
                pltpu.make_async_copy(block(g, i, me), slot(g, i), local_sems.at[g]).start()
        flips = [f for f in itertools.product((0, 1), repeat=3) if any(f)]
        for k, (fx, fy, fc) in enumerate(flips):
            peer = (1 - x if fx else x, 1 - y if fy else y, 1 - c if fc else c)
            for g in range(n_g):
                for i in range(sizes[g]):
                    pltpu.make_async_remote_copy(
                        src_ref=block(g, i, peer), dst_ref=slot(g, i), send_sem=send_sems.at[g, k],
                        recv_sem=recv_sems.at[g, k], device_id=peer, device_id_type=MESH_T).start()

    def finish(srcs, lands, scr):
        send_sems, recv_sems, local_sems = scr
        me = _mesh_pos()
        for k in range(7):
            for g in range(n_g):
                pltpu.make_async_remote_copy(
                    src_ref=lands[g].at[0], dst_ref=lands[g].at[0], send_sem=send_sems.at[g, k],
                    recv_sem=recv_sems.at[g, k], device_id=me, device_id_type=MESH_T).wait()
        for g in range(n_g):
            pltpu.make_async_copy(lands[g].at[0], lands[g].at[0], local_sems.at[g]).wait()

    return _Comm(
        flat, [jax.ShapeDtypeStruct((N_DEV, sizes[g] * rows[g], groups[g][0].shape[1]), groups[g][0].dtype) for g in range(n_g)],
        [pltpu.SemaphoreType.DMA((n_g, 7)), pltpu.SemaphoreType.DMA((n_g, 7)), pltpu.SemaphoreType.DMA((n_g,))],
        [start, finish])


def _comm_parts(comm):
    if comm is None:
        return 0, 0, [], [], [], [], []
    in_specs, out_specs = comm.specs()
    return len(comm.inputs), len(comm.out_shape), in_specs, out_specs, comm.out_shape, comm.scratch, comm.inputs


def _ffn_fwd(x, gn, wg_t, wu_t, wd, name, tm, tf, comm=None, comm_at=None, loss_head=None):
    t, d = x.shape
    f_all = wg_t.shape[0]
    n_f = f_all // tf
    n_ci, n_co, c_in_specs, c_out_specs, c_shapes, c_scratch, c_inputs = _comm_parts(comm)
    n_head, n_y = (2, 4) if loss_head else (0, 1)

    def body(*refs):
        ((x_ref, gn_ref, wg_ref, wu_ref, wd_ref), head_in, c_in, y_outs, (h_ref, gp_ref, up_ref, hid_ref), c_out,
         (h_s, hid_s), c_scr) = _split(refs, 5, n_head, n_ci, n_y, 4, n_co, 2)
        f = pl.program_id(1)
        if comm:
            _run_hosted(comm, comm_at, pl.program_id(0) * n_f + f, c_in, c_out, c_scr)

        @pl.when(f == 0)
        def _():
            xv = x_ref[...]
            h = ((xv * _rstd(xv)) * gn_ref[...]).astype(BF16)
            h_s[...] = h
            h_ref[...] = h

        h = h_s[...]
        cols = pl.ds(pl.multiple_of(f * tf, tf), tf)
        g = _dot_nt(h, wg_ref[cols, :])
        u = _dot_nt(h, wu_ref[cols, :])
        sg = jax.nn.sigmoid(g)
        silu = g * sg
        gp_ref[...] = (u * (sg * (1.0 + g * (1.0 - sg)))).astype(BF16)
        up_ref[...] = silu.astype(BF16)
        hid = (silu * u).astype(BF16)
        hid_ref[...] = hid
        for f0 in range(n_f):
            @pl.when(f == f0)
            def _(f0=f0):
                hid_s[:, f0 * tf:(f0 + 1) * tf] = hid

        @pl.when(f == n_f - 1)
        def _():
            y = x_ref[...] + FFN_RES_WEIGHT * _dot_nn(hid_s[...], wd_ref[...])
            if not loss_head:
                y_outs[0][...] = y
            else:
                (gf_ref, tg_ref), (dy_ref, dgf_ref, sq_ref, half_ref) = head_in, y_outs
                gfv, r, first = gf_ref[...], _rstd(y), pl.program_id(0) == 0
                diff = (y * r) * gfv - tg_ref[...]
                dy, dgf = _norm_bwd(diff * (1.0 / d), y, r, gfv)
                dy_ref[...] = dy
                half_ref[...] = (FFN_RES_WEIGHT * dy).astype(BF16)
                _accumulate(dgf_ref, dgf, first)
                _accumulate(sq_ref, jnp.sum(diff * diff, axis=0, keepdims=True), first)

    tok = pl.BlockSpec((tm, d), lambda i, f: (i, 0))
    row = pl.BlockSpec((1, d), lambda i, f: (0, 0))
    whole = pl.BlockSpec((f_all, d), lambda i, f: (0, 0), pipeline_mode=pl.Buffered(1))
    act = pl.BlockSpec((tm, tf), lambda i, f: (i, f))
    act_shape = jax.ShapeDtypeStruct((t, f_all), BF16)
    tok_f32, tok_bf16, row_f32 = (jax.ShapeDtypeStruct((t, d), F32), jax.ShapeDtypeStruct((t, d), BF16),
                                  jax.ShapeDtypeStruct((1, d), F32))
    y_shapes, y_specs = ([tok_f32, row_f32, row_f32, tok_bf16], [tok, row, row, tok]) if loss_head else ([tok_f32], [tok])
    res = pl.pallas_call(
        body, name=name, grid=(t // tm, n_f),
        out_shape=y_shapes + [tok_bf16, act_shape, act_shape, act_shape] + c_shapes,
        in_specs=[tok, row, whole, whole, whole] + ([row, tok] if loss_head else []) + c_in_specs,
        out_specs=y_specs + [tok, act, act, act] + c_out_specs,
        scratch_shapes=[pltpu.VMEM((tm, d), BF16), pltpu.VMEM((tm, f_all), BF16)] + c_scratch,
        compiler_params=_cp(("arbitrary", "arbitrary")),
    )(x, gn, wg_t, wu_t, wd, *(loss_head or ()), *c_inputs)
    return res[:n_y + 4], res[n_y + 4:]


def _ffn_bwd_hidden(dout, gp, up, hid, wd, name, tt, tf, comm=None, comm_at=None):
    t, d = dout.shape
    f_all = wd.shape[0]
    n_t = t // tt
    n_ci, n_co, c_in_specs, c_out_specs, c_shapes, c_scratch, c_inputs = _comm_parts(comm)

    def body(*refs):
        (dout_ref, gp_ref, up_ref, hid_ref, wd_ref), c_in, (dg_ref, du_ref, gwd_ref), c_out, (acc,), c_scr = _split(
            refs, 5, n_ci, 3, n_co, 1)
        s = pl.program_id(1)
        if comm:
            _run_hosted(comm, comm_at, pl.program_id(0) * n_t + s, c_in, c_out, c_scr)
        doutv = dout_ref[...]
        dhid = _dot_nt(doutv, wd_ref[...])
        dg_ref[...] = (dhid * gp_ref[...].astype(F32)).astype(BF16)
        du_ref[...] = (dhid * up_ref[...].astype(F32)).astype(BF16)
        _accumulate(acc, _dot_tn(hid_ref[...], doutv), s == 0)

        @pl.when(s == n_t - 1)
        def _():
            gwd_ref[...] = acc[...].astype(BF16)

    tok = pl.BlockSpec((tt, d), lambda f, s: (s, 0))
    wblk = pl.BlockSpec((tf, d), lambda f, s: (f, 0))
    act = pl.BlockSpec((tt, tf), lambda f, s: (s, f))
    act_shape = jax.ShapeDtypeStruct((t, f_all), BF16)
    res = pl.pallas_call(
        body, name=name, grid=(f_all // tf, n_t),
        out_shape=[act_shape, act_shape, jax.ShapeDtypeStruct((f_all, d), BF16)] + c_shapes,
        in_specs=[tok, act, act, act, wblk] + c_in_specs, out_specs=[act, act, wblk] + c_out_specs,
        scratch_shapes=[pltpu.VMEM((tf, d), F32)] + c_scratch,
        compiler_params=_cp(("arbitrary", "arbitrary")),
    )(dout, gp, up, hid, wd, *c_inputs)
    return res[:3], res[3:]


def _ffn_bwd_input(dy, x, gn, dg, du, wg_t, wu_t, name, tm, comm=None, comm_at=None):
    t, d = x.shape
    f_all = wg_t.shape[0]
    n_ci, n_co, c_in_specs, c_out_specs, c_shapes, c_scratch, c_inputs = _comm_parts(comm)

    def body(*refs):
        (dy_ref, x_ref, gn_ref, dg_ref, du_ref, wg_ref, wu_ref), c_in, (dx_ref, dgn_ref), c_out, c_scr = _split(
            refs, 7, n_ci, 2, n_co)
        i = pl.program_id(0)
        if comm:
            _run_hosted(comm, comm_at, i, c_in, c_out, c_scr)
        dh = _dot_nn(dg_ref[...], wg_ref[...]) + _dot_nn(du_ref[...], wu_ref[...])
        xv = x_ref[...]
        dxn, dgn = _norm_bwd(dh, xv, _rstd(xv), gn_ref[...])
        dx_ref[...] = dy_ref[...] + dxn
        _accumulate(dgn_ref, dgn, i == 0)

    tok = pl.BlockSpec((tm, d), lambda i: (i, 0))
    row = pl.BlockSpec((1, d), lambda i: (0, 0))
    act = pl.BlockSpec((tm, f_all), lambda i: (i, 0))
    whole = pl.BlockSpec((f_all, d), lambda i: (0, 0))
    res = pl.pallas_call(
        body, name=name, grid=(t // tm,),
        out_shape=[jax.ShapeDtypeStruct((t, d), F32), jax.ShapeDtypeStruct((1, d), F32)] + c_shapes,
        in_specs=[tok, tok, row, act, act, whole, whole] + c_in_specs,
        out_specs=[tok, row] + c_out_specs,
        scratch_shapes=c_scratch,
        compiler_params=_cp(("arbitrary",)),
    )(dy, x, gn, dg, du, wg_t, wu_t, *c_inputs)
    return res[:2], res[2:]


def _token_products(lhs_list, rhs, name, tt, tf, comm=None, comm_at=None):
    n_l = len(lhs_list)
    t, f_all = lhs_list[0].shape
    d = rhs.shape[1]
    n_t = t // tt
    n_ci, n_co, c_in_specs, c_out_specs, c_shapes, c_scratch, c_inputs = _comm_parts(comm)

    def body(*refs):
        lhs_refs, (rhs_ref,), c_in, out_refs, c_out, accs, c_scr = _split(refs, n_l, 1, n_ci, n_l, n_co, n_l)
        s = pl.program_id(1)
        if comm:
            _run_hosted(comm, comm_at, pl.program_id(0) * n_t + s, c_in, c_out, c_scr)
        rv = rhs_ref[...]
        for l_ref, acc in zip(lhs_refs, accs):
            _accumulate(acc, _dot_tn(l_ref[...], rv), s == 0)

        @pl.when(s == n_t - 1)
        def _():
            for o_ref, acc in zip(out_refs, accs):
                o_ref[...] = acc[...].astype(BF16)

    act = pl.BlockSpec((tt, tf), lambda f, s: (s, f))
    tok = pl.BlockSpec((tt, d), lambda f, s: (s, 0))
    wblk = pl.BlockSpec((tf, d), lambda f, s: (f, 0))
    res = pl.pallas_call(
        body, name=name, grid=(f_all // tf, n_t),
        out_shape=[jax.ShapeDtypeStruct((f_all, d), BF16)] * n_l + c_shapes,
        in_specs=[act] * n_l + [tok] + c_in_specs, out_specs=[wblk] * n_l + c_out_specs,
        scratch_shapes=[pltpu.VMEM((tf, d), F32)] * n_l + c_scratch,
        compiler_params=_cp(("arbitrary", "arbitrary")),
    )(*lhs_list, rhs, *c_inputs)
    return res[:n_l], res[n_l:]


def _swap_halves(t):
    w = t.shape[-1]
    lane = lax.broadcasted_iota(jnp.int32, (1, w), 1)
    return jnp.where((lane % HEAD_DIM) < HEAD_DIM // 2, pltpu.roll(t, w - HEAD_DIM // 2, 1), pltpu.roll(t, HEAD_DIM // 2, 1))


def _rope(t, cos, sin_signed):
    reps = t.shape[-1] // LANES
    return t * jnp.tile(cos, (1, reps)) + _swap_halves(t) * jnp.tile(sin_signed, (1, reps))


def _rope_bwd(dt, cos, sin_signed):
    reps = dt.shape[-1] // LANES
    return dt * jnp.tile(cos, (1, reps)) + _swap_halves(dt * jnp.tile(sin_signed, (1, reps)))


DILATIONS = tuple(dil for _, dil in B_PATTERNS if dil > 1)


def _seg_shape(t, dil, w, dtype):
    return jax.ShapeDtypeStruct((dil, t // dil, w), dtype)


def _seg_spec(tm, dil, w):
    return pl.BlockSpec((dil, tm // dil, w), lambda i: (0, i, 0))


def _tile_scratch(tm, w):
    return [pltpu.VMEM((tm, LANES), F32)] * (w // LANES)


def _put_tile(tile, val):
    for c, ref in enumerate(tile):
        ref[...] = val[:, c * LANES:(c + 1) * LANES]


def _get_tile(tile):
    return jnp.concatenate([ref[...] for ref in tile], axis=1)


def _scatter_to_segments(tile, seg_refs):
    for seg_ref, dil in zip(seg_refs, DILATIONS):
        rows = tile[0].shape[0] // dil
        for r in range(dil):
            for c, ref in enumerate(tile):
                seg_ref[r, :, c * LANES:(c + 1) * LANES] = ref[pl.ds(r, rows, stride=dil), :].astype(seg_ref.dtype)


def _gather_from_segments(seg_ref, dil, tile, add=False):
    rows = tile[0].shape[0] // dil
    for r in range(dil):
        for c, ref in enumerate(tile):
            idx = (pl.ds(r, rows, stride=dil), slice(None))
            v = seg_ref[r, :, c * LANES:(c + 1) * LANES].astype(F32)
            ref[idx] = ref[idx] + v if add else v


def _in_proj_fwd(x, gn, win_t, cos, sin_signed, name, tm):
    t, d = x.shape
    in_w = win_t.shape[0]
    n_dil = len(DILATIONS)

    def body(x_ref, gn_ref, w_ref, cos_ref, sin_ref, h_ref, aq_ref, akx_ref, avx_ref, bq_ref, bk_ref, bv_ref, *rest):
        seg_refs, tile = rest[:3 * n_dil], rest[3 * n_dil:]
        xv = x_ref[...]
        h = ((xv * _rstd(xv)) * gn_ref[...]).astype(BF16)
        h_ref[...] = h
        p = _dot_nt(h, w_ref[...])
        cs, sn = cos_ref[...], sin_ref[...]
        o = 0
        aq_ref[...] = (_rope(p[:, o:o + A_Q_W], cs, sn) * QK_SCALE).astype(BF16)
        o += A_Q_W
        ak = _rope(p[:, o:o + A_KV_W], cs, sn)
        o += A_KV_W
        av = p[:, o:o + A_KV_W]
        o += A_KV_W
        low = lax.broadcasted_iota(jnp.int32, (1, LANES), 1) < HEAD_DIM
        for src, dst in ((ak, akx_ref), (av, avx_ref)):
            other = pltpu.roll(src, HEAD_DIM, 1)
            dst[0] = jnp.where(low, src, other).astype(BF16)
            dst[1] = jnp.where(low, other, src).astype(BF16)
        for k, nat_ref in enumerate((bq_ref, bk_ref, bv_ref)):
            val = p[:, o:o + B_W]
            o += B_W
            if k < 2:
                val = _rope(val, cs, sn)
            if k == 0:
                val = val * QK_SCALE
            nat_ref[...] = val.astype(BF16)
            _put_tile(tile, val)
            _scatter_to_segments(tile, seg_refs[k * n_dil:(k + 1) * n_dil])

    tok = lambda w: pl.BlockSpec((tm, w), lambda i: (i, 0))
    kvx = pl.BlockSpec((2, tm, LANES), lambda i: (0, i, 0))
    sd = lambda *s: jax.ShapeDtypeStruct(s, BF16)
    res = pl.pallas_call(
        body, name=name, grid=(t // tm,),
        out_shape=[sd(t, d), sd(t, A_Q_W), sd(2, t, LANES), sd(2, t, LANES), sd(t, B_W), sd(t, B_W), sd(t, B_W)]
        + [_seg_shape(t, dil, B_W, BF16) for _ in range(3) for dil in DILATIONS],
        in_specs=[tok(d), pl.BlockSpec((1, d), lambda i: (0, 0)), pl.BlockSpec((in_w, d), lambda i: (0, 0)),
                  tok(LANES), tok(LANES)],
        out_specs=[tok(d), tok(A_Q_W), kvx, kvx, tok(B_W), tok(B_W), tok(B_W)]
        + [_seg_spec(tm, dil, B_W) for _ in range(3) for dil in DILATIONS],
        scratch_shapes=_tile_scratch(tm, B_W),
        compiler_params=_cp(("arbitrary",)),
    )(x, gn, win_t, cos, sin_signed)
    return res[:7], [res[7 + k * n_dil:7 + (k + 1) * n_dil] for k in range(3)]


def _in_proj_bwd(dres, x, gn, win_t, h, cos, sin_signed, daq, dakx, davx, dbq, dbk, dbv, name, tm):
    t, d = x.shape
    in_w = win_t.shape[0]
    n_t = t // tm
    chunk = in_w // 3

    def body(dres_ref, x_ref, gn_ref, w_ref, h_ref, cos_ref, sin_ref, daq_ref, dakx_ref, davx_ref, dbq_ref, dbk_ref,
             dbv_ref, dx_ref, dgn_ref, gw_ref, half_ref, dp_s, gw_s):
        i = pl.program_id(0)
        cs, sn = cos_ref[...], sin_ref[...]
        low = lax.broadcasted_iota(jnp.int32, (1, LANES), 1) < HEAD_DIM

        def fold(ref):
            a, b = ref[0].astype(F32), ref[1].astype(F32)
            return jnp.where(low, a + pltpu.roll(a, HEAD_DIM, 1), b + pltpu.roll(b, HEAD_DIM, 1))

        o = 0
        dp_s[:, o:o + A_Q_W] = _rope_bwd(daq_ref[...].astype(F32) * QK_SCALE, cs, sn).astype(BF16)
        o += A_Q_W
        dp_s[:, o:o + A_KV_W] = _rope_bwd(fold(dakx_ref), cs, sn).astype(BF16)
        o += A_KV_W
        dp_s[:, o:o + A_KV_W] = fold(davx_ref).astype(BF16)
        o += A_KV_W
        dp_s[:, o:o + B_W] = _rope_bwd(dbq_ref[...].astype(F32) * QK_SCALE, cs, sn).astype(BF16)
        o += B_W
        dp_s[:, o:o + B_W] = _rope_bwd(dbk_ref[...].astype(F32), cs, sn).astype(BF16)
        o += B_W
        dp_s[:, o:o + B_W] = dbv_ref[...].astype(BF16)
        dh = _dot_nn(dp_s[...], w_ref[...])
        hv = h_ref[...]
        for c0 in range(0, in_w, chunk):
            _accumulate(gw_s.at[pl.ds(c0, chunk), :], _dot_tn(dp_s[:, c0:c0 + chunk], hv), i == 0)
        xv = x_ref[...]
        dxn, dgn = _norm_bwd(dh, xv, _rstd(xv), gn_ref[...])
        dx = dres_ref[...] + dxn
        dx_ref[...] = dx
        half_ref[...] = (FFN_RES_WEIGHT * dx).astype(BF16)
        _accumulate(dgn_ref, dgn, i == 0)

        @pl.when(i == n_t - 1)
        def _():
            gw_ref[...] = gw_s[...].astype(BF16)

    tok = lambda w: pl.BlockSpec((tm, w), lambda i: (i, 0))
    row = pl.BlockSpec((1, d), lambda i: (0, 0))
    whole = pl.BlockSpec((in_w, d), lambda i: (0, 0))
    kvx = pl.BlockSpec((2, tm, LANES), lambda i: (0, i, 0))
    return pl.pallas_call(
        body, name=name, grid=(n_t,),
        out_shape=[jax.ShapeDtypeStruct((t, d), F32), jax.ShapeDtypeStruct((1, d), F32),
                   jax.ShapeDtypeStruct((in_w, d), BF16), jax.ShapeDtypeStruct((t, d), BF16)],
        in_specs=[tok(d), tok(d), row, whole, tok(d), tok(LANES), tok(LANES), tok(A_Q_W), kvx, kvx,
                  tok(B_W), tok(B_W), tok(B_W)],
        out_specs=[tok(d), row, whole, tok(d)],
        scratch_shapes=[pltpu.VMEM((tm, in_w), BF16), pltpu.VMEM((in_w, d), F32)],
        compiler_params=_cp(("arbitrary",)),
    )(dres, x, gn, win_t, h, cos, sin_signed, daq, dakx, davx, dbq, dbk, dbv)


def _merge_out_proj_fwd(x, a_out, outs, lses, wout, name, tm):
    t, d = x.shape
    n_dil = len(DILATIONS)

    def body(x_ref, a_ref, *rest):
        o_refs, l_refs, (w_ref, y_ref, b_ref, lt_ref), b_segs, lt_segs, scratch = _split(rest, 1 + n_dil, 1 + n_dil, 4, n_dil, n_dil)
        n_c = B_W // LANES
        tiles = [scratch[j * n_c:(j + 1) * n_c] for j in range(2 * n_dil)]
        os_, ls = [o_refs[0][...].astype(F32)], [l_refs[0][...]]
        for k, dil in enumerate(DILATIONS):
            _gather_from_segments(o_refs[1 + k], dil, tiles[2 * k])
            _gather_from_segments(l_refs[1 + k], dil, tiles[2 * k + 1])
            os_.append(_get_tile(tiles[2 * k]))
            ls.append(_get_tile(tiles[2 * k + 1]))
        mx = functools.reduce(jnp.maximum, ls)
        es = [jnp.exp(l - mx) for l in ls]
        den = functools.reduce(jnp.add, es)
        b = functools.reduce(jnp.add, [e * o for e, o in zip(es, os_)]) / den
        lt = mx + jnp.log(den)
        bb = b.astype(BF16)
        b_ref[...] = bb
        lt_ref[...] = lt
        y_ref[...] = x_ref[...] + _dot_nn(a_ref[...], w_ref[0:A_Q_W, :]) + _dot_nn(bb, w_ref[A_Q_W:A_Q_W + B_W, :])
        _put_tile(tiles[0], b)
        _scatter_to_segments(tiles[0], b_segs)
        _put_tile(tiles[1], lt)
        _scatter_to_segments(tiles[1], lt_segs)

    tok = lambda w: pl.BlockSpec((tm, w), lambda i: (i, 0))
    segs = [_seg_spec(tm, dil, B_W) for dil in DILATIONS]
    res = pl.pallas_call(
        body, name=name, grid=(t // tm,),
        out_shape=[jax.ShapeDtypeStruct((t, d), F32), jax.ShapeDtypeStruct((t, B_W), BF16), jax.ShapeDtypeStruct((t, B_W), F32)]
        + [_seg_shape(t, dil, B_W, BF16) for dil in DILATIONS] + [_seg_shape(t, dil, B_W, F32) for dil in DILATIONS],
        in_specs=[tok(d), tok(A_Q_W)] + ([tok(B_W)] + segs) * 2 + [pl.BlockSpec(wout.shape, lambda i: (0, 0))],
        out_specs=[tok(d), tok(B_W), tok(B_W)] + segs * 2,
        scratch_shapes=_tile_scratch(tm, B_W) * (2 * n_dil),
        compiler_params=_cp(("arbitrary",)),
    )(x, a_out, *outs, *lses, wout)
    return res[0], [res[1]] + list(res[3:3 + n_dil]), [res[2]] + list(res[3 + n_dil:])


def _out_proj_bwd(dy, a_out, b_out, wout, name, tm):
    t, d = dy.shape
    n_t = t // tm
    n_dil = len(DILATIONS)

    def body(dy_ref, a_ref, b_ref, w_ref, da_ref, db_ref, gw_ref, *rest):
        db_segs, gw_s, tile = rest[:n_dil], rest[n_dil], rest[n_dil + 1:]
        i = pl.program_id(0)
        dyb = dy_ref[...].astype(BF16)
        da_ref[...] = _dot_nt(dyb, w_ref[0:A_Q_W, :]).astype(BF16)
        db = _dot_nt(dyb, w_ref[A_Q_W:A_Q_W + B_W, :])
        db_ref[...] = db.astype(BF16)
        _put_tile(tile, db)
        _scatter_to_segments(tile, db_segs)
        ga = _dot_tn(a_ref[...], dyb)
        gb = _dot_tn(b_ref[...], dyb)

        @pl.when(i == 0)
        def _():
            gw_s[0:A_Q_W, :] = ga
            gw_s[A_Q_W:A_Q_W + B_W, :] = gb

        @pl.when(i > 0)
        def _():
            gw_s[0:A_Q_W, :] += ga
            gw_s[A_Q_W:A_Q_W + B_W, :] += gb

        @pl.when(i == n_t - 1)
        def _():
            gw_ref[...] = gw_s[...].astype(BF16)

    tok = lambda w: pl.BlockSpec((tm, w), lambda i: (i, 0))
    whole = pl.BlockSpec(wout.shape, lambda i: (0, 0))
    res = pl.pallas_call(
        body, name=name, grid=(n_t,),
        out_shape=[jax.ShapeDtypeStruct((t, A_Q_W), BF16), jax.ShapeDtypeStruct((t, B_W), BF16),
                   jax.ShapeDtypeStruct(wout.shape, BF16)] + [_seg_shape(t, dil, B_W, BF16) for dil in DILATIONS],
        in_specs=[tok(d), tok(A_Q_W), tok(B_W), whole],
        out_specs=[tok(A_Q_W), tok(B_W), whole] + [_seg_spec(tm, dil, B_W) for dil in DILATIONS],
        scratch_shapes=[pltpu.VMEM(wout.shape, F32)] + _tile_scratch(tm, B_W),
        compiler_params=_cp(("arbitrary",)),
    )(dy, a_out, b_out, wout)
    return res[0], [res[1]] + list(res[3:]), res[2]


def _sum_pattern_grads(per_pattern, name, tm):
    t = per_pattern[0][0].shape[0]
    n_dil = len(DILATIONS)

    def body(*refs):
        ins, outs, tile = _split(refs, 3 * (1 + n_dil), 3)
        for j, o_ref in enumerate(outs):
            _put_tile(tile, ins[j][...].astype(F32))
            for k, dil in enumerate(DILATIONS):
                _gather_from_segments(ins[3 * (1 + k) + j], dil, tile, add=True)
            o_ref[...] = _get_tile(tile).astype(BF16)

    tok = pl.BlockSpec((tm, B_W), lambda i: (i, 0))
    flat = [a if k == 0 else a.reshape(DILATIONS[k - 1], -1, B_W) for k, grads in enumerate(per_pattern) for a in grads]
    return pl.pallas_call(
        body, name=name, grid=(t // tm,),
        out_shape=[jax.ShapeDtypeStruct((t, B_W), BF16)] * 3,
        in_specs=[tok] * 3 + [_seg_spec(tm, dil, B_W) for dil in DILATIONS for _ in range(3)],
        out_specs=[tok] * 3, scratch_shapes=_tile_scratch(tm, B_W),
        compiler_params=_cp(("arbitrary",)),
    )(*flat)


SUB_ROWS = 64
WINDOW_ALIGN = 64


def _sub_band(r0, hw, win, seg_len, t, rel, col):
    ks = pl.multiple_of(jnp.clip(r0 - hw, 0, t - win), WINDOW_ALIGN)
    kpos = col + ks
    seg_lo = (r0 // seg_len) * seg_len
    valid = (jnp.abs(rel + (ks - r0)) <= hw) & (kpos >= seg_lo) & (kpos < seg_lo + seg_len)
    return ks, valid


def _split_heads(v, low):
    zero = jnp.zeros_like(v)
    return jnp.concatenate([jnp.where(low, v, zero), jnp.where(low, zero, v)], axis=0)


def _kv_spec(kv, t):
    if kv.ndim == 3:
        return pl.BlockSpec((None, t, LANES), lambda p, i: (p // 2, 0, 0))
    return pl.BlockSpec((t, LANES), lambda p, i: (0, p))


def _attn_fwd(q, k, v, sink, name, hw, seg_len, tq, has_sink):
    t, width = q.shape
    sb = min(SUB_ROWS, tq)
    win = 2 * hw + LANES

    def body(sink_ref, q_ref, k_ref, v_ref, o_ref, lse_ref):
        p, i = pl.program_id(0), pl.program_id(1)
        q0 = i * tq
        low = lax.broadcasted_iota(jnp.int32, (1, LANES), 1) < HEAD_DIM
        rel = lax.broadcasted_iota(jnp.int32, (sb, win), 1) - lax.broadcasted_iota(jnp.int32, (sb, win), 0)
        col = lax.broadcasted_iota(jnp.int32, (1, win), 1)
        subs = []
        for j in range(tq // sb):
            rows = pl.ds(j * sb, sb)
            ks, valid = _sub_band(q0 + j * sb, hw, win, seg_len, t, rel, col)
            subs.append((rows, ks, valid, _dot_nt(_split_heads(q_ref[rows, :], low), k_ref[pl.ds(ks, win), :])))
        for rows, ks, valid, s in subs:
            vw = v_ref[pl.ds(ks, win), :]
            es, inv, lses = [], [], []
            for a in range(2):
                sa = jnp.where(valid, s[a * sb:(a + 1) * sb], NEG)
                m = jnp.max(sa, axis=1, keepdims=True)
                if has_sink:
                    sk = sink_ref[2 * p + a]
                    m = jnp.maximum(m, sk)
                e = jnp.exp(sa - m)
                den = jnp.sum(e, axis=1, keepdims=True)
                if has_sink:
                    den = den + jnp.exp(sk - m)
                es.append(e.astype(BF16))
                inv.append(1.0 / den)
                lses.append(m + jnp.log(den))
            pv = _dot_nn(jnp.concatenate(es, axis=0), vw)
            o_ref[rows, :] = jnp.where(low, pv[0:sb] * inv[0], pv[sb:2 * sb] * inv[1]).astype(BF16)
            lse_ref[rows, :] = jnp.where(low, lses[0], lses[1])

    tile = pl.BlockSpec((tq, LANES), lambda p, i: (i, p))
    return pl.pallas_call(
        body, name=name, grid=(width // LANES, t // tq),
        out_shape=[jax.ShapeDtypeStruct((t, width), BF16), jax.ShapeDtypeStruct((t, width), F32)],
        in_specs=[pl.BlockSpec(memory_space=pltpu.SMEM), tile, _kv_spec(k, t), _kv_spec(v, t)],
        out_specs=[tile, tile],
        compiler_params=_cp(("arbitrary", "arbitrary")),
    )(sink, q, k, v)


def _attn_bwd(q, k, v, o, do, lse, sink, name, hw, seg_len, tq, has_sink, comm=None, comm_at=None):
    t, width = q.shape
    sb = min(SUB_ROWS, tq)
    win = 2 * hw + LANES
    n_q = t // tq
    shared_kv = k.ndim == 3
    n_ci, n_co, c_in_specs, c_out_specs, c_shapes, c_scratch, c_inputs = _comm_parts(comm)

    def body(*refs):
        ((sink_ref, q_ref, k_ref, v_ref, o_ref, do_ref, lse_ref), c_in, (dq_ref, dk_ref, dv_ref, ds_ref), c_out,
         (dk_s, dv_s), c_scr) = _split(refs, 7, n_ci, 4, n_co, 2)
        p, i = pl.program_id(0), pl.program_id(1)
        if comm:
            _run_hosted(comm, comm_at, p * n_q + i, c_in, c_out, c_scr)
        fresh = (i == 0) & (p % 2 == 0) if shared_kv else i == 0
        last = (i == n_q - 1) & (p % 2 == 1) if shared_kv else i == n_q - 1

        @pl.when(fresh)
        def _():
            dk_s[...] = jnp.zeros_like(dk_s)
            dv_s[...] = jnp.zeros_like(dv_s)

        q0 = i * tq
        low = lax.broadcasted_iota(jnp.int32, (1, LANES), 1) < HEAD_DIM
        rel = lax.broadcasted_iota(jnp.int32, (sb, win), 1) - lax.broadcasted_iota(jnp.int32, (sb, win), 0)
        col = lax.broadcasted_iota(jnp.int32, (1, win), 1)
        dsink = [jnp.zeros((1, 1), F32), jnp.zeros((1, 1), F32)]
        subs = []
        for j in range(tq // sb):
            rows = pl.ds(j * sb, sb)
            ks, valid = _sub_band(q0 + j * sb, hw, win, seg_len, t, rel, col)
            kw = k_ref[pl.ds(ks, win), :]
            dov = do_ref[rows, :]
            q2 = _split_heads(q_ref[rows, :], low)
            do2 = _split_heads(dov, low)
            subs.append((rows, ks, valid, kw, dov, q2, do2, _dot_nt(q2, kw), _dot_nt(do2, v_ref[pl.ds(ks, win), :])))
        probs = []
        for rows, ks, valid, kw, dov, q2, do2, s, dpr in subs:
            prod = dov.astype(F32) * o_ref[rows, :].astype(F32)
            lse_t = lse_ref[rows, :]
            prs, dss = [], []
            for a in range(2):
                mine = low if a == 0 else jnp.logical_not(low)
                lse_a = jnp.max(jnp.where(mine, lse_t, -jnp.inf), axis=1, keepdims=True)
                delta = jnp.sum(jnp.where(mine, prod, 0.0), axis=1, keepdims=True)
                pr = jnp.exp(jnp.where(valid, s[a * sb:(a + 1) * sb], NEG) - lse_a)
                prs.append(pr.astype(BF16))
                dss.append((pr * (dpr[a * sb:(a + 1) * sb] - delta)).astype(BF16))
                if has_sink:
                    dsink[a] = dsink[a] - jnp.sum(jnp.exp(sink_ref[2 * p + a] - lse_a) * delta, axis=0, keepdims=True)
            probs.append((jnp.concatenate(prs, axis=0), jnp.concatenate(dss, axis=0)))
        for (rows, ks, valid, kw, dov, q2, do2, s, dpr), (pr2, ds2) in zip(subs, probs):
            dv_s[pl.ds(ks, win), :] += _dot_tn(pr2, do2)
            dk_s[pl.ds(ks, win), :] += _dot_tn(ds2, q2)
            dq2 = _dot_nn(ds2, kw)
            dq_ref[rows, :] = jnp.where(low, dq2[0:sb], dq2[sb:2 * sb]).astype(BF16)
        ds_ref[...] = jnp.broadcast_to(jnp.where(low, dsink[0], dsink[1]), ds_ref.shape)

        @pl.when(last)
        def _():
            dk_ref[...] = dk_s[...].astype(BF16)
            dv_ref[...] = dv_s[...].astype(BF16)

    tile = pl.BlockSpec((tq, LANES), lambda p, i: (i, p))
    kv_shape = jax.ShapeDtypeStruct(k.shape, BF16)
    res = pl.pallas_call(
        body, name=name, grid=(width // LANES, n_q),
        out_shape=[jax.ShapeDtypeStruct((t, width), BF16), kv_shape, kv_shape,
                   jax.ShapeDtypeStruct((width // LANES, n_q, 8, LANES), F32)] + c_shapes,
        in_specs=[pl.BlockSpec(memory_space=pltpu.SMEM), tile, _kv_spec(k, t), _kv_spec(v, t), tile, tile, tile] + c_in_specs,
        out_specs=[tile, _kv_spec(k, t), _kv_spec(v, t), pl.BlockSpec((None, None, 8, LANES), lambda p, i: (p, i, 0, 0))] + c_out_specs,
        scratch_shapes=[pltpu.VMEM((t, LANES), F32)] * 2 + c_scratch,
        compiler_params=_cp(("arbitrary", "arbitrary")),
    )(sink, q, k, v, o, do, lse, *c_inputs)
    return res[:4], res[4:]


def _adamw(w, g, m, v, name):
    def body(w_ref, g_ref, m_ref, v_ref, d_ref, nm_ref, nv_ref):
        gv = g_ref[...]
        nm = ADAM_B1 * m_ref[...] + (1.0 - ADAM_B1) * gv
        nv = ADAM_B2 * v_ref[...] + (1.0 - ADAM_B2) * (gv * gv)
        m_hat = nm / (1.0 - ADAM_B1 ** ADAM_STEP)
        v_hat = nv / (1.0 - ADAM_B2 ** ADAM_STEP)
        d_ref[...] = -ADAM_LR * (m_hat / (jnp.sqrt(v_hat) + ADAM_EPS) + ADAM_WD * w_ref[...])
        nm_ref[...] = nm
        nv_ref[...] = nv

    shape = jax.ShapeDtypeStruct(w.shape, F32)
    return pl.pallas_call(body, name=name, out_shape=[shape, shape, shape], compiler_params=_cp())(w, g, m, v)


def _adamw_rows(land, j, rows, w, m, v, name):
    d = w.shape[1]

    def body(l_ref, w_ref, m_ref, v_ref, g_ref, d_ref, nm_ref, nv_ref):
        gv = l_ref[0].astype(F32)
        for s in range(1, N_DEV):
            gv = gv + l_ref[s].astype(F32)
        g_ref[...] = gv
        nm = ADAM_B1 * m_ref[...] + (1.0 - ADAM_B1) * gv
        nv = ADAM_B2 * v_ref[...] + (1.0 - ADAM_B2) * (gv * gv)
        m_hat = nm / (1.0 - ADAM_B1 ** ADAM_STEP)
        v_hat = nv / (1.0 - ADAM_B2 ** ADAM_STEP)
        d_ref[...] = -ADAM_LR * (m_hat / (jnp.sqrt(v_hat) + ADAM_EPS) + ADAM_WD * w_ref[...])
        nm_ref[...] = nm
        nv_ref[...] = nv

    halves = 2
    half = pl.BlockSpec((rows // halves, d), lambda i: (i, 0))
    shape = jax.ShapeDtypeStruct((rows, d), F32)
    return pl.pallas_call(
        body, name=name, grid=(halves,), out_shape=[shape] * 4,
        in_specs=[pl.BlockSpec((N_DEV, rows // halves, d), lambda i: (0, halves * j + i, 0)), half, half, half],
        out_specs=[half] * 4,
        compiler_params=_cp(("arbitrary",)),
    )(land, w, m, v)


def _sum_small(land, name):
    def body(l_ref, o_ref):
        acc = l_ref[0]
        for s in range(1, N_DEV):
            acc = acc + l_ref[s]
        o_ref[...] = acc

    return pl.pallas_call(body, name=name, out_shape=jax.ShapeDtypeStruct(land.shape[1:], F32), compiler_params=_cp())(land)


def _rope_lanes(positions):
    inv_freq = 1.0 / (ROPE_THETA ** (jnp.arange(0, HEAD_DIM, 2, dtype=F32) / HEAD_DIM))
    ang = positions.astype(F32)[:, None] * inv_freq
    cos, sin = jnp.cos(ang), jnp.sin(ang)
    return jnp.concatenate([cos, cos, cos, cos], axis=1), jnp.concatenate([-sin, sin, -sin, sin], axis=1)


def kernel(x, positions, norm_ffn1, w_gate1, w_up1, w_down1, norm_mix, w_in, a_sink, w_out, norm_ffn2, w_gate2, w_up2, w_down2, norm_final, loss_target, m_norm_ffn1, m_w_gate1, m_w_up1, m_w_down1, m_norm_mix, m_w_in, m_a_sink, m_w_out, m_norm_ffn2, m_w_gate2, m_w_up2, m_w_down2, m_norm_final, v_norm_ffn1, v_w_gate1, v_w_up1, v_w_down1, v_norm_mix, v_w_in, v_a_sink, v_w_out, v_norm_ffn2, v_w_gate2, v_w_up2, v_w_down2, v_norm_final):
    x = x[0]
    target = loss_target[0]
    t, d = x.shape
    tm_mix = min(512, t)
    tq_of = lambda seg_len, most=2048: min(most, t, max(seg_len, 16 * SUB_ROWS))

    tt = min(1024, t)
    f_all = w_gate1.shape[2] * N_DEV
    tfp = f_all // 2
    tm_fwd, tf_fwd = tm_mix, tfp
    n_steps = (t // tm_fwd) * (f_all // tf_fwd)

    def stacked(shards):
        return jnp.concatenate([s.astype(BF16) for s in shards], axis=0), [s.shape[0] for s in shards]

    packed1, rows1 = stacked([w_gate1[0].T, w_up1[0].T, w_down1[0]])
    packed2, rows2 = stacked([w_in[0].T, w_out[0], w_gate2[0].T, w_up2[0].T, w_down2[0]])
    wg1, wu1, wd1 = _run_alone(_gather_plan(packed1, rows1), "gather_ffn1_weights")

    cos, sin_signed = _rope_lanes(positions[0])
    sink = a_sink[0]
    no_sink = jnp.zeros_like(sink)

    (x1, h1, gp1, up1, hid1), (win, wout, wg2, wu2, wd2) = _ffn_fwd(
        x, norm_ffn1, wg1, wu1, wd1, "ffn1_fwd", tm_fwd, tf_fwd, _gather_plan(packed2, rows2), [0, (3 * n_steps) // 4, n_steps - 1])
    (h_mix, aq, akx, avx, bq, bk, bv), seg_qkv = _in_proj_fwd(x1, norm_mix, win, cos, sin_signed, "in_proj_fwd", tm_mix)
    a_out, a_lse = _attn_fwd(aq, akx, avx, sink, "attn_a_fwd", A_HALF_WINDOW, t, tq_of(t), True)
    rows_of = lambda a: a.reshape(t, B_W)
    segments_of = lambda a, dil: a if dil == 1 else a.reshape(dil, t // dil, B_W)
    b_qkv, b_outs, b_lses = [], [], []
    for n, (window, dil) in enumerate(B_PATTERNS):
        qs, ks, vs = (bq, bk, bv) if dil == 1 else (rows_of(seg_qkv[k][n - 1]) for k in range(3))
        o_seg, lse_seg = _attn_fwd(qs, ks, vs, no_sink, f"attn_b{dil}_fwd", window // (2 * dil), t // dil,
                                  tq_of(t // dil, 16 * SUB_ROWS), False)
        b_qkv.append((qs, ks, vs))
        b_outs.append(segments_of(o_seg, dil))
        b_lses.append(segments_of(lse_seg, dil))
    x2, b_out, b_lse = _merge_out_proj_fwd(x1, a_out, b_outs, b_lses, wout, "out_proj_fwd", tm_mix)
    (dx3, dg_final, sq, dout2, h2, gp2, up2, hid2), _ = _ffn_fwd(
        x2, norm_ffn2, wg2, wu2, wd2, "ffn2_fwd_loss", tm_fwd, tf_fwd, loss_head=(norm_final.reshape(1, d), target))


    (dgt2, dut2, gwd2), _ = _ffn_bwd_hidden(dout2, gp2, up2, hid2, wd2, "ffn2_bwd_hidden", tm_mix, tfp)
    (dx2, dg_ffn2), _ = _ffn_bwd_input(dx3, x2, norm_ffn2, dgt2, dut2, wg2, wu2, "ffn2_bwd_input", tm_mix)
    (gwg2, gwu2), _ = _token_products([dgt2, dut2], h2, "ffn2_bwd_gate_up", tt, tfp)

    n_att = lambda width, tq: (width // LANES) * (t // tq)
    n_prod = (f_all // tfp) * (t // tt)
    da, db, gwout = _out_proj_bwd(dx2, a_out, b_out[0], wout, "out_proj_bwd", tm_mix)
    (daq, dakx, davx, dsink_parts), (land_wg2,) = _attn_bwd(
        aq, akx, avx, a_out, da, a_lse, sink, "attn_a_bwd", A_HALF_WINDOW, t, tq_of(t), True,
        _exchange_plan([[gwg2]]), [0, n_att(A_Q_W, tq_of(t)) - 1])
    pattern_grads, pattern_lands = [], []
    for n, ((window, dil), (qs, ks, vs)) in enumerate(zip(B_PATTERNS, b_qkv)):
        carried = [gwu2, gwd2][n:n + 1]
        grads, lands = _attn_bwd(qs, ks, vs, rows_of(b_out[n]), rows_of(db[n]), rows_of(b_lse[n]), no_sink, f"attn_b{dil}_bwd",
                                 window // (2 * dil), t // dil, tq_of(t // dil), False,
                                 _exchange_plan([carried]) if carried else None, [0, n_att(B_W, tq_of(t // dil)) - 1])
        pattern_grads.append(grads[:3])
        pattern_lands.extend(lands)
    land_wu2, land_wd2 = pattern_lands
    dbq, dbk, dbv = _sum_pattern_grads(pattern_grads, "sum_pattern_grads", tm_mix)
    dx1, dg_mix, gwin, dout1 = _in_proj_bwd(dx2, x1, norm_mix, win, h_mix, cos, sin_signed, daq, dakx, davx, dbq, dbk, dbv, "in_proj_bwd", tm_mix)

    (dgt1, dut1, gwd1), (land_in, land_out) = _ffn_bwd_hidden(
        dout1, gp1, up1, hid1, wd1, "ffn1_bwd_hidden", tm_mix, tfp, _exchange_plan([[gwin], [gwout]]),
        [0, (f_all // tfp) * (t // tm_mix) - 1])
    (gwg1,), (land_wd1,) = _token_products([dgt1], h1, "ffn1_bwd_gate", tt, tfp, _exchange_plan([[gwd1]]), [0, n_prod - 1])
    (gwu1,), (land_wg1,) = _token_products([dut1], h1, "ffn1_bwd_up", tt, tfp, _exchange_plan([[gwg1]]), [0, n_prod - 1])
    (grad_x, dg_ffn1), (land_wu1,) = _ffn_bwd_input(
        dx1, x, norm_ffn1, dgt1, dut1, wg1, wu1, "ffn1_bwd_input", tm_mix, _exchange_plan([[gwu1]]), [0, t // tm_mix - 1])

    dsink_pairs = jnp.sum(dsink_parts[:, :, 0, :], axis=1)
    dsink = jnp.stack([dsink_pairs[:, 0], dsink_pairs[:, HEAD_DIM]], axis=1).reshape(1, -1)
    small = jnp.concatenate([dg_ffn1, dg_mix, dg_ffn2, dg_final, jnp.pad(dsink, ((0, 0), (0, d - dsink.shape[1]))),
                             sq, jnp.zeros((2, d), F32)], axis=0)
    (land_small,) = _run_alone(_exchange_plan([[jnp.tile(small, (N_DEV, 1))]]), "gather_small_gradients")
    red_small = _sum_small(land_small, "sum_small_grads")
    loss = 0.5 * jnp.sum(red_small[5]) / d

    rf = rows1[0]
    sharded = {"w_gate1": (land_wg1, 0, rf, True), "w_up1": (land_wu1, 0, rf, True), "w_down1": (land_wd1, 0, rf, False),
               "w_in": (land_in, 0, rows2[0], True), "w_out": (land_out, 0, rows2[1], False),
               "w_gate2": (land_wg2, 0, rf, True), "w_up2": (land_wu2, 0, rf, True), "w_down2": (land_wd2, 0, rf, False)}
    n_sink = a_sink.shape[1]
    small_grads = {"norm_ffn1": red_small[0:1], "norm_mix": red_small[1:2], "norm_ffn2": red_small[2:3], "norm_final": red_small[3],
                   "a_sink": red_small[4:5, :n_sink]}
    params = {
        "norm_ffn1": (norm_ffn1, m_norm_ffn1, v_norm_ffn1), "w_gate1": (w_gate1, m_w_gate1, v_w_gate1),
        "w_up1": (w_up1, m_w_up1, v_w_up1), "w_down1": (w_down1, m_w_down1, v_w_down1),
        "norm_mix": (norm_mix, m_norm_mix, v_norm_mix), "w_in": (w_in, m_w_in, v_w_in),
        "a_sink": (a_sink, m_a_sink, v_a_sink), "w_out": (w_out, m_w_out, v_w_out),
        "norm_ffn2": (norm_ffn2, m_norm_ffn2, v_norm_ffn2), "w_gate2": (w_gate2, m_w_gate2, v_w_gate2),
        "w_up2": (w_up2, m_w_up2, v_w_up2), "w_down2": (w_down2, m_w_down2, v_w_down2),
        "norm_final": (norm_final, m_norm_final, v_norm_final),
    }
    grad_list, deltas, new_ms, new_vs = [], [], [], []
    for name, (w, m, v) in params.items():
        if name in sharded:
            land, j, rows, is_transposed = sharded[name]
            view = (lambda a: a[0].T) if is_transposed else (lambda a: a[0])
            back = (lambda a: a.T[None]) if is_transposed else (lambda a: a[None])
            outs = [back(o) for o in _adamw_rows(land, j, rows, view(w), view(m), view(v), f"adamw_{name}")]
        else:
            as_block = (lambda a: a.reshape(1, -1)) if w.ndim == 1 else (lambda a: a)
            g = small_grads[name]
            outs = [g] + [o.reshape(w.shape) for o in _adamw(as_block(w), as_block(g), as_block(m), as_block(v), f"adamw_{name}")]
        for lst, o in zip((grad_list, deltas, new_ms, new_vs), outs):
            lst.append(o)
    return (loss, grad_x[None], *grad_list, *deltas, *new_ms, *new_vs)
```

```python
import functools
import itertools

import numpy as np
import jax
import jax.numpy as jnp
from jax import lax
from jax.experimental import pallas as pl
from jax.experimental.pallas import tpu as pltpu

F32 = jnp.float32
BF16 = jnp.bfloat16

N_DEV = 8
HEAD_DIM = 64
LANES = 128
A_Q_W, A_KV_W, B_W = 512, 128, 512
A_HALF_WINDOW = 128
B_PATTERNS = ((128, 1), (512, 4), (2048, 16))
ROPE_THETA = 10000.0
NORM_EPS = 1e-6
FFN_RES_WEIGHT = 0.5
QK_SCALE = HEAD_DIM ** -0.5
NEG = -1e30

ADAM_LR = 0.001
ADAM_B1 = 0.9
ADAM_B2 = 0.999
ADAM_EPS = 1e-08
ADAM_WD = 0.01
ADAM_STEP = 10

MESH_T = pl.DeviceIdType.MESH
VMEM_LIMIT = 60 * 1024 * 1024


def _cp(sem=None, vmem=VMEM_LIMIT):
    return pltpu.CompilerParams(dimension_semantics=sem, vmem_limit_bytes=vmem)


def _dot_nn(a, b):
    return jnp.dot(a, b, preferred_element_type=F32)


def _dot_nt(a, b):
    return lax.dot_general(a, b, (((1,), (1,)), ((), ())), preferred_element_type=F32)


def _dot_tn(a, b):
    return lax.dot_general(a, b, (((0,), (0,)), ((), ())), preferred_element_type=F32)


def _rstd(xv):
    return lax.rsqrt(jnp.mean(xv * xv, axis=-1, keepdims=True) + NORM_EPS)


def _norm_bwd(dh, xv, r, gn):
    gy = dh * gn
    c = jnp.sum(gy * xv, axis=-1, keepdims=True) * (1.0 / xv.shape[-1])
    dx = r * gy - xv * (r * r * r * c)
    dgn = jnp.sum(dh * (xv * r), axis=0, keepdims=True)
    return dx, dgn


def _accumulate(ref, val, first):
    @pl.when(first)
    def _():
        ref[...] = val

    @pl.when(jnp.logical_not(first))
    def _():
        ref[...] += val


def _mesh_pos():
    return lax.axis_index("x"), lax.axis_index("y"), lax.axis_index("c")


def _dev_index(d):
    return 4 * d[0] + 2 * d[1] + d[2]


class _Comm:
    def __init__(self, inputs, out_shape, scratch, phases):
        self.inputs, self.out_shape, self.scratch, self.phases = inputs, out_shape, scratch, phases

    def specs(self):
        any_spec = pl.BlockSpec(memory_space=pl.ANY)
        return [any_spec] * len(self.inputs), [any_spec] * len(self.out_shape)


def _run_alone(comm, name):
    n_in, n_out = len(comm.inputs), len(comm.out_shape)

    def body(*refs):
        for phase in comm.phases:
            phase(refs[:n_in], refs[n_in:n_in + n_out], refs[n_in + n_out:])

    in_specs, out_specs = comm.specs()
    return pl.pallas_call(body, name=name, out_shape=comm.out_shape, in_specs=in_specs, out_specs=out_specs,
                          scratch_shapes=comm.scratch)(*comm.inputs)


def _run_hosted(comm, at, step, ins, outs, scr):
    for phase, when in zip(comm.phases, at):
        @pl.when(step == when)
        def _(phase=phase):
            phase(ins, outs, scr)


def _split(refs, *counts):
    parts, o = [], 0
    for n in counts:
        parts.append(refs[o:o + n])
        o += n
    return parts + [refs[o:]]


def _gather_plan(packed, rows_list):
    n_w = len(rows_list)
    offs = [int(o) for o in np.cumsum([0] + list(rows_list[:-1]))]
    d = packed.shape[1]

    def tools(ins, outs, scr):
        p_ref = ins[0]
        send_sems, recv_sems, local_sem = scr
        x, y, c = _mesh_pos()
        me, sibling = (x, y, c), (x, y, 1 - c)
        chips = [(1 - x, y), (x, 1 - y), (1 - x, 1 - y)]

        def rows(w, dev):
            start = pl.multiple_of(_dev_index(dev) * rows_list[w], 16)
            return outs[w].at[pl.ds(start, rows_list[w]), :]

        def mine(w):
            return p_ref.at[pl.ds(offs[w], rows_list[w]), :]

        def copy(k, w, block, to, own):
            return pltpu.make_async_remote_copy(
                src_ref=mine(w) if own else rows(w, block), dst_ref=rows(w, block),
                send_sem=send_sems.at[k], recv_sem=recv_sems.at[k], device_id=to, device_id_type=MESH_T)

        def all_blocks(k):
            return pltpu.make_async_remote_copy(
                src_ref=p_ref, dst_ref=p_ref, send_sem=send_sems.at[k], recv_sem=recv_sems.at[k],
                device_id=me, device_id_type=MESH_T)

        return p_ref, local_sem, me, sibling, chips, c, rows, mine, copy, all_blocks

    def start(ins, outs, scr):
        _, local_sem, me, sibling, chips, c, rows, mine, copy, _ = tools(ins, outs, scr)
        for w in range(n_w):
            pltpu.make_async_copy(mine(w), rows(w, me), local_sem).start()
        for w in range(n_w):
            copy(0, w, me, sibling, True).start()
        for j, chip in enumerate(chips):
            for w in range(n_w):
                copy(1 + j, w, me, (*chip, c), True).start()

    def relay(ins, outs, scr):
        _, _, _, sibling, chips, c, _, _, copy, all_blocks = tools(ins, outs, scr)
        for j, chip in enumerate(chips):
            all_blocks(1 + j).wait_recv()
            for w in range(n_w):
                copy(4 + j, w, (*chip, c), sibling, False).start()

    def finish(ins, outs, scr):
        p_ref, local_sem, _, _, _, _, _, _, _, all_blocks = tools(ins, outs, scr)
        all_blocks(0).wait_recv()
        for j in range(3):
            all_blocks(4 + j).wait_recv()
        for k in range(7):
            all_blocks(k).wait_send()
        pltpu.make_async_copy(p_ref, p_ref, local_sem).wait()

    return _Comm(
        [packed], [jax.ShapeDtypeStruct((N_DEV * r, d), packed.dtype) for r in rows_list],
        [pltpu.SemaphoreType.DMA((7,)), pltpu.SemaphoreType.DMA((7,)), pltpu.SemaphoreType.DMA], [start, relay, finish])


def _exchange_plan(groups):
    flat = [a for g in groups for a in g]
    n_g = len(groups)
    sizes = [len(g) for g in groups]
    rows = [g[0].shape[0] // N_DEV for g in groups]
    first = [int(o) for o in np.cumsum([0] + sizes[:-1])]

    def start(srcs, lands, scr):
        send_sems, recv_sems, local_sems = scr
        x, y, c = _mesh_pos()
        me = (x, y, c)
        me_idx = _dev_index(me)

        def block(g, i, dev):
            start_row = pl.multiple_of(_dev_index(dev) * rows[g], 8)
            return srcs[first[g] + i].at[pl.ds(start_row, rows[g]), :]

        def slot(g, i):
            return lands[g].at[me_idx, pl.ds(i * rows[g], rows[g]), :]

        for g in range(n_g):
            for i in range(sizes[g]):
                pltpu.make_async_copy(block(g, i, me), slot(g, i), local_sems.at[g]).start()
        flips = [f for f in itertools.product((0, 1), repeat=3) if any(f)]
        for k, (fx, fy, fc) in enumerate(flips):
            peer = (1 - x if fx else x, 1 - y if fy else y, 1 - c if fc else c)
            for g in range(n_g):
                for i in range(sizes[g]):
                    pltpu.make_async_remote_copy(
                        src_ref=block(g, i, peer), dst_ref=slot(g, i), send_sem=send_sems.at[g, k],
                        recv_sem=recv_sems.at[g, k], device_id=peer, device_id_type=MESH_T).start()

    def finish(srcs, lands, scr):
        send_sems, recv_sems, local_sems = scr
        me = _mesh_pos()
        for k in range(7):
            for g in range(n_g):
                pltpu.make_async_remote_copy(
                    src_ref=lands[g].at[0], dst_ref=lands[g].at[0], send_sem=send_sems.at[g, k],
                    recv_sem=recv_sems.at[g, k], device_id=me, device_id_type=MESH_T).wait()
        for g in range(n_g):
            pltpu.make_async_copy(lands[g].at[0], lands[g].at[0], local_sems.at[g]).wait()

    return _Comm(
        flat, [jax.ShapeDtypeStruct((N_DEV, sizes[g] * rows[g], groups[g][0].shape[1]), groups[g][0].dtype) for g in range(n_g)],
        [pltpu.SemaphoreType.DMA((n_g, 7)), pltpu.SemaphoreType.DMA((n_g, 7)), pltpu.SemaphoreType.DMA((n_g,))],
        [start, finish])


def _comm_parts(comm):
    if comm is None:
        return 0, 0, [], [], [], [], []
    in_specs, out_specs = comm.specs()
    return len(comm.inputs), len(comm.out_shape), in_specs, out_specs, comm.out_shape, comm.scratch, comm.inputs


def _ffn_fwd(x, gn, wg_t, wu_t, wd, name, tm, tf, comm=None, comm_at=None, loss_head=None):
    t, d = x.shape
    f_all = wg_t.shape[0]
    n_f = f_all // tf
    n_ci, n_co, c_in_specs, c_out_specs, c_shapes, c_scratch, c_inputs = _comm_parts(comm)
    n_head, n_y = (2, 4) if loss_head else (0, 1)

    def body(*refs):
        ((x_ref, gn_ref, wg_ref, wu_ref, wd_ref), head_in, c_in, y_outs, (h_ref, gp_ref, up_ref, hid_ref), c_out,
         (h_s, hid_s), c_scr) = _split(refs, 5, n_head, n_ci, n_y, 4, n_co, 2)
        f = pl.program_id(1)
        if comm:
            _run_hosted(comm, comm_at, pl.program_id(0) * n_f + f, c_in, c_out, c_scr)

        @pl.when(f == 0)
        def _():
            xv = x_ref[...]
            h = ((xv * _rstd(xv)) * gn_ref[...]).astype(BF16)
            h_s[...] = h
            h_ref[...] = h

        h = h_s[...]
        cols = pl.ds(pl.multiple_of(f * tf, tf), tf)
        g = _dot_nt(h, wg_ref[cols, :])
        u = _dot_nt(h, wu_ref[cols, :])
        sg = jax.nn.sigmoid(g)
        silu = g * sg
        gp_ref[...] = (u * (sg * (1.0 + g * (1.0 - sg)))).astype(BF16)
        up_ref[...] = silu.astype(BF16)
        hid = (silu * u).astype(BF16)
        hid_ref[...] = hid
        for f0 in range(n_f):
            @pl.when(f == f0)
            def _(f0=f0):
                hid_s[:, f0 * tf:(f0 + 1) * tf] = hid

        @pl.when(f == n_f - 1)
        def _():
            y = x_ref[...] + FFN_RES_WEIGHT * _dot_nn(hid_s[...], wd_ref[...])
            if not loss_head:
                y_outs[0][...] = y
            else:
                (gf_ref, tg_ref), (dy_ref, dgf_ref, sq_ref, half_ref) = head_in, y_outs
                gfv, r, first = gf_ref[...], _rstd(y), pl.program_id(0) == 0
                diff = (y * r) * gfv - tg_ref[...]
                dy, dgf = _norm_bwd(diff * (1.0 / d), y, r, gfv)
                dy_ref[...] = dy
                half_ref[...] = (FFN_RES_WEIGHT * dy).astype(BF16)
                _accumulate(dgf_ref, dgf, first)
                _accumulate(sq_ref, jnp.sum(diff * diff, axis=0, keepdims=True), first)

    tok = pl.BlockSpec((tm, d), lambda i, f: (i, 0))
    row = pl.BlockSpec((1, d), lambda i, f: (0, 0))
    whole = pl.BlockSpec((f_all, d), lambda i, f: (0, 0), pipeline_mode=pl.Buffered(1))
    act = pl.BlockSpec((tm, tf), lambda i, f: (i, f))
    act_shape = jax.ShapeDtypeStruct((t, f_all), BF16)
    tok_f32, tok_bf16, row_f32 = (jax.ShapeDtypeStruct((t, d), F32), jax.ShapeDtypeStruct((t, d), BF16),
                                  jax.ShapeDtypeStruct((1, d), F32))
    y_shapes, y_specs = ([tok_f32, row_f32, row_f32, tok_bf16], [tok, row, row, tok]) if loss_head else ([tok_f32], [tok])
    res = pl.pallas_call(
        body, name=name, grid=(t // tm, n_f),
        out_shape=y_shapes + [tok_bf16, act_shape, act_shape, act_shape] + c_shapes,
        in_specs=[tok, row, whole, whole, whole] + ([row, tok] if loss_head else []) + c_in_specs,
        out_specs=y_specs + [tok, act, act, act] + c_out_specs,
        scratch_shapes=[pltpu.VMEM((tm, d), BF16), pltpu.VMEM((tm, f_all), BF16)] + c_scratch,
        compiler_params=_cp(("arbitrary", "arbitrary")),
    )(x, gn, wg_t, wu_t, wd, *(loss_head or ()), *c_inputs)
    return res[:n_y + 4], res[n_y + 4:]


def _ffn_bwd_hidden(dout, gp, up, hid, wd, name, tt, tf, comm=None, comm_at=None):
    t, d = dout.shape
    f_all = wd.shape[0]
    n_t = t // tt
    n_ci, n_co, c_in_specs, c_out_specs, c_shapes, c_scratch, c_inputs = _comm_parts(comm)

    def body(*refs):
        (dout_ref, gp_ref, up_ref, hid_ref, wd_ref), c_in, (dg_ref, du_ref, gwd_ref), c_out, (acc,), c_scr = _split(
            refs, 5, n_ci, 3, n_co, 1)
        s = pl.program_id(1)
        if comm:
            _run_hosted(comm, comm_at, pl.program_id(0) * n_t + s, c_in, c_out, c_scr)
        doutv = dout_ref[...]
        dhid = _dot_nt(doutv, wd_ref[...])
        dg_ref[...] = (dhid * gp_ref[...].astype(F32)).astype(BF16)
        du_ref[...] = (dhid * up_ref[...].astype(F32)).astype(BF16)
        _accumulate(acc, _dot_tn(hid_ref[...], doutv), s == 0)

        @pl.when(s == n_t - 1)
        def _():
            gwd_ref[...] = acc[...].astype(BF16)

    tok = pl.BlockSpec((tt, d), lambda f, s: (s, 0))
    wblk = pl.BlockSpec((tf, d), lambda f, s: (f, 0))
    act = pl.BlockSpec((tt, tf), lambda f, s: (s, f))
    act_shape = jax.ShapeDtypeStruct((t, f_all), BF16)
    res = pl.pallas_call(
        body, name=name, grid=(f_all // tf, n_t),
        out_shape=[act_shape, act_shape, jax.ShapeDtypeStruct((f_all, d), BF16)] + c_shapes,
        in_specs=[tok, act, act, act, wblk] + c_in_specs, out_specs=[act, act, wblk] + c_out_specs,
        scratch_shapes=[pltpu.VMEM((tf, d), F32)] + c_scratch,
        compiler_params=_cp(("arbitrary", "arbitrary")),
    )(dout, gp, up, hid, wd, *c_inputs)
    return res[:3], res[3:]


def _ffn_bwd_input(dy, x, gn, dg, du, wg_t, wu_t, name, tm, comm=None, comm_at=None):
    t, d = x.shape
    f_all = wg_t.shape[0]
    n_ci, n_co, c_in_specs, c_out_specs, c_shapes, c_scratch, c_inputs = _comm_parts(comm)

    def body(*refs):
        (dy_ref, x_ref, gn_ref, dg_ref, du_ref, wg_ref, wu_ref), c_in, (dx_ref, dgn_ref), c_out, c_scr = _split(
            refs, 7, n_ci, 2, n_co)
        i = pl.program_id(0)
        if comm:
            _run_hosted(comm, comm_at, i, c_in, c_out, c_scr)
        dh = _dot_nn(dg_ref[...], wg_ref[...]) + _dot_nn(du_ref[...], wu_ref[...])
        xv = x_ref[...]
        dxn, dgn = _norm_bwd(dh, xv, _rstd(xv), gn_ref[...])
        dx_ref[...] = dy_ref[...] + dxn
        _accumulate(dgn_ref, dgn, i == 0)

    tok = pl.BlockSpec((tm, d), lambda i: (i, 0))
    row = pl.BlockSpec((1, d), lambda i: (0, 0))
    act = pl.BlockSpec((tm, f_all), lambda i: (i, 0))
    whole = pl.BlockSpec((f_all, d), lambda i: (0, 0))
    res = pl.pallas_call(
        body, name=name, grid=(t // tm,),
        out_shape=[jax.ShapeDtypeStruct((t, d), F32), jax.ShapeDtypeStruct((1, d), F32)] + c_shapes,
        in_specs=[tok, tok, row, act, act, whole, whole] + c_in_specs,
        out_specs=[tok, row] + c_out_specs,
        scratch_shapes=c_scratch,
        compiler_params=_cp(("arbitrary",)),
    )(dy, x, gn, dg, du, wg_t, wu_t, *c_inputs)
    return res[:2], res[2:]


def _token_products(lhs_list, rhs, name, tt, tf, comm=None, comm_at=None):
    n_l = len(lhs_list)
    t, f_all = lhs_list[0].shape
    d = rhs.shape[1]
    n_t = t // tt
    n_ci, n_co, c_in_specs, c_out_specs, c_shapes, c_scratch, c_inputs = _comm_parts(comm)

    def body(*refs):
        lhs_refs, (rhs_ref,), c_in, out_refs, c_out, accs, c_scr = _split(refs, n_l, 1, n_ci, n_l, n_co, n_l)
        s = pl.program_id(1)
        if comm:
            _run_hosted(comm, comm_at, pl.program_id(0) * n_t + s, c_in, c_out, c_scr)
        rv = rhs_ref[...]
        for l_ref, acc in zip(lhs_refs, accs):
            _accumulate(acc, _dot_tn(l_ref[...], rv), s == 0)

        @pl.when(s == n_t - 1)
        def _():
            for o_ref, acc in zip(out_refs, accs):
                o_ref[...] = acc[...].astype(BF16)

    act = pl.BlockSpec((tt, tf), lambda f, s: (s, f))
    tok = pl.BlockSpec((tt, d), lambda f, s: (s, 0))
    wblk = pl.BlockSpec((tf, d), lambda f, s: (f, 0))
    res = pl.pallas_call(
        body, name=name, grid=(f_all // tf, n_t),
        out_shape=[jax.ShapeDtypeStruct((f_all, d), BF16)] * n_l + c_shapes,
        in_specs=[act] * n_l + [tok] + c_in_specs, out_specs=[wblk] * n_l + c_out_specs,
        scratch_shapes=[pltpu.VMEM((tf, d), F32)] * n_l + c_scratch,
        compiler_params=_cp(("arbitrary", "arbitrary")),
    )(*lhs_list, rhs, *c_inputs)
    return res[:n_l], res[n_l:]


def _swap_halves(t):
    w = t.shape[-1]
    lane = lax.broadcasted_iota(jnp.int32, (1, w), 1)
    return jnp.where((lane % HEAD_DIM) < HEAD_DIM // 2, pltpu.roll(t, w - HEAD_DIM // 2, 1), pltpu.roll(t, HEAD_DIM // 2, 1))


def _rope(t, cos, sin_signed):
    reps = t.shape[-1] // LANES
    return t * jnp.tile(cos, (1, reps)) + _swap_halves(t) * jnp.tile(sin_signed, (1, reps))


def _rope_bwd(dt, cos, sin_signed):
    reps = dt.shape[-1] // LANES
    return dt * jnp.tile(cos, (1, reps)) + _swap_halves(dt * jnp.tile(sin_signed, (1, reps)))


DILATIONS = tuple(dil for _, dil in B_PATTERNS if dil > 1)


def _seg_shape(t, dil, w, dtype):
    return jax.ShapeDtypeStruct((dil, t // dil, w), dtype)


def _seg_spec(tm, dil, w):
    return pl.BlockSpec((dil, tm // dil, w), lambda i: (0, i, 0))


def _tile_scratch(tm, w):
    return [pltpu.VMEM((tm, LANES), F32)] * (w // LANES)


def _put_tile(tile, val):
    for c, ref in enumerate(tile):
        ref[...] = val[:, c * LANES:(c + 1) * LANES]


def _get_tile(tile):
    return jnp.concatenate([ref[...] for ref in tile], axis=1)


def _scatter_to_segments(tile, seg_refs):
    for seg_ref, dil in zip(seg_refs, DILATIONS):
        rows = tile[0].shape[0] // dil
        for r in range(dil):
            for c, ref in enumerate(tile):
                seg_ref[r, :, c * LANES:(c + 1) * LANES] = ref[pl.ds(r, rows, stride=dil), :].astype(seg_ref.dtype)


def _gather_from_segments(seg_ref, dil, tile, add=False):
    rows = tile[0].shape[0] // dil
    for r in range(dil):
        for c, ref in enumerate(tile):
            idx = (pl.ds(r, rows, stride=dil), slice(None))
            v = seg_ref[r, :, c * LANES:(c + 1) * LANES].astype(F32)
            ref[idx] = ref[idx] + v if add else v


def _in_proj_fwd(x, gn, win_t, cos, sin_signed, name, tm):
    t, d = x.shape
    in_w = win_t.shape[0]
    n_dil = len(DILATIONS)

    def body(x_ref, gn_ref, w_ref, cos_ref, sin_ref, h_ref, aq_ref, akx_ref, avx_ref, bq_ref, bk_ref, bv_ref, *rest):
        seg_refs, tile = rest[:3 * n_dil], rest[3 * n_dil:]
        xv = x_ref[...]
        h = ((xv * _rstd(xv)) * gn_ref[...]).astype(BF16)
        h_ref[...] = h
        p = _dot_nt(h, w_ref[...])
        cs, sn = cos_ref[...], sin_ref[...]
        o = 0
        aq_ref[...] = (_rope(p[:, o:o + A_Q_W], cs, sn) * QK_SCALE).astype(BF16)
        o += A_Q_W
        ak = _rope(p[:, o:o + A_KV_W], cs, sn)
        o += A_KV_W
        av = p[:, o:o + A_KV_W]
        o += A_KV_W
        low = lax.broadcasted_iota(jnp.int32, (1, LANES), 1) < HEAD_DIM
        for src, dst in ((ak, akx_ref), (av, avx_ref)):
            other = pltpu.roll(src, HEAD_DIM, 1)
            dst[0] = jnp.where(low, src, other).astype(BF16)
            dst[1] = jnp.where(low, other, src).astype(BF16)
        for k, nat_ref in enumerate((bq_ref, bk_ref, bv_ref)):
            val = p[:, o:o + B_W]
            o += B_W
            if k < 2:
                val = _rope(val, cs, sn)
            if k == 0:
                val = val * QK_SCALE
            nat_ref[...] = val.astype(BF16)
            _put_tile(tile, val)
            _scatter_to_segments(tile, seg_refs[k * n_dil:(k + 1) * n_dil])

    tok = lambda w: pl.BlockSpec((tm, w), lambda i: (i, 0))
    kvx = pl.BlockSpec((2, tm, LANES), lambda i: (0, i, 0))
    sd = lambda *s: jax.ShapeDtypeStruct(s, BF16)
    res = pl.pallas_call(
        body, name=name, grid=(t // tm,),
        out_shape=[sd(t, d), sd(t, A_Q_W), sd(2, t, LANES), sd(2, t, LANES), sd(t, B_W), sd(t, B_W), sd(t, B_W)]
        + [_seg_shape(t, dil, B_W, BF16) for _ in range(3) for dil in DILATIONS],
        in_specs=[tok(d), pl.BlockSpec((1, d), lambda i: (0, 0)), pl.BlockSpec((in_w, d), lambda i: (0, 0)),
                  tok(LANES), tok(LANES)],
        out_specs=[tok(d), tok(A_Q_W), kvx, kvx, tok(B_W), tok(B_W), tok(B_W)]
        + [_seg_spec(tm, dil, B_W) for _ in range(3) for dil in DILATIONS],
        scratch_shapes=_tile_scratch(tm, B_W),
        compiler_params=_cp(("arbitrary",)),
    )(x, gn, win_t, cos, sin_signed)
    return res[:7], [res[7 + k * n_dil:7 + (k + 1) * n_dil] for k in range(3)]


def _in_proj_bwd(dres, x, gn, win_t, h, cos, sin_signed, daq, dakx, davx, per_pattern, name, tm):
    t, d = x.shape
    in_w = win_t.shape[0]
    n_t = t // tm
    chunk = in_w // 3
    n_pat = 3 * (1 + len(DILATIONS))

    def body(*refs):
        ((dres_ref, x_ref, gn_ref, w_ref, h_ref, cos_ref, sin_ref, daq_ref, dakx_ref, davx_ref), pat,
         (dx_ref, dgn_ref, gw_ref, half_ref), (dp_s, gw_s), tile) = _split(refs, 10, n_pat, 4, 2)
        i = pl.program_id(0)
        cs, sn = cos_ref[...], sin_ref[...]
        low = lax.broadcasted_iota(jnp.int32, (1, LANES), 1) < HEAD_DIM

        def fold(ref):
            a, b = ref[0].astype(F32), ref[1].astype(F32)
            return jnp.where(low, a + pltpu.roll(a, HEAD_DIM, 1), b + pltpu.roll(b, HEAD_DIM, 1))

        def summed(j):
            _put_tile(tile, pat[j][...].astype(F32))
            for k, dil in enumerate(DILATIONS):
                _gather_from_segments(pat[3 * (1 + k) + j], dil, tile, add=True)
            return _get_tile(tile)

        o = 0
        dp_s[:, o:o + A_Q_W] = _rope_bwd(daq_ref[...].astype(F32) * QK_SCALE, cs, sn).astype(BF16)
        o += A_Q_W
        dp_s[:, o:o + A_KV_W] = _rope_bwd(fold(dakx_ref), cs, sn).astype(BF16)
        o += A_KV_W
        dp_s[:, o:o + A_KV_W] = fold(davx_ref).astype(BF16)
        o += A_KV_W
        dp_s[:, o:o + B_W] = _rope_bwd(summed(0) * QK_SCALE, cs, sn).astype(BF16)
        o += B_W
        dp_s[:, o:o + B_W] = _rope_bwd(summed(1), cs, sn).astype(BF16)
        o += B_W
        dp_s[:, o:o + B_W] = summed(2).astype(BF16)
        dh = _dot_nn(dp_s[...], w_ref[...])
        hv = h_ref[...]
        for c0 in range(0, in_w, chunk):
            _accumulate(gw_s.at[pl.ds(c0, chunk), :], _dot_tn(dp_s[:, c0:c0 + chunk], hv), i == 0)
        xv = x_ref[...]
        dxn, dgn = _norm_bwd(dh, xv, _rstd(xv), gn_ref[...])
        dx = dres_ref[...] + dxn
        dx_ref[...] = dx
        half_ref[...] = (FFN_RES_WEIGHT * dx).astype(BF16)
        _accumulate(dgn_ref, dgn, i == 0)

        @pl.when(i == n_t - 1)
        def _():
            gw_ref[...] = gw_s[...].astype(BF16)

    tok = lambda w: pl.BlockSpec((tm, w), lambda i: (i, 0))
    row = pl.BlockSpec((1, d), lambda i: (0, 0))
    whole = pl.BlockSpec((in_w, d), lambda i: (0, 0))
    weight = pl.BlockSpec((in_w, d), lambda i: (0, 0), pipeline_mode=pl.Buffered(1))
    kvx = pl.BlockSpec((2, tm, LANES), lambda i: (0, i, 0))
    flat = [a if k == 0 else a.reshape(DILATIONS[k - 1], -1, B_W) for k, grads in enumerate(per_pattern) for a in grads]
    return pl.pallas_call(
        body, name=name, grid=(n_t,),
        out_shape=[jax.ShapeDtypeStruct((t, d), F32), jax.ShapeDtypeStruct((1, d), F32),
                   jax.ShapeDtypeStruct((in_w, d), BF16), jax.ShapeDtypeStruct((t, d), BF16)],
        in_specs=[tok(d), tok(d), row, weight, tok(d), tok(LANES), tok(LANES), tok(A_Q_W), kvx, kvx]
        + [tok(B_W)] * 3 + [_seg_spec(tm, dil, B_W) for dil in DILATIONS for _ in range(3)],
        out_specs=[tok(d), row, whole, tok(d)],
        scratch_shapes=[pltpu.VMEM((tm, in_w), BF16), pltpu.VMEM((in_w, d), F32)] + _tile_scratch(tm, B_W),
        compiler_params=_cp(("arbitrary",)),
    )(dres, x, gn, win_t, h, cos, sin_signed, daq, dakx, davx, *flat)


def _merge_out_proj_fwd(x, a_out, outs, lses, wout, name, tm):
    t, d = x.shape
    n_dil = len(DILATIONS)

    def body(x_ref, a_ref, *rest):
        o_refs, l_refs, (w_ref, y_ref, b_ref, lt_ref), b_segs, lt_segs, scratch = _split(rest, 1 + n_dil, 1 + n_dil, 4, n_dil, n_dil)
        n_c = B_W // LANES
        tiles = [scratch[j * n_c:(j + 1) * n_c] for j in range(2 * n_dil)]
        os_, ls = [o_refs[0][...].astype(F32)], [l_refs[0][...]]
        for k, dil in enumerate(DILATIONS):
            _gather_from_segments(o_refs[1 + k], dil, tiles[2 * k])
            _gather_from_segments(l_refs[1 + k], dil, tiles[2 * k + 1])
            os_.append(_get_tile(tiles[2 * k]))
            ls.append(_get_tile(tiles[2 * k + 1]))
        mx = functools.reduce(jnp.maximum, ls)
        es = [jnp.exp(l - mx) for l in ls]
        den = functools.reduce(jnp.add, es)
        b = functools.reduce(jnp.add, [e * o for e, o in zip(es, os_)]) / den
        lt = mx + jnp.log(den)
        bb = b.astype(BF16)
        b_ref[...] = bb
        lt_ref[...] = lt
        y_ref[...] = x_ref[...] + _dot_nn(a_ref[...], w_ref[0:A_Q_W, :]) + _dot_nn(bb, w_ref[A_Q_W:A_Q_W + B_W, :])
        _put_tile(tiles[0], b)
        _scatter_to_segments(tiles[0], b_segs)
        _put_tile(tiles[1], lt)
        _scatter_to_segments(tiles[1], lt_segs)

    tok = lambda w: pl.BlockSpec((tm, w), lambda i: (i, 0))
    segs = [_seg_spec(tm, dil, B_W) for dil in DILATIONS]
    res = pl.pallas_call(
        body, name=name, grid=(t // tm,),
        out_shape=[jax.ShapeDtypeStruct((t, d), F32), jax.ShapeDtypeStruct((t, B_W), BF16), jax.ShapeDtypeStruct((t, B_W), F32)]
        + [_seg_shape(t, dil, B_W, BF16) for dil in DILATIONS] + [_seg_shape(t, dil, B_W, F32) for dil in DILATIONS],
        in_specs=[tok(d), tok(A_Q_W)] + ([tok(B_W)] + segs) * 2 + [pl.BlockSpec(wout.shape, lambda i: (0, 0))],
        out_specs=[tok(d), tok(B_W), tok(B_W)] + segs * 2,
        scratch_shapes=_tile_scratch(tm, B_W) * (2 * n_dil),
        compiler_params=_cp(("arbitrary",)),
    )(x, a_out, *outs, *lses, wout)
    return res[0], [res[1]] + list(res[3:3 + n_dil]), [res[2]] + list(res[3 + n_dil:])


def _out_proj_bwd(dy, a_out, b_out, wout, name, tm):
    t, d = dy.shape
    n_t = t // tm
    n_dil = len(DILATIONS)

    def body(dy_ref, a_ref, b_ref, w_ref, da_ref, db_ref, gw_ref, *rest):
        db_segs, gw_s, tile = rest[:n_dil], rest[n_dil], rest[n_dil + 1:]
        i = pl.program_id(0)
        dyb = dy_ref[...].astype(BF16)
        da_ref[...] = _dot_nt(dyb, w_ref[0:A_Q_W, :]).astype(BF16)
        db = _dot_nt(dyb, w_ref[A_Q_W:A_Q_W + B_W, :])
        db_ref[...] = db.astype(BF16)
        _put_tile(tile, db)
        _scatter_to_segments(tile, db_segs)
        ga = _dot_tn(a_ref[...], dyb)
        gb = _dot_tn(b_ref[...], dyb)

        @pl.when(i == 0)
        def _():
            gw_s[0:A_Q_W, :] = ga
            gw_s[A_Q_W:A_Q_W + B_W, :] = gb

        @pl.when(i > 0)
        def _():
            gw_s[0:A_Q_W, :] += ga
            gw_s[A_Q_W:A_Q_W + B_W, :] += gb

        @pl.when(i == n_t - 1)
        def _():
            gw_ref[...] = gw_s[...].astype(BF16)

    tok = lambda w: pl.BlockSpec((tm, w), lambda i: (i, 0))
    whole = pl.BlockSpec(wout.shape, lambda i: (0, 0))
    res = pl.pallas_call(
        body, name=name, grid=(n_t,),
        out_shape=[jax.ShapeDtypeStruct((t, A_Q_W), BF16), jax.ShapeDtypeStruct((t, B_W), BF16),
                   jax.ShapeDtypeStruct(wout.shape, BF16)] + [_seg_shape(t, dil, B_W, BF16) for dil in DILATIONS],
        in_specs=[tok(d), tok(A_Q_W), tok(B_W), whole],
        out_specs=[tok(A_Q_W), tok(B_W), whole] + [_seg_spec(tm, dil, B_W) for dil in DILATIONS],
        scratch_shapes=[pltpu.VMEM(wout.shape, F32)] + _tile_scratch(tm, B_W),
        compiler_params=_cp(("arbitrary",)),
    )(dy, a_out, b_out, wout)
    return res[0], [res[1]] + list(res[3:]), res[2]


SUB_ROWS = 64
WINDOW_ALIGN = 64


def _sub_band(r0, hw, win, seg_len, t, rel, col):
    ks = pl.multiple_of(jnp.clip(r0 - hw, 0, t - win), WINDOW_ALIGN)
    kpos = col + ks
    seg_lo = (r0 // seg_len) * seg_len
    valid = (jnp.abs(rel + (ks - r0)) <= hw) & (kpos >= seg_lo) & (kpos < seg_lo + seg_len)
    return ks, valid


def _split_heads(v, low):
    zero = jnp.zeros_like(v)
    return jnp.concatenate([jnp.where(low, v, zero), jnp.where(low, zero, v)], axis=0)


def _kv_spec(kv, t):
    if kv.ndim == 3:
        return pl.BlockSpec((None, t, LANES), lambda p, i: (p // 2, 0, 0))
    return pl.BlockSpec((t, LANES), lambda p, i: (0, p))


def _attn_fwd(q, k, v, sink, name, hw, seg_len, tq, has_sink):
    t, width = q.shape
    sb = min(SUB_ROWS, tq)
    win = 2 * hw + LANES

    def body(sink_ref, q_ref, k_ref, v_ref, o_ref, lse_ref):
        p, i = pl.program_id(0), pl.program_id(1)
        q0 = i * tq
        low = lax.broadcasted_iota(jnp.int32, (1, LANES), 1) < HEAD_DIM
        rel = lax.broadcasted_iota(jnp.int32, (sb, win), 1) - lax.broadcasted_iota(jnp.int32, (sb, win), 0)
        col = lax.broadcasted_iota(jnp.int32, (1, win), 1)
        subs = []
        for j in range(tq // sb):
            rows = pl.ds(j * sb, sb)
            ks, valid = _sub_band(q0 + j * sb, hw, win, seg_len, t, rel, col)
            subs.append((rows, ks, valid, _dot_nt(_split_heads(q_ref[rows, :], low), k_ref[pl.ds(ks, win), :])))
        for rows, ks, valid, s in subs:
            vw = v_ref[pl.ds(ks, win), :]
            es, inv, lses = [], [], []
            for a in range(2):
                sa = jnp.where(valid, s[a * sb:(a + 1) * sb], NEG)
                m = jnp.max(sa, axis=1, keepdims=True)
                if has_sink:
                    sk = sink_ref[2 * p + a]
                    m = jnp.maximum(m, sk)
                e = jnp.exp(sa - m)
                den = jnp.sum(e, axis=1, keepdims=True)
                if has_sink:
                    den = den + jnp.exp(sk - m)
                es.append(e.astype(BF16))
                inv.append(1.0 / den)
                lses.append(m + jnp.log(den))
            pv = _dot_nn(jnp.concatenate(es, axis=0), vw)
            o_ref[rows, :] = jnp.where(low, pv[0:sb] * inv[0], pv[sb:2 * sb] * inv[1]).astype(BF16)
            lse_ref[rows, :] = jnp.where(low, lses[0], lses[1])

    tile = pl.BlockSpec((tq, LANES), lambda p, i: (i, p))
    return pl.pallas_call(
        body, name=name, grid=(width // LANES, t // tq),
        out_shape=[jax.ShapeDtypeStruct((t, width), BF16), jax.ShapeDtypeStruct((t, width), F32)],
        in_specs=[pl.BlockSpec(memory_space=pltpu.SMEM), tile, _kv_spec(k, t), _kv_spec(v, t)],
        out_specs=[tile, tile],
        compiler_params=_cp(("arbitrary", "arbitrary")),
    )(sink, q, k, v)


def _attn_bwd(q, k, v, o, do, lse, sink, name, hw, seg_len, tq, has_sink, comm=None, comm_at=None):
    t, width = q.shape
    sb = min(SUB_ROWS, tq)
    win = 2 * hw + LANES
    n_q = t // tq
    shared_kv = k.ndim == 3
    n_ci, n_co, c_in_specs, c_out_specs, c_shapes, c_scratch, c_inputs = _comm_parts(comm)

    def body(*refs):
        ((sink_ref, q_ref, k_ref, v_ref, o_ref, do_ref, lse_ref), c_in, (dq_ref, dk_ref, dv_ref, ds_ref), c_out,
         (dk_s, dv_s), c_scr) = _split(refs, 7, n_ci, 4, n_co, 2)
        p, i = pl.program_id(0), pl.program_id(1)
        if comm:
            _run_hosted(comm, comm_at, p * n_q + i, c_in, c_out, c_scr)
        fresh = (i == 0) & (p % 2 == 0) if shared_kv else i == 0
        last = (i == n_q - 1) & (p % 2 == 1) if shared_kv else i == n_q - 1

        @pl.when(fresh)
        def _():
            dk_s[...] = jnp.zeros_like(dk_s)
            dv_s[...] = jnp.zeros_like(dv_s)

        q0 = i * tq
        low = lax.broadcasted_iota(jnp.int32, (1, LANES), 1) < HEAD_DIM
        rel = lax.broadcasted_iota(jnp.int32, (sb, win), 1) - lax.broadcasted_iota(jnp.int32, (sb, win), 0)
        col = lax.broadcasted_iota(jnp.int32, (1, win), 1)
        dsink = [jnp.zeros((1, 1), F32), jnp.zeros((1, 1), F32)]
        subs = []
        for j in range(tq // sb):
            rows = pl.ds(j * sb, sb)
            ks, valid = _sub_band(q0 + j * sb, hw, win, seg_len, t, rel, col)
            kw = k_ref[pl.ds(ks, win), :]
            dov = do_ref[rows, :]
            q2 = _split_heads(q_ref[rows, :], low)
            do2 = _split_heads(dov, low)
            subs.append((rows, ks, valid, kw, dov, q2, do2, _dot_nt(q2, kw), _dot_nt(do2, v_ref[pl.ds(ks, win), :])))
        probs = []
        for rows, ks, valid, kw, dov, q2, do2, s, dpr in subs:
            prod = dov.astype(F32) * o_ref[rows, :].astype(F32)
            lse_t = lse_ref[rows, :]
            prs, dss = [], []
            for a in range(2):
                mine = low if a == 0 else jnp.logical_not(low)
                lse_a = jnp.max(jnp.where(mine, lse_t, -jnp.inf), axis=1, keepdims=True)
                delta = jnp.sum(jnp.where(mine, prod, 0.0), axis=1, keepdims=True)
                pr = jnp.exp(jnp.where(valid, s[a * sb:(a + 1) * sb], NEG) - lse_a)
                prs.append(pr.astype(BF16))
                dss.append((pr * (dpr[a * sb:(a + 1) * sb] - delta)).astype(BF16))
                if has_sink:
                    dsink[a] = dsink[a] - jnp.sum(jnp.exp(sink_ref[2 * p + a] - lse_a) * delta, axis=0, keepdims=True)
            probs.append((jnp.concatenate(prs, axis=0), jnp.concatenate(dss, axis=0)))
        for (rows, ks, valid, kw, dov, q2, do2, s, dpr), (pr2, ds2) in zip(subs, probs):
            dv_s[pl.ds(ks, win), :] += _dot_tn(pr2, do2)
            dk_s[pl.ds(ks, win), :] += _dot_tn(ds2, q2)
            dq2 = _dot_nn(ds2, kw)
            dq_ref[rows, :] = jnp.where(low, dq2[0:sb], dq2[sb:2 * sb]).astype(BF16)
        ds_ref[...] = jnp.broadcast_to(jnp.where(low, dsink[0], dsink[1]), ds_ref.shape)

        @pl.when(last)
        def _():
            dk_ref[...] = dk_s[...].astype(BF16)
            dv_ref[...] = dv_s[...].astype(BF16)

    tile = pl.BlockSpec((tq, LANES), lambda p, i: (i, p))
    kv_shape = jax.ShapeDtypeStruct(k.shape, BF16)
    res = pl.pallas_call(
        body, name=name, grid=(width // LANES, n_q),
        out_shape=[jax.ShapeDtypeStruct((t, width), BF16), kv_shape, kv_shape,
                   jax.ShapeDtypeStruct((width // LANES, n_q, 8, LANES), F32)] + c_shapes,
        in_specs=[pl.BlockSpec(memory_space=pltpu.SMEM), tile, _kv_spec(k, t), _kv_spec(v, t), tile, tile, tile] + c_in_specs,
        out_specs=[tile, _kv_spec(k, t), _kv_spec(v, t), pl.BlockSpec((None, None, 8, LANES), lambda p, i: (p, i, 0, 0))] + c_out_specs,
        scratch_shapes=[pltpu.VMEM((t, LANES), F32)] * 2 + c_scratch,
        compiler_params=_cp(("arbitrary", "arbitrary")),
    )(sink, q, k, v, o, do, lse, *c_inputs)
    return res[:4], res[4:]


def _adamw(w, g, m, v, name):
    def body(w_ref, g_ref, m_ref, v_ref, d_ref, nm_ref, nv_ref):
        gv = g_ref[...]
        nm = ADAM_B1 * m_ref[...] + (1.0 - ADAM_B1) * gv
        nv = ADAM_B2 * v_ref[...] + (1.0 - ADAM_B2) * (gv * gv)
        m_hat = nm / (1.0 - ADAM_B1 ** ADAM_STEP)
        v_hat = nv / (1.0 - ADAM_B2 ** ADAM_STEP)
        d_ref[...] = -ADAM_LR * (m_hat / (jnp.sqrt(v_hat) + ADAM_EPS) + ADAM_WD * w_ref[...])
        nm_ref[...] = nm
        nv_ref[...] = nv

    shape = jax.ShapeDtypeStruct(w.shape, F32)
    return pl.pallas_call(body, name=name, out_shape=[shape, shape, shape], compiler_params=_cp())(w, g, m, v)


def _adamw_rows(land, j, rows, w, m, v, name):
    d = w.shape[1]

    def body(l_ref, w_ref, m_ref, v_ref, g_ref, d_ref, nm_ref, nv_ref):
        gv = l_ref[0].astype(F32)
        for s in range(1, N_DEV):
            gv = gv + l_ref[s].astype(F32)
        g_ref[...] = gv
        nm = ADAM_B1 * m_ref[...] + (1.0 - ADAM_B1) * gv
        nv = ADAM_B2 * v_ref[...] + (1.0 - ADAM_B2) * (gv * gv)
        m_hat = nm / (1.0 - ADAM_B1 ** ADAM_STEP)
        v_hat = nv / (1.0 - ADAM_B2 ** ADAM_STEP)
        d_ref[...] = -ADAM_LR * (m_hat / (jnp.sqrt(v_hat) + ADAM_EPS) + ADAM_WD * w_ref[...])
        nm_ref[...] = nm
        nv_ref[...] = nv

    halves = 2
    half = pl.BlockSpec((rows // halves, d), lambda i: (i, 0))
    shape = jax.ShapeDtypeStruct((rows, d), F32)
    return pl.pallas_call(
        body, name=name, grid=(halves,), out_shape=[shape] * 4,
        in_specs=[pl.BlockSpec((N_DEV, rows // halves, d), lambda i: (0, halves * j + i, 0)), half, half, half],
        out_specs=[half] * 4,
        compiler_params=_cp(("arbitrary",)),
    )(land, w, m, v)


def _sum_small(land, name):
    def body(l_ref, o_ref):
        acc = l_ref[0]
        for s in range(1, N_DEV):
            acc = acc + l_ref[s]
        o_ref[...] = acc

    return pl.pallas_call(body, name=name, out_shape=jax.ShapeDtypeStruct(land.shape[1:], F32), compiler_params=_cp())(land)


def _rope_lanes(positions):
    inv_freq = 1.0 / (ROPE_THETA ** (jnp.arange(0, HEAD_DIM, 2, dtype=F32) / HEAD_DIM))
    ang = positions.astype(F32)[:, None] * inv_freq
    cos, sin = jnp.cos(ang), jnp.sin(ang)
    return jnp.concatenate([cos, cos, cos, cos], axis=1), jnp.concatenate([-sin, sin, -sin, sin], axis=1)


def kernel(x, positions, norm_ffn1, w_gate1, w_up1, w_down1, norm_mix, w_in, a_sink, w_out, norm_ffn2, w_gate2, w_up2, w_down2, norm_final, loss_target, m_norm_ffn1, m_w_gate1, m_w_up1, m_w_down1, m_norm_mix, m_w_in, m_a_sink, m_w_out, m_norm_ffn2, m_w_gate2, m_w_up2, m_w_down2, m_norm_final, v_norm_ffn1, v_w_gate1, v_w_up1, v_w_down1, v_norm_mix, v_w_in, v_a_sink, v_w_out, v_norm_ffn2, v_w_gate2, v_w_up2, v_w_down2, v_norm_final):
    x = x[0]
    target = loss_target[0]
    t, d = x.shape
    tm_mix = min(512, t)
    tq_of = lambda seg_len, most=2048: min(most, t, max(seg_len, 16 * SUB_ROWS))

    tt = min(1024, t)
    f_all = w_gate1.shape[2] * N_DEV
    tfp = f_all // 2
    tm_fwd, tf_fwd = tm_mix, tfp
    n_steps = (t // tm_fwd) * (f_all // tf_fwd)

    def stacked(shards):
        return jnp.concatenate([s.astype(BF16) for s in shards], axis=0), [s.shape[0] for s in shards]

    packed1, rows1 = stacked([w_gate1[0].T, w_up1[0].T, w_down1[0]])
    packed2, rows2 = stacked([w_in[0].T, w_out[0], w_gate2[0].T, w_up2[0].T, w_down2[0]])
    wg1, wu1, wd1 = _run_alone(_gather_plan(packed1, rows1), "gather_ffn1_weights")

    cos, sin_signed = _rope_lanes(positions[0])
    sink = a_sink[0]
    no_sink = jnp.zeros_like(sink)

    (x1, h1, gp1, up1, hid1), (win, wout, wg2, wu2, wd2) = _ffn_fwd(
        x, norm_ffn1, wg1, wu1, wd1, "ffn1_fwd", tm_fwd, tf_fwd, _gather_plan(packed2, rows2), [0, (3 * n_steps) // 4, n_steps - 1])
    (h_mix, aq, akx, avx, bq, bk, bv), seg_qkv = _in_proj_fwd(x1, norm_mix, win, cos, sin_signed, "in_proj_fwd", tm_mix)
    a_out, a_lse = _attn_fwd(aq, akx, avx, sink, "attn_a_fwd", A_HALF_WINDOW, t, tq_of(t), True)
    rows_of = lambda a: a.reshape(t, B_W)
    segments_of = lambda a, dil: a if dil == 1 else a.reshape(dil, t // dil, B_W)
    b_qkv, b_outs, b_lses = [], [], []
    for n, (window, dil) in enumerate(B_PATTERNS):
        qs, ks, vs = (bq, bk, bv) if dil == 1 else (rows_of(seg_qkv[k][n - 1]) for k in range(3))
        o_seg, lse_seg = _attn_fwd(qs, ks, vs, no_sink, f"attn_b{dil}_fwd", window // (2 * dil), t // dil,
                                  tq_of(t // dil, 16 * SUB_ROWS), False)
        b_qkv.append((qs, ks, vs))
        b_outs.append(segments_of(o_seg, dil))
        b_lses.append(segments_of(lse_seg, dil))
    x2, b_out, b_lse = _merge_out_proj_fwd(x1, a_out, b_outs, b_lses, wout, "out_proj_fwd", tm_mix)
    (dx3, dg_final, sq, dout2, h2, gp2, up2, hid2), _ = _ffn_fwd(
        x2, norm_ffn2, wg2, wu2, wd2, "ffn2_fwd_loss", tm_fwd, tf_fwd, loss_head=(norm_final.reshape(1, d), target))


    (dgt2, dut2, gwd2), _ = _ffn_bwd_hidden(dout2, gp2, up2, hid2, wd2, "ffn2_bwd_hidden", tm_mix, tfp)
    (dx2, dg_ffn2), _ = _ffn_bwd_input(dx3, x2, norm_ffn2, dgt2, dut2, wg2, wu2, "ffn2_bwd_input", tm_mix)
    (gwg2, gwu2), _ = _token_products([dgt2, dut2], h2, "ffn2_bwd_gate_up", tt, tfp)

    n_att = lambda width, tq: (width // LANES) * (t // tq)
    n_prod = (f_all // tfp) * (t // tt)
    da, db, gwout = _out_proj_bwd(dx2, a_out, b_out[0], wout, "out_proj_bwd", tm_mix)
    (daq, dakx, davx, dsink_parts), (land_wg2,) = _attn_bwd(
        aq, akx, avx, a_out, da, a_lse, sink, "attn_a_bwd", A_HALF_WINDOW, t, tq_of(t), True,
        _exchange_plan([[gwg2]]), [0, n_att(A_Q_W, tq_of(t)) - 1])
    pattern_grads, pattern_lands = [], []
    for n, ((window, dil), (qs, ks, vs)) in enumerate(zip(B_PATTERNS, b_qkv)):
        carried = [gwu2, gwd2][n:n + 1]
        grads, lands = _attn_bwd(qs, ks, vs, rows_of(b_out[n]), rows_of(db[n]), rows_of(b_lse[n]), no_sink, f"attn_b{dil}_bwd",
                                 window // (2 * dil), t // dil, tq_of(t // dil), False,
                                 _exchange_plan([carried]) if carried else None, [0, n_att(B_W, tq_of(t // dil)) - 1])
        pattern_grads.append(grads[:3])
        pattern_lands.extend(lands)
    land_wu2, land_wd2 = pattern_lands
    dx1, dg_mix, gwin, dout1 = _in_proj_bwd(dx2, x1, norm_mix, win, h_mix, cos, sin_signed, daq, dakx, davx, pattern_grads, "in_proj_bwd", tm_mix)

    (dgt1, dut1, gwd1), (land_in, land_out) = _ffn_bwd_hidden(
        dout1, gp1, up1, hid1, wd1, "ffn1_bwd_hidden", tm_mix, tfp, _exchange_plan([[gwin], [gwout]]),
        [0, (f_all // tfp) * (t // tm_mix) - 1])
    (gwg1,), (land_wd1,) = _token_products([dgt1], h1, "ffn1_bwd_gate", tt, tfp, _exchange_plan([[gwd1]]), [0, n_prod - 1])
    (gwu1,), (land_wg1,) = _token_products([dut1], h1, "ffn1_bwd_up", tt, tfp, _exchange_plan([[gwg1]]), [0, n_prod - 1])
    (grad_x, dg_ffn1), (land_wu1,) = _ffn_bwd_input(
        dx1, x, norm_ffn1, dgt1, dut1, wg1, wu1, "ffn1_bwd_input", tm_mix, _exchange_plan([[gwu1]]), [0, t // tm_mix - 1])

    dsink_pairs = jnp.sum(dsink_parts[:, :, 0, :], axis=1)
    dsink = jnp.stack([dsink_pairs[:, 0], dsink_pairs[:, HEAD_DIM]], axis=1).reshape(1, -1)
    small = jnp.concatenate([dg_ffn1, dg_mix, dg_ffn2, dg_final, jnp.pad(dsink, ((0, 0), (0, d - dsink.shape[1]))),
                             sq, jnp.zeros((2, d), F32)], axis=0)
    (land_small,) = _run_alone(_exchange_plan([[jnp.tile(small, (N_DEV, 1))]]), "gather_small_gradients")
    red_small = _sum_small(land_small, "sum_small_grads")
    loss = 0.5 * jnp.sum(red_small[5]) / d

    rf = rows1[0]
    sharded = {"w_gate1": (land_wg1, 0, rf, True), "w_up1": (land_wu1, 0, rf, True), "w_down1": (land_wd1, 0, rf, False),
               "w_in": (land_in, 0, rows2[0], True), "w_out": (land_out, 0, rows2[1], False),
               "w_gate2": (land_wg2, 0, rf, True), "w_up2": (land_wu2, 0, rf, True), "w_down2": (land_wd2, 0, rf, False)}
    n_sink = a_sink.shape[1]
    small_grads = {"norm_ffn1": red_small[0:1], "norm_mix": red_small[1:2], "norm_ffn2": red_small[2:3], "norm_final": red_small[3],
                   "a_sink": red_small[4:5, :n_sink]}
    params = {
        "norm_ffn1": (norm_ffn1, m_norm_ffn1, v_norm_ffn1), "w_gate1": (w_gate1, m_w_gate1, v_w_gate1),
        "w_up1": (w_up1, m_w_up1, v_w_up1), "w_down1": (w_down1, m_w_down1, v_w_down1),
        "norm_mix": (norm_mix, m_norm_mix, v_norm_mix), "w_in": (w_in, m_w_in, v_w_in),
        "a_sink": (a_sink, m_a_sink, v_a_sink), "w_out": (w_out, m_w_out, v_w_out),
        "norm_ffn2": (norm_ffn2, m_norm_ffn2, v_norm_ffn2), "w_gate2": (w_gate2, m_w_gate2, v_w_gate2),
        "w_up2": (w_up2, m_w_up2, v_w_up2), "w_down2": (w_down2, m_w_down2, v_w_down2),
        "norm_final": (norm_final, m_norm_final, v_norm_final),
    }
    grad_list, deltas, new_ms, new_vs = [], [], [], []
    for name, (w, m, v) in params.items():
        if name in sharded:
            land, j, rows, is_transposed = sharded[name]
            view = (lambda a: a[0].T) if is_transposed else (lambda a: a[0])
            back = (lambda a: a.T[None]) if is_transposed else (lambda a: a[None])
            outs = [back(o) for o in _adamw_rows(land, j, rows, view(w), view(m), view(v), f"adamw_{name}")]
        else:
            as_block = (lambda a: a.reshape(1, -1)) if w.ndim == 1 else (lambda a: a)
            g = small_grads[name]
            outs = [g] + [o.reshape(w.shape) for o in _adamw(as_block(w), as_block(g), as_block(m), as_block(v), f"adamw_{name}")]
        for lst, o in zip((grad_list, deltas, new_ms, new_vs), outs):
            lst.append(o)
    return (loss, grad_x[None], *grad_list, *deltas, *new_ms, *new_vs)
```

```python
import functools
import itertools

import numpy as np
import jax
import jax.numpy as jnp
from jax import lax
from jax.experimental import pallas as pl
from jax.experimental.pallas import tpu as pltpu

F32 = jnp.float32
BF16 = jnp.bfloat16

N_DEV = 8
HEAD_DIM = 64
LANES = 128
A_Q_W, A_KV_W, B_W = 512, 128, 512
A_HALF_WINDOW = 128
B_PATTERNS = ((128, 1), (512, 4), (2048, 16))
ROPE_THETA = 10000.0
NORM_EPS = 1e-6
FFN_RES_WEIGHT = 0.5
QK_SCALE = HEAD_DIM ** -0.5
NEG = -1e30

ADAM_LR = 0.001
ADAM_B1 = 0.9
ADAM_B2 = 0.999
ADAM_EPS = 1e-08
ADAM_WD = 0.01
ADAM_STEP = 10

MESH_T = pl.DeviceIdType.MESH
VMEM_LIMIT = 60 * 1024 * 1024


def _cp(sem=None, vmem=VMEM_LIMIT):
    return pltpu.CompilerParams(dimension_semantics=sem, vmem_limit_bytes=vmem)


def _dot_nn(a, b):
    return jnp.dot(a, b, preferred_element_type=F32)


def _dot_nt(a, b):
    return lax.dot_general(a, b, (((1,), (1,)), ((), ())), preferred_element_type=F32)


def _dot_tn(a, b):
    return lax.dot_general(a, b, (((0,), (0,)), ((), ())), preferred_element_type=F32)


def _rstd(xv):
    return lax.rsqrt(jnp.mean(xv * xv, axis=-1, keepdims=True) + NORM_EPS)


def _norm_bwd(dh, xv, r, gn):
    gy = dh * gn
    c = jnp.sum(gy * xv, axis=-1, keepdims=True) * (1.0 / xv.shape[-1])
    dx = r * gy - xv * (r * r * r * c)
    dgn = jnp.sum(dh * (xv * r), axis=0, keepdims=True)
    return dx, dgn


def _accumulate(ref, val, first):
    @pl.when(first)
    def _():
        ref[...] = val

    @pl.when(jnp.logical_not(first))
    def _():
        ref[...] += val


def _mesh_pos():
    return lax.axis_index("x"), lax.axis_index("y"), lax.axis_index("c")


def _dev_index(d):
    return 4 * d[0] + 2 * d[1] + d[2]


class _Comm:
    def __init__(self, inputs, out_shape, scratch, phases):
        self.inputs, self.out_shape, self.scratch, self.phases = inputs, out_shape, scratch, phases

    def specs(self):
        any_spec = pl.BlockSpec(memory_space=pl.ANY)
        return [any_spec] * len(self.inputs), [any_spec] * len(self.out_shape)


def _run_alone(comm, name):
    n_in, n_out = len(comm.inputs), len(comm.out_shape)

    def body(*refs):
        for phase in comm.phases:
            phase(refs[:n_in], refs[n_in:n_in + n_out], refs[n_in + n_out:])

    in_specs, out_specs = comm.specs()
    return pl.pallas_call(body, name=name, out_shape=comm.out_shape, in_specs=in_specs, out_specs=out_specs,
                          scratch_shapes=comm.scratch)(*comm.inputs)


def _run_hosted(comm, at, step, ins, outs, scr):
    for phase, when in zip(comm.phases, at):
        @pl.when(step == when)
        def _(phase=phase):
            phase(ins, outs, scr)


def _split(refs, *counts):
    parts, o = [], 0
    for n in counts:
        parts.append(refs[o:o + n])
        o += n
    return parts + [refs[o:]]


def _gather_plan(packed, rows_list, stages=1):
    n_w = len(rows_list)
    offs = [int(o) for o in np.cumsum([0] + list(rows_list))]
    d = packed.shape[1]
    per = n_w // stages
    groups = [range(s * per, (s + 1) * per) for s in range(stages)]

    def tools(ins, outs, scr):
        p_ref = ins[0]
        send_sems, recv_sems, local_sem = scr
        x, y, c = _mesh_pos()
        me, sibling = (x, y, c), (x, y, 1 - c)
        chips = [(1 - x, y), (x, 1 - y), (1 - x, 1 - y)]

        def rows(w, dev):
            start = pl.multiple_of(_dev_index(dev) * rows_list[w], 16)
            return outs[w].at[pl.ds(start, rows_list[w]), :]

        def mine(w):
            return p_ref.at[pl.ds(offs[w], rows_list[w]), :]

        def copy(k, s, w, block, to, own):
            return pltpu.make_async_remote_copy(
                src_ref=mine(w) if own else rows(w, block), dst_ref=rows(w, block),
                send_sem=send_sems.at[k, s], recv_sem=recv_sems.at[k, s], device_id=to, device_id_type=MESH_T)

        def whole_group(k, s):
            span = p_ref.at[pl.ds(offs[groups[s][0]], offs[groups[s][-1] + 1] - offs[groups[s][0]]), :]
            return pltpu.make_async_remote_copy(
                src_ref=span, dst_ref=span, send_sem=send_sems.at[k, s], recv_sem=recv_sems.at[k, s],
                device_id=me, device_id_type=MESH_T)

        return p_ref, local_sem, me, sibling, chips, c, rows, mine, copy, whole_group

    def start(ins, outs, scr):
        _, local_sem, me, sibling, chips, c, rows, mine, copy, _ = tools(ins, outs, scr)
        for w in range(n_w):
            pltpu.make_async_copy(mine(w), rows(w, me), local_sem).start()
        for s, group in enumerate(groups):
            for w in group:
                copy(0, s, w, me, sibling, True).start()
            for j, chip in enumerate(chips):
                for w in group:
                    copy(1 + j, s, w, me, (*chip, c), True).start()

    def relay(ins, outs, scr):
        _, _, _, sibling, chips, c, _, _, copy, whole_group = tools(ins, outs, scr)
        for s, group in enumerate(groups):
            for j, chip in enumerate(chips):
                whole_group(1 + j, s).wait_recv()
                for w in group:
                    copy(4 + j, s, w, (*chip, c), sibling, False).start()

    def finish(ins, outs, scr):
        p_ref, local_sem, _, _, _, _, _, _, _, whole_group = tools(ins, outs, scr)
        for s in range(stages):
            whole_group(0, s).wait_recv()
            for j in range(3):
                whole_group(4 + j, s).wait_recv()
            for k in range(7):
                whole_group(k, s).wait_send()
        pltpu.make_async_copy(p_ref, p_ref, local_sem).wait()

    return _Comm(
        [packed], [jax.ShapeDtypeStruct((N_DEV * r, d), packed.dtype) for r in rows_list],
        [pltpu.SemaphoreType.DMA((7, stages)), pltpu.SemaphoreType.DMA((7, stages)), pltpu.SemaphoreType.DMA], [start, relay, finish])


def _exchange_plan(groups):
    flat = [a for g in groups for a in g]
    n_g = len(groups)
    sizes = [len(g) for g in groups]
    rows = [g[0].shape[0] // N_DEV for g in groups]
    first = [int(o) for o in np.cumsum([0] + sizes[:-1])]

    def start(srcs, lands, scr):
        send_sems, recv_sems, local_sems = scr
        x, y, c = _mesh_pos()
        me = (x, y, c)
        me_idx = _dev_index(me)

        def block(g, i, dev):
            start_row = pl.multiple_of(_dev_index(dev) * rows[g], 8)
            return srcs[first[g] + i].at[pl.ds(start_row, rows[g]), :]

        def slot(g, i):
            return lands[g].at[me_idx, pl.ds(i * rows[g], rows[g]), :]

        for g in range(n_g):
            for i in range(sizes[g]):
                pltpu.make_async_copy(block(g, i, me), slot(g, i), local_sems.at[g]).start()
        flips = [f for f in itertools.product((0, 1), repeat=3) if any(f)]
        for k, (fx, fy, fc) in enumerate(flips):
            peer = (1 - x if fx else x, 1 - y if fy else y, 1 - c if fc else c)
            for g in range(n_g):
                for i in range(sizes[g]):
                    pltpu.make_async_remote_copy(
                        src_ref=block(g, i, peer), dst_ref=slot(g, i), send_sem=send_sems.at[g, k],
                        recv_sem=recv_sems.at[g, k], device_id=peer, device_id_type=MESH_T).start()

    def finish(srcs, lands, scr):
        send_sems, recv_sems, local_sems = scr
        me = _mesh_pos()
        for k in range(7):
            for g in range(n_g):
                pltpu.make_async_remote_copy(
                    src_ref=lands[g].at[0], dst_ref=lands[g].at[0], send_sem=send_sems.at[g, k],
                    recv_sem=recv_sems.at[g, k], device_id=me, device_id_type=MESH_T).wait()
        for g in range(n_g):
            pltpu.make_async_copy(lands[g].at[0], lands[g].at[0], local_sems.at[g]).wait()

    return _Comm(
        flat, [jax.ShapeDtypeStruct((N_DEV, sizes[g] * rows[g], groups[g][0].shape[1]), groups[g][0].dtype) for g in range(n_g)],
        [pltpu.SemaphoreType.DMA((n_g, 7)), pltpu.SemaphoreType.DMA((n_g, 7)), pltpu.SemaphoreType.DMA((n_g,))],
        [start, finish])


def _comm_parts(comm):
    if comm is None:
        return 0, 0, [], [], [], [], []
    in_specs, out_specs = comm.specs()
    return len(comm.inputs), len(comm.out_shape), in_specs, out_specs, comm.out_shape, comm.scratch, comm.inputs


def _ffn_fwd(x, gn, wg_t, wu_t, wd, name, tm, tf, comm=None, comm_at=None, loss_head=None):
    t, d = x.shape
    f_all = wg_t.shape[0]
    n_f = f_all // tf
    n_ci, n_co, c_in_specs, c_out_specs, c_shapes, c_scratch, c_inputs = _comm_parts(comm)
    n_head, n_y = (2, 4) if loss_head else (0, 1)

    def body(*refs):
        ((x_ref, gn_ref, wg_ref, wu_ref, wd_ref), head_in, c_in, y_outs, (h_ref, gp_ref, up_ref, hid_ref), c_out,
         (h_s, hid_s), c_scr) = _split(refs, 5, n_head, n_ci, n_y, 4, n_co, 2)
        f = pl.program_id(1)
        if comm:
            _run_hosted(comm, comm_at, pl.program_id(0) * n_f + f, c_in, c_out, c_scr)

        @pl.when(f == 0)
        def _():
            xv = x_ref[...]
            h = ((xv * _rstd(xv)) * gn_ref[...]).astype(BF16)
            h_s[...] = h
            h_ref[...] = h

        h = h_s[...]
        cols = pl.ds(pl.multiple_of(f * tf, tf), tf)
        g = _dot_nt(h, wg_ref[cols, :])
        u = _dot_nt(h, wu_ref[cols, :])
        sg = jax.nn.sigmoid(g)
        silu = g * sg
        gp_ref[...] = (u * (sg * (1.0 + g * (1.0 - sg)))).astype(BF16)
        up_ref[...] = silu.astype(BF16)
        hid = (silu * u).astype(BF16)
        hid_ref[...] = hid
        for f0 in range(n_f):
            @pl.when(f == f0)
            def _(f0=f0):
                hid_s[:, f0 * tf:(f0 + 1) * tf] = hid

        @pl.when(f == n_f - 1)
        def _():
            y = x_ref[...] + FFN_RES_WEIGHT * _dot_nn(hid_s[...], wd_ref[...])
            if not loss_head:
                y_outs[0][...] = y
            else:
                (gf_ref, tg_ref), (dy_ref, dgf_ref, sq_ref, half_ref) = head_in, y_outs
                gfv, r, first = gf_ref[...], _rstd(y), pl.program_id(0) == 0
                diff = (y * r) * gfv - tg_ref[...]
                dy, dgf = _norm_bwd(diff * (1.0 / d), y, r, gfv)
                dy_ref[...] = dy
                half_ref[...] = (FFN_RES_WEIGHT * dy).astype(BF16)
                _accumulate(dgf_ref, dgf, first)
                _accumulate(sq_ref, jnp.sum(diff * diff, axis=0, keepdims=True), first)

    tok = pl.BlockSpec((tm, d), lambda i, f: (i, 0))
    row = pl.BlockSpec((1, d), lambda i, f: (0, 0))
    whole = pl.BlockSpec((f_all, d), lambda i, f: (0, 0), pipeline_mode=pl.Buffered(1))
    act = pl.BlockSpec((tm, tf), lambda i, f: (i, f))
    act_shape = jax.ShapeDtypeStruct((t, f_all), BF16)
    tok_f32, tok_bf16, row_f32 = (jax.ShapeDtypeStruct((t, d), F32), jax.ShapeDtypeStruct((t, d), BF16),
                                  jax.ShapeDtypeStruct((1, d), F32))
    y_shapes, y_specs = ([tok_f32, row_f32, row_f32, tok_bf16], [tok, row, row, tok]) if loss_head else ([tok_f32], [tok])
    res = pl.pallas_call(
        body, name=name, grid=(t // tm, n_f),
        out_shape=y_shapes + [tok_bf16, act_shape, act_shape, act_shape] + c_shapes,
        in_specs=[tok, row, whole, whole, whole] + ([row, tok] if loss_head else []) + c_in_specs,
        out_specs=y_specs + [tok, act, act, act] + c_out_specs,
        scratch_shapes=[pltpu.VMEM((tm, d), BF16), pltpu.VMEM((tm, f_all), BF16)] + c_scratch,
        compiler_params=_cp(("arbitrary", "arbitrary")),
    )(x, gn, wg_t, wu_t, wd, *(loss_head or ()), *c_inputs)
    return res[:n_y + 4], res[n_y + 4:]


def _ffn_bwd_hidden(dout, gp, up, hid, wd, name, tt, tf, comm=None, comm_at=None):
    t, d = dout.shape
    f_all = wd.shape[0]
    n_t = t // tt
    n_ci, n_co, c_in_specs, c_out_specs, c_shapes, c_scratch, c_inputs = _comm_parts(comm)

    def body(*refs):
        (dout_ref, gp_ref, up_ref, hid_ref, wd_ref), c_in, (dg_ref, du_ref, gwd_ref), c_out, (acc,), c_scr = _split(
            refs, 5, n_ci, 3, n_co, 1)
        s = pl.program_id(1)
        if comm:
            _run_hosted(comm, comm_at, pl.program_id(0) * n_t + s, c_in, c_out, c_scr)
        doutv = dout_ref[...]
        dhid = _dot_nt(doutv, wd_ref[...])
        dg_ref[...] = (dhid * gp_ref[...].astype(F32)).astype(BF16)
        du_ref[...] = (dhid * up_ref[...].astype(F32)).astype(BF16)
        _accumulate(acc, _dot_tn(hid_ref[...], doutv), s == 0)

        @pl.when(s == n_t - 1)
        def _():
            gwd_ref[...] = acc[...].astype(BF16)

    tok = pl.BlockSpec((tt, d), lambda f, s: (s, 0))
    wblk = pl.BlockSpec((tf, d), lambda f, s: (f, 0))
    act = pl.BlockSpec((tt, tf), lambda f, s: (s, f))
    act_shape = jax.ShapeDtypeStruct((t, f_all), BF16)
    res = pl.pallas_call(
        body, name=name, grid=(f_all // tf, n_t),
        out_shape=[act_shape, act_shape, jax.ShapeDtypeStruct((f_all, d), BF16)] + c_shapes,
        in_specs=[tok, act, act, act, wblk] + c_in_specs, out_specs=[act, act, wblk] + c_out_specs,
        scratch_shapes=[pltpu.VMEM((tf, d), F32)] + c_scratch,
        compiler_params=_cp(("arbitrary", "arbitrary")),
    )(dout, gp, up, hid, wd, *c_inputs)
    return res[:3], res[3:]


def _ffn_bwd_input(dy, x, gn, dg, du, wg_t, wu_t, name, tm, comm=None, comm_at=None):
    t, d = x.shape
    f_all = wg_t.shape[0]
    n_ci, n_co, c_in_specs, c_out_specs, c_shapes, c_scratch, c_inputs = _comm_parts(comm)

    def body(*refs):
        (dy_ref, x_ref, gn_ref, dg_ref, du_ref, wg_ref, wu_ref), c_in, (dx_ref, dgn_ref), c_out, c_scr = _split(
            refs, 7, n_ci, 2, n_co)
        i = pl.program_id(0)
        if comm:
            _run_hosted(comm, comm_at, i, c_in, c_out, c_scr)
        dh = _dot_nn(dg_ref[...], wg_ref[...]) + _dot_nn(du_ref[...], wu_ref[...])
        xv = x_ref[...]
        dxn, dgn = _norm_bwd(dh, xv, _rstd(xv), gn_ref[...])
        dx_ref[...] = dy_ref[...] + dxn
        _accumulate(dgn_ref, dgn, i == 0)

    tok = pl.BlockSpec((tm, d), lambda i: (i, 0))
    row = pl.BlockSpec((1, d), lambda i: (0, 0))
    act = pl.BlockSpec((tm, f_all), lambda i: (i, 0))
    whole = pl.BlockSpec((f_all, d), lambda i: (0, 0))
    res = pl.pallas_call(
        body, name=name, grid=(t // tm,),
        out_shape=[jax.ShapeDtypeStruct((t, d), F32), jax.ShapeDtypeStruct((1, d), F32)] + c_shapes,
        in_specs=[tok, tok, row, act, act, whole, whole] + c_in_specs,
        out_specs=[tok, row] + c_out_specs,
        scratch_shapes=c_scratch,
        compiler_params=_cp(("arbitrary",)),
    )(dy, x, gn, dg, du, wg_t, wu_t, *c_inputs)
    return res[:2], res[2:]


def _token_products(lhs_list, rhs, name, tt, tf, comm=None, comm_at=None):
    n_l = len(lhs_list)
    t, f_all = lhs_list[0].shape
    d = rhs.shape[1]
    n_t = t // tt
    n_ci, n_co, c_in_specs, c_out_specs, c_shapes, c_scratch, c_inputs = _comm_parts(comm)

    def body(*refs):
        lhs_refs, (rhs_ref,), c_in, out_refs, c_out, accs, c_scr = _split(refs, n_l, 1, n_ci, n_l, n_co, n_l)
        s = pl.program_id(1)
        if comm:
            _run_hosted(comm, comm_at, pl.program_id(0) * n_t + s, c_in, c_out, c_scr)
        rv = rhs_ref[...]
        for l_ref, acc in zip(lhs_refs, accs):
            _accumulate(acc, _dot_tn(l_ref[...], rv), s == 0)

        @pl.when(s == n_t - 1)
        def _():
            for o_ref, acc in zip(out_refs, accs):
                o_ref[...] = acc[...].astype(BF16)

    act = pl.BlockSpec((tt, tf), lambda f, s: (s, f))
    tok = pl.BlockSpec((tt, d), lambda f, s: (s, 0))
    wblk = pl.BlockSpec((tf, d), lambda f, s: (f, 0))
    res = pl.pallas_call(
        body, name=name, grid=(f_all // tf, n_t),
        out_shape=[jax.ShapeDtypeStruct((f_all, d), BF16)] * n_l + c_shapes,
        in_specs=[act] * n_l + [tok] + c_in_specs, out_specs=[wblk] * n_l + c_out_specs,
        scratch_shapes=[pltpu.VMEM((tf, d), F32)] * n_l + c_scratch,
        compiler_params=_cp(("arbitrary", "arbitrary")),
    )(*lhs_list, rhs, *c_inputs)
    return res[:n_l], res[n_l:]


def _swap_halves(t):
    w = t.shape[-1]
    lane = lax.broadcasted_iota(jnp.int32, (1, w), 1)
    return jnp.where((lane % HEAD_DIM) < HEAD_DIM // 2, pltpu.roll(t, w - HEAD_DIM // 2, 1), pltpu.roll(t, HEAD_DIM // 2, 1))


def _rope(t, cos, sin_signed):
    reps = t.shape[-1] // LANES
    return t * jnp.tile(cos, (1, reps)) + _swap_halves(t) * jnp.tile(sin_signed, (1, reps))


def _rope_bwd(dt, cos, sin_signed):
    reps = dt.shape[-1] // LANES
    return dt * jnp.tile(cos, (1, reps)) + _swap_halves(dt * jnp.tile(sin_signed, (1, reps)))


DILATIONS = tuple(dil for _, dil in B_PATTERNS if dil > 1)


def _seg_shape(t, dil, w, dtype):
    return jax.ShapeDtypeStruct((dil, t // dil, w), dtype)


def _seg_spec(tm, dil, w):
    return pl.BlockSpec((dil, tm // dil, w), lambda i: (0, i, 0))


def _tile_scratch(tm, w):
    return [pltpu.VMEM((tm, LANES), F32)] * (w // LANES)


def _put_tile(tile, val):
    for c, ref in enumerate(tile):
        ref[...] = val[:, c * LANES:(c + 1) * LANES]


def _get_tile(tile):
    return jnp.concatenate([ref[...] for ref in tile], axis=1)


def _scatter_to_segments(tile, seg_refs):
    for seg_ref, dil in zip(seg_refs, DILATIONS):
        rows = tile[0].shape[0] // dil
        for r in range(dil):
            for c, ref in enumerate(tile):
                seg_ref[r, :, c * LANES:(c + 1) * LANES] = ref[pl.ds(r, rows, stride=dil), :].astype(seg_ref.dtype)


def _gather_from_segments(seg_ref, dil, tile, add=False):
    rows = tile[0].shape[0] // dil
    for r in range(dil):
        for c, ref in enumerate(tile):
            idx = (pl.ds(r, rows, stride=dil), slice(None))
            v = seg_ref[r, :, c * LANES:(c + 1) * LANES].astype(F32)
            ref[idx] = ref[idx] + v if add else v


def _in_proj_fwd(x, gn, win_t, cos, sin_signed, name, tm):
    t, d = x.shape
    in_w = win_t.shape[0]
    n_dil = len(DILATIONS)

    def body(x_ref, gn_ref, w_ref, cos_ref, sin_ref, h_ref, aq_ref, akx_ref, avx_ref, bq_ref, bk_ref, bv_ref, *rest):
        seg_refs, tile = rest[:3 * n_dil], rest[3 * n_dil:]
        xv = x_ref[...]
        h = ((xv * _rstd(xv)) * gn_ref[...]).astype(BF16)
        h_ref[...] = h
        p = _dot_nt(h, w_ref[...])
        cs, sn = cos_ref[...], sin_ref[...]
        o = 0
        aq_ref[...] = (_rope(p[:, o:o + A_Q_W], cs, sn) * QK_SCALE).astype(BF16)
        o += A_Q_W
        ak = _rope(p[:, o:o + A_KV_W], cs, sn)
        o += A_KV_W
        av = p[:, o:o + A_KV_W]
        o += A_KV_W
        low = lax.broadcasted_iota(jnp.int32, (1, LANES), 1) < HEAD_DIM
        for src, dst in ((ak, akx_ref), (av, avx_ref)):
            other = pltpu.roll(src, HEAD_DIM, 1)
            dst[0] = jnp.where(low, src, other).astype(BF16)
            dst[1] = jnp.where(low, other, src).astype(BF16)
        for k, nat_ref in enumerate((bq_ref, bk_ref, bv_ref)):
            val = p[:, o:o + B_W]
            o += B_W
            if k < 2:
                val = _rope(val, cs, sn)
            if k == 0:
                val = val * QK_SCALE
            nat_ref[...] = val.astype(BF16)
            _put_tile(tile, val)
            _scatter_to_segments(tile, seg_refs[k * n_dil:(k + 1) * n_dil])

    tok = lambda w: pl.BlockSpec((tm, w), lambda i: (i, 0))
    kvx = pl.BlockSpec((2, tm, LANES), lambda i: (0, i, 0))
    sd = lambda *s: jax.ShapeDtypeStruct(s, BF16)
    res = pl.pallas_call(
        body, name=name, grid=(t // tm,),
        out_shape=[sd(t, d), sd(t, A_Q_W), sd(2, t, LANES), sd(2, t, LANES), sd(t, B_W), sd(t, B_W), sd(t, B_W)]
        + [_seg_shape(t, dil, B_W, BF16) for _ in range(3) for dil in DILATIONS],
        in_specs=[tok(d), pl.BlockSpec((1, d), lambda i: (0, 0)), pl.BlockSpec((in_w, d), lambda i: (0, 0)),
                  tok(LANES), tok(LANES)],
        out_specs=[tok(d), tok(A_Q_W), kvx, kvx, tok(B_W), tok(B_W), tok(B_W)]
        + [_seg_spec(tm, dil, B_W) for _ in range(3) for dil in DILATIONS],
        scratch_shapes=_tile_scratch(tm, B_W),
        compiler_params=_cp(("arbitrary",)),
    )(x, gn, win_t, cos, sin_signed)
    return res[:7], [res[7 + k * n_dil:7 + (k + 1) * n_dil] for k in range(3)]


def _in_proj_bwd(dres, x, gn, win_t, h, cos, sin_signed, daq, dakx, davx, per_pattern, name, tm):
    t, d = x.shape
    in_w = win_t.shape[0]
    n_t = t // tm
    chunk = in_w // 3
    n_pat = 3 * (1 + len(DILATIONS))

    def body(*refs):
        ((dres_ref, x_ref, gn_ref, w_ref, h_ref, cos_ref, sin_ref, daq_ref, dakx_ref, davx_ref), pat,
         (dx_ref, dgn_ref, gw_ref, half_ref), (dp_s, gw_s), tile) = _split(refs, 10, n_pat, 4, 2)
        i = pl.program_id(0)
        cs, sn = cos_ref[...], sin_ref[...]
        low = lax.broadcasted_iota(jnp.int32, (1, LANES), 1) < HEAD_DIM

        def fold(ref):
            a, b = ref[0].astype(F32), ref[1].astype(F32)
            return jnp.where(low, a + pltpu.roll(a, HEAD_DIM, 1), b + pltpu.roll(b, HEAD_DIM, 1))

        def summed(j):
            _put_tile(tile, pat[j][...].astype(F32))
            for k, dil in enumerate(DILATIONS):
                _gather_from_segments(pat[3 * (1 + k) + j], dil, tile, add=True)
            return _get_tile(tile)

        o = 0
        dp_s[:, o:o + A_Q_W] = _rope_bwd(daq_ref[...].astype(F32) * QK_SCALE, cs, sn).astype(BF16)
        o += A_Q_W
        dp_s[:, o:o + A_KV_W] = _rope_bwd(fold(dakx_ref), cs, sn).astype(BF16)
        o += A_KV_W
        dp_s[:, o:o + A_KV_W] = fold(davx_ref).astype(BF16)
        o += A_KV_W
        dp_s[:, o:o + B_W] = _rope_bwd(summed(0) * QK_SCALE, cs, sn).astype(BF16)
        o += B_W
        dp_s[:, o:o + B_W] = _rope_bwd(summed(1), cs, sn).astype(BF16)
        o += B_W
        dp_s[:, o:o + B_W] = summed(2).astype(BF16)
        dh = _dot_nn(dp_s[...], w_ref[...])
        hv = h_ref[...]
        for c0 in range(0, in_w, chunk):
            _accumulate(gw_s.at[pl.ds(c0, chunk), :], _dot_tn(dp_s[:, c0:c0 + chunk], hv), i == 0)
        xv = x_ref[...]
        dxn, dgn = _norm_bwd(dh, xv, _rstd(xv), gn_ref[...])
        dx = dres_ref[...] + dxn
        dx_ref[...] = dx
        half_ref[...] = (FFN_RES_WEIGHT * dx).astype(BF16)
        _accumulate(dgn_ref, dgn, i == 0)

        @pl.when(i == n_t - 1)
        def _():
            gw_ref[...] = gw_s[...].astype(BF16)

    tok = lambda w: pl.BlockSpec((tm, w), lambda i: (i, 0))
    row = pl.BlockSpec((1, d), lambda i: (0, 0))
    whole = pl.BlockSpec((in_w, d), lambda i: (0, 0))
    weight = pl.BlockSpec((in_w, d), lambda i: (0, 0), pipeline_mode=pl.Buffered(1))
    kvx = pl.BlockSpec((2, tm, LANES), lambda i: (0, i, 0))
    flat = [a if k == 0 else a.reshape(DILATIONS[k - 1], -1, B_W) for k, grads in enumerate(per_pattern) for a in grads]
    return pl.pallas_call(
        body, name=name, grid=(n_t,),
        out_shape=[jax.ShapeDtypeStruct((t, d), F32), jax.ShapeDtypeStruct((1, d), F32),
                   jax.ShapeDtypeStruct((in_w, d), BF16), jax.ShapeDtypeStruct((t, d), BF16)],
        in_specs=[tok(d), tok(d), row, weight, tok(d), tok(LANES), tok(LANES), tok(A_Q_W), kvx, kvx]
        + [tok(B_W)] * 3 + [_seg_spec(tm, dil, B_W) for dil in DILATIONS for _ in range(3)],
        out_specs=[tok(d), row, whole, tok(d)],
        scratch_shapes=[pltpu.VMEM((tm, in_w), BF16), pltpu.VMEM((in_w, d), F32)] + _tile_scratch(tm, B_W),
        compiler_params=_cp(("arbitrary",)),
    )(dres, x, gn, win_t, h, cos, sin_signed, daq, dakx, davx, *flat)


def _merge_out_proj_fwd(x, a_out, outs, lses, wout, name, tm):
    t, d = x.shape
    n_dil = len(DILATIONS)

    def body(x_ref, a_ref, *rest):
        o_refs, l_refs, (w_ref, y_ref, b_ref, lt_ref), b_segs, lt_segs, scratch = _split(rest, 1 + n_dil, 1 + n_dil, 4, n_dil, n_dil)
        n_c = B_W // LANES
        tiles = [scratch[j * n_c:(j + 1) * n_c] for j in range(2 * n_dil)]
        os_, ls = [o_refs[0][...].astype(F32)], [l_refs[0][...]]
        for k, dil in enumerate(DILATIONS):
            _gather_from_segments(o_refs[1 + k], dil, tiles[2 * k])
            _gather_from_segments(l_refs[1 + k], dil, tiles[2 * k + 1])
            os_.append(_get_tile(tiles[2 * k]))
            ls.append(_get_tile(tiles[2 * k + 1]))
        mx = functools.reduce(jnp.maximum, ls)
        es = [jnp.exp(l - mx) for l in ls]
        den = functools.reduce(jnp.add, es)
        b = functools.reduce(jnp.add, [e * o for e, o in zip(es, os_)]) / den
        lt = mx + jnp.log(den)
        bb = b.astype(BF16)
        b_ref[...] = bb
        lt_ref[...] = lt
        y_ref[...] = x_ref[...] + _dot_nn(a_ref[...], w_ref[0:A_Q_W, :]) + _dot_nn(bb, w_ref[A_Q_W:A_Q_W + B_W, :])
        _put_tile(tiles[0], b)
        _scatter_to_segments(tiles[0], b_segs)
        _put_tile(tiles[1], lt)
        _scatter_to_segments(tiles[1], lt_segs)

    tok = lambda w: pl.BlockSpec((tm, w), lambda i: (i, 0))
    segs = [_seg_spec(tm, dil, B_W) for dil in DILATIONS]
    res = pl.pallas_call(
        body, name=name, grid=(t // tm,),
        out_shape=[jax.ShapeDtypeStruct((t, d), F32), jax.ShapeDtypeStruct((t, B_W), BF16), jax.ShapeDtypeStruct((t, B_W), F32)]
        + [_seg_shape(t, dil, B_W, BF16) for dil in DILATIONS] + [_seg_shape(t, dil, B_W, F32) for dil in DILATIONS],
        in_specs=[tok(d), tok(A_Q_W)] + ([tok(B_W)] + segs) * 2 + [pl.BlockSpec(wout.shape, lambda i: (0, 0))],
        out_specs=[tok(d), tok(B_W), tok(B_W)] + segs * 2,
        scratch_shapes=_tile_scratch(tm, B_W) * (2 * n_dil),
        compiler_params=_cp(("arbitrary",)),
    )(x, a_out, *outs, *lses, wout)
    return res[0], [res[1]] + list(res[3:3 + n_dil]), [res[2]] + list(res[3 + n_dil:])


def _out_proj_bwd(dy, a_out, b_out, wout, name, tm):
    t, d = dy.shape
    n_t = t // tm
    n_dil = len(DILATIONS)

    def body(dy_ref, a_ref, b_ref, w_ref, da_ref, db_ref, gw_ref, *rest):
        db_segs, gw_s, tile = rest[:n_dil], rest[n_dil], rest[n_dil + 1:]
        i = pl.program_id(0)
        dyb = dy_ref[...].astype(BF16)
        da_ref[...] = _dot_nt(dyb, w_ref[0:A_Q_W, :]).astype(BF16)
        db = _dot_nt(dyb, w_ref[A_Q_W:A_Q_W + B_W, :])
        db_ref[...] = db.astype(BF16)
        _put_tile(tile, db)
        _scatter_to_segments(tile, db_segs)
        ga = _dot_tn(a_ref[...], dyb)
        gb = _dot_tn(b_ref[...], dyb)

        @pl.when(i == 0)
        def _():
            gw_s[0:A_Q_W, :] = ga
            gw_s[A_Q_W:A_Q_W + B_W, :] = gb

        @pl.when(i > 0)
        def _():
            gw_s[0:A_Q_W, :] += ga
            gw_s[A_Q_W:A_Q_W + B_W, :] += gb

        @pl.when(i == n_t - 1)
        def _():
            gw_ref[...] = gw_s[...].astype(BF16)

    tok = lambda w: pl.BlockSpec((tm, w), lambda i: (i, 0))
    whole = pl.BlockSpec(wout.shape, lambda i: (0, 0))
    res = pl.pallas_call(
        body, name=name, grid=(n_t,),
        out_shape=[jax.ShapeDtypeStruct((t, A_Q_W), BF16), jax.ShapeDtypeStruct((t, B_W), BF16),
                   jax.ShapeDtypeStruct(wout.shape, BF16)] + [_seg_shape(t, dil, B_W, BF16) for dil in DILATIONS],
        in_specs=[tok(d), tok(A_Q_W), tok(B_W), whole],
        out_specs=[tok(A_Q_W), tok(B_W), whole] + [_seg_spec(tm, dil, B_W) for dil in DILATIONS],
        scratch_shapes=[pltpu.VMEM(wout.shape, F32)] + _tile_scratch(tm, B_W),
        compiler_params=_cp(("arbitrary",)),
    )(dy, a_out, b_out, wout)
    return res[0], [res[1]] + list(res[3:]), res[2]


SUB_ROWS = 64
WINDOW_ALIGN = 64


def _sub_band(r0, hw, win, seg_len, t, rel, col):
    ks = pl.multiple_of(jnp.clip(r0 - hw, 0, t - win), WINDOW_ALIGN)
    kpos = col + ks
    seg_lo = (r0 // seg_len) * seg_len
    valid = (jnp.abs(rel + (ks - r0)) <= hw) & (kpos >= seg_lo) & (kpos < seg_lo + seg_len)
    return ks, valid


def _split_heads(v, low):
    zero = jnp.zeros_like(v)
    return jnp.concatenate([jnp.where(low, v, zero), jnp.where(low, zero, v)], axis=0)


def _kv_spec(kv, t):
    if kv.ndim == 3:
        return pl.BlockSpec((None, t, LANES), lambda p, i: (p // 2, 0, 0))
    return pl.BlockSpec((t, LANES), lambda p, i: (0, p))


def _attn_fwd(q, k, v, sink, name, hw, seg_len, tq, has_sink):
    t, width = q.shape
    sb = min(SUB_ROWS, tq)
    win = 2 * hw + LANES

    def body(sink_ref, q_ref, k_ref, v_ref, o_ref, lse_ref):
        p, i = pl.program_id(0), pl.program_id(1)
        q0 = i * tq
        low = lax.broadcasted_iota(jnp.int32, (1, LANES), 1) < HEAD_DIM
        rel = lax.broadcasted_iota(jnp.int32, (sb, win), 1) - lax.broadcasted_iota(jnp.int32, (sb, win), 0)
        col = lax.broadcasted_iota(jnp.int32, (1, win), 1)
        subs = []
        for j in range(tq // sb):
            rows = pl.ds(j * sb, sb)
            ks, valid = _sub_band(q0 + j * sb, hw, win, seg_len, t, rel, col)
            subs.append((rows, ks, valid, _dot_nt(_split_heads(q_ref[rows, :], low), k_ref[pl.ds(ks, win), :])))
        for rows, ks, valid, s in subs:
            vw = v_ref[pl.ds(ks, win), :]
            es, inv, lses = [], [], []
            for a in range(2):
                sa = jnp.where(valid, s[a * sb:(a + 1) * sb], NEG)
                m = jnp.max(sa, axis=1, keepdims=True)
                if has_sink:
                    sk = sink_ref[2 * p + a]
                    m = jnp.maximum(m, sk)
                e = jnp.exp(sa - m)
                den = jnp.sum(e, axis=1, keepdims=True)
                if has_sink:
                    den = den + jnp.exp(sk - m)
                es.append(e.astype(BF16))
                inv.append(1.0 / den)
                lses.append(m + jnp.log(den))
            pv = _dot_nn(jnp.concatenate(es, axis=0), vw)
            o_ref[rows, :] = jnp.where(low, pv[0:sb] * inv[0], pv[sb:2 * sb] * inv[1]).astype(BF16)
            lse_ref[rows, :] = jnp.where(low, lses[0], lses[1])

    tile = pl.BlockSpec((tq, LANES), lambda p, i: (i, p))
    return pl.pallas_call(
        body, name=name, grid=(width // LANES, t // tq),
        out_shape=[jax.ShapeDtypeStruct((t, width), BF16), jax.ShapeDtypeStruct((t, width), F32)],
        in_specs=[pl.BlockSpec(memory_space=pltpu.SMEM), tile, _kv_spec(k, t), _kv_spec(v, t)],
        out_specs=[tile, tile],
        compiler_params=_cp(("arbitrary", "arbitrary")),
    )(sink, q, k, v)


def _attn_bwd(q, k, v, o, do, lse, sink, name, hw, seg_len, tq, has_sink, comm=None, comm_at=None):
    t, width = q.shape
    sb = min(SUB_ROWS, tq)
    win = 2 * hw + LANES
    n_q = t // tq
    shared_kv = k.ndim == 3
    n_ci, n_co, c_in_specs, c_out_specs, c_shapes, c_scratch, c_inputs = _comm_parts(comm)

    def body(*refs):
        ((sink_ref, q_ref, k_ref, v_ref, o_ref, do_ref, lse_ref), c_in, (dq_ref, dk_ref, dv_ref, ds_ref), c_out,
         (dk_s, dv_s), c_scr) = _split(refs, 7, n_ci, 4, n_co, 2)
        p, i = pl.program_id(0), pl.program_id(1)
        if comm:
            _run_hosted(comm, comm_at, p * n_q + i, c_in, c_out, c_scr)
        fresh = (i == 0) & (p % 2 == 0) if shared_kv else i == 0
        last = (i == n_q - 1) & (p % 2 == 1) if shared_kv else i == n_q - 1

        @pl.when(fresh)
        def _():
            dk_s[...] = jnp.zeros_like(dk_s)
            dv_s[...] = jnp.zeros_like(dv_s)

        q0 = i * tq
        low = lax.broadcasted_iota(jnp.int32, (1, LANES), 1) < HEAD_DIM
        rel = lax.broadcasted_iota(jnp.int32, (sb, win), 1) - lax.broadcasted_iota(jnp.int32, (sb, win), 0)
        col = lax.broadcasted_iota(jnp.int32, (1, win), 1)
        dsink = [jnp.zeros((1, 1), F32), jnp.zeros((1, 1), F32)]
        subs = []
        for j in range(tq // sb):
            rows = pl.ds(j * sb, sb)
            ks, valid = _sub_band(q0 + j * sb, hw, win, seg_len, t, rel, col)
            kw = k_ref[pl.ds(ks, win), :]
            dov = do_ref[rows, :]
            q2 = _split_heads(q_ref[rows, :], low)
            do2 = _split_heads(dov, low)
            subs.append((rows, ks, valid, kw, dov, q2, do2, _dot_nt(q2, kw), _dot_nt(do2, v_ref[pl.ds(ks, win), :])))
        probs = []
        for rows, ks, valid, kw, dov, q2, do2, s, dpr in subs:
            prod = dov.astype(F32) * o_ref[rows, :].astype(F32)
            lse_t = lse_ref[rows, :]
            prs, dss = [], []
            for a in range(2):
                mine = low if a == 0 else jnp.logical_not(low)
                lse_a = jnp.max(jnp.where(mine, lse_t, -jnp.inf), axis=1, keepdims=True)
                delta = jnp.sum(jnp.where(mine, prod, 0.0), axis=1, keepdims=True)
                pr = jnp.exp(jnp.where(valid, s[a * sb:(a + 1) * sb], NEG) - lse_a)
                prs.append(pr.astype(BF16))
                dss.append((pr * (dpr[a * sb:(a + 1) * sb] - delta)).astype(BF16))
                if has_sink:
                    dsink[a] = dsink[a] - jnp.sum(jnp.exp(sink_ref[2 * p + a] - lse_a) * delta, axis=0, keepdims=True)
            probs.append((jnp.concatenate(prs, axis=0), jnp.concatenate(dss, axis=0)))
        for (rows, ks, valid, kw, dov, q2, do2, s, dpr), (pr2, ds2) in zip(subs, probs):
            dv_s[pl.ds(ks, win), :] += _dot_tn(pr2, do2)
            dk_s[pl.ds(ks, win), :] += _dot_tn(ds2, q2)
            dq2 = _dot_nn(ds2, kw)
            dq_ref[rows, :] = jnp.where(low, dq2[0:sb], dq2[sb:2 * sb]).astype(BF16)
        ds_ref[...] = jnp.broadcast_to(jnp.where(low, dsink[0], dsink[1]), ds_ref.shape)

        @pl.when(last)
        def _():
            dk_ref[...] = dk_s[...].astype(BF16)
            dv_ref[...] = dv_s[...].astype(BF16)

    tile = pl.BlockSpec((tq, LANES), lambda p, i: (i, p))
    kv_shape = jax.ShapeDtypeStruct(k.shape, BF16)
    res = pl.pallas_call(
        body, name=name, grid=(width // LANES, n_q),
        out_shape=[jax.ShapeDtypeStruct((t, width), BF16), kv_shape, kv_shape,
                   jax.ShapeDtypeStruct((width // LANES, n_q, 8, LANES), F32)] + c_shapes,
        in_specs=[pl.BlockSpec(memory_space=pltpu.SMEM), tile, _kv_spec(k, t), _kv_spec(v, t), tile, tile, tile] + c_in_specs,
        out_specs=[tile, _kv_spec(k, t), _kv_spec(v, t), pl.BlockSpec((None, None, 8, LANES), lambda p, i: (p, i, 0, 0))] + c_out_specs,
        scratch_shapes=[pltpu.VMEM((t, LANES), F32)] * 2 + c_scratch,
        compiler_params=_cp(("arbitrary", "arbitrary")),
    )(sink, q, k, v, o, do, lse, *c_inputs)
    return res[:4], res[4:]


def _adamw(w, g, m, v, name):
    def body(w_ref, g_ref, m_ref, v_ref, d_ref, nm_ref, nv_ref):
        gv = g_ref[...]
        nm = ADAM_B1 * m_ref[...] + (1.0 - ADAM_B1) * gv
        nv = ADAM_B2 * v_ref[...] + (1.0 - ADAM_B2) * (gv * gv)
        m_hat = nm / (1.0 - ADAM_B1 ** ADAM_STEP)
        v_hat = nv / (1.0 - ADAM_B2 ** ADAM_STEP)
        d_ref[...] = -ADAM_LR * (m_hat / (jnp.sqrt(v_hat) + ADAM_EPS) + ADAM_WD * w_ref[...])
        nm_ref[...] = nm
        nv_ref[...] = nv

    shape = jax.ShapeDtypeStruct(w.shape, F32)
    return pl.pallas_call(body, name=name, out_shape=[shape, shape, shape], compiler_params=_cp())(w, g, m, v)


def _adamw_rows(land, j, rows, w, m, v, name):
    d = w.shape[1]

    def body(l_ref, w_ref, m_ref, v_ref, g_ref, d_ref, nm_ref, nv_ref):
        gv = l_ref[0].astype(F32)
        for s in range(1, N_DEV):
            gv = gv + l_ref[s].astype(F32)
        g_ref[...] = gv
        nm = ADAM_B1 * m_ref[...] + (1.0 - ADAM_B1) * gv
        nv = ADAM_B2 * v_ref[...] + (1.0 - ADAM_B2) * (gv * gv)
        m_hat = nm / (1.0 - ADAM_B1 ** ADAM_STEP)
        v_hat = nv / (1.0 - ADAM_B2 ** ADAM_STEP)
        d_ref[...] = -ADAM_LR * (m_hat / (jnp.sqrt(v_hat) + ADAM_EPS) + ADAM_WD * w_ref[...])
        nm_ref[...] = nm
        nv_ref[...] = nv

    halves = 2
    half = pl.BlockSpec((rows // halves, d), lambda i: (i, 0))
    shape = jax.ShapeDtypeStruct((rows, d), F32)
    return pl.pallas_call(
        body, name=name, grid=(halves,), out_shape=[shape] * 4,
        in_specs=[pl.BlockSpec((N_DEV, rows // halves, d), lambda i: (0, halves * j + i, 0)), half, half, half],
        out_specs=[half] * 4,
        compiler_params=_cp(("arbitrary",)),
    )(land, w, m, v)


def _sum_small(land, name):
    def body(l_ref, o_ref):
        acc = l_ref[0]
        for s in range(1, N_DEV):
            acc = acc + l_ref[s]
        o_ref[...] = acc

    return pl.pallas_call(body, name=name, out_shape=jax.ShapeDtypeStruct(land.shape[1:], F32), compiler_params=_cp())(land)


def _rope_lanes(positions):
    inv_freq = 1.0 / (ROPE_THETA ** (jnp.arange(0, HEAD_DIM, 2, dtype=F32) / HEAD_DIM))
    ang = positions.astype(F32)[:, None] * inv_freq
    cos, sin = jnp.cos(ang), jnp.sin(ang)
    return jnp.concatenate([cos, cos, cos, cos], axis=1), jnp.concatenate([-sin, sin, -sin, sin], axis=1)


def kernel(x, positions, norm_ffn1, w_gate1, w_up1, w_down1, norm_mix, w_in, a_sink, w_out, norm_ffn2, w_gate2, w_up2, w_down2, norm_final, loss_target, m_norm_ffn1, m_w_gate1, m_w_up1, m_w_down1, m_norm_mix, m_w_in, m_a_sink, m_w_out, m_norm_ffn2, m_w_gate2, m_w_up2, m_w_down2, m_norm_final, v_norm_ffn1, v_w_gate1, v_w_up1, v_w_down1, v_norm_mix, v_w_in, v_a_sink, v_w_out, v_norm_ffn2, v_w_gate2, v_w_up2, v_w_down2, v_norm_final):
    x = x[0]
    target = loss_target[0]
    t, d = x.shape
    tm_mix = min(512, t)
    tq_of = lambda seg_len, most=2048: min(most, t, max(seg_len, 16 * SUB_ROWS))

    tt = min(1024, t)
    f_all = w_gate1.shape[2] * N_DEV
    tfp = f_all // 2
    tm_fwd, tf_fwd = tm_mix, tfp
    n_steps = (t // tm_fwd) * (f_all // tf_fwd)

    def stacked(shards):
        return jnp.concatenate([s.astype(BF16) for s in shards], axis=0), [s.shape[0] for s in shards]

    packed1, rows1 = stacked([w_gate1[0].T, w_up1[0].T, w_down1[0]])
    packed2, rows2 = stacked([w_in[0].T, w_out[0], w_gate2[0].T, w_up2[0].T, w_down2[0]])
    wg1, wu1, wd1 = _run_alone(_gather_plan(packed1, rows1, stages=len(rows1)), "gather_ffn1_weights")

    cos, sin_signed = _rope_lanes(positions[0])
    sink = a_sink[0]
    no_sink = jnp.zeros_like(sink)

    (x1, h1, gp1, up1, hid1), (win, wout, wg2, wu2, wd2) = _ffn_fwd(
        x, norm_ffn1, wg1, wu1, wd1, "ffn1_fwd", tm_fwd, tf_fwd, _gather_plan(packed2, rows2), [0, (3 * n_steps) // 4, n_steps - 1])
    (h_mix, aq, akx, avx, bq, bk, bv), seg_qkv = _in_proj_fwd(x1, norm_mix, win, cos, sin_signed, "in_proj_fwd", tm_mix)
    a_out, a_lse = _attn_fwd(aq, akx, avx, sink, "attn_a_fwd", A_HALF_WINDOW, t, tq_of(t), True)
    rows_of = lambda a: a.reshape(t, B_W)
    segments_of = lambda a, dil: a if dil == 1 else a.reshape(dil, t // dil, B_W)
    b_qkv, b_outs, b_lses = [], [], []
    for n, (window, dil) in enumerate(B_PATTERNS):
        qs, ks, vs = (bq, bk, bv) if dil == 1 else (rows_of(seg_qkv[k][n - 1]) for k in range(3))
        o_seg, lse_seg = _attn_fwd(qs, ks, vs, no_sink, f"attn_b{dil}_fwd", window // (2 * dil), t // dil,
                                  tq_of(t // dil, 16 * SUB_ROWS), False)
        b_qkv.append((qs, ks, vs))
        b_outs.append(segments_of(o_seg, dil))
        b_lses.append(segments_of(lse_seg, dil))
    x2, b_out, b_lse = _merge_out_proj_fwd(x1, a_out, b_outs, b_lses, wout, "out_proj_fwd", tm_mix)
    (dx3, dg_final, sq, dout2, h2, gp2, up2, hid2), _ = _ffn_fwd(
        x2, norm_ffn2, wg2, wu2, wd2, "ffn2_fwd_loss", tm_fwd, tf_fwd, loss_head=(norm_final.reshape(1, d), target))


    (dgt2, dut2, gwd2), _ = _ffn_bwd_hidden(dout2, gp2, up2, hid2, wd2, "ffn2_bwd_hidden", tm_mix, tfp)
    (dx2, dg_ffn2), _ = _ffn_bwd_input(dx3, x2, norm_ffn2, dgt2, dut2, wg2, wu2, "ffn2_bwd_input", tm_mix)
    (gwg2, gwu2), _ = _token_products([dgt2, dut2], h2, "ffn2_bwd_gate_up", tt, tfp)

    n_att = lambda width, tq: (width // LANES) * (t // tq)
    n_prod = (f_all // tfp) * (t // tt)
    da, db, gwout = _out_proj_bwd(dx2, a_out, b_out[0], wout, "out_proj_bwd", tm_mix)
    (daq, dakx, davx, dsink_parts), (land_wg2,) = _attn_bwd(
        aq, akx, avx, a_out, da, a_lse, sink, "attn_a_bwd", A_HALF_WINDOW, t, tq_of(t), True,
        _exchange_plan([[gwg2]]), [0, n_att(A_Q_W, tq_of(t)) - 1])
    pattern_grads, pattern_lands = [], []
    for n, ((window, dil), (qs, ks, vs)) in enumerate(zip(B_PATTERNS, b_qkv)):
        carried = [gwu2, gwd2][n:n + 1]
        grads, lands = _attn_bwd(qs, ks, vs, rows_of(b_out[n]), rows_of(db[n]), rows_of(b_lse[n]), no_sink, f"attn_b{dil}_bwd",
                                 window // (2 * dil), t // dil, tq_of(t // dil), False,
                                 _exchange_plan([carried]) if carried else None, [0, n_att(B_W, tq_of(t // dil)) - 1])
        pattern_grads.append(grads[:3])
        pattern_lands.extend(lands)
    land_wu2, land_wd2 = pattern_lands
    dx1, dg_mix, gwin, dout1 = _in_proj_bwd(dx2, x1, norm_mix, win, h_mix, cos, sin_signed, daq, dakx, davx, pattern_grads, "in_proj_bwd", tm_mix)

    (dgt1, dut1, gwd1), (land_in, land_out) = _ffn_bwd_hidden(
        dout1, gp1, up1, hid1, wd1, "ffn1_bwd_hidden", tm_mix, tfp, _exchange_plan([[gwin], [gwout]]),
        [0, (f_all // tfp) * (t // tm_mix) - 1])
    (gwg1,), (land_wd1,) = _token_products([dgt1], h1, "ffn1_bwd_gate", tt, tfp, _exchange_plan([[gwd1]]), [0, n_prod - 1])
    (gwu1,), (land_wg1,) = _token_products([dut1], h1, "ffn1_bwd_up", tt, tfp, _exchange_plan([[gwg1]]), [0, n_prod - 1])
    (grad_x, dg_ffn1), (land_wu1,) = _ffn_bwd_input(
        dx1, x, norm_ffn1, dgt1, dut1, wg1, wu1, "ffn1_bwd_input", tm_mix, _exchange_plan([[gwu1]]), [0, t // tm_mix - 1])

    dsink_pairs = jnp.sum(dsink_parts[:, :, 0, :], axis=1)
    dsink = jnp.stack([dsink_pairs[:, 0], dsink_pairs[:, HEAD_DIM]], axis=1).reshape(1, -1)
    small = jnp.concatenate([dg_ffn1, dg_mix, dg_ffn2, dg_final, jnp.pad(dsink, ((0, 0), (0, d - dsink.shape[1]))),
                             sq, jnp.zeros((2, d), F32)], axis=0)
    (land_small,) = _run_alone(_exchange_plan([[jnp.tile(small, (N_DEV, 1))]]), "gather_small_gradients")
    red_small = _sum_small(land_small, "sum_small_grads")
    loss = 0.5 * jnp.sum(red_small[5]) / d

    rf = rows1[0]
    sharded = {"w_gate1": (land_wg1, 0, rf, True), "w_up1": (land_wu1, 0, rf, True), "w_down1": (land_wd1, 0, rf, False),
               "w_in": (land_in, 0, rows2[0], True), "w_out": (land_out, 0, rows2[1], False),
               "w_gate2": (land_wg2, 0, rf, True), "w_up2": (land_wu2, 0, rf, True), "w_down2": (land_wd2, 0, rf, False)}
    n_sink = a_sink.shape[1]
    small_grads = {"norm_ffn1": red_small[0:1], "norm_mix": red_small[1:2], "norm_ffn2": red_small[2:3], "norm_final": red_small[3],
                   "a_sink": red_small[4:5, :n_sink]}
    params = {
        "norm_ffn1": (norm_ffn1, m_norm_ffn1, v_norm_ffn1), "w_gate1": (w_gate1, m_w_gate1, v_w_gate1),
        "w_up1": (w_up1, m_w_up1, v_w_up1), "w_down1": (w_down1, m_w_down1, v_w_down1),
        "norm_mix": (norm_mix, m_norm_mix, v_norm_mix), "w_in": (w_in, m_w_in, v_w_in),
        "a_sink": (a_sink, m_a_sink, v_a_sink), "w_out": (w_out, m_w_out, v_w_out),
        "norm_ffn2": (norm_ffn2, m_norm_ffn2, v_norm_ffn2), "w_gate2": (w_gate2, m_w_gate2, v_w_gate2),
        "w_up2": (w_up2, m_w_up2, v_w_up2), "w_down2": (w_down2, m_w_down2, v_w_down2),
        "norm_final": (norm_final, m_norm_final, v_norm_final),
    }
    grad_list, deltas, new_ms, new_vs = [], [], [], []
    for name, (w, m, v) in params.items():
        if name in sharded:
            land, j, rows, is_transposed = sharded[name]
            view = (lambda a: a[0].T) if is_transposed else (lambda a: a[0])
            back = (lambda a: a.T[None]) if is_transposed else (lambda a: a[None])
            outs = [back(o) for o in _adamw_rows(land, j, rows, view(w), view(m), view(v), f"adamw_{name}")]
        else:
            as_block = (lambda a: a.reshape(1, -1)) if w.ndim == 1 else (lambda a: a)
            g = small_grads[name]
            outs = [g] + [o.reshape(w.shape) for o in _adamw(as_block(w), as_block(g), as_block(m), as_block(v), f"adamw_{name}")]
        for lst, o in zip((grad_list, deltas, new_ms, new_vs), outs):
            lst.append(o)
    return (loss, grad_x[None], *grad_list, *deltas, *new_ms, *new_vs)
```

```python
import functools
import itertools

import numpy as np
import jax
import jax.numpy as jnp
from jax import lax
from jax.experimental import pallas as pl
from jax.experimental.pallas import tpu as pltpu

F32 = jnp.float32
BF16 = jnp.bfloat16

N_DEV = 8
HEAD_DIM = 64
LANES = 128
A_Q_W, A_KV_W, B_W = 512, 128, 512
A_HALF_WINDOW = 128
B_PATTERNS = ((128, 1), (512, 4), (2048, 16))
ROPE_THETA = 10000.0
NORM_EPS = 1e-6
FFN_RES_WEIGHT = 0.5
QK_SCALE = HEAD_DIM ** -0.5
NEG = -1e30

ADAM_LR = 0.001
ADAM_B1 = 0.9
ADAM_B2 = 0.999
ADAM_EPS = 1e-08
ADAM_WD = 0.01
ADAM_STEP = 10

MESH_T = pl.DeviceIdType.MESH
VMEM_LIMIT = 60 * 1024 * 1024


def _cp(sem=None, vmem=VMEM_LIMIT):
    return pltpu.CompilerParams(dimension_semantics=sem, vmem_limit_bytes=vmem)


def _dot_nn(a, b):
    return jnp.dot(a, b, preferred_element_type=F32)


def _dot_nt(a, b):
    return lax.dot_general(a, b, (((1,), (1,)), ((), ())), preferred_element_type=F32)


def _dot_tn(a, b):
    return lax.dot_general(a, b, (((0,), (0,)), ((), ())), preferred_element_type=F32)


def _rstd(xv):
    return lax.rsqrt(jnp.mean(xv * xv, axis=-1, keepdims=True) + NORM_EPS)


def _norm_bwd(dh, xv, r, gn):
    gy = dh * gn
    c = jnp.sum(gy * xv, axis=-1, keepdims=True) * (1.0 / xv.shape[-1])
    dx = r * gy - xv * (r * r * r * c)
    dgn = jnp.sum(dh * (xv * r), axis=0, keepdims=True)
    return dx, dgn


def _accumulate(ref, val, first):
    @pl.when(first)
    def _():
        ref[...] = val

    @pl.when(jnp.logical_not(first))
    def _():
        ref[...] += val


def _mesh_pos():
    return lax.axis_index("x"), lax.axis_index("y"), lax.axis_index("c")


def _dev_index(d):
    return 4 * d[0] + 2 * d[1] + d[2]


class _Comm:
    def __init__(self, inputs, out_shape, scratch, phases):
        self.inputs, self.out_shape, self.scratch, self.phases = inputs, out_shape, scratch, phases

    def specs(self):
        any_spec = pl.BlockSpec(memory_space=pl.ANY)
        return [any_spec] * len(self.inputs), [any_spec] * len(self.out_shape)


def _run_alone(comm, name):
    n_in, n_out = len(comm.inputs), len(comm.out_shape)

    def body(*refs):
        for phase in comm.phases:
            phase(refs[:n_in], refs[n_in:n_in + n_out], refs[n_in + n_out:])

    in_specs, out_specs = comm.specs()
    return pl.pallas_call(body, name=name, out_shape=comm.out_shape, in_specs=in_specs, out_specs=out_specs,
                          scratch_shapes=comm.scratch)(*comm.inputs)


def _run_alone_stacking(comm, shards, name):
    n_in, n_out = len(comm.inputs), len(comm.out_shape)
    offs = [int(o) for o in np.cumsum([0] + [s.shape[0] for s in shards])]

    def body(*refs):
        c_in, s_refs, c_out, (stack_ref,), scr = _split(refs, n_in, len(shards), n_out, 1)
        comm.phases[0](c_in, c_out, scr)
        for s_ref, lo, hi in zip(s_refs, offs[:-1], offs[1:]):
            stack_ref[lo:hi, :] = s_ref[...].astype(BF16)
        for phase in comm.phases[1:]:
            phase(c_in, c_out, scr)

    in_specs, out_specs = comm.specs()
    vmem = pl.BlockSpec(memory_space=pltpu.VMEM)
    res = pl.pallas_call(
        body, name=name, out_shape=comm.out_shape + [jax.ShapeDtypeStruct((offs[-1], shards[0].shape[1]), BF16)],
        in_specs=in_specs + [vmem] * len(shards), out_specs=out_specs + [vmem], scratch_shapes=comm.scratch,
        compiler_params=_cp())(*comm.inputs, *shards)
    return res[:n_out], res[n_out]


def _run_hosted(comm, at, step, ins, outs, scr):
    for phase, when in zip(comm.phases, at):
        @pl.when(step == when)
        def _(phase=phase):
            phase(ins, outs, scr)


def _split(refs, *counts):
    parts, o = [], 0
    for n in counts:
        parts.append(refs[o:o + n])
        o += n
    return parts + [refs[o:]]


def _gather_plan(packed, rows_list, stages=1):
    n_w = len(rows_list)
    offs = [int(o) for o in np.cumsum([0] + list(rows_list))]
    d = packed.shape[1]
    per = n_w // stages
    groups = [range(s * per, (s + 1) * per) for s in range(stages)]

    def tools(ins, outs, scr):
        p_ref = ins[0]
        send_sems, recv_sems, local_sem = scr
        x, y, c = _mesh_pos()
        me, sibling = (x, y, c), (x, y, 1 - c)
        chips = [(1 - x, y), (x, 1 - y), (1 - x, 1 - y)]

        def rows(w, dev):
            start = pl.multiple_of(_dev_index(dev) * rows_list[w], 16)
            return outs[w].at[pl.ds(start, rows_list[w]), :]

        def mine(w):
            return p_ref.at[pl.ds(offs[w], rows_list[w]), :]

        def copy(k, s, w, block, to, own):
            return pltpu.make_async_remote_copy(
                src_ref=mine(w) if own else rows(w, block), dst_ref=rows(w, block),
                send_sem=send_sems.at[k, s], recv_sem=recv_sems.at[k, s], device_id=to, device_id_type=MESH_T)

        def whole_group(k, s):
            span = p_ref.at[pl.ds(offs[groups[s][0]], offs[groups[s][-1] + 1] - offs[groups[s][0]]), :]
            return pltpu.make_async_remote_copy(
                src_ref=span, dst_ref=span, send_sem=send_sems.at[k, s], recv_sem=recv_sems.at[k, s],
                device_id=me, device_id_type=MESH_T)

        return p_ref, local_sem, me, sibling, chips, c, rows, mine, copy, whole_group

    def start(ins, outs, scr):
        _, local_sem, me, sibling, chips, c, rows, mine, copy, _ = tools(ins, outs, scr)
        for w in range(n_w):
            pltpu.make_async_copy(mine(w), rows(w, me), local_sem).start()
        for s, group in enumerate(groups):
            for w in group:
                copy(0, s, w, me, sibling, True).start()
            for j, chip in enumerate(chips):
                for w in group:
                    copy(1 + j, s, w, me, (*chip, c), True).start()

    def relay(ins, outs, scr):
        _, _, _, sibling, chips, c, _, _, copy, whole_group = tools(ins, outs, scr)
        for s, group in enumerate(groups):
            for j, chip in enumerate(chips):
                whole_group(1 + j, s).wait_recv()
                for w in group:
                    copy(4 + j, s, w, (*chip, c), sibling, False).start()

    def finish(ins, outs, scr):
        p_ref, local_sem, _, _, _, _, _, _, _, whole_group = tools(ins, outs, scr)
        for s in range(stages):
            whole_group(0, s).wait_recv()
            for j in range(3):
                whole_group(4 + j, s).wait_recv()
            for k in range(7):
                whole_group(k, s).wait_send()
        pltpu.make_async_copy(p_ref, p_ref, local_sem).wait()

    return _Comm(
        [packed], [jax.ShapeDtypeStruct((N_DEV * r, d), packed.dtype) for r in rows_list],
        [pltpu.SemaphoreType.DMA((7, stages)), pltpu.SemaphoreType.DMA((7, stages)), pltpu.SemaphoreType.DMA], [start, relay, finish])


def _exchange_plan(groups):
    flat = [a for g in groups for a in g]
    n_g = len(groups)
    sizes = [len(g) for g in groups]
    rows = [g[0].shape[0] // N_DEV for g in groups]
    first = [int(o) for o in np.cumsum([0] + sizes[:-1])]

    def start(srcs, lands, scr):
        send_sems, recv_sems, local_sems = scr
        x, y, c = _mesh_pos()
        me = (x, y, c)
        me_idx = _dev_index(me)

        def block(g, i, dev):
            start_row = pl.multiple_of(_dev_index(dev) * rows[g], 8)
            return srcs[first[g] + i].at[pl.ds(start_row, rows[g]), :]

        def slot(g, i):
            return lands[g].at[me_idx, pl.ds(i * rows[g], rows[g]), :]

        for g in range(n_g):
            for i in range(sizes[g]):
                pltpu.make_async_copy(block(g, i, me), slot(g, i), local_sems.at[g]).start()
        flips = [f for f in itertools.product((0, 1), repeat=3) if any(f)]
        for k, (fx, fy, fc) in enumerate(flips):
            peer = (1 - x if fx else x, 1 - y if fy else y, 1 - c if fc else c)
            for g in range(n_g):
                for i in range(sizes[g]):
                    pltpu.make_async_remote_copy(
                        src_ref=block(g, i, peer), dst_ref=slot(g, i), send_sem=send_sems.at[g, k],
                        recv_sem=recv_sems.at[g, k], device_id=peer, device_id_type=MESH_T).start()

    def finish(srcs, lands, scr):
        send_sems, recv_sems, local_sems = scr
        me = _mesh_pos()
        for k in range(7):
            for g in range(n_g):
                pltpu.make_async_remote_copy(
                    src_ref=lands[g].at[0], dst_ref=lands[g].at[0], send_sem=send_sems.at[g, k],
                    recv_sem=recv_sems.at[g, k], device_id=me, device_id_type=MESH_T).wait()
        for g in range(n_g):
            pltpu.make_async_copy(lands[g].at[0], lands[g].at[0], local_sems.at[g]).wait()

    return _Comm(
        flat, [jax.ShapeDtypeStruct((N_DEV, sizes[g] * rows[g], groups[g][0].shape[1]), groups[g][0].dtype) for g in range(n_g)],
        [pltpu.SemaphoreType.DMA((n_g, 7)), pltpu.SemaphoreType.DMA((n_g, 7)), pltpu.SemaphoreType.DMA((n_g,))],
        [start, finish])


def _comm_parts(comm):
    if comm is None:
        return 0, 0, [], [], [], [], []
    in_specs, out_specs = comm.specs()
    return len(comm.inputs), len(comm.out_shape), in_specs, out_specs, comm.out_shape, comm.scratch, comm.inputs


def _ffn_fwd(x, gn, wg_t, wu_t, wd, name, tm, tf, comm=None, comm_at=None, loss_head=None):
    t, d = x.shape
    f_all = wg_t.shape[0]
    n_f = f_all // tf
    n_ci, n_co, c_in_specs, c_out_specs, c_shapes, c_scratch, c_inputs = _comm_parts(comm)
    n_head, n_y = (2, 4) if loss_head else (0, 1)

    def body(*refs):
        ((x_ref, gn_ref, wg_ref, wu_ref, wd_ref), head_in, c_in, y_outs, (h_ref, gp_ref, up_ref, hid_ref), c_out,
         (h_s, hid_s), c_scr) = _split(refs, 5, n_head, n_ci, n_y, 4, n_co, 2)
        f = pl.program_id(1)
        if comm:
            _run_hosted(comm, comm_at, pl.program_id(0) * n_f + f, c_in, c_out, c_scr)

        @pl.when(f == 0)
        def _():
            xv = x_ref[...]
            h = ((xv * _rstd(xv)) * gn_ref[...]).astype(BF16)
            h_s[...] = h
            h_ref[...] = h

        h = h_s[...]
        cols = pl.ds(pl.multiple_of(f * tf, tf), tf)
        g = _dot_nt(h, wg_ref[cols, :])
        u = _dot_nt(h, wu_ref[cols, :])
        sg = jax.nn.sigmoid(g)
        silu = g * sg
        gp_ref[...] = (u * (sg * (1.0 + g * (1.0 - sg)))).astype(BF16)
        up_ref[...] = silu.astype(BF16)
        hid = (silu * u).astype(BF16)
        hid_ref[...] = hid
        for f0 in range(n_f):
            @pl.when(f == f0)
            def _(f0=f0):
                hid_s[:, f0 * tf:(f0 + 1) * tf] = hid

        @pl.when(f == n_f - 1)
        def _():
            y = x_ref[...] + FFN_RES_WEIGHT * _dot_nn(hid_s[...], wd_ref[...])
            if not loss_head:
                y_outs[0][...] = y
            else:
                (gf_ref, tg_ref), (dy_ref, dgf_ref, sq_ref, half_ref) = head_in, y_outs
                gfv, r, first = gf_ref[...], _rstd(y), pl.program_id(0) == 0
                diff = (y * r) * gfv - tg_ref[...]
                dy, dgf = _norm_bwd(diff * (1.0 / d), y, r, gfv)
                dy_ref[...] = dy
                half_ref[...] = (FFN_RES_WEIGHT * dy).astype(BF16)
                _accumulate(dgf_ref, dgf, first)
                _accumulate(sq_ref, jnp.sum(diff * diff, axis=0, keepdims=True), first)

    tok = pl.BlockSpec((tm, d), lambda i, f: (i, 0))
    row = pl.BlockSpec((1, d), lambda i, f: (0, 0))
    whole = pl.BlockSpec((f_all, d), lambda i, f: (0, 0), pipeline_mode=pl.Buffered(1))
    act = pl.BlockSpec((tm, tf), lambda i, f: (i, f))
    act_shape = jax.ShapeDtypeStruct((t, f_all), BF16)
    tok_f32, tok_bf16, row_f32 = (jax.ShapeDtypeStruct((t, d), F32), jax.ShapeDtypeStruct((t, d), BF16),
                                  jax.ShapeDtypeStruct((1, d), F32))
    y_shapes, y_specs = ([tok_f32, row_f32, row_f32, tok_bf16], [tok, row, row, tok]) if loss_head else ([tok_f32], [tok])
    res = pl.pallas_call(
        body, name=name, grid=(t // tm, n_f),
        out_shape=y_shapes + [tok_bf16, act_shape, act_shape, act_shape] + c_shapes,
        in_specs=[tok, row, whole, whole, whole] + ([row, tok] if loss_head else []) + c_in_specs,
        out_specs=y_specs + [tok, act, act, act] + c_out_specs,
        scratch_shapes=[pltpu.VMEM((tm, d), BF16), pltpu.VMEM((tm, f_all), BF16)] + c_scratch,
        compiler_params=_cp(("arbitrary", "arbitrary")),
    )(x, gn, wg_t, wu_t, wd, *(loss_head or ()), *c_inputs)
    return res[:n_y + 4], res[n_y + 4:]


def _ffn_bwd_hidden(dout, gp, up, hid, wd, name, tt, tf, comm=None, comm_at=None):
    t, d = dout.shape
    f_all = wd.shape[0]
    n_t = t // tt
    n_ci, n_co, c_in_specs, c_out_specs, c_shapes, c_scratch, c_inputs = _comm_parts(comm)

    def body(*refs):
        (dout_ref, gp_ref, up_ref, hid_ref, wd_ref), c_in, (dg_ref, du_ref, gwd_ref), c_out, (acc,), c_scr = _split(
            refs, 5, n_ci, 3, n_co, 1)
        s = pl.program_id(1)
        if comm:
            _run_hosted(comm, comm_at, pl.program_id(0) * n_t + s, c_in, c_out, c_scr)
        doutv = dout_ref[...]
        dhid = _dot_nt(doutv, wd_ref[...])
        dg_ref[...] = (dhid * gp_ref[...].astype(F32)).astype(BF16)
        du_ref[...] = (dhid * up_ref[...].astype(F32)).astype(BF16)
        _accumulate(acc, _dot_tn(hid_ref[...], doutv), s == 0)

        @pl.when(s == n_t - 1)
        def _():
            gwd_ref[...] = acc[...].astype(BF16)

    tok = pl.BlockSpec((tt, d), lambda f, s: (s, 0))
    wblk = pl.BlockSpec((tf, d), lambda f, s: (f, 0))
    act = pl.BlockSpec((tt, tf), lambda f, s: (s, f))
    act_shape = jax.ShapeDtypeStruct((t, f_all), BF16)
    res = pl.pallas_call(
        body, name=name, grid=(f_all // tf, n_t),
        out_shape=[act_shape, act_shape, jax.ShapeDtypeStruct((f_all, d), BF16)] + c_shapes,
        in_specs=[tok, act, act, act, wblk] + c_in_specs, out_specs=[act, act, wblk] + c_out_specs,
        scratch_shapes=[pltpu.VMEM((tf, d), F32)] + c_scratch,
        compiler_params=_cp(("arbitrary", "arbitrary")),
    )(dout, gp, up, hid, wd, *c_inputs)
    return res[:3], res[3:]


def _ffn_bwd_input(dy, x, gn, dg, du, wg_t, wu_t, name, tm, comm=None, comm_at=None):
    t, d = x.shape
    f_all = wg_t.shape[0]
    n_ci, n_co, c_in_specs, c_out_specs, c_shapes, c_scratch, c_inputs = _comm_parts(comm)

    def body(*refs):
        (dy_ref, x_ref, gn_ref, dg_ref, du_ref, wg_ref, wu_ref), c_in, (dx_ref, dgn_ref), c_out, c_scr = _split(
            refs, 7, n_ci, 2, n_co)
        i = pl.program_id(0)
        if comm:
            _run_hosted(comm, comm_at, i, c_in, c_out, c_scr)
        dh = _dot_nn(dg_ref[...], wg_ref[...]) + _dot_nn(du_ref[...], wu_ref[...])
        xv = x_ref[...]
        dxn, dgn = _norm_bwd(dh, xv, _rstd(xv), gn_ref[...])
        dx_ref[...] = dy_ref[...] + dxn
        _accumulate(dgn_ref, dgn, i == 0)

    tok = pl.BlockSpec((tm, d), lambda i: (i, 0))
    row = pl.BlockSpec((1, d), lambda i: (0, 0))
    act = pl.BlockSpec((tm, f_all), lambda i: (i, 0))
    whole = pl.BlockSpec((f_all, d), lambda i: (0, 0))
    res = pl.pallas_call(
        body, name=name, grid=(t // tm,),
        out_shape=[jax.ShapeDtypeStruct((t, d), F32), jax.ShapeDtypeStruct((1, d), F32)] + c_shapes,
        in_specs=[tok, tok, row, act, act, whole, whole] + c_in_specs,
        out_specs=[tok, row] + c_out_specs,
        scratch_shapes=c_scratch,
        compiler_params=_cp(("arbitrary",)),
    )(dy, x, gn, dg, du, wg_t, wu_t, *c_inputs)
    return res[:2], res[2:]


def _token_products(lhs_list, rhs, name, tt, tf, comm=None, comm_at=None):
    n_l = len(lhs_list)
    t, f_all = lhs_list[0].shape
    d = rhs.shape[1]
    n_t = t // tt
    n_ci, n_co, c_in_specs, c_out_specs, c_shapes, c_scratch, c_inputs = _comm_parts(comm)

    def body(*refs):
        lhs_refs, (rhs_ref,), c_in, out_refs, c_out, accs, c_scr = _split(refs, n_l, 1, n_ci, n_l, n_co, n_l)
        s = pl.program_id(1)
        if comm:
            _run_hosted(comm, comm_at, pl.program_id(0) * n_t + s, c_in, c_out, c_scr)
        rv = rhs_ref[...]
        for l_ref, acc in zip(lhs_refs, accs):
            _accumulate(acc, _dot_tn(l_ref[...], rv), s == 0)

        @pl.when(s == n_t - 1)
        def _():
            for o_ref, acc in zip(out_refs, accs):
                o_ref[...] = acc[...].astype(BF16)

    act = pl.BlockSpec((tt, tf), lambda f, s: (s, f))
    tok = pl.BlockSpec((tt, d), lambda f, s: (s, 0))
    wblk = pl.BlockSpec((tf, d), lambda f, s: (f, 0))
    res = pl.pallas_call(
        body, name=name, grid=(f_all // tf, n_t),
        out_shape=[jax.ShapeDtypeStruct((f_all, d), BF16)] * n_l + c_shapes,
        in_specs=[act] * n_l + [tok] + c_in_specs, out_specs=[wblk] * n_l + c_out_specs,
        scratch_shapes=[pltpu.VMEM((tf, d), F32)] * n_l + c_scratch,
        compiler_params=_cp(("arbitrary", "arbitrary")),
    )(*lhs_list, rhs, *c_inputs)
    return res[:n_l], res[n_l:]


def _swap_halves(t):
    w = t.shape[-1]
    lane = lax.broadcasted_iota(jnp.int32, (1, w), 1)
    return jnp.where((lane % HEAD_DIM) < HEAD_DIM // 2, pltpu.roll(t, w - HEAD_DIM // 2, 1), pltpu.roll(t, HEAD_DIM // 2, 1))


def _rope(t, cos, sin_signed):
    reps = t.shape[-1] // LANES
    return t * jnp.tile(cos, (1, reps)) + _swap_halves(t) * jnp.tile(sin_signed, (1, reps))


def _rope_bwd(dt, cos, sin_signed):
    reps = dt.shape[-1] // LANES
    return dt * jnp.tile(cos, (1, reps)) + _swap_halves(dt * jnp.tile(sin_signed, (1, reps)))


DILATIONS = tuple(dil for _, dil in B_PATTERNS if dil > 1)


def _seg_shape(t, dil, w, dtype):
    return jax.ShapeDtypeStruct((dil, t // dil, w), dtype)


def _seg_spec(tm, dil, w):
    return pl.BlockSpec((dil, tm // dil, w), lambda i: (0, i, 0))


def _tile_scratch(tm, w):
    return [pltpu.VMEM((tm, LANES), F32)] * (w // LANES)


def _put_tile(tile, val):
    for c, ref in enumerate(tile):
        ref[...] = val[:, c * LANES:(c + 1) * LANES]


def _get_tile(tile):
    return jnp.concatenate([ref[...] for ref in tile], axis=1)


def _scatter_to_segments(tile, seg_refs):
    for seg_ref, dil in zip(seg_refs, DILATIONS):
        rows = tile[0].shape[0] // dil
        for r in range(dil):
            for c, ref in enumerate(tile):
                seg_ref[r, :, c * LANES:(c + 1) * LANES] = ref[pl.ds(r, rows, stride=dil), :].astype(seg_ref.dtype)


def _gather_from_segments(seg_ref, dil, tile, add=False):
    rows = tile[0].shape[0] // dil
    for r in range(dil):
        for c, ref in enumerate(tile):
            idx = (pl.ds(r, rows, stride=dil), slice(None))
            v = seg_ref[r, :, c * LANES:(c + 1) * LANES].astype(F32)
            ref[idx] = ref[idx] + v if add else v


def _in_proj_fwd(x, gn, win_t, cos, sin_signed, name, tm):
    t, d = x.shape
    in_w = win_t.shape[0]
    n_dil = len(DILATIONS)

    def body(x_ref, gn_ref, w_ref, cos_ref, sin_ref, h_ref, aq_ref, akx_ref, avx_ref, bq_ref, bk_ref, bv_ref, *rest):
        seg_refs, tile = rest[:3 * n_dil], rest[3 * n_dil:]
        xv = x_ref[...]
        h = ((xv * _rstd(xv)) * gn_ref[...]).astype(BF16)
        h_ref[...] = h
        p = _dot_nt(h, w_ref[...])
        cs, sn = cos_ref[...], sin_ref[...]
        o = 0
        aq_ref[...] = (_rope(p[:, o:o + A_Q_W], cs, sn) * QK_SCALE).astype(BF16)
        o += A_Q_W
        ak = _rope(p[:, o:o + A_KV_W], cs, sn)
        o += A_KV_W
        av = p[:, o:o + A_KV_W]
        o += A_KV_W
        low = lax.broadcasted_iota(jnp.int32, (1, LANES), 1) < HEAD_DIM
        for src, dst in ((ak, akx_ref), (av, avx_ref)):
            other = pltpu.roll(src, HEAD_DIM, 1)
            dst[0] = jnp.where(low, src, other).astype(BF16)
            dst[1] = jnp.where(low, other, src).astype(BF16)
        for k, nat_ref in enumerate((bq_ref, bk_ref, bv_ref)):
            val = p[:, o:o + B_W]
            o += B_W
            if k < 2:
                val = _rope(val, cs, sn)
            if k == 0:
                val = val * QK_SCALE
            nat_ref[...] = val.astype(BF16)
            _put_tile(tile, val)
            _scatter_to_segments(tile, seg_refs[k * n_dil:(k + 1) * n_dil])

    tok = lambda w: pl.BlockSpec((tm, w), lambda i: (i, 0))
    kvx = pl.BlockSpec((2, tm, LANES), lambda i: (0, i, 0))
    sd = lambda *s: jax.ShapeDtypeStruct(s, BF16)
    res = pl.pallas_call(
        body, name=name, grid=(t // tm,),
        out_shape=[sd(t, d), sd(t, A_Q_W), sd(2, t, LANES), sd(2, t, LANES), sd(t, B_W), sd(t, B_W), sd(t, B_W)]
        + [_seg_shape(t, dil, B_W, BF16) for _ in range(3) for dil in DILATIONS],
        in_specs=[tok(d), pl.BlockSpec((1, d), lambda i: (0, 0)), pl.BlockSpec((in_w, d), lambda i: (0, 0)),
                  tok(LANES), tok(LANES)],
        out_specs=[tok(d), tok(A_Q_W), kvx, kvx, tok(B_W), tok(B_W), tok(B_W)]
        + [_seg_spec(tm, dil, B_W) for _ in range(3) for dil in DILATIONS],
        scratch_shapes=_tile_scratch(tm, B_W),
        compiler_params=_cp(("arbitrary",)),
    )(x, gn, win_t, cos, sin_signed)
    return res[:7], [res[7 + k * n_dil:7 + (k + 1) * n_dil] for k in range(3)]


def _in_proj_bwd(dres, x, gn, win_t, h, cos, sin_signed, daq, dakx, davx, per_pattern, name, tm):
    t, d = x.shape
    in_w = win_t.shape[0]
    n_t = t // tm
    chunk = in_w // 3
    n_pat = 3 * (1 + len(DILATIONS))

    def body(*refs):
        ((dres_ref, x_ref, gn_ref, w_ref, h_ref, cos_ref, sin_ref, daq_ref, dakx_ref, davx_ref), pat,
         (dx_ref, dgn_ref, gw_ref, half_ref), (dp_s, gw_s), tile) = _split(refs, 10, n_pat, 4, 2)
        i = pl.program_id(0)
        cs, sn = cos_ref[...], sin_ref[...]
        low = lax.broadcasted_iota(jnp.int32, (1, LANES), 1) < HEAD_DIM

        def fold(ref):
            a, b = ref[0].astype(F32), ref[1].astype(F32)
            return jnp.where(low, a + pltpu.roll(a, HEAD_DIM, 1), b + pltpu.roll(b, HEAD_DIM, 1))

        def summed(j):
            _put_tile(tile, pat[j][...].astype(F32))
            for k, dil in enumerate(DILATIONS):
                _gather_from_segments(pat[3 * (1 + k) + j], dil, tile, add=True)
            return _get_tile(tile)

        o = 0
        dp_s[:, o:o + A_Q_W] = _rope_bwd(daq_ref[...].astype(F32) * QK_SCALE, cs, sn).astype(BF16)
        o += A_Q_W
        dp_s[:, o:o + A_KV_W] = _rope_bwd(fold(dakx_ref), cs, sn).astype(BF16)
        o += A_KV_W
        dp_s[:, o:o + A_KV_W] = fold(davx_ref).astype(BF16)
        o += A_KV_W
        dp_s[:, o:o + B_W] = _rope_bwd(summed(0) * QK_SCALE, cs, sn).astype(BF16)
        o += B_W
        dp_s[:, o:o + B_W] = _rope_bwd(summed(1), cs, sn).astype(BF16)
        o += B_W
        dp_s[:, o:o + B_W] = summed(2).astype(BF16)
        dh = _dot_nn(dp_s[...], w_ref[...])
        hv = h_ref[...]
        for c0 in range(0, in_w, chunk):
            _accumulate(gw_s.at[pl.ds(c0, chunk), :], _dot_tn(dp_s[:, c0:c0 + chunk], hv), i == 0)
        xv = x_ref[...]
        dxn, dgn = _norm_bwd(dh, xv, _rstd(xv), gn_ref[...])
        dx = dres_ref[...] + dxn
        dx_ref[...] = dx
        half_ref[...] = (FFN_RES_WEIGHT * dx).astype(BF16)
        _accumulate(dgn_ref, dgn, i == 0)

        @pl.when(i == n_t - 1)
        def _():
            gw_ref[...] = gw_s[...].astype(BF16)

    tok = lambda w: pl.BlockSpec((tm, w), lambda i: (i, 0))
    row = pl.BlockSpec((1, d), lambda i: (0, 0))
    whole = pl.BlockSpec((in_w, d), lambda i: (0, 0))
    weight = pl.BlockSpec((in_w, d), lambda i: (0, 0), pipeline_mode=pl.Buffered(1))
    kvx = pl.BlockSpec((2, tm, LANES), lambda i: (0, i, 0))
    flat = [a if k == 0 else a.reshape(DILATIONS[k - 1], -1, B_W) for k, grads in enumerate(per_pattern) for a in grads]
    return pl.pallas_call(
        body, name=name, grid=(n_t,),
        out_shape=[jax.ShapeDtypeStruct((t, d), F32), jax.ShapeDtypeStruct((1, d), F32),
                   jax.ShapeDtypeStruct((in_w, d), BF16), jax.ShapeDtypeStruct((t, d), BF16)],
        in_specs=[tok(d), tok(d), row, weight, tok(d), tok(LANES), tok(LANES), tok(A_Q_W), kvx, kvx]
        + [tok(B_W)] * 3 + [_seg_spec(tm, dil, B_W) for dil in DILATIONS for _ in range(3)],
        out_specs=[tok(d), row, whole, tok(d)],
        scratch_shapes=[pltpu.VMEM((tm, in_w), BF16), pltpu.VMEM((in_w, d), F32)] + _tile_scratch(tm, B_W),
        compiler_params=_cp(("arbitrary",)),
    )(dres, x, gn, win_t, h, cos, sin_signed, daq, dakx, davx, *flat)


def _merge_out_proj_fwd(x, a_out, outs, lses, wout, name, tm):
    t, d = x.shape
    n_dil = len(DILATIONS)

    def body(x_ref, a_ref, *rest):
        o_refs, l_refs, (w_ref, y_ref, b_ref, lt_ref), b_segs, lt_segs, scratch = _split(rest, 1 + n_dil, 1 + n_dil, 4, n_dil, n_dil)
        n_c = B_W // LANES
        tiles = [scratch[j * n_c:(j + 1) * n_c] for j in range(2 * n_dil)]
        os_, ls = [o_refs[0][...].astype(F32)], [l_refs[0][...]]
        for k, dil in enumerate(DILATIONS):
            _gather_from_segments(o_refs[1 + k], dil, tiles[2 * k])
            _gather_from_segments(l_refs[1 + k], dil, tiles[2 * k + 1])
            os_.append(_get_tile(tiles[2 * k]))
            ls.append(_get_tile(tiles[2 * k + 1]))
        mx = functools.reduce(jnp.maximum, ls)
        es = [jnp.exp(l - mx) for l in ls]
        den = functools.reduce(jnp.add, es)
        b = functools.reduce(jnp.add, [e * o for e, o in zip(es, os_)]) / den
        lt = mx + jnp.log(den)
        bb = b.astype(BF16)
        b_ref[...] = bb
        lt_ref[...] = lt
        y_ref[...] = x_ref[...] + _dot_nn(a_ref[...], w_ref[0:A_Q_W, :]) + _dot_nn(bb, w_ref[A_Q_W:A_Q_W + B_W, :])
        _put_tile(tiles[0], b)
        _scatter_to_segments(tiles[0], b_segs)
        _put_tile(tiles[1], lt)
        _scatter_to_segments(tiles[1], lt_segs)

    tok = lambda w: pl.BlockSpec((tm, w), lambda i: (i, 0))
    segs = [_seg_spec(tm, dil, B_W) for dil in DILATIONS]
    res = pl.pallas_call(
        body, name=name, grid=(t // tm,),
        out_shape=[jax.ShapeDtypeStruct((t, d), F32), jax.ShapeDtypeStruct((t, B_W), BF16), jax.ShapeDtypeStruct((t, B_W), F32)]
        + [_seg_shape(t, dil, B_W, BF16) for dil in DILATIONS] + [_seg_shape(t, dil, B_W, F32) for dil in DILATIONS],
        in_specs=[tok(d), tok(A_Q_W)] + ([tok(B_W)] + segs) * 2 + [pl.BlockSpec(wout.shape, lambda i: (0, 0))],
        out_specs=[tok(d), tok(B_W), tok(B_W)] + segs * 2,
        scratch_shapes=_tile_scratch(tm, B_W) * (2 * n_dil),
        compiler_params=_cp(("arbitrary",)),
    )(x, a_out, *outs, *lses, wout)
    return res[0], [res[1]] + list(res[3:3 + n_dil]), [res[2]] + list(res[3 + n_dil:])


def _out_proj_bwd(dy, a_out, b_out, wout, name, tm):
    t, d = dy.shape
    n_t = t // tm
    n_dil = len(DILATIONS)

    def body(dy_ref, a_ref, b_ref, w_ref, da_ref, db_ref, gw_ref, *rest):
        db_segs, gw_s, tile = rest[:n_dil], rest[n_dil], rest[n_dil + 1:]
        i = pl.program_id(0)
        dyb = dy_ref[...].astype(BF16)
        da_ref[...] = _dot_nt(dyb, w_ref[0:A_Q_W, :]).astype(BF16)
        db = _dot_nt(dyb, w_ref[A_Q_W:A_Q_W + B_W, :])
        db_ref[...] = db.astype(BF16)
        _put_tile(tile, db)
        _scatter_to_segments(tile, db_segs)
        ga = _dot_tn(a_ref[...], dyb)
        gb = _dot_tn(b_ref[...], dyb)

        @pl.when(i == 0)
        def _():
            gw_s[0:A_Q_W, :] = ga
            gw_s[A_Q_W:A_Q_W + B_W, :] = gb

        @pl.when(i > 0)
        def _():
            gw_s[0:A_Q_W, :] += ga
            gw_s[A_Q_W:A_Q_W + B_W, :] += gb

        @pl.when(i == n_t - 1)
        def _():
            gw_ref[...] = gw_s[...].astype(BF16)

    tok = lambda w: pl.BlockSpec((tm, w), lambda i: (i, 0))
    whole = pl.BlockSpec(wout.shape, lambda i: (0, 0))
    res = pl.pallas_call(
        body, name=name, grid=(n_t,),
        out_shape=[jax.ShapeDtypeStruct((t, A_Q_W), BF16), jax.ShapeDtypeStruct((t, B_W), BF16),
                   jax.ShapeDtypeStruct(wout.shape, BF16)] + [_seg_shape(t, dil, B_W, BF16) for dil in DILATIONS],
        in_specs=[tok(d), tok(A_Q_W), tok(B_W), whole],
        out_specs=[tok(A_Q_W), tok(B_W), whole] + [_seg_spec(tm, dil, B_W) for dil in DILATIONS],
        scratch_shapes=[pltpu.VMEM(wout.shape, F32)] + _tile_scratch(tm, B_W),
        compiler_params=_cp(("arbitrary",)),
    )(dy, a_out, b_out, wout)
    return res[0], [res[1]] + list(res[3:]), res[2]


SUB_ROWS = 64
WINDOW_ALIGN = 64


def _sub_band(r0, hw, win, seg_len, t, rel, col):
    ks = pl.multiple_of(jnp.clip(r0 - hw, 0, t - win), WINDOW_ALIGN)
    kpos = col + ks
    seg_lo = (r0 // seg_len) * seg_len
    valid = (jnp.abs(rel + (ks - r0)) <= hw) & (kpos >= seg_lo) & (kpos < seg_lo + seg_len)
    return ks, valid


def _split_heads(v, low):
    zero = jnp.zeros_like(v)
    return jnp.concatenate([jnp.where(low, v, zero), jnp.where(low, zero, v)], axis=0)


def _kv_spec(kv, t):
    if kv.ndim == 3:
        return pl.BlockSpec((None, t, LANES), lambda p, i: (p // 2, 0, 0))
    return pl.BlockSpec((t, LANES), lambda p, i: (0, p))


def _attn_fwd(q, k, v, sink, name, hw, seg_len, tq, has_sink):
    t, width = q.shape
    sb = min(SUB_ROWS, tq)
    win = 2 * hw + LANES

    def body(sink_ref, q_ref, k_ref, v_ref, o_ref, lse_ref):
        p, i = pl.program_id(0), pl.program_id(1)
        q0 = i * tq
        low = lax.broadcasted_iota(jnp.int32, (1, LANES), 1) < HEAD_DIM
        rel = lax.broadcasted_iota(jnp.int32, (sb, win), 1) - lax.broadcasted_iota(jnp.int32, (sb, win), 0)
        col = lax.broadcasted_iota(jnp.int32, (1, win), 1)
        subs = []
        for j in range(tq // sb):
            rows = pl.ds(j * sb, sb)
            ks, valid = _sub_band(q0 + j * sb, hw, win, seg_len, t, rel, col)
            subs.append((rows, ks, valid, _dot_nt(_split_heads(q_ref[rows, :], low), k_ref[pl.ds(ks, win), :])))
        for rows, ks, valid, s in subs:
            vw = v_ref[pl.ds(ks, win), :]
            es, inv, lses = [], [], []
            for a in range(2):
                sa = jnp.where(valid, s[a * sb:(a + 1) * sb], NEG)
                m = jnp.max(sa, axis=1, keepdims=True)
                if has_sink:
                    sk = sink_ref[2 * p + a]
                    m = jnp.maximum(m, sk)
                e = jnp.exp(sa - m)
                den = jnp.sum(e, axis=1, keepdims=True)
                if has_sink:
                    den = den + jnp.exp(sk - m)
                es.append(e.astype(BF16))
                inv.append(1.0 / den)
                lses.append(m + jnp.log(den))
            pv = _dot_nn(jnp.concatenate(es, axis=0), vw)
            o_ref[rows, :] = jnp.where(low, pv[0:sb] * inv[0], pv[sb:2 * sb] * inv[1]).astype(BF16)
            lse_ref[rows, :] = jnp.where(low, lses[0], lses[1])

    tile = pl.BlockSpec((tq, LANES), lambda p, i: (i, p))
    return pl.pallas_call(
        body, name=name, grid=(width // LANES, t // tq),
        out_shape=[jax.ShapeDtypeStruct((t, width), BF16), jax.ShapeDtypeStruct((t, width), F32)],
        in_specs=[pl.BlockSpec(memory_space=pltpu.SMEM), tile, _kv_spec(k, t), _kv_spec(v, t)],
        out_specs=[tile, tile],
        compiler_params=_cp(("arbitrary", "arbitrary")),
    )(sink, q, k, v)


def _attn_bwd(q, k, v, o, do, lse, sink, name, hw, seg_len, tq, has_sink, comm=None, comm_at=None):
    t, width = q.shape
    sb = min(SUB_ROWS, tq)
    win = 2 * hw + LANES
    n_q = t // tq
    shared_kv = k.ndim == 3
    n_ci, n_co, c_in_specs, c_out_specs, c_shapes, c_scratch, c_inputs = _comm_parts(comm)

    def body(*refs):
        ((sink_ref, q_ref, k_ref, v_ref, o_ref, do_ref, lse_ref), c_in, (dq_ref, dk_ref, dv_ref, ds_ref), c_out,
         (dk_s, dv_s), c_scr) = _split(refs, 7, n_ci, 4, n_co, 2)
        p, i = pl.program_id(0), pl.program_id(1)
        if comm:
            _run_hosted(comm, comm_at, p * n_q + i, c_in, c_out, c_scr)
        fresh = (i == 0) & (p % 2 == 0) if shared_kv else i == 0
        last = (i == n_q - 1) & (p % 2 == 1) if shared_kv else i == n_q - 1

        @pl.when(fresh)
        def _():
            dk_s[...] = jnp.zeros_like(dk_s)
            dv_s[...] = jnp.zeros_like(dv_s)

        q0 = i * tq
        low = lax.broadcasted_iota(jnp.int32, (1, LANES), 1) < HEAD_DIM
        rel = lax.broadcasted_iota(jnp.int32, (sb, win), 1) - lax.broadcasted_iota(jnp.int32, (sb, win), 0)
        col = lax.broadcasted_iota(jnp.int32, (1, win), 1)
        dsink = [jnp.zeros((1, 1), F32), jnp.zeros((1, 1), F32)]
        subs = []
        for j in range(tq // sb):
            rows = pl.ds(j * sb, sb)
            ks, valid = _sub_band(q0 + j * sb, hw, win, seg_len, t, rel, col)
            kw = k_ref[pl.ds(ks, win), :]
            dov = do_ref[rows, :]
            q2 = _split_heads(q_ref[rows, :], low)
            do2 = _split_heads(dov, low)
            subs.append((rows, ks, valid, kw, dov, q2, do2, _dot_nt(q2, kw), _dot_nt(do2, v_ref[pl.ds(ks, win), :])))
        probs = []
        for rows, ks, valid, kw, dov, q2, do2, s, dpr in subs:
            prod = dov.astype(F32) * o_ref[rows, :].astype(F32)
            lse_t = lse_ref[rows, :]
            prs, dss = [], []
            for a in range(2):
                mine = low if a == 0 else jnp.logical_not(low)
                lse_a = jnp.max(jnp.where(mine, lse_t, -jnp.inf), axis=1, keepdims=True)
                delta = jnp.sum(jnp.where(mine, prod, 0.0), axis=1, keepdims=True)
                pr = jnp.exp(jnp.where(valid, s[a * sb:(a + 1) * sb], NEG) - lse_a)
                prs.append(pr.astype(BF16))
                dss.append((pr * (dpr[a * sb:(a + 1) * sb] - delta)).astype(BF16))
                if has_sink:
                    dsink[a] = dsink[a] - jnp.sum(jnp.exp(sink_ref[2 * p + a] - lse_a) * delta, axis=0, keepdims=True)
            probs.append((jnp.concatenate(prs, axis=0), jnp.concatenate(dss, axis=0)))
        for (rows, ks, valid, kw, dov, q2, do2, s, dpr), (pr2, ds2) in zip(subs, probs):
            dv_s[pl.ds(ks, win), :] += _dot_tn(pr2, do2)
            dk_s[pl.ds(ks, win), :] += _dot_tn(ds2, q2)
            dq2 = _dot_nn(ds2, kw)
            dq_ref[rows, :] = jnp.where(low, dq2[0:sb], dq2[sb:2 * sb]).astype(BF16)
        ds_ref[...] = jnp.broadcast_to(jnp.where(low, dsink[0], dsink[1]), ds_ref.shape)

        @pl.when(last)
        def _():
            dk_ref[...] = dk_s[...].astype(BF16)
            dv_ref[...] = dv_s[...].astype(BF16)

    tile = pl.BlockSpec((tq, LANES), lambda p, i: (i, p))
    kv_shape = jax.ShapeDtypeStruct(k.shape, BF16)
    res = pl.pallas_call(
        body, name=name, grid=(width // LANES, n_q),
        out_shape=[jax.ShapeDtypeStruct((t, width), BF16), kv_shape, kv_shape,
                   jax.ShapeDtypeStruct((width // LANES, n_q, 8, LANES), F32)] + c_shapes,
        in_specs=[pl.BlockSpec(memory_space=pltpu.SMEM), tile, _kv_spec(k, t), _kv_spec(v, t), tile, tile, tile] + c_in_specs,
        out_specs=[tile, _kv_spec(k, t), _kv_spec(v, t), pl.BlockSpec((None, None, 8, LANES), lambda p, i: (p, i, 0, 0))] + c_out_specs,
        scratch_shapes=[pltpu.VMEM((t, LANES), F32)] * 2 + c_scratch,
        compiler_params=_cp(("arbitrary", "arbitrary")),
    )(sink, q, k, v, o, do, lse, *c_inputs)
    return res[:4], res[4:]


def _adamw(w, g, m, v, name):
    def body(w_ref, g_ref, m_ref, v_ref, d_ref, nm_ref, nv_ref):
        gv = g_ref[...]
        nm = ADAM_B1 * m_ref[...] + (1.0 - ADAM_B1) * gv
        nv = ADAM_B2 * v_ref[...] + (1.0 - ADAM_B2) * (gv * gv)
        m_hat = nm / (1.0 - ADAM_B1 ** ADAM_STEP)
        v_hat = nv / (1.0 - ADAM_B2 ** ADAM_STEP)
        d_ref[...] = -ADAM_LR * (m_hat / (jnp.sqrt(v_hat) + ADAM_EPS) + ADAM_WD * w_ref[...])
        nm_ref[...] = nm
        nv_ref[...] = nv

    shape = jax.ShapeDtypeStruct(w.shape, F32)
    return pl.pallas_call(body, name=name, out_shape=[shape, shape, shape], compiler_params=_cp())(w, g, m, v)


def _adamw_rows(land, j, rows, w, m, v, name):
    d = w.shape[1]

    def body(l_ref, w_ref, m_ref, v_ref, g_ref, d_ref, nm_ref, nv_ref):
        gv = l_ref[0].astype(F32)
        for s in range(1, N_DEV):
            gv = gv + l_ref[s].astype(F32)
        g_ref[...] = gv
        nm = ADAM_B1 * m_ref[...] + (1.0 - ADAM_B1) * gv
        nv = ADAM_B2 * v_ref[...] + (1.0 - ADAM_B2) * (gv * gv)
        m_hat = nm / (1.0 - ADAM_B1 ** ADAM_STEP)
        v_hat = nv / (1.0 - ADAM_B2 ** ADAM_STEP)
        d_ref[...] = -ADAM_LR * (m_hat / (jnp.sqrt(v_hat) + ADAM_EPS) + ADAM_WD * w_ref[...])
        nm_ref[...] = nm
        nv_ref[...] = nv

    halves = 2
    half = pl.BlockSpec((rows // halves, d), lambda i: (i, 0))
    shape = jax.ShapeDtypeStruct((rows, d), F32)
    return pl.pallas_call(
        body, name=name, grid=(halves,), out_shape=[shape] * 4,
        in_specs=[pl.BlockSpec((N_DEV, rows // halves, d), lambda i: (0, halves * j + i, 0)), half, half, half],
        out_specs=[half] * 4,
        compiler_params=_cp(("arbitrary",)),
    )(land, w, m, v)


def _sum_small(land, name):
    def body(l_ref, o_ref):
        acc = l_ref[0]
        for s in range(1, N_DEV):
            acc = acc + l_ref[s]
        o_ref[...] = acc

    return pl.pallas_call(body, name=name, out_shape=jax.ShapeDtypeStruct(land.shape[1:], F32), compiler_params=_cp())(land)


def _rope_lanes(positions):
    inv_freq = 1.0 / (ROPE_THETA ** (jnp.arange(0, HEAD_DIM, 2, dtype=F32) / HEAD_DIM))
    ang = positions.astype(F32)[:, None] * inv_freq
    cos, sin = jnp.cos(ang), jnp.sin(ang)
    return jnp.concatenate([cos, cos, cos, cos], axis=1), jnp.concatenate([-sin, sin, -sin, sin], axis=1)


def kernel(x, positions, norm_ffn1, w_gate1, w_up1, w_down1, norm_mix, w_in, a_sink, w_out, norm_ffn2, w_gate2, w_up2, w_down2, norm_final, loss_target, m_norm_ffn1, m_w_gate1, m_w_up1, m_w_down1, m_norm_mix, m_w_in, m_a_sink, m_w_out, m_norm_ffn2, m_w_gate2, m_w_up2, m_w_down2, m_norm_final, v_norm_ffn1, v_w_gate1, v_w_up1, v_w_down1, v_norm_mix, v_w_in, v_a_sink, v_w_out, v_norm_ffn2, v_w_gate2, v_w_up2, v_w_down2, v_norm_final):
    x = x[0]
    target = loss_target[0]
    t, d = x.shape
    tm_mix = min(512, t)
    tq_of = lambda seg_len, most=2048: min(most, t, max(seg_len, 16 * SUB_ROWS))

    tt = min(1024, t)
    f_all = w_gate1.shape[2] * N_DEV
    tfp = f_all // 2
    tm_fwd, tf_fwd = tm_mix, tfp
    n_steps = (t // tm_fwd) * (f_all // tf_fwd)

    def stacked(shards):
        return jnp.concatenate([s.astype(BF16) for s in shards], axis=0), [s.shape[0] for s in shards]

    packed1, rows1 = stacked([w_gate1[0].T, w_up1[0].T, w_down1[0]])
    later = [w_in[0].T, w_out[0], w_gate2[0].T, w_up2[0].T, w_down2[0]]
    rows2 = [s.shape[0] for s in later]
    (wg1, wu1, wd1), packed2 = _run_alone_stacking(_gather_plan(packed1, rows1, stages=len(rows1)), later, "gather_ffn1_weights")

    cos, sin_signed = _rope_lanes(positions[0])
    sink = a_sink[0]
    no_sink = jnp.zeros_like(sink)

    (x1, h1, gp1, up1, hid1), (win, wout, wg2, wu2, wd2) = _ffn_fwd(
        x, norm_ffn1, wg1, wu1, wd1, "ffn1_fwd", tm_fwd, tf_fwd, _gather_plan(packed2, rows2), [0, (3 * n_steps) // 4, n_steps - 1])
    (h_mix, aq, akx, avx, bq, bk, bv), seg_qkv = _in_proj_fwd(x1, norm_mix, win, cos, sin_signed, "in_proj_fwd", tm_mix)
    a_out, a_lse = _attn_fwd(aq, akx, avx, sink, "attn_a_fwd", A_HALF_WINDOW, t, tq_of(t), True)
    rows_of = lambda a: a.reshape(t, B_W)
    segments_of = lambda a, dil: a if dil == 1 else a.reshape(dil, t // dil, B_W)
    b_qkv, b_outs, b_lses = [], [], []
    for n, (window, dil) in enumerate(B_PATTERNS):
        qs, ks, vs = (bq, bk, bv) if dil == 1 else (rows_of(seg_qkv[k][n - 1]) for k in range(3))
        o_seg, lse_seg = _attn_fwd(qs, ks, vs, no_sink, f"attn_b{dil}_fwd", window // (2 * dil), t // dil,
                                  tq_of(t // dil, 16 * SUB_ROWS), False)
        b_qkv.append((qs, ks, vs))
        b_outs.append(segments_of(o_seg, dil))
        b_lses.append(segments_of(lse_seg, dil))
    x2, b_out, b_lse = _merge_out_proj_fwd(x1, a_out, b_outs, b_lses, wout, "out_proj_fwd", tm_mix)
    (dx3, dg_final, sq, dout2, h2, gp2, up2, hid2), _ = _ffn_fwd(
        x2, norm_ffn2, wg2, wu2, wd2, "ffn2_fwd_loss", tm_fwd, tf_fwd, loss_head=(norm_final.reshape(1, d), target))


    (dgt2, dut2, gwd2), _ = _ffn_bwd_hidden(dout2, gp2, up2, hid2, wd2, "ffn2_bwd_hidden", tm_mix, tfp)
    (dx2, dg_ffn2), _ = _ffn_bwd_input(dx3, x2, norm_ffn2, dgt2, dut2, wg2, wu2, "ffn2_bwd_input", tm_mix)
    (gwg2, gwu2), _ = _token_products([dgt2, dut2], h2, "ffn2_bwd_gate_up", tt, tfp)

    n_att = lambda width, tq: (width // LANES) * (t // tq)
    n_prod = (f_all // tfp) * (t // tt)
    da, db, gwout = _out_proj_bwd(dx2, a_out, b_out[0], wout, "out_proj_bwd", tm_mix)
    (daq, dakx, davx, dsink_parts), (land_wg2,) = _attn_bwd(
        aq, akx, avx, a_out, da, a_lse, sink, "attn_a_bwd", A_HALF_WINDOW, t, tq_of(t), True,
        _exchange_plan([[gwg2]]), [0, n_att(A_Q_W, tq_of(t)) - 1])
    pattern_grads, pattern_lands = [], []
    for n, ((window, dil), (qs, ks, vs)) in enumerate(zip(B_PATTERNS, b_qkv)):
        carried = [gwu2, gwd2][n:n + 1]
        grads, lands = _attn_bwd(qs, ks, vs, rows_of(b_out[n]), rows_of(db[n]), rows_of(b_lse[n]), no_sink, f"attn_b{dil}_bwd",
                                 window // (2 * dil), t // dil, tq_of(t // dil), False,
                                 _exchange_plan([carried]) if carried else None, [0, n_att(B_W, tq_of(t // dil)) - 1])
        pattern_grads.append(grads[:3])
        pattern_lands.extend(lands)
    land_wu2, land_wd2 = pattern_lands
    dx1, dg_mix, gwin, dout1 = _in_proj_bwd(dx2, x1, norm_mix, win, h_mix, cos, sin_signed, daq, dakx, davx, pattern_grads, "in_proj_bwd", tm_mix)

    (dgt1, dut1, gwd1), (land_in, land_out) = _ffn_bwd_hidden(
        dout1, gp1, up1, hid1, wd1, "ffn1_bwd_hidden", tm_mix, tfp, _exchange_plan([[gwin], [gwout]]),
        [0, (f_all // tfp) * (t // tm_mix) - 1])
    (gwg1,), (land_wd1,) = _token_products([dgt1], h1, "ffn1_bwd_gate", tt, tfp, _exchange_plan([[gwd1]]), [0, n_prod - 1])
    (gwu1,), (land_wg1,) = _token_products([dut1], h1, "ffn1_bwd_up", tt, tfp, _exchange_plan([[gwg1]]), [0, n_prod - 1])
    (grad_x, dg_ffn1), (land_wu1,) = _ffn_bwd_input(
        dx1, x, norm_ffn1, dgt1, dut1, wg1, wu1, "ffn1_bwd_input", tm_mix, _exchange_plan([[gwu1]]), [0, t // tm_mix - 1])

    dsink_pairs = jnp.sum(dsink_parts[:, :, 0, :], axis=1)
    dsink = jnp.stack([dsink_pairs[:, 0], dsink_pairs[:, HEAD_DIM]], axis=1).reshape(1, -1)
    small = jnp.concatenate([dg_ffn1, dg_mix, dg_ffn2, dg_final, jnp.pad(dsink, ((0, 0), (0, d - dsink.shape[1]))),
                             sq, jnp.zeros((2, d), F32)], axis=0)
    (land_small,) = _run_alone(_exchange_plan([[jnp.tile(small, (N_DEV, 1))]]), "gather_small_gradients")
    red_small = _sum_small(land_small, "sum_small_grads")
    loss = 0.5 * jnp.sum(red_small[5]) / d

    rf = rows1[0]
    sharded = {"w_gate1": (land_wg1, 0, rf, True), "w_up1": (land_wu1, 0, rf, True), "w_down1": (land_wd1, 0, rf, False),
               "w_in": (land_in, 0, rows2[0], True), "w_out": (land_out, 0, rows2[1], False),
               "w_gate2": (land_wg2, 0, rf, True), "w_up2": (land_wu2, 0, rf, True), "w_down2": (land_wd2, 0, rf, False)}
    n_sink = a_sink.shape[1]
    small_grads = {"norm_ffn1": red_small[0:1], "norm_mix": red_small[1:2], "norm_ffn2": red_small[2:3], "norm_final": red_small[3],
                   "a_sink": red_small[4:5, :n_sink]}
    params = {
        "norm_ffn1": (norm_ffn1, m_norm_ffn1, v_norm_ffn1), "w_gate1": (w_gate1, m_w_gate1, v_w_gate1),
        "w_up1": (w_up1, m_w_up1, v_w_up1), "w_down1": (w_down1, m_w_down1, v_w_down1),
        "norm_mix": (norm_mix, m_norm_mix, v_norm_mix), "w_in": (w_in, m_w_in, v_w_in),
        "a_sink": (a_sink, m_a_sink, v_a_sink), "w_out": (w_out, m_w_out, v_w_out),
        "norm_ffn2": (norm_ffn2, m_norm_ffn2, v_norm_ffn2), "w_gate2": (w_gate2, m_w_gate2, v_w_gate2),
        "w_up2": (w_up2, m_w_up2, v_w_up2), "w_down2": (w_down2, m_w_down2, v_w_down2),
        "norm_final": (norm_final, m_norm_final, v_norm_final),
    }
    grad_list, deltas, new_ms, new_vs = [], [], [], []
    for name, (w, m, v) in params.items():
        if name in sharded:
            land, j, rows, is_transposed = sharded[name]
            view = (lambda a: a[0].T) if is_transposed else (lambda a: a[0])
            back = (lambda a: a.T[None]) if is_transposed else (lambda a: a[None])
            outs = [back(o) for o in _adamw_rows(land, j, rows, view(w), view(m), view(v), f"adamw_{name}")]
        else:
            as_block = (lambda a: a.reshape(1, -1)) if w.ndim == 1 else (lambda a: a)
            g = small_grads[name]
            outs = [g] + [o.reshape(w.shape) for o in _adamw(as_block(w), as_block(g), as_block(m), as_block(v), f"adamw_{name}")]
        for lst, o in zip((grad_list, deltas, new_ms, new_vs), outs):
            lst.append(o)
    return (loss, grad_x[None], *grad_list, *deltas, *new_ms, *new_vs)
```

```python
import functools
import itertools

import numpy as np
import jax
import jax.numpy as jnp
from jax import lax
from jax.experimental import pallas as pl
from jax.experimental.pallas import tpu as pltpu

F32 = jnp.float32
BF16 = jnp.bfloat16

N_DEV = 8
HEAD_DIM = 64
LANES = 128
A_Q_W, A_KV_W, B_W = 512, 128, 512
A_HALF_WINDOW = 128
B_PATTERNS = ((128, 1), (512, 4), (2048, 16))
ROPE_THETA = 10000.0
NORM_EPS = 1e-6
FFN_RES_WEIGHT = 0.5
QK_SCALE = HEAD_DIM ** -0.5
NEG = -1e30

ADAM_LR = 0.001
ADAM_B1 = 0.9
ADAM_B2 = 0.999
ADAM_EPS = 1e-08
ADAM_WD = 0.01
ADAM_STEP = 10

MESH_T = pl.DeviceIdType.MESH
VMEM_LIMIT = 60 * 1024 * 1024


def _cp(sem=None, vmem=VMEM_LIMIT):
    return pltpu.CompilerParams(dimension_semantics=sem, vmem_limit_bytes=vmem)


def _dot_nn(a, b):
    return jnp.dot(a, b, preferred_element_type=F32)


def _dot_nt(a, b):
    return lax.dot_general(a, b, (((1,), (1,)), ((), ())), preferred_element_type=F32)


def _dot_tn(a, b):
    return lax.dot_general(a, b, (((0,), (0,)), ((), ())), preferred_element_type=F32)


def _rstd(xv):
    return lax.rsqrt(jnp.mean(xv * xv, axis=-1, keepdims=True) + NORM_EPS)


def _norm_bwd(dh, xv, r, gn):
    gy = dh * gn
    c = jnp.sum(gy * xv, axis=-1, keepdims=True) * (1.0 / xv.shape[-1])
    dx = r * gy - xv * (r * r * r * c)
    dgn = jnp.sum(dh * (xv * r), axis=0, keepdims=True)
    return dx, dgn


def _accumulate(ref, val, first):
    @pl.when(first)
    def _():
        ref[...] = val

    @pl.when(jnp.logical_not(first))
    def _():
        ref[...] += val


def _mesh_pos():
    return lax.axis_index("x"), lax.axis_index("y"), lax.axis_index("c")


def _dev_index(d):
    return 4 * d[0] + 2 * d[1] + d[2]


class _Comm:
    def __init__(self, inputs, out_shape, scratch, phases):
        self.inputs, self.out_shape, self.scratch, self.phases = inputs, out_shape, scratch, phases

    def specs(self):
        any_spec = pl.BlockSpec(memory_space=pl.ANY)
        return [any_spec] * len(self.inputs), [any_spec] * len(self.out_shape)


def _run_alone(comm, name):
    n_in, n_out = len(comm.inputs), len(comm.out_shape)

    def body(*refs):
        for phase in comm.phases:
            phase(refs[:n_in], refs[n_in:n_in + n_out], refs[n_in + n_out:])

    in_specs, out_specs = comm.specs()
    return pl.pallas_call(body, name=name, out_shape=comm.out_shape, in_specs=in_specs, out_specs=out_specs,
                          scratch_shapes=comm.scratch)(*comm.inputs)


def _run_alone_busy(comm, shards, positions, name):
    n_in, n_out = len(comm.inputs), len(comm.out_shape)
    offs = [int(o) for o in np.cumsum([0] + [s.shape[0] for s in shards])]
    t = positions.shape[0]
    half = HEAD_DIM // 2
    inv_freq = 1.0 / (ROPE_THETA ** (jnp.arange(0, HEAD_DIM, 2, dtype=F32) / HEAD_DIM))
    lanes = jnp.stack([jnp.tile(inv_freq, LANES // half),
                       jnp.tile(jnp.concatenate([-jnp.ones(half, F32), jnp.ones(half, F32)]), LANES // HEAD_DIM)])

    def body(*refs):
        c_in, s_refs, (pos_ref, lanes_ref), c_out, (stack_ref, cos_ref, sin_ref), scr = _split(refs, n_in, len(shards), 2, n_out, 3)
        comm.phases[0](c_in, c_out, scr)
        for s_ref, lo, hi in zip(s_refs, offs[:-1], offs[1:]):
            stack_ref[lo:hi, :] = s_ref[...].astype(BF16)
        ang = pos_ref[...] * lanes_ref[0:1, :]
        cos_ref[...] = jnp.cos(ang)
        sin_ref[...] = jnp.sin(ang) * lanes_ref[1:2, :]
        for phase in comm.phases[1:]:
            phase(c_in, c_out, scr)

    in_specs, out_specs = comm.specs()
    vmem = pl.BlockSpec(memory_space=pltpu.VMEM)
    table = jax.ShapeDtypeStruct((t, LANES), F32)
    res = pl.pallas_call(
        body, name=name, out_shape=comm.out_shape + [jax.ShapeDtypeStruct((offs[-1], shards[0].shape[1]), BF16), table, table],
        in_specs=in_specs + [vmem] * (len(shards) + 2), out_specs=out_specs + [vmem] * 3, scratch_shapes=comm.scratch,
        compiler_params=_cp())(*comm.inputs, *shards, positions, lanes)
    return res[:n_out], res[n_out], res[n_out + 1], res[n_out + 2]


def _run_hosted(comm, at, step, ins, outs, scr):
    for phase, when in zip(comm.phases, at):
        @pl.when(step == when)
        def _(phase=phase):
            phase(ins, outs, scr)


def _split(refs, *counts):
    parts, o = [], 0
    for n in counts:
        parts.append(refs[o:o + n])
        o += n
    return parts + [refs[o:]]


def _gather_plan(packed, rows_list, stages=1):
    n_w = len(rows_list)
    offs = [int(o) for o in np.cumsum([0] + list(rows_list))]
    d = packed.shape[1]
    per = n_w // stages
    groups = [range(s * per, (s + 1) * per) for s in range(stages)]

    def tools(ins, outs, scr):
        p_ref = ins[0]
        send_sems, recv_sems, local_sem = scr
        x, y, c = _mesh_pos()
        me, sibling = (x, y, c), (x, y, 1 - c)
        chips = [(1 - x, y), (x, 1 - y), (1 - x, 1 - y)]

        def rows(w, dev):
            start = pl.multiple_of(_dev_index(dev) * rows_list[w], 16)
            return outs[w].at[pl.ds(start, rows_list[w]), :]

        def mine(w):
            return p_ref.at[pl.ds(offs[w], rows_list[w]), :]

        def copy(k, s, w, block, to, own):
            return pltpu.make_async_remote_copy(
                src_ref=mine(w) if own else rows(w, block), dst_ref=rows(w, block),
                send_sem=send_sems.at[k, s], recv_sem=recv_sems.at[k, s], device_id=to, device_id_type=MESH_T)

        def whole_group(k, s):
            span = p_ref.at[pl.ds(offs[groups[s][0]], offs[groups[s][-1] + 1] - offs[groups[s][0]]), :]
            return pltpu.make_async_remote_copy(
                src_ref=span, dst_ref=span, send_sem=send_sems.at[k, s], recv_sem=recv_sems.at[k, s],
                device_id=me, device_id_type=MESH_T)

        return p_ref, local_sem, me, sibling, chips, c, rows, mine, copy, whole_group

    def start(ins, outs, scr):
        _, local_sem, me, sibling, chips, c, rows, mine, copy, _ = tools(ins, outs, scr)
        for w in range(n_w):
            pltpu.make_async_copy(mine(w), rows(w, me), local_sem).start()
        for s, group in enumerate(groups):
            for w in group:
                copy(0, s, w, me, sibling, True).start()
            for j, chip in enumerate(chips):
                for w in group:
                    copy(1 + j, s, w, me, (*chip, c), True).start()

    def relay(ins, outs, scr):
        _, _, _, sibling, chips, c, _, _, copy, whole_group = tools(ins, outs, scr)
        for s, group in enumerate(groups):
            for j, chip in enumerate(chips):
                whole_group(1 + j, s).wait_recv()
                for w in group:
                    copy(4 + j, s, w, (*chip, c), sibling, False).start()

    def finish(ins, outs, scr):
        p_ref, local_sem, _, _, _, _, _, _, _, whole_group = tools(ins, outs, scr)
        for s in range(stages):
            whole_group(0, s).wait_recv()
            for j in range(3):
                whole_group(4 + j, s).wait_recv()
            for k in range(7):
                whole_group(k, s).wait_send()
        pltpu.make_async_copy(p_ref, p_ref, local_sem).wait()

    return _Comm(
        [packed], [jax.ShapeDtypeStruct((N_DEV * r, d), packed.dtype) for r in rows_list],
        [pltpu.SemaphoreType.DMA((7, stages)), pltpu.SemaphoreType.DMA((7, stages)), pltpu.SemaphoreType.DMA], [start, relay, finish])


def _exchange_plan(groups):
    flat = [a for g in groups for a in g]
    n_g = len(groups)
    sizes = [len(g) for g in groups]
    rows = [g[0].shape[0] // N_DEV for g in groups]
    first = [int(o) for o in np.cumsum([0] + sizes[:-1])]

    def start(srcs, lands, scr):
        send_sems, recv_sems, local_sems = scr
        x, y, c = _mesh_pos()
        me = (x, y, c)
        me_idx = _dev_index(me)

        def block(g, i, dev):
            start_row = pl.multiple_of(_dev_index(dev) * rows[g], 8)
            return srcs[first[g] + i].at[pl.ds(start_row, rows[g]), :]

        def slot(g, i):
            return lands[g].at[me_idx, pl.ds(i * rows[g], rows[g]), :]

        for g in range(n_g):
            for i in range(sizes[g]):
                pltpu.make_async_copy(block(g, i, me), slot(g, i), local_sems.at[g]).start()
        flips = [f for f in itertools.product((0, 1), repeat=3) if any(f)]
        for k, (fx, fy, fc) in enumerate(flips):
            peer = (1 - x if fx else x, 1 - y if fy else y, 1 - c if fc else c)
            for g in range(n_g):
                for i in range(sizes[g]):
                    pltpu.make_async_remote_copy(
                        src_ref=block(g, i, peer), dst_ref=slot(g, i), send_sem=send_sems.at[g, k],
                        recv_sem=recv_sems.at[g, k], device_id=peer, device_id_type=MESH_T).start()

    def finish(srcs, lands, scr):
        send_sems, recv_sems, local_sems = scr
        me = _mesh_pos()
        for k in range(7):
            for g in range(n_g):
                pltpu.make_async_remote_copy(
                    src_ref=lands[g].at[0], dst_ref=lands[g].at[0], send_sem=send_sems.at[g, k],
                    recv_sem=recv_sems.at[g, k], device_id=me, device_id_type=MESH_T).wait()
        for g in range(n_g):
            pltpu.make_async_copy(lands[g].at[0], lands[g].at[0], local_sems.at[g]).wait()

    return _Comm(
        flat, [jax.ShapeDtypeStruct((N_DEV, sizes[g] * rows[g], groups[g][0].shape[1]), groups[g][0].dtype) for g in range(n_g)],
        [pltpu.SemaphoreType.DMA((n_g, 7)), pltpu.SemaphoreType.DMA((n_g, 7)), pltpu.SemaphoreType.DMA((n_g,))],
        [start, finish])


def _comm_parts(comm):
    if comm is None:
        return 0, 0, [], [], [], [], []
    in_specs, out_specs = comm.specs()
    return len(comm.inputs), len(comm.out_shape), in_specs, out_specs, comm.out_shape, comm.scratch, comm.inputs


def _ffn_fwd(x, gn, wg_t, wu_t, wd, name, tm, tf, comm=None, comm_at=None, loss_head=None):
    t, d = x.shape
    f_all = wg_t.shape[0]
    n_f = f_all // tf
    n_ci, n_co, c_in_specs, c_out_specs, c_shapes, c_scratch, c_inputs = _comm_parts(comm)
    n_head, n_y = (2, 4) if loss_head else (0, 1)

    def body(*refs):
        ((x_ref, gn_ref, wg_ref, wu_ref, wd_ref), head_in, c_in, y_outs, (h_ref, gp_ref, up_ref, hid_ref), c_out,
         (h_s, hid_s), c_scr) = _split(refs, 5, n_head, n_ci, n_y, 4, n_co, 2)
        f = pl.program_id(1)
        if comm:
            _run_hosted(comm, comm_at, pl.program_id(0) * n_f + f, c_in, c_out, c_scr)

        @pl.when(f == 0)
        def _():
            xv = x_ref[...]
            h = ((xv * _rstd(xv)) * gn_ref[...]).astype(BF16)
            h_s[...] = h
            h_ref[...] = h

        h = h_s[...]
        cols = pl.ds(pl.multiple_of(f * tf, tf), tf)
        g = _dot_nt(h, wg_ref[cols, :])
        u = _dot_nt(h, wu_ref[cols, :])
        sg = jax.nn.sigmoid(g)
        silu = g * sg
        gp_ref[...] = (u * (sg * (1.0 + g * (1.0 - sg)))).astype(BF16)
        up_ref[...] = silu.astype(BF16)
        hid = (silu * u).astype(BF16)
        hid_ref[...] = hid
        for f0 in range(n_f):
            @pl.when(f == f0)
            def _(f0=f0):
                hid_s[:, f0 * tf:(f0 + 1) * tf] = hid

        @pl.when(f == n_f - 1)
        def _():
            y = x_ref[...] + FFN_RES_WEIGHT * _dot_nn(hid_s[...], wd_ref[...])
            if not loss_head:
                y_outs[0][...] = y
            else:
                (gf_ref, tg_ref), (dy_ref, dgf_ref, sq_ref, half_ref) = head_in, y_outs
                gfv, r, first = gf_ref[...], _rstd(y), pl.program_id(0) == 0
                diff = (y * r) * gfv - tg_ref[...]
                dy, dgf = _norm_bwd(diff * (1.0 / d), y, r, gfv)
                dy_ref[...] = dy
                half_ref[...] = (FFN_RES_WEIGHT * dy).astype(BF16)
                _accumulate(dgf_ref, dgf, first)
                _accumulate(sq_ref, jnp.sum(diff * diff, axis=0, keepdims=True), first)

    tok = pl.BlockSpec((tm, d), lambda i, f: (i, 0))
    row = pl.BlockSpec((1, d), lambda i, f: (0, 0))
    whole = pl.BlockSpec((f_all, d), lambda i, f: (0, 0), pipeline_mode=pl.Buffered(1))
    act = pl.BlockSpec((tm, tf), lambda i, f: (i, f))
    act_shape = jax.ShapeDtypeStruct((t, f_all), BF16)
    tok_f32, tok_bf16, row_f32 = (jax.ShapeDtypeStruct((t, d), F32), jax.ShapeDtypeStruct((t, d), BF16),
                                  jax.ShapeDtypeStruct((1, d), F32))
    y_shapes, y_specs = ([tok_f32, row_f32, row_f32, tok_bf16], [tok, row, row, tok]) if loss_head else ([tok_f32], [tok])
    res = pl.pallas_call(
        body, name=name, grid=(t // tm, n_f),
        out_shape=y_shapes + [tok_bf16, act_shape, act_shape, act_shape] + c_shapes,
        in_specs=[tok, row, whole, whole, whole] + ([row, tok] if loss_head else []) + c_in_specs,
        out_specs=y_specs + [tok, act, act, act] + c_out_specs,
        scratch_shapes=[pltpu.VMEM((tm, d), BF16), pltpu.VMEM((tm, f_all), BF16)] + c_scratch,
        compiler_params=_cp(("arbitrary", "arbitrary")),
    )(x, gn, wg_t, wu_t, wd, *(loss_head or ()), *c_inputs)
    return res[:n_y + 4], res[n_y + 4:]


def _ffn_bwd_hidden(dout, gp, up, hid, wd, name, tt, tf, comm=None, comm_at=None):
    t, d = dout.shape
    f_all = wd.shape[0]
    n_t = t // tt
    n_ci, n_co, c_in_specs, c_out_specs, c_shapes, c_scratch, c_inputs = _comm_parts(comm)

    def body(*refs):
        (dout_ref, gp_ref, up_ref, hid_ref, wd_ref), c_in, (dg_ref, du_ref, gwd_ref), c_out, (acc,), c_scr = _split(
            refs, 5, n_ci, 3, n_co, 1)
        s = pl.program_id(1)
        if comm:
            _run_hosted(comm, comm_at, pl.program_id(0) * n_t + s, c_in, c_out, c_scr)
        doutv = dout_ref[...]
        dhid = _dot_nt(doutv, wd_ref[...])
        dg_ref[...] = (dhid * gp_ref[...].astype(F32)).astype(BF16)
        du_ref[...] = (dhid * up_ref[...].astype(F32)).astype(BF16)
        _accumulate(acc, _dot_tn(hid_ref[...], doutv), s == 0)

        @pl.when(s == n_t - 1)
        def _():
            gwd_ref[...] = acc[...].astype(BF16)

    tok = pl.BlockSpec((tt, d), lambda f, s: (s, 0))
    wblk = pl.BlockSpec((tf, d), lambda f, s: (f, 0))
    act = pl.BlockSpec((tt, tf), lambda f, s: (s, f))
    act_shape = jax.ShapeDtypeStruct((t, f_all), BF16)
    res = pl.pallas_call(
        body, name=name, grid=(f_all // tf, n_t),
        out_shape=[act_shape, act_shape, jax.ShapeDtypeStruct((f_all, d), BF16)] + c_shapes,
        in_specs=[tok, act, act, act, wblk] + c_in_specs, out_specs=[act, act, wblk] + c_out_specs,
        scratch_shapes=[pltpu.VMEM((tf, d), F32)] + c_scratch,
        compiler_params=_cp(("arbitrary", "arbitrary")),
    )(dout, gp, up, hid, wd, *c_inputs)
    return res[:3], res[3:]


def _ffn_bwd_input(dy, x, gn, dg, du, wg_t, wu_t, name, tm, comm=None, comm_at=None):
    t, d = x.shape
    f_all = wg_t.shape[0]
    n_ci, n_co, c_in_specs, c_out_specs, c_shapes, c_scratch, c_inputs = _comm_parts(comm)

    def body(*refs):
        (dy_ref, x_ref, gn_ref, dg_ref, du_ref, wg_ref, wu_ref), c_in, (dx_ref, dgn_ref), c_out, c_scr = _split(
            refs, 7, n_ci, 2, n_co)
        i = pl.program_id(0)
        if comm:
            _run_hosted(comm, comm_at, i, c_in, c_out, c_scr)
        dh = _dot_nn(dg_ref[...], wg_ref[...]) + _dot_nn(du_ref[...], wu_ref[...])
        xv = x_ref[...]
        dxn, dgn = _norm_bwd(dh, xv, _rstd(xv), gn_ref[...])
        dx_ref[...] = dy_ref[...] + dxn
        _accumulate(dgn_ref, dgn, i == 0)

    tok = pl.BlockSpec((tm, d), lambda i: (i, 0))
    row = pl.BlockSpec((1, d), lambda i: (0, 0))
    act = pl.BlockSpec((tm, f_all), lambda i: (i, 0))
    whole = pl.BlockSpec((f_all, d), lambda i: (0, 0))
    res = pl.pallas_call(
        body, name=name, grid=(t // tm,),
        out_shape=[jax.ShapeDtypeStruct((t, d), F32), jax.ShapeDtypeStruct((1, d), F32)] + c_shapes,
        in_specs=[tok, tok, row, act, act, whole, whole] + c_in_specs,
        out_specs=[tok, row] + c_out_specs,
        scratch_shapes=c_scratch,
        compiler_params=_cp(("arbitrary",)),
    )(dy, x, gn, dg, du, wg_t, wu_t, *c_inputs)
    return res[:2], res[2:]


def _token_products(lhs_list, rhs, name, tt, tf, comm=None, comm_at=None):
    n_l = len(lhs_list)
    t, f_all = lhs_list[0].shape
    d = rhs.shape[1]
    n_t = t // tt
    n_ci, n_co, c_in_specs, c_out_specs, c_shapes, c_scratch, c_inputs = _comm_parts(comm)

    def body(*refs):
        lhs_refs, (rhs_ref,), c_in, out_refs, c_out, accs, c_scr = _split(refs, n_l, 1, n_ci, n_l, n_co, n_l)
        s = pl.program_id(1)
        if comm:
            _run_hosted(comm, comm_at, pl.program_id(0) * n_t + s, c_in, c_out, c_scr)
        rv = rhs_ref[...]
        for l_ref, acc in zip(lhs_refs, accs):
            _accumulate(acc, _dot_tn(l_ref[...], rv), s == 0)

        @pl.when(s == n_t - 1)
        def _():
            for o_ref, acc in zip(out_refs, accs):
                o_ref[...] = acc[...].astype(BF16)

    act = pl.BlockSpec((tt, tf), lambda f, s: (s, f))
    tok = pl.BlockSpec((tt, d), lambda f, s: (s, 0))
    wblk = pl.BlockSpec((tf, d), lambda f, s: (f, 0))
    res = pl.pallas_call(
        body, name=name, grid=(f_all // tf, n_t),
        out_shape=[jax.ShapeDtypeStruct((f_all, d), BF16)] * n_l + c_shapes,
        in_specs=[act] * n_l + [tok] + c_in_specs, out_specs=[wblk] * n_l + c_out_specs,
        scratch_shapes=[pltpu.VMEM((tf, d), F32)] * n_l + c_scratch,
        compiler_params=_cp(("arbitrary", "arbitrary")),
    )(*lhs_list, rhs, *c_inputs)
    return res[:n_l], res[n_l:]


def _swap_halves(t):
    w = t.shape[-1]
    lane = lax.broadcasted_iota(jnp.int32, (1, w), 1)
    return jnp.where((lane % HEAD_DIM) < HEAD_DIM // 2, pltpu.roll(t, w - HEAD_DIM // 2, 1), pltpu.roll(t, HEAD_DIM // 2, 1))


def _rope(t, cos, sin_signed):
    reps = t.shape[-1] // LANES
    return t * jnp.tile(cos, (1, reps)) + _swap_halves(t) * jnp.tile(sin_signed, (1, reps))


def _rope_bwd(dt, cos, sin_signed):
    reps = dt.shape[-1] // LANES
    return dt * jnp.tile(cos, (1, reps)) + _swap_halves(dt * jnp.tile(sin_signed, (1, reps)))


DILATIONS = tuple(dil for _, dil in B_PATTERNS if dil > 1)


def _seg_shape(t, dil, w, dtype):
    return jax.ShapeDtypeStruct((dil, t // dil, w), dtype)


def _seg_spec(tm, dil, w):
    return pl.BlockSpec((dil, tm // dil, w), lambda i: (0, i, 0))


def _tile_scratch(tm, w):
    return [pltpu.VMEM((tm, LANES), F32)] * (w // LANES)


def _put_tile(tile, val):
    for c, ref in enumerate(tile):
        ref[...] = val[:, c * LANES:(c + 1) * LANES]


def _get_tile(tile):
    return jnp.concatenate([ref[...] for ref in tile], axis=1)


def _scatter_to_segments(tile, seg_refs):
    for seg_ref, dil in zip(seg_refs, DILATIONS):
        rows = tile[0].shape[0] // dil
        for r in range(dil):
            for c, ref in enumerate(tile):
                seg_ref[r, :, c * LANES:(c + 1) * LANES] = ref[pl.ds(r, rows, stride=dil), :].astype(seg_ref.dtype)


def _gather_from_segments(seg_ref, dil, tile, add=False):
    rows = tile[0].shape[0] // dil
    for r in range(dil):
        for c, ref in enumerate(tile):
            idx = (pl.ds(r, rows, stride=dil), slice(None))
            v = seg_ref[r, :, c * LANES:(c + 1) * LANES].astype(F32)
            ref[idx] = ref[idx] + v if add else v


def _in_proj_fwd(x, gn, win_t, cos, sin_signed, name, tm):
    t, d = x.shape
    in_w = win_t.shape[0]
    n_dil = len(DILATIONS)

    def body(x_ref, gn_ref, w_ref, cos_ref, sin_ref, h_ref, aq_ref, akx_ref, avx_ref, bq_ref, bk_ref, bv_ref, *rest):
        seg_refs, tile = rest[:3 * n_dil], rest[3 * n_dil:]
        xv = x_ref[...]
        h = ((xv * _rstd(xv)) * gn_ref[...]).astype(BF16)
        h_ref[...] = h
        p = _dot_nt(h, w_ref[...])
        cs, sn = cos_ref[...], sin_ref[...]
        o = 0
        aq_ref[...] = (_rope(p[:, o:o + A_Q_W], cs, sn) * QK_SCALE).astype(BF16)
        o += A_Q_W
        ak = _rope(p[:, o:o + A_KV_W], cs, sn)
        o += A_KV_W
        av = p[:, o:o + A_KV_W]
        o += A_KV_W
        low = lax.broadcasted_iota(jnp.int32, (1, LANES), 1) < HEAD_DIM
        for src, dst in ((ak, akx_ref), (av, avx_ref)):
            other = pltpu.roll(src, HEAD_DIM, 1)
            dst[0] = jnp.where(low, src, other).astype(BF16)
            dst[1] = jnp.where(low, other, src).astype(BF16)
        for k, nat_ref in enumerate((bq_ref, bk_ref, bv_ref)):
            val = p[:, o:o + B_W]
            o += B_W
            if k < 2:
                val = _rope(val, cs, sn)
            if k == 0:
                val = val * QK_SCALE
            nat_ref[...] = val.astype(BF16)
            _put_tile(tile, val)
            _scatter_to_segments(tile, seg_refs[k * n_dil:(k + 1) * n_dil])

    tok = lambda w: pl.BlockSpec((tm, w), lambda i: (i, 0))
    kvx = pl.BlockSpec((2, tm, LANES), lambda i: (0, i, 0))
    sd = lambda *s: jax.ShapeDtypeStruct(s, BF16)
    res = pl.pallas_call(
        body, name=name, grid=(t // tm,),
        out_shape=[sd(t, d), sd(t, A_Q_W), sd(2, t, LANES), sd(2, t, LANES), sd(t, B_W), sd(t, B_W), sd(t, B_W)]
        + [_seg_shape(t, dil, B_W, BF16) for _ in range(3) for dil in DILATIONS],
        in_specs=[tok(d), pl.BlockSpec((1, d), lambda i: (0, 0)), pl.BlockSpec((in_w, d), lambda i: (0, 0)),
                  tok(LANES), tok(LANES)],
        out_specs=[tok(d), tok(A_Q_W), kvx, kvx, tok(B_W), tok(B_W), tok(B_W)]
        + [_seg_spec(tm, dil, B_W) for _ in range(3) for dil in DILATIONS],
        scratch_shapes=_tile_scratch(tm, B_W),
        compiler_params=_cp(("arbitrary",)),
    )(x, gn, win_t, cos, sin_signed)
    return res[:7], [res[7 + k * n_dil:7 + (k + 1) * n_dil] for k in range(3)]


def _in_proj_bwd(dres, x, gn, win_t, h, cos, sin_signed, daq, dakx, davx, per_pattern, name, tm):
    t, d = x.shape
    in_w = win_t.shape[0]
    n_t = t // tm
    chunk = in_w // 3
    n_pat = 3 * (1 + len(DILATIONS))

    def body(*refs):
        ((dres_ref, x_ref, gn_ref, w_ref, h_ref, cos_ref, sin_ref, daq_ref, dakx_ref, davx_ref), pat,
         (dx_ref, dgn_ref, gw_ref, half_ref), (dp_s, gw_s), tile) = _split(refs, 10, n_pat, 4, 2)
        i = pl.program_id(0)
        cs, sn = cos_ref[...], sin_ref[...]
        low = lax.broadcasted_iota(jnp.int32, (1, LANES), 1) < HEAD_DIM

        def fold(ref):
            a, b = ref[0].astype(F32), ref[1].astype(F32)
            return jnp.where(low, a + pltpu.roll(a, HEAD_DIM, 1), b + pltpu.roll(b, HEAD_DIM, 1))

        def summed(j):
            _put_tile(tile, pat[j][...].astype(F32))
            for k, dil in enumerate(DILATIONS):
                _gather_from_segments(pat[3 * (1 + k) + j], dil, tile, add=True)
            return _get_tile(tile)

        o = 0
        dp_s[:, o:o + A_Q_W] = _rope_bwd(daq_ref[...].astype(F32) * QK_SCALE, cs, sn).astype(BF16)
        o += A_Q_W
        dp_s[:, o:o + A_KV_W] = _rope_bwd(fold(dakx_ref), cs, sn).astype(BF16)
        o += A_KV_W
        dp_s[:, o:o + A_KV_W] = fold(davx_ref).astype(BF16)
        o += A_KV_W
        dp_s[:, o:o + B_W] = _rope_bwd(summed(0) * QK_SCALE, cs, sn).astype(BF16)
        o += B_W
        dp_s[:, o:o + B_W] = _rope_bwd(summed(1), cs, sn).astype(BF16)
        o += B_W
        dp_s[:, o:o + B_W] = summed(2).astype(BF16)
        dh = _dot_nn(dp_s[...], w_ref[...])
        hv = h_ref[...]
        for c0 in range(0, in_w, chunk):
            _accumulate(gw_s.at[pl.ds(c0, chunk), :], _dot_tn(dp_s[:, c0:c0 + chunk], hv), i == 0)
        xv = x_ref[...]
        dxn, dgn = _norm_bwd(dh, xv, _rstd(xv), gn_ref[...])
        dx = dres_ref[...] + dxn
        dx_ref[...] = dx
        half_ref[...] = (FFN_RES_WEIGHT * dx).astype(BF16)
        _accumulate(dgn_ref, dgn, i == 0)

        @pl.when(i == n_t - 1)
        def _():
            gw_ref[...] = gw_s[...].astype(BF16)

    tok = lambda w: pl.BlockSpec((tm, w), lambda i: (i, 0))
    row = pl.BlockSpec((1, d), lambda i: (0, 0))
    whole = pl.BlockSpec((in_w, d), lambda i: (0, 0))
    weight = pl.BlockSpec((in_w, d), lambda i: (0, 0), pipeline_mode=pl.Buffered(1))
    kvx = pl.BlockSpec((2, tm, LANES), lambda i: (0, i, 0))
    flat = [a if k == 0 else a.reshape(DILATIONS[k - 1], -1, B_W) for k, grads in enumerate(per_pattern) for a in grads]
    return pl.pallas_call(
        body, name=name, grid=(n_t,),
        out_shape=[jax.ShapeDtypeStruct((t, d), F32), jax.ShapeDtypeStruct((1, d), F32),
                   jax.ShapeDtypeStruct((in_w, d), BF16), jax.ShapeDtypeStruct((t, d), BF16)],
        in_specs=[tok(d), tok(d), row, weight, tok(d), tok(LANES), tok(LANES), tok(A_Q_W), kvx, kvx]
        + [tok(B_W)] * 3 + [_seg_spec(tm, dil, B_W) for dil in DILATIONS for _ in range(3)],
        out_specs=[tok(d), row, whole, tok(d)],
        scratch_shapes=[pltpu.VMEM((tm, in_w), BF16), pltpu.VMEM((in_w, d), F32)] + _tile_scratch(tm, B_W),
        compiler_params=_cp(("arbitrary",)),
    )(dres, x, gn, win_t, h, cos, sin_signed, daq, dakx, davx, *flat)


def _merge_out_proj_fwd(x, a_out, outs, lses, wout, name, tm):
    t, d = x.shape
    n_dil = len(DILATIONS)

    def body(x_ref, a_ref, *rest):
        o_refs, l_refs, (w_ref, y_ref, b_ref, lt_ref), b_segs, lt_segs, scratch = _split(rest, 1 + n_dil, 1 + n_dil, 4, n_dil, n_dil)
        n_c = B_W // LANES
        tiles = [scratch[j * n_c:(j + 1) * n_c] for j in range(2 * n_dil)]
        os_, ls = [o_refs[0][...].astype(F32)], [l_refs[0][...]]
        for k, dil in enumerate(DILATIONS):
            _gather_from_segments(o_refs[1 + k], dil, tiles[2 * k])
            _gather_from_segments(l_refs[1 + k], dil, tiles[2 * k + 1])
            os_.append(_get_tile(tiles[2 * k]))
            ls.append(_get_tile(tiles[2 * k + 1]))
        mx = functools.reduce(jnp.maximum, ls)
        es = [jnp.exp(l - mx) for l in ls]
        den = functools.reduce(jnp.add, es)
        b = functools.reduce(jnp.add, [e * o for e, o in zip(es, os_)]) / den
        lt = mx + jnp.log(den)
        bb = b.astype(BF16)
        b_ref[...] = bb
        lt_ref[...] = lt
        y_ref[...] = x_ref[...] + _dot_nn(a_ref[...], w_ref[0:A_Q_W, :]) + _dot_nn(bb, w_ref[A_Q_W:A_Q_W + B_W, :])
        _put_tile(tiles[0], b)
        _scatter_to_segments(tiles[0], b_segs)
        _put_tile(tiles[1], lt)
        _scatter_to_segments(tiles[1], lt_segs)

    tok = lambda w: pl.BlockSpec((tm, w), lambda i: (i, 0))
    segs = [_seg_spec(tm, dil, B_W) for dil in DILATIONS]
    res = pl.pallas_call(
        body, name=name, grid=(t // tm,),
        out_shape=[jax.ShapeDtypeStruct((t, d), F32), jax.ShapeDtypeStruct((t, B_W), BF16), jax.ShapeDtypeStruct((t, B_W), F32)]
        + [_seg_shape(t, dil, B_W, BF16) for dil in DILATIONS] + [_seg_shape(t, dil, B_W, F32) for dil in DILATIONS],
        in_specs=[tok(d), tok(A_Q_W)] + ([tok(B_W)] + segs) * 2 + [pl.BlockSpec(wout.shape, lambda i: (0, 0))],
        out_specs=[tok(d), tok(B_W), tok(B_W)] + segs * 2,
        scratch_shapes=_tile_scratch(tm, B_W) * (2 * n_dil),
        compiler_params=_cp(("arbitrary",)),
    )(x, a_out, *outs, *lses, wout)
    return res[0], [res[1]] + list(res[3:3 + n_dil]), [res[2]] + list(res[3 + n_dil:])


def _out_proj_bwd(dy, a_out, b_out, wout, name, tm):
    t, d = dy.shape
    n_t = t // tm
    n_dil = len(DILATIONS)

    def body(dy_ref, a_ref, b_ref, w_ref, da_ref, db_ref, gw_ref, *rest):
        db_segs, gw_s, tile = rest[:n_dil], rest[n_dil], rest[n_dil + 1:]
        i = pl.program_id(0)
        dyb = dy_ref[...].astype(BF16)
        da_ref[...] = _dot_nt(dyb, w_ref[0:A_Q_W, :]).astype(BF16)
        db = _dot_nt(dyb, w_ref[A_Q_W:A_Q_W + B_W, :])
        db_ref[...] = db.astype(BF16)
        _put_tile(tile, db)
        _scatter_to_segments(tile, db_segs)
        ga = _dot_tn(a_ref[...], dyb)
        gb = _dot_tn(b_ref[...], dyb)

        @pl.when(i == 0)
        def _():
            gw_s[0:A_Q_W, :] = ga
            gw_s[A_Q_W:A_Q_W + B_W, :] = gb

        @pl.when(i > 0)
        def _():
            gw_s[0:A_Q_W, :] += ga
            gw_s[A_Q_W:A_Q_W + B_W, :] += gb

        @pl.when(i == n_t - 1)
        def _():
            gw_ref[...] = gw_s[...].astype(BF16)

    tok = lambda w: pl.BlockSpec((tm, w), lambda i: (i, 0))
    whole = pl.BlockSpec(wout.shape, lambda i: (0, 0))
    res = pl.pallas_call(
        body, name=name, grid=(n_t,),
        out_shape=[jax.ShapeDtypeStruct((t, A_Q_W), BF16), jax.ShapeDtypeStruct((t, B_W), BF16),
                   jax.ShapeDtypeStruct(wout.shape, BF16)] + [_seg_shape(t, dil, B_W, BF16) for dil in DILATIONS],
        in_specs=[tok(d), tok(A_Q_W), tok(B_W), whole],
        out_specs=[tok(A_Q_W), tok(B_W), whole] + [_seg_spec(tm, dil, B_W) for dil in DILATIONS],
        scratch_shapes=[pltpu.VMEM(wout.shape, F32)] + _tile_scratch(tm, B_W),
        compiler_params=_cp(("arbitrary",)),
    )(dy, a_out, b_out, wout)
    return res[0], [res[1]] + list(res[3:]), res[2]


SUB_ROWS = 64
WINDOW_ALIGN = 64


def _sub_band(r0, hw, win, seg_len, t, rel, col):
    ks = pl.multiple_of(jnp.clip(r0 - hw, 0, t - win), WINDOW_ALIGN)
    kpos = col + ks
    seg_lo = (r0 // seg_len) * seg_len
    valid = (jnp.abs(rel + (ks - r0)) <= hw) & (kpos >= seg_lo) & (kpos < seg_lo + seg_len)
    return ks, valid


def _split_heads(v, low):
    zero = jnp.zeros_like(v)
    return jnp.concatenate([jnp.where(low, v, zero), jnp.where(low, zero, v)], axis=0)


def _kv_spec(kv, t):
    if kv.ndim == 3:
        return pl.BlockSpec((None, t, LANES), lambda p, i: (p // 2, 0, 0))
    return pl.BlockSpec((t, LANES), lambda p, i: (0, p))


def _attn_fwd(q, k, v, sink, name, hw, seg_len, tq, has_sink):
    t, width = q.shape
    sb = min(SUB_ROWS, tq)
    win = 2 * hw + LANES

    def body(sink_ref, q_ref, k_ref, v_ref, o_ref, lse_ref):
        p, i = pl.program_id(0), pl.program_id(1)
        q0 = i * tq
        low = lax.broadcasted_iota(jnp.int32, (1, LANES), 1) < HEAD_DIM
        rel = lax.broadcasted_iota(jnp.int32, (sb, win), 1) - lax.broadcasted_iota(jnp.int32, (sb, win), 0)
        col = lax.broadcasted_iota(jnp.int32, (1, win), 1)
        subs = []
        for j in range(tq // sb):
            rows = pl.ds(j * sb, sb)
            ks, valid = _sub_band(q0 + j * sb, hw, win, seg_len, t, rel, col)
            subs.append((rows, ks, valid, _dot_nt(_split_heads(q_ref[rows, :], low), k_ref[pl.ds(ks, win), :])))
        for rows, ks, valid, s in subs:
            vw = v_ref[pl.ds(ks, win), :]
            es, inv, lses = [], [], []
            for a in range(2):
                sa = jnp.where(valid, s[a * sb:(a + 1) * sb], NEG)
                m = jnp.max(sa, axis=1, keepdims=True)
                if has_sink:
                    sk = sink_ref[2 * p + a]
                    m = jnp.maximum(m, sk)
                e = jnp.exp(sa - m)
                den = jnp.sum(e, axis=1, keepdims=True)
                if has_sink:
                    den = den + jnp.exp(sk - m)
                es.append(e.astype(BF16))
                inv.append(1.0 / den)
                lses.append(m + jnp.log(den))
            pv = _dot_nn(jnp.concatenate(es, axis=0), vw)
            o_ref[rows, :] = jnp.where(low, pv[0:sb] * inv[0], pv[sb:2 * sb] * inv[1]).astype(BF16)
            lse_ref[rows, :] = jnp.where(low, lses[0], lses[1])

    tile = pl.BlockSpec((tq, LANES), lambda p, i: (i, p))
    return pl.pallas_call(
        body, name=name, grid=(width // LANES, t // tq),
        out_shape=[jax.ShapeDtypeStruct((t, width), BF16), jax.ShapeDtypeStruct((t, width), F32)],
        in_specs=[pl.BlockSpec(memory_space=pltpu.SMEM), tile, _kv_spec(k, t), _kv_spec(v, t)],
        out_specs=[tile, tile],
        compiler_params=_cp(("arbitrary", "arbitrary")),
    )(sink, q, k, v)


def _attn_bwd(q, k, v, o, do, lse, sink, name, hw, seg_len, tq, has_sink, comm=None, comm_at=None):
    t, width = q.shape
    sb = min(SUB_ROWS, tq)
    win = 2 * hw + LANES
    n_q = t // tq
    shared_kv = k.ndim == 3
    n_ci, n_co, c_in_specs, c_out_specs, c_shapes, c_scratch, c_inputs = _comm_parts(comm)

    def body(*refs):
        ((sink_ref, q_ref, k_ref, v_ref, o_ref, do_ref, lse_ref), c_in, (dq_ref, dk_ref, dv_ref, ds_ref), c_out,
         (dk_s, dv_s), c_scr) = _split(refs, 7, n_ci, 4, n_co, 2)
        p, i = pl.program_id(0), pl.program_id(1)
        if comm:
            _run_hosted(comm, comm_at, p * n_q + i, c_in, c_out, c_scr)
        fresh = (i == 0) & (p % 2 == 0) if shared_kv else i == 0
        last = (i == n_q - 1) & (p % 2 == 1) if shared_kv else i == n_q - 1

        @pl.when(fresh)
        def _():
            dk_s[...] = jnp.zeros_like(dk_s)
            dv_s[...] = jnp.zeros_like(dv_s)

        q0 = i * tq
        low = lax.broadcasted_iota(jnp.int32, (1, LANES), 1) < HEAD_DIM
        rel = lax.broadcasted_iota(jnp.int32, (sb, win), 1) - lax.broadcasted_iota(jnp.int32, (sb, win), 0)
        col = lax.broadcasted_iota(jnp.int32, (1, win), 1)
        dsink = [jnp.zeros((1, 1), F32), jnp.zeros((1, 1), F32)]
        subs = []
        for j in range(tq // sb):
            rows = pl.ds(j * sb, sb)
            ks, valid = _sub_band(q0 + j * sb, hw, win, seg_len, t, rel, col)
            kw = k_ref[pl.ds(ks, win), :]
            dov = do_ref[rows, :]
            q2 = _split_heads(q_ref[rows, :], low)
            do2 = _split_heads(dov, low)
            subs.append((rows, ks, valid, kw, dov, q2, do2, _dot_nt(q2, kw), _dot_nt(do2, v_ref[pl.ds(ks, win), :])))
        probs = []
        for rows, ks, valid, kw, dov, q2, do2, s, dpr in subs:
            prod = dov.astype(F32) * o_ref[rows, :].astype(F32)
            lse_t = lse_ref[rows, :]
            prs, dss = [], []
            for a in range(2):
                mine = low if a == 0 else jnp.logical_not(low)
                lse_a = jnp.max(jnp.where(mine, lse_t, -jnp.inf), axis=1, keepdims=True)
                delta = jnp.sum(jnp.where(mine, prod, 0.0), axis=1, keepdims=True)
                pr = jnp.exp(jnp.where(valid, s[a * sb:(a + 1) * sb], NEG) - lse_a)
                prs.append(pr.astype(BF16))
                dss.append((pr * (dpr[a * sb:(a + 1) * sb] - delta)).astype(BF16))
                if has_sink:
                    dsink[a] = dsink[a] - jnp.sum(jnp.exp(sink_ref[2 * p + a] - lse_a) * delta, axis=0, keepdims=True)
            probs.append((jnp.concatenate(prs, axis=0), jnp.concatenate(dss, axis=0)))
        for (rows, ks, valid, kw, dov, q2, do2, s, dpr), (pr2, ds2) in zip(subs, probs):
            dv_s[pl.ds(ks, win), :] += _dot_tn(pr2, do2)
            dk_s[pl.ds(ks, win), :] += _dot_tn(ds2, q2)
            dq2 = _dot_nn(ds2, kw)
            dq_ref[rows, :] = jnp.where(low, dq2[0:sb], dq2[sb:2 * sb]).astype(BF16)
        ds_ref[...] = jnp.broadcast_to(jnp.where(low, dsink[0], dsink[1]), ds_ref.shape)

        @pl.when(last)
        def _():
            dk_ref[...] = dk_s[...].astype(BF16)
            dv_ref[...] = dv_s[...].astype(BF16)

    tile = pl.BlockSpec((tq, LANES), lambda p, i: (i, p))
    kv_shape = jax.ShapeDtypeStruct(k.shape, BF16)
    res = pl.pallas_call(
        body, name=name, grid=(width // LANES, n_q),
        out_shape=[jax.ShapeDtypeStruct((t, width), BF16), kv_shape, kv_shape,
                   jax.ShapeDtypeStruct((width // LANES, n_q, 8, LANES), F32)] + c_shapes,
        in_specs=[pl.BlockSpec(memory_space=pltpu.SMEM), tile, _kv_spec(k, t), _kv_spec(v, t), tile, tile, tile] + c_in_specs,
        out_specs=[tile, _kv_spec(k, t), _kv_spec(v, t), pl.BlockSpec((None, None, 8, LANES), lambda p, i: (p, i, 0, 0))] + c_out_specs,
        scratch_shapes=[pltpu.VMEM((t, LANES), F32)] * 2 + c_scratch,
        compiler_params=_cp(("arbitrary", "arbitrary")),
    )(sink, q, k, v, o, do, lse, *c_inputs)
    return res[:4], res[4:]


def _adamw(w, g, m, v, name):
    def body(w_ref, g_ref, m_ref, v_ref, d_ref, nm_ref, nv_ref):
        gv = g_ref[...]
        nm = ADAM_B1 * m_ref[...] + (1.0 - ADAM_B1) * gv
        nv = ADAM_B2 * v_ref[...] + (1.0 - ADAM_B2) * (gv * gv)
        m_hat = nm / (1.0 - ADAM_B1 ** ADAM_STEP)
        v_hat = nv / (1.0 - ADAM_B2 ** ADAM_STEP)
        d_ref[...] = -ADAM_LR * (m_hat / (jnp.sqrt(v_hat) + ADAM_EPS) + ADAM_WD * w_ref[...])
        nm_ref[...] = nm
        nv_ref[...] = nv

    shape = jax.ShapeDtypeStruct(w.shape, F32)
    return pl.pallas_call(body, name=name, out_shape=[shape, shape, shape], compiler_params=_cp())(w, g, m, v)


def _adamw_rows(land, j, rows, w, m, v, name):
    d = w.shape[1]

    def body(l_ref, w_ref, m_ref, v_ref, g_ref, d_ref, nm_ref, nv_ref):
        gv = l_ref[0].astype(F32)
        for s in range(1, N_DEV):
            gv = gv + l_ref[s].astype(F32)
        g_ref[...] = gv
        nm = ADAM_B1 * m_ref[...] + (1.0 - ADAM_B1) * gv
        nv = ADAM_B2 * v_ref[...] + (1.0 - ADAM_B2) * (gv * gv)
        m_hat = nm / (1.0 - ADAM_B1 ** ADAM_STEP)
        v_hat = nv / (1.0 - ADAM_B2 ** ADAM_STEP)
        d_ref[...] = -ADAM_LR * (m_hat / (jnp.sqrt(v_hat) + ADAM_EPS) + ADAM_WD * w_ref[...])
        nm_ref[...] = nm
        nv_ref[...] = nv

    halves = 2
    half = pl.BlockSpec((rows // halves, d), lambda i: (i, 0))
    shape = jax.ShapeDtypeStruct((rows, d), F32)
    return pl.pallas_call(
        body, name=name, grid=(halves,), out_shape=[shape] * 4,
        in_specs=[pl.BlockSpec((N_DEV, rows // halves, d), lambda i: (0, halves * j + i, 0)), half, half, half],
        out_specs=[half] * 4,
        compiler_params=_cp(("arbitrary",)),
    )(land, w, m, v)


def _sum_small(land, name):
    def body(l_ref, o_ref):
        acc = l_ref[0]
        for s in range(1, N_DEV):
            acc = acc + l_ref[s]
        o_ref[...] = acc

    return pl.pallas_call(body, name=name, out_shape=jax.ShapeDtypeStruct(land.shape[1:], F32), compiler_params=_cp())(land)


def kernel(x, positions, norm_ffn1, w_gate1, w_up1, w_down1, norm_mix, w_in, a_sink, w_out, norm_ffn2, w_gate2, w_up2, w_down2, norm_final, loss_target, m_norm_ffn1, m_w_gate1, m_w_up1, m_w_down1, m_norm_mix, m_w_in, m_a_sink, m_w_out, m_norm_ffn2, m_w_gate2, m_w_up2, m_w_down2, m_norm_final, v_norm_ffn1, v_w_gate1, v_w_up1, v_w_down1, v_norm_mix, v_w_in, v_a_sink, v_w_out, v_norm_ffn2, v_w_gate2, v_w_up2, v_w_down2, v_norm_final):
    x = x[0]
    target = loss_target[0]
    t, d = x.shape
    tm_mix = min(512, t)
    tq_of = lambda seg_len, most=2048: min(most, t, max(seg_len, 16 * SUB_ROWS))

    tt = min(1024, t)
    f_all = w_gate1.shape[2] * N_DEV
    tfp = f_all // 2
    tm_fwd, tf_fwd = tm_mix, tfp
    n_steps = (t // tm_fwd) * (f_all // tf_fwd)

    def stacked(shards):
        return jnp.concatenate([s.astype(BF16) for s in shards], axis=0), [s.shape[0] for s in shards]

    packed1, rows1 = stacked([w_gate1[0].T, w_up1[0].T, w_down1[0]])
    later = [w_in[0].T, w_out[0], w_gate2[0].T, w_up2[0].T, w_down2[0]]
    rows2 = [s.shape[0] for s in later]
    (wg1, wu1, wd1), packed2, cos, sin_signed = _run_alone_busy(
        _gather_plan(packed1, rows1, stages=len(rows1)), later, positions[0].astype(F32)[:, None], "gather_ffn1_weights")

    sink = a_sink[0]
    no_sink = jnp.zeros_like(sink)

    (x1, h1, gp1, up1, hid1), (win, wout, wg2, wu2, wd2) = _ffn_fwd(
        x, norm_ffn1, wg1, wu1, wd1, "ffn1_fwd", tm_fwd, tf_fwd, _gather_plan(packed2, rows2), [0, (3 * n_steps) // 4, n_steps - 1])
    (h_mix, aq, akx, avx, bq, bk, bv), seg_qkv = _in_proj_fwd(x1, norm_mix, win, cos, sin_signed, "in_proj_fwd", tm_mix)
    a_out, a_lse = _attn_fwd(aq, akx, avx, sink, "attn_a_fwd", A_HALF_WINDOW, t, tq_of(t), True)
    rows_of = lambda a: a.reshape(t, B_W)
    segments_of = lambda a, dil: a if dil == 1 else a.reshape(dil, t // dil, B_W)
    b_qkv, b_outs, b_lses = [], [], []
    for n, (window, dil) in enumerate(B_PATTERNS):
        qs, ks, vs = (bq, bk, bv) if dil == 1 else (rows_of(seg_qkv[k][n - 1]) for k in range(3))
        o_seg, lse_seg = _attn_fwd(qs, ks, vs, no_sink, f"attn_b{dil}_fwd", window // (2 * dil), t // dil,
                                  tq_of(t // dil, 16 * SUB_ROWS), False)
        b_qkv.append((qs, ks, vs))
        b_outs.append(segments_of(o_seg, dil))
        b_lses.append(segments_of(lse_seg, dil))
    x2, b_out, b_lse = _merge_out_proj_fwd(x1, a_out, b_outs, b_lses, wout, "out_proj_fwd", tm_mix)
    (dx3, dg_final, sq, dout2, h2, gp2, up2, hid2), _ = _ffn_fwd(
        x2, norm_ffn2, wg2, wu2, wd2, "ffn2_fwd_loss", tm_fwd, tf_fwd, loss_head=(norm_final.reshape(1, d), target))


    (dgt2, dut2, gwd2), _ = _ffn_bwd_hidden(dout2, gp2, up2, hid2, wd2, "ffn2_bwd_hidden", tm_mix, tfp)
    (dx2, dg_ffn2), _ = _ffn_bwd_input(dx3, x2, norm_ffn2, dgt2, dut2, wg2, wu2, "ffn2_bwd_input", tm_mix)
    (gwg2, gwu2), _ = _token_products([dgt2, dut2], h2, "ffn2_bwd_gate_up", tt, tfp)

    n_att = lambda width, tq: (width // LANES) * (t // tq)
    n_prod = (f_all // tfp) * (t // tt)
    da, db, gwout = _out_proj_bwd(dx2, a_out, b_out[0], wout, "out_proj_bwd", tm_mix)
    (daq, dakx, davx, dsink_parts), (land_wg2,) = _attn_bwd(
        aq, akx, avx, a_out, da, a_lse, sink, "attn_a_bwd", A_HALF_WINDOW, t, tq_of(t), True,
        _exchange_plan([[gwg2]]), [0, n_att(A_Q_W, tq_of(t)) - 1])
    pattern_grads, pattern_lands = [], []
    for n, ((window, dil), (qs, ks, vs)) in enumerate(zip(B_PATTERNS, b_qkv)):
        carried = [gwu2, gwd2][n:n + 1]
        grads, lands = _attn_bwd(qs, ks, vs, rows_of(b_out[n]), rows_of(db[n]), rows_of(b_lse[n]), no_sink, f"attn_b{dil}_bwd",
                                 window // (2 * dil), t // dil, tq_of(t // dil), False,
                                 _exchange_plan([carried]) if carried else None, [0, n_att(B_W, tq_of(t // dil)) - 1])
        pattern_grads.append(grads[:3])
        pattern_lands.extend(lands)
    land_wu2, land_wd2 = pattern_lands
    dx1, dg_mix, gwin, dout1 = _in_proj_bwd(dx2, x1, norm_mix, win, h_mix, cos, sin_signed, daq, dakx, davx, pattern_grads, "in_proj_bwd", tm_mix)

    (dgt1, dut1, gwd1), (land_in, land_out) = _ffn_bwd_hidden(
        dout1, gp1, up1, hid1, wd1, "ffn1_bwd_hidden", tm_mix, tfp, _exchange_plan([[gwin], [gwout]]),
        [0, (f_all // tfp) * (t // tm_mix) - 1])
    (gwg1,), (land_wd1,) = _token_products([dgt1], h1, "ffn1_bwd_gate", tt, tfp, _exchange_plan([[gwd1]]), [0, n_prod - 1])
    (gwu1,), (land_wg1,) = _token_products([dut1], h1, "ffn1_bwd_up", tt, tfp, _exchange_plan([[gwg1]]), [0, n_prod - 1])
    (grad_x, dg_ffn1), (land_wu1,) = _ffn_bwd_input(
        dx1, x, norm_ffn1, dgt1, dut1, wg1, wu1, "ffn1_bwd_input", tm_mix, _exchange_plan([[gwu1]]), [0, t // tm_mix - 1])

    dsink_pairs = jnp.sum(dsink_parts[:, :, 0, :], axis=1)
    dsink = jnp.stack([dsink_pairs[:, 0], dsink_pairs[:, HEAD_DIM]], axis=1).reshape(1, -1)
    small = jnp.concatenate([dg_ffn1, dg_mix, dg_ffn2, dg_final, jnp.pad(dsink, ((0, 0), (0, d - dsink.shape[1]))),
                             sq, jnp.zeros((2, d), F32)], axis=0)
    (land_small,) = _run_alone(_exchange_plan([[jnp.tile(small, (N_DEV, 1))]]), "gather_small_gradients")
    red_small = _sum_small(land_small, "sum_small_grads")
    loss = 0.5 * jnp.sum(red_small[5]) / d

    rf = rows1[0]
    sharded = {"w_gate1": (land_wg1, 0, rf, True), "w_up1": (land_wu1, 0, rf, True), "w_down1": (land_wd1, 0, rf, False),
               "w_in": (land_in, 0, rows2[0], True), "w_out": (land_out, 0, rows2[1], False),
               "w_gate2": (land_wg2, 0, rf, True), "w_up2": (land_wu2, 0, rf, True), "w_down2": (land_wd2, 0, rf, False)}
    n_sink = a_sink.shape[1]
    small_grads = {"norm_ffn1": red_small[0:1], "norm_mix": red_small[1:2], "norm_ffn2": red_small[2:3], "norm_final": red_small[3],
                   "a_sink": red_small[4:5, :n_sink]}
    params = {
        "norm_ffn1": (norm_ffn1, m_norm_ffn1, v_norm_ffn1), "w_gate1": (w_gate1, m_w_gate1, v_w_gate1),
        "w_up1": (w_up1, m_w_up1, v_w_up1), "w_down1": (w_down1, m_w_down1, v_w_down1),
        "norm_mix": (norm_mix, m_norm_mix, v_norm_mix), "w_in": (w_in, m_w_in, v_w_in),
        "a_sink": (a_sink, m_a_sink, v_a_sink), "w_out": (w_out, m_w_out, v_w_out),
        "norm_ffn2": (norm_ffn2, m_norm_ffn2, v_norm_ffn2), "w_gate2": (w_gate2, m_w_gate2, v_w_gate2),
        "w_up2": (w_up2, m_w_up2, v_w_up2), "w_down2": (w_down2, m_w_down2, v_w_down2),
        "norm_final": (norm_final, m_norm_final, v_norm_final),
    }
    grad_list, deltas, new_ms, new_vs = [], [], [], []
    for name, (w, m, v) in params.items():
        if name in sharded:
            land, j, rows, is_transposed = sharded[name]
            view = (lambda a: a[0].T) if is_transposed else (lambda a: a[0])
            back = (lambda a: a.T[None]) if is_transposed else (lambda a: a[None])
            outs = [back(o) for o in _adamw_rows(land, j, rows, view(w), view(m), view(v), f"adamw_{name}")]
        else:
            as_block = (lambda a: a.reshape(1, -1)) if w.ndim == 1 else (lambda a: a)
            g = small_grads[name]
            outs = [g] + [o.reshape(w.shape) for o in _adamw(as_block(w), as_block(g), as_block(m), as_block(v), f"adamw_{name}")]
        for lst, o in zip((grad_list, deltas, new_ms, new_vs), outs):
            lst.append(o)
    return (loss, grad_x[None], *grad_list, *deltas, *new_ms, *new_vs)
```

```python
import functools
import itertools

import numpy as np
import jax
import jax.numpy as jnp
from jax import lax
from jax.experimental import pallas as pl
from jax.experimental.pallas import tpu as pltpu

F32 = jnp.float32
BF16 = jnp.bfloat16

N_DEV = 8
HEAD_DIM = 64
LANES = 128
A_Q_W, A_KV_W, B_W = 512, 128, 512
A_HALF_WINDOW = 128
B_PATTERNS = ((128, 1), (512, 4), (2048, 16))
ROPE_THETA = 10000.0
NORM_EPS = 1e-6
FFN_RES_WEIGHT = 0.5
QK_SCALE = HEAD_DIM ** -0.5
NEG = -1e30

ADAM_LR = 0.001
ADAM_B1 = 0.9
ADAM_B2 = 0.999
ADAM_EPS = 1e-08
ADAM_WD = 0.01
ADAM_STEP = 10

MESH_T = pl.DeviceIdType.MESH
VMEM_LIMIT = 60 * 1024 * 1024


def _cp(sem=None, vmem=VMEM_LIMIT):
    return pltpu.CompilerParams(dimension_semantics=sem, vmem_limit_bytes=vmem)


def _dot_nn(a, b):
    return jnp.dot(a, b, preferred_element_type=F32)


def _dot_nt(a, b):
    return lax.dot_general(a, b, (((1,), (1,)), ((), ())), preferred_element_type=F32)


def _dot_tn(a, b):
    return lax.dot_general(a, b, (((0,), (0,)), ((), ())), preferred_element_type=F32)


def _rstd(xv):
    return lax.rsqrt(jnp.mean(xv * xv, axis=-1, keepdims=True) + NORM_EPS)


def _norm_bwd(dh, xv, r, gn):
    gy = dh * gn
    c = jnp.sum(gy * xv, axis=-1, keepdims=True) * (1.0 / xv.shape[-1])
    dx = r * gy - xv * (r * r * r * c)
    dgn = jnp.sum(dh * (xv * r), axis=0, keepdims=True)
    return dx, dgn


def _accumulate(ref, val, first):
    @pl.when(first)
    def _():
        ref[...] = val

    @pl.when(jnp.logical_not(first))
    def _():
        ref[...] += val


def _mesh_pos():
    return lax.axis_index("x"), lax.axis_index("y"), lax.axis_index("c")


def _dev_index(d):
    return 4 * d[0] + 2 * d[1] + d[2]


class _Comm:
    def __init__(self, inputs, out_shape, scratch, phases):
        self.inputs, self.out_shape, self.scratch, self.phases = inputs, out_shape, scratch, phases

    def specs(self):
        any_spec = pl.BlockSpec(memory_space=pl.ANY)
        return [any_spec] * len(self.inputs), [any_spec] * len(self.out_shape)


def _run_alone(comm, name):
    n_in, n_out = len(comm.inputs), len(comm.out_shape)

    def body(*refs):
        for phase in comm.phases:
            phase(refs[:n_in], refs[n_in:n_in + n_out], refs[n_in + n_out:])

    in_specs, out_specs = comm.specs()
    return pl.pallas_call(body, name=name, out_shape=comm.out_shape, in_specs=in_specs, out_specs=out_specs,
                          scratch_shapes=comm.scratch)(*comm.inputs)


def _run_alone_busy(comm, shards, positions, name):
    n_in, n_out = len(comm.inputs), len(comm.out_shape)
    offs = [int(o) for o in np.cumsum([0] + [s.shape[0] for s in shards])]
    t = positions.shape[0]
    half = HEAD_DIM // 2
    inv_freq = 1.0 / (ROPE_THETA ** (jnp.arange(0, HEAD_DIM, 2, dtype=F32) / HEAD_DIM))
    lanes = jnp.stack([jnp.tile(inv_freq, LANES // half),
                       jnp.tile(jnp.concatenate([-jnp.ones(half, F32), jnp.ones(half, F32)]), LANES // HEAD_DIM)])

    def body(*refs):
        c_in, s_refs, (pos_ref, lanes_ref), c_out, (stack_ref, cos_ref, sin_ref), scr = _split(refs, n_in, len(shards), 2, n_out, 3)
        comm.phases[0](c_in, c_out, scr)
        for s_ref, lo, hi in zip(s_refs, offs[:-1], offs[1:]):
            stack_ref[lo:hi, :] = s_ref[...].astype(BF16)
        ang = pos_ref[...] * lanes_ref[0:1, :]
        cos_ref[...] = jnp.cos(ang)
        sin_ref[...] = jnp.sin(ang) * lanes_ref[1:2, :]
        for phase in comm.phases[1:]:
            phase(c_in, c_out, scr)

    in_specs, out_specs = comm.specs()
    vmem = pl.BlockSpec(memory_space=pltpu.VMEM)
    table = jax.ShapeDtypeStruct((t, LANES), F32)
    res = pl.pallas_call(
        body, name=name, out_shape=comm.out_shape + [jax.ShapeDtypeStruct((offs[-1], shards[0].shape[1]), BF16), table, table],
        in_specs=in_specs + [vmem] * (len(shards) + 2), out_specs=out_specs + [vmem] * 3, scratch_shapes=comm.scratch,
        compiler_params=_cp())(*comm.inputs, *shards, positions, lanes)
    return res[:n_out], res[n_out], res[n_out + 1], res[n_out + 2]


def _run_hosted(comm, at, step, ins, outs, scr):
    for phase, when in zip(comm.phases, at):
        @pl.when(step == when)
        def _(phase=phase):
            phase(ins, outs, scr)


def _split(refs, *counts):
    parts, o = [], 0
    for n in counts:
        parts.append(refs[o:o + n])
        o += n
    return parts + [refs[o:]]


def _gather_plan(packed, rows_list, stages=1):
    n_w = len(rows_list)
    offs = [int(o) for o in np.cumsum([0] + list(rows_list))]
    d = packed.shape[1]
    per = n_w // stages
    groups = [range(s * per, (s + 1) * per) for s in range(stages)]

    def tools(ins, outs, scr):
        p_ref = ins[0]
        send_sems, recv_sems, local_sem = scr
        x, y, c = _mesh_pos()
        me, sibling = (x, y, c), (x, y, 1 - c)
        chips = [(1 - x, y), (x, 1 - y), (1 - x, 1 - y)]

        def rows(w, dev):
            start = pl.multiple_of(_dev_index(dev) * rows_list[w], 16)
            return outs[w].at[pl.ds(start, rows_list[w]), :]

        def mine(w):
            return p_ref.at[pl.ds(offs[w], rows_list[w]), :]

        def copy(k, s, w, block, to, own):
            return pltpu.make_async_remote_copy(
                src_ref=mine(w) if own else rows(w, block), dst_ref=rows(w, block),
                send_sem=send_sems.at[k, s], recv_sem=recv_sems.at[k, s], device_id=to, device_id_type=MESH_T)

        def whole_group(k, s):
            span = p_ref.at[pl.ds(offs[groups[s][0]], offs[groups[s][-1] + 1] - offs[groups[s][0]]), :]
            return pltpu.make_async_remote_copy(
                src_ref=span, dst_ref=span, send_sem=send_sems.at[k, s], recv_sem=recv_sems.at[k, s],
                device_id=me, device_id_type=MESH_T)

        return p_ref, local_sem, me, sibling, chips, c, rows, mine, copy, whole_group

    def start(ins, outs, scr):
        _, local_sem, me, sibling, chips, c, rows, mine, copy, _ = tools(ins, outs, scr)
        for w in range(n_w):
            pltpu.make_async_copy(mine(w), rows(w, me), local_sem).start()
        for s, group in enumerate(groups):
            for w in group:
                copy(0, s, w, me, sibling, True).start()
            for j, chip in enumerate(chips):
                for w in group:
                    copy(1 + j, s, w, me, (*chip, c), True).start()

    def relay(ins, outs, scr):
        _, _, _, sibling, chips, c, _, _, copy, whole_group = tools(ins, outs, scr)
        for s, group in enumerate(groups):
            for j, chip in enumerate(chips):
                whole_group(1 + j, s).wait_recv()
                for w in group:
                    copy(4 + j, s, w, (*chip, c), sibling, False).start()

    def finish(ins, outs, scr):
        p_ref, local_sem, _, _, _, _, _, _, _, whole_group = tools(ins, outs, scr)
        for s in range(stages):
            whole_group(0, s).wait_recv()
            for j in range(3):
                whole_group(4 + j, s).wait_recv()
            for k in range(7):
                whole_group(k, s).wait_send()
        pltpu.make_async_copy(p_ref, p_ref, local_sem).wait()

    return _Comm(
        [packed], [jax.ShapeDtypeStruct((N_DEV * r, d), packed.dtype) for r in rows_list],
        [pltpu.SemaphoreType.DMA((7, stages)), pltpu.SemaphoreType.DMA((7, stages)), pltpu.SemaphoreType.DMA], [start, relay, finish])


def _exchange_plan(groups):
    flat = [a for g in groups for a in g]
    n_g = len(groups)
    sizes = [len(g) for g in groups]
    rows = [g[0].shape[0] // N_DEV for g in groups]
    first = [int(o) for o in np.cumsum([0] + sizes[:-1])]

    def start(srcs, lands, scr):
        send_sems, recv_sems, local_sems = scr
        x, y, c = _mesh_pos()
        me = (x, y, c)
        me_idx = _dev_index(me)

        def block(g, i, dev):
            start_row = pl.multiple_of(_dev_index(dev) * rows[g], 8)
            return srcs[first[g] + i].at[pl.ds(start_row, rows[g]), :]

        def slot(g, i):
            return lands[g].at[me_idx, pl.ds(i * rows[g], rows[g]), :]

        for g in range(n_g):
            for i in range(sizes[g]):
                pltpu.make_async_copy(block(g, i, me), slot(g, i), local_sems.at[g]).start()
        flips = [f for f in itertools.product((0, 1), repeat=3) if any(f)]
        for k, (fx, fy, fc) in enumerate(flips):
            peer = (1 - x if fx else x, 1 - y if fy else y, 1 - c if fc else c)
            for g in range(n_g):
                for i in range(sizes[g]):
                    pltpu.make_async_remote_copy(
                        src_ref=block(g, i, peer), dst_ref=slot(g, i), send_sem=send_sems.at[g, k],
                        recv_sem=recv_sems.at[g, k], device_id=peer, device_id_type=MESH_T).start()

    def finish(srcs, lands, scr):
        send_sems, recv_sems, local_sems = scr
        me = _mesh_pos()
        for k in range(7):
            for g in range(n_g):
                pltpu.make_async_remote_copy(
                    src_ref=lands[g].at[0], dst_ref=lands[g].at[0], send_sem=send_sems.at[g, k],
                    recv_sem=recv_sems.at[g, k], device_id=me, device_id_type=MESH_T).wait()
        for g in range(n_g):
            pltpu.make_async_copy(lands[g].at[0], lands[g].at[0], local_sems.at[g]).wait()

    return _Comm(
        flat, [jax.ShapeDtypeStruct((N_DEV, sizes[g] * rows[g], groups[g][0].shape[1]), groups[g][0].dtype) for g in range(n_g)],
        [pltpu.SemaphoreType.DMA((n_g, 7)), pltpu.SemaphoreType.DMA((n_g, 7)), pltpu.SemaphoreType.DMA((n_g,))],
        [start, finish])


def _comm_parts(comm):
    if comm is None:
        return 0, 0, [], [], [], [], []
    in_specs, out_specs = comm.specs()
    return len(comm.inputs), len(comm.out_shape), in_specs, out_specs, comm.out_shape, comm.scratch, comm.inputs


def _ffn_fwd(x, gn, wg_t, wu_t, wd, name, tm, tf, comm=None, comm_at=None, loss_head=None):
    t, d = x.shape
    f_all = wg_t.shape[0]
    n_f = f_all // tf
    n_ci, n_co, c_in_specs, c_out_specs, c_shapes, c_scratch, c_inputs = _comm_parts(comm)
    n_head, n_y = (2, 4) if loss_head else (0, 1)

    def body(*refs):
        ((x_ref, gn_ref, wg_ref, wu_ref, wd_ref), head_in, c_in, y_outs, (h_ref, gp_ref, up_ref, hid_ref), c_out,
         (h_s, hid_s), c_scr) = _split(refs, 5, n_head, n_ci, n_y, 4, n_co, 2)
        f = pl.program_id(1)
        if comm:
            _run_hosted(comm, comm_at, pl.program_id(0) * n_f + f, c_in, c_out, c_scr)

        @pl.when(f == 0)
        def _():
            xv = x_ref[...]
            h = ((xv * _rstd(xv)) * gn_ref[...]).astype(BF16)
            h_s[...] = h
            h_ref[...] = h

        h = h_s[...]
        cols = pl.ds(pl.multiple_of(f * tf, tf), tf)
        g = _dot_nt(h, wg_ref[cols, :])
        u = _dot_nt(h, wu_ref[cols, :])
        sg = jax.nn.sigmoid(g)
        silu = g * sg
        gp_ref[...] = (u * (sg * (1.0 + g * (1.0 - sg)))).astype(BF16)
        up_ref[...] = silu.astype(BF16)
        hid = (silu * u).astype(BF16)
        hid_ref[...] = hid
        for f0 in range(n_f):
            @pl.when(f == f0)
            def _(f0=f0):
                hid_s[:, f0 * tf:(f0 + 1) * tf] = hid

        @pl.when(f == n_f - 1)
        def _():
            y = x_ref[...] + FFN_RES_WEIGHT * _dot_nn(hid_s[...], wd_ref[...])
            if not loss_head:
                y_outs[0][...] = y
            else:
                (gf_ref, tg_ref), (dy_ref, dgf_ref, sq_ref, half_ref) = head_in, y_outs
                gfv, r, first = gf_ref[...], _rstd(y), pl.program_id(0) == 0
                diff = (y * r) * gfv - tg_ref[...]
                dy, dgf = _norm_bwd(diff * (1.0 / d), y, r, gfv)
                dy_ref[...] = dy
                half_ref[...] = (FFN_RES_WEIGHT * dy).astype(BF16)
                _accumulate(dgf_ref, dgf, first)
                _accumulate(sq_ref, jnp.sum(diff * diff, axis=0, keepdims=True), first)

    tok = pl.BlockSpec((tm, d), lambda i, f: (i, 0))
    row = pl.BlockSpec((1, d), lambda i, f: (0, 0))
    whole = pl.BlockSpec((f_all, d), lambda i, f: (0, 0), pipeline_mode=pl.Buffered(1))
    act = pl.BlockSpec((tm, tf), lambda i, f: (i, f))
    act_shape = jax.ShapeDtypeStruct((t, f_all), BF16)
    tok_f32, tok_bf16, row_f32 = (jax.ShapeDtypeStruct((t, d), F32), jax.ShapeDtypeStruct((t, d), BF16),
                                  jax.ShapeDtypeStruct((1, d), F32))
    y_shapes, y_specs = ([tok_f32, row_f32, row_f32, tok_bf16], [tok, row, row, tok]) if loss_head else ([tok_f32], [tok])
    res = pl.pallas_call(
        body, name=name, grid=(t // tm, n_f),
        out_shape=y_shapes + [tok_bf16, act_shape, act_shape, act_shape] + c_shapes,
        in_specs=[tok, row, whole, whole, whole] + ([row, tok] if loss_head else []) + c_in_specs,
        out_specs=y_specs + [tok, act, act, act] + c_out_specs,
        scratch_shapes=[pltpu.VMEM((tm, d), BF16), pltpu.VMEM((tm, f_all), BF16)] + c_scratch,
        compiler_params=_cp(("arbitrary", "arbitrary")),
    )(x, gn, wg_t, wu_t, wd, *(loss_head or ()), *c_inputs)
    return res[:n_y + 4], res[n_y + 4:]


def _ffn_bwd_hidden(dout, gp, up, hid, wd, name, tt, tf, comm=None, comm_at=None):
    t, d = dout.shape
    f_all = wd.shape[0]
    n_t = t // tt
    n_ci, n_co, c_in_specs, c_out_specs, c_shapes, c_scratch, c_inputs = _comm_parts(comm)

    def body(*refs):
        (dout_ref, gp_ref, up_ref, hid_ref, wd_ref), c_in, (dg_ref, du_ref, gwd_ref), c_out, (acc,), c_scr = _split(
            refs, 5, n_ci, 3, n_co, 1)
        s = pl.program_id(1)
        if comm:
            _run_hosted(comm, comm_at, pl.program_id(0) * n_t + s, c_in, c_out, c_scr)
        doutv = dout_ref[...]
        dhid = _dot_nt(doutv, wd_ref[...])
        dg_ref[...] = (dhid * gp_ref[...].astype(F32)).astype(BF16)
        du_ref[...] = (dhid * up_ref[...].astype(F32)).astype(BF16)
        _accumulate(acc, _dot_tn(hid_ref[...], doutv), s == 0)

        @pl.when(s == n_t - 1)
        def _():
            gwd_ref[...] = acc[...].astype(BF16)

    tok = pl.BlockSpec((tt, d), lambda f, s: (s, 0))
    wblk = pl.BlockSpec((tf, d), lambda f, s: (f, 0))
    act = pl.BlockSpec((tt, tf), lambda f, s: (s, f))
    act_shape = jax.ShapeDtypeStruct((t, f_all), BF16)
    res = pl.pallas_call(
        body, name=name, grid=(f_all // tf, n_t),
        out_shape=[act_shape, act_shape, jax.ShapeDtypeStruct((f_all, d), BF16)] + c_shapes,
        in_specs=[tok, act, act, act, wblk] + c_in_specs, out_specs=[act, act, wblk] + c_out_specs,
        scratch_shapes=[pltpu.VMEM((tf, d), F32)] + c_scratch,
        compiler_params=_cp(("arbitrary", "arbitrary")),
    )(dout, gp, up, hid, wd, *c_inputs)
    return res[:3], res[3:]


def _ffn_bwd_input(dy, x, gn, dg, du, wg_t, wu_t, name, tm, comm=None, comm_at=None):
    t, d = x.shape
    f_all = wg_t.shape[0]
    n_ci, n_co, c_in_specs, c_out_specs, c_shapes, c_scratch, c_inputs = _comm_parts(comm)

    def body(*refs):
        (dy_ref, x_ref, gn_ref, dg_ref, du_ref, wg_ref, wu_ref), c_in, (dx_ref, dgn_ref), c_out, c_scr = _split(
            refs, 7, n_ci, 2, n_co)
        i = pl.program_id(0)
        if comm:
            _run_hosted(comm, comm_at, i, c_in, c_out, c_scr)
        dh = _dot_nn(dg_ref[...], wg_ref[...]) + _dot_nn(du_ref[...], wu_ref[...])
        xv = x_ref[...]
        dxn, dgn = _norm_bwd(dh, xv, _rstd(xv), gn_ref[...])
        dx_ref[...] = dy_ref[...] + dxn
        _accumulate(dgn_ref, dgn, i == 0)

    tok = pl.BlockSpec((tm, d), lambda i: (i, 0))
    row = pl.BlockSpec((1, d), lambda i: (0, 0))
    act = pl.BlockSpec((tm, f_all), lambda i: (i, 0))
    whole = pl.BlockSpec((f_all, d), lambda i: (0, 0))
    res = pl.pallas_call(
        body, name=name, grid=(t // tm,),
        out_shape=[jax.ShapeDtypeStruct((t, d), F32), jax.ShapeDtypeStruct((1, d), F32)] + c_shapes,
        in_specs=[tok, tok, row, act, act, whole, whole] + c_in_specs,
        out_specs=[tok, row] + c_out_specs,
        scratch_shapes=c_scratch,
        compiler_params=_cp(("arbitrary",)),
    )(dy, x, gn, dg, du, wg_t, wu_t, *c_inputs)
    return res[:2], res[2:]


def _token_products(lhs_list, rhs, name, tt, tf, comm=None, comm_at=None):
    n_l = len(lhs_list)
    t, f_all = lhs_list[0].shape
    d = rhs.shape[1]
    n_t = t // tt
    n_ci, n_co, c_in_specs, c_out_specs, c_shapes, c_scratch, c_inputs = _comm_parts(comm)

    def body(*refs):
        lhs_refs, (rhs_ref,), c_in, out_refs, c_out, accs, c_scr = _split(refs, n_l, 1, n_ci, n_l, n_co, n_l)
        s = pl.program_id(1)
        if comm:
            _run_hosted(comm, comm_at, pl.program_id(0) * n_t + s, c_in, c_out, c_scr)
        rv = rhs_ref[...]
        for l_ref, acc in zip(lhs_refs, accs):
            _accumulate(acc, _dot_tn(l_ref[...], rv), s == 0)

        @pl.when(s == n_t - 1)
        def _():
            for o_ref, acc in zip(out_refs, accs):
                o_ref[...] = acc[...].astype(BF16)

    act = pl.BlockSpec((tt, tf), lambda f, s: (s, f))
    tok = pl.BlockSpec((tt, d), lambda f, s: (s, 0))
    wblk = pl.BlockSpec((tf, d), lambda f, s: (f, 0))
    res = pl.pallas_call(
        body, name=name, grid=(f_all // tf, n_t),
        out_shape=[jax.ShapeDtypeStruct((f_all, d), BF16)] * n_l + c_shapes,
        in_specs=[act] * n_l + [tok] + c_in_specs, out_specs=[wblk] * n_l + c_out_specs,
        scratch_shapes=[pltpu.VMEM((tf, d), F32)] * n_l + c_scratch,
        compiler_params=_cp(("arbitrary", "arbitrary")),
    )(*lhs_list, rhs, *c_inputs)
    return res[:n_l], res[n_l:]


def _swap_halves(t):
    w = t.shape[-1]
    lane = lax.broadcasted_iota(jnp.int32, (1, w), 1)
    return jnp.where((lane % HEAD_DIM) < HEAD_DIM // 2, pltpu.roll(t, w - HEAD_DIM // 2, 1), pltpu.roll(t, HEAD_DIM // 2, 1))


def _rope(t, cos, sin_signed):
    reps = t.shape[-1] // LANES
    return t * jnp.tile(cos, (1, reps)) + _swap_halves(t) * jnp.tile(sin_signed, (1, reps))


def _rope_bwd(dt, cos, sin_signed):
    reps = dt.shape[-1] // LANES
    return dt * jnp.tile(cos, (1, reps)) + _swap_halves(dt * jnp.tile(sin_signed, (1, reps)))


DILATIONS = tuple(dil for _, dil in B_PATTERNS if dil > 1)


def _seg_shape(t, dil, w, dtype):
    return jax.ShapeDtypeStruct((dil, t // dil, w), dtype)


def _seg_spec(tm, dil, w):
    return pl.BlockSpec((dil, tm // dil, w), lambda i: (0, i, 0))


def _tile_scratch(tm, w):
    return [pltpu.VMEM((tm, LANES), F32)] * (w // LANES)


def _put_tile(tile, val):
    for c, ref in enumerate(tile):
        ref[...] = val[:, c * LANES:(c + 1) * LANES]


def _get_tile(tile):
    return jnp.concatenate([ref[...] for ref in tile], axis=1)


def _scatter_to_segments(tile, seg_refs):
    for seg_ref, dil in zip(seg_refs, DILATIONS):
        rows = tile[0].shape[0] // dil
        for r in range(dil):
            for c, ref in enumerate(tile):
                seg_ref[r, :, c * LANES:(c + 1) * LANES] = ref[pl.ds(r, rows, stride=dil), :].astype(seg_ref.dtype)


def _gather_from_segments(seg_ref, dil, tile, add=False):
    rows = tile[0].shape[0] // dil
    for r in range(dil):
        for c, ref in enumerate(tile):
            idx = (pl.ds(r, rows, stride=dil), slice(None))
            v = seg_ref[r, :, c * LANES:(c + 1) * LANES].astype(F32)
            ref[idx] = ref[idx] + v if add else v


def _in_proj_fwd(x, gn, win_t, cos, sin_signed, name, tm):
    t, d = x.shape
    in_w = win_t.shape[0]
    n_dil = len(DILATIONS)

    def body(x_ref, gn_ref, w_ref, cos_ref, sin_ref, h_ref, aq_ref, akx_ref, avx_ref, bq_ref, bk_ref, bv_ref, *rest):
        seg_refs, tile = rest[:3 * n_dil], rest[3 * n_dil:]
        xv = x_ref[...]
        h = ((xv * _rstd(xv)) * gn_ref[...]).astype(BF16)
        h_ref[...] = h
        p = _dot_nt(h, w_ref[...])
        cs, sn = cos_ref[...], sin_ref[...]
        o = 0
        aq_ref[...] = (_rope(p[:, o:o + A_Q_W], cs, sn) * QK_SCALE).astype(BF16)
        o += A_Q_W
        ak = _rope(p[:, o:o + A_KV_W], cs, sn)
        o += A_KV_W
        av = p[:, o:o + A_KV_W]
        o += A_KV_W
        low = lax.broadcasted_iota(jnp.int32, (1, LANES), 1) < HEAD_DIM
        for src, dst in ((ak, akx_ref), (av, avx_ref)):
            other = pltpu.roll(src, HEAD_DIM, 1)
            dst[0] = jnp.where(low, src, other).astype(BF16)
            dst[1] = jnp.where(low, other, src).astype(BF16)
        for k, nat_ref in enumerate((bq_ref, bk_ref, bv_ref)):
            val = p[:, o:o + B_W]
            o += B_W
            if k < 2:
                val = _rope(val, cs, sn)
            if k == 0:
                val = val * QK_SCALE
            nat_ref[...] = val.astype(BF16)
            _put_tile(tile, val)
            _scatter_to_segments(tile, seg_refs[k * n_dil:(k + 1) * n_dil])

    tok = lambda w: pl.BlockSpec((tm, w), lambda i: (i, 0))
    kvx = pl.BlockSpec((2, tm, LANES), lambda i: (0, i, 0))
    sd = lambda *s: jax.ShapeDtypeStruct(s, BF16)
    res = pl.pallas_call(
        body, name=name, grid=(t // tm,),
        out_shape=[sd(t, d), sd(t, A_Q_W), sd(2, t, LANES), sd(2, t, LANES), sd(t, B_W), sd(t, B_W), sd(t, B_W)]
        + [_seg_shape(t, dil, B_W, BF16) for _ in range(3) for dil in DILATIONS],
        in_specs=[tok(d), pl.BlockSpec((1, d), lambda i: (0, 0)), pl.BlockSpec((in_w, d), lambda i: (0, 0)),
                  tok(LANES), tok(LANES)],
        out_specs=[tok(d), tok(A_Q_W), kvx, kvx, tok(B_W), tok(B_W), tok(B_W)]
        + [_seg_spec(tm, dil, B_W) for _ in range(3) for dil in DILATIONS],
        scratch_shapes=_tile_scratch(tm, B_W),
        compiler_params=_cp(("arbitrary",)),
    )(x, gn, win_t, cos, sin_signed)
    return res[:7], [res[7 + k * n_dil:7 + (k + 1) * n_dil] for k in range(3)]


def _in_proj_bwd(dres, x, gn, win_t, h, cos, sin_signed, daq, dakx, davx, per_pattern, name, tm):
    t, d = x.shape
    in_w = win_t.shape[0]
    n_t = t // tm
    chunk = in_w // 3
    n_pat = 3 * (1 + len(DILATIONS))

    def body(*refs):
        ((dres_ref, x_ref, gn_ref, w_ref, h_ref, cos_ref, sin_ref, daq_ref, dakx_ref, davx_ref), pat,
         (dx_ref, dgn_ref, gw_ref, half_ref), (dp_s, gw_s), tile) = _split(refs, 10, n_pat, 4, 2)
        i = pl.program_id(0)
        cs, sn = cos_ref[...], sin_ref[...]
        low = lax.broadcasted_iota(jnp.int32, (1, LANES), 1) < HEAD_DIM

        def fold(ref):
            a, b = ref[0].astype(F32), ref[1].astype(F32)
            return jnp.where(low, a + pltpu.roll(a, HEAD_DIM, 1), b + pltpu.roll(b, HEAD_DIM, 1))

        def summed(j):
            _put_tile(tile, pat[j][...].astype(F32))
            for k, dil in enumerate(DILATIONS):
                _gather_from_segments(pat[3 * (1 + k) + j], dil, tile, add=True)
            return _get_tile(tile)

        o = 0
        dp_s[:, o:o + A_Q_W] = _rope_bwd(daq_ref[...].astype(F32) * QK_SCALE, cs, sn).astype(BF16)
        o += A_Q_W
        dp_s[:, o:o + A_KV_W] = _rope_bwd(fold(dakx_ref), cs, sn).astype(BF16)
        o += A_KV_W
        dp_s[:, o:o + A_KV_W] = fold(davx_ref).astype(BF16)
        o += A_KV_W
        dp_s[:, o:o + B_W] = _rope_bwd(summed(0) * QK_SCALE, cs, sn).astype(BF16)
        o += B_W
        dp_s[:, o:o + B_W] = _rope_bwd(summed(1), cs, sn).astype(BF16)
        o += B_W
        dp_s[:, o:o + B_W] = summed(2).astype(BF16)
        dh = _dot_nn(dp_s[...], w_ref[...])
        hv = h_ref[...]
        for c0 in range(0, in_w, chunk):
            _accumulate(gw_s.at[pl.ds(c0, chunk), :], _dot_tn(dp_s[:, c0:c0 + chunk], hv), i == 0)
        xv = x_ref[...]
        dxn, dgn = _norm_bwd(dh, xv, _rstd(xv), gn_ref[...])
        dx = dres_ref[...] + dxn
        dx_ref[...] = dx
        half_ref[...] = (FFN_RES_WEIGHT * dx).astype(BF16)
        _accumulate(dgn_ref, dgn, i == 0)

        @pl.when(i == n_t - 1)
        def _():
            gw_ref[...] = gw_s[...].astype(BF16)

    tok = lambda w: pl.BlockSpec((tm, w), lambda i: (i, 0))
    row = pl.BlockSpec((1, d), lambda i: (0, 0))
    whole = pl.BlockSpec((in_w, d), lambda i: (0, 0))
    weight = pl.BlockSpec((in_w, d), lambda i: (0, 0), pipeline_mode=pl.Buffered(1))
    kvx = pl.BlockSpec((2, tm, LANES), lambda i: (0, i, 0))
    flat = [a if k == 0 else a.reshape(DILATIONS[k - 1], -1, B_W) for k, grads in enumerate(per_pattern) for a in grads]
    return pl.pallas_call(
        body, name=name, grid=(n_t,),
        out_shape=[jax.ShapeDtypeStruct((t, d), F32), jax.ShapeDtypeStruct((1, d), F32),
                   jax.ShapeDtypeStruct((in_w, d), BF16), jax.ShapeDtypeStruct((t, d), BF16)],
        in_specs=[tok(d), tok(d), row, weight, tok(d), tok(LANES), tok(LANES), tok(A_Q_W), kvx, kvx]
        + [tok(B_W)] * 3 + [_seg_spec(tm, dil, B_W) for dil in DILATIONS for _ in range(3)],
        out_specs=[tok(d), row, whole, tok(d)],
        scratch_shapes=[pltpu.VMEM((tm, in_w), BF16), pltpu.VMEM((in_w, d), F32)] + _tile_scratch(tm, B_W),
        compiler_params=_cp(("arbitrary",)),
    )(dres, x, gn, win_t, h, cos, sin_signed, daq, dakx, davx, *flat)


def _merge_out_proj_fwd(x, a_out, outs, lses, wout, name, tm):
    t, d = x.shape
    n_dil = len(DILATIONS)

    def body(x_ref, a_ref, *rest):
        o_refs, l_refs, (w_ref, y_ref, b_ref, lt_ref), b_segs, lt_segs, scratch = _split(rest, 1 + n_dil, 1 + n_dil, 4, n_dil, n_dil)
        n_c = B_W // LANES
        tiles = [scratch[j * n_c:(j + 1) * n_c] for j in range(2 * n_dil)]
        os_, ls = [o_refs[0][...].astype(F32)], [l_refs[0][...]]
        for k, dil in enumerate(DILATIONS):
            _gather_from_segments(o_refs[1 + k], dil, tiles[2 * k])
            _gather_from_segments(l_refs[1 + k], dil, tiles[2 * k + 1])
            os_.append(_get_tile(tiles[2 * k]))
            ls.append(_get_tile(tiles[2 * k + 1]))
        mx = functools.reduce(jnp.maximum, ls)
        es = [jnp.exp(l - mx) for l in ls]
        den = functools.reduce(jnp.add, es)
        b = functools.reduce(jnp.add, [e * o for e, o in zip(es, os_)]) / den
        lt = mx + jnp.log(den)
        bb = b.astype(BF16)
        b_ref[...] = bb
        lt_ref[...] = lt
        y_ref[...] = x_ref[...] + _dot_nn(a_ref[...], w_ref[0:A_Q_W, :]) + _dot_nn(bb, w_ref[A_Q_W:A_Q_W + B_W, :])
        _put_tile(tiles[0], b)
        _scatter_to_segments(tiles[0], b_segs)
        _put_tile(tiles[1], lt)
        _scatter_to_segments(tiles[1], lt_segs)

    tok = lambda w: pl.BlockSpec((tm, w), lambda i: (i, 0))
    segs = [_seg_spec(tm, dil, B_W) for dil in DILATIONS]
    res = pl.pallas_call(
        body, name=name, grid=(t // tm,),
        out_shape=[jax.ShapeDtypeStruct((t, d), F32), jax.ShapeDtypeStruct((t, B_W), BF16), jax.ShapeDtypeStruct((t, B_W), F32)]
        + [_seg_shape(t, dil, B_W, BF16) for dil in DILATIONS] + [_seg_shape(t, dil, B_W, F32) for dil in DILATIONS],
        in_specs=[tok(d), tok(A_Q_W)] + ([tok(B_W)] + segs) * 2 + [pl.BlockSpec(wout.shape, lambda i: (0, 0))],
        out_specs=[tok(d), tok(B_W), tok(B_W)] + segs * 2,
        scratch_shapes=_tile_scratch(tm, B_W) * (2 * n_dil),
        compiler_params=_cp(("arbitrary",)),
    )(x, a_out, *outs, *lses, wout)
    return res[0], [res[1]] + list(res[3:3 + n_dil]), [res[2]] + list(res[3 + n_dil:])


def _out_proj_bwd(dy, a_out, b_out, wout, name, tm):
    t, d = dy.shape
    n_t = t // tm
    n_dil = len(DILATIONS)

    def body(dy_ref, a_ref, b_ref, w_ref, da_ref, db_ref, gw_ref, *rest):
        db_segs, gw_s, tile = rest[:n_dil], rest[n_dil], rest[n_dil + 1:]
        i = pl.program_id(0)
        dyb = dy_ref[...].astype(BF16)
        da_ref[...] = _dot_nt(dyb, w_ref[0:A_Q_W, :]).astype(BF16)
        db = _dot_nt(dyb, w_ref[A_Q_W:A_Q_W + B_W, :])
        db_ref[...] = db.astype(BF16)
        _put_tile(tile, db)
        _scatter_to_segments(tile, db_segs)
        ga = _dot_tn(a_ref[...], dyb)
        gb = _dot_tn(b_ref[...], dyb)

        @pl.when(i == 0)
        def _():
            gw_s[0:A_Q_W, :] = ga
            gw_s[A_Q_W:A_Q_W + B_W, :] = gb

        @pl.when(i > 0)
        def _():
            gw_s[0:A_Q_W, :] += ga
            gw_s[A_Q_W:A_Q_W + B_W, :] += gb

        @pl.when(i == n_t - 1)
        def _():
            gw_ref[...] = gw_s[...].astype(BF16)

    tok = lambda w: pl.BlockSpec((tm, w), lambda i: (i, 0))
    whole = pl.BlockSpec(wout.shape, lambda i: (0, 0))
    res = pl.pallas_call(
        body, name=name, grid=(n_t,),
        out_shape=[jax.ShapeDtypeStruct((t, A_Q_W), BF16), jax.ShapeDtypeStruct((t, B_W), BF16),
                   jax.ShapeDtypeStruct(wout.shape, BF16)] + [_seg_shape(t, dil, B_W, BF16) for dil in DILATIONS],
        in_specs=[tok(d), tok(A_Q_W), tok(B_W), whole],
        out_specs=[tok(A_Q_W), tok(B_W), whole] + [_seg_spec(tm, dil, B_W) for dil in DILATIONS],
        scratch_shapes=[pltpu.VMEM(wout.shape, F32)] + _tile_scratch(tm, B_W),
        compiler_params=_cp(("arbitrary",)),
    )(dy, a_out, b_out, wout)
    return res[0], [res[1]] + list(res[3:]), res[2]


SUB_ROWS = 64
WINDOW_ALIGN = 64


def _sub_band(r0, hw, win, seg_len, t, rel, col):
    ks = pl.multiple_of(jnp.clip(r0 - hw, 0, t - win), WINDOW_ALIGN)
    kpos = col + ks
    seg_lo = (r0 // seg_len) * seg_len
    valid = (jnp.abs(rel + (ks - r0)) <= hw) & (kpos >= seg_lo) & (kpos < seg_lo + seg_len)
    return ks, valid


def _split_heads(v, low):
    zero = jnp.zeros_like(v)
    return jnp.concatenate([jnp.where(low, v, zero), jnp.where(low, zero, v)], axis=0)


def _kv_spec(kv, t):
    if kv.ndim == 3:
        return pl.BlockSpec((None, t, LANES), lambda p, i: (p // 2, 0, 0))
    return pl.BlockSpec((t, LANES), lambda p, i: (0, p))


def _attn_fwd(q, k, v, sink, name, hw, seg_len, tq, has_sink):
    t, width = q.shape
    sb = min(SUB_ROWS, tq)
    win = 2 * hw + LANES

    def body(sink_ref, q_ref, k_ref, v_ref, o_ref, lse_ref):
        p, i = pl.program_id(0), pl.program_id(1)
        q0 = i * tq
        low = lax.broadcasted_iota(jnp.int32, (1, LANES), 1) < HEAD_DIM
        rel = lax.broadcasted_iota(jnp.int32, (sb, win), 1) - lax.broadcasted_iota(jnp.int32, (sb, win), 0)
        col = lax.broadcasted_iota(jnp.int32, (1, win), 1)
        subs = []
        for j in range(tq // sb):
            rows = pl.ds(j * sb, sb)
            ks, valid = _sub_band(q0 + j * sb, hw, win, seg_len, t, rel, col)
            subs.append((rows, ks, valid, _dot_nt(_split_heads(q_ref[rows, :], low), k_ref[pl.ds(ks, win), :])))
        for rows, ks, valid, s in subs:
            vw = v_ref[pl.ds(ks, win), :]
            es, inv, lses = [], [], []
            for a in range(2):
                sa = jnp.where(valid, s[a * sb:(a + 1) * sb], NEG)
                m = jnp.max(sa, axis=1, keepdims=True)
                if has_sink:
                    sk = sink_ref[2 * p + a]
                    m = jnp.maximum(m, sk)
                e = jnp.exp(sa - m)
                den = jnp.sum(e, axis=1, keepdims=True)
                if has_sink:
                    den = den + jnp.exp(sk - m)
                es.append(e.astype(BF16))
                inv.append(1.0 / den)
                lses.append(m + jnp.log(den))
            pv = _dot_nn(jnp.concatenate(es, axis=0), vw)
            o_ref[rows, :] = jnp.where(low, pv[0:sb] * inv[0], pv[sb:2 * sb] * inv[1]).astype(BF16)
            lse_ref[rows, :] = jnp.where(low, lses[0], lses[1])

    tile = pl.BlockSpec((tq, LANES), lambda p, i: (i, p))
    return pl.pallas_call(
        body, name=name, grid=(width // LANES, t // tq),
        out_shape=[jax.ShapeDtypeStruct((t, width), BF16), jax.ShapeDtypeStruct((t, width), F32)],
        in_specs=[pl.BlockSpec(memory_space=pltpu.SMEM), tile, _kv_spec(k, t), _kv_spec(v, t)],
        out_specs=[tile, tile],
        compiler_params=_cp(("arbitrary", "arbitrary")),
    )(sink, q, k, v)


def _attn_bwd(q, k, v, o, do, lse, sink, name, hw, seg_len, tq, has_sink, comm=None, comm_at=None):
    t, width = q.shape
    sb = min(SUB_ROWS, tq)
    win = 2 * hw + LANES
    n_q = t // tq
    shared_kv = k.ndim == 3
    n_ci, n_co, c_in_specs, c_out_specs, c_shapes, c_scratch, c_inputs = _comm_parts(comm)

    def body(*refs):
        ((sink_ref, q_ref, k_ref, v_ref, o_ref, do_ref, lse_ref), c_in, (dq_ref, dk_ref, dv_ref, ds_ref), c_out,
         (dk_s, dv_s), c_scr) = _split(refs, 7, n_ci, 4, n_co, 2)
        p, i = pl.program_id(0), pl.program_id(1)
        if comm:
            _run_hosted(comm, comm_at, p * n_q + i, c_in, c_out, c_scr)
        fresh = (i == 0) & (p % 2 == 0) if shared_kv else i == 0
        last = (i == n_q - 1) & (p % 2 == 1) if shared_kv else i == n_q - 1

        @pl.when(fresh)
        def _():
            dk_s[...] = jnp.zeros_like(dk_s)
            dv_s[...] = jnp.zeros_like(dv_s)

        q0 = i * tq
        low = lax.broadcasted_iota(jnp.int32, (1, LANES), 1) < HEAD_DIM
        rel = lax.broadcasted_iota(jnp.int32, (sb, win), 1) - lax.broadcasted_iota(jnp.int32, (sb, win), 0)
        col = lax.broadcasted_iota(jnp.int32, (1, win), 1)
        dsink = [jnp.zeros((1, 1), F32), jnp.zeros((1, 1), F32)]
        subs = []
        for j in range(tq // sb):
            rows = pl.ds(j * sb, sb)
            ks, valid = _sub_band(q0 + j * sb, hw, win, seg_len, t, rel, col)
            kw = k_ref[pl.ds(ks, win), :]
            dov = do_ref[rows, :]
            q2 = _split_heads(q_ref[rows, :], low)
            do2 = _split_heads(dov, low)
            subs.append((rows, ks, valid, kw, dov, q2, do2, _dot_nt(q2, kw), _dot_nt(do2, v_ref[pl.ds(ks, win), :])))
        probs = []
        for rows, ks, valid, kw, dov, q2, do2, s, dpr in subs:
            prod = dov.astype(F32) * o_ref[rows, :].astype(F32)
            lse_t = lse_ref[rows, :]
            prs, dss = [], []
            for a in range(2):
                mine = low if a == 0 else jnp.logical_not(low)
                lse_a = jnp.max(jnp.where(mine, lse_t, -jnp.inf), axis=1, keepdims=True)
                delta = jnp.sum(jnp.where(mine, prod, 0.0), axis=1, keepdims=True)
                pr = jnp.exp(jnp.where(valid, s[a * sb:(a + 1) * sb], NEG) - lse_a)
                prs.append(pr.astype(BF16))
                dss.append((pr * (dpr[a * sb:(a + 1) * sb] - delta)).astype(BF16))
                if has_sink:
                    dsink[a] = dsink[a] - jnp.sum(jnp.exp(sink_ref[2 * p + a] - lse_a) * delta, axis=0, keepdims=True)
            probs.append((jnp.concatenate(prs, axis=0), jnp.concatenate(dss, axis=0)))
        for (rows, ks, valid, kw, dov, q2, do2, s, dpr), (pr2, ds2) in zip(subs, probs):
            dv_s[pl.ds(ks, win), :] += _dot_tn(pr2, do2)
            dk_s[pl.ds(ks, win), :] += _dot_tn(ds2, q2)
            dq2 = _dot_nn(ds2, kw)
            dq_ref[rows, :] = jnp.where(low, dq2[0:sb], dq2[sb:2 * sb]).astype(BF16)
        ds_ref[...] = jnp.broadcast_to(jnp.where(low, dsink[0], dsink[1]), ds_ref.shape)

        @pl.when(last)
        def _():
            dk_ref[...] = dk_s[...].astype(BF16)
            dv_ref[...] = dv_s[...].astype(BF16)

    tile = pl.BlockSpec((tq, LANES), lambda p, i: (i, p))
    kv_shape = jax.ShapeDtypeStruct(k.shape, BF16)
    res = pl.pallas_call(
        body, name=name, grid=(width // LANES, n_q),
        out_shape=[jax.ShapeDtypeStruct((t, width), BF16), kv_shape, kv_shape,
                   jax.ShapeDtypeStruct((width // LANES, n_q, 8, LANES), F32)] + c_shapes,
        in_specs=[pl.BlockSpec(memory_space=pltpu.SMEM), tile, _kv_spec(k, t), _kv_spec(v, t), tile, tile, tile] + c_in_specs,
        out_specs=[tile, _kv_spec(k, t), _kv_spec(v, t), pl.BlockSpec((None, None, 8, LANES), lambda p, i: (p, i, 0, 0))] + c_out_specs,
        scratch_shapes=[pltpu.VMEM((t, LANES), F32)] * 2 + c_scratch,
        compiler_params=_cp(("arbitrary", "arbitrary")),
    )(sink, q, k, v, o, do, lse, *c_inputs)
    return res[:4], res[4:]


def _adamw(w, g, m, v, name):
    def body(w_ref, g_ref, m_ref, v_ref, d_ref, nm_ref, nv_ref):
        gv = g_ref[...]
        nm = ADAM_B1 * m_ref[...] + (1.0 - ADAM_B1) * gv
        nv = ADAM_B2 * v_ref[...] + (1.0 - ADAM_B2) * (gv * gv)
        m_hat = nm / (1.0 - ADAM_B1 ** ADAM_STEP)
        v_hat = nv / (1.0 - ADAM_B2 ** ADAM_STEP)
        d_ref[...] = -ADAM_LR * (m_hat / (jnp.sqrt(v_hat) + ADAM_EPS) + ADAM_WD * w_ref[...])
        nm_ref[...] = nm
        nv_ref[...] = nv

    shape = jax.ShapeDtypeStruct(w.shape, F32)
    return pl.pallas_call(body, name=name, out_shape=[shape, shape, shape], compiler_params=_cp())(w, g, m, v)


def _adamw_rows(land, j, rows, w, m, v, name):
    d = w.shape[1]

    def body(l_ref, w_ref, m_ref, v_ref, g_ref, d_ref, nm_ref, nv_ref):
        gv = l_ref[0].astype(F32)
        for s in range(1, N_DEV):
            gv = gv + l_ref[s].astype(F32)
        g_ref[...] = gv
        nm = ADAM_B1 * m_ref[...] + (1.0 - ADAM_B1) * gv
        nv = ADAM_B2 * v_ref[...] + (1.0 - ADAM_B2) * (gv * gv)
        m_hat = nm / (1.0 - ADAM_B1 ** ADAM_STEP)
        v_hat = nv / (1.0 - ADAM_B2 ** ADAM_STEP)
        d_ref[...] = -ADAM_LR * (m_hat / (jnp.sqrt(v_hat) + ADAM_EPS) + ADAM_WD * w_ref[...])
        nm_ref[...] = nm
        nv_ref[...] = nv

    halves = 2
    half = pl.BlockSpec((rows // halves, d), lambda i: (i, 0))
    shape = jax.ShapeDtypeStruct((rows, d), F32)
    return pl.pallas_call(
        body, name=name, grid=(halves,), out_shape=[shape] * 4,
        in_specs=[pl.BlockSpec((N_DEV, rows // halves, d), lambda i: (0, halves * j + i, 0)), half, half, half],
        out_specs=[half] * 4,
        compiler_params=_cp(("arbitrary",)),
    )(land, w, m, v)


def _sum_small(land, name):
    def body(l_ref, o_ref):
        acc = l_ref[0]
        for s in range(1, N_DEV):
            acc = acc + l_ref[s]
        o_ref[...] = acc

    return pl.pallas_call(body, name=name, out_shape=jax.ShapeDtypeStruct(land.shape[1:], F32), compiler_params=_cp())(land)


def kernel(x, positions, norm_ffn1, w_gate1, w_up1, w_down1, norm_mix, w_in, a_sink, w_out, norm_ffn2, w_gate2, w_up2, w_down2, norm_final, loss_target, m_norm_ffn1, m_w_gate1, m_w_up1, m_w_down1, m_norm_mix, m_w_in, m_a_sink, m_w_out, m_norm_ffn2, m_w_gate2, m_w_up2, m_w_down2, m_norm_final, v_norm_ffn1, v_w_gate1, v_w_up1, v_w_down1, v_norm_mix, v_w_in, v_a_sink, v_w_out, v_norm_ffn2, v_w_gate2, v_w_up2, v_w_down2, v_norm_final):
    x = x[0]
    target = loss_target[0]
    t, d = x.shape
    tm_mix = min(512, t)
    tq_of = lambda seg_len, most=2048: min(most, t, max(seg_len, 16 * SUB_ROWS))

    tt = min(1024, t)
    f_all = w_gate1.shape[2] * N_DEV
    tfp = f_all // 2
    tm_fwd, tf_fwd = tm_mix, tfp
    n_steps = (t // tm_fwd) * (f_all // tf_fwd)

    def stacked(shards):
        return jnp.concatenate([s.astype(BF16) for s in shards], axis=0), [s.shape[0] for s in shards]

    packed1, rows1 = stacked([w_gate1[0].T, w_up1[0].T, w_down1[0]])
    later = [w_in[0].T, w_out[0], w_gate2[0].T, w_up2[0].T, w_down2[0]]
    rows2 = [s.shape[0] for s in later]
    (wg1, wu1, wd1), packed2, cos, sin_signed = _run_alone_busy(
        _gather_plan(packed1, rows1, stages=len(rows1)), later, positions[0].astype(F32)[:, None], "gather_ffn1_weights")

    sink = a_sink[0]
    no_sink = jnp.zeros_like(sink)

    (x1, h1, gp1, up1, hid1), (win, wout, wg2, wu2, wd2) = _ffn_fwd(
        x, norm_ffn1, wg1, wu1, wd1, "ffn1_fwd", tm_fwd, tf_fwd, _gather_plan(packed2, rows2), [0, (3 * n_steps) // 4, n_steps - 1])
    (h_mix, aq, akx, avx, bq, bk, bv), seg_qkv = _in_proj_fwd(x1, norm_mix, win, cos, sin_signed, "in_proj_fwd", tm_mix)
    a_out, a_lse = _attn_fwd(aq, akx, avx, sink, "attn_a_fwd", A_HALF_WINDOW, t, tq_of(t), True)
    rows_of = lambda a: a.reshape(t, B_W)
    segments_of = lambda a, dil: a if dil == 1 else a.reshape(dil, t // dil, B_W)
    b_qkv, b_outs, b_lses = [], [], []
    for n, (window, dil) in enumerate(B_PATTERNS):
        qs, ks, vs = (bq, bk, bv) if dil == 1 else (rows_of(seg_qkv[k][n - 1]) for k in range(3))
        o_seg, lse_seg = _attn_fwd(qs, ks, vs, no_sink, f"attn_b{dil}_fwd", window // (2 * dil), t // dil,
                                  tq_of(t // dil, 16 * SUB_ROWS), False)
        b_qkv.append((qs, ks, vs))
        b_outs.append(segments_of(o_seg, dil))
        b_lses.append(segments_of(lse_seg, dil))
    x2, b_out, b_lse = _merge_out_proj_fwd(x1, a_out, b_outs, b_lses, wout, "out_proj_fwd", tm_mix)
    (dx3, dg_final, sq, dout2, h2, gp2, up2, hid2), _ = _ffn_fwd(
        x2, norm_ffn2, wg2, wu2, wd2, "ffn2_fwd_loss", tm_fwd, tf_fwd, loss_head=(norm_final.reshape(1, d), target))


    (dgt2, dut2, gwd2), _ = _ffn_bwd_hidden(dout2, gp2, up2, hid2, wd2, "ffn2_bwd_hidden", tm_mix, tfp)
    (dx2, dg_ffn2), _ = _ffn_bwd_input(dx3, x2, norm_ffn2, dgt2, dut2, wg2, wu2, "ffn2_bwd_input", tm_mix)
    (gwg2, gwu2), _ = _token_products([dgt2, dut2], h2, "ffn2_bwd_gate_up", tt, tfp)

    n_att = lambda width, tq: (width // LANES) * (t // tq)
    n_prod = (f_all // tfp) * (t // tt)
    da, db, gwout = _out_proj_bwd(dx2, a_out, b_out[0], wout, "out_proj_bwd", tm_mix)
    (daq, dakx, davx, dsink_parts), (land_wg2,) = _attn_bwd(
        aq, akx, avx, a_out, da, a_lse, sink, "attn_a_bwd", A_HALF_WINDOW, t, tq_of(t), True,
        _exchange_plan([[gwg2]]), [0, n_att(A_Q_W, tq_of(t)) - 1])
    pattern_grads, pattern_lands = [], []
    for n, ((window, dil), (qs, ks, vs)) in enumerate(zip(B_PATTERNS, b_qkv)):
        carried = [gwu2, gwd2][n:n + 1]
        grads, lands = _attn_bwd(qs, ks, vs, rows_of(b_out[n]), rows_of(db[n]), rows_of(b_lse[n]), no_sink, f"attn_b{dil}_bwd",
                                 window // (2 * dil), t // dil, tq_of(t // dil), False,
                                 _exchange_plan([carried]) if carried else None, [0, n_att(B_W, tq_of(t // dil)) - 1])
        pattern_grads.append(grads[:3])
        pattern_lands.extend(lands)
    land_wu2, land_wd2 = pattern_lands
    dx1, dg_mix, gwin, dout1 = _in_proj_bwd(dx2, x1, norm_mix, win, h_mix, cos, sin_signed, daq, dakx, davx, pattern_grads, "in_proj_bwd", tm_mix)

    (dgt1, dut1, gwd1), (land_in, land_out) = _ffn_bwd_hidden(
        dout1, gp1, up1, hid1, wd1, "ffn1_bwd_hidden", tm_mix, tfp, _exchange_plan([[gwin], [gwout]]),
        [0, (f_all // tfp) * (t // tm_mix) - 1])
    (gwg1, gwu1), (land_wd1,) = _token_products(
        [dgt1, dut1], h1, "ffn1_bwd_gate_up", tt, tfp, _exchange_plan([[gwd1]]), [0, n_prod - 1])
    (grad_x, dg_ffn1), (land_wg1, land_wu1) = _ffn_bwd_input(
        dx1, x, norm_ffn1, dgt1, dut1, wg1, wu1, "ffn1_bwd_input", tm_mix, _exchange_plan([[gwg1], [gwu1]]), [0, t // tm_mix - 1])

    dsink_pairs = jnp.sum(dsink_parts[:, :, 0, :], axis=1)
    dsink = jnp.stack([dsink_pairs[:, 0], dsink_pairs[:, HEAD_DIM]], axis=1).reshape(1, -1)
    small = jnp.concatenate([dg_ffn1, dg_mix, dg_ffn2, dg_final, jnp.pad(dsink, ((0, 0), (0, d - dsink.shape[1]))),
                             sq, jnp.zeros((2, d), F32)], axis=0)
    (land_small,) = _run_alone(_exchange_plan([[jnp.tile(small, (N_DEV, 1))]]), "gather_small_gradients")
    red_small = _sum_small(land_small, "sum_small_grads")
    loss = 0.5 * jnp.sum(red_small[5]) / d

    rf = rows1[0]
    sharded = {"w_gate1": (land_wg1, 0, rf, True), "w_up1": (land_wu1, 0, rf, True), "w_down1": (land_wd1, 0, rf, False),
               "w_in": (land_in, 0, rows2[0], True), "w_out": (land_out, 0, rows2[1], False),
               "w_gate2": (land_wg2, 0, rf, True), "w_up2": (land_wu2, 0, rf, True), "w_down2": (land_wd2, 0, rf, False)}
    n_sink = a_sink.shape[1]
    small_grads = {"norm_ffn1": red_small[0:1], "norm_mix": red_small[1:2], "norm_ffn2": red_small[2:3], "norm_final": red_small[3],
                   "a_sink": red_small[4:5, :n_sink]}
    params = {
        "norm_ffn1": (norm_ffn1, m_norm_ffn1, v_norm_ffn1), "w_gate1": (w_gate1, m_w_gate1, v_w_gate1),
        "w_up1": (w_up1, m_w_up1, v_w_up1), "w_down1": (w_down1, m_w_down1, v_w_down1),
        "norm_mix": (norm_mix, m_norm_mix, v_norm_mix), "w_in": (w_in, m_w_in, v_w_in),
        "a_sink": (a_sink, m_a_sink, v_a_sink), "w_out": (w_out, m_w_out, v_w_out),
        "norm_ffn2": (norm_ffn2, m_norm_ffn2, v_norm_ffn2), "w_gate2": (w_gate2, m_w_gate2, v_w_gate2),
        "w_up2": (w_up2, m_w_up2, v_w_up2), "w_down2": (w_down2, m_w_down2, v_w_down2),
        "norm_final": (norm_final, m_norm_final, v_norm_final),
    }
    grad_list, deltas, new_ms, new_vs = [], [], [], []
    for name, (w, m, v) in params.items():
        if name in sharded:
            land, j, rows, is_transposed = sharded[name]
            view = (lambda a: a[0].T) if is_transposed else (lambda a: a[0])
            back = (lambda a: a.T[None]) if is_transposed else (lambda a: a[None])
            outs = [back(o) for o in _adamw_rows(land, j, rows, view(w), view(m), view(v), f"adamw_{name}")]
        else:
            as_block = (lambda a: a.reshape(1, -1)) if w.ndim == 1 else (lambda a: a)
            g = small_grads[name]
            outs = [g] + [o.reshape(w.shape) for o in _adamw(as_block(w), as_block(g), as_block(m), as_block(v), f"adamw_{name}")]
        for lst, o in zip((grad_list, deltas, new_ms, new_vs), outs):
            lst.append(o)
    return (loss, grad_x[None], *grad_list, *deltas, *new_ms, *new_vs)
```

```python
import functools
import itertools

import numpy as np
import jax
import jax.numpy as jnp
from jax import lax
from jax.experimental import pallas as pl
from jax.experimental.pallas import tpu as pltpu

F32 = jnp.float32
BF16 = jnp.bfloat16

N_DEV = 8
HEAD_DIM = 64
LANES = 128
A_Q_W, A_KV_W, B_W = 512, 128, 512
A_HALF_WINDOW = 128
B_PATTERNS = ((128, 1), (512, 4), (2048, 16))
ROPE_THETA = 10000.0
NORM_EPS = 1e-6
FFN_RES_WEIGHT = 0.5
QK_SCALE = HEAD_DIM ** -0.5
NEG = -1e30

ADAM_LR = 0.001
ADAM_B1 = 0.9
ADAM_B2 = 0.999
ADAM_EPS = 1e-08
ADAM_WD = 0.01
ADAM_STEP = 10

MESH_T = pl.DeviceIdType.MESH
VMEM_LIMIT = 60 * 1024 * 1024


def _cp(sem=None, vmem=VMEM_LIMIT):
    return pltpu.CompilerParams(dimension_semantics=sem, vmem_limit_bytes=vmem)


def _dot_nn(a, b):
    return jnp.dot(a, b, preferred_element_type=F32)


def _dot_nt(a, b):
    return lax.dot_general(a, b, (((1,), (1,)), ((), ())), preferred_element_type=F32)


def _dot_tn(a, b):
    return lax.dot_general(a, b, (((0,), (0,)), ((), ())), preferred_element_type=F32)


def _rstd(xv):
    return lax.rsqrt(jnp.mean(xv * xv, axis=-1, keepdims=True) + NORM_EPS)


def _norm_bwd(dh, xv, r, gn):
    gy = dh * gn
    c = jnp.sum(gy * xv, axis=-1, keepdims=True) * (1.0 / xv.shape[-1])
    dx = r * gy - xv * (r * r * r * c)
    dgn = jnp.sum(dh * (xv * r), axis=0, keepdims=True)
    return dx, dgn


def _accumulate(ref, val, first):
    @pl.when(first)
    def _():
        ref[...] = val

    @pl.when(jnp.logical_not(first))
    def _():
        ref[...] += val


def _mesh_pos():
    return lax.axis_index("x"), lax.axis_index("y"), lax.axis_index("c")


def _dev_index(d):
    return 4 * d[0] + 2 * d[1] + d[2]


class _Comm:
    def __init__(self, inputs, out_shape, scratch, phases):
        self.inputs, self.out_shape, self.scratch, self.phases = inputs, out_shape, scratch, phases

    def specs(self):
        any_spec = pl.BlockSpec(memory_space=pl.ANY)
        return [any_spec] * len(self.inputs), [any_spec] * len(self.out_shape)


def _run_alone(comm, name):
    n_in, n_out = len(comm.inputs), len(comm.out_shape)

    def body(*refs):
        for phase in comm.phases:
            phase(refs[:n_in], refs[n_in:n_in + n_out], refs[n_in + n_out:])

    in_specs, out_specs = comm.specs()
    return pl.pallas_call(body, name=name, out_shape=comm.out_shape, in_specs=in_specs, out_specs=out_specs,
                          scratch_shapes=comm.scratch)(*comm.inputs)


def _run_alone_busy(comm, shards, positions, name):
    n_in, n_out = len(comm.inputs), len(comm.out_shape)
    offs = [int(o) for o in np.cumsum([0] + [s.shape[0] for s in shards])]
    t = positions.shape[0]
    half = HEAD_DIM // 2
    inv_freq = 1.0 / (ROPE_THETA ** (jnp.arange(0, HEAD_DIM, 2, dtype=F32) / HEAD_DIM))
    lanes = jnp.stack([jnp.tile(inv_freq, LANES // half),
                       jnp.tile(jnp.concatenate([-jnp.ones(half, F32), jnp.ones(half, F32)]), LANES // HEAD_DIM)])

    def body(*refs):
        c_in, s_refs, (pos_ref, lanes_ref), c_out, (stack_ref, cos_ref, sin_ref), scr = _split(refs, n_in, len(shards), 2, n_out, 3)
        comm.phases[0](c_in, c_out, scr)
        for s_ref, lo, hi in zip(s_refs, offs[:-1], offs[1:]):
            stack_ref[lo:hi, :] = s_ref[...].astype(BF16)
        ang = pos_ref[...] * lanes_ref[0:1, :]
        cos_ref[...] = jnp.cos(ang)
        sin_ref[...] = jnp.sin(ang) * lanes_ref[1:2, :]
        for phase in comm.phases[1:]:
            phase(c_in, c_out, scr)

    in_specs, out_specs = comm.specs()
    vmem = pl.BlockSpec(memory_space=pltpu.VMEM)
    table = jax.ShapeDtypeStruct((t, LANES), F32)
    res = pl.pallas_call(
        body, name=name, out_shape=comm.out_shape + [jax.ShapeDtypeStruct((offs[-1], shards[0].shape[1]), BF16), table, table],
        in_specs=in_specs + [vmem] * (len(shards) + 2), out_specs=out_specs + [vmem] * 3, scratch_shapes=comm.scratch,
        compiler_params=_cp())(*comm.inputs, *shards, positions, lanes)
    return res[:n_out], res[n_out], res[n_out + 1], res[n_out + 2]


def _run_hosted(comm, at, step, ins, outs, scr):
    for phase, when in zip(comm.phases, at):
        @pl.when(step == when)
        def _(phase=phase):
            phase(ins, outs, scr)


def _split(refs, *counts):
    parts, o = [], 0
    for n in counts:
        parts.append(refs[o:o + n])
        o += n
    return parts + [refs[o:]]


def _gather_plan(packed, rows_list, stages=1):
    n_w = len(rows_list)
    offs = [int(o) for o in np.cumsum([0] + list(rows_list))]
    d = packed.shape[1]
    per = n_w // stages
    groups = [range(s * per, (s + 1) * per) for s in range(stages)]

    def tools(ins, outs, scr):
        p_ref = ins[0]
        send_sems, recv_sems, local_sem = scr
        x, y, c = _mesh_pos()
        me, sibling = (x, y, c), (x, y, 1 - c)
        chips = [(1 - x, y), (x, 1 - y), (1 - x, 1 - y)]

        def rows(w, dev):
            start = pl.multiple_of(_dev_index(dev) * rows_list[w], 16)
            return outs[w].at[pl.ds(start, rows_list[w]), :]

        def mine(w):
            return p_ref.at[pl.ds(offs[w], rows_list[w]), :]

        def copy(k, s, w, block, to, own):
            return pltpu.make_async_remote_copy(
                src_ref=mine(w) if own else rows(w, block), dst_ref=rows(w, block),
                send_sem=send_sems.at[k, s], recv_sem=recv_sems.at[k, s], device_id=to, device_id_type=MESH_T)

        def whole_group(k, s):
            span = p_ref.at[pl.ds(offs[groups[s][0]], offs[groups[s][-1] + 1] - offs[groups[s][0]]), :]
            return pltpu.make_async_remote_copy(
                src_ref=span, dst_ref=span, send_sem=send_sems.at[k, s], recv_sem=recv_sems.at[k, s],
                device_id=me, device_id_type=MESH_T)

        return p_ref, local_sem, me, sibling, chips, c, rows, mine, copy, whole_group

    def start(ins, outs, scr):
        _, local_sem, me, sibling, chips, c, rows, mine, copy, _ = tools(ins, outs, scr)
        for w in range(n_w):
            pltpu.make_async_copy(mine(w), rows(w, me), local_sem).start()
        for s, group in enumerate(groups):
            for w in group:
                copy(0, s, w, me, sibling, True).start()
            for j, chip in enumerate(chips[:2]):
                for w in group:
                    copy(1 + j, s, w, me, (*chip, c), True).start()

    def forward(ins, outs, scr):
        _, _, (x, y, _), sibling, chips, c, _, _, copy, whole_group = tools(ins, outs, scr)
        south = c == 0
        source = (jnp.where(south, x, 1 - x), jnp.where(south, 1 - y, y), c)
        target = (jnp.where(south, 1 - x, x), jnp.where(south, y, 1 - y), c)
        for s, group in enumerate(groups):
            for j, chip in enumerate(chips[:2]):
                whole_group(1 + j, s).wait_recv()
            for w in group:
                copy(3, s, w, source, target, False).start()
            for j, chip in enumerate(chips[:2]):
                for w in group:
                    copy(4 + j, s, w, (*chip, c), sibling, False).start()

    def relay(ins, outs, scr):
        _, _, _, sibling, chips, c, _, _, copy, whole_group = tools(ins, outs, scr)
        for s, group in enumerate(groups):
            whole_group(3, s).wait_recv()
            for w in group:
                copy(6, s, w, (*chips[2], c), sibling, False).start()

    def finish(ins, outs, scr):
        p_ref, local_sem, _, _, _, _, _, _, _, whole_group = tools(ins, outs, scr)
        for s in range(stages):
            whole_group(0, s).wait_recv()
            for j in range(3):
                whole_group(4 + j, s).wait_recv()
            for k in range(7):
                whole_group(k, s).wait_send()
        pltpu.make_async_copy(p_ref, p_ref, local_sem).wait()

    return _Comm(
        [packed], [jax.ShapeDtypeStruct((N_DEV * r, d), packed.dtype) for r in rows_list],
        [pltpu.SemaphoreType.DMA((7, stages)), pltpu.SemaphoreType.DMA((7, stages)), pltpu.SemaphoreType.DMA],
        [start, forward, relay, finish])


def _exchange_plan(groups):
    flat = [a for g in groups for a in g]
    n_g = len(groups)
    sizes = [len(g) for g in groups]
    rows = [g[0].shape[0] // N_DEV for g in groups]
    first = [int(o) for o in np.cumsum([0] + sizes[:-1])]

    def start(srcs, lands, scr):
        send_sems, recv_sems, local_sems = scr
        x, y, c = _mesh_pos()
        me = (x, y, c)
        me_idx = _dev_index(me)

        def block(g, i, dev):
            start_row = pl.multiple_of(_dev_index(dev) * rows[g], 8)
            return srcs[first[g] + i].at[pl.ds(start_row, rows[g]), :]

        def slot(g, i):
            return lands[g].at[me_idx, pl.ds(i * rows[g], rows[g]), :]

        for g in range(n_g):
            for i in range(sizes[g]):
                pltpu.make_async_copy(block(g, i, me), slot(g, i), local_sems.at[g]).start()
        flips = [f for f in itertools.product((0, 1), repeat=3) if any(f)]
        for k, (fx, fy, fc) in enumerate(flips):
            peer = (1 - x if fx else x, 1 - y if fy else y, 1 - c if fc else c)
            for g in range(n_g):
                for i in range(sizes[g]):
                    pltpu.make_async_remote_copy(
                        src_ref=block(g, i, peer), dst_ref=slot(g, i), send_sem=send_sems.at[g, k],
                        recv_sem=recv_sems.at[g, k], device_id=peer, device_id_type=MESH_T).start()

    def finish(srcs, lands, scr):
        send_sems, recv_sems, local_sems = scr
        me = _mesh_pos()
        for k in range(7):
            for g in range(n_g):
                pltpu.make_async_remote_copy(
                    src_ref=lands[g].at[0], dst_ref=lands[g].at[0], send_sem=send_sems.at[g, k],
                    recv_sem=recv_sems.at[g, k], device_id=me, device_id_type=MESH_T).wait()
        for g in range(n_g):
            pltpu.make_async_copy(lands[g].at[0], lands[g].at[0], local_sems.at[g]).wait()

    return _Comm(
        flat, [jax.ShapeDtypeStruct((N_DEV, sizes[g] * rows[g], groups[g][0].shape[1]), groups[g][0].dtype) for g in range(n_g)],
        [pltpu.SemaphoreType.DMA((n_g, 7)), pltpu.SemaphoreType.DMA((n_g, 7)), pltpu.SemaphoreType.DMA((n_g,))],
        [start, finish])


def _comm_parts(comm):
    if comm is None:
        return 0, 0, [], [], [], [], []
    in_specs, out_specs = comm.specs()
    return len(comm.inputs), len(comm.out_shape), in_specs, out_specs, comm.out_shape, comm.scratch, comm.inputs


def _ffn_fwd(x, gn, wg_t, wu_t, wd, name, tm, tf, comm=None, comm_at=None, loss_head=None):
    t, d = x.shape
    f_all = wg_t.shape[0]
    n_f = f_all // tf
    n_ci, n_co, c_in_specs, c_out_specs, c_shapes, c_scratch, c_inputs = _comm_parts(comm)
    n_head, n_y = (2, 4) if loss_head else (0, 1)

    def body(*refs):
        ((x_ref, gn_ref, wg_ref, wu_ref, wd_ref), head_in, c_in, y_outs, (h_ref, gp_ref, up_ref, hid_ref), c_out,
         (h_s, hid_s), c_scr) = _split(refs, 5, n_head, n_ci, n_y, 4, n_co, 2)
        f = pl.program_id(1)
        if comm:
            _run_hosted(comm, comm_at, pl.program_id(0) * n_f + f, c_in, c_out, c_scr)

        @pl.when(f == 0)
        def _():
            xv = x_ref[...]
            h = ((xv * _rstd(xv)) * gn_ref[...]).astype(BF16)
            h_s[...] = h
            h_ref[...] = h

        h = h_s[...]
        cols = pl.ds(pl.multiple_of(f * tf, tf), tf)
        g = _dot_nt(h, wg_ref[cols, :])
        u = _dot_nt(h, wu_ref[cols, :])
        sg = jax.nn.sigmoid(g)
        silu = g * sg
        gp_ref[...] = (u * (sg * (1.0 + g * (1.0 - sg)))).astype(BF16)
        up_ref[...] = silu.astype(BF16)
        hid = (silu * u).astype(BF16)
        hid_ref[...] = hid
        for f0 in range(n_f):
            @pl.when(f == f0)
            def _(f0=f0):
                hid_s[:, f0 * tf:(f0 + 1) * tf] = hid

        @pl.when(f == n_f - 1)
        def _():
            y = x_ref[...] + FFN_RES_WEIGHT * _dot_nn(hid_s[...], wd_ref[...])
            if not loss_head:
                y_outs[0][...] = y
            else:
                (gf_ref, tg_ref), (dy_ref, dgf_ref, sq_ref, half_ref) = head_in, y_outs
                gfv, r, first = gf_ref[...], _rstd(y), pl.program_id(0) == 0
                diff = (y * r) * gfv - tg_ref[...]
                dy, dgf = _norm_bwd(diff * (1.0 / d), y, r, gfv)
                dy_ref[...] = dy
                half_ref[...] = (FFN_RES_WEIGHT * dy).astype(BF16)
                _accumulate(dgf_ref, dgf, first)
                _accumulate(sq_ref, jnp.sum(diff * diff, axis=0, keepdims=True), first)

    tok = pl.BlockSpec((tm, d), lambda i, f: (i, 0))
    row = pl.BlockSpec((1, d), lambda i, f: (0, 0))
    whole = pl.BlockSpec((f_all, d), lambda i, f: (0, 0), pipeline_mode=pl.Buffered(1))
    act = pl.BlockSpec((tm, tf), lambda i, f: (i, f))
    act_shape = jax.ShapeDtypeStruct((t, f_all), BF16)
    tok_f32, tok_bf16, row_f32 = (jax.ShapeDtypeStruct((t, d), F32), jax.ShapeDtypeStruct((t, d), BF16),
                                  jax.ShapeDtypeStruct((1, d), F32))
    y_shapes, y_specs = ([tok_f32, row_f32, row_f32, tok_bf16], [tok, row, row, tok]) if loss_head else ([tok_f32], [tok])
    res = pl.pallas_call(
        body, name=name, grid=(t // tm, n_f),
        out_shape=y_shapes + [tok_bf16, act_shape, act_shape, act_shape] + c_shapes,
        in_specs=[tok, row, whole, whole, whole] + ([row, tok] if loss_head else []) + c_in_specs,
        out_specs=y_specs + [tok, act, act, act] + c_out_specs,
        scratch_shapes=[pltpu.VMEM((tm, d), BF16), pltpu.VMEM((tm, f_all), BF16)] + c_scratch,
        compiler_params=_cp(("arbitrary", "arbitrary")),
    )(x, gn, wg_t, wu_t, wd, *(loss_head or ()), *c_inputs)
    return res[:n_y + 4], res[n_y + 4:]


def _ffn_bwd_hidden(dout, gp, up, hid, wd, name, tt, tf, comm=None, comm_at=None):
    t, d = dout.shape
    f_all = wd.shape[0]
    n_t = t // tt
    n_ci, n_co, c_in_specs, c_out_specs, c_shapes, c_scratch, c_inputs = _comm_parts(comm)

    def body(*refs):
        (dout_ref, gp_ref, up_ref, hid_ref, wd_ref), c_in, (dg_ref, du_ref, gwd_ref), c_out, (acc,), c_scr = _split(
            refs, 5, n_ci, 3, n_co, 1)
        s = pl.program_id(1)
        if comm:
            _run_hosted(comm, comm_at, pl.program_id(0) * n_t + s, c_in, c_out, c_scr)
        doutv = dout_ref[...]
        dhid = _dot_nt(doutv, wd_ref[...])
        dg_ref[...] = (dhid * gp_ref[...].astype(F32)).astype(BF16)
        du_ref[...] = (dhid * up_ref[...].astype(F32)).astype(BF16)
        _accumulate(acc, _dot_tn(hid_ref[...], doutv), s == 0)

        @pl.when(s == n_t - 1)
        def _():
            gwd_ref[...] = acc[...].astype(BF16)

    tok = pl.BlockSpec((tt, d), lambda f, s: (s, 0))
    wblk = pl.BlockSpec((tf, d), lambda f, s: (f, 0))
    act = pl.BlockSpec((tt, tf), lambda f, s: (s, f))
    act_shape = jax.ShapeDtypeStruct((t, f_all), BF16)
    res = pl.pallas_call(
        body, name=name, grid=(f_all // tf, n_t),
        out_shape=[act_shape, act_shape, jax.ShapeDtypeStruct((f_all, d), BF16)] + c_shapes,
        in_specs=[tok, act, act, act, wblk] + c_in_specs, out_specs=[act, act, wblk] + c_out_specs,
        scratch_shapes=[pltpu.VMEM((tf, d), F32)] + c_scratch,
        compiler_params=_cp(("arbitrary", "arbitrary")),
    )(dout, gp, up, hid, wd, *c_inputs)
    return res[:3], res[3:]


def _ffn_bwd_input(dy, x, gn, dg, du, wg_t, wu_t, name, tm, comm=None, comm_at=None):
    t, d = x.shape
    f_all = wg_t.shape[0]
    n_ci, n_co, c_in_specs, c_out_specs, c_shapes, c_scratch, c_inputs = _comm_parts(comm)

    def body(*refs):
        (dy_ref, x_ref, gn_ref, dg_ref, du_ref, wg_ref, wu_ref), c_in, (dx_ref, dgn_ref), c_out, c_scr = _split(
            refs, 7, n_ci, 2, n_co)
        i = pl.program_id(0)
        if comm:
            _run_hosted(comm, comm_at, i, c_in, c_out, c_scr)
        dh = _dot_nn(dg_ref[...], wg_ref[...]) + _dot_nn(du_ref[...], wu_ref[...])
        xv = x_ref[...]
        dxn, dgn = _norm_bwd(dh, xv, _rstd(xv), gn_ref[...])
        dx_ref[...] = dy_ref[...] + dxn
        _accumulate(dgn_ref, dgn, i == 0)

    tok = pl.BlockSpec((tm, d), lambda i: (i, 0))
    row = pl.BlockSpec((1, d), lambda i: (0, 0))
    act = pl.BlockSpec((tm, f_all), lambda i: (i, 0))
    whole = pl.BlockSpec((f_all, d), lambda i: (0, 0))
    res = pl.pallas_call(
        body, name=name, grid=(t // tm,),
        out_shape=[jax.ShapeDtypeStruct((t, d), F32), jax.ShapeDtypeStruct((1, d), F32)] + c_shapes,
        in_specs=[tok, tok, row, act, act, whole, whole] + c_in_specs,
        out_specs=[tok, row] + c_out_specs,
        scratch_shapes=c_scratch,
        compiler_params=_cp(("arbitrary",)),
    )(dy, x, gn, dg, du, wg_t, wu_t, *c_inputs)
    return res[:2], res[2:]


def _token_products(lhs_list, rhs, name, tt, tf, comm=None, comm_at=None):
    n_l = len(lhs_list)
    t, f_all = lhs_list[0].shape
    d = rhs.shape[1]
    n_t = t // tt
    n_ci, n_co, c_in_specs, c_out_specs, c_shapes, c_scratch, c_inputs = _comm_parts(comm)

    def body(*refs):
        lhs_refs, (rhs_ref,), c_in, out_refs, c_out, accs, c_scr = _split(refs, n_l, 1, n_ci, n_l, n_co, n_l)
        s = pl.program_id(1)
        if comm:
            _run_hosted(comm, comm_at, pl.program_id(0) * n_t + s, c_in, c_out, c_scr)
        rv = rhs_ref[...]
        for l_ref, acc in zip(lhs_refs, accs):
            _accumulate(acc, _dot_tn(l_ref[...], rv), s == 0)

        @pl.when(s == n_t - 1)
        def _():
            for o_ref, acc in zip(out_refs, accs):
                o_ref[...] = acc[...].astype(BF16)

    act = pl.BlockSpec((tt, tf), lambda f, s: (s, f))
    tok = pl.BlockSpec((tt, d), lambda f, s: (s, 0))
    wblk = pl.BlockSpec((tf, d), lambda f, s: (f, 0))
    res = pl.pallas_call(
        body, name=name, grid=(f_all // tf, n_t),
        out_shape=[jax.ShapeDtypeStruct((f_all, d), BF16)] * n_l + c_shapes,
        in_specs=[act] * n_l + [tok] + c_in_specs, out_specs=[wblk] * n_l + c_out_specs,
        scratch_shapes=[pltpu.VMEM((tf, d), F32)] * n_l + c_scratch,
        compiler_params=_cp(("arbitrary", "arbitrary")),
    )(*lhs_list, rhs, *c_inputs)
    return res[:n_l], res[n_l:]


def _swap_halves(t):
    w = t.shape[-1]
    lane = lax.broadcasted_iota(jnp.int32, (1, w), 1)
    return jnp.where((lane % HEAD_DIM) < HEAD_DIM // 2, pltpu.roll(t, w - HEAD_DIM // 2, 1), pltpu.roll(t, HEAD_DIM // 2, 1))


def _rope(t, cos, sin_signed):
    reps = t.shape[-1] // LANES
    return t * jnp.tile(cos, (1, reps)) + _swap_halves(t) * jnp.tile(sin_signed, (1, reps))


def _rope_bwd(dt, cos, sin_signed):
    reps = dt.shape[-1] // LANES
    return dt * jnp.tile(cos, (1, reps)) + _swap_halves(dt * jnp.tile(sin_signed, (1, reps)))


DILATIONS = tuple(dil for _, dil in B_PATTERNS if dil > 1)


def _seg_shape(t, dil, w, dtype):
    return jax.ShapeDtypeStruct((dil, t // dil, w), dtype)


def _seg_spec(tm, dil, w):
    return pl.BlockSpec((dil, tm // dil, w), lambda i: (0, i, 0))


def _tile_scratch(tm, w):
    return [pltpu.VMEM((tm, LANES), F32)] * (w // LANES)


def _put_tile(tile, val):
    for c, ref in enumerate(tile):
        ref[...] = val[:, c * LANES:(c + 1) * LANES]


def _get_tile(tile):
    return jnp.concatenate([ref[...] for ref in tile], axis=1)


def _scatter_to_segments(tile, seg_refs):
    for seg_ref, dil in zip(seg_refs, DILATIONS):
        rows = tile[0].shape[0] // dil
        for r in range(dil):
            for c, ref in enumerate(tile):
                seg_ref[r, :, c * LANES:(c + 1) * LANES] = ref[pl.ds(r, rows, stride=dil), :].astype(seg_ref.dtype)


def _gather_from_segments(seg_ref, dil, tile, add=False):
    rows = tile[0].shape[0] // dil
    for r in range(dil):
        for c, ref in enumerate(tile):
            idx = (pl.ds(r, rows, stride=dil), slice(None))
            v = seg_ref[r, :, c * LANES:(c + 1) * LANES].astype(F32)
            ref[idx] = ref[idx] + v if add else v


def _in_proj_fwd(x, gn, win_t, cos, sin_signed, name, tm):
    t, d = x.shape
    in_w = win_t.shape[0]
    n_dil = len(DILATIONS)

    def body(x_ref, gn_ref, w_ref, cos_ref, sin_ref, h_ref, aq_ref, akx_ref, avx_ref, bq_ref, bk_ref, bv_ref, *rest):
        seg_refs, tile = rest[:3 * n_dil], rest[3 * n_dil:]
        xv = x_ref[...]
        h = ((xv * _rstd(xv)) * gn_ref[...]).astype(BF16)
        h_ref[...] = h
        p = _dot_nt(h, w_ref[...])
        cs, sn = cos_ref[...], sin_ref[...]
        o = 0
        aq_ref[...] = (_rope(p[:, o:o + A_Q_W], cs, sn) * QK_SCALE).astype(BF16)
        o += A_Q_W
        ak = _rope(p[:, o:o + A_KV_W], cs, sn)
        o += A_KV_W
        av = p[:, o:o + A_KV_W]
        o += A_KV_W
        low = lax.broadcasted_iota(jnp.int32, (1, LANES), 1) < HEAD_DIM
        for src, dst in ((ak, akx_ref), (av, avx_ref)):
            other = pltpu.roll(src, HEAD_DIM, 1)
            dst[0] = jnp.where(low, src, other).astype(BF16)
            dst[1] = jnp.where(low, other, src).astype(BF16)
        for k, nat_ref in enumerate((bq_ref, bk_ref, bv_ref)):
            val = p[:, o:o + B_W]
            o += B_W
            if k < 2:
                val = _rope(val, cs, sn)
            if k == 0:
                val = val * QK_SCALE
            nat_ref[...] = val.astype(BF16)
            _put_tile(tile, val)
            _scatter_to_segments(tile, seg_refs[k * n_dil:(k + 1) * n_dil])

    tok = lambda w: pl.BlockSpec((tm, w), lambda i: (i, 0))
    kvx = pl.BlockSpec((2, tm, LANES), lambda i: (0, i, 0))
    sd = lambda *s: jax.ShapeDtypeStruct(s, BF16)
    res = pl.pallas_call(
        body, name=name, grid=(t // tm,),
        out_shape=[sd(t, d), sd(t, A_Q_W), sd(2, t, LANES), sd(2, t, LANES), sd(t, B_W), sd(t, B_W), sd(t, B_W)]
        + [_seg_shape(t, dil, B_W, BF16) for _ in range(3) for dil in DILATIONS],
        in_specs=[tok(d), pl.BlockSpec((1, d), lambda i: (0, 0)), pl.BlockSpec((in_w, d), lambda i: (0, 0)),
                  tok(LANES), tok(LANES)],
        out_specs=[tok(d), tok(A_Q_W), kvx, kvx, tok(B_W), tok(B_W), tok(B_W)]
        + [_seg_spec(tm, dil, B_W) for _ in range(3) for dil in DILATIONS],
        scratch_shapes=_tile_scratch(tm, B_W),
        compiler_params=_cp(("arbitrary",)),
    )(x, gn, win_t, cos, sin_signed)
    return res[:7], [res[7 + k * n_dil:7 + (k + 1) * n_dil] for k in range(3)]


def _in_proj_bwd(dres, x, gn, win_t, h, cos, sin_signed, daq, dakx, davx, per_pattern, name, tm):
    t, d = x.shape
    in_w = win_t.shape[0]
    n_t = t // tm
    chunk = in_w // 3
    n_pat = 3 * (1 + len(DILATIONS))

    def body(*refs):
        ((dres_ref, x_ref, gn_ref, w_ref, h_ref, cos_ref, sin_ref, daq_ref, dakx_ref, davx_ref), pat,
         (dx_ref, dgn_ref, gw_ref, half_ref), (dp_s, gw_s), tile) = _split(refs, 10, n_pat, 4, 2)
        i = pl.program_id(0)
        cs, sn = cos_ref[...], sin_ref[...]
        low = lax.broadcasted_iota(jnp.int32, (1, LANES), 1) < HEAD_DIM

        def fold(ref):
            a, b = ref[0].astype(F32), ref[1].astype(F32)
            return jnp.where(low, a + pltpu.roll(a, HEAD_DIM, 1), b + pltpu.roll(b, HEAD_DIM, 1))

        def summed(j):
            _put_tile(tile, pat[j][...].astype(F32))
            for k, dil in enumerate(DILATIONS):
                _gather_from_segments(pat[3 * (1 + k) + j], dil, tile, add=True)
            return _get_tile(tile)

        o = 0
        dp_s[:, o:o + A_Q_W] = _rope_bwd(daq_ref[...].astype(F32) * QK_SCALE, cs, sn).astype(BF16)
        o += A_Q_W
        dp_s[:, o:o + A_KV_W] = _rope_bwd(fold(dakx_ref), cs, sn).astype(BF16)
        o += A_KV_W
        dp_s[:, o:o + A_KV_W] = fold(davx_ref).astype(BF16)
        o += A_KV_W
        dp_s[:, o:o + B_W] = _rope_bwd(summed(0) * QK_SCALE, cs, sn).astype(BF16)
        o += B_W
        dp_s[:, o:o + B_W] = _rope_bwd(summed(1), cs, sn).astype(BF16)
        o += B_W
        dp_s[:, o:o + B_W] = summed(2).astype(BF16)
        dh = _dot_nn(dp_s[...], w_ref[...])
        hv = h_ref[...]
        for c0 in range(0, in_w, chunk):
            _accumulate(gw_s.at[pl.ds(c0, chunk), :], _dot_tn(dp_s[:, c0:c0 + chunk], hv), i == 0)
        xv = x_ref[...]
        dxn, dgn = _norm_bwd(dh, xv, _rstd(xv), gn_ref[...])
        dx = dres_ref[...] + dxn
        dx_ref[...] = dx
        half_ref[...] = (FFN_RES_WEIGHT * dx).astype(BF16)
        _accumulate(dgn_ref, dgn, i == 0)

        @pl.when(i == n_t - 1)
        def _():
            gw_ref[...] = gw_s[...].astype(BF16)

    tok = lambda w: pl.BlockSpec((tm, w), lambda i: (i, 0))
    row = pl.BlockSpec((1, d), lambda i: (0, 0))
    whole = pl.BlockSpec((in_w, d), lambda i: (0, 0))
    weight = pl.BlockSpec((in_w, d), lambda i: (0, 0), pipeline_mode=pl.Buffered(1))
    kvx = pl.BlockSpec((2, tm, LANES), lambda i: (0, i, 0))
    flat = [a if k == 0 else a.reshape(DILATIONS[k - 1], -1, B_W) for k, grads in enumerate(per_pattern) for a in grads]
    return pl.pallas_call(
        body, name=name, grid=(n_t,),
        out_shape=[jax.ShapeDtypeStruct((t, d), F32), jax.ShapeDtypeStruct((1, d), F32),
                   jax.ShapeDtypeStruct((in_w, d), BF16), jax.ShapeDtypeStruct((t, d), BF16)],
        in_specs=[tok(d), tok(d), row, weight, tok(d), tok(LANES), tok(LANES), tok(A_Q_W), kvx, kvx]
        + [tok(B_W)] * 3 + [_seg_spec(tm, dil, B_W) for dil in DILATIONS for _ in range(3)],
        out_specs=[tok(d), row, whole, tok(d)],
        scratch_shapes=[pltpu.VMEM((tm, in_w), BF16), pltpu.VMEM((in_w, d), F32)] + _tile_scratch(tm, B_W),
        compiler_params=_cp(("arbitrary",)),
    )(dres, x, gn, win_t, h, cos, sin_signed, daq, dakx, davx, *flat)


def _merge_out_proj_fwd(x, a_out, outs, lses, wout, name, tm):
    t, d = x.shape
    n_dil = len(DILATIONS)

    def body(x_ref, a_ref, *rest):
        o_refs, l_refs, (w_ref, y_ref, b_ref, lt_ref), b_segs, lt_segs, scratch = _split(rest, 1 + n_dil, 1 + n_dil, 4, n_dil, n_dil)
        n_c = B_W // LANES
        tiles = [scratch[j * n_c:(j + 1) * n_c] for j in range(2 * n_dil)]
        os_, ls = [o_refs[0][...].astype(F32)], [l_refs[0][...]]
        for k, dil in enumerate(DILATIONS):
            _gather_from_segments(o_refs[1 + k], dil, tiles[2 * k])
            _gather_from_segments(l_refs[1 + k], dil, tiles[2 * k + 1])
            os_.append(_get_tile(tiles[2 * k]))
            ls.append(_get_tile(tiles[2 * k + 1]))
        mx = functools.reduce(jnp.maximum, ls)
        es = [jnp.exp(l - mx) for l in ls]
        den = functools.reduce(jnp.add, es)
        b = functools.reduce(jnp.add, [e * o for e, o in zip(es, os_)]) / den
        lt = mx + jnp.log(den)
        bb = b.astype(BF16)
        b_ref[...] = bb
        lt_ref[...] = lt
        y_ref[...] = x_ref[...] + _dot_nn(a_ref[...], w_ref[0:A_Q_W, :]) + _dot_nn(bb, w_ref[A_Q_W:A_Q_W + B_W, :])
        _put_tile(tiles[0], b)
        _scatter_to_segments(tiles[0], b_segs)
        _put_tile(tiles[1], lt)
        _scatter_to_segments(tiles[1], lt_segs)

    tok = lambda w: pl.BlockSpec((tm, w), lambda i: (i, 0))
    segs = [_seg_spec(tm, dil, B_W) for dil in DILATIONS]
    res = pl.pallas_call(
        body, name=name, grid=(t // tm,),
        out_shape=[jax.ShapeDtypeStruct((t, d), F32), jax.ShapeDtypeStruct((t, B_W), BF16), jax.ShapeDtypeStruct((t, B_W), F32)]
        + [_seg_shape(t, dil, B_W, BF16) for dil in DILATIONS] + [_seg_shape(t, dil, B_W, F32) for dil in DILATIONS],
        in_specs=[tok(d), tok(A_Q_W)] + ([tok(B_W)] + segs) * 2 + [pl.BlockSpec(wout.shape, lambda i: (0, 0))],
        out_specs=[tok(d), tok(B_W), tok(B_W)] + segs * 2,
        scratch_shapes=_tile_scratch(tm, B_W) * (2 * n_dil),
        compiler_params=_cp(("arbitrary",)),
    )(x, a_out, *outs, *lses, wout)
    return res[0], [res[1]] + list(res[3:3 + n_dil]), [res[2]] + list(res[3 + n_dil:])


def _out_proj_bwd(dy, a_out, b_out, wout, name, tm):
    t, d = dy.shape
    n_t = t // tm
    n_dil = len(DILATIONS)

    def body(dy_ref, a_ref, b_ref, w_ref, da_ref, db_ref, gw_ref, *rest):
        db_segs, gw_s, tile = rest[:n_dil], rest[n_dil], rest[n_dil + 1:]
        i = pl.program_id(0)
        dyb = dy_ref[...].astype(BF16)
        da_ref[...] = _dot_nt(dyb, w_ref[0:A_Q_W, :]).astype(BF16)
        db = _dot_nt(dyb, w_ref[A_Q_W:A_Q_W + B_W, :])
        db_ref[...] = db.astype(BF16)
        _put_tile(tile, db)
        _scatter_to_segments(tile, db_segs)
        ga = _dot_tn(a_ref[...], dyb)
        gb = _dot_tn(b_ref[...], dyb)

        @pl.when(i == 0)
        def _():
            gw_s[0:A_Q_W, :] = ga
            gw_s[A_Q_W:A_Q_W + B_W, :] = gb

        @pl.when(i > 0)
        def _():
            gw_s[0:A_Q_W, :] += ga
            gw_s[A_Q_W:A_Q_W + B_W, :] += gb

        @pl.when(i == n_t - 1)
        def _():
            gw_ref[...] = gw_s[...].astype(BF16)

    tok = lambda w: pl.BlockSpec((tm, w), lambda i: (i, 0))
    whole = pl.BlockSpec(wout.shape, lambda i: (0, 0))
    res = pl.pallas_call(
        body, name=name, grid=(n_t,),
        out_shape=[jax.ShapeDtypeStruct((t, A_Q_W), BF16), jax.ShapeDtypeStruct((t, B_W), BF16),
                   jax.ShapeDtypeStruct(wout.shape, BF16)] + [_seg_shape(t, dil, B_W, BF16) for dil in DILATIONS],
        in_specs=[tok(d), tok(A_Q_W), tok(B_W), whole],
        out_specs=[tok(A_Q_W), tok(B_W), whole] + [_seg_spec(tm, dil, B_W) for dil in DILATIONS],
        scratch_shapes=[pltpu.VMEM(wout.shape, F32)] + _tile_scratch(tm, B_W),
        compiler_params=_cp(("arbitrary",)),
    )(dy, a_out, b_out, wout)
    return res[0], [res[1]] + list(res[3:]), res[2]


SUB_ROWS = 64
WINDOW_ALIGN = 64


def _sub_band(r0, hw, win, seg_len, t, rel, col):
    ks = pl.multiple_of(jnp.clip(r0 - hw, 0, t - win), WINDOW_ALIGN)
    kpos = col + ks
    seg_lo = (r0 // seg_len) * seg_len
    valid = (jnp.abs(rel + (ks - r0)) <= hw) & (kpos >= seg_lo) & (kpos < seg_lo + seg_len)
    return ks, valid


def _split_heads(v, low):
    zero = jnp.zeros_like(v)
    return jnp.concatenate([jnp.where(low, v, zero), jnp.where(low, zero, v)], axis=0)


def _kv_spec(kv, t):
    if kv.ndim == 3:
        return pl.BlockSpec((None, t, LANES), lambda p, i: (p // 2, 0, 0))
    return pl.BlockSpec((t, LANES), lambda p, i: (0, p))


def _attn_fwd(q, k, v, sink, name, hw, seg_len, tq, has_sink):
    t, width = q.shape
    sb = min(SUB_ROWS, tq)
    win = 2 * hw + LANES

    def body(sink_ref, q_ref, k_ref, v_ref, o_ref, lse_ref):
        p, i = pl.program_id(0), pl.program_id(1)
        q0 = i * tq
        low = lax.broadcasted_iota(jnp.int32, (1, LANES), 1) < HEAD_DIM
        rel = lax.broadcasted_iota(jnp.int32, (sb, win), 1) - lax.broadcasted_iota(jnp.int32, (sb, win), 0)
        col = lax.broadcasted_iota(jnp.int32, (1, win), 1)
        subs = []
        for j in range(tq // sb):
            rows = pl.ds(j * sb, sb)
            ks, valid = _sub_band(q0 + j * sb, hw, win, seg_len, t, rel, col)
            subs.append((rows, ks, valid, _dot_nt(_split_heads(q_ref[rows, :], low), k_ref[pl.ds(ks, win), :])))
        for rows, ks, valid, s in subs:
            vw = v_ref[pl.ds(ks, win), :]
            es, inv, lses = [], [], []
            for a in range(2):
                sa = jnp.where(valid, s[a * sb:(a + 1) * sb], NEG)
                m = jnp.max(sa, axis=1, keepdims=True)
                if has_sink:
                    sk = sink_ref[2 * p + a]
                    m = jnp.maximum(m, sk)
                e = jnp.exp(sa - m)
                den = jnp.sum(e, axis=1, keepdims=True)
                if has_sink:
                    den = den + jnp.exp(sk - m)
                es.append(e.astype(BF16))
                inv.append(1.0 / den)
                lses.append(m + jnp.log(den))
            pv = _dot_nn(jnp.concatenate(es, axis=0), vw)
            o_ref[rows, :] = jnp.where(low, pv[0:sb] * inv[0], pv[sb:2 * sb] * inv[1]).astype(BF16)
            lse_ref[rows, :] = jnp.where(low, lses[0], lses[1])

    tile = pl.BlockSpec((tq, LANES), lambda p, i: (i, p))
    return pl.pallas_call(
        body, name=name, grid=(width // LANES, t // tq),
        out_shape=[jax.ShapeDtypeStruct((t, width), BF16), jax.ShapeDtypeStruct((t, width), F32)],
        in_specs=[pl.BlockSpec(memory_space=pltpu.SMEM), tile, _kv_spec(k, t), _kv_spec(v, t)],
        out_specs=[tile, tile],
        compiler_params=_cp(("arbitrary", "arbitrary")),
    )(sink, q, k, v)


def _attn_bwd(q, k, v, o, do, lse, sink, name, hw, seg_len, tq, has_sink, comm=None, comm_at=None):
    t, width = q.shape
    sb = min(SUB_ROWS, tq)
    win = 2 * hw + LANES
    n_q = t // tq
    shared_kv = k.ndim == 3
    n_ci, n_co, c_in_specs, c_out_specs, c_shapes, c_scratch, c_inputs = _comm_parts(comm)

    def body(*refs):
        ((sink_ref, q_ref, k_ref, v_ref, o_ref, do_ref, lse_ref), c_in, (dq_ref, dk_ref, dv_ref, ds_ref), c_out,
         (dk_s, dv_s), c_scr) = _split(refs, 7, n_ci, 4, n_co, 2)
        p, i = pl.program_id(0), pl.program_id(1)
        if comm:
            _run_hosted(comm, comm_at, p * n_q + i, c_in, c_out, c_scr)
        fresh = (i == 0) & (p % 2 == 0) if shared_kv else i == 0
        last = (i == n_q - 1) & (p % 2 == 1) if shared_kv else i == n_q - 1

        @pl.when(fresh)
        def _():
            dk_s[...] = jnp.zeros_like(dk_s)
            dv_s[...] = jnp.zeros_like(dv_s)

        q0 = i * tq
        low = lax.broadcasted_iota(jnp.int32, (1, LANES), 1) < HEAD_DIM
        rel = lax.broadcasted_iota(jnp.int32, (sb, win), 1) - lax.broadcasted_iota(jnp.int32, (sb, win), 0)
        col = lax.broadcasted_iota(jnp.int32, (1, win), 1)
        dsink = [jnp.zeros((1, 1), F32), jnp.zeros((1, 1), F32)]
        subs = []
        for j in range(tq // sb):
            rows = pl.ds(j * sb, sb)
            ks, valid = _sub_band(q0 + j * sb, hw, win, seg_len, t, rel, col)
            kw = k_ref[pl.ds(ks, win), :]
            dov = do_ref[rows, :]
            q2 = _split_heads(q_ref[rows, :], low)
            do2 = _split_heads(dov, low)
            subs.append((rows, ks, valid, kw, dov, q2, do2, _dot_nt(q2, kw), _dot_nt(do2, v_ref[pl.ds(ks, win), :])))
        probs = []
        for rows, ks, valid, kw, dov, q2, do2, s, dpr in subs:
            prod = dov.astype(F32) * o_ref[rows, :].astype(F32)
            lse_t = lse_ref[rows, :]
            prs, dss = [], []
            for a in range(2):
                mine = low if a == 0 else jnp.logical_not(low)
                lse_a = jnp.max(jnp.where(mine, lse_t, -jnp.inf), axis=1, keepdims=True)
                delta = jnp.sum(jnp.where(mine, prod, 0.0), axis=1, keepdims=True)
                pr = jnp.exp(jnp.where(valid, s[a * sb:(a + 1) * sb], NEG) - lse_a)
                prs.append(pr.astype(BF16))
                dss.append((pr * (dpr[a * sb:(a + 1) * sb] - delta)).astype(BF16))
                if has_sink:
                    dsink[a] = dsink[a] - jnp.sum(jnp.exp(sink_ref[2 * p + a] - lse_a) * delta, axis=0, keepdims=True)
            probs.append((jnp.concatenate(prs, axis=0), jnp.concatenate(dss, axis=0)))
        for (rows, ks, valid, kw, dov, q2, do2, s, dpr), (pr2, ds2) in zip(subs, probs):
            dv_s[pl.ds(ks, win), :] += _dot_tn(pr2, do2)
            dk_s[pl.ds(ks, win), :] += _dot_tn(ds2, q2)
            dq2 = _dot_nn(ds2, kw)
            dq_ref[rows, :] = jnp.where(low, dq2[0:sb], dq2[sb:2 * sb]).astype(BF16)
        ds_ref[...] = jnp.broadcast_to(jnp.where(low, dsink[0], dsink[1]), ds_ref.shape)

        @pl.when(last)
        def _():
            dk_ref[...] = dk_s[...].astype(BF16)
            dv_ref[...] = dv_s[...].astype(BF16)

    tile = pl.BlockSpec((tq, LANES), lambda p, i: (i, p))
    kv_shape = jax.ShapeDtypeStruct(k.shape, BF16)
    res = pl.pallas_call(
        body, name=name, grid=(width // LANES, n_q),
        out_shape=[jax.ShapeDtypeStruct((t, width), BF16), kv_shape, kv_shape,
                   jax.ShapeDtypeStruct((width // LANES, n_q, 8, LANES), F32)] + c_shapes,
        in_specs=[pl.BlockSpec(memory_space=pltpu.SMEM), tile, _kv_spec(k, t), _kv_spec(v, t), tile, tile, tile] + c_in_specs,
        out_specs=[tile, _kv_spec(k, t), _kv_spec(v, t), pl.BlockSpec((None, None, 8, LANES), lambda p, i: (p, i, 0, 0))] + c_out_specs,
        scratch_shapes=[pltpu.VMEM((t, LANES), F32)] * 2 + c_scratch,
        compiler_params=_cp(("arbitrary", "arbitrary")),
    )(sink, q, k, v, o, do, lse, *c_inputs)
    return res[:4], res[4:]


def _adamw(w, g, m, v, name):
    def body(w_ref, g_ref, m_ref, v_ref, d_ref, nm_ref, nv_ref):
        gv = g_ref[...]
        nm = ADAM_B1 * m_ref[...] + (1.0 - ADAM_B1) * gv
        nv = ADAM_B2 * v_ref[...] + (1.0 - ADAM_B2) * (gv * gv)
        m_hat = nm / (1.0 - ADAM_B1 ** ADAM_STEP)
        v_hat = nv / (1.0 - ADAM_B2 ** ADAM_STEP)
        d_ref[...] = -ADAM_LR * (m_hat / (jnp.sqrt(v_hat) + ADAM_EPS) + ADAM_WD * w_ref[...])
        nm_ref[...] = nm
        nv_ref[...] = nv

    shape = jax.ShapeDtypeStruct(w.shape, F32)
    return pl.pallas_call(body, name=name, out_shape=[shape, shape, shape], compiler_params=_cp())(w, g, m, v)


def _adamw_rows(land, j, rows, w, m, v, name):
    d = w.shape[1]

    def body(l_ref, w_ref, m_ref, v_ref, g_ref, d_ref, nm_ref, nv_ref):
        gv = l_ref[0].astype(F32)
        for s in range(1, N_DEV):
            gv = gv + l_ref[s].astype(F32)
        g_ref[...] = gv
        nm = ADAM_B1 * m_ref[...] + (1.0 - ADAM_B1) * gv
        nv = ADAM_B2 * v_ref[...] + (1.0 - ADAM_B2) * (gv * gv)
        m_hat = nm / (1.0 - ADAM_B1 ** ADAM_STEP)
        v_hat = nv / (1.0 - ADAM_B2 ** ADAM_STEP)
        d_ref[...] = -ADAM_LR * (m_hat / (jnp.sqrt(v_hat) + ADAM_EPS) + ADAM_WD * w_ref[...])
        nm_ref[...] = nm
        nv_ref[...] = nv

    halves = 2
    half = pl.BlockSpec((rows // halves, d), lambda i: (i, 0))
    shape = jax.ShapeDtypeStruct((rows, d), F32)
    return pl.pallas_call(
        body, name=name, grid=(halves,), out_shape=[shape] * 4,
        in_specs=[pl.BlockSpec((N_DEV, rows // halves, d), lambda i: (0, halves * j + i, 0)), half, half, half],
        out_specs=[half] * 4,
        compiler_params=_cp(("arbitrary",)),
    )(land, w, m, v)


def _sum_small(land, name):
    def body(l_ref, o_ref):
        acc = l_ref[0]
        for s in range(1, N_DEV):
            acc = acc + l_ref[s]
        o_ref[...] = acc

    return pl.pallas_call(body, name=name, out_shape=jax.ShapeDtypeStruct(land.shape[1:], F32), compiler_params=_cp())(land)


def kernel(x, positions, norm_ffn1, w_gate1, w_up1, w_down1, norm_mix, w_in, a_sink, w_out, norm_ffn2, w_gate2, w_up2, w_down2, norm_final, loss_target, m_norm_ffn1, m_w_gate1, m_w_up1, m_w_down1, m_norm_mix, m_w_in, m_a_sink, m_w_out, m_norm_ffn2, m_w_gate2, m_w_up2, m_w_down2, m_norm_final, v_norm_ffn1, v_w_gate1, v_w_up1, v_w_down1, v_norm_mix, v_w_in, v_a_sink, v_w_out, v_norm_ffn2, v_w_gate2, v_w_up2, v_w_down2, v_norm_final):
    x = x[0]
    target = loss_target[0]
    t, d = x.shape
    tm_mix = min(512, t)
    tq_of = lambda seg_len, most=2048: min(most, t, max(seg_len, 16 * SUB_ROWS))

    tt = min(1024, t)
    f_all = w_gate1.shape[2] * N_DEV
    tfp = f_all // 2
    tm_fwd, tf_fwd = tm_mix, tfp
    n_steps = (t // tm_fwd) * (f_all // tf_fwd)

    def stacked(shards):
        return jnp.concatenate([s.astype(BF16) for s in shards], axis=0), [s.shape[0] for s in shards]

    packed1, rows1 = stacked([w_gate1[0].T, w_up1[0].T, w_down1[0]])
    later = [w_in[0].T, w_out[0], w_gate2[0].T, w_up2[0].T, w_down2[0]]
    rows2 = [s.shape[0] for s in later]
    (wg1, wu1, wd1), packed2, cos, sin_signed = _run_alone_busy(
        _gather_plan(packed1, rows1, stages=len(rows1)), later, positions[0].astype(F32)[:, None], "gather_ffn1_weights")

    sink = a_sink[0]
    no_sink = jnp.zeros_like(sink)

    (x1, h1, gp1, up1, hid1), (win, wout, wg2, wu2, wd2) = _ffn_fwd(
        x, norm_ffn1, wg1, wu1, wd1, "ffn1_fwd", tm_fwd, tf_fwd, _gather_plan(packed2, rows2),
        [0, n_steps // 2, (3 * n_steps) // 4, n_steps - 1])
    (h_mix, aq, akx, avx, bq, bk, bv), seg_qkv = _in_proj_fwd(x1, norm_mix, win, cos, sin_signed, "in_proj_fwd", tm_mix)
    a_out, a_lse = _attn_fwd(aq, akx, avx, sink, "attn_a_fwd", A_HALF_WINDOW, t, tq_of(t), True)
    rows_of = lambda a: a.reshape(t, B_W)
    segments_of = lambda a, dil: a if dil == 1 else a.reshape(dil, t // dil, B_W)
    b_qkv, b_outs, b_lses = [], [], []
    for n, (window, dil) in enumerate(B_PATTERNS):
        qs, ks, vs = (bq, bk, bv) if dil == 1 else (rows_of(seg_qkv[k][n - 1]) for k in range(3))
        o_seg, lse_seg = _attn_fwd(qs, ks, vs, no_sink, f"attn_b{dil}_fwd", window // (2 * dil), t // dil,
                                  tq_of(t // dil, 16 * SUB_ROWS), False)
        b_qkv.append((qs, ks, vs))
        b_outs.append(segments_of(o_seg, dil))
        b_lses.append(segments_of(lse_seg, dil))
    x2, b_out, b_lse = _merge_out_proj_fwd(x1, a_out, b_outs, b_lses, wout, "out_proj_fwd", tm_mix)
    (dx3, dg_final, sq, dout2, h2, gp2, up2, hid2), _ = _ffn_fwd(
        x2, norm_ffn2, wg2, wu2, wd2, "ffn2_fwd_loss", tm_fwd, tf_fwd, loss_head=(norm_final.reshape(1, d), target))


    (dgt2, dut2, gwd2), _ = _ffn_bwd_hidden(dout2, gp2, up2, hid2, wd2, "ffn2_bwd_hidden", tm_mix, tfp)
    (dx2, dg_ffn2), _ = _ffn_bwd_input(dx3, x2, norm_ffn2, dgt2, dut2, wg2, wu2, "ffn2_bwd_input", tm_mix)
    (gwg2, gwu2), _ = _token_products([dgt2, dut2], h2, "ffn2_bwd_gate_up", tt, tfp)

    n_att = lambda width, tq: (width // LANES) * (t // tq)
    n_prod = (f_all // tfp) * (t // tt)
    da, db, gwout = _out_proj_bwd(dx2, a_out, b_out[0], wout, "out_proj_bwd", tm_mix)
    (daq, dakx, davx, dsink_parts), (land_wg2,) = _attn_bwd(
        aq, akx, avx, a_out, da, a_lse, sink, "attn_a_bwd", A_HALF_WINDOW, t, tq_of(t), True,
        _exchange_plan([[gwg2]]), [0, n_att(A_Q_W, tq_of(t)) - 1])
    pattern_grads, pattern_lands = [], []
    for n, ((window, dil), (qs, ks, vs)) in enumerate(zip(B_PATTERNS, b_qkv)):
        carried = [gwu2, gwd2][n:n + 1]
        grads, lands = _attn_bwd(qs, ks, vs, rows_of(b_out[n]), rows_of(db[n]), rows_of(b_lse[n]), no_sink, f"attn_b{dil}_bwd",
                                 window // (2 * dil), t // dil, tq_of(t // dil), False,
                                 _exchange_plan([carried]) if carried else None, [0, n_att(B_W, tq_of(t // dil)) - 1])
        pattern_grads.append(grads[:3])
        pattern_lands.extend(lands)
    land_wu2, land_wd2 = pattern_lands
    dx1, dg_mix, gwin, dout1 = _in_proj_bwd(dx2, x1, norm_mix, win, h_mix, cos, sin_signed, daq, dakx, davx, pattern_grads, "in_proj_bwd", tm_mix)

    (dgt1, dut1, gwd1), (land_in, land_out) = _ffn_bwd_hidden(
        dout1, gp1, up1, hid1, wd1, "ffn1_bwd_hidden", tm_mix, tfp, _exchange_plan([[gwin], [gwout]]),
        [0, (f_all // tfp) * (t // tm_mix) - 1])
    (gwg1, gwu1), (land_wd1,) = _token_products(
        [dgt1, dut1], h1, "ffn1_bwd_gate_up", tt, tfp, _exchange_plan([[gwd1]]), [0, n_prod - 1])
    (grad_x, dg_ffn1), (land_wg1, land_wu1) = _ffn_bwd_input(
        dx1, x, norm_ffn1, dgt1, dut1, wg1, wu1, "ffn1_bwd_input", tm_mix, _exchange_plan([[gwg1], [gwu1]]), [0, t // tm_mix - 1])

    dsink_pairs = jnp.sum(dsink_parts[:, :, 0, :], axis=1)
    dsink = jnp.stack([dsink_pairs[:, 0], dsink_pairs[:, HEAD_DIM]], axis=1).reshape(1, -1)
    small = jnp.concatenate([dg_ffn1, dg_mix, dg_ffn2, dg_final, jnp.pad(dsink, ((0, 0), (0, d - dsink.shape[1]))),
                             sq, jnp.zeros((2, d), F32)], axis=0)
    (land_small,) = _run_alone(_exchange_plan([[jnp.tile(small, (N_DEV, 1))]]), "gather_small_gradients")
    red_small = _sum_small(land_small, "sum_small_grads")
    loss = 0.5 * jnp.sum(red_small[5]) / d

    rf = rows1[0]
    sharded = {"w_gate1": (land_wg1, 0, rf, True), "w_up1": (land_wu1, 0, rf, True), "w_down1": (land_wd1, 0, rf, False),
               "w_in": (land_in, 0, rows2[0], True), "w_out": (land_out, 0, rows2[1], False),
               "w_gate2": (land_wg2, 0, rf, True), "w_up2": (land_wu2, 0, rf, True), "w_down2": (land_wd2, 0, rf, False)}
    n_sink = a_sink.shape[1]
    small_grads = {"norm_ffn1": red_small[0:1], "norm_mix": red_small[1:2], "norm_ffn2": red_small[2:3], "norm_final": red_small[3],
                   "a_sink": red_small[4:5, :n_sink]}
    params = {
        "norm_ffn1": (norm_ffn1, m_norm_ffn1, v_norm_ffn1), "w_gate1": (w_gate1, m_w_gate1, v_w_gate1),
        "w_up1": (w_up1, m_w_up1, v_w_up1), "w_down1": (w_down1, m_w_down1, v_w_down1),
        "norm_mix": (norm_mix, m_norm_mix, v_norm_mix), "w_in": (w_in, m_w_in, v_w_in),
        "a_sink": (a_sink, m_a_sink, v_a_sink), "w_out": (w_out, m_w_out, v_w_out),
        "norm_ffn2": (norm_ffn2, m_norm_ffn2, v_norm_ffn2), "w_gate2": (w_gate2, m_w_gate2, v_w_gate2),
        "w_up2": (w_up2, m_w_up2, v_w_up2), "w_down2": (w_down2, m_w_down2, v_w_down2),
        "norm_final": (norm_final, m_norm_final, v_norm_final),
    }
    grad_list, deltas, new_ms, new_vs = [], [], [], []
    for name, (w, m, v) in params.items():
        if name in sharded:
            land, j, rows, is_transposed = sharded[name]
            view = (lambda a: a[0].T) if is_transposed else (lambda a: a[0])
            back = (lambda a: a.T[None]) if is_transposed else (lambda a: a[None])
            outs = [back(o) for o in _adamw_rows(land, j, rows, view(w), view(m), view(v), f"adamw_{name}")]
        else:
            as_block = (lambda a: a.reshape(1, -1)) if w.ndim == 1 else (lambda a: a)
            g = small_grads[name]
            outs = [g] + [o.reshape(w.shape) for o in _adamw(as_block(w), as_block(g), as_block(m), as_block(v), f"adamw_{name}")]
        for lst, o in zip((grad_list, deltas, new_ms, new_vs), outs):
            lst.append(o)
    return (loss, grad_x[None], *grad_list, *deltas, *new_ms, *new_vs)
```

```python
import functools
import itertools

import numpy as np
import jax
import jax.numpy as jnp
from jax import lax
from jax.experimental import pallas as pl
from jax.experimental.pallas import tpu as pltpu

F32 = jnp.float32
BF16 = jnp.bfloat16

N_DEV = 8
HEAD_DIM = 64
LANES = 128
A_Q_W, A_KV_W, B_W = 512, 128, 512
A_HALF_WINDOW = 128
B_PATTERNS = ((128, 1), (512, 4), (2048, 16))
ROPE_THETA = 10000.0
NORM_EPS = 1e-6
FFN_RES_WEIGHT = 0.5
QK_SCALE = HEAD_DIM ** -0.5
NEG = -1e30

ADAM_LR = 0.001
ADAM_B1 = 0.9
ADAM_B2 = 0.999
ADAM_EPS = 1e-08
ADAM_WD = 0.01
ADAM_STEP = 10

MESH_T = pl.DeviceIdType.MESH
VMEM_LIMIT = 60 * 1024 * 1024


def _cp(sem=None, vmem=VMEM_LIMIT):
    return pltpu.CompilerParams(dimension_semantics=sem, vmem_limit_bytes=vmem)


def _dot_nn(a, b):
    return jnp.dot(a, b, preferred_element_type=F32)


def _dot_nt(a, b):
    return lax.dot_general(a, b, (((1,), (1,)), ((), ())), preferred_element_type=F32)


def _dot_tn(a, b):
    return lax.dot_general(a, b, (((0,), (0,)), ((), ())), preferred_element_type=F32)


def _rstd(xv):
    return lax.rsqrt(jnp.mean(xv * xv, axis=-1, keepdims=True) + NORM_EPS)


def _norm_bwd(dh, xv, r, gn):
    gy = dh * gn
    c = jnp.sum(gy * xv, axis=-1, keepdims=True) * (1.0 / xv.shape[-1])
    dx = r * gy - xv * (r * r * r * c)
    dgn = jnp.sum(dh * (xv * r), axis=0, keepdims=True)
    return dx, dgn


def _accumulate(ref, val, first):
    @pl.when(first)
    def _():
        ref[...] = val

    @pl.when(jnp.logical_not(first))
    def _():
        ref[...] += val


def _mesh_pos():
    return lax.axis_index("x"), lax.axis_index("y"), lax.axis_index("c")


def _dev_index(d):
    return 4 * d[0] + 2 * d[1] + d[2]


class _Comm:
    def __init__(self, inputs, out_shape, scratch, phases):
        self.inputs, self.out_shape, self.scratch, self.phases = inputs, out_shape, scratch, phases

    def specs(self):
        any_spec = pl.BlockSpec(memory_space=pl.ANY)
        return [any_spec] * len(self.inputs), [any_spec] * len(self.out_shape)


def _run_alone(comm, name):
    n_in, n_out = len(comm.inputs), len(comm.out_shape)

    def body(*refs):
        for phase in comm.phases:
            phase(refs[:n_in], refs[n_in:n_in + n_out], refs[n_in + n_out:])

    in_specs, out_specs = comm.specs()
    return pl.pallas_call(body, name=name, out_shape=comm.out_shape, in_specs=in_specs, out_specs=out_specs,
                          scratch_shapes=comm.scratch)(*comm.inputs)


def _run_alone_busy(comm, shards, positions, name):
    n_in, n_out = len(comm.inputs), len(comm.out_shape)
    offs = [int(o) for o in np.cumsum([0] + [s.shape[0] for s in shards])]
    t = positions.shape[0]
    half = HEAD_DIM // 2
    inv_freq = 1.0 / (ROPE_THETA ** (jnp.arange(0, HEAD_DIM, 2, dtype=F32) / HEAD_DIM))
    lanes = jnp.stack([jnp.tile(inv_freq, LANES // half),
                       jnp.tile(jnp.concatenate([-jnp.ones(half, F32), jnp.ones(half, F32)]), LANES // HEAD_DIM)])

    n_s = len(shards)

    def body(*refs):
        (c_in, s_refs, (pos_ref, lanes_ref), c_out, outs, scr, bufs, (stack_s, cos_s, sin_s), (in_sem, out_sem)) = _split(
            refs, n_in, n_s, 2, n_out, 3, len(comm.scratch), n_s, 3)
        comm.phases[0](c_in, c_out, scr)
        loads = [pltpu.make_async_copy(s_ref, buf, in_sem.at[k]) for k, (s_ref, buf) in enumerate(zip(s_refs, bufs))]
        for load in loads:
            load.start()
        ang = pos_ref[...] * lanes_ref[0:1, :]
        cos_s[...] = jnp.cos(ang)
        sin_s[...] = jnp.sin(ang) * lanes_ref[1:2, :]
        for load, buf, lo, hi in zip(loads, bufs, offs[:-1], offs[1:]):
            load.wait()
            stack_s[lo:hi, :] = buf[...].astype(BF16)
        stores = [pltpu.make_async_copy(src, dst, out_sem.at[k]) for k, (src, dst) in enumerate(zip((stack_s, cos_s, sin_s), outs))]
        for store in stores:
            store.start()
        for phase in comm.phases[1:]:
            phase(c_in, c_out, scr)
        for store in stores:
            store.wait()

    in_specs, out_specs = comm.specs()
    vmem, hbm = pl.BlockSpec(memory_space=pltpu.VMEM), pl.BlockSpec(memory_space=pl.ANY)
    table = jax.ShapeDtypeStruct((t, LANES), F32)
    stack = jax.ShapeDtypeStruct((offs[-1], shards[0].shape[1]), BF16)
    res = pl.pallas_call(
        body, name=name, out_shape=comm.out_shape + [stack, table, table],
        in_specs=in_specs + [hbm] * n_s + [vmem] * 2, out_specs=out_specs + [hbm] * 3,
        scratch_shapes=comm.scratch + [pltpu.VMEM(s.shape, F32) for s in shards]
        + [pltpu.VMEM(stack.shape, BF16), pltpu.VMEM(table.shape, F32), pltpu.VMEM(table.shape, F32),
           pltpu.SemaphoreType.DMA((n_s,)), pltpu.SemaphoreType.DMA((3,))],
        compiler_params=_cp())(*comm.inputs, *shards, positions, lanes)
    return res[:n_out], res[n_out], res[n_out + 1], res[n_out + 2]


def _run_hosted(comm, at, step, ins, outs, scr):
    for phase, when in zip(comm.phases, at):
        @pl.when(step == when)
        def _(phase=phase):
            phase(ins, outs, scr)


def _split(refs, *counts):
    parts, o = [], 0
    for n in counts:
        parts.append(refs[o:o + n])
        o += n
    return parts + [refs[o:]]


def _gather_plan(packed, rows_list, stages=1):
    n_w = len(rows_list)
    offs = [int(o) for o in np.cumsum([0] + list(rows_list))]
    d = packed.shape[1]
    per = n_w // stages
    groups = [range(s * per, (s + 1) * per) for s in range(stages)]

    def tools(ins, outs, scr):
        p_ref = ins[0]
        send_sems, recv_sems, local_sem = scr
        x, y, c = _mesh_pos()
        me, sibling = (x, y, c), (x, y, 1 - c)
        chips = [(1 - x, y), (x, 1 - y), (1 - x, 1 - y)]

        def rows(w, dev):
            start = pl.multiple_of(_dev_index(dev) * rows_list[w], 16)
            return outs[w].at[pl.ds(start, rows_list[w]), :]

        def mine(w):
            return p_ref.at[pl.ds(offs[w], rows_list[w]), :]

        def copy(k, s, w, block, to, own):
            return pltpu.make_async_remote_copy(
                src_ref=mine(w) if own else rows(w, block), dst_ref=rows(w, block),
                send_sem=send_sems.at[k, s], recv_sem=recv_sems.at[k, s], device_id=to, device_id_type=MESH_T)

        def whole_group(k, s):
            span = p_ref.at[pl.ds(offs[groups[s][0]], offs[groups[s][-1] + 1] - offs[groups[s][0]]), :]
            return pltpu.make_async_remote_copy(
                src_ref=span, dst_ref=span, send_sem=send_sems.at[k, s], recv_sem=recv_sems.at[k, s],
                device_id=me, device_id_type=MESH_T)

        return p_ref, local_sem, me, sibling, chips, c, rows, mine, copy, whole_group

    def start(ins, outs, scr):
        _, local_sem, me, sibling, chips, c, rows, mine, copy, _ = tools(ins, outs, scr)
        for w in range(n_w):
            pltpu.make_async_copy(mine(w), rows(w, me), local_sem).start()
        for s, group in enumerate(groups):
            for w in group:
                copy(0, s, w, me, sibling, True).start()
            for j, chip in enumerate(chips[:2]):
                for w in group:
                    copy(1 + j, s, w, me, (*chip, c), True).start()

    def forward(ins, outs, scr):
        _, _, (x, y, _), sibling, chips, c, _, _, copy, whole_group = tools(ins, outs, scr)
        south = c == 0
        source = (jnp.where(south, x, 1 - x), jnp.where(south, 1 - y, y), c)
        target = (jnp.where(south, 1 - x, x), jnp.where(south, y, 1 - y), c)
        for s, group in enumerate(groups):
            for j, chip in enumerate(chips[:2]):
                whole_group(1 + j, s).wait_recv()
            for w in group:
                copy(3, s, w, source, target, False).start()
            for j, chip in enumerate(chips[:2]):
                for w in group:
                    copy(4 + j, s, w, (*chip, c), sibling, False).start()

    def relay(ins, outs, scr):
        _, _, _, sibling, chips, c, _, _, copy, whole_group = tools(ins, outs, scr)
        for s, group in enumerate(groups):
            whole_group(3, s).wait_recv()
            for w in group:
                copy(6, s, w, (*chips[2], c), sibling, False).start()

    def finish(ins, outs, scr):
        p_ref, local_sem, _, _, _, _, _, _, _, whole_group = tools(ins, outs, scr)
        for s in range(stages):
            whole_group(0, s).wait_recv()
            for j in range(3):
                whole_group(4 + j, s).wait_recv()
            for k in range(7):
                whole_group(k, s).wait_send()
        pltpu.make_async_copy(p_ref, p_ref, local_sem).wait()

    return _Comm(
        [packed], [jax.ShapeDtypeStruct((N_DEV * r, d), packed.dtype) for r in rows_list],
        [pltpu.SemaphoreType.DMA((7, stages)), pltpu.SemaphoreType.DMA((7, stages)), pltpu.SemaphoreType.DMA],
        [start, forward, relay, finish])


def _exchange_plan(groups):
    flat = [a for g in groups for a in g]
    n_g = len(groups)
    sizes = [len(g) for g in groups]
    rows = [g[0].shape[0] // N_DEV for g in groups]
    first = [int(o) for o in np.cumsum([0] + sizes[:-1])]

    def start(srcs, lands, scr):
        send_sems, recv_sems, local_sems = scr
        x, y, c = _mesh_pos()
        me = (x, y, c)
        me_idx = _dev_index(me)

        def block(g, i, dev):
            start_row = pl.multiple_of(_dev_index(dev) * rows[g], 8)
            return srcs[first[g] + i].at[pl.ds(start_row, rows[g]), :]

        def slot(g, i):
            return lands[g].at[me_idx, pl.ds(i * rows[g], rows[g]), :]

        for g in range(n_g):
            for i in range(sizes[g]):
                pltpu.make_async_copy(block(g, i, me), slot(g, i), local_sems.at[g]).start()
        flips = [f for f in itertools.product((0, 1), repeat=3) if any(f)]
        for k, (fx, fy, fc) in enumerate(flips):
            peer = (1 - x if fx else x, 1 - y if fy else y, 1 - c if fc else c)
            for g in range(n_g):
                for i in range(sizes[g]):
                    pltpu.make_async_remote_copy(
                        src_ref=block(g, i, peer), dst_ref=slot(g, i), send_sem=send_sems.at[g, k],
                        recv_sem=recv_sems.at[g, k], device_id=peer, device_id_type=MESH_T).start()

    def finish(srcs, lands, scr):
        send_sems, recv_sems, local_sems = scr
        me = _mesh_pos()
        for k in range(7):
            for g in range(n_g):
                pltpu.make_async_remote_copy(
                    src_ref=lands[g].at[0], dst_ref=lands[g].at[0], send_sem=send_sems.at[g, k],
                    recv_sem=recv_sems.at[g, k], device_id=me, device_id_type=MESH_T).wait()
        for g in range(n_g):
            pltpu.make_async_copy(lands[g].at[0], lands[g].at[0], local_sems.at[g]).wait()

    return _Comm(
        flat, [jax.ShapeDtypeStruct((N_DEV, sizes[g] * rows[g], groups[g][0].shape[1]), groups[g][0].dtype) for g in range(n_g)],
        [pltpu.SemaphoreType.DMA((n_g, 7)), pltpu.SemaphoreType.DMA((n_g, 7)), pltpu.SemaphoreType.DMA((n_g,))],
        [start, finish])


def _comm_parts(comm):
    if comm is None:
        return 0, 0, [], [], [], [], []
    in_specs, out_specs = comm.specs()
    return len(comm.inputs), len(comm.out_shape), in_specs, out_specs, comm.out_shape, comm.scratch, comm.inputs


def _ffn_fwd(x, gn, wg_t, wu_t, wd, name, tm, tf, comm=None, comm_at=None, loss_head=None):
    t, d = x.shape
    f_all = wg_t.shape[0]
    n_f = f_all // tf
    n_ci, n_co, c_in_specs, c_out_specs, c_shapes, c_scratch, c_inputs = _comm_parts(comm)
    n_head, n_y = (2, 4) if loss_head else (0, 1)

    def body(*refs):
        ((x_ref, gn_ref, wg_ref, wu_ref, wd_ref), head_in, c_in, y_outs, (h_ref, gp_ref, up_ref, hid_ref), c_out,
         (h_s, hid_s), c_scr) = _split(refs, 5, n_head, n_ci, n_y, 4, n_co, 2)
        f = pl.program_id(1)
        if comm:
            _run_hosted(comm, comm_at, pl.program_id(0) * n_f + f, c_in, c_out, c_scr)

        @pl.when(f == 0)
        def _():
            xv = x_ref[...]
            h = ((xv * _rstd(xv)) * gn_ref[...]).astype(BF16)
            h_s[...] = h
            h_ref[...] = h

        h = h_s[...]
        cols = pl.ds(pl.multiple_of(f * tf, tf), tf)
        g = _dot_nt(h, wg_ref[cols, :])
        u = _dot_nt(h, wu_ref[cols, :])
        sg = jax.nn.sigmoid(g)
        silu = g * sg
        gp_ref[...] = (u * (sg * (1.0 + g * (1.0 - sg)))).astype(BF16)
        up_ref[...] = silu.astype(BF16)
        hid = (silu * u).astype(BF16)
        hid_ref[...] = hid
        for f0 in range(n_f):
            @pl.when(f == f0)
            def _(f0=f0):
                hid_s[:, f0 * tf:(f0 + 1) * tf] = hid

        @pl.when(f == n_f - 1)
        def _():
            y = x_ref[...] + FFN_RES_WEIGHT * _dot_nn(hid_s[...], wd_ref[...])
            if not loss_head:
                y_outs[0][...] = y
            else:
                (gf_ref, tg_ref), (dy_ref, dgf_ref, sq_ref, half_ref) = head_in, y_outs
                gfv, r, first = gf_ref[...], _rstd(y), pl.program_id(0) == 0
                diff = (y * r) * gfv - tg_ref[...]
                dy, dgf = _norm_bwd(diff * (1.0 / d), y, r, gfv)
                dy_ref[...] = dy
                half_ref[...] = (FFN_RES_WEIGHT * dy).astype(BF16)
                _accumulate(dgf_ref, dgf, first)
                _accumulate(sq_ref, jnp.sum(diff * diff, axis=0, keepdims=True), first)

    tok = pl.BlockSpec((tm, d), lambda i, f: (i, 0))
    row = pl.BlockSpec((1, d), lambda i, f: (0, 0))
    whole = pl.BlockSpec((f_all, d), lambda i, f: (0, 0), pipeline_mode=pl.Buffered(1))
    act = pl.BlockSpec((tm, tf), lambda i, f: (i, f))
    act_shape = jax.ShapeDtypeStruct((t, f_all), BF16)
    tok_f32, tok_bf16, row_f32 = (jax.ShapeDtypeStruct((t, d), F32), jax.ShapeDtypeStruct((t, d), BF16),
                                  jax.ShapeDtypeStruct((1, d), F32))
    y_shapes, y_specs = ([tok_f32, row_f32, row_f32, tok_bf16], [tok, row, row, tok]) if loss_head else ([tok_f32], [tok])
    res = pl.pallas_call(
        body, name=name, grid=(t // tm, n_f),
        out_shape=y_shapes + [tok_bf16, act_shape, act_shape, act_shape] + c_shapes,
        in_specs=[tok, row, whole, whole, whole] + ([row, tok] if loss_head else []) + c_in_specs,
        out_specs=y_specs + [tok, act, act, act] + c_out_specs,
        scratch_shapes=[pltpu.VMEM((tm, d), BF16), pltpu.VMEM((tm, f_all), BF16)] + c_scratch,
        compiler_params=_cp(("arbitrary", "arbitrary")),
    )(x, gn, wg_t, wu_t, wd, *(loss_head or ()), *c_inputs)
    return res[:n_y + 4], res[n_y + 4:]


def _ffn_bwd_hidden(dout, gp, up, hid, wd, name, tt, tf, comm=None, comm_at=None):
    t, d = dout.shape
    f_all = wd.shape[0]
    n_t = t // tt
    n_ci, n_co, c_in_specs, c_out_specs, c_shapes, c_scratch, c_inputs = _comm_parts(comm)

    def body(*refs):
        (dout_ref, gp_ref, up_ref, hid_ref, wd_ref), c_in, (dg_ref, du_ref, gwd_ref), c_out, (acc,), c_scr = _split(
            refs, 5, n_ci, 3, n_co, 1)
        s = pl.program_id(1)
        if comm:
            _run_hosted(comm, comm_at, pl.program_id(0) * n_t + s, c_in, c_out, c_scr)
        doutv = dout_ref[...]
        dhid = _dot_nt(doutv, wd_ref[...])
        dg_ref[...] = (dhid * gp_ref[...].astype(F32)).astype(BF16)
        du_ref[...] = (dhid * up_ref[...].astype(F32)).astype(BF16)
        _accumulate(acc, _dot_tn(hid_ref[...], doutv), s == 0)

        @pl.when(s == n_t - 1)
        def _():
            gwd_ref[...] = acc[...].astype(BF16)

    tok = pl.BlockSpec((tt, d), lambda f, s: (s, 0))
    wblk = pl.BlockSpec((tf, d), lambda f, s: (f, 0))
    act = pl.BlockSpec((tt, tf), lambda f, s: (s, f))
    act_shape = jax.ShapeDtypeStruct((t, f_all), BF16)
    res = pl.pallas_call(
        body, name=name, grid=(f_all // tf, n_t),
        out_shape=[act_shape, act_shape, jax.ShapeDtypeStruct((f_all, d), BF16)] + c_shapes,
        in_specs=[tok, act, act, act, wblk] + c_in_specs, out_specs=[act, act, wblk] + c_out_specs,
        scratch_shapes=[pltpu.VMEM((tf, d), F32)] + c_scratch,
        compiler_params=_cp(("arbitrary", "arbitrary")),
    )(dout, gp, up, hid, wd, *c_inputs)
    return res[:3], res[3:]


def _ffn_bwd_input(dy, x, gn, dg, du, wg_t, wu_t, name, tm, comm=None, comm_at=None):
    t, d = x.shape
    f_all = wg_t.shape[0]
    n_ci, n_co, c_in_specs, c_out_specs, c_shapes, c_scratch, c_inputs = _comm_parts(comm)

    def body(*refs):
        (dy_ref, x_ref, gn_ref, dg_ref, du_ref, wg_ref, wu_ref), c_in, (dx_ref, dgn_ref), c_out, c_scr = _split(
            refs, 7, n_ci, 2, n_co)
        i = pl.program_id(0)
        if comm:
            _run_hosted(comm, comm_at, i, c_in, c_out, c_scr)
        dh = _dot_nn(dg_ref[...], wg_ref[...]) + _dot_nn(du_ref[...], wu_ref[...])
        xv = x_ref[...]
        dxn, dgn = _norm_bwd(dh, xv, _rstd(xv), gn_ref[...])
        dx_ref[...] = dy_ref[...] + dxn
        _accumulate(dgn_ref, dgn, i == 0)

    tok = pl.BlockSpec((tm, d), lambda i: (i, 0))
    row = pl.BlockSpec((1, d), lambda i: (0, 0))
    act = pl.BlockSpec((tm, f_all), lambda i: (i, 0))
    whole = pl.BlockSpec((f_all, d), lambda i: (0, 0))
    res = pl.pallas_call(
        body, name=name, grid=(t // tm,),
        out_shape=[jax.ShapeDtypeStruct((t, d), F32), jax.ShapeDtypeStruct((1, d), F32)] + c_shapes,
        in_specs=[tok, tok, row, act, act, whole, whole] + c_in_specs,
        out_specs=[tok, row] + c_out_specs,
        scratch_shapes=c_scratch,
        compiler_params=_cp(("arbitrary",)),
    )(dy, x, gn, dg, du, wg_t, wu_t, *c_inputs)
    return res[:2], res[2:]


def _token_products(lhs_list, rhs, name, tt, tf, comm=None, comm_at=None):
    n_l = len(lhs_list)
    t, f_all = lhs_list[0].shape
    d = rhs.shape[1]
    n_t = t // tt
    n_ci, n_co, c_in_specs, c_out_specs, c_shapes, c_scratch, c_inputs = _comm_parts(comm)

    def body(*refs):
        lhs_refs, (rhs_ref,), c_in, out_refs, c_out, accs, c_scr = _split(refs, n_l, 1, n_ci, n_l, n_co, n_l)
        s = pl.program_id(1)
        if comm:
            _run_hosted(comm, comm_at, pl.program_id(0) * n_t + s, c_in, c_out, c_scr)
        rv = rhs_ref[...]
        for l_ref, acc in zip(lhs_refs, accs):
            _accumulate(acc, _dot_tn(l_ref[...], rv), s == 0)

        @pl.when(s == n_t - 1)
        def _():
            for o_ref, acc in zip(out_refs, accs):
                o_ref[...] = acc[...].astype(BF16)

    act = pl.BlockSpec((tt, tf), lambda f, s: (s, f))
    tok = pl.BlockSpec((tt, d), lambda f, s: (s, 0))
    wblk = pl.BlockSpec((tf, d), lambda f, s: (f, 0))
    res = pl.pallas_call(
        body, name=name, grid=(f_all // tf, n_t),
        out_shape=[jax.ShapeDtypeStruct((f_all, d), BF16)] * n_l + c_shapes,
        in_specs=[act] * n_l + [tok] + c_in_specs, out_specs=[wblk] * n_l + c_out_specs,
        scratch_shapes=[pltpu.VMEM((tf, d), F32)] * n_l + c_scratch,
        compiler_params=_cp(("arbitrary", "arbitrary")),
    )(*lhs_list, rhs, *c_inputs)
    return res[:n_l], res[n_l:]


def _swap_halves(t):
    w = t.shape[-1]
    lane = lax.broadcasted_iota(jnp.int32, (1, w), 1)
    return jnp.where((lane % HEAD_DIM) < HEAD_DIM // 2, pltpu.roll(t, w - HEAD_DIM // 2, 1), pltpu.roll(t, HEAD_DIM // 2, 1))


def _rope(t, cos, sin_signed):
    reps = t.shape[-1] // LANES
    return t * jnp.tile(cos, (1, reps)) + _swap_halves(t) * jnp.tile(sin_signed, (1, reps))


def _rope_bwd(dt, cos, sin_signed):
    reps = dt.shape[-1] // LANES
    return dt * jnp.tile(cos, (1, reps)) + _swap_halves(dt * jnp.tile(sin_signed, (1, reps)))


DILATIONS = tuple(dil for _, dil in B_PATTERNS if dil > 1)


def _seg_shape(t, dil, w, dtype):
    return jax.ShapeDtypeStruct((dil, t // dil, w), dtype)


def _seg_spec(tm, dil, w):
    return pl.BlockSpec((dil, tm // dil, w), lambda i: (0, i, 0))


def _tile_scratch(tm, w):
    return [pltpu.VMEM((tm, LANES), F32)] * (w // LANES)


def _put_tile(tile, val):
    for c, ref in enumerate(tile):
        ref[...] = val[:, c * LANES:(c + 1) * LANES]


def _get_tile(tile):
    return jnp.concatenate([ref[...] for ref in tile], axis=1)


def _scatter_to_segments(tile, seg_refs):
    for seg_ref, dil in zip(seg_refs, DILATIONS):
        rows = tile[0].shape[0] // dil
        for r in range(dil):
            for c, ref in enumerate(tile):
                seg_ref[r, :, c * LANES:(c + 1) * LANES] = ref[pl.ds(r, rows, stride=dil), :].astype(seg_ref.dtype)


def _gather_from_segments(seg_ref, dil, tile, add=False):
    rows = tile[0].shape[0] // dil
    for r in range(dil):
        for c, ref in enumerate(tile):
            idx = (pl.ds(r, rows, stride=dil), slice(None))
            v = seg_ref[r, :, c * LANES:(c + 1) * LANES].astype(F32)
            ref[idx] = ref[idx] + v if add else v


def _in_proj_fwd(x, gn, win_t, cos, sin_signed, name, tm):
    t, d = x.shape
    in_w = win_t.shape[0]
    n_dil = len(DILATIONS)

    def body(x_ref, gn_ref, w_ref, cos_ref, sin_ref, h_ref, aq_ref, akx_ref, avx_ref, bq_ref, bk_ref, bv_ref, *rest):
        seg_refs, tile = rest[:3 * n_dil], rest[3 * n_dil:]
        xv = x_ref[...]
        h = ((xv * _rstd(xv)) * gn_ref[...]).astype(BF16)
        h_ref[...] = h
        p = _dot_nt(h, w_ref[...])
        cs, sn = cos_ref[...], sin_ref[...]
        o = 0
        aq_ref[...] = (_rope(p[:, o:o + A_Q_W], cs, sn) * QK_SCALE).astype(BF16)
        o += A_Q_W
        ak = _rope(p[:, o:o + A_KV_W], cs, sn)
        o += A_KV_W
        av = p[:, o:o + A_KV_W]
        o += A_KV_W
        low = lax.broadcasted_iota(jnp.int32, (1, LANES), 1) < HEAD_DIM
        for src, dst in ((ak, akx_ref), (av, avx_ref)):
            other = pltpu.roll(src, HEAD_DIM, 1)
            dst[0] = jnp.where(low, src, other).astype(BF16)
            dst[1] = jnp.where(low, other, src).astype(BF16)
        for k, nat_ref in enumerate((bq_ref, bk_ref, bv_ref)):
            val = p[:, o:o + B_W]
            o += B_W
            if k < 2:
                val = _rope(val, cs, sn)
            if k == 0:
                val = val * QK_SCALE
            nat_ref[...] = val.astype(BF16)
            _put_tile(tile, val)
            _scatter_to_segments(tile, seg_refs[k * n_dil:(k + 1) * n_dil])

    tok = lambda w: pl.BlockSpec((tm, w), lambda i: (i, 0))
    kvx = pl.BlockSpec((2, tm, LANES), lambda i: (0, i, 0))
    sd = lambda *s: jax.ShapeDtypeStruct(s, BF16)
    res = pl.pallas_call(
        body, name=name, grid=(t // tm,),
        out_shape=[sd(t, d), sd(t, A_Q_W), sd(2, t, LANES), sd(2, t, LANES), sd(t, B_W), sd(t, B_W), sd(t, B_W)]
        + [_seg_shape(t, dil, B_W, BF16) for _ in range(3) for dil in DILATIONS],
        in_specs=[tok(d), pl.BlockSpec((1, d), lambda i: (0, 0)), pl.BlockSpec((in_w, d), lambda i: (0, 0)),
                  tok(LANES), tok(LANES)],
        out_specs=[tok(d), tok(A_Q_W), kvx, kvx, tok(B_W), tok(B_W), tok(B_W)]
        + [_seg_spec(tm, dil, B_W) for _ in range(3) for dil in DILATIONS],
        scratch_shapes=_tile_scratch(tm, B_W),
        compiler_params=_cp(("arbitrary",)),
    )(x, gn, win_t, cos, sin_signed)
    return res[:7], [res[7 + k * n_dil:7 + (k + 1) * n_dil] for k in range(3)]


def _in_proj_bwd(dres, x, gn, win_t, h, cos, sin_signed, daq, dakx, davx, per_pattern, name, tm):
    t, d = x.shape
    in_w = win_t.shape[0]
    n_t = t // tm
    chunk = in_w // 3
    n_pat = 3 * (1 + len(DILATIONS))

    def body(*refs):
        ((dres_ref, x_ref, gn_ref, w_ref, h_ref, cos_ref, sin_ref, daq_ref, dakx_ref, davx_ref), pat,
         (dx_ref, dgn_ref, gw_ref, half_ref), (dp_s, gw_s), tile) = _split(refs, 10, n_pat, 4, 2)
        i = pl.program_id(0)
        cs, sn = cos_ref[...], sin_ref[...]
        low = lax.broadcasted_iota(jnp.int32, (1, LANES), 1) < HEAD_DIM

        def fold(ref):
            a, b = ref[0].astype(F32), ref[1].astype(F32)
            return jnp.where(low, a + pltpu.roll(a, HEAD_DIM, 1), b + pltpu.roll(b, HEAD_DIM, 1))

        def summed(j):
            _put_tile(tile, pat[j][...].astype(F32))
            for k, dil in enumerate(DILATIONS):
                _gather_from_segments(pat[3 * (1 + k) + j], dil, tile, add=True)
            return _get_tile(tile)

        o = 0
        dp_s[:, o:o + A_Q_W] = _rope_bwd(daq_ref[...].astype(F32) * QK_SCALE, cs, sn).astype(BF16)
        o += A_Q_W
        dp_s[:, o:o + A_KV_W] = _rope_bwd(fold(dakx_ref), cs, sn).astype(BF16)
        o += A_KV_W
        dp_s[:, o:o + A_KV_W] = fold(davx_ref).astype(BF16)
        o += A_KV_W
        dp_s[:, o:o + B_W] = _rope_bwd(summed(0) * QK_SCALE, cs, sn).astype(BF16)
        o += B_W
        dp_s[:, o:o + B_W] = _rope_bwd(summed(1), cs, sn).astype(BF16)
        o += B_W
        dp_s[:, o:o + B_W] = summed(2).astype(BF16)
        dh = _dot_nn(dp_s[...], w_ref[...])
        hv = h_ref[...]
        for c0 in range(0, in_w, chunk):
            _accumulate(gw_s.at[pl.ds(c0, chunk), :], _dot_tn(dp_s[:, c0:c0 + chunk], hv), i == 0)
        xv = x_ref[...]
        dxn, dgn = _norm_bwd(dh, xv, _rstd(xv), gn_ref[...])
        dx = dres_ref[...] + dxn
        dx_ref[...] = dx
        half_ref[...] = (FFN_RES_WEIGHT * dx).astype(BF16)
        _accumulate(dgn_ref, dgn, i == 0)

        @pl.when(i == n_t - 1)
        def _():
            gw_ref[...] = gw_s[...].astype(BF16)

    tok = lambda w: pl.BlockSpec((tm, w), lambda i: (i, 0))
    row = pl.BlockSpec((1, d), lambda i: (0, 0))
    whole = pl.BlockSpec((in_w, d), lambda i: (0, 0))
    weight = pl.BlockSpec((in_w, d), lambda i: (0, 0), pipeline_mode=pl.Buffered(1))
    kvx = pl.BlockSpec((2, tm, LANES), lambda i: (0, i, 0))
    flat = [a if k == 0 else a.reshape(DILATIONS[k - 1], -1, B_W) for k, grads in enumerate(per_pattern) for a in grads]
    return pl.pallas_call(
        body, name=name, grid=(n_t,),
        out_shape=[jax.ShapeDtypeStruct((t, d), F32), jax.ShapeDtypeStruct((1, d), F32),
                   jax.ShapeDtypeStruct((in_w, d), BF16), jax.ShapeDtypeStruct((t, d), BF16)],
        in_specs=[tok(d), tok(d), row, weight, tok(d), tok(LANES), tok(LANES), tok(A_Q_W), kvx, kvx]
        + [tok(B_W)] * 3 + [_seg_spec(tm, dil, B_W) for dil in DILATIONS for _ in range(3)],
        out_specs=[tok(d), row, whole, tok(d)],
        scratch_shapes=[pltpu.VMEM((tm, in_w), BF16), pltpu.VMEM((in_w, d), F32)] + _tile_scratch(tm, B_W),
        compiler_params=_cp(("arbitrary",)),
    )(dres, x, gn, win_t, h, cos, sin_signed, daq, dakx, davx, *flat)


def _merge_out_proj_fwd(x, a_out, outs, lses, wout, name, tm):
    t, d = x.shape
    n_dil = len(DILATIONS)

    def body(x_ref, a_ref, *rest):
        o_refs, l_refs, (w_ref, y_ref, b_ref, lt_ref), b_segs, lt_segs, scratch = _split(rest, 1 + n_dil, 1 + n_dil, 4, n_dil, n_dil)
        n_c = B_W // LANES
        tiles = [scratch[j * n_c:(j + 1) * n_c] for j in range(2 * n_dil)]
        os_, ls = [o_refs[0][...].astype(F32)], [l_refs[0][...]]
        for k, dil in enumerate(DILATIONS):
            _gather_from_segments(o_refs[1 + k], dil, tiles[2 * k])
            _gather_from_segments(l_refs[1 + k], dil, tiles[2 * k + 1])
            os_.append(_get_tile(tiles[2 * k]))
            ls.append(_get_tile(tiles[2 * k + 1]))
        mx = functools.reduce(jnp.maximum, ls)
        es = [jnp.exp(l - mx) for l in ls]
        den = functools.reduce(jnp.add, es)
        b = functools.reduce(jnp.add, [e * o for e, o in zip(es, os_)]) / den
        lt = mx + jnp.log(den)
        bb = b.astype(BF16)
        b_ref[...] = bb
        lt_ref[...] = lt
        y_ref[...] = x_ref[...] + _dot_nn(a_ref[...], w_ref[0:A_Q_W, :]) + _dot_nn(bb, w_ref[A_Q_W:A_Q_W + B_W, :])
        _put_tile(tiles[0], b)
        _scatter_to_segments(tiles[0], b_segs)
        _put_tile(tiles[1], lt)
        _scatter_to_segments(tiles[1], lt_segs)

    tok = lambda w: pl.BlockSpec((tm, w), lambda i: (i, 0))
    segs = [_seg_spec(tm, dil, B_W) for dil in DILATIONS]
    res = pl.pallas_call(
        body, name=name, grid=(t // tm,),
        out_shape=[jax.ShapeDtypeStruct((t, d), F32), jax.ShapeDtypeStruct((t, B_W), BF16), jax.ShapeDtypeStruct((t, B_W), F32)]
        + [_seg_shape(t, dil, B_W, BF16) for dil in DILATIONS] + [_seg_shape(t, dil, B_W, F32) for dil in DILATIONS],
        in_specs=[tok(d), tok(A_Q_W)] + ([tok(B_W)] + segs) * 2 + [pl.BlockSpec(wout.shape, lambda i: (0, 0))],
        out_specs=[tok(d), tok(B_W), tok(B_W)] + segs * 2,
        scratch_shapes=_tile_scratch(tm, B_W) * (2 * n_dil),
        compiler_params=_cp(("arbitrary",)),
    )(x, a_out, *outs, *lses, wout)
    return res[0], [res[1]] + list(res[3:3 + n_dil]), [res[2]] + list(res[3 + n_dil:])


def _out_proj_bwd(dy, a_out, b_out, wout, name, tm):
    t, d = dy.shape
    n_t = t // tm
    n_dil = len(DILATIONS)

    def body(dy_ref, a_ref, b_ref, w_ref, da_ref, db_ref, gw_ref, *rest):
        db_segs, gw_s, tile = rest[:n_dil], rest[n_dil], rest[n_dil + 1:]
        i = pl.program_id(0)
        dyb = dy_ref[...].astype(BF16)
        da_ref[...] = _dot_nt(dyb, w_ref[0:A_Q_W, :]).astype(BF16)
        db = _dot_nt(dyb, w_ref[A_Q_W:A_Q_W + B_W, :])
        db_ref[...] = db.astype(BF16)
        _put_tile(tile, db)
        _scatter_to_segments(tile, db_segs)
        ga = _dot_tn(a_ref[...], dyb)
        gb = _dot_tn(b_ref[...], dyb)

        @pl.when(i == 0)
        def _():
            gw_s[0:A_Q_W, :] = ga
            gw_s[A_Q_W:A_Q_W + B_W, :] = gb

        @pl.when(i > 0)
        def _():
            gw_s[0:A_Q_W, :] += ga
            gw_s[A_Q_W:A_Q_W + B_W, :] += gb

        @pl.when(i == n_t - 1)
        def _():
            gw_ref[...] = gw_s[...].astype(BF16)

    tok = lambda w: pl.BlockSpec((tm, w), lambda i: (i, 0))
    whole = pl.BlockSpec(wout.shape, lambda i: (0, 0))
    res = pl.pallas_call(
        body, name=name, grid=(n_t,),
        out_shape=[jax.ShapeDtypeStruct((t, A_Q_W), BF16), jax.ShapeDtypeStruct((t, B_W), BF16),
                   jax.ShapeDtypeStruct(wout.shape, BF16)] + [_seg_shape(t, dil, B_W, BF16) for dil in DILATIONS],
        in_specs=[tok(d), tok(A_Q_W), tok(B_W), whole],
        out_specs=[tok(A_Q_W), tok(B_W), whole] + [_seg_spec(tm, dil, B_W) for dil in DILATIONS],
        scratch_shapes=[pltpu.VMEM(wout.shape, F32)] + _tile_scratch(tm, B_W),
        compiler_params=_cp(("arbitrary",)),
    )(dy, a_out, b_out, wout)
    return res[0], [res[1]] + list(res[3:]), res[2]


SUB_ROWS = 64
WINDOW_ALIGN = 64


def _sub_band(r0, hw, win, seg_len, t, rel, col):
    ks = pl.multiple_of(jnp.clip(r0 - hw, 0, t - win), WINDOW_ALIGN)
    kpos = col + ks
    seg_lo = (r0 // seg_len) * seg_len
    valid = (jnp.abs(rel + (ks - r0)) <= hw) & (kpos >= seg_lo) & (kpos < seg_lo + seg_len)
    return ks, valid


def _split_heads(v, low):
    zero = jnp.zeros_like(v)
    return jnp.concatenate([jnp.where(low, v, zero), jnp.where(low, zero, v)], axis=0)


def _kv_spec(kv, t):
    if kv.ndim == 3:
        return pl.BlockSpec((None, t, LANES), lambda p, i: (p // 2, 0, 0))
    return pl.BlockSpec((t, LANES), lambda p, i: (0, p))


def _attn_fwd(q, k, v, sink, name, hw, seg_len, tq, has_sink):
    t, width = q.shape
    sb = min(SUB_ROWS, tq)
    win = 2 * hw + LANES

    def body(sink_ref, q_ref, k_ref, v_ref, o_ref, lse_ref):
        p, i = pl.program_id(0), pl.program_id(1)
        q0 = i * tq
        low = lax.broadcasted_iota(jnp.int32, (1, LANES), 1) < HEAD_DIM
        rel = lax.broadcasted_iota(jnp.int32, (sb, win), 1) - lax.broadcasted_iota(jnp.int32, (sb, win), 0)
        col = lax.broadcasted_iota(jnp.int32, (1, win), 1)
        subs = []
        for j in range(tq // sb):
            rows = pl.ds(j * sb, sb)
            ks, valid = _sub_band(q0 + j * sb, hw, win, seg_len, t, rel, col)
            subs.append((rows, ks, valid, _dot_nt(_split_heads(q_ref[rows, :], low), k_ref[pl.ds(ks, win), :])))
        for rows, ks, valid, s in subs:
            vw = v_ref[pl.ds(ks, win), :]
            es, inv, lses = [], [], []
            for a in range(2):
                sa = jnp.where(valid, s[a * sb:(a + 1) * sb], NEG)
                m = jnp.max(sa, axis=1, keepdims=True)
                if has_sink:
                    sk = sink_ref[2 * p + a]
                    m = jnp.maximum(m, sk)
                e = jnp.exp(sa - m)
                den = jnp.sum(e, axis=1, keepdims=True)
                if has_sink:
                    den = den + jnp.exp(sk - m)
                es.append(e.astype(BF16))
                inv.append(1.0 / den)
                lses.append(m + jnp.log(den))
            pv = _dot_nn(jnp.concatenate(es, axis=0), vw)
            o_ref[rows, :] = jnp.where(low, pv[0:sb] * inv[0], pv[sb:2 * sb] * inv[1]).astype(BF16)
            lse_ref[rows, :] = jnp.where(low, lses[0], lses[1])

    tile = pl.BlockSpec((tq, LANES), lambda p, i: (i, p))
    return pl.pallas_call(
        body, name=name, grid=(width // LANES, t // tq),
        out_shape=[jax.ShapeDtypeStruct((t, width), BF16), jax.ShapeDtypeStruct((t, width), F32)],
        in_specs=[pl.BlockSpec(memory_space=pltpu.SMEM), tile, _kv_spec(k, t), _kv_spec(v, t)],
        out_specs=[tile, tile],
        compiler_params=_cp(("arbitrary", "arbitrary")),
    )(sink, q, k, v)


def _attn_bwd(q, k, v, o, do, lse, sink, name, hw, seg_len, tq, has_sink, comm=None, comm_at=None):
    t, width = q.shape
    sb = min(SUB_ROWS, tq)
    win = 2 * hw + LANES
    n_q = t // tq
    shared_kv = k.ndim == 3
    n_ci, n_co, c_in_specs, c_out_specs, c_shapes, c_scratch, c_inputs = _comm_parts(comm)

    def body(*refs):
        ((sink_ref, q_ref, k_ref, v_ref, o_ref, do_ref, lse_ref), c_in, (dq_ref, dk_ref, dv_ref, ds_ref), c_out,
         (dk_s, dv_s), c_scr) = _split(refs, 7, n_ci, 4, n_co, 2)
        p, i = pl.program_id(0), pl.program_id(1)
        if comm:
            _run_hosted(comm, comm_at, p * n_q + i, c_in, c_out, c_scr)
        fresh = (i == 0) & (p % 2 == 0) if shared_kv else i == 0
        last = (i == n_q - 1) & (p % 2 == 1) if shared_kv else i == n_q - 1

        @pl.when(fresh)
        def _():
            dk_s[...] = jnp.zeros_like(dk_s)
            dv_s[...] = jnp.zeros_like(dv_s)

        q0 = i * tq
        low = lax.broadcasted_iota(jnp.int32, (1, LANES), 1) < HEAD_DIM
        rel = lax.broadcasted_iota(jnp.int32, (sb, win), 1) - lax.broadcasted_iota(jnp.int32, (sb, win), 0)
        col = lax.broadcasted_iota(jnp.int32, (1, win), 1)
        dsink = [jnp.zeros((1, 1), F32), jnp.zeros((1, 1), F32)]
        subs = []
        for j in range(tq // sb):
            rows = pl.ds(j * sb, sb)
            ks, valid = _sub_band(q0 + j * sb, hw, win, seg_len, t, rel, col)
            kw = k_ref[pl.ds(ks, win), :]
            dov = do_ref[rows, :]
            q2 = _split_heads(q_ref[rows, :], low)
            do2 = _split_heads(dov, low)
            subs.append((rows, ks, valid, kw, dov, q2, do2, _dot_nt(q2, kw), _dot_nt(do2, v_ref[pl.ds(ks, win), :])))
        probs = []
        for rows, ks, valid, kw, dov, q2, do2, s, dpr in subs:
            prod = dov.astype(F32) * o_ref[rows, :].astype(F32)
            lse_t = lse_ref[rows, :]
            prs, dss = [], []
            for a in range(2):
                mine = low if a == 0 else jnp.logical_not(low)
                lse_a = jnp.max(jnp.where(mine, lse_t, -jnp.inf), axis=1, keepdims=True)
                delta = jnp.sum(jnp.where(mine, prod, 0.0), axis=1, keepdims=True)
                pr = jnp.exp(jnp.where(valid, s[a * sb:(a + 1) * sb], NEG) - lse_a)
                prs.append(pr.astype(BF16))
                dss.append((pr * (dpr[a * sb:(a + 1) * sb] - delta)).astype(BF16))
                if has_sink:
                    dsink[a] = dsink[a] - jnp.sum(jnp.exp(sink_ref[2 * p + a] - lse_a) * delta, axis=0, keepdims=True)
            probs.append((jnp.concatenate(prs, axis=0), jnp.concatenate(dss, axis=0)))
        for (rows, ks, valid, kw, dov, q2, do2, s, dpr), (pr2, ds2) in zip(subs, probs):
            dv_s[pl.ds(ks, win), :] += _dot_tn(pr2, do2)
            dk_s[pl.ds(ks, win), :] += _dot_tn(ds2, q2)
            dq2 = _dot_nn(ds2, kw)
            dq_ref[rows, :] = jnp.where(low, dq2[0:sb], dq2[sb:2 * sb]).astype(BF16)
        ds_ref[...] = jnp.broadcast_to(jnp.where(low, dsink[0], dsink[1]), ds_ref.shape)

        @pl.when(last)
        def _():
            dk_ref[...] = dk_s[...].astype(BF16)
            dv_ref[...] = dv_s[...].astype(BF16)

    tile = pl.BlockSpec((tq, LANES), lambda p, i: (i, p))
    kv_shape = jax.ShapeDtypeStruct(k.shape, BF16)
    res = pl.pallas_call(
        body, name=name, grid=(width // LANES, n_q),
        out_shape=[jax.ShapeDtypeStruct((t, width), BF16), kv_shape, kv_shape,
                   jax.ShapeDtypeStruct((width // LANES, n_q, 8, LANES), F32)] + c_shapes,
        in_specs=[pl.BlockSpec(memory_space=pltpu.SMEM), tile, _kv_spec(k, t), _kv_spec(v, t), tile, tile, tile] + c_in_specs,
        out_specs=[tile, _kv_spec(k, t), _kv_spec(v, t), pl.BlockSpec((None, None, 8, LANES), lambda p, i: (p, i, 0, 0))] + c_out_specs,
        scratch_shapes=[pltpu.VMEM((t, LANES), F32)] * 2 + c_scratch,
        compiler_params=_cp(("arbitrary", "arbitrary")),
    )(sink, q, k, v, o, do, lse, *c_inputs)
    return res[:4], res[4:]


def _adamw(w, g, m, v, name):
    def body(w_ref, g_ref, m_ref, v_ref, d_ref, nm_ref, nv_ref):
        gv = g_ref[...]
        nm = ADAM_B1 * m_ref[...] + (1.0 - ADAM_B1) * gv
        nv = ADAM_B2 * v_ref[...] + (1.0 - ADAM_B2) * (gv * gv)
        m_hat = nm / (1.0 - ADAM_B1 ** ADAM_STEP)
        v_hat = nv / (1.0 - ADAM_B2 ** ADAM_STEP)
        d_ref[...] = -ADAM_LR * (m_hat / (jnp.sqrt(v_hat) + ADAM_EPS) + ADAM_WD * w_ref[...])
        nm_ref[...] = nm
        nv_ref[...] = nv

    shape = jax.ShapeDtypeStruct(w.shape, F32)
    return pl.pallas_call(body, name=name, out_shape=[shape, shape, shape], compiler_params=_cp())(w, g, m, v)


def _adamw_rows(land, j, rows, w, m, v, name):
    d = w.shape[1]

    def body(l_ref, w_ref, m_ref, v_ref, g_ref, d_ref, nm_ref, nv_ref):
        gv = l_ref[0].astype(F32)
        for s in range(1, N_DEV):
            gv = gv + l_ref[s].astype(F32)
        g_ref[...] = gv
        nm = ADAM_B1 * m_ref[...] + (1.0 - ADAM_B1) * gv
        nv = ADAM_B2 * v_ref[...] + (1.0 - ADAM_B2) * (gv * gv)
        m_hat = nm / (1.0 - ADAM_B1 ** ADAM_STEP)
        v_hat = nv / (1.0 - ADAM_B2 ** ADAM_STEP)
        d_ref[...] = -ADAM_LR * (m_hat / (jnp.sqrt(v_hat) + ADAM_EPS) + ADAM_WD * w_ref[...])
        nm_ref[...] = nm
        nv_ref[...] = nv

    halves = 2
    half = pl.BlockSpec((rows // halves, d), lambda i: (i, 0))
    shape = jax.ShapeDtypeStruct((rows, d), F32)
    return pl.pallas_call(
        body, name=name, grid=(halves,), out_shape=[shape] * 4,
        in_specs=[pl.BlockSpec((N_DEV, rows // halves, d), lambda i: (0, halves * j + i, 0)), half, half, half],
        out_specs=[half] * 4,
        compiler_params=_cp(("arbitrary",)),
    )(land, w, m, v)


def _sum_small(land, name):
    def body(l_ref, o_ref):
        acc = l_ref[0]
        for s in range(1, N_DEV):
            acc = acc + l_ref[s]
        o_ref[...] = acc

    return pl.pallas_call(body, name=name, out_shape=jax.ShapeDtypeStruct(land.shape[1:], F32), compiler_params=_cp())(land)


def kernel(x, positions, norm_ffn1, w_gate1, w_up1, w_down1, norm_mix, w_in, a_sink, w_out, norm_ffn2, w_gate2, w_up2, w_down2, norm_final, loss_target, m_norm_ffn1, m_w_gate1, m_w_up1, m_w_down1, m_norm_mix, m_w_in, m_a_sink, m_w_out, m_norm_ffn2, m_w_gate2, m_w_up2, m_w_down2, m_norm_final, v_norm_ffn1, v_w_gate1, v_w_up1, v_w_down1, v_norm_mix, v_w_in, v_a_sink, v_w_out, v_norm_ffn2, v_w_gate2, v_w_up2, v_w_down2, v_norm_final):
    x = x[0]
    target = loss_target[0]
    t, d = x.shape
    tm_mix = min(512, t)
    tq_of = lambda seg_len, most=2048: min(most, t, max(seg_len, 16 * SUB_ROWS))

    tt = min(1024, t)
    f_all = w_gate1.shape[2] * N_DEV
    tfp = f_all // 2
    tm_fwd, tf_fwd = tm_mix, tfp
    n_steps = (t // tm_fwd) * (f_all // tf_fwd)

    def stacked(shards):
        return jnp.concatenate([s.astype(BF16) for s in shards], axis=0), [s.shape[0] for s in shards]

    packed1, rows1 = stacked([w_gate1[0].T, w_up1[0].T, w_down1[0]])
    later = [w_in[0].T, w_out[0], w_gate2[0].T, w_up2[0].T, w_down2[0]]
    rows2 = [s.shape[0] for s in later]
    (wg1, wu1, wd1), packed2, cos, sin_signed = _run_alone_busy(
        _gather_plan(packed1, rows1, stages=len(rows1)), later, positions[0].astype(F32)[:, None], "gather_ffn1_weights")

    sink = a_sink[0]
    no_sink = jnp.zeros_like(sink)

    (x1, h1, gp1, up1, hid1), (win, wout, wg2, wu2, wd2) = _ffn_fwd(
        x, norm_ffn1, wg1, wu1, wd1, "ffn1_fwd", tm_fwd, tf_fwd, _gather_plan(packed2, rows2),
        [0, n_steps // 2, (3 * n_steps) // 4, n_steps - 1])
    (h_mix, aq, akx, avx, bq, bk, bv), seg_qkv = _in_proj_fwd(x1, norm_mix, win, cos, sin_signed, "in_proj_fwd", tm_mix)
    a_out, a_lse = _attn_fwd(aq, akx, avx, sink, "attn_a_fwd", A_HALF_WINDOW, t, tq_of(t), True)
    rows_of = lambda a: a.reshape(t, B_W)
    segments_of = lambda a, dil: a if dil == 1 else a.reshape(dil, t // dil, B_W)
    b_qkv, b_outs, b_lses = [], [], []
    for n, (window, dil) in enumerate(B_PATTERNS):
        qs, ks, vs = (bq, bk, bv) if dil == 1 else (rows_of(seg_qkv[k][n - 1]) for k in range(3))
        o_seg, lse_seg = _attn_fwd(qs, ks, vs, no_sink, f"attn_b{dil}_fwd", window // (2 * dil), t // dil,
                                  tq_of(t // dil, 16 * SUB_ROWS), False)
        b_qkv.append((qs, ks, vs))
        b_outs.append(segments_of(o_seg, dil))
        b_lses.append(segments_of(lse_seg, dil))
    x2, b_out, b_lse = _merge_out_proj_fwd(x1, a_out, b_outs, b_lses, wout, "out_proj_fwd", tm_mix)
    (dx3, dg_final, sq, dout2, h2, gp2, up2, hid2), _ = _ffn_fwd(
        x2, norm_ffn2, wg2, wu2, wd2, "ffn2_fwd_loss", tm_fwd, tf_fwd, loss_head=(norm_final.reshape(1, d), target))


    (dgt2, dut2, gwd2), _ = _ffn_bwd_hidden(dout2, gp2, up2, hid2, wd2, "ffn2_bwd_hidden", tm_mix, tfp)
    (dx2, dg_ffn2), _ = _ffn_bwd_input(dx3, x2, norm_ffn2, dgt2, dut2, wg2, wu2, "ffn2_bwd_input", tm_mix)
    (gwg2, gwu2), _ = _token_products([dgt2, dut2], h2, "ffn2_bwd_gate_up", tt, tfp)

    n_att = lambda width, tq: (width // LANES) * (t // tq)
    n_prod = (f_all // tfp) * (t // tt)
    da, db, gwout = _out_proj_bwd(dx2, a_out, b_out[0], wout, "out_proj_bwd", tm_mix)
    (daq, dakx, davx, dsink_parts), (land_wg2,) = _attn_bwd(
        aq, akx, avx, a_out, da, a_lse, sink, "attn_a_bwd", A_HALF_WINDOW, t, tq_of(t), True,
        _exchange_plan([[gwg2]]), [0, n_att(A_Q_W, tq_of(t)) - 1])
    pattern_grads, pattern_lands = [], []
    for n, ((window, dil), (qs, ks, vs)) in enumerate(zip(B_PATTERNS, b_qkv)):
        carried = [gwu2, gwd2][n:n + 1]
        grads, lands = _attn_bwd(qs, ks, vs, rows_of(b_out[n]), rows_of(db[n]), rows_of(b_lse[n]), no_sink, f"attn_b{dil}_bwd",
                                 window // (2 * dil), t // dil, tq_of(t // dil), False,
                                 _exchange_plan([carried]) if carried else None, [0, n_att(B_W, tq_of(t // dil)) - 1])
        pattern_grads.append(grads[:3])
        pattern_lands.extend(lands)
    land_wu2, land_wd2 = pattern_lands
    dx1, dg_mix, gwin, dout1 = _in_proj_bwd(dx2, x1, norm_mix, win, h_mix, cos, sin_signed, daq, dakx, davx, pattern_grads, "in_proj_bwd", tm_mix)

    (dgt1, dut1, gwd1), (land_in, land_out) = _ffn_bwd_hidden(
        dout1, gp1, up1, hid1, wd1, "ffn1_bwd_hidden", tm_mix, tfp, _exchange_plan([[gwin], [gwout]]),
        [0, (f_all // tfp) * (t // tm_mix) - 1])
    (gwg1, gwu1), (land_wd1,) = _token_products(
        [dgt1, dut1], h1, "ffn1_bwd_gate_up", tt, tfp, _exchange_plan([[gwd1]]), [0, n_prod - 1])
    (grad_x, dg_ffn1), (land_wg1, land_wu1) = _ffn_bwd_input(
        dx1, x, norm_ffn1, dgt1, dut1, wg1, wu1, "ffn1_bwd_input", tm_mix, _exchange_plan([[gwg1], [gwu1]]), [0, t // tm_mix - 1])

    dsink_pairs = jnp.sum(dsink_parts[:, :, 0, :], axis=1)
    dsink = jnp.stack([dsink_pairs[:, 0], dsink_pairs[:, HEAD_DIM]], axis=1).reshape(1, -1)
    small = jnp.concatenate([dg_ffn1, dg_mix, dg_ffn2, dg_final, jnp.pad(dsink, ((0, 0), (0, d - dsink.shape[1]))),
                             sq, jnp.zeros((2, d), F32)], axis=0)
    (land_small,) = _run_alone(_exchange_plan([[jnp.tile(small, (N_DEV, 1))]]), "gather_small_gradients")
    red_small = _sum_small(land_small, "sum_small_grads")
    loss = 0.5 * jnp.sum(red_small[5]) / d

    rf = rows1[0]
    sharded = {"w_gate1": (land_wg1, 0, rf, True), "w_up1": (land_wu1, 0, rf, True), "w_down1": (land_wd1, 0, rf, False),
               "w_in": (land_in, 0, rows2[0], True), "w_out": (land_out, 0, rows2[1], False),
               "w_gate2": (land_wg2, 0, rf, True), "w_up2": (land_wu2, 0, rf, True), "w_down2": (land_wd2, 0, rf, False)}
    n_sink = a_sink.shape[1]
    small_grads = {"norm_ffn1": red_small[0:1], "norm_mix": red_small[1:2], "norm_ffn2": red_small[2:3], "norm_final": red_small[3],
                   "a_sink": red_small[4:5, :n_sink]}
    params = {
        "norm_ffn1": (norm_ffn1, m_norm_ffn1, v_norm_ffn1), "w_gate1": (w_gate1, m_w_gate1, v_w_gate1),
        "w_up1": (w_up1, m_w_up1, v_w_up1), "w_down1": (w_down1, m_w_down1, v_w_down1),
        "norm_mix": (norm_mix, m_norm_mix, v_norm_mix), "w_in": (w_in, m_w_in, v_w_in),
        "a_sink": (a_sink, m_a_sink, v_a_sink), "w_out": (w_out, m_w_out, v_w_out),
        "norm_ffn2": (norm_ffn2, m_norm_ffn2, v_norm_ffn2), "w_gate2": (w_gate2, m_w_gate2, v_w_gate2),
        "w_up2": (w_up2, m_w_up2, v_w_up2), "w_down2": (w_down2, m_w_down2, v_w_down2),
        "norm_final": (norm_final, m_norm_final, v_norm_final),
    }
    grad_list, deltas, new_ms, new_vs = [], [], [], []
    for name, (w, m, v) in params.items():
        if name in sharded:
            land, j, rows, is_transposed = sharded[name]
            view = (lambda a: a[0].T) if is_transposed else (lambda a: a[0])
            back = (lambda a: a.T[None]) if is_transposed else (lambda a: a[None])
            outs = [back(o) for o in _adamw_rows(land, j, rows, view(w), view(m), view(v), f"adamw_{name}")]
        else:
            as_block = (lambda a: a.reshape(1, -1)) if w.ndim == 1 else (lambda a: a)
            g = small_grads[name]
            outs = [g] + [o.reshape(w.shape) for o in _adamw(as_block(w), as_block(g), as_block(m), as_block(v), f"adamw_{name}")]
        for lst, o in zip((grad_list, deltas, new_ms, new_vs), outs):
            lst.append(o)
    return (loss, grad_x[None], *grad_list, *deltas, *new_ms, *new_vs)
```

```python
import functools
import itertools

import numpy as np
import jax
import jax.numpy as jnp
from jax import lax
from jax.experimental import pallas as pl
from jax.experimental.pallas import tpu as pltpu

F32 = jnp.float32
BF16 = jnp.bfloat16

N_DEV = 8
HEAD_DIM = 64
LANES = 128
A_Q_W, A_KV_W, B_W = 512, 128, 512
A_HALF_WINDOW = 128
B_PATTERNS = ((128, 1), (512, 4), (2048, 16))
ROPE_THETA = 10000.0
NORM_EPS = 1e-6
FFN_RES_WEIGHT = 0.5
QK_SCALE = HEAD_DIM ** -0.5
NEG = -1e30

ADAM_LR = 0.001
ADAM_B1 = 0.9
ADAM_B2 = 0.999
ADAM_EPS = 1e-08
ADAM_WD = 0.01
ADAM_STEP = 10

MESH_T = pl.DeviceIdType.MESH
VMEM_LIMIT = 60 * 1024 * 1024


def _cp(sem=None, vmem=VMEM_LIMIT):
    return pltpu.CompilerParams(dimension_semantics=sem, vmem_limit_bytes=vmem)


def _dot_nn(a, b):
    return jnp.dot(a, b, preferred_element_type=F32)


def _dot_nt(a, b):
    return lax.dot_general(a, b, (((1,), (1,)), ((), ())), preferred_element_type=F32)


def _dot_tn(a, b):
    return lax.dot_general(a, b, (((0,), (0,)), ((), ())), preferred_element_type=F32)


def _rstd(xv):
    return lax.rsqrt(jnp.mean(xv * xv, axis=-1, keepdims=True) + NORM_EPS)


def _norm_bwd(dh, xv, r, gn):
    gy = dh * gn
    c = jnp.sum(gy * xv, axis=-1, keepdims=True) * (1.0 / xv.shape[-1])
    dx = r * gy - xv * (r * r * r * c)
    dgn = jnp.sum(dh * (xv * r), axis=0, keepdims=True)
    return dx, dgn


def _accumulate(ref, val, first):
    @pl.when(first)
    def _():
        ref[...] = val

    @pl.when(jnp.logical_not(first))
    def _():
        ref[...] += val


def _mesh_pos():
    return lax.axis_index("x"), lax.axis_index("y"), lax.axis_index("c")


def _dev_index(d):
    return 4 * d[0] + 2 * d[1] + d[2]


class _Comm:
    def __init__(self, inputs, out_shape, scratch, phases):
        self.inputs, self.out_shape, self.scratch, self.phases = inputs, out_shape, scratch, phases

    def specs(self):
        any_spec = pl.BlockSpec(memory_space=pl.ANY)
        return [any_spec] * len(self.inputs), [any_spec] * len(self.out_shape)


def _run_alone(comm, name):
    n_in, n_out = len(comm.inputs), len(comm.out_shape)

    def body(*refs):
        for phase in comm.phases:
            phase(refs[:n_in], refs[n_in:n_in + n_out], refs[n_in + n_out:])

    in_specs, out_specs = comm.specs()
    return pl.pallas_call(body, name=name, out_shape=comm.out_shape, in_specs=in_specs, out_specs=out_specs,
                          scratch_shapes=comm.scratch)(*comm.inputs)


def _run_alone_busy(comm, shards, positions, name):
    n_in, n_out = len(comm.inputs), len(comm.out_shape)
    offs = [int(o) for o in np.cumsum([0] + [s.shape[0] for s in shards])]
    t = positions.shape[0]
    half = HEAD_DIM // 2
    inv_freq = 1.0 / (ROPE_THETA ** (jnp.arange(0, HEAD_DIM, 2, dtype=F32) / HEAD_DIM))
    lanes = jnp.stack([jnp.tile(inv_freq, LANES // half),
                       jnp.tile(jnp.concatenate([-jnp.ones(half, F32), jnp.ones(half, F32)]), LANES // HEAD_DIM)])

    def body(*refs):
        c_in, s_refs, (pos_ref, lanes_ref), c_out, (stack_ref, cos_ref, sin_ref), scr = _split(refs, n_in, len(shards), 2, n_out, 3)
        comm.phases[0](c_in, c_out, scr)
        for s_ref, lo, hi in zip(s_refs, offs[:-1], offs[1:]):
            stack_ref[lo:hi, :] = s_ref[...].astype(BF16)
        ang = pos_ref[...] * lanes_ref[0:1, :]
        cos_ref[...] = jnp.cos(ang)
        sin_ref[...] = jnp.sin(ang) * lanes_ref[1:2, :]
        for phase in comm.phases[1:]:
            phase(c_in, c_out, scr)

    in_specs, out_specs = comm.specs()
    vmem = pl.BlockSpec(memory_space=pltpu.VMEM)
    table = jax.ShapeDtypeStruct((t, LANES), F32)
    res = pl.pallas_call(
        body, name=name, out_shape=comm.out_shape + [jax.ShapeDtypeStruct((offs[-1], shards[0].shape[1]), BF16), table, table],
        in_specs=in_specs + [vmem] * (len(shards) + 2), out_specs=out_specs + [vmem] * 3, scratch_shapes=comm.scratch,
        compiler_params=_cp())(*comm.inputs, *shards, positions, lanes)
    return res[:n_out], res[n_out], res[n_out + 1], res[n_out + 2]


def _run_hosted(comm, at, step, ins, outs, scr):
    for phase, when in zip(comm.phases, at):
        @pl.when(step == when)
        def _(phase=phase):
            phase(ins, outs, scr)


def _split(refs, *counts):
    parts, o = [], 0
    for n in counts:
        parts.append(refs[o:o + n])
        o += n
    return parts + [refs[o:]]


def _gather_plan(packed, rows_list, stages=1):
    n_w = len(rows_list)
    offs = [int(o) for o in np.cumsum([0] + list(rows_list))]
    d = packed.shape[1]
    per = n_w // stages
    groups = [range(s * per, (s + 1) * per) for s in range(stages)]

    def tools(ins, outs, scr):
        p_ref = ins[0]
        send_sems, recv_sems, local_sem = scr
        x, y, c = _mesh_pos()
        me, sibling = (x, y, c), (x, y, 1 - c)
        chips = [(1 - x, y), (x, 1 - y), (1 - x, 1 - y)]

        def rows(w, dev):
            start = pl.multiple_of(_dev_index(dev) * rows_list[w], 16)
            return outs[w].at[pl.ds(start, rows_list[w]), :]

        def mine(w):
            return p_ref.at[pl.ds(offs[w], rows_list[w]), :]

        def copy(k, s, w, block, to, own):
            return pltpu.make_async_remote_copy(
                src_ref=mine(w) if own else rows(w, block), dst_ref=rows(w, block),
                send_sem=send_sems.at[k, s], recv_sem=recv_sems.at[k, s], device_id=to, device_id_type=MESH_T)

        def whole_group(k, s):
            span = p_ref.at[pl.ds(offs[groups[s][0]], offs[groups[s][-1] + 1] - offs[groups[s][0]]), :]
            return pltpu.make_async_remote_copy(
                src_ref=span, dst_ref=span, send_sem=send_sems.at[k, s], recv_sem=recv_sems.at[k, s],
                device_id=me, device_id_type=MESH_T)

        return p_ref, local_sem, me, sibling, chips, c, rows, mine, copy, whole_group

    def start(ins, outs, scr):
        _, local_sem, me, sibling, chips, c, rows, mine, copy, _ = tools(ins, outs, scr)
        for w in range(n_w):
            pltpu.make_async_copy(mine(w), rows(w, me), local_sem).start()
        for s, group in enumerate(groups):
            for w in group:
                copy(0, s, w, me, sibling, True).start()
            for j, chip in enumerate(chips[:2]):
                for w in group:
                    copy(1 + j, s, w, me, (*chip, c), True).start()

    def forward(ins, outs, scr):
        _, _, (x, y, _), sibling, chips, c, _, _, copy, whole_group = tools(ins, outs, scr)
        south = c == 0
        source = (jnp.where(south, x, 1 - x), jnp.where(south, 1 - y, y), c)
        target = (jnp.where(south, 1 - x, x), jnp.where(south, y, 1 - y), c)
        for s, group in enumerate(groups):
            for j, chip in enumerate(chips[:2]):
                whole_group(1 + j, s).wait_recv()
            for w in group:
                copy(3, s, w, source, target, False).start()
            for j, chip in enumerate(chips[:2]):
                for w in group:
                    copy(4 + j, s, w, (*chip, c), sibling, False).start()

    def relay(ins, outs, scr):
        _, _, _, sibling, chips, c, _, _, copy, whole_group = tools(ins, outs, scr)
        for s, group in enumerate(groups):
            whole_group(3, s).wait_recv()
            for w in group:
                copy(6, s, w, (*chips[2], c), sibling, False).start()

    def finish(ins, outs, scr):
        p_ref, local_sem, _, _, _, _, _, _, _, whole_group = tools(ins, outs, scr)
        for s in range(stages):
            whole_group(0, s).wait_recv()
            for j in range(3):
                whole_group(4 + j, s).wait_recv()
            for k in range(7):
                whole_group(k, s).wait_send()
        pltpu.make_async_copy(p_ref, p_ref, local_sem).wait()

    return _Comm(
        [packed], [jax.ShapeDtypeStruct((N_DEV * r, d), packed.dtype) for r in rows_list],
        [pltpu.SemaphoreType.DMA((7, stages)), pltpu.SemaphoreType.DMA((7, stages)), pltpu.SemaphoreType.DMA],
        [start, forward, relay, finish])


def _exchange_plan(groups):
    flat = [a for g in groups for a in g]
    n_g = len(groups)
    sizes = [len(g) for g in groups]
    rows = [g[0].shape[0] // N_DEV for g in groups]
    first = [int(o) for o in np.cumsum([0] + sizes[:-1])]

    def start(srcs, lands, scr):
        send_sems, recv_sems, local_sems = scr
        x, y, c = _mesh_pos()
        me = (x, y, c)
        me_idx = _dev_index(me)

        def block(g, i, dev):
            start_row = pl.multiple_of(_dev_index(dev) * rows[g], 8)
            return srcs[first[g] + i].at[pl.ds(start_row, rows[g]), :]

        def slot(g, i):
            return lands[g].at[me_idx, pl.ds(i * rows[g], rows[g]), :]

        for g in range(n_g):
            for i in range(sizes[g]):
                pltpu.make_async_copy(block(g, i, me), slot(g, i), local_sems.at[g]).start()
        flips = [f for f in itertools.product((0, 1), repeat=3) if any(f)]
        for k, (fx, fy, fc) in enumerate(flips):
            peer = (1 - x if fx else x, 1 - y if fy else y, 1 - c if fc else c)
            for g in range(n_g):
                for i in range(sizes[g]):
                    pltpu.make_async_remote_copy(
                        src_ref=block(g, i, peer), dst_ref=slot(g, i), send_sem=send_sems.at[g, k],
                        recv_sem=recv_sems.at[g, k], device_id=peer, device_id_type=MESH_T).start()

    def finish(srcs, lands, scr):
        send_sems, recv_sems, local_sems = scr
        me = _mesh_pos()
        for k in range(7):
            for g in range(n_g):
                pltpu.make_async_remote_copy(
                    src_ref=lands[g].at[0], dst_ref=lands[g].at[0], send_sem=send_sems.at[g, k],
                    recv_sem=recv_sems.at[g, k], device_id=me, device_id_type=MESH_T).wait()
        for g in range(n_g):
            pltpu.make_async_copy(lands[g].at[0], lands[g].at[0], local_sems.at[g]).wait()

    return _Comm(
        flat, [jax.ShapeDtypeStruct((N_DEV, sizes[g] * rows[g], groups[g][0].shape[1]), groups[g][0].dtype) for g in range(n_g)],
        [pltpu.SemaphoreType.DMA((n_g, 7)), pltpu.SemaphoreType.DMA((n_g, 7)), pltpu.SemaphoreType.DMA((n_g,))],
        [start, finish])


def _comm_parts(comm):
    if comm is None:
        return 0, 0, [], [], [], [], []
    in_specs, out_specs = comm.specs()
    return len(comm.inputs), len(comm.out_shape), in_specs, out_specs, comm.out_shape, comm.scratch, comm.inputs


def _ffn_fwd(x, gn, wg_t, wu_t, wd, name, tm, tf, comm=None, comm_at=None, loss_head=None):
    t, d = x.shape
    f_all = wg_t.shape[0]
    n_f = f_all // tf
    n_ci, n_co, c_in_specs, c_out_specs, c_shapes, c_scratch, c_inputs = _comm_parts(comm)
    n_head, n_y = (2, 4) if loss_head else (0, 1)

    def body(*refs):
        ((x_ref, gn_ref, wg_ref, wu_ref, wd_ref), head_in, c_in, y_outs, (h_ref, gp_ref, up_ref, hid_ref), c_out,
         (h_s, hid_s), c_scr) = _split(refs, 5, n_head, n_ci, n_y, 4, n_co, 2)
        f = pl.program_id(1)
        if comm:
            _run_hosted(comm, comm_at, pl.program_id(0) * n_f + f, c_in, c_out, c_scr)

        @pl.when(f == 0)
        def _():
            xv = x_ref[...]
            h = ((xv * _rstd(xv)) * gn_ref[...]).astype(BF16)
            h_s[...] = h
            h_ref[...] = h

        h = h_s[...]
        cols = pl.ds(pl.multiple_of(f * tf, tf), tf)
        g = _dot_nt(h, wg_ref[cols, :])
        u = _dot_nt(h, wu_ref[cols, :])
        sg = jax.nn.sigmoid(g)
        silu = g * sg
        gp_ref[...] = (u * (sg * (1.0 + g * (1.0 - sg)))).astype(BF16)
        up_ref[...] = silu.astype(BF16)
        hid = (silu * u).astype(BF16)
        hid_ref[...] = hid
        for f0 in range(n_f):
            @pl.when(f == f0)
            def _(f0=f0):
                hid_s[:, f0 * tf:(f0 + 1) * tf] = hid

        @pl.when(f == n_f - 1)
        def _():
            y = x_ref[...] + FFN_RES_WEIGHT * _dot_nn(hid_s[...], wd_ref[...])
            if not loss_head:
                y_outs[0][...] = y
            else:
                (gf_ref, tg_ref), (dy_ref, dgf_ref, sq_ref, half_ref) = head_in, y_outs
                gfv, r, first = gf_ref[...], _rstd(y), pl.program_id(0) == 0
                diff = (y * r) * gfv - tg_ref[...]
                dy, dgf = _norm_bwd(diff * (1.0 / d), y, r, gfv)
                dy_ref[...] = dy
                half_ref[...] = (FFN_RES_WEIGHT * dy).astype(BF16)
                _accumulate(dgf_ref, dgf, first)
                _accumulate(sq_ref, jnp.sum(diff * diff, axis=0, keepdims=True), first)

    tok = pl.BlockSpec((tm, d), lambda i, f: (i, 0))
    row = pl.BlockSpec((1, d), lambda i, f: (0, 0))
    whole = pl.BlockSpec((f_all, d), lambda i, f: (0, 0), pipeline_mode=pl.Buffered(1))
    act = pl.BlockSpec((tm, tf), lambda i, f: (i, f))
    act_shape = jax.ShapeDtypeStruct((t, f_all), BF16)
    tok_f32, tok_bf16, row_f32 = (jax.ShapeDtypeStruct((t, d), F32), jax.ShapeDtypeStruct((t, d), BF16),
                                  jax.ShapeDtypeStruct((1, d), F32))
    y_shapes, y_specs = ([tok_f32, row_f32, row_f32, tok_bf16], [tok, row, row, tok]) if loss_head else ([tok_f32], [tok])
    res = pl.pallas_call(
        body, name=name, grid=(t // tm, n_f),
        out_shape=y_shapes + [tok_bf16, act_shape, act_shape, act_shape] + c_shapes,
        in_specs=[tok, row, whole, whole, whole] + ([row, tok] if loss_head else []) + c_in_specs,
        out_specs=y_specs + [tok, act, act, act] + c_out_specs,
        scratch_shapes=[pltpu.VMEM((tm, d), BF16), pltpu.VMEM((tm, f_all), BF16)] + c_scratch,
        compiler_params=_cp(("arbitrary", "arbitrary")),
    )(x, gn, wg_t, wu_t, wd, *(loss_head or ()), *c_inputs)
    return res[:n_y + 4], res[n_y + 4:]


def _ffn_bwd_hidden(dout, gp, up, hid, wd, name, tt, tf, comm=None, comm_at=None):
    t, d = dout.shape
    f_all = wd.shape[0]
    n_t = t // tt
    n_ci, n_co, c_in_specs, c_out_specs, c_shapes, c_scratch, c_inputs = _comm_parts(comm)

    def body(*refs):
        (dout_ref, gp_ref, up_ref, hid_ref, wd_ref), c_in, (dg_ref, du_ref, gwd_ref), c_out, (acc,), c_scr = _split(
            refs, 5, n_ci, 3, n_co, 1)
        s = pl.program_id(1)
        if comm:
            _run_hosted(comm, comm_at, pl.program_id(0) * n_t + s, c_in, c_out, c_scr)
        doutv = dout_ref[...]
        dhid = _dot_nt(doutv, wd_ref[...])
        dg_ref[...] = (dhid * gp_ref[...].astype(F32)).astype(BF16)
        du_ref[...] = (dhid * up_ref[...].astype(F32)).astype(BF16)
        _accumulate(acc, _dot_tn(hid_ref[...], doutv), s == 0)

        @pl.when(s == n_t - 1)
        def _():
            gwd_ref[...] = acc[...].astype(BF16)

    tok = pl.BlockSpec((tt, d), lambda f, s: (s, 0))
    wblk = pl.BlockSpec((tf, d), lambda f, s: (f, 0))
    act = pl.BlockSpec((tt, tf), lambda f, s: (s, f))
    act_shape = jax.ShapeDtypeStruct((t, f_all), BF16)
    res = pl.pallas_call(
        body, name=name, grid=(f_all // tf, n_t),
        out_shape=[act_shape, act_shape, jax.ShapeDtypeStruct((f_all, d), BF16)] + c_shapes,
        in_specs=[tok, act, act, act, wblk] + c_in_specs, out_specs=[act, act, wblk] + c_out_specs,
        scratch_shapes=[pltpu.VMEM((tf, d), F32)] + c_scratch,
        compiler_params=_cp(("arbitrary", "arbitrary")),
    )(dout, gp, up, hid, wd, *c_inputs)
    return res[:3], res[3:]


def _ffn_bwd_input(dy, x, gn, dg, du, wg_t, wu_t, name, tm, comm=None, comm_at=None):
    t, d = x.shape
    f_all = wg_t.shape[0]
    n_ci, n_co, c_in_specs, c_out_specs, c_shapes, c_scratch, c_inputs = _comm_parts(comm)

    def body(*refs):
        (dy_ref, x_ref, gn_ref, dg_ref, du_ref, wg_ref, wu_ref), c_in, (dx_ref, dgn_ref), c_out, c_scr = _split(
            refs, 7, n_ci, 2, n_co)
        i = pl.program_id(0)
        if comm:
            _run_hosted(comm, comm_at, i, c_in, c_out, c_scr)
        dh = _dot_nn(dg_ref[...], wg_ref[...]) + _dot_nn(du_ref[...], wu_ref[...])
        xv = x_ref[...]
        dxn, dgn = _norm_bwd(dh, xv, _rstd(xv), gn_ref[...])
        dx_ref[...] = dy_ref[...] + dxn
        _accumulate(dgn_ref, dgn, i == 0)

    tok = pl.BlockSpec((tm, d), lambda i: (i, 0))
    row = pl.BlockSpec((1, d), lambda i: (0, 0))
    act = pl.BlockSpec((tm, f_all), lambda i: (i, 0))
    whole = pl.BlockSpec((f_all, d), lambda i: (0, 0))
    res = pl.pallas_call(
        body, name=name, grid=(t // tm,),
        out_shape=[jax.ShapeDtypeStruct((t, d), F32), jax.ShapeDtypeStruct((1, d), F32)] + c_shapes,
        in_specs=[tok, tok, row, act, act, whole, whole] + c_in_specs,
        out_specs=[tok, row] + c_out_specs,
        scratch_shapes=c_scratch,
        compiler_params=_cp(("arbitrary",)),
    )(dy, x, gn, dg, du, wg_t, wu_t, *c_inputs)
    return res[:2], res[2:]


def _token_products(lhs_list, rhs, name, tt, tf, comm=None, comm_at=None):
    n_l = len(lhs_list)
    t, f_all = lhs_list[0].shape
    d = rhs.shape[1]
    n_t = t // tt
    n_ci, n_co, c_in_specs, c_out_specs, c_shapes, c_scratch, c_inputs = _comm_parts(comm)

    def body(*refs):
        lhs_refs, (rhs_ref,), c_in, out_refs, c_out, accs, c_scr = _split(refs, n_l, 1, n_ci, n_l, n_co, n_l)
        s = pl.program_id(1)
        if comm:
            _run_hosted(comm, comm_at, pl.program_id(0) * n_t + s, c_in, c_out, c_scr)
        rv = rhs_ref[...]
        for l_ref, acc in zip(lhs_refs, accs):
            _accumulate(acc, _dot_tn(l_ref[...], rv), s == 0)

        @pl.when(s == n_t - 1)
        def _():
            for o_ref, acc in zip(out_refs, accs):
                o_ref[...] = acc[...].astype(BF16)

    act = pl.BlockSpec((tt, tf), lambda f, s: (s, f))
    tok = pl.BlockSpec((tt, d), lambda f, s: (s, 0))
    wblk = pl.BlockSpec((tf, d), lambda f, s: (f, 0))
    res = pl.pallas_call(
        body, name=name, grid=(f_all // tf, n_t),
        out_shape=[jax.ShapeDtypeStruct((f_all, d), BF16)] * n_l + c_shapes,
        in_specs=[act] * n_l + [tok] + c_in_specs, out_specs=[wblk] * n_l + c_out_specs,
        scratch_shapes=[pltpu.VMEM((tf, d), F32)] * n_l + c_scratch,
        compiler_params=_cp(("arbitrary", "arbitrary")),
    )(*lhs_list, rhs, *c_inputs)
    return res[:n_l], res[n_l:]


def _swap_halves(t):
    w = t.shape[-1]
    lane = lax.broadcasted_iota(jnp.int32, (1, w), 1)
    return jnp.where((lane % HEAD_DIM) < HEAD_DIM // 2, pltpu.roll(t, w - HEAD_DIM // 2, 1), pltpu.roll(t, HEAD_DIM // 2, 1))


def _rope(t, cos, sin_signed):
    reps = t.shape[-1] // LANES
    return t * jnp.tile(cos, (1, reps)) + _swap_halves(t) * jnp.tile(sin_signed, (1, reps))


def _rope_bwd(dt, cos, sin_signed):
    reps = dt.shape[-1] // LANES
    return dt * jnp.tile(cos, (1, reps)) + _swap_halves(dt * jnp.tile(sin_signed, (1, reps)))


DILATIONS = tuple(dil for _, dil in B_PATTERNS if dil > 1)


def _seg_shape(t, dil, w, dtype):
    return jax.ShapeDtypeStruct((dil, t // dil, w), dtype)


def _seg_spec(tm, dil, w):
    return pl.BlockSpec((dil, tm // dil, w), lambda i: (0, i, 0))


def _tile_scratch(tm, w):
    return [pltpu.VMEM((tm, LANES), F32)] * (w // LANES)


def _put_tile(tile, val):
    for c, ref in enumerate(tile):
        ref[...] = val[:, c * LANES:(c + 1) * LANES]


def _get_tile(tile):
    return jnp.concatenate([ref[...] for ref in tile], axis=1)


def _scatter_to_segments(tile, seg_refs):
    for seg_ref, dil in zip(seg_refs, DILATIONS):
        rows = tile[0].shape[0] // dil
        for r in range(dil):
            for c, ref in enumerate(tile):
                seg_ref[r, :, c * LANES:(c + 1) * LANES] = ref[pl.ds(r, rows, stride=dil), :].astype(seg_ref.dtype)


def _gather_from_segments(seg_ref, dil, tile, add=False):
    rows = tile[0].shape[0] // dil
    for r in range(dil):
        for c, ref in enumerate(tile):
            idx = (pl.ds(r, rows, stride=dil), slice(None))
            v = seg_ref[r, :, c * LANES:(c + 1) * LANES].astype(F32)
            ref[idx] = ref[idx] + v if add else v


def _in_proj_fwd(x, gn, win_t, cos, sin_signed, name, tm):
    t, d = x.shape
    in_w = win_t.shape[0]
    n_dil = len(DILATIONS)

    def body(x_ref, gn_ref, w_ref, cos_ref, sin_ref, h_ref, aq_ref, akx_ref, avx_ref, bq_ref, bk_ref, bv_ref, *rest):
        seg_refs, tile = rest[:3 * n_dil], rest[3 * n_dil:]
        xv = x_ref[...]
        h = ((xv * _rstd(xv)) * gn_ref[...]).astype(BF16)
        h_ref[...] = h
        p = _dot_nt(h, w_ref[...])
        cs, sn = cos_ref[...], sin_ref[...]
        o = 0
        aq_ref[...] = (_rope(p[:, o:o + A_Q_W], cs, sn) * QK_SCALE).astype(BF16)
        o += A_Q_W
        ak = _rope(p[:, o:o + A_KV_W], cs, sn)
        o += A_KV_W
        av = p[:, o:o + A_KV_W]
        o += A_KV_W
        low = lax.broadcasted_iota(jnp.int32, (1, LANES), 1) < HEAD_DIM
        for src, dst in ((ak, akx_ref), (av, avx_ref)):
            other = pltpu.roll(src, HEAD_DIM, 1)
            dst[0] = jnp.where(low, src, other).astype(BF16)
            dst[1] = jnp.where(low, other, src).astype(BF16)
        for k, nat_ref in enumerate((bq_ref, bk_ref, bv_ref)):
            val = p[:, o:o + B_W]
            o += B_W
            if k < 2:
                val = _rope(val, cs, sn)
            if k == 0:
                val = val * QK_SCALE
            nat_ref[...] = val.astype(BF16)
            _put_tile(tile, val)
            _scatter_to_segments(tile, seg_refs[k * n_dil:(k + 1) * n_dil])

    tok = lambda w: pl.BlockSpec((tm, w), lambda i: (i, 0))
    kvx = pl.BlockSpec((2, tm, LANES), lambda i: (0, i, 0))
    sd = lambda *s: jax.ShapeDtypeStruct(s, BF16)
    res = pl.pallas_call(
        body, name=name, grid=(t // tm,),
        out_shape=[sd(t, d), sd(t, A_Q_W), sd(2, t, LANES), sd(2, t, LANES), sd(t, B_W), sd(t, B_W), sd(t, B_W)]
        + [_seg_shape(t, dil, B_W, BF16) for _ in range(3) for dil in DILATIONS],
        in_specs=[tok(d), pl.BlockSpec((1, d), lambda i: (0, 0)), pl.BlockSpec((in_w, d), lambda i: (0, 0)),
                  tok(LANES), tok(LANES)],
        out_specs=[tok(d), tok(A_Q_W), kvx, kvx, tok(B_W), tok(B_W), tok(B_W)]
        + [_seg_spec(tm, dil, B_W) for _ in range(3) for dil in DILATIONS],
        scratch_shapes=_tile_scratch(tm, B_W),
        compiler_params=_cp(("arbitrary",)),
    )(x, gn, win_t, cos, sin_signed)
    return res[:7], [res[7 + k * n_dil:7 + (k + 1) * n_dil] for k in range(3)]


def _in_proj_bwd(dres, x, gn, win_t, h, cos, sin_signed, daq, dakx, davx, per_pattern, name, tm):
    t, d = x.shape
    in_w = win_t.shape[0]
    n_t = t // tm
    chunk = in_w // 3
    n_pat = 3 * (1 + len(DILATIONS))

    def body(*refs):
        ((dres_ref, x_ref, gn_ref, w_ref, h_ref, cos_ref, sin_ref, daq_ref, dakx_ref, davx_ref), pat,
         (dx_ref, dgn_ref, gw_ref, half_ref), (dp_s, gw_s), tile) = _split(refs, 10, n_pat, 4, 2)
        i = pl.program_id(0)
        cs, sn = cos_ref[...], sin_ref[...]
        low = lax.broadcasted_iota(jnp.int32, (1, LANES), 1) < HEAD_DIM

        def fold(ref):
            a, b = ref[0].astype(F32), ref[1].astype(F32)
            return jnp.where(low, a + pltpu.roll(a, HEAD_DIM, 1), b + pltpu.roll(b, HEAD_DIM, 1))

        def summed(j):
            _put_tile(tile, pat[j][...].astype(F32))
            for k, dil in enumerate(DILATIONS):
                _gather_from_segments(pat[3 * (1 + k) + j], dil, tile, add=True)
            return _get_tile(tile)

        o = 0
        dp_s[:, o:o + A_Q_W] = _rope_bwd(daq_ref[...].astype(F32) * QK_SCALE, cs, sn).astype(BF16)
        o += A_Q_W
        dp_s[:, o:o + A_KV_W] = _rope_bwd(fold(dakx_ref), cs, sn).astype(BF16)
        o += A_KV_W
        dp_s[:, o:o + A_KV_W] = fold(davx_ref).astype(BF16)
        o += A_KV_W
        dp_s[:, o:o + B_W] = _rope_bwd(summed(0) * QK_SCALE, cs, sn).astype(BF16)
        o += B_W
        dp_s[:, o:o + B_W] = _rope_bwd(summed(1), cs, sn).astype(BF16)
        o += B_W
        dp_s[:, o:o + B_W] = summed(2).astype(BF16)
        dh = _dot_nn(dp_s[...], w_ref[...])
        hv = h_ref[...]
        for c0 in range(0, in_w, chunk):
            _accumulate(gw_s.at[pl.ds(c0, chunk), :], _dot_tn(dp_s[:, c0:c0 + chunk], hv), i == 0)
        xv = x_ref[...]
        dxn, dgn = _norm_bwd(dh, xv, _rstd(xv), gn_ref[...])
        dx = dres_ref[...] + dxn
        dx_ref[...] = dx
        half_ref[...] = (FFN_RES_WEIGHT * dx).astype(BF16)
        _accumulate(dgn_ref, dgn, i == 0)

        @pl.when(i == n_t - 1)
        def _():
            gw_ref[...] = gw_s[...].astype(BF16)

    tok = lambda w: pl.BlockSpec((tm, w), lambda i: (i, 0))
    row = pl.BlockSpec((1, d), lambda i: (0, 0))
    whole = pl.BlockSpec((in_w, d), lambda i: (0, 0))
    weight = pl.BlockSpec((in_w, d), lambda i: (0, 0), pipeline_mode=pl.Buffered(1))
    kvx = pl.BlockSpec((2, tm, LANES), lambda i: (0, i, 0))
    flat = [a if k == 0 else a.reshape(DILATIONS[k - 1], -1, B_W) for k, grads in enumerate(per_pattern) for a in grads]
    return pl.pallas_call(
        body, name=name, grid=(n_t,),
        out_shape=[jax.ShapeDtypeStruct((t, d), F32), jax.ShapeDtypeStruct((1, d), F32),
                   jax.ShapeDtypeStruct((in_w, d), BF16), jax.ShapeDtypeStruct((t, d), BF16)],
        in_specs=[tok(d), tok(d), row, weight, tok(d), tok(LANES), tok(LANES), tok(A_Q_W), kvx, kvx]
        + [tok(B_W)] * 3 + [_seg_spec(tm, dil, B_W) for dil in DILATIONS for _ in range(3)],
        out_specs=[tok(d), row, whole, tok(d)],
        scratch_shapes=[pltpu.VMEM((tm, in_w), BF16), pltpu.VMEM((in_w, d), F32)] + _tile_scratch(tm, B_W),
        compiler_params=_cp(("arbitrary",)),
    )(dres, x, gn, win_t, h, cos, sin_signed, daq, dakx, davx, *flat)


def _merge_out_proj_fwd(x, a_out, outs, lses, wout, name, tm):
    t, d = x.shape
    n_dil = len(DILATIONS)

    def body(x_ref, a_ref, *rest):
        o_refs, l_refs, (w_ref, y_ref, b_ref, lt_ref), b_segs, lt_segs, scratch = _split(rest, 1 + n_dil, 1 + n_dil, 4, n_dil, n_dil)
        n_c = B_W // LANES
        tiles = [scratch[j * n_c:(j + 1) * n_c] for j in range(2 * n_dil)]
        os_, ls = [o_refs[0][...].astype(F32)], [l_refs[0][...]]
        for k, dil in enumerate(DILATIONS):
            _gather_from_segments(o_refs[1 + k], dil, tiles[2 * k])
            _gather_from_segments(l_refs[1 + k], dil, tiles[2 * k + 1])
            os_.append(_get_tile(tiles[2 * k]))
            ls.append(_get_tile(tiles[2 * k + 1]))
        mx = functools.reduce(jnp.maximum, ls)
        es = [jnp.exp(l - mx) for l in ls]
        den = functools.reduce(jnp.add, es)
        b = functools.reduce(jnp.add, [e * o for e, o in zip(es, os_)]) / den
        lt = mx + jnp.log(den)
        bb = b.astype(BF16)
        b_ref[...] = bb
        lt_ref[...] = lt
        y_ref[...] = x_ref[...] + _dot_nn(a_ref[...], w_ref[0:A_Q_W, :]) + _dot_nn(bb, w_ref[A_Q_W:A_Q_W + B_W, :])
        _put_tile(tiles[0], b)
        _scatter_to_segments(tiles[0], b_segs)
        _put_tile(tiles[1], lt)
        _scatter_to_segments(tiles[1], lt_segs)

    tok = lambda w: pl.BlockSpec((tm, w), lambda i: (i, 0))
    segs = [_seg_spec(tm, dil, B_W) for dil in DILATIONS]
    res = pl.pallas_call(
        body, name=name, grid=(t // tm,),
        out_shape=[jax.ShapeDtypeStruct((t, d), F32), jax.ShapeDtypeStruct((t, B_W), BF16), jax.ShapeDtypeStruct((t, B_W), F32)]
        + [_seg_shape(t, dil, B_W, BF16) for dil in DILATIONS] + [_seg_shape(t, dil, B_W, F32) for dil in DILATIONS],
        in_specs=[tok(d), tok(A_Q_W)] + ([tok(B_W)] + segs) * 2 + [pl.BlockSpec(wout.shape, lambda i: (0, 0))],
        out_specs=[tok(d), tok(B_W), tok(B_W)] + segs * 2,
        scratch_shapes=_tile_scratch(tm, B_W) * (2 * n_dil),
        compiler_params=_cp(("arbitrary",)),
    )(x, a_out, *outs, *lses, wout)
    return res[0], [res[1]] + list(res[3:3 + n_dil]), [res[2]] + list(res[3 + n_dil:])


def _out_proj_bwd(dy, a_out, b_out, wout, name, tm):
    t, d = dy.shape
    n_t = t // tm
    n_dil = len(DILATIONS)

    def body(dy_ref, a_ref, b_ref, w_ref, da_ref, db_ref, gw_ref, *rest):
        db_segs, gw_s, tile = rest[:n_dil], rest[n_dil], rest[n_dil + 1:]
        i = pl.program_id(0)
        dyb = dy_ref[...].astype(BF16)
        da_ref[...] = _dot_nt(dyb, w_ref[0:A_Q_W, :]).astype(BF16)
        db = _dot_nt(dyb, w_ref[A_Q_W:A_Q_W + B_W, :])
        db_ref[...] = db.astype(BF16)
        _put_tile(tile, db)
        _scatter_to_segments(tile, db_segs)
        ga = _dot_tn(a_ref[...], dyb)
        gb = _dot_tn(b_ref[...], dyb)

        @pl.when(i == 0)
        def _():
            gw_s[0:A_Q_W, :] = ga
            gw_s[A_Q_W:A_Q_W + B_W, :] = gb

        @pl.when(i > 0)
        def _():
            gw_s[0:A_Q_W, :] += ga
            gw_s[A_Q_W:A_Q_W + B_W, :] += gb

        @pl.when(i == n_t - 1)
        def _():
            gw_ref[...] = gw_s[...].astype(BF16)

    tok = lambda w: pl.BlockSpec((tm, w), lambda i: (i, 0))
    whole = pl.BlockSpec(wout.shape, lambda i: (0, 0))
    res = pl.pallas_call(
        body, name=name, grid=(n_t,),
        out_shape=[jax.ShapeDtypeStruct((t, A_Q_W), BF16), jax.ShapeDtypeStruct((t, B_W), BF16),
                   jax.ShapeDtypeStruct(wout.shape, BF16)] + [_seg_shape(t, dil, B_W, BF16) for dil in DILATIONS],
        in_specs=[tok(d), tok(A_Q_W), tok(B_W), whole],
        out_specs=[tok(A_Q_W), tok(B_W), whole] + [_seg_spec(tm, dil, B_W) for dil in DILATIONS],
        scratch_shapes=[pltpu.VMEM(wout.shape, F32)] + _tile_scratch(tm, B_W),
        compiler_params=_cp(("arbitrary",)),
    )(dy, a_out, b_out, wout)
    return res[0], [res[1]] + list(res[3:]), res[2]


SUB_ROWS = 64
WINDOW_ALIGN = 64


def _sub_band(r0, hw, win, seg_len, t, rel, col):
    ks = pl.multiple_of(jnp.clip(r0 - hw, 0, t - win), WINDOW_ALIGN)
    kpos = col + ks
    seg_lo = (r0 // seg_len) * seg_len
    valid = (jnp.abs(rel + (ks - r0)) <= hw) & (kpos >= seg_lo) & (kpos < seg_lo + seg_len)
    return ks, valid


def _split_heads(v, low):
    zero = jnp.zeros_like(v)
    return jnp.concatenate([jnp.where(low, v, zero), jnp.where(low, zero, v)], axis=0)


def _kv_spec(kv, t):
    if kv.ndim == 3:
        return pl.BlockSpec((None, t, LANES), lambda p, i: (p // 2, 0, 0))
    return pl.BlockSpec((t, LANES), lambda p, i: (0, p))


def _attn_fwd(q, k, v, sink, name, hw, seg_len, tq, has_sink):
    t, width = q.shape
    sb = min(SUB_ROWS, tq)
    win = 2 * hw + LANES

    def body(sink_ref, q_ref, k_ref, v_ref, o_ref, lse_ref):
        p, i = pl.program_id(0), pl.program_id(1)
        q0 = i * tq
        low = lax.broadcasted_iota(jnp.int32, (1, LANES), 1) < HEAD_DIM
        rel = lax.broadcasted_iota(jnp.int32, (sb, win), 1) - lax.broadcasted_iota(jnp.int32, (sb, win), 0)
        col = lax.broadcasted_iota(jnp.int32, (1, win), 1)
        subs = []
        for j in range(tq // sb):
            rows = pl.ds(j * sb, sb)
            ks, valid = _sub_band(q0 + j * sb, hw, win, seg_len, t, rel, col)
            subs.append((rows, ks, valid, _dot_nt(_split_heads(q_ref[rows, :], low), k_ref[pl.ds(ks, win), :])))
        for rows, ks, valid, s in subs:
            vw = v_ref[pl.ds(ks, win), :]
            es, inv, lses = [], [], []
            for a in range(2):
                sa = jnp.where(valid, s[a * sb:(a + 1) * sb], NEG)
                m = jnp.max(sa, axis=1, keepdims=True)
                if has_sink:
                    sk = sink_ref[2 * p + a]
                    m = jnp.maximum(m, sk)
                e = jnp.exp(sa - m)
                den = jnp.sum(e, axis=1, keepdims=True)
                if has_sink:
                    den = den + jnp.exp(sk - m)
                es.append(e.astype(BF16))
                inv.append(1.0 / den)
                lses.append(m + jnp.log(den))
            pv = _dot_nn(jnp.concatenate(es, axis=0), vw)
            o_ref[rows, :] = jnp.where(low, pv[0:sb] * inv[0], pv[sb:2 * sb] * inv[1]).astype(BF16)
            lse_ref[rows, :] = jnp.where(low, lses[0], lses[1])

    tile = pl.BlockSpec((tq, LANES), lambda p, i: (i, p))
    return pl.pallas_call(
        body, name=name, grid=(width // LANES, t // tq),
        out_shape=[jax.ShapeDtypeStruct((t, width), BF16), jax.ShapeDtypeStruct((t, width), F32)],
        in_specs=[pl.BlockSpec(memory_space=pltpu.SMEM), tile, _kv_spec(k, t), _kv_spec(v, t)],
        out_specs=[tile, tile],
        compiler_params=_cp(("arbitrary", "arbitrary")),
    )(sink, q, k, v)


def _attn_bwd(q, k, v, o, do, lse, sink, name, hw, seg_len, tq, has_sink, comm=None, comm_at=None):
    t, width = q.shape
    sb = min(SUB_ROWS, tq)
    win = 2 * hw + LANES
    n_q = t // tq
    shared_kv = k.ndim == 3
    n_ci, n_co, c_in_specs, c_out_specs, c_shapes, c_scratch, c_inputs = _comm_parts(comm)

    def body(*refs):
        ((sink_ref, q_ref, k_ref, v_ref, o_ref, do_ref, lse_ref), c_in, (dq_ref, dk_ref, dv_ref, ds_ref), c_out,
         (dk_s, dv_s), c_scr) = _split(refs, 7, n_ci, 4, n_co, 2)
        p, i = pl.program_id(0), pl.program_id(1)
        if comm:
            _run_hosted(comm, comm_at, p * n_q + i, c_in, c_out, c_scr)
        fresh = (i == 0) & (p % 2 == 0) if shared_kv else i == 0
        last = (i == n_q - 1) & (p % 2 == 1) if shared_kv else i == n_q - 1

        @pl.when(fresh)
        def _():
            dk_s[...] = jnp.zeros_like(dk_s)
            dv_s[...] = jnp.zeros_like(dv_s)

        q0 = i * tq
        low = lax.broadcasted_iota(jnp.int32, (1, LANES), 1) < HEAD_DIM
        rel = lax.broadcasted_iota(jnp.int32, (sb, win), 1) - lax.broadcasted_iota(jnp.int32, (sb, win), 0)
        col = lax.broadcasted_iota(jnp.int32, (1, win), 1)
        dsink = [jnp.zeros((1, 1), F32), jnp.zeros((1, 1), F32)]
        subs = []
        for j in range(tq // sb):
            rows = pl.ds(j * sb, sb)
            ks, valid = _sub_band(q0 + j * sb, hw, win, seg_len, t, rel, col)
            kw = k_ref[pl.ds(ks, win), :]
            dov = do_ref[rows, :]
            q2 = _split_heads(q_ref[rows, :], low)
            do2 = _split_heads(dov, low)
            subs.append((rows, ks, valid, kw, dov, q2, do2, _dot_nt(q2, kw), _dot_nt(do2, v_ref[pl.ds(ks, win), :])))
        probs = []
        for rows, ks, valid, kw, dov, q2, do2, s, dpr in subs:
            prod = dov.astype(F32) * o_ref[rows, :].astype(F32)
            lse_t = lse_ref[rows, :]
            prs, dss = [], []
            for a in range(2):
                mine = low if a == 0 else jnp.logical_not(low)
                lse_a = jnp.max(jnp.where(mine, lse_t, -jnp.inf), axis=1, keepdims=True)
                delta = jnp.sum(jnp.where(mine, prod, 0.0), axis=1, keepdims=True)
                pr = jnp.exp(jnp.where(valid, s[a * sb:(a + 1) * sb], NEG) - lse_a)
                prs.append(pr.astype(BF16))
                dss.append((pr * (dpr[a * sb:(a + 1) * sb] - delta)).astype(BF16))
                if has_sink:
                    dsink[a] = dsink[a] - jnp.sum(jnp.exp(sink_ref[2 * p + a] - lse_a) * delta, axis=0, keepdims=True)
            probs.append((jnp.concatenate(prs, axis=0), jnp.concatenate(dss, axis=0)))
        for (rows, ks, valid, kw, dov, q2, do2, s, dpr), (pr2, ds2) in zip(subs, probs):
            dv_s[pl.ds(ks, win), :] += _dot_tn(pr2, do2)
            dk_s[pl.ds(ks, win), :] += _dot_tn(ds2, q2)
            dq2 = _dot_nn(ds2, kw)
            dq_ref[rows, :] = jnp.where(low, dq2[0:sb], dq2[sb:2 * sb]).astype(BF16)
        ds_ref[...] = jnp.broadcast_to(jnp.where(low, dsink[0], dsink[1]), ds_ref.shape)

        @pl.when(last)
        def _():
            dk_ref[...] = dk_s[...].astype(BF16)
            dv_ref[...] = dv_s[...].astype(BF16)

    tile = pl.BlockSpec((tq, LANES), lambda p, i: (i, p))
    kv_shape = jax.ShapeDtypeStruct(k.shape, BF16)
    res = pl.pallas_call(
        body, name=name, grid=(width // LANES, n_q),
        out_shape=[jax.ShapeDtypeStruct((t, width), BF16), kv_shape, kv_shape,
                   jax.ShapeDtypeStruct((width // LANES, n_q, 8, LANES), F32)] + c_shapes,
        in_specs=[pl.BlockSpec(memory_space=pltpu.SMEM), tile, _kv_spec(k, t), _kv_spec(v, t), tile, tile, tile] + c_in_specs,
        out_specs=[tile, _kv_spec(k, t), _kv_spec(v, t), pl.BlockSpec((None, None, 8, LANES), lambda p, i: (p, i, 0, 0))] + c_out_specs,
        scratch_shapes=[pltpu.VMEM((t, LANES), F32)] * 2 + c_scratch,
        compiler_params=_cp(("arbitrary", "arbitrary")),
    )(sink, q, k, v, o, do, lse, *c_inputs)
    return res[:4], res[4:]


def _small_update(land, params, rows, name):
    n_p = len(params)

    def body(*refs):
        (l_ref,), wmv, (red_ref,), outs = _split(refs, 1, 3 * n_p, 1)
        acc = l_ref[0]
        for s in range(1, N_DEV):
            acc = acc + l_ref[s]
        red_ref[...] = acc
        for k in range(n_p):
            (w_ref, m_ref, v_ref), (d_ref, nm_ref, nv_ref) = wmv[3 * k:3 * k + 3], outs[3 * k:3 * k + 3]
            gv = red_ref[rows[k]:rows[k] + 1, 0:w_ref.shape[1]]
            nm = ADAM_B1 * m_ref[...] + (1.0 - ADAM_B1) * gv
            nv = ADAM_B2 * v_ref[...] + (1.0 - ADAM_B2) * (gv * gv)
            m_hat = nm / (1.0 - ADAM_B1 ** ADAM_STEP)
            v_hat = nv / (1.0 - ADAM_B2 ** ADAM_STEP)
            d_ref[...] = -ADAM_LR * (m_hat / (jnp.sqrt(v_hat) + ADAM_EPS) + ADAM_WD * w_ref[...])
            nm_ref[...] = nm
            nv_ref[...] = nv

    flat = [a for p in params for a in p]
    res = pl.pallas_call(
        body, name=name,
        out_shape=[jax.ShapeDtypeStruct(land.shape[1:], F32)] + [jax.ShapeDtypeStruct(p[0].shape, F32) for p in params for _ in range(3)],
        compiler_params=_cp())(land, *flat)
    return res[0], [res[1 + 3 * k:4 + 3 * k] for k in range(n_p)]


def _adamw_rows(land, j, rows, w, m, v, name):
    d = w.shape[1]

    def body(l_ref, w_ref, m_ref, v_ref, g_ref, d_ref, nm_ref, nv_ref):
        gv = l_ref[0].astype(F32)
        for s in range(1, N_DEV):
            gv = gv + l_ref[s].astype(F32)
        g_ref[...] = gv
        nm = ADAM_B1 * m_ref[...] + (1.0 - ADAM_B1) * gv
        nv = ADAM_B2 * v_ref[...] + (1.0 - ADAM_B2) * (gv * gv)
        m_hat = nm / (1.0 - ADAM_B1 ** ADAM_STEP)
        v_hat = nv / (1.0 - ADAM_B2 ** ADAM_STEP)
        d_ref[...] = -ADAM_LR * (m_hat / (jnp.sqrt(v_hat) + ADAM_EPS) + ADAM_WD * w_ref[...])
        nm_ref[...] = nm
        nv_ref[...] = nv

    halves = 2
    half = pl.BlockSpec((rows // halves, d), lambda i: (i, 0))
    shape = jax.ShapeDtypeStruct((rows, d), F32)
    return pl.pallas_call(
        body, name=name, grid=(halves,), out_shape=[shape] * 4,
        in_specs=[pl.BlockSpec((N_DEV, rows // halves, d), lambda i: (0, halves * j + i, 0)), half, half, half],
        out_specs=[half] * 4,
        compiler_params=_cp(("arbitrary",)),
    )(land, w, m, v)


def kernel(x, positions, norm_ffn1, w_gate1, w_up1, w_down1, norm_mix, w_in, a_sink, w_out, norm_ffn2, w_gate2, w_up2, w_down2, norm_final, loss_target, m_norm_ffn1, m_w_gate1, m_w_up1, m_w_down1, m_norm_mix, m_w_in, m_a_sink, m_w_out, m_norm_ffn2, m_w_gate2, m_w_up2, m_w_down2, m_norm_final, v_norm_ffn1, v_w_gate1, v_w_up1, v_w_down1, v_norm_mix, v_w_in, v_a_sink, v_w_out, v_norm_ffn2, v_w_gate2, v_w_up2, v_w_down2, v_norm_final):
    x = x[0]
    target = loss_target[0]
    t, d = x.shape
    tm_mix = min(512, t)
    tq_of = lambda seg_len, most=2048: min(most, t, max(seg_len, 16 * SUB_ROWS))

    tt = min(1024, t)
    f_all = w_gate1.shape[2] * N_DEV
    tfp = f_all // 2
    tm_fwd, tf_fwd = tm_mix, tfp
    n_steps = (t // tm_fwd) * (f_all // tf_fwd)

    def stacked(shards):
        return jnp.concatenate([s.astype(BF16) for s in shards], axis=0), [s.shape[0] for s in shards]

    packed1, rows1 = stacked([w_gate1[0].T, w_up1[0].T, w_down1[0]])
    later = [w_in[0].T, w_out[0], w_gate2[0].T, w_up2[0].T, w_down2[0]]
    rows2 = [s.shape[0] for s in later]
    (wg1, wu1, wd1), packed2, cos, sin_signed = _run_alone_busy(
        _gather_plan(packed1, rows1, stages=len(rows1)), later, positions[0].astype(F32)[:, None], "gather_ffn1_weights")

    sink = a_sink[0]
    no_sink = jnp.zeros_like(sink)

    (x1, h1, gp1, up1, hid1), (win, wout, wg2, wu2, wd2) = _ffn_fwd(
        x, norm_ffn1, wg1, wu1, wd1, "ffn1_fwd", tm_fwd, tf_fwd, _gather_plan(packed2, rows2),
        [0, n_steps // 2, (3 * n_steps) // 4, n_steps - 1])
    (h_mix, aq, akx, avx, bq, bk, bv), seg_qkv = _in_proj_fwd(x1, norm_mix, win, cos, sin_signed, "in_proj_fwd", tm_mix)
    a_out, a_lse = _attn_fwd(aq, akx, avx, sink, "attn_a_fwd", A_HALF_WINDOW, t, tq_of(t), True)
    rows_of = lambda a: a.reshape(t, B_W)
    segments_of = lambda a, dil: a if dil == 1 else a.reshape(dil, t // dil, B_W)
    b_qkv, b_outs, b_lses = [], [], []
    for n, (window, dil) in enumerate(B_PATTERNS):
        qs, ks, vs = (bq, bk, bv) if dil == 1 else (rows_of(seg_qkv[k][n - 1]) for k in range(3))
        o_seg, lse_seg = _attn_fwd(qs, ks, vs, no_sink, f"attn_b{dil}_fwd", window // (2 * dil), t // dil,
                                  tq_of(t // dil, 16 * SUB_ROWS), False)
        b_qkv.append((qs, ks, vs))
        b_outs.append(segments_of(o_seg, dil))
        b_lses.append(segments_of(lse_seg, dil))
    x2, b_out, b_lse = _merge_out_proj_fwd(x1, a_out, b_outs, b_lses, wout, "out_proj_fwd", tm_mix)
    (dx3, dg_final, sq, dout2, h2, gp2, up2, hid2), _ = _ffn_fwd(
        x2, norm_ffn2, wg2, wu2, wd2, "ffn2_fwd_loss", tm_fwd, tf_fwd, loss_head=(norm_final.reshape(1, d), target))


    (dgt2, dut2, gwd2), _ = _ffn_bwd_hidden(dout2, gp2, up2, hid2, wd2, "ffn2_bwd_hidden", tm_mix, tfp)
    (dx2, dg_ffn2), _ = _ffn_bwd_input(dx3, x2, norm_ffn2, dgt2, dut2, wg2, wu2, "ffn2_bwd_input", tm_mix)
    (gwg2, gwu2), _ = _token_products([dgt2, dut2], h2, "ffn2_bwd_gate_up", tt, tfp)

    n_att = lambda width, tq: (width // LANES) * (t // tq)
    n_prod = (f_all // tfp) * (t // tt)
    da, db, gwout = _out_proj_bwd(dx2, a_out, b_out[0], wout, "out_proj_bwd", tm_mix)
    (daq, dakx, davx, dsink_parts), (land_wg2,) = _attn_bwd(
        aq, akx, avx, a_out, da, a_lse, sink, "attn_a_bwd", A_HALF_WINDOW, t, tq_of(t), True,
        _exchange_plan([[gwg2]]), [0, n_att(A_Q_W, tq_of(t)) - 1])
    pattern_grads, pattern_lands = [], []
    for n, ((window, dil), (qs, ks, vs)) in enumerate(zip(B_PATTERNS, b_qkv)):
        carried = [gwu2, gwd2][n:n + 1]
        grads, lands = _attn_bwd(qs, ks, vs, rows_of(b_out[n]), rows_of(db[n]), rows_of(b_lse[n]), no_sink, f"attn_b{dil}_bwd",
                                 window // (2 * dil), t // dil, tq_of(t // dil), False,
                                 _exchange_plan([carried]) if carried else None, [0, n_att(B_W, tq_of(t // dil)) - 1])
        pattern_grads.append(grads[:3])
        pattern_lands.extend(lands)
    land_wu2, land_wd2 = pattern_lands
    dx1, dg_mix, gwin, dout1 = _in_proj_bwd(dx2, x1, norm_mix, win, h_mix, cos, sin_signed, daq, dakx, davx, pattern_grads, "in_proj_bwd", tm_mix)

    (dgt1, dut1, gwd1), (land_in, land_out) = _ffn_bwd_hidden(
        dout1, gp1, up1, hid1, wd1, "ffn1_bwd_hidden", tm_mix, tfp, _exchange_plan([[gwin], [gwout]]),
        [0, (f_all // tfp) * (t // tm_mix) - 1])
    (gwg1, gwu1), (land_wd1,) = _token_products(
        [dgt1, dut1], h1, "ffn1_bwd_gate_up", tt, tfp, _exchange_plan([[gwd1]]), [0, n_prod - 1])
    (grad_x, dg_ffn1), (land_wg1, land_wu1) = _ffn_bwd_input(
        dx1, x, norm_ffn1, dgt1, dut1, wg1, wu1, "ffn1_bwd_input", tm_mix, _exchange_plan([[gwg1], [gwu1]]), [0, t // tm_mix - 1])

    dsink_pairs = jnp.sum(dsink_parts[:, :, 0, :], axis=1)
    dsink = jnp.stack([dsink_pairs[:, 0], dsink_pairs[:, HEAD_DIM]], axis=1).reshape(1, -1)
    small = jnp.concatenate([dg_ffn1, dg_mix, dg_ffn2, dg_final, jnp.pad(dsink, ((0, 0), (0, d - dsink.shape[1]))),
                             sq, jnp.zeros((2, d), F32)], axis=0)
    (land_small,) = _run_alone(_exchange_plan([[jnp.tile(small, (N_DEV, 1))]]), "gather_small_gradients")
    small_rows = {"norm_ffn1": 0, "norm_mix": 1, "norm_ffn2": 2, "norm_final": 3, "a_sink": 4}
    as_row = lambda a: a.reshape(1, -1)
    small_params = {"norm_ffn1": (norm_ffn1, m_norm_ffn1, v_norm_ffn1), "norm_mix": (norm_mix, m_norm_mix, v_norm_mix),
                    "norm_ffn2": (norm_ffn2, m_norm_ffn2, v_norm_ffn2), "norm_final": (norm_final, m_norm_final, v_norm_final),
                    "a_sink": (a_sink, m_a_sink, v_a_sink)}
    red_small, small_updates = _small_update(
        land_small, [tuple(as_row(a) for a in small_params[n]) for n in small_rows], list(small_rows.values()), "update_small")
    small_updates = dict(zip(small_rows, small_updates))
    loss = 0.5 * jnp.sum(red_small[5]) / d

    rf = rows1[0]
    sharded = {"w_gate1": (land_wg1, 0, rf, True), "w_up1": (land_wu1, 0, rf, True), "w_down1": (land_wd1, 0, rf, False),
               "w_in": (land_in, 0, rows2[0], True), "w_out": (land_out, 0, rows2[1], False),
               "w_gate2": (land_wg2, 0, rf, True), "w_up2": (land_wu2, 0, rf, True), "w_down2": (land_wd2, 0, rf, False)}
    n_sink = a_sink.shape[1]
    small_grads = {"norm_ffn1": red_small[0:1], "norm_mix": red_small[1:2], "norm_ffn2": red_small[2:3], "norm_final": red_small[3],
                   "a_sink": red_small[4:5, :n_sink]}
    params = {
        "norm_ffn1": (norm_ffn1, m_norm_ffn1, v_norm_ffn1), "w_gate1": (w_gate1, m_w_gate1, v_w_gate1),
        "w_up1": (w_up1, m_w_up1, v_w_up1), "w_down1": (w_down1, m_w_down1, v_w_down1),
        "norm_mix": (norm_mix, m_norm_mix, v_norm_mix), "w_in": (w_in, m_w_in, v_w_in),
        "a_sink": (a_sink, m_a_sink, v_a_sink), "w_out": (w_out, m_w_out, v_w_out),
        "norm_ffn2": (norm_ffn2, m_norm_ffn2, v_norm_ffn2), "w_gate2": (w_gate2, m_w_gate2, v_w_gate2),
        "w_up2": (w_up2, m_w_up2, v_w_up2), "w_down2": (w_down2, m_w_down2, v_w_down2),
        "norm_final": (norm_final, m_norm_final, v_norm_final),
    }
    grad_list, deltas, new_ms, new_vs = [], [], [], []
    for name, (w, m, v) in params.items():
        if name in sharded:
            land, j, rows, is_transposed = sharded[name]
            view = (lambda a: a[0].T) if is_transposed else (lambda a: a[0])
            back = (lambda a: a.T[None]) if is_transposed else (lambda a: a[None])
            outs = [back(o) for o in _adamw_rows(land, j, rows, view(w), view(m), view(v), f"adamw_{name}")]
        else:
            outs = [small_grads[name]] + [o.reshape(w.shape) for o in small_updates[name]]
        for lst, o in zip((grad_list, deltas, new_ms, new_vs), outs):
            lst.append(o)
    return (loss, grad_x[None], *grad_list, *deltas, *new_ms, *new_vs)
```

```python
import functools
import itertools

import numpy as np
import jax
import jax.numpy as jnp
from jax import lax
from jax.experimental import pallas as pl
from jax.experimental.pallas import tpu as pltpu

F32 = jnp.float32
BF16 = jnp.bfloat16

N_DEV = 8
HEAD_DIM = 64
LANES = 128
A_Q_W, A_KV_W, B_W = 512, 128, 512
A_HALF_WINDOW = 128
B_PATTERNS = ((128, 1), (512, 4), (2048, 16))
ROPE_THETA = 10000.0
NORM_EPS = 1e-6
FFN_RES_WEIGHT = 0.5
QK_SCALE = HEAD_DIM ** -0.5
NEG = -1e30

ADAM_LR = 0.001
ADAM_B1 = 0.9
ADAM_B2 = 0.999
ADAM_EPS = 1e-08
ADAM_WD = 0.01
ADAM_STEP = 10

MESH_T = pl.DeviceIdType.MESH
VMEM_LIMIT = 60 * 1024 * 1024


def _cp(sem=None, vmem=VMEM_LIMIT):
    return pltpu.CompilerParams(dimension_semantics=sem, vmem_limit_bytes=vmem)


def _dot_nn(a, b):
    return jnp.dot(a, b, preferred_element_type=F32)


def _dot_nt(a, b):
    return lax.dot_general(a, b, (((1,), (1,)), ((), ())), preferred_element_type=F32)


def _dot_tn(a, b):
    return lax.dot_general(a, b, (((0,), (0,)), ((), ())), preferred_element_type=F32)


def _rstd(xv):
    return lax.rsqrt(jnp.mean(xv * xv, axis=-1, keepdims=True) + NORM_EPS)


def _norm_bwd(dh, xv, r, gn):
    gy = dh * gn
    c = jnp.sum(gy * xv, axis=-1, keepdims=True) * (1.0 / xv.shape[-1])
    dx = r * gy - xv * (r * r * r * c)
    dgn = jnp.sum(dh * (xv * r), axis=0, keepdims=True)
    return dx, dgn


def _accumulate(ref, val, first):
    @pl.when(first)
    def _():
        ref[...] = val

    @pl.when(jnp.logical_not(first))
    def _():
        ref[...] += val


def _mesh_pos():
    return lax.axis_index("x"), lax.axis_index("y"), lax.axis_index("c")


def _dev_index(d):
    return 4 * d[0] + 2 * d[1] + d[2]


class _Comm:
    def __init__(self, inputs, out_shape, scratch, phases):
        self.inputs, self.out_shape, self.scratch, self.phases = inputs, out_shape, scratch, phases

    def specs(self):
        any_spec = pl.BlockSpec(memory_space=pl.ANY)
        return [any_spec] * len(self.inputs), [any_spec] * len(self.out_shape)


def _run_alone(comm, name):
    n_in, n_out = len(comm.inputs), len(comm.out_shape)

    def body(*refs):
        for phase in comm.phases:
            phase(refs[:n_in], refs[n_in:n_in + n_out], refs[n_in + n_out:])

    in_specs, out_specs = comm.specs()
    return pl.pallas_call(body, name=name, out_shape=comm.out_shape, in_specs=in_specs, out_specs=out_specs,
                          scratch_shapes=comm.scratch)(*comm.inputs)


def _run_alone_busy(comm, shards, positions, name):
    n_in, n_out = len(comm.inputs), len(comm.out_shape)
    offs = [int(o) for o in np.cumsum([0] + [s.shape[0] for s in shards])]
    t = positions.shape[0]
    half = HEAD_DIM // 2
    inv_freq = 1.0 / (ROPE_THETA ** (jnp.arange(0, HEAD_DIM, 2, dtype=F32) / HEAD_DIM))
    lanes = jnp.stack([jnp.tile(inv_freq, LANES // half),
                       jnp.tile(jnp.concatenate([-jnp.ones(half, F32), jnp.ones(half, F32)]), LANES // HEAD_DIM)])

    def body(*refs):
        c_in, s_refs, (pos_ref, lanes_ref), c_out, (stack_ref, cos_ref, sin_ref), scr = _split(refs, n_in, len(shards), 2, n_out, 3)
        comm.phases[0](c_in, c_out, scr)
        for s_ref, lo, hi in zip(s_refs, offs[:-1], offs[1:]):
            stack_ref[lo:hi, :] = s_ref[...].astype(BF16)
        ang = pos_ref[...] * lanes_ref[0:1, :]
        cos_ref[...] = jnp.cos(ang)
        sin_ref[...] = jnp.sin(ang) * lanes_ref[1:2, :]
        for phase in comm.phases[1:]:
            phase(c_in, c_out, scr)

    in_specs, out_specs = comm.specs()
    vmem = pl.BlockSpec(memory_space=pltpu.VMEM)
    table = jax.ShapeDtypeStruct((t, LANES), F32)
    res = pl.pallas_call(
        body, name=name, out_shape=comm.out_shape + [jax.ShapeDtypeStruct((offs[-1], shards[0].shape[1]), BF16), table, table],
        in_specs=in_specs + [vmem] * (len(shards) + 2), out_specs=out_specs + [vmem] * 3, scratch_shapes=comm.scratch,
        compiler_params=_cp())(*comm.inputs, *shards, positions, lanes)
    return res[:n_out], res[n_out], res[n_out + 1], res[n_out + 2]


def _run_hosted(comm, at, step, ins, outs, scr):
    for phase, when in zip(comm.phases, at):
        @pl.when(step == when)
        def _(phase=phase):
            phase(ins, outs, scr)


def _split(refs, *counts):
    parts, o = [], 0
    for n in counts:
        parts.append(refs[o:o + n])
        o += n
    return parts + [refs[o:]]


def _gather_plan(packed, rows_list, stages=1):
    n_w = len(rows_list)
    offs = [int(o) for o in np.cumsum([0] + list(rows_list))]
    d = packed.shape[1]
    per = n_w // stages
    groups = [range(s * per, (s + 1) * per) for s in range(stages)]

    def tools(ins, outs, scr):
        p_ref = ins[0]
        send_sems, recv_sems, local_sem = scr
        x, y, c = _mesh_pos()
        me, sibling = (x, y, c), (x, y, 1 - c)
        chips = [(1 - x, y), (x, 1 - y), (1 - x, 1 - y)]

        def rows(w, dev):
            start = pl.multiple_of(_dev_index(dev) * rows_list[w], 16)
            return outs[w].at[pl.ds(start, rows_list[w]), :]

        def mine(w):
            return p_ref.at[pl.ds(offs[w], rows_list[w]), :]

        def copy(k, s, w, block, to, own):
            return pltpu.make_async_remote_copy(
                src_ref=mine(w) if own else rows(w, block), dst_ref=rows(w, block),
                send_sem=send_sems.at[k, s], recv_sem=recv_sems.at[k, s], device_id=to, device_id_type=MESH_T)

        def whole_group(k, s):
            span = p_ref.at[pl.ds(offs[groups[s][0]], offs[groups[s][-1] + 1] - offs[groups[s][0]]), :]
            return pltpu.make_async_remote_copy(
                src_ref=span, dst_ref=span, send_sem=send_sems.at[k, s], recv_sem=recv_sems.at[k, s],
                device_id=me, device_id_type=MESH_T)

        return p_ref, local_sem, me, sibling, chips, c, rows, mine, copy, whole_group

    def start(ins, outs, scr):
        _, local_sem, me, sibling, chips, c, rows, mine, copy, _ = tools(ins, outs, scr)
        for w in range(n_w):
            pltpu.make_async_copy(mine(w), rows(w, me), local_sem).start()
        for s, group in enumerate(groups):
            for w in group:
                copy(0, s, w, me, sibling, True).start()
            for j, chip in enumerate(chips[:2]):
                for w in group:
                    copy(1 + j, s, w, me, (*chip, c), True).start()

    def forward(ins, outs, scr):
        _, _, (x, y, _), sibling, chips, c, _, _, copy, whole_group = tools(ins, outs, scr)
        south = c == 0
        source = (jnp.where(south, x, 1 - x), jnp.where(south, 1 - y, y), c)
        target = (jnp.where(south, 1 - x, x), jnp.where(south, y, 1 - y), c)
        for s, group in enumerate(groups):
            for j, chip in enumerate(chips[:2]):
                whole_group(1 + j, s).wait_recv()
            for w in group:
                copy(3, s, w, source, target, False).start()
            for j, chip in enumerate(chips[:2]):
                for w in group:
                    copy(4 + j, s, w, (*chip, c), sibling, False).start()

    def relay(ins, outs, scr):
        _, _, _, sibling, chips, c, _, _, copy, whole_group = tools(ins, outs, scr)
        for s, group in enumerate(groups):
            whole_group(3, s).wait_recv()
            for w in group:
                copy(6, s, w, (*chips[2], c), sibling, False).start()

    def finish(ins, outs, scr):
        p_ref, local_sem, _, _, _, _, _, _, _, whole_group = tools(ins, outs, scr)
        for s in range(stages):
            whole_group(0, s).wait_recv()
            for j in range(3):
                whole_group(4 + j, s).wait_recv()
            for k in range(7):
                whole_group(k, s).wait_send()
        pltpu.make_async_copy(p_ref, p_ref, local_sem).wait()

    return _Comm(
        [packed], [jax.ShapeDtypeStruct((N_DEV * r, d), packed.dtype) for r in rows_list],
        [pltpu.SemaphoreType.DMA((7, stages)), pltpu.SemaphoreType.DMA((7, stages)), pltpu.SemaphoreType.DMA],
        [start, forward, relay, finish])


def _exchange_plan(groups):
    flat = [a for g in groups for a in g]
    n_g = len(groups)
    sizes = [len(g) for g in groups]
    rows = [g[0].shape[0] // N_DEV for g in groups]
    first = [int(o) for o in np.cumsum([0] + sizes[:-1])]

    def start(srcs, lands, scr):
        send_sems, recv_sems, local_sems = scr
        x, y, c = _mesh_pos()
        me = (x, y, c)
        me_idx = _dev_index(me)

        def block(g, i, dev):
            start_row = pl.multiple_of(_dev_index(dev) * rows[g], 8)
            return srcs[first[g] + i].at[pl.ds(start_row, rows[g]), :]

        def slot(g, i):
            return lands[g].at[me_idx, pl.ds(i * rows[g], rows[g]), :]

        for g in range(n_g):
            for i in range(sizes[g]):
                pltpu.make_async_copy(block(g, i, me), slot(g, i), local_sems.at[g]).start()
        flips = [f for f in itertools.product((0, 1), repeat=3) if any(f)]
        for k, (fx, fy, fc) in enumerate(flips):
            peer = (1 - x if fx else x, 1 - y if fy else y, 1 - c if fc else c)
            for g in range(n_g):
                for i in range(sizes[g]):
                    pltpu.make_async_remote_copy(
                        src_ref=block(g, i, peer), dst_ref=slot(g, i), send_sem=send_sems.at[g, k],
                        recv_sem=recv_sems.at[g, k], device_id=peer, device_id_type=MESH_T).start()

    def finish(srcs, lands, scr):
        send_sems, recv_sems, local_sems = scr
        me = _mesh_pos()
        for k in range(7):
            for g in range(n_g):
                pltpu.make_async_remote_copy(
                    src_ref=lands[g].at[0], dst_ref=lands[g].at[0], send_sem=send_sems.at[g, k],
                    recv_sem=recv_sems.at[g, k], device_id=me, device_id_type=MESH_T).wait()
        for g in range(n_g):
            pltpu.make_async_copy(lands[g].at[0], lands[g].at[0], local_sems.at[g]).wait()

    return _Comm(
        flat, [jax.ShapeDtypeStruct((N_DEV, sizes[g] * rows[g], groups[g][0].shape[1]), groups[g][0].dtype) for g in range(n_g)],
        [pltpu.SemaphoreType.DMA((n_g, 7)), pltpu.SemaphoreType.DMA((n_g, 7)), pltpu.SemaphoreType.DMA((n_g,))],
        [start, finish])


def _comm_parts(comm):
    if comm is None:
        return 0, 0, [], [], [], [], []
    in_specs, out_specs = comm.specs()
    return len(comm.inputs), len(comm.out_shape), in_specs, out_specs, comm.out_shape, comm.scratch, comm.inputs


def _ffn_fwd(x, gn, wg_t, wu_t, wd, name, tm, tf, comm=None, comm_at=None, loss_head=None):
    t, d = x.shape
    f_all = wg_t.shape[0]
    n_f = f_all // tf
    n_ci, n_co, c_in_specs, c_out_specs, c_shapes, c_scratch, c_inputs = _comm_parts(comm)
    n_head, n_y = (2, 4) if loss_head else (0, 1)

    def body(*refs):
        ((x_ref, gn_ref, wg_ref, wu_ref, wd_ref), head_in, c_in, y_outs, (h_ref, gp_ref, up_ref, hid_ref), c_out,
         (h_s, hid_s), c_scr) = _split(refs, 5, n_head, n_ci, n_y, 4, n_co, 2)
        f = pl.program_id(1)
        if comm:
            _run_hosted(comm, comm_at, pl.program_id(0) * n_f + f, c_in, c_out, c_scr)

        @pl.when(f == 0)
        def _():
            xv = x_ref[...]
            h = ((xv * _rstd(xv)) * gn_ref[...]).astype(BF16)
            h_s[...] = h
            h_ref[...] = h

        h = h_s[...]
        cols = pl.ds(pl.multiple_of(f * tf, tf), tf)
        g = _dot_nt(h, wg_ref[cols, :])
        u = _dot_nt(h, wu_ref[cols, :])
        sg = jax.nn.sigmoid(g)
        silu = g * sg
        gp_ref[...] = (u * (sg * (1.0 + g * (1.0 - sg)))).astype(BF16)
        up_ref[...] = silu.astype(BF16)
        hid = (silu * u).astype(BF16)
        hid_ref[...] = hid
        for f0 in range(n_f):
            @pl.when(f == f0)
            def _(f0=f0):
                hid_s[:, f0 * tf:(f0 + 1) * tf] = hid

        @pl.when(f == n_f - 1)
        def _():
            y = x_ref[...] + FFN_RES_WEIGHT * _dot_nn(hid_s[...], wd_ref[...])
            if not loss_head:
                y_outs[0][...] = y
            else:
                (gf_ref, tg_ref), (dy_ref, dgf_ref, sq_ref, half_ref) = head_in, y_outs
                gfv, r, first = gf_ref[...], _rstd(y), pl.program_id(0) == 0
                diff = (y * r) * gfv - tg_ref[...]
                dy, dgf = _norm_bwd(diff * (1.0 / d), y, r, gfv)
                dy_ref[...] = dy
                half_ref[...] = (FFN_RES_WEIGHT * dy).astype(BF16)
                _accumulate(dgf_ref, dgf, first)
                _accumulate(sq_ref, jnp.sum(diff * diff, axis=0, keepdims=True), first)

    tok = pl.BlockSpec((tm, d), lambda i, f: (i, 0))
    row = pl.BlockSpec((1, d), lambda i, f: (0, 0))
    whole = pl.BlockSpec((f_all, d), lambda i, f: (0, 0), pipeline_mode=pl.Buffered(1))
    act = pl.BlockSpec((tm, tf), lambda i, f: (i, f))
    act_shape = jax.ShapeDtypeStruct((t, f_all), BF16)
    tok_f32, tok_bf16, row_f32 = (jax.ShapeDtypeStruct((t, d), F32), jax.ShapeDtypeStruct((t, d), BF16),
                                  jax.ShapeDtypeStruct((1, d), F32))
    y_shapes, y_specs = ([tok_f32, row_f32, row_f32, tok_bf16], [tok, row, row, tok]) if loss_head else ([tok_f32], [tok])
    res = pl.pallas_call(
        body, name=name, grid=(t // tm, n_f),
        out_shape=y_shapes + [tok_bf16, act_shape, act_shape, act_shape] + c_shapes,
        in_specs=[tok, row, whole, whole, whole] + ([row, tok] if loss_head else []) + c_in_specs,
        out_specs=y_specs + [tok, act, act, act] + c_out_specs,
        scratch_shapes=[pltpu.VMEM((tm, d), BF16), pltpu.VMEM((tm, f_all), BF16)] + c_scratch,
        compiler_params=_cp(("arbitrary", "arbitrary")),
    )(x, gn, wg_t, wu_t, wd, *(loss_head or ()), *c_inputs)
    return res[:n_y + 4], res[n_y + 4:]


def _ffn_bwd_hidden(dout, gp, up, hid, wd, name, tt, tf, comm=None, comm_at=None):
    t, d = dout.shape
    f_all = wd.shape[0]
    n_t = t // tt
    n_ci, n_co, c_in_specs, c_out_specs, c_shapes, c_scratch, c_inputs = _comm_parts(comm)

    def body(*refs):
        (dout_ref, gp_ref, up_ref, hid_ref, wd_ref), c_in, (dg_ref, du_ref, gwd_ref), c_out, (acc,), c_scr = _split(
            refs, 5, n_ci, 3, n_co, 1)
        s = pl.program_id(1)
        if comm:
            _run_hosted(comm, comm_at, pl.program_id(0) * n_t + s, c_in, c_out, c_scr)
        doutv = dout_ref[...]
        dhid = _dot_nt(doutv, wd_ref[...])
        dg_ref[...] = (dhid * gp_ref[...].astype(F32)).astype(BF16)
        du_ref[...] = (dhid * up_ref[...].astype(F32)).astype(BF16)
        _accumulate(acc, _dot_tn(hid_ref[...], doutv), s == 0)

        @pl.when(s == n_t - 1)
        def _():
            gwd_ref[...] = acc[...].astype(BF16)

    tok = pl.BlockSpec((tt, d), lambda f, s: (s, 0))
    wblk = pl.BlockSpec((tf, d), lambda f, s: (f, 0))
    act = pl.BlockSpec((tt, tf), lambda f, s: (s, f))
    act_shape = jax.ShapeDtypeStruct((t, f_all), BF16)
    res = pl.pallas_call(
        body, name=name, grid=(f_all // tf, n_t),
        out_shape=[act_shape, act_shape, jax.ShapeDtypeStruct((f_all, d), BF16)] + c_shapes,
        in_specs=[tok, act, act, act, wblk] + c_in_specs, out_specs=[act, act, wblk] + c_out_specs,
        scratch_shapes=[pltpu.VMEM((tf, d), F32)] + c_scratch,
        compiler_params=_cp(("arbitrary", "arbitrary")),
    )(dout, gp, up, hid, wd, *c_inputs)
    return res[:3], res[3:]


def _ffn_bwd_input(dy, x, gn, dg, du, wg_t, wu_t, name, tm, comm=None, comm_at=None):
    t, d = x.shape
    f_all = wg_t.shape[0]
    n_ci, n_co, c_in_specs, c_out_specs, c_shapes, c_scratch, c_inputs = _comm_parts(comm)

    def body(*refs):
        (dy_ref, x_ref, gn_ref, dg_ref, du_ref, wg_ref, wu_ref), c_in, (dx_ref, dgn_ref), c_out, c_scr = _split(
            refs, 7, n_ci, 2, n_co)
        i = pl.program_id(0)
        if comm:
            _run_hosted(comm, comm_at, i, c_in, c_out, c_scr)
        dh = _dot_nn(dg_ref[...], wg_ref[...]) + _dot_nn(du_ref[...], wu_ref[...])
        xv = x_ref[...]
        dxn, dgn = _norm_bwd(dh, xv, _rstd(xv), gn_ref[...])
        dx_ref[...] = dy_ref[...] + dxn
        _accumulate(dgn_ref, dgn, i == 0)

    tok = pl.BlockSpec((tm, d), lambda i: (i, 0))
    row = pl.BlockSpec((1, d), lambda i: (0, 0))
    act = pl.BlockSpec((tm, f_all), lambda i: (i, 0))
    whole = pl.BlockSpec((f_all, d), lambda i: (0, 0))
    res = pl.pallas_call(
        body, name=name, grid=(t // tm,),
        out_shape=[jax.ShapeDtypeStruct((t, d), F32), jax.ShapeDtypeStruct((1, d), F32)] + c_shapes,
        in_specs=[tok, tok, row, act, act, whole, whole] + c_in_specs,
        out_specs=[tok, row] + c_out_specs,
        scratch_shapes=c_scratch,
        compiler_params=_cp(("arbitrary",)),
    )(dy, x, gn, dg, du, wg_t, wu_t, *c_inputs)
    return res[:2], res[2:]


def _token_products(lhs_list, rhs, name, tt, tf, comm=None, comm_at=None):
    n_l = len(lhs_list)
    t, f_all = lhs_list[0].shape
    d = rhs.shape[1]
    n_t = t // tt
    n_ci, n_co, c_in_specs, c_out_specs, c_shapes, c_scratch, c_inputs = _comm_parts(comm)

    def body(*refs):
        lhs_refs, (rhs_ref,), c_in, out_refs, c_out, accs, c_scr = _split(refs, n_l, 1, n_ci, n_l, n_co, n_l)
        s = pl.program_id(1)
        if comm:
            _run_hosted(comm, comm_at, pl.program_id(0) * n_t + s, c_in, c_out, c_scr)
        rv = rhs_ref[...]
        for l_ref, acc in zip(lhs_refs, accs):
            _accumulate(acc, _dot_tn(l_ref[...], rv), s == 0)

        @pl.when(s == n_t - 1)
        def _():
            for o_ref, acc in zip(out_refs, accs):
                o_ref[...] = acc[...].astype(BF16)

    act = pl.BlockSpec((tt, tf), lambda f, s: (s, f))
    tok = pl.BlockSpec((tt, d), lambda f, s: (s, 0))
    wblk = pl.BlockSpec((tf, d), lambda f, s: (f, 0))
    res = pl.pallas_call(
        body, name=name, grid=(f_all // tf, n_t),
        out_shape=[jax.ShapeDtypeStruct((f_all, d), BF16)] * n_l + c_shapes,
        in_specs=[act] * n_l + [tok] + c_in_specs, out_specs=[wblk] * n_l + c_out_specs,
        scratch_shapes=[pltpu.VMEM((tf, d), F32)] * n_l + c_scratch,
        compiler_params=_cp(("arbitrary", "arbitrary")),
    )(*lhs_list, rhs, *c_inputs)
    return res[:n_l], res[n_l:]


def _swap_halves(t):
    w = t.shape[-1]
    lane = lax.broadcasted_iota(jnp.int32, (1, w), 1)
    return jnp.where((lane % HEAD_DIM) < HEAD_DIM // 2, pltpu.roll(t, w - HEAD_DIM // 2, 1), pltpu.roll(t, HEAD_DIM // 2, 1))


def _rope(t, cos, sin_signed):
    reps = t.shape[-1] // LANES
    return t * jnp.tile(cos, (1, reps)) + _swap_halves(t) * jnp.tile(sin_signed, (1, reps))


def _rope_bwd(dt, cos, sin_signed):
    reps = dt.shape[-1] // LANES
    return dt * jnp.tile(cos, (1, reps)) + _swap_halves(dt * jnp.tile(sin_signed, (1, reps)))


DILATIONS = tuple(dil for _, dil in B_PATTERNS if dil > 1)


def _seg_shape(t, dil, w, dtype):
    return jax.ShapeDtypeStruct((dil, t // dil, w), dtype)


def _seg_spec(tm, dil, w):
    return pl.BlockSpec((dil, tm // dil, w), lambda i: (0, i, 0))


def _tile_scratch(tm, w):
    return [pltpu.VMEM((tm, LANES), F32)] * (w // LANES)


def _put_tile(tile, val):
    for c, ref in enumerate(tile):
        ref[...] = val[:, c * LANES:(c + 1) * LANES]


def _get_tile(tile):
    return jnp.concatenate([ref[...] for ref in tile], axis=1)


def _scatter_to_segments(tile, seg_refs):
    for seg_ref, dil in zip(seg_refs, DILATIONS):
        rows = tile[0].shape[0] // dil
        for r in range(dil):
            for c, ref in enumerate(tile):
                seg_ref[r, :, c * LANES:(c + 1) * LANES] = ref[pl.ds(r, rows, stride=dil), :].astype(seg_ref.dtype)


def _gather_from_segments(seg_ref, dil, tile, add=False):
    rows = tile[0].shape[0] // dil
    for r in range(dil):
        for c, ref in enumerate(tile):
            idx = (pl.ds(r, rows, stride=dil), slice(None))
            v = seg_ref[r, :, c * LANES:(c + 1) * LANES].astype(F32)
            ref[idx] = ref[idx] + v if add else v


def _in_proj_fwd(x, gn, win_t, cos, sin_signed, name, tm):
    t, d = x.shape
    in_w = win_t.shape[0]
    n_dil = len(DILATIONS)

    def body(x_ref, gn_ref, w_ref, cos_ref, sin_ref, h_ref, aq_ref, akx_ref, avx_ref, bq_ref, bk_ref, bv_ref, *rest):
        seg_refs, tile = rest[:3 * n_dil], rest[3 * n_dil:]
        xv = x_ref[...]
        h = ((xv * _rstd(xv)) * gn_ref[...]).astype(BF16)
        h_ref[...] = h
        p = _dot_nt(h, w_ref[...])
        cs, sn = cos_ref[...], sin_ref[...]
        o = 0
        aq_ref[...] = (_rope(p[:, o:o + A_Q_W], cs, sn) * QK_SCALE).astype(BF16)
        o += A_Q_W
        ak = _rope(p[:, o:o + A_KV_W], cs, sn)
        o += A_KV_W
        av = p[:, o:o + A_KV_W]
        o += A_KV_W
        low = lax.broadcasted_iota(jnp.int32, (1, LANES), 1) < HEAD_DIM
        for src, dst in ((ak, akx_ref), (av, avx_ref)):
            other = pltpu.roll(src, HEAD_DIM, 1)
            dst[0] = jnp.where(low, src, other).astype(BF16)
            dst[1] = jnp.where(low, other, src).astype(BF16)
        for k, nat_ref in enumerate((bq_ref, bk_ref, bv_ref)):
            val = p[:, o:o + B_W]
            o += B_W
            if k < 2:
                val = _rope(val, cs, sn)
            if k == 0:
                val = val * QK_SCALE
            nat_ref[...] = val.astype(BF16)
            _put_tile(tile, val)
            _scatter_to_segments(tile, seg_refs[k * n_dil:(k + 1) * n_dil])

    tok = lambda w: pl.BlockSpec((tm, w), lambda i: (i, 0))
    kvx = pl.BlockSpec((2, tm, LANES), lambda i: (0, i, 0))
    sd = lambda *s: jax.ShapeDtypeStruct(s, BF16)
    res = pl.pallas_call(
        body, name=name, grid=(t // tm,),
        out_shape=[sd(t, d), sd(t, A_Q_W), sd(2, t, LANES), sd(2, t, LANES), sd(t, B_W), sd(t, B_W), sd(t, B_W)]
        + [_seg_shape(t, dil, B_W, BF16) for _ in range(3) for dil in DILATIONS],
        in_specs=[tok(d), pl.BlockSpec((1, d), lambda i: (0, 0)), pl.BlockSpec((in_w, d), lambda i: (0, 0)),
                  tok(LANES), tok(LANES)],
        out_specs=[tok(d), tok(A_Q_W), kvx, kvx, tok(B_W), tok(B_W), tok(B_W)]
        + [_seg_spec(tm, dil, B_W) for _ in range(3) for dil in DILATIONS],
        scratch_shapes=_tile_scratch(tm, B_W),
        compiler_params=_cp(("arbitrary",)),
    )(x, gn, win_t, cos, sin_signed)
    return res[:7], [res[7 + k * n_dil:7 + (k + 1) * n_dil] for k in range(3)]


def _in_proj_bwd(dres, x, gn, win_t, h, cos, sin_signed, daq, dakx, davx, per_pattern, name, tm):
    t, d = x.shape
    in_w = win_t.shape[0]
    n_t = t // tm
    chunk = in_w // 3
    n_pat = 3 * (1 + len(DILATIONS))

    def body(*refs):
        ((dres_ref, x_ref, gn_ref, w_ref, h_ref, cos_ref, sin_ref, daq_ref, dakx_ref, davx_ref), pat,
         (dx_ref, dgn_ref, gw_ref, half_ref), (dp_s, gw_s), tile) = _split(refs, 10, n_pat, 4, 2)
        i = pl.program_id(0)
        cs, sn = cos_ref[...], sin_ref[...]
        low = lax.broadcasted_iota(jnp.int32, (1, LANES), 1) < HEAD_DIM

        def fold(ref):
            a, b = ref[0].astype(F32), ref[1].astype(F32)
            return jnp.where(low, a + pltpu.roll(a, HEAD_DIM, 1), b + pltpu.roll(b, HEAD_DIM, 1))

        def summed(j):
            _put_tile(tile, pat[j][...].astype(F32))
            for k, dil in enumerate(DILATIONS):
                _gather_from_segments(pat[3 * (1 + k) + j], dil, tile, add=True)
            return _get_tile(tile)

        o = 0
        dp_s[:, o:o + A_Q_W] = _rope_bwd(daq_ref[...].astype(F32) * QK_SCALE, cs, sn).astype(BF16)
        o += A_Q_W
        dp_s[:, o:o + A_KV_W] = _rope_bwd(fold(dakx_ref), cs, sn).astype(BF16)
        o += A_KV_W
        dp_s[:, o:o + A_KV_W] = fold(davx_ref).astype(BF16)
        o += A_KV_W
        dp_s[:, o:o + B_W] = _rope_bwd(summed(0) * QK_SCALE, cs, sn).astype(BF16)
        o += B_W
        dp_s[:, o:o + B_W] = _rope_bwd(summed(1), cs, sn).astype(BF16)
        o += B_W
        dp_s[:, o:o + B_W] = summed(2).astype(BF16)
        dh = _dot_nn(dp_s[...], w_ref[...])
        hv = h_ref[...]
        for c0 in range(0, in_w, chunk):
            _accumulate(gw_s.at[pl.ds(c0, chunk), :], _dot_tn(dp_s[:, c0:c0 + chunk], hv), i == 0)
        xv = x_ref[...]
        dxn, dgn = _norm_bwd(dh, xv, _rstd(xv), gn_ref[...])
        dx = dres_ref[...] + dxn
        dx_ref[...] = dx
        half_ref[...] = (FFN_RES_WEIGHT * dx).astype(BF16)
        _accumulate(dgn_ref, dgn, i == 0)

        @pl.when(i == n_t - 1)
        def _():
            gw_ref[...] = gw_s[...].astype(BF16)

    tok = lambda w: pl.BlockSpec((tm, w), lambda i: (i, 0))
    row = pl.BlockSpec((1, d), lambda i: (0, 0))
    whole = pl.BlockSpec((in_w, d), lambda i: (0, 0))
    weight = pl.BlockSpec((in_w, d), lambda i: (0, 0), pipeline_mode=pl.Buffered(1))
    kvx = pl.BlockSpec((2, tm, LANES), lambda i: (0, i, 0))
    flat = [a if k == 0 else a.reshape(DILATIONS[k - 1], -1, B_W) for k, grads in enumerate(per_pattern) for a in grads]
    return pl.pallas_call(
        body, name=name, grid=(n_t,),
        out_shape=[jax.ShapeDtypeStruct((t, d), F32), jax.ShapeDtypeStruct((1, d), F32),
                   jax.ShapeDtypeStruct((in_w, d), BF16), jax.ShapeDtypeStruct((t, d), BF16)],
        in_specs=[tok(d), tok(d), row, weight, tok(d), tok(LANES), tok(LANES), tok(A_Q_W), kvx, kvx]
        + [tok(B_W)] * 3 + [_seg_spec(tm, dil, B_W) for dil in DILATIONS for _ in range(3)],
        out_specs=[tok(d), row, whole, tok(d)],
        scratch_shapes=[pltpu.VMEM((tm, in_w), BF16), pltpu.VMEM((in_w, d), F32)] + _tile_scratch(tm, B_W),
        compiler_params=_cp(("arbitrary",)),
    )(dres, x, gn, win_t, h, cos, sin_signed, daq, dakx, davx, *flat)


def _merge_out_proj_fwd(x, a_out, outs, lses, wout, name, tm):
    t, d = x.shape
    n_dil = len(DILATIONS)

    def body(x_ref, a_ref, *rest):
        o_refs, l_refs, (w_ref, y_ref, b_ref, lt_ref), b_segs, lt_segs, scratch = _split(rest, 1 + n_dil, 1 + n_dil, 4, n_dil, n_dil)
        n_c = B_W // LANES
        tiles = [scratch[j * n_c:(j + 1) * n_c] for j in range(2 * n_dil)]
        os_, ls = [o_refs[0][...].astype(F32)], [l_refs[0][...]]
        for k, dil in enumerate(DILATIONS):
            _gather_from_segments(o_refs[1 + k], dil, tiles[2 * k])
            _gather_from_segments(l_refs[1 + k], dil, tiles[2 * k + 1])
            os_.append(_get_tile(tiles[2 * k]))
            ls.append(_get_tile(tiles[2 * k + 1]))
        mx = functools.reduce(jnp.maximum, ls)
        es = [jnp.exp(l - mx) for l in ls]
        den = functools.reduce(jnp.add, es)
        b = functools.reduce(jnp.add, [e * o for e, o in zip(es, os_)]) / den
        lt = mx + jnp.log(den)
        bb = b.astype(BF16)
        b_ref[...] = bb
        lt_ref[...] = lt
        y_ref[...] = x_ref[...] + _dot_nn(a_ref[...], w_ref[0:A_Q_W, :]) + _dot_nn(bb, w_ref[A_Q_W:A_Q_W + B_W, :])
        _put_tile(tiles[0], b)
        _scatter_to_segments(tiles[0], b_segs)
        _put_tile(tiles[1], lt)
        _scatter_to_segments(tiles[1], lt_segs)

    tok = lambda w: pl.BlockSpec((tm, w), lambda i: (i, 0))
    segs = [_seg_spec(tm, dil, B_W) for dil in DILATIONS]
    res = pl.pallas_call(
        body, name=name, grid=(t // tm,),
        out_shape=[jax.ShapeDtypeStruct((t, d), F32), jax.ShapeDtypeStruct((t, B_W), BF16), jax.ShapeDtypeStruct((t, B_W), F32)]
        + [_seg_shape(t, dil, B_W, BF16) for dil in DILATIONS] + [_seg_shape(t, dil, B_W, F32) for dil in DILATIONS],
        in_specs=[tok(d), tok(A_Q_W)] + ([tok(B_W)] + segs) * 2 + [pl.BlockSpec(wout.shape, lambda i: (0, 0))],
        out_specs=[tok(d), tok(B_W), tok(B_W)] + segs * 2,
        scratch_shapes=_tile_scratch(tm, B_W) * (2 * n_dil),
        compiler_params=_cp(("arbitrary",)),
    )(x, a_out, *outs, *lses, wout)
    return res[0], [res[1]] + list(res[3:3 + n_dil]), [res[2]] + list(res[3 + n_dil:])


def _out_proj_bwd(dy, a_out, b_out, wout, name, tm):
    t, d = dy.shape
    n_t = t // tm
    n_dil = len(DILATIONS)

    def body(dy_ref, a_ref, b_ref, w_ref, da_ref, db_ref, gw_ref, *rest):
        db_segs, gw_s, tile = rest[:n_dil], rest[n_dil], rest[n_dil + 1:]
        i = pl.program_id(0)
        dyb = dy_ref[...].astype(BF16)
        da_ref[...] = _dot_nt(dyb, w_ref[0:A_Q_W, :]).astype(BF16)
        db = _dot_nt(dyb, w_ref[A_Q_W:A_Q_W + B_W, :])
        db_ref[...] = db.astype(BF16)
        _put_tile(tile, db)
        _scatter_to_segments(tile, db_segs)
        ga = _dot_tn(a_ref[...], dyb)
        gb = _dot_tn(b_ref[...], dyb)

        @pl.when(i == 0)
        def _():
            gw_s[0:A_Q_W, :] = ga
            gw_s[A_Q_W:A_Q_W + B_W, :] = gb

        @pl.when(i > 0)
        def _():
            gw_s[0:A_Q_W, :] += ga
            gw_s[A_Q_W:A_Q_W + B_W, :] += gb

        @pl.when(i == n_t - 1)
        def _():
            gw_ref[...] = gw_s[...].astype(BF16)

    tok = lambda w: pl.BlockSpec((tm, w), lambda i: (i, 0))
    whole = pl.BlockSpec(wout.shape, lambda i: (0, 0))
    res = pl.pallas_call(
        body, name=name, grid=(n_t,),
        out_shape=[jax.ShapeDtypeStruct((t, A_Q_W), BF16), jax.ShapeDtypeStruct((t, B_W), BF16),
                   jax.ShapeDtypeStruct(wout.shape, BF16)] + [_seg_shape(t, dil, B_W, BF16) for dil in DILATIONS],
        in_specs=[tok(d), tok(A_Q_W), tok(B_W), whole],
        out_specs=[tok(A_Q_W), tok(B_W), whole] + [_seg_spec(tm, dil, B_W) for dil in DILATIONS],
        scratch_shapes=[pltpu.VMEM(wout.shape, F32)] + _tile_scratch(tm, B_W),
        compiler_params=_cp(("arbitrary",)),
    )(dy, a_out, b_out, wout)
    return res[0], [res[1]] + list(res[3:]), res[2]


SUB_ROWS = 64
WINDOW_ALIGN = 64


def _sub_band(r0, hw, win, seg_len, t, rel, col):
    ks = pl.multiple_of(jnp.clip(r0 - hw, 0, t - win), WINDOW_ALIGN)
    kpos = col + ks
    seg_lo = (r0 // seg_len) * seg_len
    valid = (jnp.abs(rel + (ks - r0)) <= hw) & (kpos >= seg_lo) & (kpos < seg_lo + seg_len)
    return ks, valid


def _split_heads(v, low):
    zero = jnp.zeros_like(v)
    return jnp.concatenate([jnp.where(low, v, zero), jnp.where(low, zero, v)], axis=0)


def _kv_spec(kv, t):
    if kv.ndim == 3:
        return pl.BlockSpec((None, t, LANES), lambda p, i: (p // 2, 0, 0))
    return pl.BlockSpec((t, LANES), lambda p, i: (0, p))


def _attn_fwd(q, k, v, sink, name, hw, seg_len, tq, has_sink):
    t, width = q.shape
    sb = min(SUB_ROWS, tq)
    win = 2 * hw + LANES

    def body(sink_ref, q_ref, k_ref, v_ref, o_ref, lse_ref):
        p, i = pl.program_id(0), pl.program_id(1)
        q0 = i * tq
        low = lax.broadcasted_iota(jnp.int32, (1, LANES), 1) < HEAD_DIM
        rel = lax.broadcasted_iota(jnp.int32, (sb, win), 1) - lax.broadcasted_iota(jnp.int32, (sb, win), 0)
        col = lax.broadcasted_iota(jnp.int32, (1, win), 1)
        subs = []
        for j in range(tq // sb):
            rows = pl.ds(j * sb, sb)
            ks, valid = _sub_band(q0 + j * sb, hw, win, seg_len, t, rel, col)
            subs.append((rows, ks, valid, _dot_nt(_split_heads(q_ref[rows, :], low), k_ref[pl.ds(ks, win), :])))
        for rows, ks, valid, s in subs:
            vw = v_ref[pl.ds(ks, win), :]
            es, inv, lses = [], [], []
            for a in range(2):
                sa = jnp.where(valid, s[a * sb:(a + 1) * sb], NEG)
                m = jnp.max(sa, axis=1, keepdims=True)
                if has_sink:
                    sk = sink_ref[2 * p + a]
                    m = jnp.maximum(m, sk)
                e = jnp.exp(sa - m)
                den = jnp.sum(e, axis=1, keepdims=True)
                if has_sink:
                    den = den + jnp.exp(sk - m)
                es.append(e.astype(BF16))
                inv.append(1.0 / den)
                lses.append(m + jnp.log(den))
            pv = _dot_nn(jnp.concatenate(es, axis=0), vw)
            o_ref[rows, :] = jnp.where(low, pv[0:sb] * inv[0], pv[sb:2 * sb] * inv[1]).astype(BF16)
            lse_ref[rows, :] = jnp.where(low, lses[0], lses[1])

    tile = pl.BlockSpec((tq, LANES), lambda p, i: (i, p))
    return pl.pallas_call(
        body, name=name, grid=(width // LANES, t // tq),
        out_shape=[jax.ShapeDtypeStruct((t, width), BF16), jax.ShapeDtypeStruct((t, width), F32)],
        in_specs=[pl.BlockSpec(memory_space=pltpu.SMEM), tile, _kv_spec(k, t), _kv_spec(v, t)],
        out_specs=[tile, tile],
        compiler_params=_cp(("arbitrary", "arbitrary")),
    )(sink, q, k, v)


def _attn_bwd(q, k, v, o, do, lse, sink, name, hw, seg_len, tq, has_sink, comm=None, comm_at=None):
    t, width = q.shape
    sb = min(SUB_ROWS, tq)
    win = 2 * hw + LANES
    n_q = t // tq
    shared_kv = k.ndim == 3
    n_ci, n_co, c_in_specs, c_out_specs, c_shapes, c_scratch, c_inputs = _comm_parts(comm)

    def body(*refs):
        ((sink_ref, q_ref, k_ref, v_ref, o_ref, do_ref, lse_ref), c_in, (dq_ref, dk_ref, dv_ref, ds_ref), c_out,
         (dk_s, dv_s), c_scr) = _split(refs, 7, n_ci, 4, n_co, 2)
        p, i = pl.program_id(0), pl.program_id(1)
        if comm:
            _run_hosted(comm, comm_at, p * n_q + i, c_in, c_out, c_scr)
        fresh = (i == 0) & (p % 2 == 0) if shared_kv else i == 0
        last = (i == n_q - 1) & (p % 2 == 1) if shared_kv else i == n_q - 1

        @pl.when(fresh)
        def _():
            dk_s[...] = jnp.zeros_like(dk_s)
            dv_s[...] = jnp.zeros_like(dv_s)

        q0 = i * tq
        low = lax.broadcasted_iota(jnp.int32, (1, LANES), 1) < HEAD_DIM
        rel = lax.broadcasted_iota(jnp.int32, (sb, win), 1) - lax.broadcasted_iota(jnp.int32, (sb, win), 0)
        col = lax.broadcasted_iota(jnp.int32, (1, win), 1)
        dsink = [jnp.zeros((1, 1), F32), jnp.zeros((1, 1), F32)]
        subs = []
        for j in range(tq // sb):
            rows = pl.ds(j * sb, sb)
            ks, valid = _sub_band(q0 + j * sb, hw, win, seg_len, t, rel, col)
            kw = k_ref[pl.ds(ks, win), :]
            dov = do_ref[rows, :]
            q2 = _split_heads(q_ref[rows, :], low)
            do2 = _split_heads(dov, low)
            subs.append((rows, ks, valid, kw, dov, q2, do2, _dot_nt(q2, kw), _dot_nt(do2, v_ref[pl.ds(ks, win), :])))
        probs = []
        for rows, ks, valid, kw, dov, q2, do2, s, dpr in subs:
            prod = dov.astype(F32) * o_ref[rows, :].astype(F32)
            lse_t = lse_ref[rows, :]
            prs, dss = [], []
            for a in range(2):
                mine = low if a == 0 else jnp.logical_not(low)
                lse_a = jnp.max(jnp.where(mine, lse_t, -jnp.inf), axis=1, keepdims=True)
                delta = jnp.sum(jnp.where(mine, prod, 0.0), axis=1, keepdims=True)
                pr = jnp.exp(jnp.where(valid, s[a * sb:(a + 1) * sb], NEG) - lse_a)
                prs.append(pr.astype(BF16))
                dss.append((pr * (dpr[a * sb:(a + 1) * sb] - delta)).astype(BF16))
                if has_sink:
                    dsink[a] = dsink[a] - jnp.sum(jnp.exp(sink_ref[2 * p + a] - lse_a) * delta, axis=0, keepdims=True)
            probs.append((jnp.concatenate(prs, axis=0), jnp.concatenate(dss, axis=0)))
        for (rows, ks, valid, kw, dov, q2, do2, s, dpr), (pr2, ds2) in zip(subs, probs):
            dv_s[pl.ds(ks, win), :] += _dot_tn(pr2, do2)
            dk_s[pl.ds(ks, win), :] += _dot_tn(ds2, q2)
            dq2 = _dot_nn(ds2, kw)
            dq_ref[rows, :] = jnp.where(low, dq2[0:sb], dq2[sb:2 * sb]).astype(BF16)
        ds_ref[...] = jnp.broadcast_to(jnp.where(low, dsink[0], dsink[1]), ds_ref.shape)

        @pl.when(last)
        def _():
            dk_ref[...] = dk_s[...].astype(BF16)
            dv_ref[...] = dv_s[...].astype(BF16)

    tile = pl.BlockSpec((tq, LANES), lambda p, i: (i, p))
    kv_shape = jax.ShapeDtypeStruct(k.shape, BF16)
    res = pl.pallas_call(
        body, name=name, grid=(width // LANES, n_q),
        out_shape=[jax.ShapeDtypeStruct((t, width), BF16), kv_shape, kv_shape,
                   jax.ShapeDtypeStruct((width // LANES, n_q, 8, LANES), F32)] + c_shapes,
        in_specs=[pl.BlockSpec(memory_space=pltpu.SMEM), tile, _kv_spec(k, t), _kv_spec(v, t), tile, tile, tile] + c_in_specs,
        out_specs=[tile, _kv_spec(k, t), _kv_spec(v, t), pl.BlockSpec((None, None, 8, LANES), lambda p, i: (p, i, 0, 0))] + c_out_specs,
        scratch_shapes=[pltpu.VMEM((t, LANES), F32)] * 2 + c_scratch,
        compiler_params=_cp(("arbitrary", "arbitrary")),
    )(sink, q, k, v, o, do, lse, *c_inputs)
    return res[:4], res[4:]


def _small_update(land, params, rows, name):
    n_p = len(params)

    def body(*refs):
        (l_ref,), wmv, (red_ref,), outs = _split(refs, 1, 3 * n_p, 1)
        acc = l_ref[0]
        for s in range(1, N_DEV):
            acc = acc + l_ref[s]
        red_ref[...] = acc
        for k in range(n_p):
            (w_ref, m_ref, v_ref), (d_ref, nm_ref, nv_ref) = wmv[3 * k:3 * k + 3], outs[3 * k:3 * k + 3]
            gv = red_ref[rows[k]:rows[k] + 1, 0:w_ref.shape[1]]
            nm = ADAM_B1 * m_ref[...] + (1.0 - ADAM_B1) * gv
            nv = ADAM_B2 * v_ref[...] + (1.0 - ADAM_B2) * (gv * gv)
            m_hat = nm / (1.0 - ADAM_B1 ** ADAM_STEP)
            v_hat = nv / (1.0 - ADAM_B2 ** ADAM_STEP)
            d_ref[...] = -ADAM_LR * (m_hat / (jnp.sqrt(v_hat) + ADAM_EPS) + ADAM_WD * w_ref[...])
            nm_ref[...] = nm
            nv_ref[...] = nv

    flat = [a for p in params for a in p]
    res = pl.pallas_call(
        body, name=name,
        out_shape=[jax.ShapeDtypeStruct(land.shape[1:], F32)] + [jax.ShapeDtypeStruct(p[0].shape, F32) for p in params for _ in range(3)],
        compiler_params=_cp())(land, *flat)
    return res[0], [res[1 + 3 * k:4 + 3 * k] for k in range(n_p)]


def _adamw_rows(land, j, rows, w, m, v, name):
    d = w.shape[1]

    def body(l_ref, w_ref, m_ref, v_ref, g_ref, d_ref, nm_ref, nv_ref):
        gv = l_ref[0].astype(F32)
        for s in range(1, N_DEV):
            gv = gv + l_ref[s].astype(F32)
        g_ref[...] = gv
        nm = ADAM_B1 * m_ref[...] + (1.0 - ADAM_B1) * gv
        nv = ADAM_B2 * v_ref[...] + (1.0 - ADAM_B2) * (gv * gv)
        m_hat = nm / (1.0 - ADAM_B1 ** ADAM_STEP)
        v_hat = nv / (1.0 - ADAM_B2 ** ADAM_STEP)
        d_ref[...] = -ADAM_LR * (m_hat / (jnp.sqrt(v_hat) + ADAM_EPS) + ADAM_WD * w_ref[...])
        nm_ref[...] = nm
        nv_ref[...] = nv

    halves = 2
    half = pl.BlockSpec((rows // halves, d), lambda i: (i, 0))
    shape = jax.ShapeDtypeStruct((rows, d), F32)
    return pl.pallas_call(
        body, name=name, grid=(halves,), out_shape=[shape] * 4,
        in_specs=[pl.BlockSpec((N_DEV, rows // halves, d), lambda i: (0, halves * j + i, 0)), half, half, half],
        out_specs=[half] * 4,
        compiler_params=_cp(("arbitrary",)),
    )(land, w, m, v)


def kernel(x, positions, norm_ffn1, w_gate1, w_up1, w_down1, norm_mix, w_in, a_sink, w_out, norm_ffn2, w_gate2, w_up2, w_down2, norm_final, loss_target, m_norm_ffn1, m_w_gate1, m_w_up1, m_w_down1, m_norm_mix, m_w_in, m_a_sink, m_w_out, m_norm_ffn2, m_w_gate2, m_w_up2, m_w_down2, m_norm_final, v_norm_ffn1, v_w_gate1, v_w_up1, v_w_down1, v_norm_mix, v_w_in, v_a_sink, v_w_out, v_norm_ffn2, v_w_gate2, v_w_up2, v_w_down2, v_norm_final):
    x = x[0]
    target = loss_target[0]
    t, d = x.shape
    tm_mix = min(512, t)
    tq_of = lambda seg_len, most=2048: min(most, t, max(seg_len, 16 * SUB_ROWS))

    tt = min(1024, t)
    f_all = w_gate1.shape[2] * N_DEV
    tfp = f_all // 2
    tm_fwd, tf_fwd = tm_mix, tfp
    n_steps = (t // tm_fwd) * (f_all // tf_fwd)

    def stacked(shards):
        return jnp.concatenate([s.astype(BF16) for s in shards], axis=0), [s.shape[0] for s in shards]

    packed1, rows1 = stacked([w_gate1[0].T, w_up1[0].T, w_down1[0]])
    later = [w_in[0].T, w_out[0], w_gate2[0].T, w_up2[0].T, w_down2[0]]
    rows2 = [s.shape[0] for s in later]
    (wg1, wu1, wd1), packed2, cos, sin_signed = _run_alone_busy(
        _gather_plan(packed1, rows1, stages=len(rows1)), later, positions[0].astype(F32)[:, None], "gather_ffn1_weights")

    sink = a_sink[0]
    no_sink = jnp.zeros_like(sink)

    (x1, h1, gp1, up1, hid1), (win, wout, wg2, wu2, wd2) = _ffn_fwd(
        x, norm_ffn1, wg1, wu1, wd1, "ffn1_fwd", tm_fwd, tf_fwd, _gather_plan(packed2, rows2),
        [0, n_steps // 2, (3 * n_steps) // 4, n_steps - 1])
    (h_mix, aq, akx, avx, bq, bk, bv), seg_qkv = _in_proj_fwd(x1, norm_mix, win, cos, sin_signed, "in_proj_fwd", tm_mix)
    a_out, a_lse = _attn_fwd(aq, akx, avx, sink, "attn_a_fwd", A_HALF_WINDOW, t, tq_of(t), True)
    rows_of = lambda a: a.reshape(t, B_W)
    segments_of = lambda a, dil: a if dil == 1 else a.reshape(dil, t // dil, B_W)
    b_qkv, b_outs, b_lses = [], [], []
    for n, (window, dil) in enumerate(B_PATTERNS):
        qs, ks, vs = (bq, bk, bv) if dil == 1 else (rows_of(seg_qkv[k][n - 1]) for k in range(3))
        o_seg, lse_seg = _attn_fwd(qs, ks, vs, no_sink, f"attn_b{dil}_fwd", window // (2 * dil), t // dil,
                                  tq_of(t // dil, 16 * SUB_ROWS), False)
        b_qkv.append((qs, ks, vs))
        b_outs.append(segments_of(o_seg, dil))
        b_lses.append(segments_of(lse_seg, dil))
    x2, b_out, b_lse = _merge_out_proj_fwd(x1, a_out, b_outs, b_lses, wout, "out_proj_fwd", tm_mix)
    (dx3, dg_final, sq, dout2, h2, gp2, up2, hid2), _ = _ffn_fwd(
        x2, norm_ffn2, wg2, wu2, wd2, "ffn2_fwd_loss", tm_fwd, tf_fwd, loss_head=(norm_final.reshape(1, d), target))


    (dgt2, dut2, gwd2), _ = _ffn_bwd_hidden(dout2, gp2, up2, hid2, wd2, "ffn2_bwd_hidden", tm_mix, tfp)
    (dx2, dg_ffn2), _ = _ffn_bwd_input(dx3, x2, norm_ffn2, dgt2, dut2, wg2, wu2, "ffn2_bwd_input", tm_mix)
    (gwg2, gwu2), _ = _token_products([dgt2, dut2], h2, "ffn2_bwd_gate_up", tt, tfp)

    n_att = lambda width, tq: (width // LANES) * (t // tq)
    n_prod = (f_all // tfp) * (t // tt)
    da, db, gwout = _out_proj_bwd(dx2, a_out, b_out[0], wout, "out_proj_bwd", tm_mix)
    (daq, dakx, davx, dsink_parts), (land_wg2,) = _attn_bwd(
        aq, akx, avx, a_out, da, a_lse, sink, "attn_a_bwd", A_HALF_WINDOW, t, tq_of(t), True,
        _exchange_plan([[gwg2]]), [0, n_att(A_Q_W, tq_of(t)) - 1])
    pattern_grads, pattern_lands = [], []
    for n, ((window, dil), (qs, ks, vs)) in enumerate(zip(B_PATTERNS, b_qkv)):
        carried = [gwu2, gwd2][n:n + 1]
        grads, lands = _attn_bwd(qs, ks, vs, rows_of(b_out[n]), rows_of(db[n]), rows_of(b_lse[n]), no_sink, f"attn_b{dil}_bwd",
                                 window // (2 * dil), t // dil, tq_of(t), False,
                                 _exchange_plan([carried]) if carried else None, [0, n_att(B_W, tq_of(t)) - 1])
        pattern_grads.append(grads[:3])
        pattern_lands.extend(lands)
    land_wu2, land_wd2 = pattern_lands
    dx1, dg_mix, gwin, dout1 = _in_proj_bwd(dx2, x1, norm_mix, win, h_mix, cos, sin_signed, daq, dakx, davx, pattern_grads, "in_proj_bwd", tm_mix)

    (dgt1, dut1, gwd1), (land_in, land_out) = _ffn_bwd_hidden(
        dout1, gp1, up1, hid1, wd1, "ffn1_bwd_hidden", tm_mix, tfp, _exchange_plan([[gwin], [gwout]]),
        [0, (f_all // tfp) * (t // tm_mix) - 1])
    (gwg1, gwu1), (land_wd1,) = _token_products(
        [dgt1, dut1], h1, "ffn1_bwd_gate_up", tt, tfp, _exchange_plan([[gwd1]]), [0, n_prod - 1])
    (grad_x, dg_ffn1), (land_wg1, land_wu1) = _ffn_bwd_input(
        dx1, x, norm_ffn1, dgt1, dut1, wg1, wu1, "ffn1_bwd_input", tm_mix, _exchange_plan([[gwg1], [gwu1]]), [0, t // tm_mix - 1])

    dsink_pairs = jnp.sum(dsink_parts[:, :, 0, :], axis=1)
    dsink = jnp.stack([dsink_pairs[:, 0], dsink_pairs[:, HEAD_DIM]], axis=1).reshape(1, -1)
    small = jnp.concatenate([dg_ffn1, dg_mix, dg_ffn2, dg_final, jnp.pad(dsink, ((0, 0), (0, d - dsink.shape[1]))),
                             sq, jnp.zeros((2, d), F32)], axis=0)
    (land_small,) = _run_alone(_exchange_plan([[jnp.tile(small, (N_DEV, 1))]]), "gather_small_gradients")
    small_rows = {"norm_ffn1": 0, "norm_mix": 1, "norm_ffn2": 2, "norm_final": 3, "a_sink": 4}
    as_row = lambda a: a.reshape(1, -1)
    small_params = {"norm_ffn1": (norm_ffn1, m_norm_ffn1, v_norm_ffn1), "norm_mix": (norm_mix, m_norm_mix, v_norm_mix),
                    "norm_ffn2": (norm_ffn2, m_norm_ffn2, v_norm_ffn2), "norm_final": (norm_final, m_norm_final, v_norm_final),
                    "a_sink": (a_sink, m_a_sink, v_a_sink)}
    red_small, small_updates = _small_update(
        land_small, [tuple(as_row(a) for a in small_params[n]) for n in small_rows], list(small_rows.values()), "update_small")
    small_updates = dict(zip(small_rows, small_updates))
    loss = 0.5 * jnp.sum(red_small[5]) / d

    rf = rows1[0]
    sharded = {"w_gate1": (land_wg1, 0, rf, True), "w_up1": (land_wu1, 0, rf, True), "w_down1": (land_wd1, 0, rf, False),
               "w_in": (land_in, 0, rows2[0], True), "w_out": (land_out, 0, rows2[1], False),
               "w_gate2": (land_wg2, 0, rf, True), "w_up2": (land_wu2, 0, rf, True), "w_down2": (land_wd2, 0, rf, False)}
    n_sink = a_sink.shape[1]
    small_grads = {"norm_ffn1": red_small[0:1], "norm_mix": red_small[1:2], "norm_ffn2": red_small[2:3], "norm_final": red_small[3],
                   "a_sink": red_small[4:5, :n_sink]}
    params = {
        "norm_ffn1": (norm_ffn1, m_norm_ffn1, v_norm_ffn1), "w_gate1": (w_gate1, m_w_gate1, v_w_gate1),
        "w_up1": (w_up1, m_w_up1, v_w_up1), "w_down1": (w_down1, m_w_down1, v_w_down1),
        "norm_mix": (norm_mix, m_norm_mix, v_norm_mix), "w_in": (w_in, m_w_in, v_w_in),
        "a_sink": (a_sink, m_a_sink, v_a_sink), "w_out": (w_out, m_w_out, v_w_out),
        "norm_ffn2": (norm_ffn2, m_norm_ffn2, v_norm_ffn2), "w_gate2": (w_gate2, m_w_gate2, v_w_gate2),
        "w_up2": (w_up2, m_w_up2, v_w_up2), "w_down2": (w_down2, m_w_down2, v_w_down2),
        "norm_final": (norm_final, m_norm_final, v_norm_final),
    }
    grad_list, deltas, new_ms, new_vs = [], [], [], []
    for name, (w, m, v) in params.items():
        if name in sharded:
            land, j, rows, is_transposed = sharded[name]
            view = (lambda a: a[0].T) if is_transposed else (lambda a: a[0])
            back = (lambda a: a.T[None]) if is_transposed else (lambda a: a[None])
            outs = [back(o) for o in _adamw_rows(land, j, rows, view(w), view(m), view(v), f"adamw_{name}")]
        else:
            outs = [small_grads[name]] + [o.reshape(w.shape) for o in small_updates[name]]
        for lst, o in zip((grad_list, deltas, new_ms, new_vs), outs):
            lst.append(o)
    return (loss, grad_x[None], *grad_list, *deltas, *new_ms, *new_vs)
```
